```python
import math
import jax
import jax.numpy as jnp
from jax import lax
import numpy as np

D_MODEL = 1024
BATCH = 16
SEQ = 256
DEPTH = 2
DEC_BATCH = 2
DEC_SEQ = 1024
PAST_LEN = 256

GRID_W = 64
HD = 64
A_HEADS = 8
A_KV_HEADS = 2
A_GROUP = A_HEADS // A_KV_HEADS
WINDOW = 128
BLK = 128
B_HEADS = 4
B_HD = 64
C_HEADS = 8
C_NOPE = 64
C_ROPE = 32
C_V = 64
Q_LORA = 256
KV_LORA = 128
BRANCH_A = A_HEADS * HD
BRANCH_B = B_HEADS * 2 * B_HD
BRANCH_C = C_HEADS * C_V
IN_SPLITS = (A_HEADS * HD, A_KV_HEADS * HD, A_KV_HEADS * HD,
             B_HEADS * 2 * B_HD, B_HEADS * 2 * B_HD, B_HEADS * 2 * B_HD,
             Q_LORA, KV_LORA, C_ROPE, 3 * D_MODEL)
IN_COLS = sum(IN_SPLITS)
N_EXPERTS = 64
TOP_K = 6
N_GROUPS = 8
TOPK_GROUPS = 4
F_EXPERT = 256
F_SHARED = 256
ROUTED_SCALE = 2.5
QBLK = 128
ROPE_BASE = 10000.0
EPS = 1e-6
NEG = -1e30

kernel_name = "hybrid_diffusion_prefix_trunk_step"


def rms_norm(x, g):
    xf = x.astype(jnp.float32)
    y = xf * lax.rsqrt(jnp.mean(xf * xf, axis=-1, keepdims=True) + EPS)
    return (y * g.astype(jnp.float32)).astype(x.dtype)


def split_cols(z, sizes):
    offs, acc = [], 0
    for s in sizes[:-1]:
        acc += s
        offs.append(acc)
    return jnp.split(z, offs, axis=-1)


def rope_1d(x, pos):
    half = x.shape[-1] // 2
    inv = ROPE_BASE ** (-jnp.arange(half, dtype=jnp.float32) / half)
    ang = pos.astype(jnp.float32)[:, None] * inv[None, :]
    shape = (1, x.shape[1]) + (1,) * (x.ndim - 3) + (half,)
    cos = jnp.cos(ang).reshape(shape)
    sin = jnp.sin(ang).reshape(shape)
    xf = x.astype(jnp.float32)
    x1, x2 = xf[..., :half], xf[..., half:]
    return jnp.concatenate([x1 * cos - x2 * sin, x1 * sin + x2 * cos], axis=-1).astype(x.dtype)


def rope_2d(x):
    t = jnp.arange(x.shape[1])
    d = x.shape[-1] // 2
    return jnp.concatenate([rope_1d(x[..., :d], t // GRID_W), rope_1d(x[..., d:], t % GRID_W)], axis=-1)


def softmax_sink(s, sink):
    if sink is None:
        return jax.nn.softmax(s, axis=-1)
    m = jnp.maximum(jnp.max(s, axis=-1, keepdims=True), sink)
    e = jnp.exp(s - m)
    return e / (jnp.sum(e, axis=-1, keepdims=True) + jnp.exp(sink - m))


def dense_attn(q, k, v, sink=None):
    B, T, Hk, G, d = q.shape
    nb = T // QBLK
    scale = 1.0 / math.sqrt(d)
    qb = jnp.moveaxis(q.reshape(B, nb, QBLK, Hk, G, d), 1, 0)
    sk = None if sink is None else sink.astype(jnp.float32).reshape(1, Hk, G, 1, 1)

    def one(qi):
        s = jnp.einsum("bqhgd,bshd->bhgqs", qi, k).astype(jnp.float32) * scale
        p = softmax_sink(s, sk)
        return jnp.einsum("bhgqs,bshe->bqhge", p.astype(v.dtype), v)

    o = lax.map(one, qb)
    return jnp.moveaxis(o, 0, 1).reshape(B, T, Hk, G, v.shape[-1])


def window_attn(q, k, v, kc, vc, sink):
    B, T, Hk, G, d = q.shape
    nb = T // BLK
    scale = 1.0 / math.sqrt(d)
    pad = ((0, 0), (BLK, BLK), (0, 0), (0, 0))
    kp, vp = jnp.pad(k, pad), jnp.pad(v, pad)

    def bands(xp):
        return jnp.concatenate([xp[:, o:o + T].reshape(B, nb, BLK, Hk, xp.shape[-1]) for o in (0, BLK, 2 * BLK)], axis=2)

    kb, vb = bands(kp), bands(vp)
    qb = q.reshape(B, nb, BLK, Hk, G, d)
    s_loc = jnp.einsum("bnqhgd,bnkhd->bnhgqk", qb, kb).astype(jnp.float32) * scale
    qi = jnp.arange(BLK)[:, None]
    ki = jnp.arange(3 * BLK)[None, :] - BLK
    key_abs = jnp.arange(nb)[:, None, None] * BLK + ki[None]
    valid = (jnp.abs(qi - ki) <= WINDOW)[None] & (key_abs >= 0) & (key_abs < T)
    s_loc = jnp.where(valid[None, :, None, None], s_loc, NEG)
    s_ctx = jnp.einsum("bnqhgd,bshd->bnhgqs", qb, kc).astype(jnp.float32) * scale
    S = kc.shape[1]
    p = softmax_sink(jnp.concatenate([s_ctx, s_loc], axis=-1),
                     sink.astype(jnp.float32).reshape(1, 1, Hk, G, 1, 1)).astype(v.dtype)
    o = (jnp.einsum("bnhgqs,bshd->bnqhgd", p[..., :S], vc)
         + jnp.einsum("bnhgqk,bnkhd->bnqhgd", p[..., S:], vb))
    return o.reshape(B, T, Hk, G, d)


def mla_kv(ckv, kpe, w_ukv):
    B, S, _ = ckv.shape
    kv = (ckv @ w_ukv).reshape(B, S, C_HEADS, C_NOPE + C_V)
    k_nope, v = kv[..., :C_NOPE], kv[..., C_NOPE:]
    k = jnp.concatenate([k_nope, jnp.broadcast_to(kpe[:, :, None, :], (B, S, C_HEADS, C_ROPE))], axis=-1)
    return k, v


def token_mix(h, P, l, ctx=None):
    B, T, _ = h.shape
    z = h @ P["w_in"][l]
    qa, ka, va, qd, kd, vd, qc, kvc, kpe, gl = split_cols(z, IN_SPLITS)
    qa = qa.reshape(B, T, A_KV_HEADS, A_GROUP, HD)
    ka = ka.reshape(B, T, A_KV_HEADS, HD)
    va = va.reshape(B, T, A_KV_HEADS, HD)
    qd = qd.reshape(B, T, B_HEADS, 2, B_HD)
    kd = kd.reshape(B, T, B_HEADS, 2, B_HD)
    vd = vd.reshape(B, T, B_HEADS, 2 * B_HD)
    qm = (rms_norm(qc, P["mla_q_norm"][l]) @ P["mla_w_uq"][l]).reshape(B, T, C_HEADS, C_NOPE + C_ROPE)
    ckv = rms_norm(kvc, P["mla_kv_norm"][l])
    new_ctx = (ka, va, kd, vd, ckv, kpe)
    sink = P["swa_sink"][l].reshape(A_KV_HEADS, A_GROUP)
    w_ukv = P["mla_w_ukv"][l]
    if ctx is None:
        oa = dense_attn(qa, ka, va, sink)
        kd_all, vd_all = kd, vd
        km, vm = mla_kv(ckv, kpe, w_ukv)
    else:
        ka_c, va_c, kd_c, vd_c, ckv_c, kpe_c = ctx
        oa = window_attn(rope_2d(qa), rope_2d(ka), va, ka_c, va_c, sink)
        qd = rope_2d(qd)
        kd_all = jnp.concatenate([kd_c, rope_2d(kd)], axis=1)
        vd_all = jnp.concatenate([vd_c, vd], axis=1)
        qm = jnp.concatenate([qm[..., :C_NOPE], rope_2d(qm[..., C_NOPE:])], axis=-1)
        km_c, vm_c = mla_kv(ckv_c, kpe_c, w_ukv)
        km_l, vm_l = mla_kv(ckv, rope_2d(kpe), w_ukv)
        km = jnp.concatenate([km_c, km_l], axis=1)
        vm = jnp.concatenate([vm_c, vm_l], axis=1)
    lam_init = 0.8 - 0.6 * math.exp(-0.3 * l)
    lam = (jnp.exp(jnp.sum(P["dif_lq1"][l].astype(jnp.float32) * P["dif_lk1"][l].astype(jnp.float32)))
           - jnp.exp(jnp.sum(P["dif_lq2"][l].astype(jnp.float32) * P["dif_lk2"][l].astype(jnp.float32)))
           + lam_init)
    o1 = dense_attn(qd[:, :, :, 0:1], kd_all[:, :, :, 0], vd_all)
    o2 = dense_attn(qd[:, :, :, 1:2], kd_all[:, :, :, 1], vd_all)
    od = o1 - lam.astype(o1.dtype) * o2
    od = rms_norm(od[:, :, :, 0], P["dif_subln"][l]) * (1.0 - lam_init)
    om = dense_attn(qm[:, :, :, None], km, vm)
    ga, gb, gc = jnp.split(jax.nn.sigmoid(gl), 3, axis=-1)
    merged = (ga * (oa.reshape(B, T, BRANCH_A) @ P["w_branch_a"][l])
              + gb * (od.reshape(B, T, BRANCH_B) @ P["w_branch_b"][l])
              + gc * (om.reshape(B, T, BRANCH_C) @ P["w_branch_c"][l]))
    return merged @ P["w_out"][l], new_ctx


def moe(h, P, l):
    B, T, D = h.shape
    N = B * T
    x = h.reshape(N, D)
    scores = jax.nn.sigmoid((x @ P["router_w"][l]).astype(jnp.float32))
    biased = scores + P["router_bias"][l].astype(jnp.float32)
    grp = biased.reshape(N, N_GROUPS, N_EXPERTS // N_GROUPS)
    grp_score = jnp.sum(lax.top_k(grp, 2)[0], axis=-1)
    _, gidx = lax.top_k(grp_score, TOPK_GROUPS)
    gmask = jnp.sum(jax.nn.one_hot(gidx, N_GROUPS, dtype=jnp.float32), axis=1)
    emask = jnp.repeat(gmask, N_EXPERTS // N_GROUPS, axis=1) > 0
    _, eidx = lax.top_k(jnp.where(emask, biased, NEG), TOP_K)
    w = jnp.take_along_axis(scores, eidx, axis=-1)
    w = w / jnp.sum(w, axis=-1, keepdims=True) * ROUTED_SCALE
    gates = jnp.einsum("nk,nke->ne", w, jax.nn.one_hot(eidx, N_EXPERTS, dtype=jnp.float32)).astype(x.dtype)
    hid = (jax.nn.silu(jnp.einsum("nd,edf->nef", x, P["moe_w_gate"][l]))
           * jnp.einsum("nd,edf->nef", x, P["moe_w_up"][l]))
    routed = jnp.einsum("nef,efd->nd", hid * gates[:, :, None], P["moe_w_down"][l])
    shared = (jax.nn.silu(x @ P["shared_w_gate"][l]) * (x @ P["shared_w_up"][l])) @ P["shared_w_down"][l]
    return (routed + shared).reshape(B, T, D)


def layer(x, mod, P, l, ctx=None):
    sh1, sc1, g1, sh2, sc2, g2 = jnp.split(mod, 6, axis=-1)
    h = rms_norm(x, P["norm_attn_pre"][l]) * (1 + sc1) + sh1
    a, new_ctx = token_mix(h, P, l, ctx)
    x = x + g1 * rms_norm(a, P["norm_attn_post"][l])
    h = rms_norm(x, P["norm_ffn_pre"][l]) * (1 + sc2) + sh2
    x = x + g2 * rms_norm(moe(h, P, l), P["norm_ffn_post"][l])
    return x, new_ctx


def setup_inputs(seed: int = 0) -> dict:
    key = jax.random.key(seed)
    ks = iter(jax.random.split(key, 64))

    def nrm(shape, scale=1.0):
        return jax.random.normal(next(ks), shape, jnp.float32) * scale

    def gain(shape):
        return 1.0 + nrm(shape, 0.05)

    D = D_MODEL
    return {
        "x_prompt": nrm((BATCH, SEQ, D)),
        "x_sample": nrm((DEC_BATCH, DEC_SEQ, D)),
        "cache_swa_k": nrm((DEC_BATCH, DEPTH, PAST_LEN, A_KV_HEADS, HD)),
        "cache_swa_v": nrm((DEC_BATCH, DEPTH, PAST_LEN, A_KV_HEADS, HD)),
        "cache_dif_k": nrm((DEC_BATCH, DEPTH, PAST_LEN, B_HEADS, 2, B_HD)),
        "cache_dif_v": nrm((DEC_BATCH, DEPTH, PAST_LEN, B_HEADS, 2 * B_HD)),
        "cache_mla_ckv": nrm((DEC_BATCH, DEPTH, PAST_LEN, KV_LORA)),
        "cache_mla_kpe": nrm((DEC_BATCH, DEPTH, PAST_LEN, C_ROPE)),
        "c": nrm((DEC_BATCH, D)),
        "c_ctx": nrm((D,)),
        "w_mod": nrm((DEPTH, D, 6 * D), 0.5 * D ** -0.5),
        "b_mod": nrm((DEPTH, 6 * D), 0.02),
        "norm_attn_pre": gain((DEPTH, D)),
        "norm_attn_post": gain((DEPTH, D)),
        "norm_ffn_pre": gain((DEPTH, D)),
        "norm_ffn_post": gain((DEPTH, D)),
        "w_in": nrm((DEPTH, D, IN_COLS), D ** -0.5),
        "swa_sink": nrm((DEPTH, A_HEADS), 0.5),
        "dif_lq1": nrm((DEPTH, B_HD), 0.1),
        "dif_lk1": nrm((DEPTH, B_HD), 0.1),
        "dif_lq2": nrm((DEPTH, B_HD), 0.1),
        "dif_lk2": nrm((DEPTH, B_HD), 0.1),
        "dif_subln": gain((DEPTH, 2 * B_HD)),
        "mla_q_norm": gain((DEPTH, Q_LORA)),
        "mla_w_uq": nrm((DEPTH, Q_LORA, C_HEADS * (C_NOPE + C_ROPE)), Q_LORA ** -0.5),
        "mla_kv_norm": gain((DEPTH, KV_LORA)),
        "mla_w_ukv": nrm((DEPTH, KV_LORA, C_HEADS * (C_NOPE + C_V)), KV_LORA ** -0.5),
        "w_branch_a": nrm((DEPTH, BRANCH_A, D), BRANCH_A ** -0.5),
        "w_branch_b": nrm((DEPTH, BRANCH_B, D), BRANCH_B ** -0.5),
        "w_branch_c": nrm((DEPTH, BRANCH_C, D), BRANCH_C ** -0.5),
        "w_out": nrm((DEPTH, D, D), D ** -0.5),
        "router_w": nrm((DEPTH, D, N_EXPERTS), D ** -0.5),
        "router_bias": nrm((DEPTH, N_EXPERTS), 0.01),
        "moe_w_gate": nrm((DEPTH, N_EXPERTS, D, F_EXPERT), D ** -0.5),
        "moe_w_up": nrm((DEPTH, N_EXPERTS, D, F_EXPERT), D ** -0.5),
        "moe_w_down": nrm((DEPTH, N_EXPERTS, F_EXPERT, D), F_EXPERT ** -0.5),
        "shared_w_gate": nrm((DEPTH, D, F_SHARED), D ** -0.5),
        "shared_w_up": nrm((DEPTH, D, F_SHARED), D ** -0.5),
        "shared_w_down": nrm((DEPTH, F_SHARED, D), F_SHARED ** -0.5),
    }


def reference(x_prompt, x_sample, cache_swa_k, cache_swa_v, cache_dif_k, cache_dif_v,
              cache_mla_ckv, cache_mla_kpe, c, c_ctx, w_mod, b_mod,
              norm_attn_pre, norm_attn_post, norm_ffn_pre, norm_ffn_post, w_in, swa_sink,
              dif_lq1, dif_lk1, dif_lq2, dif_lk2, dif_subln, mla_q_norm, mla_w_uq,
              mla_kv_norm, mla_w_ukv, w_branch_a, w_branch_b, w_branch_c, w_out,
              router_w, router_bias, moe_w_gate, moe_w_up, moe_w_down,
              shared_w_gate, shared_w_up, shared_w_down):
    P = dict(norm_attn_pre=norm_attn_pre, norm_attn_post=norm_attn_post,
             norm_ffn_pre=norm_ffn_pre, norm_ffn_post=norm_ffn_post, w_in=w_in, swa_sink=swa_sink,
             dif_lq1=dif_lq1, dif_lk1=dif_lk1, dif_lq2=dif_lq2, dif_lk2=dif_lk2, dif_subln=dif_subln,
             mla_q_norm=mla_q_norm, mla_w_uq=mla_w_uq, mla_kv_norm=mla_kv_norm, mla_w_ukv=mla_w_ukv,
             w_branch_a=w_branch_a, w_branch_b=w_branch_b, w_branch_c=w_branch_c, w_out=w_out,
             router_w=router_w, router_bias=router_bias, moe_w_gate=moe_w_gate, moe_w_up=moe_w_up,
             moe_w_down=moe_w_down, shared_w_gate=shared_w_gate, shared_w_up=shared_w_up,
             shared_w_down=shared_w_down)
    y_p, y_s = x_prompt, x_sample
    states = [[], [], [], [], [], []]
    for l in range(DEPTH):
        mod_ctx = (jax.nn.silu(c_ctx) @ w_mod[l] + b_mod[l])[None, None, :]
        mod_lat = (jax.nn.silu(c) @ w_mod[l] + b_mod[l])[:, None, :]
        y_p, ctx_p = layer(y_p, mod_ctx, P, l)
        for i in range(6):
            states[i].append(ctx_p[i])
        cached = (cache_swa_k[:, l], cache_swa_v[:, l], cache_dif_k[:, l], cache_dif_v[:, l],
                  cache_mla_ckv[:, l], cache_mla_kpe[:, l])
        y_s, _ = layer(y_s, mod_lat, P, l, cached)
    state_swa_k = jnp.stack(states[0], axis=1)
    state_swa_v = jnp.stack(states[1], axis=1)
    state_dif_k = jnp.stack(states[2], axis=1)
    state_dif_v = jnp.stack(states[3], axis=1)
    state_mla_ckv = jnp.stack(states[4], axis=1)
    state_mla_kpe = jnp.stack(states[5], axis=1)
    return (y_p, y_s, state_swa_k, state_swa_v, state_dif_k, state_dif_v, state_mla_ckv, state_mla_kpe)
```

```python
import functools
import math

import numpy as np
import jax
import jax.numpy as jnp
from jax import lax
from jax.experimental import pallas as pl
from jax.experimental.pallas import tpu as pltpu

F32 = jnp.float32
BF16 = jnp.bfloat16

D_MODEL = 1024
N_CTX_BATCH, CTX_SEQ = 16, 256
N_LAT_BATCH, LAT_SEQ = 2, 1024
PAST = 256
N_CTX = N_CTX_BATCH * CTX_SEQ
N_LAT = N_LAT_BATCH * LAT_SEQ
N_TOK = N_CTX + N_LAT
DEPTH = 2
GRID_W = 64
WINDOW = 128
N_EXPERTS = 64
N_GROUPS = 8
TOPK_GROUPS = 4
TOP_K = 6
F_EXPERT = 256
ROUTED_SCALE = 2.5
ROPE_BASE = 10000.0
EPS = 1e-6
NEG = -1e30

LANES = 128
ROW_TILE = 256
Q_BLOCK = 128
MOE_ROW_TILE = 1024
VMEM_LIMIT = 56 * 1024 * 1024

C_QA, C_KA, C_VA, C_QD, C_KD, C_VD, C_QC, C_KVC, C_KPE, C_END = (
    0, 512, 640, 768, 1280, 1792, 2304, 2560, 2688, 2816)


def _mod_row_of_tile(i, tile):
    ctx_tiles = N_CTX // tile
    per_batch = LAT_SEQ // tile
    return jnp.where(i < ctx_tiles, 0, 1 + (i - ctx_tiles) // per_batch)


def _rms(x, g):
    return x * lax.rsqrt(jnp.mean(x * x, axis=-1, keepdims=True) + EPS) * g


def _dot(a, b):
    return jnp.dot(a, b, preferred_element_type=F32)


def _dot_nt(a, b):
    return lax.dot_general(a, b, (((1,), (1,)), ((), ())), preferred_element_type=F32)


def _silu(x):
    return x * jax.nn.sigmoid(x)


def _mod_kernel(c_ref, w_ref, b_ref, o_ref):
    c = c_ref[...]
    o_ref[0] = _dot(_silu(c).astype(BF16), w_ref[0].astype(BF16)) + b_ref[0]


def _modulation(cvec, w_mod, b_mod):
    tn = 1536
    n = w_mod.shape[-1]
    return pl.pallas_call(
        _mod_kernel,
        out_shape=jax.ShapeDtypeStruct((DEPTH, 8, n), F32),
        grid=(DEPTH, n // tn),
        in_specs=[
            pl.BlockSpec((8, D_MODEL), lambda l, j: (0, 0)),
            pl.BlockSpec((1, D_MODEL, tn), lambda l, j: (l, 0, j)),
            pl.BlockSpec((1, 1, tn), lambda l, j: (l, 0, j)),
        ],
        out_specs=pl.BlockSpec((1, 8, tn), lambda l, j: (l, 0, j)),
        compiler_params=pltpu.CompilerParams(vmem_limit_bytes=VMEM_LIMIT),
        name="modulation",
    )(cvec, w_mod, b_mod.reshape(DEPTH, 1, n))


def _rope128(x, cos, sin, half):
    lane = lax.broadcasted_iota(jnp.int32, x.shape, 1)
    first = (lane % (2 * half)) < half
    partner = jnp.where(first, pltpu.roll(x, LANES - half, 1), pltpu.roll(x, half, 1))
    return x * cos + partner * sin


def _rope_cols(x, cos, sin, half):
    chunks = [_rope128(x[:, c:c + LANES], cos, sin, half) for c in range(0, x.shape[1], LANES)]
    return chunks[0] if len(chunks) == 1 else jnp.concatenate(chunks, axis=1)


def _in_kernel(x_ref, mod_ref, g_ref, w1_ref, qn_ref, wuq_ref, kvn_ref, wukv_ref,
               cos64_ref, sin64_ref, cos32_ref, sin32_ref,
               qa_o, ka_o, va_o, qd_o, kd_o, vd_o, qmn_o, qmr_o, ckv_o, kpe_o, kmn_o, vm_o):
    x = x_ref[...]
    m = mod_ref[0]
    sh1, sc1 = m[:, 0:D_MODEL], m[:, D_MODEL:2 * D_MODEL]
    h = _rms(x, g_ref[...]) * (1.0 + sc1) + sh1
    z = _dot(h.astype(BF16), w1_ref[...])
    cos64, sin64 = cos64_ref[...], sin64_ref[...]
    cos32, sin32 = cos32_ref[...], sin32_ref[...]
    qa_o[...] = _rope_cols(z[:, C_QA:C_KA], cos64, sin64, 16).astype(BF16)
    ka_o[...] = _rope_cols(z[:, C_KA:C_VA], cos64, sin64, 16)
    va_o[...] = z[:, C_VA:C_QD]
    qd_o[...] = _rope_cols(z[:, C_QD:C_KD], cos64, sin64, 16).astype(BF16)
    kd_o[...] = _rope_cols(z[:, C_KD:C_VD], cos64, sin64, 16)
    vd_o[...] = z[:, C_VD:C_QC]
    kpe_o[...] = _rope_cols(z[:, C_KPE:C_END], cos32, sin32, 8)
    qn = _rms(z[:, C_QC:C_KVC], qn_ref[...])
    qm = _dot(qn.astype(BF16), wuq_ref[...])
    qmn_o[...] = qm[:, 0:512].astype(BF16)
    qmr_o[...] = _rope_cols(qm[:, 512:768], cos32, sin32, 8).astype(BF16)
    ckv = _rms(z[:, C_KVC:C_KPE], kvn_ref[...])
    ckv_o[...] = ckv
    kv = _dot(ckv.astype(BF16), wukv_ref[...])
    kmn_o[...] = kv[:, 0:512].astype(BF16)
    vm_o[...] = kv[:, 512:1024].astype(BF16)


def _in_projection(x, mod_l, g_pre, w1, qnorm, wuq, kvnorm, wukv, tables):
    tm = ROW_TILE
    n_tiles = N_TOK // tm
    ctx_tiles = N_CTX // tm
    lat_tiles_per_batch = LAT_SEQ // tm

    def row(i):
        return (i, 0)

    def full(i):
        return (0, 0)

    def table_row(i):
        return (jnp.where(i < ctx_tiles, 0, 1 + (i - ctx_tiles) % lat_tiles_per_batch), 0)

    table_spec = pl.BlockSpec((tm, LANES), table_row)
    widths = [(512, BF16), (128, F32), (128, F32), (512, BF16), (512, F32), (512, F32),
              (512, BF16), (256, BF16), (128, F32), (128, F32), (512, BF16), (512, BF16)]
    return pl.pallas_call(
        _in_kernel,
        out_shape=[jax.ShapeDtypeStruct((N_TOK, w), dt) for w, dt in widths],
        grid=(n_tiles,),
        in_specs=[
            pl.BlockSpec((tm, D_MODEL), row),
            pl.BlockSpec((1, 1, 6 * D_MODEL), lambda i: (_mod_row_of_tile(i, tm), 0, 0)),
            pl.BlockSpec((1, D_MODEL), full),
            pl.BlockSpec(w1.shape, full),
            pl.BlockSpec((1, 256), full),
            pl.BlockSpec(wuq.shape, full),
            pl.BlockSpec((1, 128), full),
            pl.BlockSpec(wukv.shape, full),
            table_spec, table_spec, table_spec, table_spec,
        ],
        out_specs=[pl.BlockSpec((tm, w), row) for w, _ in widths],
        compiler_params=pltpu.CompilerParams(vmem_limit_bytes=VMEM_LIMIT),
        name="in_projection",
    )(x, mod_l, g_pre, w1, qnorm, wuq, kvnorm, wukv, *tables)


def _attend(q, parts, scale, sink=None):
    scores = []
    for k, _, mask in parts:
        s = _dot_nt(q, k) * scale
        if mask is not None:
            s = jnp.where(mask, s, NEG)
        scores.append(s)
    m = functools.reduce(jnp.maximum, [jnp.max(s, axis=-1, keepdims=True) for s in scores])
    if sink is not None:
        m = jnp.maximum(m, sink)
    den = None
    out = None
    for s, (_, v, _) in zip(scores, parts):
        e = jnp.exp(s - m)
        d = jnp.sum(e, axis=-1, keepdims=True)
        o = _dot(e.astype(BF16), v)
        den = d if den is None else den + d
        out = o if out is None else out + o
    if sink is not None:
        den = den + jnp.exp(sink - m)
    return out / den


def _three_mixers(scal_ref, subln_ref, q_blocks, kv_parts, oa_o, od_o, om_o, lam_init):
    qa, qd, qmn, qmr = q_blocks
    rows = qa.shape[0]
    lane = lax.broadcasted_iota(jnp.int32, (rows, LANES), 1)
    low = lane < 64
    zero = jnp.zeros((rows, LANES), BF16)

    for g in range(4):
        q128 = qa[:, g * LANES:(g + 1) * LANES]
        halves = []
        for kvh in range(2):
            qh = jnp.where(low if kvh == 0 else ~low, q128, zero)
            parts = [(p["ka"], p["va"], p.get("mask_a")) for p in kv_parts]
            halves.append(_attend(qh, parts, 0.125, sink=scal_ref[kvh * 4 + g]))
        oa_o[:, g * LANES:(g + 1) * LANES] = jnp.where(low, halves[0], halves[1]).astype(BF16)

    lam = scal_ref[8]
    subln = subln_ref[...]
    for h in range(4):
        sl = slice(h * LANES, (h + 1) * LANES)
        q128 = qd[:, sl]
        parts = [(p["kd"][:, sl], p["vd"][:, sl], None) for p in kv_parts]
        o1 = _attend(jnp.where(low, q128, zero), parts, 0.125)
        o2 = _attend(jnp.where(low, zero, q128), parts, 0.125)
        od = o1 - lam * o2
        od_o[:, sl] = (_rms(od, subln) * (1.0 - lam_init)).astype(BF16)

    scale_c = 1.0 / math.sqrt(96.0)
    for i in range(4):
        sl = slice(i * LANES, (i + 1) * LANES)
        keys = [jnp.concatenate([p["kmn"][:, sl], p["kpe"]], axis=1) for p in kv_parts]
        halves = []
        for sub in range(2):
            h = 2 * i + sub
            qn = jnp.where(low if sub == 0 else ~low, qmn[:, sl], zero)
            qr128 = qmr[:, (h // 4) * LANES:(h // 4 + 1) * LANES]
            qr = jnp.where((lane // 32) == (h % 4), qr128, zero)
            q256 = jnp.concatenate([qn, qr], axis=1)
            parts = [(k, p["vm"][:, sl], None) for k, p in zip(keys, kv_parts)]
            halves.append(_attend(q256, parts, scale_c))
        om_o[:, sl] = jnp.where(low, halves[0], halves[1]).astype(BF16)


def _attn_ctx_kernel(scal_ref, subln_ref, qa_ref, qd_ref, qmn_ref, qmr_ref,
                     ka_ref, va_ref, kd_ref, vd_ref, kmn_ref, vm_ref, kpe_ref,
                     oa_o, od_o, om_o, *, lam_init):
    part = dict(ka=ka_ref[...].astype(BF16), va=va_ref[...].astype(BF16),
                kd=kd_ref[...].astype(BF16), vd=vd_ref[...].astype(BF16),
                kmn=kmn_ref[...], vm=vm_ref[...], kpe=kpe_ref[...].astype(BF16))
    _three_mixers(scal_ref, subln_ref, (qa_ref[...], qd_ref[...], qmn_ref[...], qmr_ref[...]),
                  [part], oa_o, od_o, om_o, lam_init)


def _attn_lat_kernel(scal_ref, subln_ref, wukv_ref, qa_ref, qd_ref, qmn_ref, qmr_ref,
                     ka_ref, va_ref, kd_ref, vd_ref, kmn_ref, vm_ref, kpe_ref,
                     cka_ref, cva_ref, ckd_ref, cvd_ref, cckv_ref, ckpe_ref,
                     oa_o, od_o, om_o, *, lam_init):
    n = pl.program_id(1)
    kv_c = _dot(cckv_ref[...].astype(BF16), wukv_ref[...])
    cached = dict(ka=cka_ref[...].astype(BF16), va=cva_ref[...].astype(BF16),
                  kd=ckd_ref[...].astype(BF16), vd=cvd_ref[...].astype(BF16),
                  kmn=kv_c[:, 0:512].astype(BF16), vm=kv_c[:, 512:1024].astype(BF16),
                  kpe=ckpe_ref[...].astype(BF16))
    span = 3 * Q_BLOCK
    start = pl.multiple_of(jnp.clip((n - 1) * Q_BLOCK, 0, LAT_SEQ - span), Q_BLOCK)
    qpos = n * Q_BLOCK + lax.broadcasted_iota(jnp.int32, (Q_BLOCK, span), 0)
    kpos = start + lax.broadcasted_iota(jnp.int32, (Q_BLOCK, span), 1)
    mask_a = jnp.abs(qpos - kpos) <= WINDOW
    new = dict(ka=ka_ref[pl.ds(start, span), :].astype(BF16),
               va=va_ref[pl.ds(start, span), :].astype(BF16), mask_a=mask_a,
               kd=kd_ref[...].astype(BF16), vd=vd_ref[...].astype(BF16),
               kmn=kmn_ref[...], vm=vm_ref[...], kpe=kpe_ref[...].astype(BF16))
    _three_mixers(scal_ref, subln_ref, (qa_ref[...], qd_ref[...], qmn_ref[...], qmr_ref[...]),
                  [cached, new], oa_o, od_o, om_o, lam_init)


_SMEM_SPEC = pl.BlockSpec(memory_space=pltpu.SMEM)


def _attention_ctx(scal, subln, proj, lam_init):
    qa, ka, va, qd, kd, vd, qmn, qmr, _, kpe, kmn, vm = proj
    t = CTX_SEQ

    def blk(w):
        return pl.BlockSpec((t, w), lambda b: (b, 0))

    ins = [qa, qd, qmn, qmr, ka, va, kd, vd, kmn, vm, kpe]
    return pl.pallas_call(
        functools.partial(_attn_ctx_kernel, lam_init=lam_init),
        out_shape=[jax.ShapeDtypeStruct((N_CTX, 512), BF16)] * 3,
        grid=(N_CTX_BATCH,),
        in_specs=[_SMEM_SPEC, pl.BlockSpec((1, LANES), lambda b: (0, 0))]
                 + [blk(a.shape[1]) for a in ins],
        out_specs=[blk(512)] * 3,
        compiler_params=pltpu.CompilerParams(vmem_limit_bytes=VMEM_LIMIT),
        name="attention_ctx",
    )(scal, subln, *ins)


def _attention_lat(scal, subln, wukv, proj, caches, layer, lam_init):
    qa, ka, va, qd, kd, vd, qmn, qmr, _, kpe, kmn, vm = proj
    nq = LAT_SEQ // Q_BLOCK
    q_off = N_CTX // Q_BLOCK
    kv_off = N_CTX // LAT_SEQ

    def qblk(w):
        return pl.BlockSpec((Q_BLOCK, w), lambda b, n: (q_off + b * nq + n, 0))

    def kvblk(w):
        return pl.BlockSpec((LAT_SEQ, w), lambda b, n: (kv_off + b, 0))

    def cblk(w):
        return pl.BlockSpec((None, None, PAST, w), lambda b, n: (b, layer, 0, 0))

    qs = [qa, qd, qmn, qmr]
    kvs = [ka, va, kd, vd, kmn, vm, kpe]
    return pl.pallas_call(
        functools.partial(_attn_lat_kernel, lam_init=lam_init),
        out_shape=[jax.ShapeDtypeStruct((N_LAT, 512), BF16)] * 3,
        grid=(N_LAT_BATCH, nq),
        in_specs=[_SMEM_SPEC, pl.BlockSpec((1, LANES), lambda b, n: (0, 0)),
                  pl.BlockSpec(wukv.shape, lambda b, n: (0, 0))]
                 + [qblk(a.shape[1]) for a in qs] + [kvblk(a.shape[1]) for a in kvs]
                 + [cblk(a.shape[-1]) for a in caches],
        out_specs=[pl.BlockSpec((Q_BLOCK, 512), lambda b, n: (b * nq + n, 0))] * 3,
        compiler_params=pltpu.CompilerParams(vmem_limit_bytes=VMEM_LIMIT),
        name="attention_lat",
    )(scal, subln, wukv, *qs, *kvs, *caches)


def _route(scores, biased):
    per = N_EXPERTS // N_GROUPS
    tm = scores.shape[1]
    sub = lax.broadcasted_iota(jnp.int32, (per, tm), 0).astype(F32)
    groups = [biased[g * per:(g + 1) * per, :] for g in range(N_GROUPS)]
    gscore = []
    for v in groups:
        m1 = jnp.max(v, axis=0, keepdims=True)
        first = jnp.min(jnp.where(v == m1, sub, float(per)), axis=0, keepdims=True)
        m2 = jnp.max(jnp.where(sub == first, -jnp.inf, v), axis=0, keepdims=True)
        gscore.append(m1 + m2)
    vals = []
    for g in range(N_GROUPS):
        rank = jnp.zeros((1, tm), F32)
        for o in range(N_GROUPS):
            if o == g:
                continue
            ahead = (gscore[o] >= gscore[g]) if o < g else (gscore[o] > gscore[g])
            rank = rank + jnp.where(ahead, 1.0, 0.0)
        vals.append(jnp.where(rank < TOPK_GROUPS, groups[g], NEG))
    idx = [sub + float(g * per) for g in range(N_GROUPS)]
    chosen = [jnp.zeros((per, tm), F32) for _ in range(N_GROUPS)]
    for _ in range(TOP_K):
        best = functools.reduce(jnp.maximum, [jnp.max(v, axis=0, keepdims=True) for v in vals])
        cand = functools.reduce(jnp.minimum, [
            jnp.min(jnp.where(v == best, i, float(N_EXPERTS)), axis=0, keepdims=True)
            for v, i in zip(vals, idx)])
        for g in range(N_GROUPS):
            hit = idx[g] == cand
            chosen[g] = jnp.where(hit, 1.0, chosen[g])
            vals[g] = jnp.where(hit, -jnp.inf, vals[g])
    w = [chosen[g] * scores[g * per:(g + 1) * per, :] for g in range(N_GROUPS)]
    total = functools.reduce(lambda a, b: a + b, [jnp.sum(x, axis=0, keepdims=True) for x in w])
    return jnp.concatenate([x / total * ROUTED_SCALE for x in w], axis=0)


def _post_kernel(x_ref, mod_ref, oa_ref, od_ref, om_ref, gpre_ref, wgl_ref, wa_ref, wb_ref, wc_ref,
                 wo_ref, gpost_ref, gffn_ref, rwt_ref, rb_ref, xmid_o, h2_o, gates_o):
    x = x_ref[...]
    m = mod_ref[0]
    d = D_MODEL
    sh1, sc1, g1, sh2, sc2 = (m[:, 0:d], m[:, d:2 * d], m[:, 2 * d:3 * d], m[:, 3 * d:4 * d],
                              m[:, 4 * d:5 * d])
    h = _rms(x, gpre_ref[...]) * (1.0 + sc1) + sh1
    gate = jax.nn.sigmoid(_dot(h.astype(BF16), wgl_ref[...]))
    merged = (gate[:, 0:d] * _dot(oa_ref[...], wa_ref[...])
              + gate[:, d:2 * d] * _dot(od_ref[...], wb_ref[...])
              + gate[:, 2 * d:3 * d] * _dot(om_ref[...], wc_ref[...]))
    a = _dot(merged.astype(BF16), wo_ref[...])
    xm = x + g1 * _rms(a, gpost_ref[...])
    xmid_o[...] = xm
    h2 = (_rms(xm, gffn_ref[...]) * (1.0 + sc2) + sh2).astype(BF16)
    h2_o[...] = h2
    scores = jax.nn.sigmoid(_dot_nt(rwt_ref[...], h2))
    gates_o[...] = _route(scores, scores + rb_ref[...])


def _post_attention(x, mod_l, oa, od, om, gpre, wgl, wa, wb, wc, wo, gpost, gffn, rwt, rb):
    tm = ROW_TILE

    def row(i):
        return (i, 0)

    def full(i):
        return (0, 0)

    weights = [gpre, wgl, wa, wb, wc, wo, gpost, gffn, rwt, rb]
    return pl.pallas_call(
        _post_kernel,
        out_shape=[jax.ShapeDtypeStruct((N_TOK, D_MODEL), F32),
                   jax.ShapeDtypeStruct((N_TOK, D_MODEL), BF16),
                   jax.ShapeDtypeStruct((N_EXPERTS, N_TOK), F32)],
        grid=(N_TOK // tm,),
        in_specs=[pl.BlockSpec((tm, D_MODEL), row),
                  pl.BlockSpec((1, 1, 6 * D_MODEL), lambda i: (_mod_row_of_tile(i, tm), 0, 0)),
                  pl.BlockSpec((tm, 512), row), pl.BlockSpec((tm, 512), row),
                  pl.BlockSpec((tm, 512), row)]
                 + [pl.BlockSpec(w.shape, full) for w in weights],
        out_specs=[pl.BlockSpec((tm, D_MODEL), row), pl.BlockSpec((tm, D_MODEL), row),
                   pl.BlockSpec((N_EXPERTS, tm), lambda i: (0, i))],
        compiler_params=pltpu.CompilerParams(vmem_limit_bytes=VMEM_LIMIT),
        name="post_attention",
    )(x, mod_l, oa, od, om, *weights)


def _ffn(x, wg, wu):
    return _silu(_dot(x, wg)) * _dot(x, wu)


def _moe_kernel(h_ref, gates_ref, wg_ref, wu_ref, wd_ref, swg_ref, swu_ref, swd_ref,
                xmid_ref, mod_ref, gpost_ref, out_ref, acc_ref):
    e = pl.program_id(1)
    x = h_ref[...]

    @pl.when(e == 0)
    def _():
        hid = _ffn(x, swg_ref[...].astype(BF16), swu_ref[...].astype(BF16))
        acc_ref[...] = _dot(hid.astype(BF16), swd_ref[...].astype(BF16))

    gates = gates_ref[...]
    lane = lax.broadcasted_iota(jnp.int32, gates.shape, 1)
    gate = jnp.sum(jnp.where(lane == e, gates, 0.0), axis=1, keepdims=True)
    hid = _ffn(x, wg_ref[0].astype(BF16), wu_ref[0].astype(BF16)) * gate
    acc_ref[...] += _dot(hid.astype(BF16), wd_ref[0].astype(BF16))

    @pl.when(e == N_EXPERTS - 1)
    def _():
        g2 = mod_ref[0][:, 5 * D_MODEL:6 * D_MODEL]
        out_ref[...] = xmid_ref[...] + g2 * _rms(acc_ref[...], gpost_ref[...])


def _moe(h2, gates, wg, wu, wd, swg, swu, swd, xmid, mod_l, gpost, layer):
    tm = MOE_ROW_TILE

    def row(i, e):
        return (i, 0)

    def full(i, e):
        return (0, 0)

    def expert(i, e):
        return (layer, e, 0, 0)

    def shared(i, e):
        return (layer, 0, 0)

    return pl.pallas_call(
        _moe_kernel,
        out_shape=jax.ShapeDtypeStruct((N_TOK, D_MODEL), F32),
        grid=(N_TOK // tm, N_EXPERTS),
        in_specs=[pl.BlockSpec((tm, D_MODEL), row),
                  pl.BlockSpec((tm, N_EXPERTS), row),
                  pl.BlockSpec((None, 1, D_MODEL, F_EXPERT), expert),
                  pl.BlockSpec((None, 1, D_MODEL, F_EXPERT), expert),
                  pl.BlockSpec((None, 1, F_EXPERT, D_MODEL), expert),
                  pl.BlockSpec((None, D_MODEL, F_EXPERT), shared),
                  pl.BlockSpec((None, D_MODEL, F_EXPERT), shared),
                  pl.BlockSpec((None, F_EXPERT, D_MODEL), shared),
                  pl.BlockSpec((tm, D_MODEL), row),
                  pl.BlockSpec((1, 1, 6 * D_MODEL), lambda i, e: (_mod_row_of_tile(i, tm), 0, 0)),
                  pl.BlockSpec((1, D_MODEL), full)],
        out_specs=pl.BlockSpec((tm, D_MODEL), row),
        scratch_shapes=[pltpu.VMEM((tm, D_MODEL), F32)],
        compiler_params=pltpu.CompilerParams(
            dimension_semantics=("arbitrary", "arbitrary"), vmem_limit_bytes=VMEM_LIMIT),
        name="moe",
    )(h2, gates, wg, wu, wd, swg, swu, swd, xmid, mod_l, gpost)


def _rope_tables():
    t = np.arange(LAT_SEQ)
    pos = np.stack([t // GRID_W, t % GRID_W], axis=1).astype(np.float64)

    def table(half):
        inv = ROPE_BASE ** (-np.arange(half, dtype=np.float64) / half)
        lane = np.arange(LANES)
        axis = (lane // (2 * half)) % 2
        freq = inv[lane % half]
        ang = pos[:, axis] * freq[None, :]
        sign = np.where((lane % (2 * half)) < half, -1.0, 1.0)
        cos = np.concatenate([np.ones((ROW_TILE, LANES)), np.cos(ang)], axis=0)
        sin = np.concatenate([np.zeros((ROW_TILE, LANES)), np.sin(ang) * sign[None, :]], axis=0)
        return jnp.asarray(cos, F32), jnp.asarray(sin, F32)

    cos64, sin64 = table(16)
    cos32, sin32 = table(8)
    return cos64, sin64, cos32, sin32


def _head_perm_a():
    cols = []
    for g in range(4):
        for kvh in range(2):
            h = kvh * 4 + g
            cols.extend(range(h * 64, (h + 1) * 64))
    return np.asarray(cols)


def kernel(x_prompt, x_sample, cache_swa_k, cache_swa_v, cache_dif_k, cache_dif_v, cache_mla_ckv, cache_mla_kpe, c, c_ctx, w_mod, b_mod, norm_attn_pre, norm_attn_post, norm_ffn_pre, norm_ffn_post, w_in, swa_sink, dif_lq1, dif_lk1, dif_lq2, dif_lk2, dif_subln, mla_q_norm, mla_w_uq, mla_kv_norm, mla_w_ukv, w_branch_a, w_branch_b, w_branch_c, w_out, router_w, router_bias, moe_w_gate, moe_w_up, moe_w_down, shared_w_gate, shared_w_up, shared_w_down):
    x = jnp.concatenate([x_prompt.reshape(N_CTX, D_MODEL), x_sample.reshape(N_LAT, D_MODEL)], axis=0)
    cvec = jnp.concatenate([c_ctx[None, :], c, jnp.zeros((8 - 1 - N_LAT_BATCH, D_MODEL), F32)], axis=0)
    mod = _modulation(cvec, w_mod, b_mod)
    tables = _rope_tables()
    perm_a = _head_perm_a()
    uq_cols = np.concatenate([np.concatenate([np.arange(h * 96, h * 96 + 64) for h in range(8)]),
                              np.concatenate([np.arange(h * 96 + 64, h * 96 + 96) for h in range(8)])])
    ukv_cols = np.concatenate([np.concatenate([np.arange(h * 128, h * 128 + 64) for h in range(8)]),
                               np.concatenate([np.arange(h * 128 + 64, h * 128 + 128) for h in range(8)])])

    caches = [cache_swa_k.reshape(N_LAT_BATCH, DEPTH, PAST, 128),
              cache_swa_v.reshape(N_LAT_BATCH, DEPTH, PAST, 128),
              cache_dif_k.reshape(N_LAT_BATCH, DEPTH, PAST, 512),
              cache_dif_v.reshape(N_LAT_BATCH, DEPTH, PAST, 512),
              cache_mla_ckv,
              jnp.tile(cache_mla_kpe, (1, 1, 1, 4))]

    states = [[] for _ in range(6)]
    for l in range(DEPTH):
        lam_init = 0.8 - 0.6 * math.exp(-0.3 * l)
        w = w_in[l]
        kpe_w = w[:, 2688:2720]
        w1 = jnp.concatenate([w[:, 0:512][:, perm_a], w[:, 512:2688], kpe_w, kpe_w, kpe_w, kpe_w],
                             axis=1).astype(BF16)
        wgl = w[:, 2720:].astype(BF16)
        wuq = mla_w_uq[l][:, uq_cols].astype(BF16)
        wukv = mla_w_ukv[l][:, ukv_cols].astype(BF16)
        mod_l = mod[l].reshape(8, 1, 6 * D_MODEL)

        proj = _in_projection(x, mod_l, norm_attn_pre[l][None, :], w1, mla_q_norm[l][None, :], wuq,
                              mla_kv_norm[l][None, :], wukv, tables)
        _, ka, va, _, kd, vd, _, _, ckv, kpe, _, _ = proj
        for i, (s, width) in enumerate(zip((ka, va, kd, vd, ckv, kpe), (128, 128, 512, 512, 128, 32))):
            states[i].append(s[:N_CTX, :width].reshape(N_CTX_BATCH, CTX_SEQ, width))

        lam = (jnp.exp(jnp.sum(dif_lq1[l] * dif_lk1[l])) - jnp.exp(jnp.sum(dif_lq2[l] * dif_lk2[l]))
               + lam_init)
        scal = jnp.concatenate([swa_sink[l], lam[None]]).astype(F32)
        subln = dif_subln[l][None, :]
        oc = _attention_ctx(scal, subln, proj, lam_init)
        ol = _attention_lat(scal, subln, wukv, proj, caches, l, lam_init)
        oa, od, om = (jnp.concatenate([a, b], axis=0) for a, b in zip(oc, ol))

        xmid, h2, gates_t = _post_attention(
            x, mod_l, oa, od, om, norm_attn_pre[l][None, :], wgl,
            w_branch_a[l][perm_a, :].astype(BF16), w_branch_b[l].astype(BF16),
            w_branch_c[l].astype(BF16), w_out[l].astype(BF16), norm_attn_post[l][None, :],
            norm_ffn_pre[l][None, :], router_w[l].T.astype(BF16), router_bias[l][:, None])
        x = _moe(h2, gates_t.T, moe_w_gate, moe_w_up, moe_w_down, shared_w_gate, shared_w_up,
                 shared_w_down, xmid, mod_l, norm_ffn_post[l][None, :], l)

    y_p = x[:N_CTX].reshape(N_CTX_BATCH, CTX_SEQ, D_MODEL)
    y_s = x[N_CTX:].reshape(N_LAT_BATCH, LAT_SEQ, D_MODEL)
    st = [jnp.stack(s, axis=1) for s in states]
    return (y_p, y_s,
            st[0].reshape(N_CTX_BATCH, DEPTH, CTX_SEQ, 2, 64),
            st[1].reshape(N_CTX_BATCH, DEPTH, CTX_SEQ, 2, 64),
            st[2].reshape(N_CTX_BATCH, DEPTH, CTX_SEQ, 4, 2, 64),
            st[3].reshape(N_CTX_BATCH, DEPTH, CTX_SEQ, 4, 128),
            st[4], st[5])
```

```python
import functools
import math

import numpy as np
import jax
import jax.numpy as jnp
from jax import lax
from jax.experimental import pallas as pl
from jax.experimental.pallas import tpu as pltpu

F32 = jnp.float32
BF16 = jnp.bfloat16

D_MODEL = 1024
N_CTX_BATCH, CTX_SEQ = 16, 256
N_LAT_BATCH, LAT_SEQ = 2, 1024
PAST = 256
N_CTX = N_CTX_BATCH * CTX_SEQ
N_LAT = N_LAT_BATCH * LAT_SEQ
N_TOK = N_CTX + N_LAT
DEPTH = 2
GRID_W = 64
WINDOW = 128
N_EXPERTS = 64
N_GROUPS = 8
TOPK_GROUPS = 4
TOP_K = 6
F_EXPERT = 256
ROUTED_SCALE = 2.5
ROPE_BASE = 10000.0
EPS = 1e-6
NEG = -1e30

LANES = 128
ROW_TILE = 256
Q_BLOCK = 128
ACC_ROWS = 8
MOE_SPARE_TOKENS = 8
MOE_ZERO_ROWS = 64
MOE_VMEM_LIMIT = 62 * 1024 * 1024
MOE_SUB = 128
MOE_RMW_GROUP = 8
TOK_KEY = 8192
VMEM_LIMIT = 56 * 1024 * 1024

C_QA, C_KA, C_VA, C_QD, C_KD, C_VD, C_QC, C_KVC, C_KPE, C_END = (
    0, 512, 640, 768, 1280, 1792, 2304, 2560, 2688, 2816)


def _mod_row_of_tile(i, tile):
    ctx_tiles = N_CTX // tile
    per_batch = LAT_SEQ // tile
    return jnp.where(i < ctx_tiles, 0, 1 + (i - ctx_tiles) // per_batch)


def _rms(x, g):
    return x * lax.rsqrt(jnp.mean(x * x, axis=-1, keepdims=True) + EPS) * g


def _dot(a, b):
    return jnp.dot(a, b, preferred_element_type=F32)


def _dot_nt(a, b):
    return lax.dot_general(a, b, (((1,), (1,)), ((), ())), preferred_element_type=F32)


def _silu(x):
    return x * jax.nn.sigmoid(x)


def _mod_kernel(c_ref, w_ref, b_ref, o_ref):
    c = c_ref[...]
    o_ref[0] = _dot(_silu(c).astype(BF16), w_ref[0].astype(BF16)) + b_ref[0]


def _modulation(cvec, w_mod, b_mod):
    tn = 1536
    n = w_mod.shape[-1]
    return pl.pallas_call(
        _mod_kernel,
        out_shape=jax.ShapeDtypeStruct((DEPTH, 8, n), F32),
        grid=(DEPTH, n // tn),
        in_specs=[
            pl.BlockSpec((8, D_MODEL), lambda l, j: (0, 0)),
            pl.BlockSpec((1, D_MODEL, tn), lambda l, j: (l, 0, j)),
            pl.BlockSpec((1, 1, tn), lambda l, j: (l, 0, j)),
        ],
        out_specs=pl.BlockSpec((1, 8, tn), lambda l, j: (l, 0, j)),
        compiler_params=pltpu.CompilerParams(vmem_limit_bytes=VMEM_LIMIT),
        name="modulation",
    )(cvec, w_mod, b_mod.reshape(DEPTH, 1, n))


def _rope128(x, cos, sin, half):
    lane = lax.broadcasted_iota(jnp.int32, x.shape, 1)
    first = (lane % (2 * half)) < half
    partner = jnp.where(first, pltpu.roll(x, LANES - half, 1), pltpu.roll(x, half, 1))
    return x * cos + partner * sin


def _rope_cols(x, cos, sin, half):
    chunks = [_rope128(x[:, c:c + LANES], cos, sin, half) for c in range(0, x.shape[1], LANES)]
    return chunks[0] if len(chunks) == 1 else jnp.concatenate(chunks, axis=1)


def _in_kernel(x_ref, mod_ref, g_ref, w1_ref, qn_ref, wuq_ref, kvn_ref, wukv_ref,
               cos64_ref, sin64_ref, cos32_ref, sin32_ref,
               qa_o, ka_o, va_o, qd_o, kd_o, vd_o, qmn_o, qmr_o, ckv_o, kpe_o, kmn_o, vm_o):
    x = x_ref[...]
    m = mod_ref[0]
    sh1, sc1 = m[:, 0:D_MODEL], m[:, D_MODEL:2 * D_MODEL]
    h = _rms(x, g_ref[...]) * (1.0 + sc1) + sh1
    z = _dot(h.astype(BF16), w1_ref[...])
    cos64, sin64 = cos64_ref[...], sin64_ref[...]
    cos32, sin32 = cos32_ref[...], sin32_ref[...]
    qa_o[...] = _rope_cols(z[:, C_QA:C_KA], cos64, sin64, 16).astype(BF16)
    ka_o[...] = _rope_cols(z[:, C_KA:C_VA], cos64, sin64, 16)
    va_o[...] = z[:, C_VA:C_QD]
    qd_o[...] = _rope_cols(z[:, C_QD:C_KD], cos64, sin64, 16).astype(BF16)
    kd_o[...] = _rope_cols(z[:, C_KD:C_VD], cos64, sin64, 16)
    vd_o[...] = z[:, C_VD:C_QC]
    kpe_o[...] = _rope_cols(z[:, C_KPE:C_END], cos32, sin32, 8)
    qn = _rms(z[:, C_QC:C_KVC], qn_ref[...])
    qm = _dot(qn.astype(BF16), wuq_ref[...])
    qmn_o[...] = qm[:, 0:512].astype(BF16)
    qmr_o[...] = _rope_cols(qm[:, 512:768], cos32, sin32, 8).astype(BF16)
    ckv = _rms(z[:, C_KVC:C_KPE], kvn_ref[...])
    ckv_o[...] = ckv
    kv = _dot(ckv.astype(BF16), wukv_ref[...])
    kmn_o[...] = kv[:, 0:512].astype(BF16)
    vm_o[...] = kv[:, 512:1024].astype(BF16)


def _in_projection(x, mod_l, g_pre, w1, qnorm, wuq, kvnorm, wukv, tables):
    tm = ROW_TILE
    n_tiles = N_TOK // tm
    ctx_tiles = N_CTX // tm
    lat_tiles_per_batch = LAT_SEQ // tm

    def row(i):
        return (i, 0)

    def full(i):
        return (0, 0)

    def table_row(i):
        return (jnp.where(i < ctx_tiles, 0, 1 + (i - ctx_tiles) % lat_tiles_per_batch), 0)

    table_spec = pl.BlockSpec((tm, LANES), table_row)
    widths = [(512, BF16), (128, F32), (128, F32), (512, BF16), (512, F32), (512, F32),
              (512, BF16), (256, BF16), (128, F32), (128, F32), (512, BF16), (512, BF16)]
    return pl.pallas_call(
        _in_kernel,
        out_shape=[jax.ShapeDtypeStruct((N_TOK, w), dt) for w, dt in widths],
        grid=(n_tiles,),
        in_specs=[
            pl.BlockSpec((tm, D_MODEL), row),
            pl.BlockSpec((1, 1, 6 * D_MODEL), lambda i: (_mod_row_of_tile(i, tm), 0, 0)),
            pl.BlockSpec((1, D_MODEL), full),
            pl.BlockSpec(w1.shape, full),
            pl.BlockSpec((1, 256), full),
            pl.BlockSpec(wuq.shape, full),
            pl.BlockSpec((1, 128), full),
            pl.BlockSpec(wukv.shape, full),
            table_spec, table_spec, table_spec, table_spec,
        ],
        out_specs=[pl.BlockSpec((tm, w), row) for w, _ in widths],
        compiler_params=pltpu.CompilerParams(vmem_limit_bytes=VMEM_LIMIT),
        name="in_projection",
    )(x, mod_l, g_pre, w1, qnorm, wuq, kvnorm, wukv, *tables)


def _attend(q, parts, scale, sink=None):
    scores = []
    for k, _, mask in parts:
        s = _dot_nt(q, k) * scale
        if mask is not None:
            s = jnp.where(mask, s, NEG)
        scores.append(s)
    m = functools.reduce(jnp.maximum, [jnp.max(s, axis=-1, keepdims=True) for s in scores])
    if sink is not None:
        m = jnp.maximum(m, sink)
    den = None
    out = None
    for s, (_, v, _) in zip(scores, parts):
        e = jnp.exp(s - m)
        d = jnp.sum(e, axis=-1, keepdims=True)
        o = _dot(e.astype(BF16), v)
        den = d if den is None else den + d
        out = o if out is None else out + o
    if sink is not None:
        den = den + jnp.exp(sink - m)
    return out / den


def _three_mixers(scal_ref, subln_ref, q_blocks, kv_parts, oa_o, od_o, om_o, lam_init):
    qa, qd, qmn, qmr = q_blocks
    rows = qa.shape[0]
    lane = lax.broadcasted_iota(jnp.int32, (rows, LANES), 1)
    low = lane < 64
    zero = jnp.zeros((rows, LANES), BF16)

    for g in range(4):
        q128 = qa[:, g * LANES:(g + 1) * LANES]
        halves = []
        for kvh in range(2):
            qh = jnp.where(low if kvh == 0 else ~low, q128, zero)
            parts = [(p["ka"], p["va"], p.get("mask_a")) for p in kv_parts]
            halves.append(_attend(qh, parts, 0.125, sink=scal_ref[kvh * 4 + g]))
        oa_o[:, g * LANES:(g + 1) * LANES] = jnp.where(low, halves[0], halves[1]).astype(BF16)

    lam = scal_ref[8]
    subln = subln_ref[...]
    for h in range(4):
        sl = slice(h * LANES, (h + 1) * LANES)
        q128 = qd[:, sl]
        parts = [(p["kd"][:, sl], p["vd"][:, sl], None) for p in kv_parts]
        o1 = _attend(jnp.where(low, q128, zero), parts, 0.125)
        o2 = _attend(jnp.where(low, zero, q128), parts, 0.125)
        od = o1 - lam * o2
        od_o[:, sl] = (_rms(od, subln) * (1.0 - lam_init)).astype(BF16)

    scale_c = 1.0 / math.sqrt(96.0)
    for i in range(4):
        sl = slice(i * LANES, (i + 1) * LANES)
        keys = [jnp.concatenate([p["kmn"][:, sl], p["kpe"]], axis=1) for p in kv_parts]
        halves = []
        for sub in range(2):
            h = 2 * i + sub
            qn = jnp.where(low if sub == 0 else ~low, qmn[:, sl], zero)
            qr128 = qmr[:, (h // 4) * LANES:(h // 4 + 1) * LANES]
            qr = jnp.where((lane // 32) == (h % 4), qr128, zero)
            q256 = jnp.concatenate([qn, qr], axis=1)
            parts = [(k, p["vm"][:, sl], None) for k, p in zip(keys, kv_parts)]
            halves.append(_attend(q256, parts, scale_c))
        om_o[:, sl] = jnp.where(low, halves[0], halves[1]).astype(BF16)


def _attn_ctx_kernel(scal_ref, subln_ref, qa_ref, qd_ref, qmn_ref, qmr_ref,
                     ka_ref, va_ref, kd_ref, vd_ref, kmn_ref, vm_ref, kpe_ref,
                     oa_o, od_o, om_o, *, lam_init):
    part = dict(ka=ka_ref[...].astype(BF16), va=va_ref[...].astype(BF16),
                kd=kd_ref[...].astype(BF16), vd=vd_ref[...].astype(BF16),
                kmn=kmn_ref[...], vm=vm_ref[...], kpe=kpe_ref[...].astype(BF16))
    _three_mixers(scal_ref, subln_ref, (qa_ref[...], qd_ref[...], qmn_ref[...], qmr_ref[...]),
                  [part], oa_o, od_o, om_o, lam_init)


def _attn_lat_kernel(scal_ref, subln_ref, wukv_ref, qa_ref, qd_ref, qmn_ref, qmr_ref,
                     ka_ref, va_ref, kd_ref, vd_ref, kmn_ref, vm_ref, kpe_ref,
                     cka_ref, cva_ref, ckd_ref, cvd_ref, cckv_ref, ckpe_ref,
                     oa_o, od_o, om_o, *, lam_init):
    n = pl.program_id(1)
    kv_c = _dot(cckv_ref[...].astype(BF16), wukv_ref[...])
    cached = dict(ka=cka_ref[...].astype(BF16), va=cva_ref[...].astype(BF16),
                  kd=ckd_ref[...].astype(BF16), vd=cvd_ref[...].astype(BF16),
                  kmn=kv_c[:, 0:512].astype(BF16), vm=kv_c[:, 512:1024].astype(BF16),
                  kpe=ckpe_ref[...].astype(BF16))
    span = 3 * Q_BLOCK
    start = pl.multiple_of(jnp.clip((n - 1) * Q_BLOCK, 0, LAT_SEQ - span), Q_BLOCK)
    qpos = n * Q_BLOCK + lax.broadcasted_iota(jnp.int32, (Q_BLOCK, span), 0)
    kpos = start + lax.broadcasted_iota(jnp.int32, (Q_BLOCK, span), 1)
    mask_a = jnp.abs(qpos - kpos) <= WINDOW
    new = dict(ka=ka_ref[pl.ds(start, span), :].astype(BF16),
               va=va_ref[pl.ds(start, span), :].astype(BF16), mask_a=mask_a,
               kd=kd_ref[...].astype(BF16), vd=vd_ref[...].astype(BF16),
               kmn=kmn_ref[...], vm=vm_ref[...], kpe=kpe_ref[...].astype(BF16))
    _three_mixers(scal_ref, subln_ref, (qa_ref[...], qd_ref[...], qmn_ref[...], qmr_ref[...]),
                  [cached, new], oa_o, od_o, om_o, lam_init)


_SMEM_SPEC = pl.BlockSpec(memory_space=pltpu.SMEM)


def _attention_ctx(scal, subln, proj, lam_init):
    qa, ka, va, qd, kd, vd, qmn, qmr, _, kpe, kmn, vm = proj
    t = CTX_SEQ

    def blk(w):
        return pl.BlockSpec((t, w), lambda b: (b, 0))

    ins = [qa, qd, qmn, qmr, ka, va, kd, vd, kmn, vm, kpe]
    return pl.pallas_call(
        functools.partial(_attn_ctx_kernel, lam_init=lam_init),
        out_shape=[jax.ShapeDtypeStruct((N_CTX, 512), BF16)] * 3,
        grid=(N_CTX_BATCH,),
        in_specs=[_SMEM_SPEC, pl.BlockSpec((1, LANES), lambda b: (0, 0))]
                 + [blk(a.shape[1]) for a in ins],
        out_specs=[blk(512)] * 3,
        compiler_params=pltpu.CompilerParams(vmem_limit_bytes=VMEM_LIMIT),
        name="attention_ctx",
    )(scal, subln, *ins)


def _attention_lat(scal, subln, wukv, proj, caches, layer, lam_init):
    qa, ka, va, qd, kd, vd, qmn, qmr, _, kpe, kmn, vm = proj
    nq = LAT_SEQ // Q_BLOCK
    q_off = N_CTX // Q_BLOCK
    kv_off = N_CTX // LAT_SEQ

    def qblk(w):
        return pl.BlockSpec((Q_BLOCK, w), lambda b, n: (q_off + b * nq + n, 0))

    def kvblk(w):
        return pl.BlockSpec((LAT_SEQ, w), lambda b, n: (kv_off + b, 0))

    def cblk(w):
        return pl.BlockSpec((None, None, PAST, w), lambda b, n: (b, layer, 0, 0))

    qs = [qa, qd, qmn, qmr]
    kvs = [ka, va, kd, vd, kmn, vm, kpe]
    return pl.pallas_call(
        functools.partial(_attn_lat_kernel, lam_init=lam_init),
        out_shape=[jax.ShapeDtypeStruct((N_LAT, 512), BF16)] * 3,
        grid=(N_LAT_BATCH, nq),
        in_specs=[_SMEM_SPEC, pl.BlockSpec((1, LANES), lambda b, n: (0, 0)),
                  pl.BlockSpec(wukv.shape, lambda b, n: (0, 0))]
                 + [qblk(a.shape[1]) for a in qs] + [kvblk(a.shape[1]) for a in kvs]
                 + [cblk(a.shape[-1]) for a in caches],
        out_specs=[pl.BlockSpec((Q_BLOCK, 512), lambda b, n: (b * nq + n, 0))] * 3,
        compiler_params=pltpu.CompilerParams(vmem_limit_bytes=VMEM_LIMIT),
        name="attention_lat",
    )(scal, subln, wukv, *qs, *kvs, *caches)


def _route(scores, biased):
    per = N_EXPERTS // N_GROUPS
    tm = scores.shape[1]
    sub = lax.broadcasted_iota(jnp.int32, (per, tm), 0).astype(F32)
    groups = [biased[g * per:(g + 1) * per, :] for g in range(N_GROUPS)]
    gscore = []
    for v in groups:
        m1 = jnp.max(v, axis=0, keepdims=True)
        first = jnp.min(jnp.where(v == m1, sub, float(per)), axis=0, keepdims=True)
        m2 = jnp.max(jnp.where(sub == first, -jnp.inf, v), axis=0, keepdims=True)
        gscore.append(m1 + m2)
    vals = []
    for g in range(N_GROUPS):
        rank = jnp.zeros((1, tm), F32)
        for o in range(N_GROUPS):
            if o == g:
                continue
            ahead = (gscore[o] >= gscore[g]) if o < g else (gscore[o] > gscore[g])
            rank = rank + jnp.where(ahead, 1.0, 0.0)
        vals.append(jnp.where(rank < TOPK_GROUPS, groups[g], NEG))
    idx = [sub + float(g * per) for g in range(N_GROUPS)]
    picks, weights = [], []
    for _ in range(TOP_K):
        best = functools.reduce(jnp.maximum, [jnp.max(v, axis=0, keepdims=True) for v in vals])
        cand = functools.reduce(jnp.minimum, [
            jnp.min(jnp.where(v == best, i, float(N_EXPERTS)), axis=0, keepdims=True)
            for v, i in zip(vals, idx)])
        wsel = jnp.zeros((1, tm), F32)
        for g in range(N_GROUPS):
            hit = idx[g] == cand
            wsel = wsel + jnp.sum(jnp.where(hit, scores[g * per:(g + 1) * per, :], 0.0),
                                  axis=0, keepdims=True)
            vals[g] = jnp.where(hit, -jnp.inf, vals[g])
        picks.append(cand)
        weights.append(wsel)
    total = functools.reduce(lambda a, b: a + b, weights)
    pad = [jnp.zeros((1, tm), F32)] * (8 - TOP_K)
    experts = jnp.concatenate(picks + pad, axis=0).astype(jnp.int32)
    gates = jnp.concatenate([x / total * ROUTED_SCALE for x in weights] + pad, axis=0)
    return experts, gates


def _post_kernel(x_ref, mod_ref, oa_ref, od_ref, om_ref, gpre_ref, wgl_ref, wa_ref, wb_ref, wc_ref,
                 wo_ref, gpost_ref, gffn_ref, rwt_ref, rb_ref, xmid_o, h2_o, h2t_o, experts_o, gates_o):
    x = x_ref[...]
    m = mod_ref[0]
    d = D_MODEL
    sh1, sc1, g1, sh2, sc2 = (m[:, 0:d], m[:, d:2 * d], m[:, 2 * d:3 * d], m[:, 3 * d:4 * d],
                              m[:, 4 * d:5 * d])
    h = _rms(x, gpre_ref[...]) * (1.0 + sc1) + sh1
    gate = jax.nn.sigmoid(_dot(h.astype(BF16), wgl_ref[...]))
    merged = (gate[:, 0:d] * _dot(oa_ref[...], wa_ref[...])
              + gate[:, d:2 * d] * _dot(od_ref[...], wb_ref[...])
              + gate[:, 2 * d:3 * d] * _dot(om_ref[...], wc_ref[...]))
    a = _dot(merged.astype(BF16), wo_ref[...])
    xm = x + g1 * _rms(a, gpost_ref[...])
    xmid_o[...] = xm
    h2f = _rms(xm, gffn_ref[...]) * (1.0 + sc2) + sh2
    h2 = h2f.astype(BF16)
    h2_o[...] = h2
    for c in range(ACC_ROWS):
        h2t_o[pl.ds(c, x.shape[0], stride=ACC_ROWS), :] = h2f[:, c * LANES:(c + 1) * LANES]
    scores = jax.nn.sigmoid(_dot_nt(rwt_ref[...], h2))
    experts, gates = _route(scores, scores + rb_ref[...])
    experts_o[...] = experts
    gates_o[...] = gates


def _post_attention(x, mod_l, oa, od, om, gpre, wgl, wa, wb, wc, wo, gpost, gffn, rwt, rb):
    tm = ROW_TILE

    def row(i):
        return (i, 0)

    def full(i):
        return (0, 0)

    weights = [gpre, wgl, wa, wb, wc, wo, gpost, gffn, rwt, rb]
    return pl.pallas_call(
        _post_kernel,
        out_shape=[jax.ShapeDtypeStruct((N_TOK, D_MODEL), F32),
                   jax.ShapeDtypeStruct((N_TOK, D_MODEL), BF16),
                   jax.ShapeDtypeStruct((N_TOK * ACC_ROWS, LANES), F32),
                   jax.ShapeDtypeStruct((8, N_TOK), jnp.int32),
                   jax.ShapeDtypeStruct((8, N_TOK), F32)],
        grid=(N_TOK // tm,),
        in_specs=[pl.BlockSpec((tm, D_MODEL), row),
                  pl.BlockSpec((1, 1, 6 * D_MODEL), lambda i: (_mod_row_of_tile(i, tm), 0, 0)),
                  pl.BlockSpec((tm, 512), row), pl.BlockSpec((tm, 512), row),
                  pl.BlockSpec((tm, 512), row)]
                 + [pl.BlockSpec(w.shape, full) for w in weights],
        out_specs=[pl.BlockSpec((tm, D_MODEL), row), pl.BlockSpec((tm, D_MODEL), row),
                   pl.BlockSpec((tm * ACC_ROWS, LANES), row),
                   pl.BlockSpec((8, tm), lambda i: (0, i)), pl.BlockSpec((8, tm), lambda i: (0, i))],
        compiler_params=pltpu.CompilerParams(vmem_limit_bytes=VMEM_LIMIT),
        name="post_attention",
    )(x, mod_l, oa, od, om, *weights)


def _ffn(x, wg, wu):
    return _silu(_dot(x, wg)) * _dot(x, wu)


def _moe_kernel(off_ref, tok_ref, gate_ref, xs_ref, wg_ref, wu_ref, wd_ref, acc_ref,
                xbuf, ybuf, wgb, wub, wdb):
    e = pl.program_id(0)

    @pl.when(e == 0)
    def _():
        def clear(i, carry):
            r0 = pl.multiple_of(i * MOE_ZERO_ROWS, MOE_ZERO_ROWS)
            acc_ref[pl.ds(r0, MOE_ZERO_ROWS), :] = jnp.zeros((MOE_ZERO_ROWS, LANES), F32)
            return carry

        lax.fori_loop(0, acc_ref.shape[0] // MOE_ZERO_ROWS, clear, 0)

    wgb[...] = wg_ref[0].astype(BF16)
    wub[...] = wu_ref[0].astype(BF16)
    wdb[...] = wd_ref[0].astype(BF16)
    start = off_ref[e]
    count = off_ref[e + 1] - start

    def sub_tile(s, carry):
        base = start + s * MOE_SUB
        rows = jnp.minimum(MOE_SUB, count - s * MOE_SUB)
        for mi in range(MOE_SUB):
            tok = tok_ref[base + mi]
            src = pl.multiple_of(tok * ACC_ROWS, ACC_ROWS)
            xbuf[mi * ACC_ROWS:(mi + 1) * ACC_ROWS, :] = xs_ref[pl.ds(src, ACC_ROWS), :]
        x = jnp.concatenate([xbuf[pl.ds(c, MOE_SUB, stride=ACC_ROWS), :].astype(BF16)
                             for c in range(ACC_ROWS)], axis=1)
        hid = _ffn(x, wgb[...], wub[...])
        y = _dot(hid.astype(BF16), wdb[...])
        for c in range(ACC_ROWS):
            ybuf[pl.ds(c, MOE_SUB, stride=ACC_ROWS), :] = y[:, c * LANES:(c + 1) * LANES]
        for g0 in range(0, MOE_SUB, MOE_RMW_GROUP):
            updates = []
            for mi in range(g0, g0 + MOE_RMW_GROUP):
                live = mi < rows
                tok = jnp.where(live, tok_ref[base + mi], N_TOK)
                gate = jnp.where(live, gate_ref[base + mi], 0.0)
                dst = pl.multiple_of(tok * ACC_ROWS, ACC_ROWS)
                yv = ybuf[mi * ACC_ROWS:(mi + 1) * ACC_ROWS, :]
                updates.append((dst, acc_ref[pl.ds(dst, ACC_ROWS), :] + gate * yv))
            for dst, val in updates:
                acc_ref[pl.ds(dst, ACC_ROWS), :] = val
        return carry

    lax.fori_loop(0, (count + MOE_SUB - 1) // MOE_SUB, sub_tile, 0)


def _moe(off, tok_sorted, gate_sorted, h2t, wg, wu, wd, layer):
    def expert(e, *_):
        return (layer, e, 0, 0)

    def whole(e, *_):
        return (0, 0)

    acc_rows = (N_TOK + MOE_SPARE_TOKENS) * ACC_ROWS
    assert acc_rows % MOE_ZERO_ROWS == 0
    return pl.pallas_call(
        _moe_kernel,
        out_shape=jax.ShapeDtypeStruct((acc_rows, LANES), F32),
        grid_spec=pltpu.PrefetchScalarGridSpec(
            num_scalar_prefetch=3,
            grid=(N_EXPERTS,),
            in_specs=[pl.BlockSpec(h2t.shape, whole, pipeline_mode=pl.Buffered(1)),
                      pl.BlockSpec((None, 1, D_MODEL, F_EXPERT), expert),
                      pl.BlockSpec((None, 1, D_MODEL, F_EXPERT), expert),
                      pl.BlockSpec((None, 1, F_EXPERT, D_MODEL), expert)],
            out_specs=pl.BlockSpec((acc_rows, LANES), whole, pipeline_mode=pl.Buffered(1)),
            scratch_shapes=[pltpu.VMEM((MOE_SUB * ACC_ROWS, LANES), F32),
                            pltpu.VMEM((MOE_SUB * ACC_ROWS, LANES), F32),
                            pltpu.VMEM((D_MODEL, F_EXPERT), BF16),
                            pltpu.VMEM((D_MODEL, F_EXPERT), BF16),
                            pltpu.VMEM((F_EXPERT, D_MODEL), BF16)]),
        compiler_params=pltpu.CompilerParams(
            dimension_semantics=("arbitrary",), vmem_limit_bytes=MOE_VMEM_LIMIT),
        name="moe",
    )(off, tok_sorted, gate_sorted, h2t, wg, wu, wd)


def _final_kernel(acc_ref, h2_ref, xmid_ref, mod_ref, swg_ref, swu_ref, swd_ref, gpost_ref, out_ref):
    tm = h2_ref.shape[0]
    routed = jnp.concatenate(
        [acc_ref[pl.ds(c, tm, stride=ACC_ROWS), :] for c in range(ACC_ROWS)], axis=1)
    hid = _ffn(h2_ref[...], swg_ref[...], swu_ref[...])
    shared = _dot(hid.astype(BF16), swd_ref[...])
    g2 = mod_ref[0][:, 5 * D_MODEL:6 * D_MODEL]
    out_ref[...] = xmid_ref[...] + g2 * _rms(routed + shared, gpost_ref[...])


def _final(acc, h2, xmid, mod_l, swg, swu, swd, gpost):
    tm = ROW_TILE

    def row(i):
        return (i, 0)

    def full(i):
        return (0, 0)

    return pl.pallas_call(
        _final_kernel,
        out_shape=jax.ShapeDtypeStruct((N_TOK, D_MODEL), F32),
        grid=(N_TOK // tm,),
        in_specs=[pl.BlockSpec((tm * ACC_ROWS, LANES), row),
                  pl.BlockSpec((tm, D_MODEL), row),
                  pl.BlockSpec((tm, D_MODEL), row),
                  pl.BlockSpec((1, 1, 6 * D_MODEL), lambda i: (_mod_row_of_tile(i, tm), 0, 0)),
                  pl.BlockSpec(swg.shape, full), pl.BlockSpec(swu.shape, full),
                  pl.BlockSpec(swd.shape, full), pl.BlockSpec((1, D_MODEL), full)],
        out_specs=pl.BlockSpec((tm, D_MODEL), row),
        compiler_params=pltpu.CompilerParams(vmem_limit_bytes=VMEM_LIMIT),
        name="ffn_final",
    )(acc, h2, xmid, mod_l, swg, swu, swd, gpost)


def _routing_lists(experts, gates):
    e = experts[:TOP_K].reshape(-1)
    tok = jnp.tile(jnp.arange(N_TOK, dtype=jnp.int32), TOP_K)
    key, gate_sorted = lax.sort((e * TOK_KEY + tok, gates[:TOP_K].reshape(-1)), num_keys=1)
    counts = jnp.sum((e[None, :] == jnp.arange(N_EXPERTS, dtype=jnp.int32)[:, None]).astype(jnp.int32),
                     axis=1)
    off = jnp.concatenate([jnp.zeros((1,), jnp.int32), jnp.cumsum(counts).astype(jnp.int32)])
    pad = jnp.zeros((MOE_SUB,), jnp.int32)
    tok_sorted = jnp.concatenate([key & (TOK_KEY - 1), pad])
    gate_sorted = jnp.concatenate([gate_sorted, pad.astype(F32)])
    return off, tok_sorted, gate_sorted


def _rope_tables():
    t = np.arange(LAT_SEQ)
    pos = np.stack([t // GRID_W, t % GRID_W], axis=1).astype(np.float64)

    def table(half):
        inv = ROPE_BASE ** (-np.arange(half, dtype=np.float64) / half)
        lane = np.arange(LANES)
        axis = (lane // (2 * half)) % 2
        freq = inv[lane % half]
        ang = pos[:, axis] * freq[None, :]
        sign = np.where((lane % (2 * half)) < half, -1.0, 1.0)
        cos = np.concatenate([np.ones((ROW_TILE, LANES)), np.cos(ang)], axis=0)
        sin = np.concatenate([np.zeros((ROW_TILE, LANES)), np.sin(ang) * sign[None, :]], axis=0)
        return jnp.asarray(cos, F32), jnp.asarray(sin, F32)

    cos64, sin64 = table(16)
    cos32, sin32 = table(8)
    return cos64, sin64, cos32, sin32


def _head_perm_a():
    cols = []
    for g in range(4):
        for kvh in range(2):
            h = kvh * 4 + g
            cols.extend(range(h * 64, (h + 1) * 64))
    return np.asarray(cols)


def kernel(x_prompt, x_sample, cache_swa_k, cache_swa_v, cache_dif_k, cache_dif_v, cache_mla_ckv, cache_mla_kpe, c, c_ctx, w_mod, b_mod, norm_attn_pre, norm_attn_post, norm_ffn_pre, norm_ffn_post, w_in, swa_sink, dif_lq1, dif_lk1, dif_lq2, dif_lk2, dif_subln, mla_q_norm, mla_w_uq, mla_kv_norm, mla_w_ukv, w_branch_a, w_branch_b, w_branch_c, w_out, router_w, router_bias, moe_w_gate, moe_w_up, moe_w_down, shared_w_gate, shared_w_up, shared_w_down):
    x = jnp.concatenate([x_prompt.reshape(N_CTX, D_MODEL), x_sample.reshape(N_LAT, D_MODEL)], axis=0)
    cvec = jnp.concatenate([c_ctx[None, :], c, jnp.zeros((8 - 1 - N_LAT_BATCH, D_MODEL), F32)], axis=0)
    mod = _modulation(cvec, w_mod, b_mod)
    tables = _rope_tables()
    perm_a = _head_perm_a()
    uq_cols = np.concatenate([np.concatenate([np.arange(h * 96, h * 96 + 64) for h in range(8)]),
                              np.concatenate([np.arange(h * 96 + 64, h * 96 + 96) for h in range(8)])])
    ukv_cols = np.concatenate([np.concatenate([np.arange(h * 128, h * 128 + 64) for h in range(8)]),
                               np.concatenate([np.arange(h * 128 + 64, h * 128 + 128) for h in range(8)])])

    caches = [cache_swa_k.reshape(N_LAT_BATCH, DEPTH, PAST, 128),
              cache_swa_v.reshape(N_LAT_BATCH, DEPTH, PAST, 128),
              cache_dif_k.reshape(N_LAT_BATCH, DEPTH, PAST, 512),
              cache_dif_v.reshape(N_LAT_BATCH, DEPTH, PAST, 512),
              cache_mla_ckv,
              jnp.tile(cache_mla_kpe, (1, 1, 1, 4))]

    states = [[] for _ in range(6)]
    for l in range(DEPTH):
        lam_init = 0.8 - 0.6 * math.exp(-0.3 * l)
        w = w_in[l]
        kpe_w = w[:, 2688:2720]
        w1 = jnp.concatenate([w[:, 0:512][:, perm_a], w[:, 512:2688], kpe_w, kpe_w, kpe_w, kpe_w],
                             axis=1).astype(BF16)
        wgl = w[:, 2720:].astype(BF16)
        wuq = mla_w_uq[l][:, uq_cols].astype(BF16)
        wukv = mla_w_ukv[l][:, ukv_cols].astype(BF16)
        mod_l = mod[l].reshape(8, 1, 6 * D_MODEL)

        proj = _in_projection(x, mod_l, norm_attn_pre[l][None, :], w1, mla_q_norm[l][None, :], wuq,
                              mla_kv_norm[l][None, :], wukv, tables)
        _, ka, va, _, kd, vd, _, _, ckv, kpe, _, _ = proj
        for i, (s, width) in enumerate(zip((ka, va, kd, vd, ckv, kpe), (128, 128, 512, 512, 128, 32))):
            states[i].append(s[:N_CTX, :width].reshape(N_CTX_BATCH, CTX_SEQ, width))

        lam = (jnp.exp(jnp.sum(dif_lq1[l] * dif_lk1[l])) - jnp.exp(jnp.sum(dif_lq2[l] * dif_lk2[l]))
               + lam_init)
        scal = jnp.concatenate([swa_sink[l], lam[None]]).astype(F32)
        subln = dif_subln[l][None, :]
        oc = _attention_ctx(scal, subln, proj, lam_init)
        ol = _attention_lat(scal, subln, wukv, proj, caches, l, lam_init)
        oa, od, om = (jnp.concatenate([a, b], axis=0) for a, b in zip(oc, ol))

        xmid, h2, h2t, experts, gates = _post_attention(
            x, mod_l, oa, od, om, norm_attn_pre[l][None, :], wgl,
            w_branch_a[l][perm_a, :].astype(BF16), w_branch_b[l].astype(BF16),
            w_branch_c[l].astype(BF16), w_out[l].astype(BF16), norm_attn_post[l][None, :],
            norm_ffn_pre[l][None, :], router_w[l].T.astype(BF16), router_bias[l][:, None])
        off, tok_sorted, gate_sorted = _routing_lists(experts, gates)
        acc = _moe(off, tok_sorted, gate_sorted, h2t, moe_w_gate, moe_w_up, moe_w_down, l)
        x = _final(acc, h2, xmid, mod_l, shared_w_gate[l].astype(BF16), shared_w_up[l].astype(BF16),
                   shared_w_down[l].astype(BF16), norm_ffn_post[l][None, :])

    y_p = x[:N_CTX].reshape(N_CTX_BATCH, CTX_SEQ, D_MODEL)
    y_s = x[N_CTX:].reshape(N_LAT_BATCH, LAT_SEQ, D_MODEL)
    st = [jnp.stack(s, axis=1) for s in states]
    return (y_p, y_s,
            st[0].reshape(N_CTX_BATCH, DEPTH, CTX_SEQ, 2, 64),
            st[1].reshape(N_CTX_BATCH, DEPTH, CTX_SEQ, 2, 64),
            st[2].reshape(N_CTX_BATCH, DEPTH, CTX_SEQ, 4, 2, 64),
            st[3].reshape(N_CTX_BATCH, DEPTH, CTX_SEQ, 4, 128),
            st[4], st[5])
```

```python
import functools
import math

import numpy as np
import jax
import jax.numpy as jnp
from jax import lax
from jax.experimental import pallas as pl
from jax.experimental.pallas import tpu as pltpu

F32 = jnp.float32
BF16 = jnp.bfloat16

D_MODEL = 1024
N_CTX_BATCH, CTX_SEQ = 16, 256
N_LAT_BATCH, LAT_SEQ = 2, 1024
PAST = 256
N_CTX = N_CTX_BATCH * CTX_SEQ
N_LAT = N_LAT_BATCH * LAT_SEQ
N_TOK = N_CTX + N_LAT
DEPTH = 2
GRID_W = 64
WINDOW = 128
N_EXPERTS = 64
N_GROUPS = 8
TOPK_GROUPS = 4
TOP_K = 6
F_EXPERT = 256
ROUTED_SCALE = 2.5
ROPE_BASE = 10000.0
EPS = 1e-6
NEG = -1e30

LANES = 128
ROW_TILE = 256
Q_BLOCK = 128
VMEM_LIMIT = 56 * 1024 * 1024
ACC_ROWS = 8
MOE_SPARE_TOKENS = 8
MOE_ZERO_ROWS = 64
MOE_VMEM_LIMIT = 62 * 1024 * 1024
MOE_SUB = 128
MOE_RMW_GROUP = 8
TOK_KEY = 8192

C_QA, C_KA, C_VA, C_QD, C_KD, C_VD, C_QC, C_KVC, C_KPE, C_END = (
    0, 512, 640, 768, 1280, 1792, 2304, 2560, 2688, 2816)

CTX_TILES = N_CTX // ROW_TILE
assert ROW_TILE == CTX_SEQ and LAT_SEQ % ROW_TILE == 0


def _mod_row_of_tile(i):
    return jnp.where(i < CTX_TILES, 0, 1 + (i - CTX_TILES) // (LAT_SEQ // ROW_TILE))


def _rms(x, g):
    return x * lax.rsqrt(jnp.mean(x * x, axis=-1, keepdims=True) + EPS) * g


def _dot(a, b):
    return jnp.dot(a, b, preferred_element_type=F32)


def _dot_nt(a, b):
    return lax.dot_general(a, b, (((1,), (1,)), ((), ())), preferred_element_type=F32)


def _silu(x):
    return x * jax.nn.sigmoid(x)


def _layer_spec(arr, layer, grid_rank):
    zeros = (0,) * (arr.ndim - 1)
    return pl.BlockSpec((None,) + arr.shape[1:], lambda *_: (layer,) + zeros)


def _mod_kernel(c_ref, w_ref, b_ref, o_ref):
    c = c_ref[...]
    o_ref[0] = _dot(_silu(c).astype(BF16), w_ref[0].astype(BF16)) + b_ref[0]


def _modulation(cvec, w_mod, b_mod):
    tn = 1536
    n = w_mod.shape[-1]
    return pl.pallas_call(
        _mod_kernel,
        out_shape=jax.ShapeDtypeStruct((DEPTH, 8, n), F32),
        grid=(DEPTH, n // tn),
        in_specs=[
            pl.BlockSpec((8, D_MODEL), lambda l, j: (0, 0)),
            pl.BlockSpec((1, D_MODEL, tn), lambda l, j: (l, 0, j)),
            pl.BlockSpec((1, 1, tn), lambda l, j: (l, 0, j)),
        ],
        out_specs=pl.BlockSpec((1, 8, tn), lambda l, j: (l, 0, j)),
        compiler_params=pltpu.CompilerParams(vmem_limit_bytes=VMEM_LIMIT),
        name="modulation",
    )(cvec, w_mod, b_mod.reshape(DEPTH, 1, n))


def _mod_spec(layer):
    return pl.BlockSpec((None, 1, 1, 6 * D_MODEL), lambda i: (layer, _mod_row_of_tile(i), 0, 0))


def _rope128(x, cos, sin, half):
    lane = lax.broadcasted_iota(jnp.int32, x.shape, 1)
    first = (lane % (2 * half)) < half
    partner = jnp.where(first, pltpu.roll(x, LANES - half, 1), pltpu.roll(x, half, 1))
    return x * cos + partner * sin


def _rope_cols(x, cos, sin, half):
    chunks = [_rope128(x[:, c:c + LANES], cos, sin, half) for c in range(0, x.shape[1], LANES)]
    return chunks[0] if len(chunks) == 1 else jnp.concatenate(chunks, axis=1)


N_IN_INPUTS = 12


def _in_kernel(*refs):
    n_alias = len(refs) - N_IN_INPUTS - 17
    (x_ref, mod_ref, g_ref, w1_ref, qn_ref, wuq_ref, kvn_ref, wukv_ref,
     cos64_ref, sin64_ref, cos32_ref, sin32_ref) = refs[:N_IN_INPUTS]
    (qa_o, qd_o, qmn_o, qmr_o, kmn_o, vm_o, kpe4_o, ka_l, va_l, kd_l, vd_l,
     s_ka, s_va, s_kd, s_vd, s_ckv, s_kpe) = refs[N_IN_INPUTS + n_alias:]
    i = pl.program_id(0)
    x = x_ref[...]
    m = mod_ref[0]
    sh1, sc1 = m[:, 0:D_MODEL], m[:, D_MODEL:2 * D_MODEL]
    h = _rms(x, g_ref[...]) * (1.0 + sc1) + sh1
    z = _dot(h.astype(BF16), w1_ref[...])
    cos64, sin64 = cos64_ref[...], sin64_ref[...]
    cos32, sin32 = cos32_ref[...], sin32_ref[...]
    qa_o[...] = _rope_cols(z[:, C_QA:C_KA], cos64, sin64, 16).astype(BF16)
    qd_o[...] = _rope_cols(z[:, C_QD:C_KD], cos64, sin64, 16).astype(BF16)
    ka = _rope_cols(z[:, C_KA:C_VA], cos64, sin64, 16)
    va = z[:, C_VA:C_QD]
    kd = _rope_cols(z[:, C_KD:C_VD], cos64, sin64, 16)
    vd = z[:, C_VD:C_QC]
    kpe4 = _rope_cols(z[:, C_KPE:C_END], cos32, sin32, 8)
    kpe4_o[...] = kpe4.astype(BF16)
    qn = _rms(z[:, C_QC:C_KVC], qn_ref[...])
    qm = _dot(qn.astype(BF16), wuq_ref[...])
    qmn_o[...] = qm[:, 0:512].astype(BF16)
    qmr_o[...] = _rope_cols(qm[:, 512:768], cos32, sin32, 8).astype(BF16)
    ckv = _rms(z[:, C_KVC:C_KPE], kvn_ref[...])
    kv = _dot(ckv.astype(BF16), wukv_ref[...])
    kmn_o[...] = kv[:, 0:512].astype(BF16)
    vm_o[...] = kv[:, 512:1024].astype(BF16)

    @pl.when(i < CTX_TILES)
    def _():
        s_ka[...] = ka
        s_va[...] = va
        s_kd[...] = kd
        s_vd[...] = vd
        s_ckv[...] = ckv
        s_kpe[...] = kpe4[:, 0:32]

    @pl.when(i >= CTX_TILES)
    def _():
        ka_l[...] = ka.astype(BF16)
        va_l[...] = va.astype(BF16)
        kd_l[...] = kd.astype(BF16)
        vd_l[...] = vd.astype(BF16)


STATE_WIDTHS = (128, 128, 512, 512, 128, 32)


def _in_projection(x, mod, wts, tables, layer, prev_states):
    tm = ROW_TILE

    def row(i):
        return (i, 0)

    def table_row(i):
        return (jnp.where(i < CTX_TILES, 0, 1 + (i - CTX_TILES) % (LAT_SEQ // tm)), 0)

    def lat_row(i):
        return (jnp.maximum(i - CTX_TILES, 0), 0)

    def state_block(i):
        return (jnp.minimum(i, CTX_TILES - 1), layer, 0, 0)

    table_spec = pl.BlockSpec((tm, LANES), table_row)
    weights = [wts["g_pre"], wts["w1"], wts["qnorm"], wts["wuq"], wts["kvnorm"], wts["wukv"]]
    all_rows = [(512, BF16)] * 3 + [(256, BF16), (512, BF16), (512, BF16), (128, BF16)]
    lat_rows = [(128, BF16), (128, BF16), (512, BF16), (512, BF16)]
    aliased = [] if prev_states is None else list(prev_states)
    n_in = N_IN_INPUTS + len(aliased)
    n_plain_out = len(all_rows) + len(lat_rows)
    return pl.pallas_call(
        _in_kernel,
        out_shape=[jax.ShapeDtypeStruct((N_TOK, w), dt) for w, dt in all_rows]
                  + [jax.ShapeDtypeStruct((N_LAT, w), dt) for w, dt in lat_rows]
                  + [jax.ShapeDtypeStruct((N_CTX_BATCH, DEPTH, CTX_SEQ, w), F32) for w in STATE_WIDTHS],
        grid=(N_TOK // tm,),
        in_specs=[pl.BlockSpec((tm, D_MODEL), row), _mod_spec(layer)]
                 + [_layer_spec(w, layer, 1) for w in weights]
                 + [table_spec] * 4
                 + [pl.BlockSpec(memory_space=pl.ANY)] * len(aliased),
        out_specs=[pl.BlockSpec((tm, w), row) for w, _ in all_rows]
                  + [pl.BlockSpec((tm, w), lat_row) for w, _ in lat_rows]
                  + [pl.BlockSpec((None, None, CTX_SEQ, w), state_block) for w in STATE_WIDTHS],
        input_output_aliases={N_IN_INPUTS + k: n_plain_out + k for k in range(len(aliased))},
        compiler_params=pltpu.CompilerParams(
            dimension_semantics=("arbitrary",), vmem_limit_bytes=VMEM_LIMIT),
        name="in_projection",
    )(x, mod, *weights, *tables, *aliased)


def _attend(q, parts, scale, sink=None):
    scores = []
    for k, _, mask in parts:
        s = _dot_nt(q, k) * scale
        if mask is not None:
            s = jnp.where(mask, s, NEG)
        scores.append(s)
    m = functools.reduce(jnp.maximum, [jnp.max(s, axis=-1, keepdims=True) for s in scores])
    if sink is not None:
        m = jnp.maximum(m, sink)
    den = None
    out = None
    for s, (_, v, _) in zip(scores, parts):
        e = jnp.exp(s - m)
        d = jnp.sum(e, axis=-1, keepdims=True)
        o = _dot(e.astype(BF16), v)
        den = d if den is None else den + d
        out = o if out is None else out + o
    if sink is not None:
        den = den + jnp.exp(sink - m)
    return out / den


def _three_mixers(scal_ref, subln_ref, q_blocks, kv_parts, oa_o, od_o, om_o, lam_init):
    qa, qd, qmn, qmr = q_blocks
    rows = qa.shape[0]
    lane = lax.broadcasted_iota(jnp.int32, (rows, LANES), 1)
    low = lane < 64
    zero = jnp.zeros((rows, LANES), BF16)

    for g in range(4):
        q128 = qa[:, g * LANES:(g + 1) * LANES]
        halves = []
        for kvh in range(2):
            qh = jnp.where(low if kvh == 0 else ~low, q128, zero)
            parts = [(p["ka"], p["va"], p.get("mask_a")) for p in kv_parts]
            halves.append(_attend(qh, parts, 0.125, sink=scal_ref[kvh * 4 + g]))
        oa_o[:, g * LANES:(g + 1) * LANES] = jnp.where(low, halves[0], halves[1]).astype(BF16)

    lam = scal_ref[8]
    subln = subln_ref[...]
    for h in range(4):
        sl = slice(h * LANES, (h + 1) * LANES)
        q128 = qd[:, sl]
        parts = [(p["kd"][:, sl], p["vd"][:, sl], None) for p in kv_parts]
        o1 = _attend(jnp.where(low, q128, zero), parts, 0.125)
        o2 = _attend(jnp.where(low, zero, q128), parts, 0.125)
        od = o1 - lam * o2
        od_o[:, sl] = (_rms(od, subln) * (1.0 - lam_init)).astype(BF16)

    scale_c = 1.0 / math.sqrt(96.0)
    for i in range(4):
        sl = slice(i * LANES, (i + 1) * LANES)
        keys = [jnp.concatenate([p["kmn"][:, sl], p["kpe"]], axis=1) for p in kv_parts]
        halves = []
        for sub in range(2):
            h = 2 * i + sub
            qn = jnp.where(low if sub == 0 else ~low, qmn[:, sl], zero)
            qr128 = qmr[:, (h // 4) * LANES:(h // 4 + 1) * LANES]
            qr = jnp.where((lane // 32) == (h % 4), qr128, zero)
            q256 = jnp.concatenate([qn, qr], axis=1)
            parts = [(k, p["vm"][:, sl], None) for k, p in zip(keys, kv_parts)]
            halves.append(_attend(q256, parts, scale_c))
        om_o[:, sl] = jnp.where(low, halves[0], halves[1]).astype(BF16)


def _attn_ctx_kernel(scal_ref, subln_ref, qa_ref, qd_ref, qmn_ref, qmr_ref,
                     ka_ref, va_ref, kd_ref, vd_ref, kmn_ref, vm_ref, kpe_ref,
                     oa_o, od_o, om_o, *, lam_init):
    part = dict(ka=ka_ref[...].astype(BF16), va=va_ref[...].astype(BF16),
                kd=kd_ref[...].astype(BF16), vd=vd_ref[...].astype(BF16),
                kmn=kmn_ref[...], vm=vm_ref[...], kpe=kpe_ref[...])
    _three_mixers(scal_ref, subln_ref, (qa_ref[...], qd_ref[...], qmn_ref[...], qmr_ref[...]),
                  [part], oa_o, od_o, om_o, lam_init)


def _attn_lat_kernel(scal_ref, subln_ref, wukv_ref, qa_ref, qd_ref, qmn_ref, qmr_ref,
                     ka_ref, va_ref, kd_ref, vd_ref, kmn_ref, vm_ref, kpe_ref,
                     cka_ref, cva_ref, ckd_ref, cvd_ref, cckv_ref, ckpe_ref,
                     oa_in, od_in, om_in, oa_o, od_o, om_o, *, lam_init):
    del oa_in, od_in, om_in
    n = pl.program_id(1)
    kv_c = _dot(cckv_ref[...].astype(BF16), wukv_ref[...])
    cached = dict(ka=cka_ref[...].astype(BF16), va=cva_ref[...].astype(BF16),
                  kd=ckd_ref[...].astype(BF16), vd=cvd_ref[...].astype(BF16),
                  kmn=kv_c[:, 0:512].astype(BF16), vm=kv_c[:, 512:1024].astype(BF16),
                  kpe=ckpe_ref[...].astype(BF16))
    span = 3 * Q_BLOCK
    start = pl.multiple_of(jnp.clip((n - 1) * Q_BLOCK, 0, LAT_SEQ - span), Q_BLOCK)
    qpos = n * Q_BLOCK + lax.broadcasted_iota(jnp.int32, (Q_BLOCK, span), 0)
    kpos = start + lax.broadcasted_iota(jnp.int32, (Q_BLOCK, span), 1)
    mask_a = jnp.abs(qpos - kpos) <= WINDOW
    new = dict(ka=ka_ref[pl.ds(start, span), :], va=va_ref[pl.ds(start, span), :], mask_a=mask_a,
               kd=kd_ref[...], vd=vd_ref[...], kmn=kmn_ref[...], vm=vm_ref[...], kpe=kpe_ref[...])
    _three_mixers(scal_ref, subln_ref, (qa_ref[...], qd_ref[...], qmn_ref[...], qmr_ref[...]),
                  [cached, new], oa_o, od_o, om_o, lam_init)


_SMEM_SPEC = pl.BlockSpec(memory_space=pltpu.SMEM)


def _attention_ctx(scal, subln, proj, states, layer, lam_init):
    qa, qd, qmn, qmr, kmn, vm, kpe4 = proj[:7]
    s_ka, s_va, s_kd, s_vd = states[:4]
    t = CTX_SEQ

    def blk(a):
        return pl.BlockSpec((t, a.shape[1]), lambda b: (b, 0))

    def sblk(a):
        return pl.BlockSpec((None, None, t, a.shape[-1]), lambda b: (b, layer, 0, 0))

    return pl.pallas_call(
        functools.partial(_attn_ctx_kernel, lam_init=lam_init),
        out_shape=[jax.ShapeDtypeStruct((N_TOK, 512), BF16)] * 3,
        grid=(N_CTX_BATCH,),
        in_specs=[_SMEM_SPEC, _layer_spec(subln, layer, 1)]
                 + [blk(a) for a in (qa, qd, qmn, qmr)]
                 + [sblk(a) for a in (s_ka, s_va, s_kd, s_vd)]
                 + [blk(a) for a in (kmn, vm, kpe4)],
        out_specs=[pl.BlockSpec((t, 512), lambda b: (b, 0))] * 3,
        compiler_params=pltpu.CompilerParams(vmem_limit_bytes=VMEM_LIMIT),
        name="attention_ctx",
    )(scal, subln, qa, qd, qmn, qmr, s_ka, s_va, s_kd, s_vd, kmn, vm, kpe4)


def _attention_lat(scal, subln, wukv, proj, caches, ctx_out, layer, lam_init):
    qa, qd, qmn, qmr, kmn, vm, kpe4, ka_l, va_l, kd_l, vd_l = proj[:11]
    nq = LAT_SEQ // Q_BLOCK
    q_off = N_CTX // Q_BLOCK
    kv_off = N_CTX // LAT_SEQ

    def qblk(a):
        return pl.BlockSpec((Q_BLOCK, a.shape[1]), lambda b, n: (q_off + b * nq + n, 0))

    def kvblk(a):
        return pl.BlockSpec((LAT_SEQ, a.shape[1]), lambda b, n: (b, 0))

    def kvblk_all(a):
        return pl.BlockSpec((LAT_SEQ, a.shape[1]), lambda b, n: (kv_off + b, 0))

    def cblk(a):
        return pl.BlockSpec((None, None, PAST, a.shape[-1]), lambda b, n: (b, layer, 0, 0))

    qs = [qa, qd, qmn, qmr]
    n_in = 3 + 4 + 7 + 6
    out_spec = pl.BlockSpec((Q_BLOCK, 512), lambda b, n: (q_off + b * nq + n, 0))
    return pl.pallas_call(
        functools.partial(_attn_lat_kernel, lam_init=lam_init),
        out_shape=[jax.ShapeDtypeStruct((N_TOK, 512), BF16)] * 3,
        grid=(N_LAT_BATCH, nq),
        in_specs=[_SMEM_SPEC, _layer_spec(subln, layer, 2), _layer_spec(wukv, layer, 2)]
                 + [qblk(a) for a in qs]
                 + [kvblk(a) for a in (ka_l, va_l, kd_l, vd_l)]
                 + [kvblk_all(a) for a in (kmn, vm, kpe4)]
                 + [cblk(a) for a in caches]
                 + [pl.BlockSpec(memory_space=pl.ANY)] * 3,
        out_specs=[out_spec] * 3,
        input_output_aliases={n_in + k: k for k in range(3)},
        compiler_params=pltpu.CompilerParams(vmem_limit_bytes=VMEM_LIMIT),
        name="attention_lat",
    )(scal, subln, wukv, *qs, ka_l, va_l, kd_l, vd_l, kmn, vm, kpe4, *caches, *ctx_out)


def _route(scores, biased):
    per = N_EXPERTS // N_GROUPS
    tm = scores.shape[1]
    sub = lax.broadcasted_iota(jnp.int32, (per, tm), 0).astype(F32)
    groups = [biased[g * per:(g + 1) * per, :] for g in range(N_GROUPS)]
    gscore = []
    for v in groups:
        m1 = jnp.max(v, axis=0, keepdims=True)
        first = jnp.min(jnp.where(v == m1, sub, float(per)), axis=0, keepdims=True)
        m2 = jnp.max(jnp.where(sub == first, -jnp.inf, v), axis=0, keepdims=True)
        gscore.append(m1 + m2)
    vals = []
    for g in range(N_GROUPS):
        rank = jnp.zeros((1, tm), F32)
        for o in range(N_GROUPS):
            if o == g:
                continue
            ahead = (gscore[o] >= gscore[g]) if o < g else (gscore[o] > gscore[g])
            rank = rank + jnp.where(ahead, 1.0, 0.0)
        vals.append(jnp.where(rank < TOPK_GROUPS, groups[g], NEG))
    idx = [sub + float(g * per) for g in range(N_GROUPS)]
    picks, weights = [], []
    for _ in range(TOP_K):
        best = functools.reduce(jnp.maximum, [jnp.max(v, axis=0, keepdims=True) for v in vals])
        cand = functools.reduce(jnp.minimum, [
            jnp.min(jnp.where(v == best, i, float(N_EXPERTS)), axis=0, keepdims=True)
            for v, i in zip(vals, idx)])
        wsel = jnp.zeros((1, tm), F32)
        for g in range(N_GROUPS):
            hit = idx[g] == cand
            wsel = wsel + jnp.sum(jnp.where(hit, scores[g * per:(g + 1) * per, :], 0.0),
                                  axis=0, keepdims=True)
            vals[g] = jnp.where(hit, -jnp.inf, vals[g])
        picks.append(cand)
        weights.append(wsel)
    total = functools.reduce(lambda a, b: a + b, weights)
    pad = [jnp.zeros((1, tm), F32)] * (8 - TOP_K)
    experts = jnp.concatenate(picks + pad, axis=0).astype(jnp.int32)
    gates = jnp.concatenate([x / total * ROUTED_SCALE for x in weights] + pad, axis=0)
    return experts, gates


def _post_kernel(x_ref, mod_ref, oa_ref, od_ref, om_ref, gpre_ref, wgl_ref, wa_ref, wb_ref, wc_ref,
                 wo_ref, gpost_ref, gffn_ref, rwt_ref, rb_ref, xmid_o, h2_o, h2t_o, experts_o, gates_o):
    x = x_ref[...]
    m = mod_ref[0]
    d = D_MODEL
    sh1, sc1, g1, sh2, sc2 = (m[:, 0:d], m[:, d:2 * d], m[:, 2 * d:3 * d], m[:, 3 * d:4 * d],
                              m[:, 4 * d:5 * d])
    h = _rms(x, gpre_ref[...]) * (1.0 + sc1) + sh1
    gate = jax.nn.sigmoid(_dot(h.astype(BF16), wgl_ref[...]))
    merged = (gate[:, 0:d] * _dot(oa_ref[...], wa_ref[...])
              + gate[:, d:2 * d] * _dot(od_ref[...], wb_ref[...])
              + gate[:, 2 * d:3 * d] * _dot(om_ref[...], wc_ref[...]))
    a = _dot(merged.astype(BF16), wo_ref[...])
    xm = x + g1 * _rms(a, gpost_ref[...])
    xmid_o[...] = xm
    h2f = _rms(xm, gffn_ref[...]) * (1.0 + sc2) + sh2
    h2 = h2f.astype(BF16)
    h2_o[...] = h2
    for c in range(ACC_ROWS):
        h2t_o[pl.ds(c, x.shape[0], stride=ACC_ROWS), :] = h2f[:, c * LANES:(c + 1) * LANES]
    scores = jax.nn.sigmoid(_dot_nt(rwt_ref[...], h2))
    experts, gates = _route(scores, scores + rb_ref[...])
    experts_o[...] = experts
    gates_o[...] = gates


def _post_attention(x, mod, oa, od, om, wts, layer):
    tm = ROW_TILE

    def row(i):
        return (i, 0)

    weights = [wts[k] for k in ("g_pre", "wgl", "wa", "wb", "wc", "wo", "g_post", "g_ffn", "rwt", "rb")]
    return pl.pallas_call(
        _post_kernel,
        out_shape=[jax.ShapeDtypeStruct((N_TOK, D_MODEL), F32),
                   jax.ShapeDtypeStruct((N_TOK, D_MODEL), BF16),
                   jax.ShapeDtypeStruct((N_TOK * ACC_ROWS, LANES), F32),
                   jax.ShapeDtypeStruct((8, N_TOK), jnp.int32),
                   jax.ShapeDtypeStruct((8, N_TOK), F32)],
        grid=(N_TOK // tm,),
        in_specs=[pl.BlockSpec((tm, D_MODEL), row), _mod_spec(layer),
                  pl.BlockSpec((tm, 512), row), pl.BlockSpec((tm, 512), row),
                  pl.BlockSpec((tm, 512), row)]
                 + [_layer_spec(w, layer, 1) for w in weights],
        out_specs=[pl.BlockSpec((tm, D_MODEL), row), pl.BlockSpec((tm, D_MODEL), row),
                   pl.BlockSpec((tm * ACC_ROWS, LANES), row),
                   pl.BlockSpec((8, tm), lambda i: (0, i)), pl.BlockSpec((8, tm), lambda i: (0, i))],
        compiler_params=pltpu.CompilerParams(vmem_limit_bytes=VMEM_LIMIT),
        name="post_attention",
    )(x, mod, oa, od, om, *weights)


def _ffn(x, wg, wu):
    return _silu(_dot(x, wg)) * _dot(x, wu)


def _moe_kernel(start_ref, count_ref, src_ref, dst_ref, gate_ref, xs_ref, wg_ref, wu_ref, wd_ref,
                acc_ref, xbuf, ybuf, wgb, wub, wdb):
    e = pl.program_id(0)

    @pl.when(e == 0)
    def _():
        def clear(i, carry):
            r0 = pl.multiple_of(i * MOE_ZERO_ROWS, MOE_ZERO_ROWS)
            acc_ref[pl.ds(r0, MOE_ZERO_ROWS), :] = jnp.zeros((MOE_ZERO_ROWS, LANES), F32)
            return carry

        lax.fori_loop(0, acc_ref.shape[0] // MOE_ZERO_ROWS, clear, 0)

    wgb[...] = wg_ref[0].astype(BF16)
    wub[...] = wu_ref[0].astype(BF16)
    wdb[...] = wd_ref[0].astype(BF16)
    start = start_ref[e]
    count = count_ref[e]

    def sub_tile(s, carry):
        base = start + s * MOE_SUB
        for mi in range(MOE_SUB):
            src = pl.multiple_of(src_ref[base + mi], ACC_ROWS)
            xbuf[mi * ACC_ROWS:(mi + 1) * ACC_ROWS, :] = xs_ref[pl.ds(src, ACC_ROWS), :]
        x = jnp.concatenate([xbuf[pl.ds(c, MOE_SUB, stride=ACC_ROWS), :].astype(BF16)
                             for c in range(ACC_ROWS)], axis=1)
        hid = _ffn(x, wgb[...], wub[...])
        y = _dot(hid.astype(BF16), wdb[...])
        for c in range(ACC_ROWS):
            ybuf[pl.ds(c, MOE_SUB, stride=ACC_ROWS), :] = y[:, c * LANES:(c + 1) * LANES]
        for g0 in range(0, MOE_SUB, MOE_RMW_GROUP):
            updates = []
            for mi in range(g0, g0 + MOE_RMW_GROUP):
                dst = pl.multiple_of(dst_ref[base + mi], ACC_ROWS)
                yv = ybuf[mi * ACC_ROWS:(mi + 1) * ACC_ROWS, :]
                updates.append((dst, acc_ref[pl.ds(dst, ACC_ROWS), :] + gate_ref[base + mi] * yv))
            for dst, val in updates:
                acc_ref[pl.ds(dst, ACC_ROWS), :] = val
        return carry

    lax.fori_loop(0, (count + MOE_SUB - 1) // MOE_SUB, sub_tile, 0)


def _moe(lists, h2t, wg, wu, wd, layer):
    def expert(e, *_):
        return (layer, e, 0, 0)

    def whole(e, *_):
        return (0, 0)

    acc_rows = (N_TOK + MOE_SPARE_TOKENS) * ACC_ROWS
    assert acc_rows % MOE_ZERO_ROWS == 0
    return pl.pallas_call(
        _moe_kernel,
        out_shape=jax.ShapeDtypeStruct((acc_rows, LANES), F32),
        grid_spec=pltpu.PrefetchScalarGridSpec(
            num_scalar_prefetch=len(lists),
            grid=(N_EXPERTS,),
            in_specs=[pl.BlockSpec(h2t.shape, whole, pipeline_mode=pl.Buffered(1)),
                      pl.BlockSpec((None, 1, D_MODEL, F_EXPERT), expert),
                      pl.BlockSpec((None, 1, D_MODEL, F_EXPERT), expert),
                      pl.BlockSpec((None, 1, F_EXPERT, D_MODEL), expert)],
            out_specs=pl.BlockSpec((acc_rows, LANES), whole, pipeline_mode=pl.Buffered(1)),
            scratch_shapes=[pltpu.VMEM((MOE_SUB * ACC_ROWS, LANES), F32),
                            pltpu.VMEM((MOE_SUB * ACC_ROWS, LANES), F32),
                            pltpu.VMEM((D_MODEL, F_EXPERT), BF16),
                            pltpu.VMEM((D_MODEL, F_EXPERT), BF16),
                            pltpu.VMEM((F_EXPERT, D_MODEL), BF16)]),
        compiler_params=pltpu.CompilerParams(
            dimension_semantics=("arbitrary",), vmem_limit_bytes=MOE_VMEM_LIMIT),
        name="moe",
    )(*lists, h2t, wg, wu, wd)


def _final_kernel(acc_ref, h2_ref, xmid_ref, mod_ref, swg_ref, swu_ref, swd_ref, gpost_ref, out_ref):
    tm = h2_ref.shape[0]
    routed = jnp.concatenate(
        [acc_ref[pl.ds(c, tm, stride=ACC_ROWS), :] for c in range(ACC_ROWS)], axis=1)
    hid = _ffn(h2_ref[...], swg_ref[...], swu_ref[...])
    shared = _dot(hid.astype(BF16), swd_ref[...])
    g2 = mod_ref[0][:, 5 * D_MODEL:6 * D_MODEL]
    out_ref[...] = xmid_ref[...] + g2 * _rms(routed + shared, gpost_ref[...])


def _final(acc, h2, xmid, mod, wts, layer):
    tm = ROW_TILE

    def row(i):
        return (i, 0)

    weights = [wts[k] for k in ("swg", "swu", "swd", "g_ffn_post")]
    return pl.pallas_call(
        _final_kernel,
        out_shape=jax.ShapeDtypeStruct((N_TOK, D_MODEL), F32),
        grid=(N_TOK // tm,),
        in_specs=[pl.BlockSpec((tm * ACC_ROWS, LANES), row),
                  pl.BlockSpec((tm, D_MODEL), row),
                  pl.BlockSpec((tm, D_MODEL), row),
                  _mod_spec(layer)]
                 + [_layer_spec(w, layer, 1) for w in weights],
        out_specs=pl.BlockSpec((tm, D_MODEL), row),
        compiler_params=pltpu.CompilerParams(vmem_limit_bytes=VMEM_LIMIT),
        name="ffn_final",
    )(acc, h2, xmid, mod, *weights)


def _routing_lists(experts, gates):
    e = experts[:TOP_K].reshape(-1)
    tok = jnp.tile(jnp.arange(N_TOK, dtype=jnp.int32), TOP_K)
    pad_e = jnp.repeat(jnp.arange(N_EXPERTS, dtype=jnp.int32), MOE_SUB)
    pad_tok = N_TOK + jnp.tile(jnp.arange(MOE_SUB, dtype=jnp.int32), N_EXPERTS)
    keys = jnp.concatenate([e * TOK_KEY + tok, pad_e * TOK_KEY + pad_tok])
    vals = jnp.concatenate([gates[:TOP_K].reshape(-1), jnp.zeros((N_EXPERTS * MOE_SUB,), F32)])
    keys, gate_sorted = lax.sort((keys, vals), num_keys=1)
    tok_sorted = keys & (TOK_KEY - 1)
    src = jnp.minimum(tok_sorted, N_TOK - 1) * ACC_ROWS
    dst = jnp.minimum(tok_sorted, N_TOK) * ACC_ROWS
    counts = jnp.sum((e[None, :] == jnp.arange(N_EXPERTS, dtype=jnp.int32)[:, None]).astype(jnp.int32),
                     axis=1)
    start = jnp.cumsum(counts) - counts + MOE_SUB * jnp.arange(N_EXPERTS, dtype=jnp.int32)
    return start.astype(jnp.int32), counts, src, dst, gate_sorted


def _rope_tables():
    t = np.arange(LAT_SEQ)
    pos = np.stack([t // GRID_W, t % GRID_W], axis=1).astype(np.float64)

    def table(half):
        inv = ROPE_BASE ** (-np.arange(half, dtype=np.float64) / half)
        lane = np.arange(LANES)
        axis = (lane // (2 * half)) % 2
        freq = inv[lane % half]
        ang = pos[:, axis] * freq[None, :]
        sign = np.where((lane % (2 * half)) < half, -1.0, 1.0)
        cos = np.concatenate([np.ones((ROW_TILE, LANES)), np.cos(ang)], axis=0)
        sin = np.concatenate([np.zeros((ROW_TILE, LANES)), np.sin(ang) * sign[None, :]], axis=0)
        return jnp.asarray(cos, F32), jnp.asarray(sin, F32)

    cos64, sin64 = table(16)
    cos32, sin32 = table(8)
    return cos64, sin64, cos32, sin32


def _prepare_weights(norm_attn_pre, norm_attn_post, norm_ffn_pre, norm_ffn_post, w_in, dif_subln,
                     mla_q_norm, mla_w_uq, mla_kv_norm, mla_w_ukv, w_branch_a, w_branch_b,
                     w_branch_c, w_out, router_w, router_bias, shared_w_gate, shared_w_up,
                     shared_w_down):
    dp = DEPTH
    qa = w_in[:, :, 0:512].reshape(dp, D_MODEL, 2, 4, 64).transpose(0, 1, 3, 2, 4).reshape(dp, D_MODEL, 512)
    kpe = w_in[:, :, 2688:2720]
    w1 = jnp.concatenate([qa, w_in[:, :, 512:2688], kpe, kpe, kpe, kpe], axis=2).astype(BF16)
    uq = mla_w_uq.reshape(dp, 256, 8, 96)
    wuq = jnp.concatenate([uq[..., :64].reshape(dp, 256, 512), uq[..., 64:].reshape(dp, 256, 256)],
                          axis=2).astype(BF16)
    ukv = mla_w_ukv.reshape(dp, 128, 8, 128)
    wukv = jnp.concatenate([ukv[..., :64].reshape(dp, 128, 512), ukv[..., 64:].reshape(dp, 128, 512)],
                           axis=2).astype(BF16)
    wa = w_branch_a.reshape(dp, 2, 4, 64, D_MODEL).transpose(0, 2, 1, 3, 4).reshape(dp, 512, D_MODEL)
    return dict(
        g_pre=norm_attn_pre[:, None, :], g_post=norm_attn_post[:, None, :],
        g_ffn=norm_ffn_pre[:, None, :], g_ffn_post=norm_ffn_post[:, None, :],
        w1=w1, wgl=w_in[:, :, 2720:].astype(BF16),
        qnorm=mla_q_norm[:, None, :], wuq=wuq, kvnorm=mla_kv_norm[:, None, :], wukv=wukv,
        subln=dif_subln[:, None, :],
        wa=wa.astype(BF16), wb=w_branch_b.astype(BF16), wc=w_branch_c.astype(BF16),
        wo=w_out.astype(BF16), rwt=router_w.transpose(0, 2, 1).astype(BF16),
        rb=router_bias[:, :, None],
        swg=shared_w_gate.astype(BF16), swu=shared_w_up.astype(BF16), swd=shared_w_down.astype(BF16))


def kernel(x_prompt, x_sample, cache_swa_k, cache_swa_v, cache_dif_k, cache_dif_v, cache_mla_ckv, cache_mla_kpe, c, c_ctx, w_mod, b_mod, norm_attn_pre, norm_attn_post, norm_ffn_pre, norm_ffn_post, w_in, swa_sink, dif_lq1, dif_lk1, dif_lq2, dif_lk2, dif_subln, mla_q_norm, mla_w_uq, mla_kv_norm, mla_w_ukv, w_branch_a, w_branch_b, w_branch_c, w_out, router_w, router_bias, moe_w_gate, moe_w_up, moe_w_down, shared_w_gate, shared_w_up, shared_w_down):
    x = jnp.concatenate([x_prompt.reshape(N_CTX, D_MODEL), x_sample.reshape(N_LAT, D_MODEL)], axis=0)
    cvec = jnp.concatenate([c_ctx[None, :], c, jnp.zeros((8 - 1 - N_LAT_BATCH, D_MODEL), F32)], axis=0)
    mod = _modulation(cvec, w_mod, b_mod).reshape(DEPTH, 8, 1, 6 * D_MODEL)
    tables = _rope_tables()
    wts = _prepare_weights(norm_attn_pre, norm_attn_post, norm_ffn_pre, norm_ffn_post, w_in,
                           dif_subln, mla_q_norm, mla_w_uq, mla_kv_norm, mla_w_ukv, w_branch_a,
                           w_branch_b, w_branch_c, w_out, router_w, router_bias, shared_w_gate,
                           shared_w_up, shared_w_down)
    caches = [cache_swa_k.reshape(N_LAT_BATCH, DEPTH, PAST, 128),
              cache_swa_v.reshape(N_LAT_BATCH, DEPTH, PAST, 128),
              cache_dif_k.reshape(N_LAT_BATCH, DEPTH, PAST, 512),
              cache_dif_v.reshape(N_LAT_BATCH, DEPTH, PAST, 512),
              cache_mla_ckv,
              jnp.tile(cache_mla_kpe, (1, 1, 1, 4))]
    lam_init = [0.8 - 0.6 * math.exp(-0.3 * l) for l in range(DEPTH)]
    lam = (jnp.exp(jnp.sum(dif_lq1 * dif_lk1, axis=1)) - jnp.exp(jnp.sum(dif_lq2 * dif_lk2, axis=1))
           + jnp.asarray(lam_init, F32))
    scal = jnp.concatenate([swa_sink, lam[:, None]], axis=1).astype(F32)

    states = None
    for l in range(DEPTH):
        proj = _in_projection(x, mod, wts, tables, l, states)
        states = proj[11:]
        ctx_out = _attention_ctx(scal[l], wts["subln"], proj, states, l, lam_init[l])
        oa, od, om = _attention_lat(scal[l], wts["subln"], wts["wukv"], proj, caches, ctx_out, l,
                                    lam_init[l])
        xmid, h2, h2t, experts, gates = _post_attention(x, mod, oa, od, om, wts, l)
        lists = _routing_lists(experts, gates)
        acc = _moe(lists, h2t, moe_w_gate, moe_w_up, moe_w_down, l)
        x = _final(acc, h2, xmid, mod, wts, l)

    y_p = x[:N_CTX].reshape(N_CTX_BATCH, CTX_SEQ, D_MODEL)
    y_s = x[N_CTX:].reshape(N_LAT_BATCH, LAT_SEQ, D_MODEL)
    s_ka, s_va, s_kd, s_vd, s_ckv, s_kpe = states
    return (y_p, y_s,
            s_ka.reshape(N_CTX_BATCH, DEPTH, CTX_SEQ, 2, 64),
            s_va.reshape(N_CTX_BATCH, DEPTH, CTX_SEQ, 2, 64),
            s_kd.reshape(N_CTX_BATCH, DEPTH, CTX_SEQ, 4, 2, 64),
            s_vd.reshape(N_CTX_BATCH, DEPTH, CTX_SEQ, 4, 128),
            s_ckv, s_kpe)
```

```python
import functools
import math

import numpy as np
import jax
import jax.numpy as jnp
from jax import lax
from jax.experimental import pallas as pl
from jax.experimental.pallas import tpu as pltpu

F32 = jnp.float32
BF16 = jnp.bfloat16

D_MODEL = 1024
N_CTX_BATCH, CTX_SEQ = 16, 256
N_LAT_BATCH, LAT_SEQ = 2, 1024
PAST = 256
N_CTX = N_CTX_BATCH * CTX_SEQ
N_LAT = N_LAT_BATCH * LAT_SEQ
N_TOK = N_CTX + N_LAT
DEPTH = 2
GRID_W = 64
WINDOW = 128
N_EXPERTS = 64
N_GROUPS = 8
TOPK_GROUPS = 4
TOP_K = 6
F_EXPERT = 256
ROUTED_SCALE = 2.5
ROPE_BASE = 10000.0
EPS = 1e-6
NEG = -1e30

LANES = 128
ROW_TILE = 256
Q_BLOCK = 128
VMEM_LIMIT = 56 * 1024 * 1024
ACC_ROWS = 8
MOE_SPARE_TOKENS = 8
MOE_ZERO_ROWS = 64
MOE_VMEM_LIMIT = 62 * 1024 * 1024
MOE_SUB = 128
MOE_RMW_GROUP = 8
MOE_PAD = 2 * MOE_SUB
TOK_KEY = 8192
MOE_LIST_PAD_BASE = N_TOK * TOP_K + N_EXPERTS * MOE_PAD

C_QA, C_KA, C_VA, C_QD, C_KD, C_VD, C_QC, C_KVC, C_KPE, C_END = (
    0, 512, 640, 768, 1280, 1792, 2304, 2560, 2688, 2816)

CTX_TILES = N_CTX // ROW_TILE
assert ROW_TILE == CTX_SEQ and LAT_SEQ % ROW_TILE == 0


def _mod_row_of_tile(i):
    return jnp.where(i < CTX_TILES, 0, 1 + (i - CTX_TILES) // (LAT_SEQ // ROW_TILE))


def _rms(x, g):
    return x * lax.rsqrt(jnp.mean(x * x, axis=-1, keepdims=True) + EPS) * g


def _dot(a, b):
    return jnp.dot(a, b, preferred_element_type=F32)


def _dot_nt(a, b):
    return lax.dot_general(a, b, (((1,), (1,)), ((), ())), preferred_element_type=F32)


def _silu(x):
    return x * jax.nn.sigmoid(x)


def _layer_spec(arr, layer, grid_rank):
    zeros = (0,) * (arr.ndim - 1)
    return pl.BlockSpec((None,) + arr.shape[1:], lambda *_: (layer,) + zeros)


def _mod_kernel(c_ref, w_ref, b_ref, o_ref):
    c = c_ref[...]
    o_ref[0] = _dot(_silu(c).astype(BF16), w_ref[0].astype(BF16)) + b_ref[0]


def _modulation(cvec, w_mod, b_mod):
    tn = 1536
    n = w_mod.shape[-1]
    return pl.pallas_call(
        _mod_kernel,
        out_shape=jax.ShapeDtypeStruct((DEPTH, 8, n), F32),
        grid=(DEPTH, n // tn),
        in_specs=[
            pl.BlockSpec((8, D_MODEL), lambda l, j: (0, 0)),
            pl.BlockSpec((1, D_MODEL, tn), lambda l, j: (l, 0, j)),
            pl.BlockSpec((1, 1, tn), lambda l, j: (l, 0, j)),
        ],
        out_specs=pl.BlockSpec((1, 8, tn), lambda l, j: (l, 0, j)),
        compiler_params=pltpu.CompilerParams(vmem_limit_bytes=VMEM_LIMIT),
        name="modulation",
    )(cvec, w_mod, b_mod.reshape(DEPTH, 1, n))


def _x_specs(x):
    if isinstance(x, tuple):
        return [pl.BlockSpec((ROW_TILE, D_MODEL), lambda i: (jnp.minimum(i, CTX_TILES - 1), 0)),
                pl.BlockSpec((ROW_TILE, D_MODEL), lambda i: (jnp.maximum(i - CTX_TILES, 0), 0))]
    return [pl.BlockSpec((ROW_TILE, D_MODEL), lambda i: (i, 0))]


def _load_x(x_refs):
    if len(x_refs) == 1:
        return x_refs[0][...]
    return jnp.where(pl.program_id(0) < CTX_TILES, x_refs[0][...], x_refs[1][...])


def _mod_spec(layer):
    return pl.BlockSpec((None, 1, 1, 6 * D_MODEL), lambda i: (layer, _mod_row_of_tile(i), 0, 0))


def _rope128(x, cos, sin, half):
    lane = lax.broadcasted_iota(jnp.int32, x.shape, 1)
    first = (lane % (2 * half)) < half
    partner = jnp.where(first, pltpu.roll(x, LANES - half, 1), pltpu.roll(x, half, 1))
    return x * cos + partner * sin


def _rope_cols(x, cos, sin, half):
    chunks = [_rope128(x[:, c:c + LANES], cos, sin, half) for c in range(0, x.shape[1], LANES)]
    return chunks[0] if len(chunks) == 1 else jnp.concatenate(chunks, axis=1)


N_IN_INPUTS = 11


def _in_kernel(*refs, n_x):
    (mod_ref, g_ref, w1_ref, qn_ref, wuq_ref, kvn_ref, wukv_ref,
     cos64_ref, sin64_ref, cos32_ref, sin32_ref) = refs[n_x:n_x + N_IN_INPUTS]
    (qa_o, qd_o, qmn_o, qmr_o, kmn_o, vm_o, kpe4_o, ka_l, va_l, kd_l, vd_l,
     s_ka, s_va, s_kd, s_vd, s_ckv, s_kpe) = refs[-17:]
    i = pl.program_id(0)
    x = _load_x(refs[:n_x])
    m = mod_ref[0]
    sh1, sc1 = m[:, 0:D_MODEL], m[:, D_MODEL:2 * D_MODEL]
    h = _rms(x, g_ref[...]) * (1.0 + sc1) + sh1
    z = _dot(h.astype(BF16), w1_ref[...])
    cos64, sin64 = cos64_ref[...], sin64_ref[...]
    cos32, sin32 = cos32_ref[...], sin32_ref[...]
    qa_o[...] = _rope_cols(z[:, C_QA:C_KA], cos64, sin64, 16).astype(BF16)
    qd_o[...] = _rope_cols(z[:, C_QD:C_KD], cos64, sin64, 16).astype(BF16)
    ka = _rope_cols(z[:, C_KA:C_VA], cos64, sin64, 16)
    va = z[:, C_VA:C_QD]
    kd = _rope_cols(z[:, C_KD:C_VD], cos64, sin64, 16)
    vd = z[:, C_VD:C_QC]
    kpe4 = _rope_cols(z[:, C_KPE:C_END], cos32, sin32, 8)
    kpe4_o[...] = kpe4.astype(BF16)
    qn = _rms(z[:, C_QC:C_KVC], qn_ref[...])
    qm = _dot(qn.astype(BF16), wuq_ref[...])
    qmn_o[...] = qm[:, 0:512].astype(BF16)
    qmr_o[...] = _rope_cols(qm[:, 512:768], cos32, sin32, 8).astype(BF16)
    ckv = _rms(z[:, C_KVC:C_KPE], kvn_ref[...])
    kv = _dot(ckv.astype(BF16), wukv_ref[...])
    kmn_o[...] = kv[:, 0:512].astype(BF16)
    vm_o[...] = kv[:, 512:1024].astype(BF16)

    @pl.when(i < CTX_TILES)
    def _():
        s_ka[...] = ka
        s_va[...] = va
        s_kd[...] = kd
        s_vd[...] = vd
        s_ckv[...] = ckv
        s_kpe[...] = kpe4[:, 0:32]

    @pl.when(i >= CTX_TILES)
    def _():
        ka_l[...] = ka.astype(BF16)
        va_l[...] = va.astype(BF16)
        kd_l[...] = kd.astype(BF16)
        vd_l[...] = vd.astype(BF16)


STATE_WIDTHS = (128, 128, 512, 512, 128, 32)


def _in_projection(x, mod, wts, tables, layer, prev_states):
    tm = ROW_TILE

    def row(i):
        return (i, 0)

    def table_row(i):
        return (jnp.where(i < CTX_TILES, 0, 1 + (i - CTX_TILES) % (LAT_SEQ // tm)), 0)

    def lat_row(i):
        return (jnp.maximum(i - CTX_TILES, 0), 0)

    def state_block(i):
        return (jnp.minimum(i, CTX_TILES - 1), layer, 0, 0)

    table_spec = pl.BlockSpec((tm, LANES), table_row)
    weights = [wts["g_pre"], wts["w1"], wts["qnorm"], wts["wuq"], wts["kvnorm"], wts["wukv"]]
    all_rows = [(512, BF16)] * 3 + [(256, BF16), (512, BF16), (512, BF16), (128, BF16)]
    lat_rows = [(128, BF16), (128, BF16), (512, BF16), (512, BF16)]
    aliased = [] if prev_states is None else list(prev_states)
    xs = list(x) if isinstance(x, tuple) else [x]
    n_in = len(xs) + N_IN_INPUTS
    n_plain_out = len(all_rows) + len(lat_rows)
    return pl.pallas_call(
        functools.partial(_in_kernel, n_x=len(xs)),
        out_shape=[jax.ShapeDtypeStruct((N_TOK, w), dt) for w, dt in all_rows]
                  + [jax.ShapeDtypeStruct((N_LAT, w), dt) for w, dt in lat_rows]
                  + [jax.ShapeDtypeStruct((N_CTX_BATCH, DEPTH, CTX_SEQ, w), F32) for w in STATE_WIDTHS],
        grid=(N_TOK // tm,),
        in_specs=_x_specs(x) + [_mod_spec(layer)]
                 + [_layer_spec(w, layer, 1) for w in weights]
                 + [table_spec] * 4
                 + [pl.BlockSpec(memory_space=pl.ANY)] * len(aliased),
        out_specs=[pl.BlockSpec((tm, w), row) for w, _ in all_rows]
                  + [pl.BlockSpec((tm, w), lat_row) for w, _ in lat_rows]
                  + [pl.BlockSpec((None, None, CTX_SEQ, w), state_block) for w in STATE_WIDTHS],
        input_output_aliases={n_in + k: n_plain_out + k for k in range(len(aliased))},
        compiler_params=pltpu.CompilerParams(
            dimension_semantics=("arbitrary",), vmem_limit_bytes=VMEM_LIMIT),
        name="in_projection",
    )(*xs, mod, *weights, *tables, *aliased)


def _attend(q, parts, scale, sink=None):
    scores = []
    for k, _, mask in parts:
        s = _dot_nt(q, k) * scale
        if mask is not None:
            s = jnp.where(mask, s, NEG)
        scores.append(s)
    m = functools.reduce(jnp.maximum, [jnp.max(s, axis=-1, keepdims=True) for s in scores])
    if sink is not None:
        m = jnp.maximum(m, sink)
    den = None
    out = None
    for s, (_, v, _) in zip(scores, parts):
        e = jnp.exp(s - m)
        d = jnp.sum(e, axis=-1, keepdims=True)
        o = _dot(e.astype(BF16), v)
        den = d if den is None else den + d
        out = o if out is None else out + o
    if sink is not None:
        den = den + jnp.exp(sink - m)
    return out / den


def _three_mixers(scal_ref, subln_ref, q_blocks, kv_parts, oa_o, od_o, om_o, lam_init):
    qa, qd, qmn, qmr = q_blocks
    rows = qa.shape[0]
    lane = lax.broadcasted_iota(jnp.int32, (rows, LANES), 1)
    low = lane < 64
    zero = jnp.zeros((rows, LANES), BF16)

    for g in range(4):
        q128 = qa[:, g * LANES:(g + 1) * LANES]
        halves = []
        for kvh in range(2):
            qh = jnp.where(low if kvh == 0 else ~low, q128, zero)
            parts = [(p["ka"], p["va"], p.get("mask_a")) for p in kv_parts]
            halves.append(_attend(qh, parts, 0.125, sink=scal_ref[kvh * 4 + g]))
        oa_o[:, g * LANES:(g + 1) * LANES] = jnp.where(low, halves[0], halves[1]).astype(BF16)

    lam = scal_ref[8]
    subln = subln_ref[...]
    for h in range(4):
        sl = slice(h * LANES, (h + 1) * LANES)
        q128 = qd[:, sl]
        parts = [(p["kd"][:, sl], p["vd"][:, sl], None) for p in kv_parts]
        o1 = _attend(jnp.where(low, q128, zero), parts, 0.125)
        o2 = _attend(jnp.where(low, zero, q128), parts, 0.125)
        od = o1 - lam * o2
        od_o[:, sl] = (_rms(od, subln) * (1.0 - lam_init)).astype(BF16)

    scale_c = 1.0 / math.sqrt(96.0)
    for i in range(4):
        sl = slice(i * LANES, (i + 1) * LANES)
        keys = [jnp.concatenate([p["kmn"][:, sl], p["kpe"]], axis=1) for p in kv_parts]
        halves = []
        for sub in range(2):
            h = 2 * i + sub
            qn = jnp.where(low if sub == 0 else ~low, qmn[:, sl], zero)
            qr128 = qmr[:, (h // 4) * LANES:(h // 4 + 1) * LANES]
            qr = jnp.where((lane // 32) == (h % 4), qr128, zero)
            q256 = jnp.concatenate([qn, qr], axis=1)
            parts = [(k, p["vm"][:, sl], None) for k, p in zip(keys, kv_parts)]
            halves.append(_attend(q256, parts, scale_c))
        om_o[:, sl] = jnp.where(low, halves[0], halves[1]).astype(BF16)


def _attn_ctx_kernel(scal_ref, subln_ref, qa_ref, qd_ref, qmn_ref, qmr_ref,
                     ka_ref, va_ref, kd_ref, vd_ref, kmn_ref, vm_ref, kpe_ref,
                     oa_o, od_o, om_o, *, lam_init):
    part = dict(ka=ka_ref[...].astype(BF16), va=va_ref[...].astype(BF16),
                kd=kd_ref[...].astype(BF16), vd=vd_ref[...].astype(BF16),
                kmn=kmn_ref[...], vm=vm_ref[...], kpe=kpe_ref[...])
    _three_mixers(scal_ref, subln_ref, (qa_ref[...], qd_ref[...], qmn_ref[...], qmr_ref[...]),
                  [part], oa_o, od_o, om_o, lam_init)


def _attn_lat_kernel(scal_ref, subln_ref, wukv_ref, qa_ref, qd_ref, qmn_ref, qmr_ref,
                     ka_ref, va_ref, kd_ref, vd_ref, kmn_ref, vm_ref, kpe_ref,
                     cka_ref, cva_ref, ckd_ref, cvd_ref, cckv_ref, ckpe_ref,
                     oa_in, od_in, om_in, oa_o, od_o, om_o, *, lam_init):
    del oa_in, od_in, om_in
    n = pl.program_id(1)
    kv_c = _dot(cckv_ref[...].astype(BF16), wukv_ref[...])
    cached = dict(ka=cka_ref[...].astype(BF16), va=cva_ref[...].astype(BF16),
                  kd=ckd_ref[...].astype(BF16), vd=cvd_ref[...].astype(BF16),
                  kmn=kv_c[:, 0:512].astype(BF16), vm=kv_c[:, 512:1024].astype(BF16),
                  kpe=ckpe_ref[...].astype(BF16))
    span = 3 * Q_BLOCK
    start = pl.multiple_of(jnp.clip((n - 1) * Q_BLOCK, 0, LAT_SEQ - span), Q_BLOCK)
    qpos = n * Q_BLOCK + lax.broadcasted_iota(jnp.int32, (Q_BLOCK, span), 0)
    kpos = start + lax.broadcasted_iota(jnp.int32, (Q_BLOCK, span), 1)
    mask_a = jnp.abs(qpos - kpos) <= WINDOW
    new = dict(ka=ka_ref[pl.ds(start, span), :], va=va_ref[pl.ds(start, span), :], mask_a=mask_a,
               kd=kd_ref[...], vd=vd_ref[...], kmn=kmn_ref[...], vm=vm_ref[...], kpe=kpe_ref[...])
    _three_mixers(scal_ref, subln_ref, (qa_ref[...], qd_ref[...], qmn_ref[...], qmr_ref[...]),
                  [cached, new], oa_o, od_o, om_o, lam_init)


_SMEM_SPEC = pl.BlockSpec(memory_space=pltpu.SMEM)


def _attention_ctx(scal, subln, proj, states, layer, lam_init):
    qa, qd, qmn, qmr, kmn, vm, kpe4 = proj[:7]
    s_ka, s_va, s_kd, s_vd = states[:4]
    t = CTX_SEQ

    def blk(a):
        return pl.BlockSpec((t, a.shape[1]), lambda b: (b, 0))

    def sblk(a):
        return pl.BlockSpec((None, None, t, a.shape[-1]), lambda b: (b, layer, 0, 0))

    return pl.pallas_call(
        functools.partial(_attn_ctx_kernel, lam_init=lam_init),
        out_shape=[jax.ShapeDtypeStruct((N_TOK, 512), BF16)] * 3,
        grid=(N_CTX_BATCH,),
        in_specs=[_SMEM_SPEC, _layer_spec(subln, layer, 1)]
                 + [blk(a) for a in (qa, qd, qmn, qmr)]
                 + [sblk(a) for a in (s_ka, s_va, s_kd, s_vd)]
                 + [blk(a) for a in (kmn, vm, kpe4)],
        out_specs=[pl.BlockSpec((t, 512), lambda b: (b, 0))] * 3,
        compiler_params=pltpu.CompilerParams(vmem_limit_bytes=VMEM_LIMIT),
        name="attention_ctx",
    )(scal, subln, qa, qd, qmn, qmr, s_ka, s_va, s_kd, s_vd, kmn, vm, kpe4)


def _attention_lat(scal, subln, wukv, proj, caches, ctx_out, layer, lam_init):
    qa, qd, qmn, qmr, kmn, vm, kpe4, ka_l, va_l, kd_l, vd_l = proj[:11]
    nq = LAT_SEQ // Q_BLOCK
    q_off = N_CTX // Q_BLOCK
    kv_off = N_CTX // LAT_SEQ

    def qblk(a):
        return pl.BlockSpec((Q_BLOCK, a.shape[1]), lambda b, n: (q_off + b * nq + n, 0))

    def kvblk(a):
        return pl.BlockSpec((LAT_SEQ, a.shape[1]), lambda b, n: (b, 0))

    def kvblk_all(a):
        return pl.BlockSpec((LAT_SEQ, a.shape[1]), lambda b, n: (kv_off + b, 0))

    def cblk(a):
        return pl.BlockSpec((None, None, PAST, a.shape[-1]), lambda b, n: (b, layer, 0, 0))

    qs = [qa, qd, qmn, qmr]
    n_in = 3 + 4 + 7 + 6
    out_spec = pl.BlockSpec((Q_BLOCK, 512), lambda b, n: (q_off + b * nq + n, 0))
    return pl.pallas_call(
        functools.partial(_attn_lat_kernel, lam_init=lam_init),
        out_shape=[jax.ShapeDtypeStruct((N_TOK, 512), BF16)] * 3,
        grid=(N_LAT_BATCH, nq),
        in_specs=[_SMEM_SPEC, _layer_spec(subln, layer, 2), _layer_spec(wukv, layer, 2)]
                 + [qblk(a) for a in qs]
                 + [kvblk(a) for a in (ka_l, va_l, kd_l, vd_l)]
                 + [kvblk_all(a) for a in (kmn, vm, kpe4)]
                 + [cblk(a) for a in caches]
                 + [pl.BlockSpec(memory_space=pl.ANY)] * 3,
        out_specs=[out_spec] * 3,
        input_output_aliases={n_in + k: k for k in range(3)},
        compiler_params=pltpu.CompilerParams(vmem_limit_bytes=VMEM_LIMIT),
        name="attention_lat",
    )(scal, subln, wukv, *qs, ka_l, va_l, kd_l, vd_l, kmn, vm, kpe4, *caches, *ctx_out)


def _route(scores, biased):
    per = N_EXPERTS // N_GROUPS
    tm = scores.shape[1]
    sub = lax.broadcasted_iota(jnp.int32, (per, tm), 0).astype(F32)
    groups = [biased[g * per:(g + 1) * per, :] for g in range(N_GROUPS)]
    gscore = []
    for v in groups:
        m1 = jnp.max(v, axis=0, keepdims=True)
        first = jnp.min(jnp.where(v == m1, sub, float(per)), axis=0, keepdims=True)
        m2 = jnp.max(jnp.where(sub == first, -jnp.inf, v), axis=0, keepdims=True)
        gscore.append(m1 + m2)
    vals = []
    for g in range(N_GROUPS):
        rank = jnp.zeros((1, tm), F32)
        for o in range(N_GROUPS):
            if o == g:
                continue
            ahead = (gscore[o] >= gscore[g]) if o < g else (gscore[o] > gscore[g])
            rank = rank + jnp.where(ahead, 1.0, 0.0)
        vals.append(jnp.where(rank < TOPK_GROUPS, groups[g], NEG))
    idx = [sub + float(g * per) for g in range(N_GROUPS)]
    picks, weights = [], []
    for _ in range(TOP_K):
        best = functools.reduce(jnp.maximum, [jnp.max(v, axis=0, keepdims=True) for v in vals])
        cand = functools.reduce(jnp.minimum, [
            jnp.min(jnp.where(v == best, i, float(N_EXPERTS)), axis=0, keepdims=True)
            for v, i in zip(vals, idx)])
        wsel = jnp.zeros((1, tm), F32)
        for g in range(N_GROUPS):
            hit = idx[g] == cand
            wsel = wsel + jnp.sum(jnp.where(hit, scores[g * per:(g + 1) * per, :], 0.0),
                                  axis=0, keepdims=True)
            vals[g] = jnp.where(hit, -jnp.inf, vals[g])
        picks.append(cand)
        weights.append(wsel)
    total = functools.reduce(lambda a, b: a + b, weights)
    pad = [jnp.zeros((1, tm), F32)] * (8 - TOP_K)
    experts = jnp.concatenate(picks + pad, axis=0).astype(jnp.int32)
    gates = jnp.concatenate([x / total * ROUTED_SCALE for x in weights] + pad, axis=0)
    return experts, gates


def _post_kernel(*refs, n_x):
    (mod_ref, oa_ref, od_ref, om_ref, gpre_ref, wgl_ref, wa_ref, wb_ref, wc_ref, wo_ref, gpost_ref,
     gffn_ref, rwt_ref, rb_ref, xmid_o, h2_o, h2t_o, experts_o, gates_o) = refs[n_x:]
    x = _load_x(refs[:n_x])
    m = mod_ref[0]
    d = D_MODEL
    sh1, sc1, g1, sh2, sc2 = (m[:, 0:d], m[:, d:2 * d], m[:, 2 * d:3 * d], m[:, 3 * d:4 * d],
                              m[:, 4 * d:5 * d])
    h = _rms(x, gpre_ref[...]) * (1.0 + sc1) + sh1
    gate = jax.nn.sigmoid(_dot(h.astype(BF16), wgl_ref[...]))
    merged = (gate[:, 0:d] * _dot(oa_ref[...], wa_ref[...])
              + gate[:, d:2 * d] * _dot(od_ref[...], wb_ref[...])
              + gate[:, 2 * d:3 * d] * _dot(om_ref[...], wc_ref[...]))
    a = _dot(merged.astype(BF16), wo_ref[...])
    xm = x + g1 * _rms(a, gpost_ref[...])
    xmid_o[...] = xm
    h2f = _rms(xm, gffn_ref[...]) * (1.0 + sc2) + sh2
    h2 = h2f.astype(BF16)
    h2_o[...] = h2
    for c in range(ACC_ROWS):
        h2t_o[pl.ds(c, x.shape[0], stride=ACC_ROWS), :] = h2f[:, c * LANES:(c + 1) * LANES]
    scores = jax.nn.sigmoid(_dot_nt(rwt_ref[...], h2))
    experts, gates = _route(scores, scores + rb_ref[...])
    experts_o[...] = experts
    gates_o[...] = gates


def _post_attention(x, mod, oa, od, om, wts, layer):
    tm = ROW_TILE

    def row(i):
        return (i, 0)

    weights = [wts[k] for k in ("g_pre", "wgl", "wa", "wb", "wc", "wo", "g_post", "g_ffn", "rwt", "rb")]
    xs = list(x) if isinstance(x, tuple) else [x]
    return pl.pallas_call(
        functools.partial(_post_kernel, n_x=len(xs)),
        out_shape=[jax.ShapeDtypeStruct((N_TOK, D_MODEL), F32),
                   jax.ShapeDtypeStruct((N_TOK, D_MODEL), BF16),
                   jax.ShapeDtypeStruct((N_TOK * ACC_ROWS, LANES), F32),
                   jax.ShapeDtypeStruct((8, N_TOK), jnp.int32),
                   jax.ShapeDtypeStruct((8, N_TOK), F32)],
        grid=(N_TOK // tm,),
        in_specs=_x_specs(x) + [_mod_spec(layer),
                  pl.BlockSpec((tm, 512), row), pl.BlockSpec((tm, 512), row),
                  pl.BlockSpec((tm, 512), row)]
                 + [_layer_spec(w, layer, 1) for w in weights],
        out_specs=[pl.BlockSpec((tm, D_MODEL), row), pl.BlockSpec((tm, D_MODEL), row),
                   pl.BlockSpec((tm * ACC_ROWS, LANES), row),
                   pl.BlockSpec((8, tm), lambda i: (0, i)), pl.BlockSpec((8, tm), lambda i: (0, i))],
        compiler_params=pltpu.CompilerParams(vmem_limit_bytes=VMEM_LIMIT),
        name="post_attention",
    )(*xs, mod, oa, od, om, *weights)


def _ffn(x, wg, wu):
    return _silu(_dot(x, wg)) * _dot(x, wu)


def _moe_kernel(start_ref, count_ref, src_ref, dst_ref, gate_ref, xs_ref, wg_ref, wu_ref, wd_ref,
                acc_ref, xa, xb, ya, yb, wgb, wub, wdb, pend):
    e = pl.program_id(0)

    @pl.when(e == 0)
    def _():
        def clear(i, carry):
            r0 = pl.multiple_of(i * MOE_ZERO_ROWS, MOE_ZERO_ROWS)
            acc_ref[pl.ds(r0, MOE_ZERO_ROWS), :] = jnp.zeros((MOE_ZERO_ROWS, LANES), F32)
            return carry

        lax.fori_loop(0, acc_ref.shape[0] // MOE_ZERO_ROWS, clear, 0)

        yb[...] = jnp.zeros_like(yb)
        pend[0] = MOE_LIST_PAD_BASE

    wgb[...] = wg_ref[0].astype(BF16)
    wub[...] = wu_ref[0].astype(BF16)
    wdb[...] = wd_ref[0].astype(BF16)
    start = start_ref[e]
    pairs = (count_ref[e] + 2 * MOE_SUB - 1) // (2 * MOE_SUB)

    def gather(base, xbuf):
        for mi in range(MOE_SUB):
            src = pl.multiple_of(src_ref[base + mi], ACC_ROWS)
            xbuf[mi * ACC_ROWS:(mi + 1) * ACC_ROWS, :] = xs_ref[pl.ds(src, ACC_ROWS), :]

    def expert_ffn(xbuf, ybuf):
        x = jnp.concatenate([xbuf[pl.ds(c, MOE_SUB, stride=ACC_ROWS), :].astype(BF16)
                             for c in range(ACC_ROWS)], axis=1)
        hid = _ffn(x, wgb[...], wub[...])
        y = _dot(hid.astype(BF16), wdb[...])
        for c in range(ACC_ROWS):
            ybuf[pl.ds(c, MOE_SUB, stride=ACC_ROWS), :] = y[:, c * LANES:(c + 1) * LANES]

    def scatter(base, ybuf):
        for g0 in range(0, MOE_SUB, MOE_RMW_GROUP):
            updates = []
            for mi in range(g0, g0 + MOE_RMW_GROUP):
                dst = pl.multiple_of(dst_ref[base + mi], ACC_ROWS)
                yv = ybuf[mi * ACC_ROWS:(mi + 1) * ACC_ROWS, :]
                updates.append((dst, acc_ref[pl.ds(dst, ACC_ROWS), :] + gate_ref[base + mi] * yv))
            for dst, val in updates:
                acc_ref[pl.ds(dst, ACC_ROWS), :] = val

    @pl.when(pairs > 0)
    def _():
        gather(start, xa)

    def pair(p, carry):
        base = start + p * (2 * MOE_SUB)
        gather(base + MOE_SUB, xb)
        expert_ffn(xa, ya)
        scatter(pend[0], yb)
        gather(base + 2 * MOE_SUB, xa)
        expert_ffn(xb, yb)
        scatter(base, ya)
        pend[0] = base + MOE_SUB
        return carry

    lax.fori_loop(0, pairs, pair, 0)

    @pl.when(e == N_EXPERTS - 1)
    def _():
        scatter(pend[0], yb)


def _moe(lists, h2t, wg, wu, wd, layer):
    def expert(e, *_):
        return (layer, e, 0, 0)

    def whole(e, *_):
        return (0, 0)

    acc_rows = (N_TOK + MOE_SPARE_TOKENS) * ACC_ROWS
    assert acc_rows % MOE_ZERO_ROWS == 0
    return pl.pallas_call(
        _moe_kernel,
        out_shape=jax.ShapeDtypeStruct((acc_rows, LANES), F32),
        grid_spec=pltpu.PrefetchScalarGridSpec(
            num_scalar_prefetch=len(lists),
            grid=(N_EXPERTS,),
            in_specs=[pl.BlockSpec(h2t.shape, whole, pipeline_mode=pl.Buffered(1)),
                      pl.BlockSpec((None, 1, D_MODEL, F_EXPERT), expert),
                      pl.BlockSpec((None, 1, D_MODEL, F_EXPERT), expert),
                      pl.BlockSpec((None, 1, F_EXPERT, D_MODEL), expert)],
            out_specs=pl.BlockSpec((acc_rows, LANES), whole, pipeline_mode=pl.Buffered(1)),
            scratch_shapes=[pltpu.VMEM((MOE_SUB * ACC_ROWS, LANES), F32)] * 4
                           + [pltpu.VMEM((D_MODEL, F_EXPERT), BF16),
                              pltpu.VMEM((D_MODEL, F_EXPERT), BF16),
                              pltpu.VMEM((F_EXPERT, D_MODEL), BF16),
                              pltpu.SMEM((1,), jnp.int32)]),
        compiler_params=pltpu.CompilerParams(
            dimension_semantics=("arbitrary",), vmem_limit_bytes=MOE_VMEM_LIMIT),
        name="moe",
    )(*lists, h2t, wg, wu, wd)


def _final_kernel(acc_ref, h2_ref, xmid_ref, mod_ref, swg_ref, swu_ref, swd_ref, gpost_ref, *out_refs):
    tm = h2_ref.shape[0]
    routed = jnp.concatenate(
        [acc_ref[pl.ds(c, tm, stride=ACC_ROWS), :] for c in range(ACC_ROWS)], axis=1)
    hid = _ffn(h2_ref[...], swg_ref[...], swu_ref[...])
    shared = _dot(hid.astype(BF16), swd_ref[...])
    g2 = mod_ref[0][:, 5 * D_MODEL:6 * D_MODEL]
    out = xmid_ref[...] + g2 * _rms(routed + shared, gpost_ref[...])
    if len(out_refs) == 1:
        out_refs[0][...] = out
    else:
        @pl.when(pl.program_id(0) < CTX_TILES)
        def _():
            out_refs[0][...] = out

        @pl.when(pl.program_id(0) >= CTX_TILES)
        def _():
            out_refs[1][...] = out


def _final(acc, h2, xmid, mod, wts, layer, split):
    tm = ROW_TILE

    def row(i):
        return (i, 0)

    weights = [wts[k] for k in ("swg", "swu", "swd", "g_ffn_post")]
    return pl.pallas_call(
        _final_kernel,
        out_shape=([jax.ShapeDtypeStruct((N_CTX, D_MODEL), F32),
                    jax.ShapeDtypeStruct((N_LAT, D_MODEL), F32)] if split
                   else jax.ShapeDtypeStruct((N_TOK, D_MODEL), F32)),
        grid=(N_TOK // tm,),
        in_specs=[pl.BlockSpec((tm * ACC_ROWS, LANES), row),
                  pl.BlockSpec((tm, D_MODEL), row),
                  pl.BlockSpec((tm, D_MODEL), row),
                  _mod_spec(layer)]
                 + [_layer_spec(w, layer, 1) for w in weights],
        out_specs=_x_specs((None, None)) if split else pl.BlockSpec((tm, D_MODEL), row),
        compiler_params=pltpu.CompilerParams(
            dimension_semantics=("arbitrary",), vmem_limit_bytes=VMEM_LIMIT),
        name="ffn_final",
    )(acc, h2, xmid, mod, *weights)


def _routing_lists(experts, gates):
    e = experts[:TOP_K].reshape(-1)
    tok = jnp.tile(jnp.arange(N_TOK, dtype=jnp.int32), TOP_K)
    pad_e = jnp.repeat(jnp.arange(N_EXPERTS, dtype=jnp.int32), MOE_PAD)
    pad_tok = N_TOK + jnp.tile(jnp.arange(MOE_PAD, dtype=jnp.int32), N_EXPERTS)
    keys = jnp.concatenate([e * TOK_KEY + tok, pad_e * TOK_KEY + pad_tok])
    vals = jnp.concatenate([gates[:TOP_K].reshape(-1), jnp.zeros((N_EXPERTS * MOE_PAD,), F32)])
    keys, gate_sorted = lax.sort((keys, vals), num_keys=1)
    tail = jnp.full((MOE_SUB,), N_TOK, jnp.int32)
    tok_sorted = jnp.concatenate([keys & (TOK_KEY - 1), tail])
    gate_sorted = jnp.concatenate([gate_sorted, jnp.zeros((MOE_SUB,), F32)])
    src = jnp.minimum(tok_sorted, N_TOK - 1) * ACC_ROWS
    dst = jnp.minimum(tok_sorted, N_TOK) * ACC_ROWS
    counts = jnp.sum((e[None, :] == jnp.arange(N_EXPERTS, dtype=jnp.int32)[:, None]).astype(jnp.int32),
                     axis=1)
    start = jnp.cumsum(counts) - counts + MOE_PAD * jnp.arange(N_EXPERTS, dtype=jnp.int32)
    return start.astype(jnp.int32), counts, src, dst, gate_sorted


def _rope_tables():
    t = np.arange(LAT_SEQ)
    pos = np.stack([t // GRID_W, t % GRID_W], axis=1).astype(np.float64)

    def table(half):
        inv = ROPE_BASE ** (-np.arange(half, dtype=np.float64) / half)
        lane = np.arange(LANES)
        axis = (lane // (2 * half)) % 2
        freq = inv[lane % half]
        ang = pos[:, axis] * freq[None, :]
        sign = np.where((lane % (2 * half)) < half, -1.0, 1.0)
        cos = np.concatenate([np.ones((ROW_TILE, LANES)), np.cos(ang)], axis=0)
        sin = np.concatenate([np.zeros((ROW_TILE, LANES)), np.sin(ang) * sign[None, :]], axis=0)
        return jnp.asarray(cos, F32), jnp.asarray(sin, F32)

    cos64, sin64 = table(16)
    cos32, sin32 = table(8)
    return cos64, sin64, cos32, sin32


def _prepare_weights(norm_attn_pre, norm_attn_post, norm_ffn_pre, norm_ffn_post, w_in, dif_subln,
                     mla_q_norm, mla_w_uq, mla_kv_norm, mla_w_ukv, w_branch_a, w_branch_b,
                     w_branch_c, w_out, router_w, router_bias, shared_w_gate, shared_w_up,
                     shared_w_down):
    dp = DEPTH
    qa = w_in[:, :, 0:512].reshape(dp, D_MODEL, 2, 4, 64).transpose(0, 1, 3, 2, 4).reshape(dp, D_MODEL, 512)
    kpe = w_in[:, :, 2688:2720]
    w1 = jnp.concatenate([qa, w_in[:, :, 512:2688], kpe, kpe, kpe, kpe], axis=2).astype(BF16)
    uq = mla_w_uq.reshape(dp, 256, 8, 96)
    wuq = jnp.concatenate([uq[..., :64].reshape(dp, 256, 512), uq[..., 64:].reshape(dp, 256, 256)],
                          axis=2).astype(BF16)
    ukv = mla_w_ukv.reshape(dp, 128, 8, 128)
    wukv = jnp.concatenate([ukv[..., :64].reshape(dp, 128, 512), ukv[..., 64:].reshape(dp, 128, 512)],
                           axis=2).astype(BF16)
    wa = w_branch_a.reshape(dp, 2, 4, 64, D_MODEL).transpose(0, 2, 1, 3, 4).reshape(dp, 512, D_MODEL)
    return dict(
        g_pre=norm_attn_pre[:, None, :], g_post=norm_attn_post[:, None, :],
        g_ffn=norm_ffn_pre[:, None, :], g_ffn_post=norm_ffn_post[:, None, :],
        w1=w1, wgl=w_in[:, :, 2720:].astype(BF16),
        qnorm=mla_q_norm[:, None, :], wuq=wuq, kvnorm=mla_kv_norm[:, None, :], wukv=wukv,
        subln=dif_subln[:, None, :],
        wa=wa.astype(BF16), wb=w_branch_b.astype(BF16), wc=w_branch_c.astype(BF16),
        wo=w_out.astype(BF16), rwt=router_w.transpose(0, 2, 1).astype(BF16),
        rb=router_bias[:, :, None],
        swg=shared_w_gate.astype(BF16), swu=shared_w_up.astype(BF16), swd=shared_w_down.astype(BF16))


def kernel(x_prompt, x_sample, cache_swa_k, cache_swa_v, cache_dif_k, cache_dif_v, cache_mla_ckv, cache_mla_kpe, c, c_ctx, w_mod, b_mod, norm_attn_pre, norm_attn_post, norm_ffn_pre, norm_ffn_post, w_in, swa_sink, dif_lq1, dif_lk1, dif_lq2, dif_lk2, dif_subln, mla_q_norm, mla_w_uq, mla_kv_norm, mla_w_ukv, w_branch_a, w_branch_b, w_branch_c, w_out, router_w, router_bias, moe_w_gate, moe_w_up, moe_w_down, shared_w_gate, shared_w_up, shared_w_down):
    x = (x_prompt.reshape(N_CTX, D_MODEL), x_sample.reshape(N_LAT, D_MODEL))
    cvec = jnp.concatenate([c_ctx[None, :], c, jnp.zeros((8 - 1 - N_LAT_BATCH, D_MODEL), F32)], axis=0)
    mod = _modulation(cvec, w_mod, b_mod).reshape(DEPTH, 8, 1, 6 * D_MODEL)
    tables = _rope_tables()
    wts = _prepare_weights(norm_attn_pre, norm_attn_post, norm_ffn_pre, norm_ffn_post, w_in,
                           dif_subln, mla_q_norm, mla_w_uq, mla_kv_norm, mla_w_ukv, w_branch_a,
                           w_branch_b, w_branch_c, w_out, router_w, router_bias, shared_w_gate,
                           shared_w_up, shared_w_down)
    caches = [cache_swa_k.reshape(N_LAT_BATCH, DEPTH, PAST, 128),
              cache_swa_v.reshape(N_LAT_BATCH, DEPTH, PAST, 128),
              cache_dif_k.reshape(N_LAT_BATCH, DEPTH, PAST, 512),
              cache_dif_v.reshape(N_LAT_BATCH, DEPTH, PAST, 512),
              cache_mla_ckv,
              jnp.tile(cache_mla_kpe, (1, 1, 1, 4))]
    lam_init = [0.8 - 0.6 * math.exp(-0.3 * l) for l in range(DEPTH)]
    lam = (jnp.exp(jnp.sum(dif_lq1 * dif_lk1, axis=1)) - jnp.exp(jnp.sum(dif_lq2 * dif_lk2, axis=1))
           + jnp.asarray(lam_init, F32))
    scal = jnp.concatenate([swa_sink, lam[:, None]], axis=1).astype(F32)

    states = None
    for l in range(DEPTH):
        proj = _in_projection(x, mod, wts, tables, l, states)
        states = proj[11:]
        ctx_out = _attention_ctx(scal[l], wts["subln"], proj, states, l, lam_init[l])
        oa, od, om = _attention_lat(scal[l], wts["subln"], wts["wukv"], proj, caches, ctx_out, l,
                                    lam_init[l])
        xmid, h2, h2t, experts, gates = _post_attention(x, mod, oa, od, om, wts, l)
        lists = _routing_lists(experts, gates)
        acc = _moe(lists, h2t, moe_w_gate, moe_w_up, moe_w_down, l)
        x = _final(acc, h2, xmid, mod, wts, l, split=(l == DEPTH - 1))

    y_p = x[0].reshape(N_CTX_BATCH, CTX_SEQ, D_MODEL)
    y_s = x[1].reshape(N_LAT_BATCH, LAT_SEQ, D_MODEL)
    s_ka, s_va, s_kd, s_vd, s_ckv, s_kpe = states
    return (y_p, y_s,
            s_ka.reshape(N_CTX_BATCH, DEPTH, CTX_SEQ, 2, 64),
            s_va.reshape(N_CTX_BATCH, DEPTH, CTX_SEQ, 2, 64),
            s_kd.reshape(N_CTX_BATCH, DEPTH, CTX_SEQ, 4, 2, 64),
            s_vd.reshape(N_CTX_BATCH, DEPTH, CTX_SEQ, 4, 128),
            s_ckv, s_kpe)
```

```python
import functools
import math

import numpy as np
import jax
import jax.numpy as jnp
from jax import lax
from jax.experimental import pallas as pl
from jax.experimental.pallas import tpu as pltpu

F32 = jnp.float32
BF16 = jnp.bfloat16

D_MODEL = 1024
N_CTX_BATCH, CTX_SEQ = 16, 256
N_LAT_BATCH, LAT_SEQ = 2, 1024
PAST = 256
N_CTX = N_CTX_BATCH * CTX_SEQ
N_LAT = N_LAT_BATCH * LAT_SEQ
N_TOK = N_CTX + N_LAT
DEPTH = 2
GRID_W = 64
WINDOW = 128
N_EXPERTS = 64
N_GROUPS = 8
TOPK_GROUPS = 4
TOP_K = 6
F_EXPERT = 256
ROUTED_SCALE = 2.5
ROPE_BASE = 10000.0
EPS = 1e-6
NEG = -1e30

LANES = 128
ROW_TILE = 256
Q_BLOCK = 128
VMEM_LIMIT = 56 * 1024 * 1024
ACC_ROWS = 8
MOE_SPARE_TOKENS = 8
MOE_ZERO_ROWS = 64
MOE_VMEM_LIMIT = 62 * 1024 * 1024
MOE_SUB = 128
MOE_RMW_GROUP = 8
MOE_PAD = 2 * MOE_SUB
TOK_KEY = 8192
MOE_LIST_PAD_BASE = N_TOK * TOP_K + N_EXPERTS * MOE_PAD

C_QA, C_KA, C_VA, C_QD, C_KD, C_VD, C_QC, C_KVC, C_KPE, C_END = (
    0, 512, 640, 768, 1280, 1792, 2304, 2560, 2688, 2816)

CTX_TILES = N_CTX // ROW_TILE
assert ROW_TILE == CTX_SEQ and LAT_SEQ % ROW_TILE == 0


def _mod_row_of_tile(i):
    return jnp.where(i < CTX_TILES, 0, 1 + (i - CTX_TILES) // (LAT_SEQ // ROW_TILE))


def _rms(x, g):
    return x * lax.rsqrt(jnp.mean(x * x, axis=-1, keepdims=True) + EPS) * g


def _dot(a, b):
    return jnp.dot(a, b, preferred_element_type=F32)


def _dot_nt(a, b):
    return lax.dot_general(a, b, (((1,), (1,)), ((), ())), preferred_element_type=F32)


def _silu(x):
    return x * jax.nn.sigmoid(x)


def _layer_spec(arr, layer, grid_rank):
    zeros = (0,) * (arr.ndim - 1)
    return pl.BlockSpec((None,) + arr.shape[1:], lambda *_: (layer,) + zeros)


def _mod_kernel(c_ref, w_ref, b_ref, o_ref):
    c = c_ref[...]
    o_ref[0] = _dot(_silu(c).astype(BF16), w_ref[0].astype(BF16)) + b_ref[0]


def _modulation(cvec, w_mod, b_mod):
    tn = 1536
    n = w_mod.shape[-1]
    return pl.pallas_call(
        _mod_kernel,
        out_shape=jax.ShapeDtypeStruct((DEPTH, 8, n), F32),
        grid=(DEPTH, n // tn),
        in_specs=[
            pl.BlockSpec((8, D_MODEL), lambda l, j: (0, 0)),
            pl.BlockSpec((1, D_MODEL, tn), lambda l, j: (l, 0, j)),
            pl.BlockSpec((1, 1, tn), lambda l, j: (l, 0, j)),
        ],
        out_specs=pl.BlockSpec((1, 8, tn), lambda l, j: (l, 0, j)),
        compiler_params=pltpu.CompilerParams(vmem_limit_bytes=VMEM_LIMIT),
        name="modulation",
    )(cvec, w_mod, b_mod.reshape(DEPTH, 1, n))


def _x_specs(x):
    if isinstance(x, tuple):
        return [pl.BlockSpec((ROW_TILE, D_MODEL), lambda i: (jnp.minimum(i, CTX_TILES - 1), 0)),
                pl.BlockSpec((ROW_TILE, D_MODEL), lambda i: (jnp.maximum(i - CTX_TILES, 0), 0))]
    return [pl.BlockSpec((ROW_TILE, D_MODEL), lambda i: (i, 0))]


def _load_x(x_refs):
    if len(x_refs) == 1:
        return x_refs[0][...]
    return jnp.where(pl.program_id(0) < CTX_TILES, x_refs[0][...], x_refs[1][...])


def _mod_spec(layer):
    return pl.BlockSpec((None, 1, 1, 6 * D_MODEL), lambda i: (layer, _mod_row_of_tile(i), 0, 0))


def _rope128(x, cos, sin, half):
    lane = lax.broadcasted_iota(jnp.int32, x.shape, 1)
    first = (lane % (2 * half)) < half
    partner = jnp.where(first, pltpu.roll(x, LANES - half, 1), pltpu.roll(x, half, 1))
    return x * cos + partner * sin


def _rope_cols(x, cos, sin, half):
    chunks = [_rope128(x[:, c:c + LANES], cos, sin, half) for c in range(0, x.shape[1], LANES)]
    return chunks[0] if len(chunks) == 1 else jnp.concatenate(chunks, axis=1)


N_IN_INPUTS = 11


def _in_kernel(*refs, n_x):
    (mod_ref, g_ref, w1_ref, qn_ref, wuq_ref, kvn_ref, wukv_ref,
     cos64_ref, sin64_ref, cos32_ref, sin32_ref) = refs[n_x:n_x + N_IN_INPUTS]
    (qa_o, qd_o, qmn_o, qmr_o, kmn_o, vm_o, kpe4_o, ka_l, va_l, kd_l, vd_l,
     s_ka, s_va, s_kd, s_vd, s_ckv, s_kpe) = refs[-17:]
    i = pl.program_id(0)
    x = _load_x(refs[:n_x])
    m = mod_ref[0]
    sh1, sc1 = m[:, 0:D_MODEL], m[:, D_MODEL:2 * D_MODEL]
    h = _rms(x, g_ref[...]) * (1.0 + sc1) + sh1
    z = _dot(h.astype(BF16), w1_ref[...])
    cos64, sin64 = cos64_ref[...], sin64_ref[...]
    cos32, sin32 = cos32_ref[...], sin32_ref[...]
    qa_o[...] = _rope_cols(z[:, C_QA:C_KA], cos64, sin64, 16).astype(BF16)
    qd_o[...] = _rope_cols(z[:, C_QD:C_KD], cos64, sin64, 16).astype(BF16)
    ka = _rope_cols(z[:, C_KA:C_VA], cos64, sin64, 16)
    va = z[:, C_VA:C_QD]
    kd = _rope_cols(z[:, C_KD:C_VD], cos64, sin64, 16)
    vd = z[:, C_VD:C_QC]
    kpe4 = _rope_cols(z[:, C_KPE:C_END], cos32, sin32, 8)
    kpe4_o[...] = kpe4.astype(BF16)
    qn = _rms(z[:, C_QC:C_KVC], qn_ref[...])
    qm = _dot(qn.astype(BF16), wuq_ref[...])
    qmn_o[...] = qm[:, 0:512].astype(BF16)
    qmr_o[...] = _rope_cols(qm[:, 512:768], cos32, sin32, 8).astype(BF16)
    ckv = _rms(z[:, C_KVC:C_KPE], kvn_ref[...])
    kv = _dot(ckv.astype(BF16), wukv_ref[...])
    kmn_o[...] = kv[:, 0:512].astype(BF16)
    vm_o[...] = kv[:, 512:1024].astype(BF16)

    @pl.when(i < CTX_TILES)
    def _():
        s_ka[...] = ka
        s_va[...] = va
        s_kd[...] = kd
        s_vd[...] = vd
        s_ckv[...] = ckv
        s_kpe[...] = kpe4[:, 0:32]

    @pl.when(i >= CTX_TILES)
    def _():
        ka_l[...] = ka.astype(BF16)
        va_l[...] = va.astype(BF16)
        kd_l[...] = kd.astype(BF16)
        vd_l[...] = vd.astype(BF16)


STATE_WIDTHS = (128, 128, 512, 512, 128, 32)


def _in_projection(x, mod, wts, tables, layer, prev_states):
    tm = ROW_TILE

    def row(i):
        return (i, 0)

    def table_row(i):
        return (jnp.where(i < CTX_TILES, 0, 1 + (i - CTX_TILES) % (LAT_SEQ // tm)), 0)

    def lat_row(i):
        return (jnp.maximum(i - CTX_TILES, 0), 0)

    def state_block(i):
        return (jnp.minimum(i, CTX_TILES - 1), layer, 0, 0)

    table_spec = pl.BlockSpec((tm, LANES), table_row)
    weights = [wts["g_pre"], wts["w1"], wts["qnorm"], wts["wuq"], wts["kvnorm"], wts["wukv"]]
    all_rows = [(512, BF16)] * 3 + [(256, BF16), (512, BF16), (512, BF16), (128, BF16)]
    lat_rows = [(128, BF16), (128, BF16), (512, BF16), (512, BF16)]
    aliased = [] if prev_states is None else list(prev_states)
    xs = list(x) if isinstance(x, tuple) else [x]
    n_in = len(xs) + N_IN_INPUTS
    n_plain_out = len(all_rows) + len(lat_rows)
    return pl.pallas_call(
        functools.partial(_in_kernel, n_x=len(xs)),
        out_shape=[jax.ShapeDtypeStruct((N_TOK, w), dt) for w, dt in all_rows]
                  + [jax.ShapeDtypeStruct((N_LAT, w), dt) for w, dt in lat_rows]
                  + [jax.ShapeDtypeStruct((N_CTX_BATCH, DEPTH, CTX_SEQ, w), F32) for w in STATE_WIDTHS],
        grid=(N_TOK // tm,),
        in_specs=_x_specs(x) + [_mod_spec(layer)]
                 + [_layer_spec(w, layer, 1) for w in weights]
                 + [table_spec] * 4
                 + [pl.BlockSpec(memory_space=pl.ANY)] * len(aliased),
        out_specs=[pl.BlockSpec((tm, w), row) for w, _ in all_rows]
                  + [pl.BlockSpec((tm, w), lat_row) for w, _ in lat_rows]
                  + [pl.BlockSpec((None, None, CTX_SEQ, w), state_block) for w in STATE_WIDTHS],
        input_output_aliases={n_in + k: n_plain_out + k for k in range(len(aliased))},
        compiler_params=pltpu.CompilerParams(
            dimension_semantics=("arbitrary",), vmem_limit_bytes=VMEM_LIMIT),
        name="in_projection",
    )(*xs, mod, *weights, *tables, *aliased)


def _attend(q, parts, scale, sink=None):
    scores = []
    for k, _, mask in parts:
        s = _dot_nt(q, k) * scale
        if mask is not None:
            s = jnp.where(mask, s, NEG)
        scores.append(s)
    m = functools.reduce(jnp.maximum, [jnp.max(s, axis=-1, keepdims=True) for s in scores])
    if sink is not None:
        m = jnp.maximum(m, sink)
    den = None
    out = None
    for s, (_, v, _) in zip(scores, parts):
        e = jnp.exp(s - m)
        d = jnp.sum(e, axis=-1, keepdims=True)
        o = _dot(e.astype(BF16), v)
        den = d if den is None else den + d
        out = o if out is None else out + o
    if sink is not None:
        den = den + jnp.exp(sink - m)
    return out / den


def _three_mixers(scal_ref, subln_ref, q_blocks, kv_parts, oa_o, od_o, om_o, lam_init):
    qa, qd, qmn, qmr = q_blocks
    rows = qa.shape[0]
    lane = lax.broadcasted_iota(jnp.int32, (rows, LANES), 1)
    low = lane < 64
    zero = jnp.zeros((rows, LANES), BF16)

    for g in range(4):
        q128 = qa[:, g * LANES:(g + 1) * LANES]
        halves = []
        for kvh in range(2):
            qh = jnp.where(low if kvh == 0 else ~low, q128, zero)
            parts = [(p["ka"], p["va"], p.get("mask_a")) for p in kv_parts]
            halves.append(_attend(qh, parts, 0.125, sink=scal_ref[kvh * 4 + g]))
        oa_o[:, g * LANES:(g + 1) * LANES] = jnp.where(low, halves[0], halves[1]).astype(BF16)

    lam = scal_ref[8]
    subln = subln_ref[...]
    for h in range(4):
        sl = slice(h * LANES, (h + 1) * LANES)
        q128 = qd[:, sl]
        parts = [(p["kd"][:, sl], p["vd"][:, sl], None) for p in kv_parts]
        o1 = _attend(jnp.where(low, q128, zero), parts, 0.125)
        o2 = _attend(jnp.where(low, zero, q128), parts, 0.125)
        od = o1 - lam * o2
        od_o[:, sl] = (_rms(od, subln) * (1.0 - lam_init)).astype(BF16)

    scale_c = 1.0 / math.sqrt(96.0)
    for i in range(4):
        sl = slice(i * LANES, (i + 1) * LANES)
        keys = [jnp.concatenate([p["kmn"][:, sl], p["kpe"]], axis=1) for p in kv_parts]
        halves = []
        for sub in range(2):
            h = 2 * i + sub
            qn = jnp.where(low if sub == 0 else ~low, qmn[:, sl], zero)
            qr128 = qmr[:, (h // 4) * LANES:(h // 4 + 1) * LANES]
            qr = jnp.where((lane // 32) == (h % 4), qr128, zero)
            q256 = jnp.concatenate([qn, qr], axis=1)
            parts = [(k, p["vm"][:, sl], None) for k, p in zip(keys, kv_parts)]
            halves.append(_attend(q256, parts, scale_c))
        om_o[:, sl] = jnp.where(low, halves[0], halves[1]).astype(BF16)


def _attn_ctx_kernel(scal_ref, subln_ref, qa_ref, qd_ref, qmn_ref, qmr_ref,
                     ka_ref, va_ref, kd_ref, vd_ref, kmn_ref, vm_ref, kpe_ref,
                     oa_o, od_o, om_o, *, lam_init):
    part = dict(ka=ka_ref[...].astype(BF16), va=va_ref[...].astype(BF16),
                kd=kd_ref[...].astype(BF16), vd=vd_ref[...].astype(BF16),
                kmn=kmn_ref[...], vm=vm_ref[...], kpe=kpe_ref[...])
    _three_mixers(scal_ref, subln_ref, (qa_ref[...], qd_ref[...], qmn_ref[...], qmr_ref[...]),
                  [part], oa_o, od_o, om_o, lam_init)


def _attn_lat_kernel(scal_ref, subln_ref, wukv_ref, qa_ref, qd_ref, qmn_ref, qmr_ref,
                     ka_ref, va_ref, kd_ref, vd_ref, kmn_ref, vm_ref, kpe_ref,
                     cka_ref, cva_ref, ckd_ref, cvd_ref, cckv_ref, ckpe_ref,
                     oa_in, od_in, om_in, oa_o, od_o, om_o, *, lam_init):
    del oa_in, od_in, om_in
    n = pl.program_id(1)
    kv_c = _dot(cckv_ref[...].astype(BF16), wukv_ref[...])
    cached = dict(ka=cka_ref[...].astype(BF16), va=cva_ref[...].astype(BF16),
                  kd=ckd_ref[...].astype(BF16), vd=cvd_ref[...].astype(BF16),
                  kmn=kv_c[:, 0:512].astype(BF16), vm=kv_c[:, 512:1024].astype(BF16),
                  kpe=ckpe_ref[...].astype(BF16))
    span = 3 * Q_BLOCK
    start = pl.multiple_of(jnp.clip((n - 1) * Q_BLOCK, 0, LAT_SEQ - span), Q_BLOCK)
    qpos = n * Q_BLOCK + lax.broadcasted_iota(jnp.int32, (Q_BLOCK, span), 0)
    kpos = start + lax.broadcasted_iota(jnp.int32, (Q_BLOCK, span), 1)
    mask_a = jnp.abs(qpos - kpos) <= WINDOW
    new = dict(ka=ka_ref[pl.ds(start, span), :], va=va_ref[pl.ds(start, span), :], mask_a=mask_a,
               kd=kd_ref[...], vd=vd_ref[...], kmn=kmn_ref[...], vm=vm_ref[...], kpe=kpe_ref[...])
    _three_mixers(scal_ref, subln_ref, (qa_ref[...], qd_ref[...], qmn_ref[...], qmr_ref[...]),
                  [cached, new], oa_o, od_o, om_o, lam_init)


_SMEM_SPEC = pl.BlockSpec(memory_space=pltpu.SMEM)


def _attention_ctx(scal, subln, proj, states, layer, lam_init):
    qa, qd, qmn, qmr, kmn, vm, kpe4 = proj[:7]
    s_ka, s_va, s_kd, s_vd = states[:4]
    t = CTX_SEQ

    def blk(a):
        return pl.BlockSpec((t, a.shape[1]), lambda b: (b, 0))

    def sblk(a):
        return pl.BlockSpec((None, None, t, a.shape[-1]), lambda b: (b, layer, 0, 0))

    return pl.pallas_call(
        functools.partial(_attn_ctx_kernel, lam_init=lam_init),
        out_shape=[jax.ShapeDtypeStruct((N_TOK, 512), BF16)] * 3,
        grid=(N_CTX_BATCH,),
        in_specs=[_SMEM_SPEC, _layer_spec(subln, layer, 1)]
                 + [blk(a) for a in (qa, qd, qmn, qmr)]
                 + [sblk(a) for a in (s_ka, s_va, s_kd, s_vd)]
                 + [blk(a) for a in (kmn, vm, kpe4)],
        out_specs=[pl.BlockSpec((t, 512), lambda b: (b, 0))] * 3,
        compiler_params=pltpu.CompilerParams(vmem_limit_bytes=VMEM_LIMIT),
        name="attention_ctx",
    )(scal, subln, qa, qd, qmn, qmr, s_ka, s_va, s_kd, s_vd, kmn, vm, kpe4)


def _attention_lat(scal, subln, wukv, proj, caches, ctx_out, layer, lam_init):
    qa, qd, qmn, qmr, kmn, vm, kpe4, ka_l, va_l, kd_l, vd_l = proj[:11]
    nq = LAT_SEQ // Q_BLOCK
    q_off = N_CTX // Q_BLOCK
    kv_off = N_CTX // LAT_SEQ

    def qblk(a):
        return pl.BlockSpec((Q_BLOCK, a.shape[1]), lambda b, n: (q_off + b * nq + n, 0))

    def kvblk(a):
        return pl.BlockSpec((LAT_SEQ, a.shape[1]), lambda b, n: (b, 0))

    def kvblk_all(a):
        return pl.BlockSpec((LAT_SEQ, a.shape[1]), lambda b, n: (kv_off + b, 0))

    def cblk(a):
        return pl.BlockSpec((None, None, PAST, a.shape[-1]), lambda b, n: (b, layer, 0, 0))

    qs = [qa, qd, qmn, qmr]
    n_in = 3 + 4 + 7 + 6
    out_spec = pl.BlockSpec((Q_BLOCK, 512), lambda b, n: (q_off + b * nq + n, 0))
    return pl.pallas_call(
        functools.partial(_attn_lat_kernel, lam_init=lam_init),
        out_shape=[jax.ShapeDtypeStruct((N_TOK, 512), BF16)] * 3,
        grid=(N_LAT_BATCH, nq),
        in_specs=[_SMEM_SPEC, _layer_spec(subln, layer, 2), _layer_spec(wukv, layer, 2)]
                 + [qblk(a) for a in qs]
                 + [kvblk(a) for a in (ka_l, va_l, kd_l, vd_l)]
                 + [kvblk_all(a) for a in (kmn, vm, kpe4)]
                 + [cblk(a) for a in caches]
                 + [pl.BlockSpec(memory_space=pl.ANY)] * 3,
        out_specs=[out_spec] * 3,
        input_output_aliases={n_in + k: k for k in range(3)},
        compiler_params=pltpu.CompilerParams(vmem_limit_bytes=VMEM_LIMIT),
        name="attention_lat",
    )(scal, subln, wukv, *qs, ka_l, va_l, kd_l, vd_l, kmn, vm, kpe4, *caches, *ctx_out)


def _route(scores, biased):
    per = N_EXPERTS // N_GROUPS
    tm = scores.shape[1]
    sub = lax.broadcasted_iota(jnp.int32, (per, tm), 0).astype(F32)
    groups = [biased[g * per:(g + 1) * per, :] for g in range(N_GROUPS)]
    gscore = []
    for v in groups:
        m1 = jnp.max(v, axis=0, keepdims=True)
        first = jnp.min(jnp.where(v == m1, sub, float(per)), axis=0, keepdims=True)
        m2 = jnp.max(jnp.where(sub == first, -jnp.inf, v), axis=0, keepdims=True)
        gscore.append(m1 + m2)
    vals = []
    for g in range(N_GROUPS):
        rank = jnp.zeros((1, tm), F32)
        for o in range(N_GROUPS):
            if o == g:
                continue
            ahead = (gscore[o] >= gscore[g]) if o < g else (gscore[o] > gscore[g])
            rank = rank + jnp.where(ahead, 1.0, 0.0)
        vals.append(jnp.where(rank < TOPK_GROUPS, groups[g], NEG))
    idx = [sub + float(g * per) for g in range(N_GROUPS)]
    picks, weights = [], []
    for _ in range(TOP_K):
        best = functools.reduce(jnp.maximum, [jnp.max(v, axis=0, keepdims=True) for v in vals])
        cand = functools.reduce(jnp.minimum, [
            jnp.min(jnp.where(v == best, i, float(N_EXPERTS)), axis=0, keepdims=True)
            for v, i in zip(vals, idx)])
        wsel = jnp.zeros((1, tm), F32)
        for g in range(N_GROUPS):
            hit = idx[g] == cand
            wsel = wsel + jnp.sum(jnp.where(hit, scores[g * per:(g + 1) * per, :], 0.0),
                                  axis=0, keepdims=True)
            vals[g] = jnp.where(hit, -jnp.inf, vals[g])
        picks.append(cand)
        weights.append(wsel)
    total = functools.reduce(lambda a, b: a + b, weights)
    pad = [jnp.zeros((1, tm), F32)] * (8 - TOP_K)
    experts = jnp.concatenate(picks + pad, axis=0).astype(jnp.int32)
    gates = jnp.concatenate([x / total * ROUTED_SCALE for x in weights] + pad, axis=0)
    return experts, gates


def _post_kernel(*refs, n_x):
    (mod_ref, oa_ref, od_ref, om_ref, gpre_ref, wgl_ref, wa_ref, wb_ref, wc_ref, wo_ref, gpost_ref,
     gffn_ref, rwt_ref, rb_ref, xmid_o, h2_o, h2t_o, experts_o, gates_o) = refs[n_x:]
    x = _load_x(refs[:n_x])
    m = mod_ref[0]
    d = D_MODEL
    sh1, sc1, g1, sh2, sc2 = (m[:, 0:d], m[:, d:2 * d], m[:, 2 * d:3 * d], m[:, 3 * d:4 * d],
                              m[:, 4 * d:5 * d])
    h = _rms(x, gpre_ref[...]) * (1.0 + sc1) + sh1
    gate = jax.nn.sigmoid(_dot(h.astype(BF16), wgl_ref[...]))
    merged = (gate[:, 0:d] * _dot(oa_ref[...], wa_ref[...])
              + gate[:, d:2 * d] * _dot(od_ref[...], wb_ref[...])
              + gate[:, 2 * d:3 * d] * _dot(om_ref[...], wc_ref[...]))
    a = _dot(merged.astype(BF16), wo_ref[...])
    xm = x + g1 * _rms(a, gpost_ref[...])
    xmid_o[...] = xm
    h2f = _rms(xm, gffn_ref[...]) * (1.0 + sc2) + sh2
    h2 = h2f.astype(BF16)
    h2_o[...] = h2
    for c in range(ACC_ROWS):
        h2t_o[pl.ds(c, x.shape[0], stride=ACC_ROWS), :] = h2f[:, c * LANES:(c + 1) * LANES]
    scores = jax.nn.sigmoid(_dot_nt(rwt_ref[...], h2))
    experts, gates = _route(scores, scores + rb_ref[...])
    experts_o[...] = experts
    gates_o[...] = gates


def _post_attention(x, mod, oa, od, om, wts, layer):
    tm = ROW_TILE

    def row(i):
        return (i, 0)

    weights = [wts[k] for k in ("g_pre", "wgl", "wa", "wb", "wc", "wo", "g_post", "g_ffn", "rwt", "rb")]
    xs = list(x) if isinstance(x, tuple) else [x]
    return pl.pallas_call(
        functools.partial(_post_kernel, n_x=len(xs)),
        out_shape=[jax.ShapeDtypeStruct((N_TOK, D_MODEL), F32),
                   jax.ShapeDtypeStruct((N_TOK, D_MODEL), BF16),
                   jax.ShapeDtypeStruct((N_TOK * ACC_ROWS, LANES), F32),
                   jax.ShapeDtypeStruct((8, N_TOK), jnp.int32),
                   jax.ShapeDtypeStruct((8, N_TOK), F32)],
        grid=(N_TOK // tm,),
        in_specs=_x_specs(x) + [_mod_spec(layer),
                  pl.BlockSpec((tm, 512), row), pl.BlockSpec((tm, 512), row),
                  pl.BlockSpec((tm, 512), row)]
                 + [_layer_spec(w, layer, 1) for w in weights],
        out_specs=[pl.BlockSpec((tm, D_MODEL), row), pl.BlockSpec((tm, D_MODEL), row),
                   pl.BlockSpec((tm * ACC_ROWS, LANES), row),
                   pl.BlockSpec((8, tm), lambda i: (0, i)), pl.BlockSpec((8, tm), lambda i: (0, i))],
        compiler_params=pltpu.CompilerParams(vmem_limit_bytes=VMEM_LIMIT),
        name="post_attention",
    )(*xs, mod, oa, od, om, *weights)


def _ffn(x, wg, wu):
    return _silu(_dot(x, wg)) * _dot(x, wu)


def _moe_kernel(start_ref, count_ref, dst_ref, gate_ref, xs_ref, wg_ref, wu_ref, wd_ref,
                acc_ref, xa, xb, ya, yb, wgb, wub, wdb, pend):
    e = pl.program_id(0)

    @pl.when(e == 0)
    def _():
        def clear(i, carry):
            r0 = pl.multiple_of(i * MOE_ZERO_ROWS, MOE_ZERO_ROWS)
            acc_ref[pl.ds(r0, MOE_ZERO_ROWS), :] = jnp.zeros((MOE_ZERO_ROWS, LANES), F32)
            return carry

        lax.fori_loop(0, acc_ref.shape[0] // MOE_ZERO_ROWS, clear, 0)

        yb[...] = jnp.zeros_like(yb)
        pend[0] = MOE_LIST_PAD_BASE

    wgb[...] = wg_ref[0].astype(BF16)
    wub[...] = wu_ref[0].astype(BF16)
    wdb[...] = wd_ref[0].astype(BF16)
    start = start_ref[e]
    pairs = (count_ref[e] + 2 * MOE_SUB - 1) // (2 * MOE_SUB)

    def gather(base, xbuf):
        for mi in range(MOE_SUB):
            src = pl.multiple_of(jnp.minimum(dst_ref[base + mi], (N_TOK - 1) * ACC_ROWS), ACC_ROWS)
            xbuf[mi * ACC_ROWS:(mi + 1) * ACC_ROWS, :] = xs_ref[pl.ds(src, ACC_ROWS), :]

    def expert_ffn(xbuf, ybuf):
        x = jnp.concatenate([xbuf[pl.ds(c, MOE_SUB, stride=ACC_ROWS), :].astype(BF16)
                             for c in range(ACC_ROWS)], axis=1)
        hid = _ffn(x, wgb[...], wub[...])
        y = _dot(hid.astype(BF16), wdb[...])
        for c in range(ACC_ROWS):
            ybuf[pl.ds(c, MOE_SUB, stride=ACC_ROWS), :] = y[:, c * LANES:(c + 1) * LANES]

    def scatter(base, ybuf):
        for g0 in range(0, MOE_SUB, MOE_RMW_GROUP):
            updates = []
            for mi in range(g0, g0 + MOE_RMW_GROUP):
                dst = pl.multiple_of(dst_ref[base + mi], ACC_ROWS)
                yv = ybuf[mi * ACC_ROWS:(mi + 1) * ACC_ROWS, :]
                updates.append((dst, acc_ref[pl.ds(dst, ACC_ROWS), :] + gate_ref[base + mi] * yv))
            for dst, val in updates:
                acc_ref[pl.ds(dst, ACC_ROWS), :] = val

    @pl.when(pairs > 0)
    def _():
        gather(start, xa)

    def pair(p, carry):
        base = start + p * (2 * MOE_SUB)
        gather(base + MOE_SUB, xb)
        expert_ffn(xa, ya)
        scatter(pend[0], yb)
        gather(base + 2 * MOE_SUB, xa)
        expert_ffn(xb, yb)
        scatter(base, ya)
        pend[0] = base + MOE_SUB
        return carry

    lax.fori_loop(0, pairs, pair, 0)

    @pl.when(e == N_EXPERTS - 1)
    def _():
        scatter(pend[0], yb)


def _moe(lists, h2t, wg, wu, wd, layer):
    def expert(e, *_):
        return (layer, e, 0, 0)

    def whole(e, *_):
        return (0, 0)

    acc_rows = (N_TOK + MOE_SPARE_TOKENS) * ACC_ROWS
    assert acc_rows % MOE_ZERO_ROWS == 0
    return pl.pallas_call(
        _moe_kernel,
        out_shape=jax.ShapeDtypeStruct((acc_rows, LANES), F32),
        grid_spec=pltpu.PrefetchScalarGridSpec(
            num_scalar_prefetch=len(lists),
            grid=(N_EXPERTS,),
            in_specs=[pl.BlockSpec(h2t.shape, whole, pipeline_mode=pl.Buffered(1)),
                      pl.BlockSpec((None, 1, D_MODEL, F_EXPERT), expert),
                      pl.BlockSpec((None, 1, D_MODEL, F_EXPERT), expert),
                      pl.BlockSpec((None, 1, F_EXPERT, D_MODEL), expert)],
            out_specs=pl.BlockSpec((acc_rows, LANES), whole, pipeline_mode=pl.Buffered(1)),
            scratch_shapes=[pltpu.VMEM((MOE_SUB * ACC_ROWS, LANES), F32)] * 4
                           + [pltpu.VMEM((D_MODEL, F_EXPERT), BF16),
                              pltpu.VMEM((D_MODEL, F_EXPERT), BF16),
                              pltpu.VMEM((F_EXPERT, D_MODEL), BF16),
                              pltpu.SMEM((1,), jnp.int32)]),
        compiler_params=pltpu.CompilerParams(
            dimension_semantics=("arbitrary",), vmem_limit_bytes=MOE_VMEM_LIMIT),
        name="moe",
    )(*lists, h2t, wg, wu, wd)


def _final_kernel(acc_ref, h2_ref, xmid_ref, mod_ref, swg_ref, swu_ref, swd_ref, gpost_ref, *out_refs):
    tm = h2_ref.shape[0]
    routed = jnp.concatenate(
        [acc_ref[pl.ds(c, tm, stride=ACC_ROWS), :] for c in range(ACC_ROWS)], axis=1)
    hid = _ffn(h2_ref[...], swg_ref[...], swu_ref[...])
    shared = _dot(hid.astype(BF16), swd_ref[...])
    g2 = mod_ref[0][:, 5 * D_MODEL:6 * D_MODEL]
    out = xmid_ref[...] + g2 * _rms(routed + shared, gpost_ref[...])
    if len(out_refs) == 1:
        out_refs[0][...] = out
    else:
        @pl.when(pl.program_id(0) < CTX_TILES)
        def _():
            out_refs[0][...] = out

        @pl.when(pl.program_id(0) >= CTX_TILES)
        def _():
            out_refs[1][...] = out


def _final(acc, h2, xmid, mod, wts, layer, split):
    tm = ROW_TILE

    def row(i):
        return (i, 0)

    weights = [wts[k] for k in ("swg", "swu", "swd", "g_ffn_post")]
    return pl.pallas_call(
        _final_kernel,
        out_shape=([jax.ShapeDtypeStruct((N_CTX, D_MODEL), F32),
                    jax.ShapeDtypeStruct((N_LAT, D_MODEL), F32)] if split
                   else jax.ShapeDtypeStruct((N_TOK, D_MODEL), F32)),
        grid=(N_TOK // tm,),
        in_specs=[pl.BlockSpec((tm * ACC_ROWS, LANES), row),
                  pl.BlockSpec((tm, D_MODEL), row),
                  pl.BlockSpec((tm, D_MODEL), row),
                  _mod_spec(layer)]
                 + [_layer_spec(w, layer, 1) for w in weights],
        out_specs=_x_specs((None, None)) if split else pl.BlockSpec((tm, D_MODEL), row),
        compiler_params=pltpu.CompilerParams(
            dimension_semantics=("arbitrary",), vmem_limit_bytes=VMEM_LIMIT),
        name="ffn_final",
    )(acc, h2, xmid, mod, *weights)


def _routing_lists(experts, gates):
    e = experts[:TOP_K].reshape(-1)
    tok = jnp.tile(jnp.arange(N_TOK, dtype=jnp.int32), TOP_K)
    pad_e = jnp.repeat(jnp.arange(N_EXPERTS, dtype=jnp.int32), MOE_PAD)
    pad_tok = N_TOK + jnp.tile(jnp.arange(MOE_PAD, dtype=jnp.int32), N_EXPERTS)
    keys = jnp.concatenate([e * TOK_KEY + tok, pad_e * TOK_KEY + pad_tok])
    vals = jnp.concatenate([gates[:TOP_K].reshape(-1), jnp.zeros((N_EXPERTS * MOE_PAD,), F32)])
    keys, gate_sorted = lax.sort((keys, vals), num_keys=1)
    tail = jnp.full((MOE_SUB,), N_TOK, jnp.int32)
    tok_sorted = jnp.concatenate([keys & (TOK_KEY - 1), tail])
    gate_sorted = jnp.concatenate([gate_sorted, jnp.zeros((MOE_SUB,), F32)])
    dst = jnp.minimum(tok_sorted, N_TOK) * ACC_ROWS
    counts = jnp.sum((e[None, :] == jnp.arange(N_EXPERTS, dtype=jnp.int32)[:, None]).astype(jnp.int32),
                     axis=1)
    start = jnp.cumsum(counts) - counts + MOE_PAD * jnp.arange(N_EXPERTS, dtype=jnp.int32)
    return start.astype(jnp.int32), counts, dst, gate_sorted


def _rope_tables():
    t = np.arange(LAT_SEQ)
    pos = np.stack([t // GRID_W, t % GRID_W], axis=1).astype(np.float64)

    def table(half):
        inv = ROPE_BASE ** (-np.arange(half, dtype=np.float64) / half)
        lane = np.arange(LANES)
        axis = (lane // (2 * half)) % 2
        freq = inv[lane % half]
        ang = pos[:, axis] * freq[None, :]
        sign = np.where((lane % (2 * half)) < half, -1.0, 1.0)
        cos = np.concatenate([np.ones((ROW_TILE, LANES)), np.cos(ang)], axis=0)
        sin = np.concatenate([np.zeros((ROW_TILE, LANES)), np.sin(ang) * sign[None, :]], axis=0)
        return jnp.asarray(cos, F32), jnp.asarray(sin, F32)

    cos64, sin64 = table(16)
    cos32, sin32 = table(8)
    return cos64, sin64, cos32, sin32


def _prepare_weights(norm_attn_pre, norm_attn_post, norm_ffn_pre, norm_ffn_post, w_in, dif_subln,
                     mla_q_norm, mla_w_uq, mla_kv_norm, mla_w_ukv, w_branch_a, w_branch_b,
                     w_branch_c, w_out, router_w, router_bias, shared_w_gate, shared_w_up,
                     shared_w_down):
    dp = DEPTH
    qa = w_in[:, :, 0:512].reshape(dp, D_MODEL, 2, 4, 64).transpose(0, 1, 3, 2, 4).reshape(dp, D_MODEL, 512)
    kpe = w_in[:, :, 2688:2720]
    w1 = jnp.concatenate([qa, w_in[:, :, 512:2688], kpe, kpe, kpe, kpe], axis=2).astype(BF16)
    uq = mla_w_uq.reshape(dp, 256, 8, 96)
    wuq = jnp.concatenate([uq[..., :64].reshape(dp, 256, 512), uq[..., 64:].reshape(dp, 256, 256)],
                          axis=2).astype(BF16)
    ukv = mla_w_ukv.reshape(dp, 128, 8, 128)
    wukv = jnp.concatenate([ukv[..., :64].reshape(dp, 128, 512), ukv[..., 64:].reshape(dp, 128, 512)],
                           axis=2).astype(BF16)
    wa = w_branch_a.reshape(dp, 2, 4, 64, D_MODEL).transpose(0, 2, 1, 3, 4).reshape(dp, 512, D_MODEL)
    return dict(
        g_pre=norm_attn_pre[:, None, :], g_post=norm_attn_post[:, None, :],
        g_ffn=norm_ffn_pre[:, None, :], g_ffn_post=norm_ffn_post[:, None, :],
        w1=w1, wgl=w_in[:, :, 2720:].astype(BF16),
        qnorm=mla_q_norm[:, None, :], wuq=wuq, kvnorm=mla_kv_norm[:, None, :], wukv=wukv,
        subln=dif_subln[:, None, :],
        wa=wa.astype(BF16), wb=w_branch_b.astype(BF16), wc=w_branch_c.astype(BF16),
        wo=w_out.astype(BF16), rwt=router_w.transpose(0, 2, 1).astype(BF16),
        rb=router_bias[:, :, None],
        swg=shared_w_gate.astype(BF16), swu=shared_w_up.astype(BF16), swd=shared_w_down.astype(BF16))


def kernel(x_prompt, x_sample, cache_swa_k, cache_swa_v, cache_dif_k, cache_dif_v, cache_mla_ckv, cache_mla_kpe, c, c_ctx, w_mod, b_mod, norm_attn_pre, norm_attn_post, norm_ffn_pre, norm_ffn_post, w_in, swa_sink, dif_lq1, dif_lk1, dif_lq2, dif_lk2, dif_subln, mla_q_norm, mla_w_uq, mla_kv_norm, mla_w_ukv, w_branch_a, w_branch_b, w_branch_c, w_out, router_w, router_bias, moe_w_gate, moe_w_up, moe_w_down, shared_w_gate, shared_w_up, shared_w_down):
    x = (x_prompt.reshape(N_CTX, D_MODEL), x_sample.reshape(N_LAT, D_MODEL))
    cvec = jnp.concatenate([c_ctx[None, :], c, jnp.zeros((8 - 1 - N_LAT_BATCH, D_MODEL), F32)], axis=0)
    mod = _modulation(cvec, w_mod, b_mod).reshape(DEPTH, 8, 1, 6 * D_MODEL)
    tables = _rope_tables()
    wts = _prepare_weights(norm_attn_pre, norm_attn_post, norm_ffn_pre, norm_ffn_post, w_in,
                           dif_subln, mla_q_norm, mla_w_uq, mla_kv_norm, mla_w_ukv, w_branch_a,
                           w_branch_b, w_branch_c, w_out, router_w, router_bias, shared_w_gate,
                           shared_w_up, shared_w_down)
    caches = [cache_swa_k.reshape(N_LAT_BATCH, DEPTH, PAST, 128),
              cache_swa_v.reshape(N_LAT_BATCH, DEPTH, PAST, 128),
              cache_dif_k.reshape(N_LAT_BATCH, DEPTH, PAST, 512),
              cache_dif_v.reshape(N_LAT_BATCH, DEPTH, PAST, 512),
              cache_mla_ckv,
              jnp.tile(cache_mla_kpe, (1, 1, 1, 4))]
    lam_init = [0.8 - 0.6 * math.exp(-0.3 * l) for l in range(DEPTH)]
    lam = (jnp.exp(jnp.sum(dif_lq1 * dif_lk1, axis=1)) - jnp.exp(jnp.sum(dif_lq2 * dif_lk2, axis=1))
           + jnp.asarray(lam_init, F32))
    scal = jnp.concatenate([swa_sink, lam[:, None]], axis=1).astype(F32)

    states = None
    for l in range(DEPTH):
        proj = _in_projection(x, mod, wts, tables, l, states)
        states = proj[11:]
        ctx_out = _attention_ctx(scal[l], wts["subln"], proj, states, l, lam_init[l])
        oa, od, om = _attention_lat(scal[l], wts["subln"], wts["wukv"], proj, caches, ctx_out, l,
                                    lam_init[l])
        xmid, h2, h2t, experts, gates = _post_attention(x, mod, oa, od, om, wts, l)
        lists = _routing_lists(experts, gates)
        acc = _moe(lists, h2t, moe_w_gate, moe_w_up, moe_w_down, l)
        x = _final(acc, h2, xmid, mod, wts, l, split=(l == DEPTH - 1))

    y_p = x[0].reshape(N_CTX_BATCH, CTX_SEQ, D_MODEL)
    y_s = x[1].reshape(N_LAT_BATCH, LAT_SEQ, D_MODEL)
    s_ka, s_va, s_kd, s_vd, s_ckv, s_kpe = states
    return (y_p, y_s,
            s_ka.reshape(N_CTX_BATCH, DEPTH, CTX_SEQ, 2, 64),
            s_va.reshape(N_CTX_BATCH, DEPTH, CTX_SEQ, 2, 64),
            s_kd.reshape(N_CTX_BATCH, DEPTH, CTX_SEQ, 4, 2, 64),
            s_vd.reshape(N_CTX_BATCH, DEPTH, CTX_SEQ, 4, 128),
            s_ckv, s_kpe)
```

```python
import functools
import math

import numpy as np
import jax
import jax.numpy as jnp
from jax import lax
from jax.experimental import pallas as pl
from jax.experimental.pallas import tpu as pltpu

F32 = jnp.float32
BF16 = jnp.bfloat16

D_MODEL = 1024
N_CTX_BATCH, CTX_SEQ = 16, 256
N_LAT_BATCH, LAT_SEQ = 2, 1024
PAST = 256
N_CTX = N_CTX_BATCH * CTX_SEQ
N_LAT = N_LAT_BATCH * LAT_SEQ
N_TOK = N_CTX + N_LAT
DEPTH = 2
GRID_W = 64
WINDOW = 128
N_EXPERTS = 64
N_GROUPS = 8
TOPK_GROUPS = 4
TOP_K = 6
F_EXPERT = 256
ROUTED_SCALE = 2.5
ROPE_BASE = 10000.0
EPS = 1e-6
NEG = -1e30

LANES = 128
ROW_TILE = 256
Q_BLOCK = 128
VMEM_LIMIT = 56 * 1024 * 1024
ACC_ROWS = 8
MOE_SPARE_TOKENS = 8
MOE_ZERO_ROWS = 64
MOE_VMEM_LIMIT = 62 * 1024 * 1024
MOE_SUB = 128
MOE_RMW_GROUP = 8
MOE_PAD = 2 * MOE_SUB
TOK_KEY = 8192
MOE_LIST_PAD_BASE = N_TOK * TOP_K + N_EXPERTS * MOE_PAD

C_QA, C_KA, C_VA, C_QD, C_KD, C_VD, C_QC, C_KVC, C_KPE, C_END = (
    0, 512, 640, 768, 1280, 1792, 2304, 2560, 2688, 2816)

SCALE_64 = 1.0 / math.sqrt(64.0)
SCALE_96 = 1.0 / math.sqrt(96.0)

CTX_TILES = N_CTX // ROW_TILE
assert ROW_TILE == CTX_SEQ and LAT_SEQ % ROW_TILE == 0


def _mod_row_of_tile(i):
    return jnp.where(i < CTX_TILES, 0, 1 + (i - CTX_TILES) // (LAT_SEQ // ROW_TILE))


def _rms(x, g):
    return x * lax.rsqrt(jnp.mean(x * x, axis=-1, keepdims=True) + EPS) * g


def _dot(a, b):
    return jnp.dot(a, b, preferred_element_type=F32)


def _dot_nt(a, b):
    return lax.dot_general(a, b, (((1,), (1,)), ((), ())), preferred_element_type=F32)


def _silu(x):
    return x * jax.nn.sigmoid(x)


def _layer_spec(arr, layer, grid_rank):
    zeros = (0,) * (arr.ndim - 1)
    return pl.BlockSpec((None,) + arr.shape[1:], lambda *_: (layer,) + zeros)


def _mod_kernel(c_ref, w_ref, b_ref, o_ref):
    c = c_ref[...]
    o_ref[0] = _dot(_silu(c).astype(BF16), w_ref[0].astype(BF16)) + b_ref[0]


def _modulation(cvec, w_mod, b_mod):
    tn = 1536
    n = w_mod.shape[-1]
    return pl.pallas_call(
        _mod_kernel,
        out_shape=jax.ShapeDtypeStruct((DEPTH, 8, n), F32),
        grid=(DEPTH, n // tn),
        in_specs=[
            pl.BlockSpec((8, D_MODEL), lambda l, j: (0, 0)),
            pl.BlockSpec((1, D_MODEL, tn), lambda l, j: (l, 0, j)),
            pl.BlockSpec((1, 1, tn), lambda l, j: (l, 0, j)),
        ],
        out_specs=pl.BlockSpec((1, 8, tn), lambda l, j: (l, 0, j)),
        compiler_params=pltpu.CompilerParams(vmem_limit_bytes=VMEM_LIMIT),
        name="modulation",
    )(cvec, w_mod, b_mod.reshape(DEPTH, 1, n))


def _x_specs(x):
    if isinstance(x, tuple):
        return [pl.BlockSpec((ROW_TILE, D_MODEL), lambda i: (jnp.minimum(i, CTX_TILES - 1), 0)),
                pl.BlockSpec((ROW_TILE, D_MODEL), lambda i: (jnp.maximum(i - CTX_TILES, 0), 0))]
    return [pl.BlockSpec((ROW_TILE, D_MODEL), lambda i: (i, 0))]


def _load_x(x_refs):
    if len(x_refs) == 1:
        return x_refs[0][...]
    return jnp.where(pl.program_id(0) < CTX_TILES, x_refs[0][...], x_refs[1][...])


def _mod_spec(layer):
    return pl.BlockSpec((None, 1, 1, 6 * D_MODEL), lambda i: (layer, _mod_row_of_tile(i), 0, 0))


def _rope128(x, cos, sin, half):
    lane = lax.broadcasted_iota(jnp.int32, x.shape, 1)
    first = (lane % (2 * half)) < half
    partner = jnp.where(first, pltpu.roll(x, LANES - half, 1), pltpu.roll(x, half, 1))
    return x * cos + partner * sin


def _rope_cols(x, cos, sin, half):
    chunks = [_rope128(x[:, c:c + LANES], cos, sin, half) for c in range(0, x.shape[1], LANES)]
    return chunks[0] if len(chunks) == 1 else jnp.concatenate(chunks, axis=1)


N_IN_INPUTS = 11


def _in_kernel(*refs, n_x):
    (mod_ref, g_ref, w1_ref, qn_ref, wuq_ref, kvn_ref, wukv_ref,
     cos64_ref, sin64_ref, cos32_ref, sin32_ref) = refs[n_x:n_x + N_IN_INPUTS]
    (qa_o, qd_o, qmn_o, qmr_o, kmn_o, vm_o, kpe4_o, ka_l, va_l, kd_l, vd_l,
     s_ka, s_va, s_kd, s_vd, s_ckv, s_kpe) = refs[-17:]
    i = pl.program_id(0)
    x = _load_x(refs[:n_x])
    m = mod_ref[0]
    sh1, sc1 = m[:, 0:D_MODEL], m[:, D_MODEL:2 * D_MODEL]
    h = _rms(x, g_ref[...]) * (1.0 + sc1) + sh1
    z = _dot(h.astype(BF16), w1_ref[...])
    cos64, sin64 = cos64_ref[...], sin64_ref[...]
    cos32, sin32 = cos32_ref[...], sin32_ref[...]
    qa_o[...] = (_rope_cols(z[:, C_QA:C_KA], cos64, sin64, 16) * SCALE_64).astype(BF16)
    qd_o[...] = (_rope_cols(z[:, C_QD:C_KD], cos64, sin64, 16) * SCALE_64).astype(BF16)
    ka = _rope_cols(z[:, C_KA:C_VA], cos64, sin64, 16)
    va = z[:, C_VA:C_QD]
    kd = _rope_cols(z[:, C_KD:C_VD], cos64, sin64, 16)
    vd = z[:, C_VD:C_QC]
    kpe4 = _rope_cols(z[:, C_KPE:C_END], cos32, sin32, 8)
    kpe4_o[...] = kpe4.astype(BF16)
    qn = _rms(z[:, C_QC:C_KVC], qn_ref[...])
    qm = _dot(qn.astype(BF16), wuq_ref[...]) * SCALE_96
    qmn_o[...] = qm[:, 0:512].astype(BF16)
    qmr_o[...] = _rope_cols(qm[:, 512:768], cos32, sin32, 8).astype(BF16)
    ckv = _rms(z[:, C_KVC:C_KPE], kvn_ref[...])
    kv = _dot(ckv.astype(BF16), wukv_ref[...])
    kmn_o[...] = kv[:, 0:512].astype(BF16)
    vm_o[...] = kv[:, 512:1024].astype(BF16)

    @pl.when(i < CTX_TILES)
    def _():
        s_ka[...] = ka
        s_va[...] = va
        s_kd[...] = kd
        s_vd[...] = vd
        s_ckv[...] = ckv
        s_kpe[...] = kpe4[:, 0:32]

    @pl.when(i >= CTX_TILES)
    def _():
        ka_l[...] = ka.astype(BF16)
        va_l[...] = va.astype(BF16)
        kd_l[...] = kd.astype(BF16)
        vd_l[...] = vd.astype(BF16)


STATE_WIDTHS = (128, 128, 512, 512, 128, 32)


def _in_projection(x, mod, wts, tables, layer, prev_states):
    tm = ROW_TILE

    def row(i):
        return (i, 0)

    def table_row(i):
        return (jnp.where(i < CTX_TILES, 0, 1 + (i - CTX_TILES) % (LAT_SEQ // tm)), 0)

    def lat_row(i):
        return (jnp.maximum(i - CTX_TILES, 0), 0)

    def state_block(i):
        return (jnp.minimum(i, CTX_TILES - 1), layer, 0, 0)

    table_spec = pl.BlockSpec((tm, LANES), table_row)
    weights = [wts["g_pre"], wts["w1"], wts["qnorm"], wts["wuq"], wts["kvnorm"], wts["wukv"]]
    all_rows = [(512, BF16)] * 3 + [(256, BF16), (512, BF16), (512, BF16), (128, BF16)]
    lat_rows = [(128, BF16), (128, BF16), (512, BF16), (512, BF16)]
    aliased = [] if prev_states is None else list(prev_states)
    xs = list(x) if isinstance(x, tuple) else [x]
    n_in = len(xs) + N_IN_INPUTS
    n_plain_out = len(all_rows) + len(lat_rows)
    return pl.pallas_call(
        functools.partial(_in_kernel, n_x=len(xs)),
        out_shape=[jax.ShapeDtypeStruct((N_TOK, w), dt) for w, dt in all_rows]
                  + [jax.ShapeDtypeStruct((N_LAT, w), dt) for w, dt in lat_rows]
                  + [jax.ShapeDtypeStruct((N_CTX_BATCH, DEPTH, CTX_SEQ, w), F32) for w in STATE_WIDTHS],
        grid=(N_TOK // tm,),
        in_specs=_x_specs(x) + [_mod_spec(layer)]
                 + [_layer_spec(w, layer, 1) for w in weights]
                 + [table_spec] * 4
                 + [pl.BlockSpec(memory_space=pl.ANY)] * len(aliased),
        out_specs=[pl.BlockSpec((tm, w), row) for w, _ in all_rows]
                  + [pl.BlockSpec((tm, w), lat_row) for w, _ in lat_rows]
                  + [pl.BlockSpec((None, None, CTX_SEQ, w), state_block) for w in STATE_WIDTHS],
        input_output_aliases={n_in + k: n_plain_out + k for k in range(len(aliased))},
        compiler_params=pltpu.CompilerParams(
            dimension_semantics=("arbitrary",), vmem_limit_bytes=VMEM_LIMIT),
        name="in_projection",
    )(*xs, mod, *weights, *tables, *aliased)


def _attend(q, parts, sink=None, sums_in_other_half=False):
    scores = []
    for k, _, mask in parts:
        s = _dot_nt(q, k)
        if mask is not None:
            s = jnp.where(mask, s, NEG)
        scores.append(s)
    m = functools.reduce(jnp.maximum, [jnp.max(s, axis=-1, keepdims=True) for s in scores])
    if sink is not None:
        m = jnp.maximum(m, sink)
    den = None
    out = None
    for s, (_, v, _) in zip(scores, parts):
        e = jnp.exp((s - m).astype(BF16))
        o = _dot(e, v)
        out = o if out is None else out + o
        if not sums_in_other_half:
            d = jnp.sum(e.astype(F32), axis=-1, keepdims=True)
            den = d if den is None else den + d
    if sums_in_other_half:
        den = pltpu.roll(out, LANES // 2, 1)
    if sink is not None:
        den = den + jnp.exp(sink - m)
    return out / den


def _three_mixers(scal_ref, subln_ref, q_blocks, kv_parts, oa_o, od_o, om_o, lam_init):
    qa, qd, qmn, qmr = q_blocks
    rows = qa.shape[0]
    lane = lax.broadcasted_iota(jnp.int32, (rows, LANES), 1)
    low = lane < 64
    zero = jnp.zeros((rows, LANES), BF16)

    def half_values(v, keep_low):
        low_k = lax.broadcasted_iota(jnp.int32, v.shape, 1) < 64
        return jnp.where(low_k if keep_low else ~low_k, v, jnp.ones_like(v))

    va_half = [[half_values(p["va"], kvh == 0) for p in kv_parts] for kvh in range(2)]
    for g in range(4):
        q128 = qa[:, g * LANES:(g + 1) * LANES]
        halves = []
        for kvh in range(2):
            qh = jnp.where(low if kvh == 0 else ~low, q128, zero)
            parts = [(p["ka"], v, p.get("mask_a")) for p, v in zip(kv_parts, va_half[kvh])]
            halves.append(_attend(qh, parts, sink=scal_ref[kvh * 4 + g], sums_in_other_half=True))
        oa_o[:, g * LANES:(g + 1) * LANES] = jnp.where(low, halves[0], halves[1]).astype(BF16)

    lam = scal_ref[8]
    subln = subln_ref[...]
    for h in range(4):
        sl = slice(h * LANES, (h + 1) * LANES)
        q128 = qd[:, sl]
        parts = [(p["kd"][:, sl], p["vd"][:, sl], None) for p in kv_parts]
        o1 = _attend(jnp.where(low, q128, zero), parts)
        o2 = _attend(jnp.where(low, zero, q128), parts)
        od = o1 - lam * o2
        od_o[:, sl] = (_rms(od, subln) * (1.0 - lam_init)).astype(BF16)

    for i in range(4):
        sl = slice(i * LANES, (i + 1) * LANES)
        keys = [jnp.concatenate([p["kmn"][:, sl], p["kpe"]], axis=1) for p in kv_parts]
        halves = []
        for sub in range(2):
            h = 2 * i + sub
            qn = jnp.where(low if sub == 0 else ~low, qmn[:, sl], zero)
            qr128 = qmr[:, (h // 4) * LANES:(h // 4 + 1) * LANES]
            qr = jnp.where((lane // 32) == (h % 4), qr128, zero)
            q256 = jnp.concatenate([qn, qr], axis=1)
            parts = [(k, half_values(p["vm"][:, sl], sub == 0), None)
                     for k, p in zip(keys, kv_parts)]
            halves.append(_attend(q256, parts, sums_in_other_half=True))
        om_o[:, sl] = jnp.where(low, halves[0], halves[1]).astype(BF16)


def _attn_ctx_kernel(scal_ref, subln_ref, qa_ref, qd_ref, qmn_ref, qmr_ref,
                     ka_ref, va_ref, kd_ref, vd_ref, kmn_ref, vm_ref, kpe_ref,
                     oa_o, od_o, om_o, *, lam_init):
    part = dict(ka=ka_ref[...].astype(BF16), va=va_ref[...].astype(BF16),
                kd=kd_ref[...].astype(BF16), vd=vd_ref[...].astype(BF16),
                kmn=kmn_ref[...], vm=vm_ref[...], kpe=kpe_ref[...])
    _three_mixers(scal_ref, subln_ref, (qa_ref[...], qd_ref[...], qmn_ref[...], qmr_ref[...]),
                  [part], oa_o, od_o, om_o, lam_init)


def _attn_lat_kernel(scal_ref, subln_ref, wukv_ref, qa_ref, qd_ref, qmn_ref, qmr_ref,
                     ka_ref, va_ref, kd_ref, vd_ref, kmn_ref, vm_ref, kpe_ref,
                     cka_ref, cva_ref, ckd_ref, cvd_ref, cckv_ref, ckpe_ref,
                     oa_in, od_in, om_in, oa_o, od_o, om_o, *, lam_init):
    del oa_in, od_in, om_in
    n = pl.program_id(1)
    kv_c = _dot(cckv_ref[...].astype(BF16), wukv_ref[...])
    cached = dict(ka=cka_ref[...].astype(BF16), va=cva_ref[...].astype(BF16),
                  kd=ckd_ref[...].astype(BF16), vd=cvd_ref[...].astype(BF16),
                  kmn=kv_c[:, 0:512].astype(BF16), vm=kv_c[:, 512:1024].astype(BF16),
                  kpe=ckpe_ref[...].astype(BF16))
    span = 3 * Q_BLOCK
    start = pl.multiple_of(jnp.clip((n - 1) * Q_BLOCK, 0, LAT_SEQ - span), Q_BLOCK)
    qpos = n * Q_BLOCK + lax.broadcasted_iota(jnp.int32, (Q_BLOCK, span), 0)
    kpos = start + lax.broadcasted_iota(jnp.int32, (Q_BLOCK, span), 1)
    mask_a = jnp.abs(qpos - kpos) <= WINDOW
    new = dict(ka=ka_ref[pl.ds(start, span), :], va=va_ref[pl.ds(start, span), :], mask_a=mask_a,
               kd=kd_ref[...], vd=vd_ref[...], kmn=kmn_ref[...], vm=vm_ref[...], kpe=kpe_ref[...])
    _three_mixers(scal_ref, subln_ref, (qa_ref[...], qd_ref[...], qmn_ref[...], qmr_ref[...]),
                  [cached, new], oa_o, od_o, om_o, lam_init)


_SMEM_SPEC = pl.BlockSpec(memory_space=pltpu.SMEM)


def _attention_ctx(scal, subln, proj, states, layer, lam_init):
    qa, qd, qmn, qmr, kmn, vm, kpe4 = proj[:7]
    s_ka, s_va, s_kd, s_vd = states[:4]
    t = CTX_SEQ

    def blk(a):
        return pl.BlockSpec((t, a.shape[1]), lambda b: (b, 0))

    def sblk(a):
        return pl.BlockSpec((None, None, t, a.shape[-1]), lambda b: (b, layer, 0, 0))

    return pl.pallas_call(
        functools.partial(_attn_ctx_kernel, lam_init=lam_init),
        out_shape=[jax.ShapeDtypeStruct((N_TOK, 512), BF16)] * 3,
        grid=(N_CTX_BATCH,),
        in_specs=[_SMEM_SPEC, _layer_spec(subln, layer, 1)]
                 + [blk(a) for a in (qa, qd, qmn, qmr)]
                 + [sblk(a) for a in (s_ka, s_va, s_kd, s_vd)]
                 + [blk(a) for a in (kmn, vm, kpe4)],
        out_specs=[pl.BlockSpec((t, 512), lambda b: (b, 0))] * 3,
        compiler_params=pltpu.CompilerParams(vmem_limit_bytes=VMEM_LIMIT),
        name="attention_ctx",
    )(scal, subln, qa, qd, qmn, qmr, s_ka, s_va, s_kd, s_vd, kmn, vm, kpe4)


def _attention_lat(scal, subln, wukv, proj, caches, ctx_out, layer, lam_init):
    qa, qd, qmn, qmr, kmn, vm, kpe4, ka_l, va_l, kd_l, vd_l = proj[:11]
    nq = LAT_SEQ // Q_BLOCK
    q_off = N_CTX // Q_BLOCK
    kv_off = N_CTX // LAT_SEQ

    def qblk(a):
        return pl.BlockSpec((Q_BLOCK, a.shape[1]), lambda b, n: (q_off + b * nq + n, 0))

    def kvblk(a):
        return pl.BlockSpec((LAT_SEQ, a.shape[1]), lambda b, n: (b, 0))

    def kvblk_all(a):
        return pl.BlockSpec((LAT_SEQ, a.shape[1]), lambda b, n: (kv_off + b, 0))

    def cblk(a):
        return pl.BlockSpec((None, None, PAST, a.shape[-1]), lambda b, n: (b, layer, 0, 0))

    qs = [qa, qd, qmn, qmr]
    n_in = 3 + 4 + 7 + 6
    out_spec = pl.BlockSpec((Q_BLOCK, 512), lambda b, n: (q_off + b * nq + n, 0))
    return pl.pallas_call(
        functools.partial(_attn_lat_kernel, lam_init=lam_init),
        out_shape=[jax.ShapeDtypeStruct((N_TOK, 512), BF16)] * 3,
        grid=(N_LAT_BATCH, nq),
        in_specs=[_SMEM_SPEC, _layer_spec(subln, layer, 2), _layer_spec(wukv, layer, 2)]
                 + [qblk(a) for a in qs]
                 + [kvblk(a) for a in (ka_l, va_l, kd_l, vd_l)]
                 + [kvblk_all(a) for a in (kmn, vm, kpe4)]
                 + [cblk(a) for a in caches]
                 + [pl.BlockSpec(memory_space=pl.ANY)] * 3,
        out_specs=[out_spec] * 3,
        input_output_aliases={n_in + k: k for k in range(3)},
        compiler_params=pltpu.CompilerParams(vmem_limit_bytes=VMEM_LIMIT),
        name="attention_lat",
    )(scal, subln, wukv, *qs, ka_l, va_l, kd_l, vd_l, kmn, vm, kpe4, *caches, *ctx_out)


def _route(scores, biased):
    per = N_EXPERTS // N_GROUPS
    tm = scores.shape[1]
    sub = lax.broadcasted_iota(jnp.int32, (per, tm), 0).astype(F32)
    groups = [biased[g * per:(g + 1) * per, :] for g in range(N_GROUPS)]
    gscore = []
    for v in groups:
        m1 = jnp.max(v, axis=0, keepdims=True)
        first = jnp.min(jnp.where(v == m1, sub, float(per)), axis=0, keepdims=True)
        m2 = jnp.max(jnp.where(sub == first, -jnp.inf, v), axis=0, keepdims=True)
        gscore.append(m1 + m2)
    vals = []
    for g in range(N_GROUPS):
        rank = jnp.zeros((1, tm), F32)
        for o in range(N_GROUPS):
            if o == g:
                continue
            ahead = (gscore[o] >= gscore[g]) if o < g else (gscore[o] > gscore[g])
            rank = rank + jnp.where(ahead, 1.0, 0.0)
        vals.append(jnp.where(rank < TOPK_GROUPS, groups[g], NEG))
    idx = [sub + float(g * per) for g in range(N_GROUPS)]
    picks, weights = [], []
    for _ in range(TOP_K):
        best = functools.reduce(jnp.maximum, [jnp.max(v, axis=0, keepdims=True) for v in vals])
        cand = functools.reduce(jnp.minimum, [
            jnp.min(jnp.where(v == best, i, float(N_EXPERTS)), axis=0, keepdims=True)
            for v, i in zip(vals, idx)])
        wsel = jnp.zeros((1, tm), F32)
        for g in range(N_GROUPS):
            hit = idx[g] == cand
            wsel = wsel + jnp.sum(jnp.where(hit, scores[g * per:(g + 1) * per, :], 0.0),
                                  axis=0, keepdims=True)
            vals[g] = jnp.where(hit, -jnp.inf, vals[g])
        picks.append(cand)
        weights.append(wsel)
    total = functools.reduce(lambda a, b: a + b, weights)
    pad = [jnp.zeros((1, tm), F32)] * (8 - TOP_K)
    experts = jnp.concatenate(picks + pad, axis=0).astype(jnp.int32)
    gates = jnp.concatenate([x / total * ROUTED_SCALE for x in weights] + pad, axis=0)
    return experts, gates


def _post_kernel(*refs, n_x):
    (mod_ref, oa_ref, od_ref, om_ref, gpre_ref, wgl_ref, wa_ref, wb_ref, wc_ref, wo_ref, gpost_ref,
     gffn_ref, rwt_ref, rb_ref, xmid_o, h2_o, h2t_o, experts_o, gates_o) = refs[n_x:]
    x = _load_x(refs[:n_x])
    m = mod_ref[0]
    d = D_MODEL
    sh1, sc1, g1, sh2, sc2 = (m[:, 0:d], m[:, d:2 * d], m[:, 2 * d:3 * d], m[:, 3 * d:4 * d],
                              m[:, 4 * d:5 * d])
    h = _rms(x, gpre_ref[...]) * (1.0 + sc1) + sh1
    gate = jax.nn.sigmoid(_dot(h.astype(BF16), wgl_ref[...]))
    merged = (gate[:, 0:d] * _dot(oa_ref[...], wa_ref[...])
              + gate[:, d:2 * d] * _dot(od_ref[...], wb_ref[...])
              + gate[:, 2 * d:3 * d] * _dot(om_ref[...], wc_ref[...]))
    a = _dot(merged.astype(BF16), wo_ref[...])
    xm = x + g1 * _rms(a, gpost_ref[...])
    xmid_o[...] = xm
    h2f = _rms(xm, gffn_ref[...]) * (1.0 + sc2) + sh2
    h2 = h2f.astype(BF16)
    h2_o[...] = h2
    for c in range(ACC_ROWS):
        h2t_o[pl.ds(c, x.shape[0], stride=ACC_ROWS), :] = h2f[:, c * LANES:(c + 1) * LANES]
    scores = jax.nn.sigmoid(_dot_nt(rwt_ref[...], h2))
    experts, gates = _route(scores, scores + rb_ref[...])
    experts_o[...] = experts
    gates_o[...] = gates


def _post_attention(x, mod, oa, od, om, wts, layer):
    tm = ROW_TILE

    def row(i):
        return (i, 0)

    weights = [wts[k] for k in ("g_pre", "wgl", "wa", "wb", "wc", "wo", "g_post", "g_ffn", "rwt", "rb")]
    xs = list(x) if isinstance(x, tuple) else [x]
    return pl.pallas_call(
        functools.partial(_post_kernel, n_x=len(xs)),
        out_shape=[jax.ShapeDtypeStruct((N_TOK, D_MODEL), F32),
                   jax.ShapeDtypeStruct((N_TOK, D_MODEL), BF16),
                   jax.ShapeDtypeStruct((N_TOK * ACC_ROWS, LANES), F32),
                   jax.ShapeDtypeStruct((8, N_TOK), jnp.int32),
                   jax.ShapeDtypeStruct((8, N_TOK), F32)],
        grid=(N_TOK // tm,),
        in_specs=_x_specs(x) + [_mod_spec(layer),
                  pl.BlockSpec((tm, 512), row), pl.BlockSpec((tm, 512), row),
                  pl.BlockSpec((tm, 512), row)]
                 + [_layer_spec(w, layer, 1) for w in weights],
        out_specs=[pl.BlockSpec((tm, D_MODEL), row), pl.BlockSpec((tm, D_MODEL), row),
                   pl.BlockSpec((tm * ACC_ROWS, LANES), row),
                   pl.BlockSpec((8, tm), lambda i: (0, i)), pl.BlockSpec((8, tm), lambda i: (0, i))],
        compiler_params=pltpu.CompilerParams(vmem_limit_bytes=VMEM_LIMIT),
        name="post_attention",
    )(*xs, mod, oa, od, om, *weights)


def _ffn(x, wg, wu):
    return _silu(_dot(x, wg)) * _dot(x, wu)


def _moe_kernel(start_ref, count_ref, dst_ref, gate_ref, xs_ref, wg_ref, wu_ref, wd_ref,
                acc_ref, xa, xb, ya, yb, wgb, wub, wdb, pend):
    e = pl.program_id(0)

    @pl.when(e == 0)
    def _():
        def clear(i, carry):
            r0 = pl.multiple_of(i * MOE_ZERO_ROWS, MOE_ZERO_ROWS)
            acc_ref[pl.ds(r0, MOE_ZERO_ROWS), :] = jnp.zeros((MOE_ZERO_ROWS, LANES), F32)
            return carry

        lax.fori_loop(0, acc_ref.shape[0] // MOE_ZERO_ROWS, clear, 0)

        yb[...] = jnp.zeros_like(yb)
        pend[0] = MOE_LIST_PAD_BASE

    wgb[...] = wg_ref[0].astype(BF16)
    wub[...] = wu_ref[0].astype(BF16)
    wdb[...] = wd_ref[0].astype(BF16)
    start = start_ref[e]
    pairs = (count_ref[e] + 2 * MOE_SUB - 1) // (2 * MOE_SUB)

    def gather(base, xbuf):
        for mi in range(MOE_SUB):
            src = pl.multiple_of(jnp.minimum(dst_ref[base + mi], (N_TOK - 1) * ACC_ROWS), ACC_ROWS)
            xbuf[mi * ACC_ROWS:(mi + 1) * ACC_ROWS, :] = xs_ref[pl.ds(src, ACC_ROWS), :]

    def expert_ffn(xbuf, ybuf):
        x = jnp.concatenate([xbuf[pl.ds(c, MOE_SUB, stride=ACC_ROWS), :].astype(BF16)
                             for c in range(ACC_ROWS)], axis=1)
        hid = _ffn(x, wgb[...], wub[...])
        y = _dot(hid.astype(BF16), wdb[...])
        for c in range(ACC_ROWS):
            ybuf[pl.ds(c, MOE_SUB, stride=ACC_ROWS), :] = y[:, c * LANES:(c + 1) * LANES]

    def scatter(base, ybuf):
        for g0 in range(0, MOE_SUB, MOE_RMW_GROUP):
            updates = []
            for mi in range(g0, g0 + MOE_RMW_GROUP):
                dst = pl.multiple_of(dst_ref[base + mi], ACC_ROWS)
                yv = ybuf[mi * ACC_ROWS:(mi + 1) * ACC_ROWS, :]
                updates.append((dst, acc_ref[pl.ds(dst, ACC_ROWS), :] + gate_ref[base + mi] * yv))
            for dst, val in updates:
                acc_ref[pl.ds(dst, ACC_ROWS), :] = val

    @pl.when(pairs > 0)
    def _():
        gather(start, xa)

    def pair(p, carry):
        base = start + p * (2 * MOE_SUB)
        gather(base + MOE_SUB, xb)
        expert_ffn(xa, ya)
        scatter(pend[0], yb)
        gather(base + 2 * MOE_SUB, xa)
        expert_ffn(xb, yb)
        scatter(base, ya)
        pend[0] = base + MOE_SUB
        return carry

    lax.fori_loop(0, pairs, pair, 0)

    @pl.when(e == N_EXPERTS - 1)
    def _():
        scatter(pend[0], yb)


def _moe(lists, h2t, wg, wu, wd, layer):
    def expert(e, *_):
        return (layer, e, 0, 0)

    def whole(e, *_):
        return (0, 0)

    acc_rows = (N_TOK + MOE_SPARE_TOKENS) * ACC_ROWS
    assert acc_rows % MOE_ZERO_ROWS == 0
    return pl.pallas_call(
        _moe_kernel,
        out_shape=jax.ShapeDtypeStruct((acc_rows, LANES), F32),
        grid_spec=pltpu.PrefetchScalarGridSpec(
            num_scalar_prefetch=len(lists),
            grid=(N_EXPERTS,),
            in_specs=[pl.BlockSpec(h2t.shape, whole, pipeline_mode=pl.Buffered(1)),
                      pl.BlockSpec((None, 1, D_MODEL, F_EXPERT), expert),
                      pl.BlockSpec((None, 1, D_MODEL, F_EXPERT), expert),
                      pl.BlockSpec((None, 1, F_EXPERT, D_MODEL), expert)],
            out_specs=pl.BlockSpec((acc_rows, LANES), whole, pipeline_mode=pl.Buffered(1)),
            scratch_shapes=[pltpu.VMEM((MOE_SUB * ACC_ROWS, LANES), F32)] * 4
                           + [pltpu.VMEM((D_MODEL, F_EXPERT), BF16),
                              pltpu.VMEM((D_MODEL, F_EXPERT), BF16),
                              pltpu.VMEM((F_EXPERT, D_MODEL), BF16),
                              pltpu.SMEM((1,), jnp.int32)]),
        compiler_params=pltpu.CompilerParams(
            dimension_semantics=("arbitrary",), vmem_limit_bytes=MOE_VMEM_LIMIT),
        name="moe",
    )(*lists, h2t, wg, wu, wd)


def _final_kernel(acc_ref, h2_ref, xmid_ref, mod_ref, swg_ref, swu_ref, swd_ref, gpost_ref, *out_refs):
    tm = h2_ref.shape[0]
    routed = jnp.concatenate(
        [acc_ref[pl.ds(c, tm, stride=ACC_ROWS), :] for c in range(ACC_ROWS)], axis=1)
    hid = _ffn(h2_ref[...], swg_ref[...], swu_ref[...])
    shared = _dot(hid.astype(BF16), swd_ref[...])
    g2 = mod_ref[0][:, 5 * D_MODEL:6 * D_MODEL]
    out = xmid_ref[...] + g2 * _rms(routed + shared, gpost_ref[...])
    if len(out_refs) == 1:
        out_refs[0][...] = out
    else:
        @pl.when(pl.program_id(0) < CTX_TILES)
        def _():
            out_refs[0][...] = out

        @pl.when(pl.program_id(0) >= CTX_TILES)
        def _():
            out_refs[1][...] = out


def _final(acc, h2, xmid, mod, wts, layer, split):
    tm = ROW_TILE

    def row(i):
        return (i, 0)

    weights = [wts[k] for k in ("swg", "swu", "swd", "g_ffn_post")]
    return pl.pallas_call(
        _final_kernel,
        out_shape=([jax.ShapeDtypeStruct((N_CTX, D_MODEL), F32),
                    jax.ShapeDtypeStruct((N_LAT, D_MODEL), F32)] if split
                   else jax.ShapeDtypeStruct((N_TOK, D_MODEL), F32)),
        grid=(N_TOK // tm,),
        in_specs=[pl.BlockSpec((tm * ACC_ROWS, LANES), row),
                  pl.BlockSpec((tm, D_MODEL), row),
                  pl.BlockSpec((tm, D_MODEL), row),
                  _mod_spec(layer)]
                 + [_layer_spec(w, layer, 1) for w in weights],
        out_specs=_x_specs((None, None)) if split else pl.BlockSpec((tm, D_MODEL), row),
        compiler_params=pltpu.CompilerParams(
            dimension_semantics=("arbitrary",), vmem_limit_bytes=VMEM_LIMIT),
        name="ffn_final",
    )(acc, h2, xmid, mod, *weights)


def _routing_lists(experts, gates):
    e = experts[:TOP_K].reshape(-1)
    tok = jnp.tile(jnp.arange(N_TOK, dtype=jnp.int32), TOP_K)
    pad_e = jnp.repeat(jnp.arange(N_EXPERTS, dtype=jnp.int32), MOE_PAD)
    pad_tok = N_TOK + jnp.tile(jnp.arange(MOE_PAD, dtype=jnp.int32), N_EXPERTS)
    keys = jnp.concatenate([e * TOK_KEY + tok, pad_e * TOK_KEY + pad_tok])
    vals = jnp.concatenate([gates[:TOP_K].reshape(-1), jnp.zeros((N_EXPERTS * MOE_PAD,), F32)])
    keys, gate_sorted = lax.sort((keys, vals), num_keys=1)
    tail = jnp.full((MOE_SUB,), N_TOK, jnp.int32)
    tok_sorted = jnp.concatenate([keys & (TOK_KEY - 1), tail])
    gate_sorted = jnp.concatenate([gate_sorted, jnp.zeros((MOE_SUB,), F32)])
    dst = jnp.minimum(tok_sorted, N_TOK) * ACC_ROWS
    counts = jnp.sum((e[None, :] == jnp.arange(N_EXPERTS, dtype=jnp.int32)[:, None]).astype(jnp.int32),
                     axis=1)
    start = jnp.cumsum(counts) - counts + MOE_PAD * jnp.arange(N_EXPERTS, dtype=jnp.int32)
    return start.astype(jnp.int32), counts, dst, gate_sorted


def _rope_tables():
    t = np.arange(LAT_SEQ)
    pos = np.stack([t // GRID_W, t % GRID_W], axis=1).astype(np.float64)

    def table(half):
        inv = ROPE_BASE ** (-np.arange(half, dtype=np.float64) / half)
        lane = np.arange(LANES)
        axis = (lane // (2 * half)) % 2
        freq = inv[lane % half]
        ang = pos[:, axis] * freq[None, :]
        sign = np.where((lane % (2 * half)) < half, -1.0, 1.0)
        cos = np.concatenate([np.ones((ROW_TILE, LANES)), np.cos(ang)], axis=0)
        sin = np.concatenate([np.zeros((ROW_TILE, LANES)), np.sin(ang) * sign[None, :]], axis=0)
        return jnp.asarray(cos, F32), jnp.asarray(sin, F32)

    cos64, sin64 = table(16)
    cos32, sin32 = table(8)
    return cos64, sin64, cos32, sin32


def _prepare_weights(norm_attn_pre, norm_attn_post, norm_ffn_pre, norm_ffn_post, w_in, dif_subln,
                     mla_q_norm, mla_w_uq, mla_kv_norm, mla_w_ukv, w_branch_a, w_branch_b,
                     w_branch_c, w_out, router_w, router_bias, shared_w_gate, shared_w_up,
                     shared_w_down):
    dp = DEPTH
    qa = w_in[:, :, 0:512].reshape(dp, D_MODEL, 2, 4, 64).transpose(0, 1, 3, 2, 4).reshape(dp, D_MODEL, 512)
    kpe = w_in[:, :, 2688:2720]
    w1 = jnp.concatenate([qa, w_in[:, :, 512:2688], kpe, kpe, kpe, kpe], axis=2).astype(BF16)
    uq = mla_w_uq.reshape(dp, 256, 8, 96)
    wuq = jnp.concatenate([uq[..., :64].reshape(dp, 256, 512), uq[..., 64:].reshape(dp, 256, 256)],
                          axis=2).astype(BF16)
    ukv = mla_w_ukv.reshape(dp, 128, 8, 128)
    wukv = jnp.concatenate([ukv[..., :64].reshape(dp, 128, 512), ukv[..., 64:].reshape(dp, 128, 512)],
                           axis=2).astype(BF16)
    wa = w_branch_a.reshape(dp, 2, 4, 64, D_MODEL).transpose(0, 2, 1, 3, 4).reshape(dp, 512, D_MODEL)
    return dict(
        g_pre=norm_attn_pre[:, None, :], g_post=norm_attn_post[:, None, :],
        g_ffn=norm_ffn_pre[:, None, :], g_ffn_post=norm_ffn_post[:, None, :],
        w1=w1, wgl=w_in[:, :, 2720:].astype(BF16),
        qnorm=mla_q_norm[:, None, :], wuq=wuq, kvnorm=mla_kv_norm[:, None, :], wukv=wukv,
        subln=dif_subln[:, None, :],
        wa=wa.astype(BF16), wb=w_branch_b.astype(BF16), wc=w_branch_c.astype(BF16),
        wo=w_out.astype(BF16), rwt=router_w.transpose(0, 2, 1).astype(BF16),
        rb=router_bias[:, :, None],
        swg=shared_w_gate.astype(BF16), swu=shared_w_up.astype(BF16), swd=shared_w_down.astype(BF16))


def kernel(x_prompt, x_sample, cache_swa_k, cache_swa_v, cache_dif_k, cache_dif_v, cache_mla_ckv, cache_mla_kpe, c, c_ctx, w_mod, b_mod, norm_attn_pre, norm_attn_post, norm_ffn_pre, norm_ffn_post, w_in, swa_sink, dif_lq1, dif_lk1, dif_lq2, dif_lk2, dif_subln, mla_q_norm, mla_w_uq, mla_kv_norm, mla_w_ukv, w_branch_a, w_branch_b, w_branch_c, w_out, router_w, router_bias, moe_w_gate, moe_w_up, moe_w_down, shared_w_gate, shared_w_up, shared_w_down):
    x = (x_prompt.reshape(N_CTX, D_MODEL), x_sample.reshape(N_LAT, D_MODEL))
    cvec = jnp.concatenate([c_ctx[None, :], c, jnp.zeros((8 - 1 - N_LAT_BATCH, D_MODEL), F32)], axis=0)
    mod = _modulation(cvec, w_mod, b_mod).reshape(DEPTH, 8, 1, 6 * D_MODEL)
    tables = _rope_tables()
    wts = _prepare_weights(norm_attn_pre, norm_attn_post, norm_ffn_pre, norm_ffn_post, w_in,
                           dif_subln, mla_q_norm, mla_w_uq, mla_kv_norm, mla_w_ukv, w_branch_a,
                           w_branch_b, w_branch_c, w_out, router_w, router_bias, shared_w_gate,
                           shared_w_up, shared_w_down)
    caches = [cache_swa_k.reshape(N_LAT_BATCH, DEPTH, PAST, 128),
              cache_swa_v.reshape(N_LAT_BATCH, DEPTH, PAST, 128),
              cache_dif_k.reshape(N_LAT_BATCH, DEPTH, PAST, 512),
              cache_dif_v.reshape(N_LAT_BATCH, DEPTH, PAST, 512),
              cache_mla_ckv,
              jnp.tile(cache_mla_kpe, (1, 1, 1, 4))]
    lam_init = [0.8 - 0.6 * math.exp(-0.3 * l) for l in range(DEPTH)]
    lam = (jnp.exp(jnp.sum(dif_lq1 * dif_lk1, axis=1)) - jnp.exp(jnp.sum(dif_lq2 * dif_lk2, axis=1))
           + jnp.asarray(lam_init, F32))
    scal = jnp.concatenate([swa_sink, lam[:, None]], axis=1).astype(F32)

    states = None
    for l in range(DEPTH):
        proj = _in_projection(x, mod, wts, tables, l, states)
        states = proj[11:]
        ctx_out = _attention_ctx(scal[l], wts["subln"], proj, states, l, lam_init[l])
        oa, od, om = _attention_lat(scal[l], wts["subln"], wts["wukv"], proj, caches, ctx_out, l,
                                    lam_init[l])
        xmid, h2, h2t, experts, gates = _post_attention(x, mod, oa, od, om, wts, l)
        lists = _routing_lists(experts, gates)
        acc = _moe(lists, h2t, moe_w_gate, moe_w_up, moe_w_down, l)
        x = _final(acc, h2, xmid, mod, wts, l, split=(l == DEPTH - 1))

    y_p = x[0].reshape(N_CTX_BATCH, CTX_SEQ, D_MODEL)
    y_s = x[1].reshape(N_LAT_BATCH, LAT_SEQ, D_MODEL)
    s_ka, s_va, s_kd, s_vd, s_ckv, s_kpe = states
    return (y_p, y_s,
            s_ka.reshape(N_CTX_BATCH, DEPTH, CTX_SEQ, 2, 64),
            s_va.reshape(N_CTX_BATCH, DEPTH, CTX_SEQ, 2, 64),
            s_kd.reshape(N_CTX_BATCH, DEPTH, CTX_SEQ, 4, 2, 64),
            s_vd.reshape(N_CTX_BATCH, DEPTH, CTX_SEQ, 4, 128),
            s_ckv, s_kpe)
```

```python
import functools
import math

import numpy as np
import jax
import jax.numpy as jnp
from jax import lax
from jax.experimental import pallas as pl
from jax.experimental.pallas import tpu as pltpu

F32 = jnp.float32
BF16 = jnp.bfloat16

D_MODEL = 1024
N_CTX_BATCH, CTX_SEQ = 16, 256
N_LAT_BATCH, LAT_SEQ = 2, 1024
PAST = 256
N_CTX = N_CTX_BATCH * CTX_SEQ
N_LAT = N_LAT_BATCH * LAT_SEQ
N_TOK = N_CTX + N_LAT
DEPTH = 2
GRID_W = 64
WINDOW = 128
N_EXPERTS = 64
N_GROUPS = 8
TOPK_GROUPS = 4
TOP_K = 6
F_EXPERT = 256
ROUTED_SCALE = 2.5
ROPE_BASE = 10000.0
EPS = 1e-6
NEG = -1e30

LANES = 128
ROW_TILE = 256
Q_BLOCK = 128
VMEM_LIMIT = 56 * 1024 * 1024
ACC_ROWS = 8
MOE_SPARE_TOKENS = 8
MOE_ZERO_ROWS = 64
MOE_VMEM_LIMIT = 62 * 1024 * 1024
MOE_SUB = 128
MOE_RMW_GROUP = 8
MOE_PAD = 2 * MOE_SUB
TOK_KEY = 8192
MOE_LIST_PAD_BASE = N_TOK * TOP_K + N_EXPERTS * MOE_PAD

C_QA, C_KA, C_VA, C_QD, C_KD, C_VD, C_QC, C_KVC, C_KPE, C_END = (
    0, 512, 640, 768, 1280, 1792, 2304, 2560, 2688, 2816)

SCALE_64 = 1.0 / math.sqrt(64.0)
SCALE_96 = 1.0 / math.sqrt(96.0)

CTX_TILES = N_CTX // ROW_TILE
assert ROW_TILE == CTX_SEQ and LAT_SEQ % ROW_TILE == 0


def _mod_row_of_tile(i):
    return jnp.where(i < CTX_TILES, 0, 1 + (i - CTX_TILES) // (LAT_SEQ // ROW_TILE))


def _rms(x, g):
    return x * lax.rsqrt(jnp.mean(x * x, axis=-1, keepdims=True) + EPS) * g


def _dot(a, b):
    return jnp.dot(a, b, preferred_element_type=F32)


def _dot_nt(a, b):
    return lax.dot_general(a, b, (((1,), (1,)), ((), ())), preferred_element_type=F32)


def _silu(x):
    return x * jax.nn.sigmoid(x)


def _layer_spec(arr, layer, grid_rank):
    zeros = (0,) * (arr.ndim - 1)
    return pl.BlockSpec((None,) + arr.shape[1:], lambda *_: (layer,) + zeros)


def _mod_kernel(c_ref, w_ref, b_ref, o_ref):
    c = c_ref[...]
    o_ref[0] = _dot(_silu(c).astype(BF16), w_ref[0].astype(BF16)) + b_ref[0]


def _modulation(cvec, w_mod, b_mod):
    tn = 1536
    n = w_mod.shape[-1]
    return pl.pallas_call(
        _mod_kernel,
        out_shape=jax.ShapeDtypeStruct((DEPTH, 8, n), F32),
        grid=(DEPTH, n // tn),
        in_specs=[
            pl.BlockSpec((8, D_MODEL), lambda l, j: (0, 0)),
            pl.BlockSpec((1, D_MODEL, tn), lambda l, j: (l, 0, j)),
            pl.BlockSpec((1, 1, tn), lambda l, j: (l, 0, j)),
        ],
        out_specs=pl.BlockSpec((1, 8, tn), lambda l, j: (l, 0, j)),
        compiler_params=pltpu.CompilerParams(vmem_limit_bytes=VMEM_LIMIT),
        name="modulation",
    )(cvec, w_mod, b_mod.reshape(DEPTH, 1, n))


def _x_specs(x):
    if isinstance(x, tuple):
        return [pl.BlockSpec((ROW_TILE, D_MODEL), lambda i: (jnp.minimum(i, CTX_TILES - 1), 0)),
                pl.BlockSpec((ROW_TILE, D_MODEL), lambda i: (jnp.maximum(i - CTX_TILES, 0), 0))]
    return [pl.BlockSpec((ROW_TILE, D_MODEL), lambda i: (i, 0))]


def _load_x(x_refs):
    if len(x_refs) == 1:
        return x_refs[0][...]
    return jnp.where(pl.program_id(0) < CTX_TILES, x_refs[0][...], x_refs[1][...])


def _mod_spec(layer):
    return pl.BlockSpec((None, 1, 1, 6 * D_MODEL), lambda i: (layer, _mod_row_of_tile(i), 0, 0))


def _rope128(x, cos, sin, half):
    lane = lax.broadcasted_iota(jnp.int32, x.shape, 1)
    first = (lane % (2 * half)) < half
    partner = jnp.where(first, pltpu.roll(x, LANES - half, 1), pltpu.roll(x, half, 1))
    return x * cos + partner * sin


def _rope_cols(x, cos, sin, half):
    chunks = [_rope128(x[:, c:c + LANES], cos, sin, half) for c in range(0, x.shape[1], LANES)]
    return chunks[0] if len(chunks) == 1 else jnp.concatenate(chunks, axis=1)


N_IN_INPUTS = 11


def _in_kernel(*refs, n_x):
    (mod_ref, g_ref, w1_ref, qn_ref, wuq_ref, kvn_ref, wukv_ref,
     cos64_ref, sin64_ref, cos32_ref, sin32_ref) = refs[n_x:n_x + N_IN_INPUTS]
    (qa_o, qd_o, qmn_o, qmr_o, kmn_o, vm_o, kpe4_o, ka_l, va_l, kd_l, vd_l,
     s_ka, s_va, s_kd, s_vd, s_ckv, s_kpe) = refs[-17:]
    i = pl.program_id(0)
    x = _load_x(refs[:n_x])
    m = mod_ref[0]
    sh1, sc1 = m[:, 0:D_MODEL], m[:, D_MODEL:2 * D_MODEL]
    h = _rms(x, g_ref[...]) * (1.0 + sc1) + sh1
    z = _dot(h.astype(BF16), w1_ref[...])
    cos64, sin64 = cos64_ref[...], sin64_ref[...]
    cos32, sin32 = cos32_ref[...], sin32_ref[...]
    qa_o[...] = (_rope_cols(z[:, C_QA:C_KA], cos64, sin64, 16) * SCALE_64).astype(BF16)
    qd_o[...] = (_rope_cols(z[:, C_QD:C_KD], cos64, sin64, 16) * SCALE_64).astype(BF16)
    ka = _rope_cols(z[:, C_KA:C_VA], cos64, sin64, 16)
    va = z[:, C_VA:C_QD]
    kd = _rope_cols(z[:, C_KD:C_VD], cos64, sin64, 16)
    vd = z[:, C_VD:C_QC]
    kpe4 = _rope_cols(z[:, C_KPE:C_END], cos32, sin32, 8)
    kpe4_o[...] = kpe4.astype(BF16)
    qn = _rms(z[:, C_QC:C_KVC], qn_ref[...])
    qm = _dot(qn.astype(BF16), wuq_ref[...]) * SCALE_96
    qmn_o[...] = qm[:, 0:512].astype(BF16)
    qmr_o[...] = _rope_cols(qm[:, 512:768], cos32, sin32, 8).astype(BF16)
    ckv = _rms(z[:, C_KVC:C_KPE], kvn_ref[...])
    kv = _dot(ckv.astype(BF16), wukv_ref[...])
    kmn_o[...] = kv[:, 0:512].astype(BF16)
    vm_o[...] = kv[:, 512:1024].astype(BF16)

    @pl.when(i < CTX_TILES)
    def _():
        s_ka[...] = ka
        s_va[...] = va
        s_kd[...] = kd
        s_vd[...] = vd
        s_ckv[...] = ckv
        s_kpe[...] = kpe4[:, 0:32]

    @pl.when(i >= CTX_TILES)
    def _():
        ka_l[...] = ka.astype(BF16)
        va_l[...] = va.astype(BF16)
        kd_l[...] = kd.astype(BF16)
        vd_l[...] = vd.astype(BF16)


STATE_WIDTHS = (128, 128, 512, 512, 128, 32)


def _in_projection(x, mod, wts, tables, layer, prev_states):
    tm = ROW_TILE

    def row(i):
        return (i, 0)

    def table_row(i):
        return (jnp.where(i < CTX_TILES, 0, 1 + (i - CTX_TILES) % (LAT_SEQ // tm)), 0)

    def lat_row(i):
        return (jnp.maximum(i - CTX_TILES, 0), 0)

    def state_block(i):
        return (jnp.minimum(i, CTX_TILES - 1), layer, 0, 0)

    table_spec = pl.BlockSpec((tm, LANES), table_row)
    weights = [wts["g_pre"], wts["w1"], wts["qnorm"], wts["wuq"], wts["kvnorm"], wts["wukv"]]
    all_rows = [(512, BF16)] * 3 + [(256, BF16), (512, BF16), (512, BF16), (128, BF16)]
    lat_rows = [(128, BF16), (128, BF16), (512, BF16), (512, BF16)]
    aliased = [] if prev_states is None else list(prev_states)
    xs = list(x) if isinstance(x, tuple) else [x]
    n_in = len(xs) + N_IN_INPUTS
    n_plain_out = len(all_rows) + len(lat_rows)
    return pl.pallas_call(
        functools.partial(_in_kernel, n_x=len(xs)),
        out_shape=[jax.ShapeDtypeStruct((N_TOK, w), dt) for w, dt in all_rows]
                  + [jax.ShapeDtypeStruct((N_LAT, w), dt) for w, dt in lat_rows]
                  + [jax.ShapeDtypeStruct((N_CTX_BATCH, DEPTH, CTX_SEQ, w), F32) for w in STATE_WIDTHS],
        grid=(N_TOK // tm,),
        in_specs=_x_specs(x) + [_mod_spec(layer)]
                 + [_layer_spec(w, layer, 1) for w in weights]
                 + [table_spec] * 4
                 + [pl.BlockSpec(memory_space=pl.ANY)] * len(aliased),
        out_specs=[pl.BlockSpec((tm, w), row) for w, _ in all_rows]
                  + [pl.BlockSpec((tm, w), lat_row) for w, _ in lat_rows]
                  + [pl.BlockSpec((None, None, CTX_SEQ, w), state_block) for w in STATE_WIDTHS],
        input_output_aliases={n_in + k: n_plain_out + k for k in range(len(aliased))},
        compiler_params=pltpu.CompilerParams(
            dimension_semantics=("arbitrary",), vmem_limit_bytes=VMEM_LIMIT),
        name="in_projection",
    )(*xs, mod, *weights, *tables, *aliased)


def _attend(q, parts, sink=None):
    scores = []
    for k, _, mask in parts:
        s = _dot_nt(q, k)
        if mask is not None:
            s = jnp.where(mask, s, NEG)
        scores.append(s)
    m = functools.reduce(jnp.maximum, [jnp.max(s, axis=-1, keepdims=True) for s in scores])
    if sink is not None:
        m = jnp.maximum(m, sink)
    den = None
    out = None
    for s, (_, v, _) in zip(scores, parts):
        e = jnp.exp(s - m)
        d = jnp.sum(e, axis=-1, keepdims=True)
        o = _dot(e.astype(BF16), v)
        den = d if den is None else den + d
        out = o if out is None else out + o
    if sink is not None:
        den = den + jnp.exp(sink - m)
    return out / den


A_STACK = 4


def _three_mixers(scal_ref, subln_ref, q_blocks, kv_parts, oa_o, od_o, om_o, lam_init):
    qa, qd, qmn, qmr = q_blocks
    rows = qa.shape[0]
    lane = lax.broadcasted_iota(jnp.int32, (rows, LANES), 1)
    low = lane < 64
    zero = jnp.zeros((rows, LANES), BF16)

    parts = [(p["ka"], p["va"], p.get("mask_a")) for p in kv_parts]
    out_a = []
    for kvh in range(2):
        half = low if kvh == 0 else ~low
        q = jnp.concatenate([jnp.where(half, qa[:, g * LANES:(g + 1) * LANES], zero)
                             for g in range(A_STACK)], axis=0)
        sink = jnp.concatenate([jnp.full((rows, 1), scal_ref[kvh * A_STACK + g], F32)
                                for g in range(A_STACK)], axis=0)
        out_a.append(_attend(q, parts, sink))
    for g in range(A_STACK):
        rs = slice(g * rows, (g + 1) * rows)
        oa_o[:, g * LANES:(g + 1) * LANES] = jnp.where(low, out_a[0][rs], out_a[1][rs]).astype(BF16)

    lam = scal_ref[8]
    subln = subln_ref[...]
    for h in range(4):
        sl = slice(h * LANES, (h + 1) * LANES)
        q128 = qd[:, sl]
        q = jnp.concatenate([jnp.where(low, q128, zero), jnp.where(low, zero, q128)], axis=0)
        o = _attend(q, [(p["kd"][:, sl], p["vd"][:, sl], None) for p in kv_parts])
        od = o[:rows] - lam * o[rows:]
        od_o[:, sl] = (_rms(od, subln) * (1.0 - lam_init)).astype(BF16)

    for i in range(4):
        sl = slice(i * LANES, (i + 1) * LANES)
        qs = []
        for sub in range(2):
            h = 2 * i + sub
            qn = jnp.where(low if sub == 0 else ~low, qmn[:, sl], zero)
            qr128 = qmr[:, (h // 4) * LANES:(h // 4 + 1) * LANES]
            qr = jnp.where((lane // 32) == (h % 4), qr128, zero)
            qs.append(jnp.concatenate([qn, qr], axis=1))
        parts = [(jnp.concatenate([p["kmn"][:, sl], p["kpe"]], axis=1), p["vm"][:, sl], None)
                 for p in kv_parts]
        o = _attend(jnp.concatenate(qs, axis=0), parts)
        om_o[:, sl] = jnp.where(low, o[:rows], o[rows:]).astype(BF16)


def _attn_ctx_kernel(scal_ref, subln_ref, qa_ref, qd_ref, qmn_ref, qmr_ref,
                     ka_ref, va_ref, kd_ref, vd_ref, kmn_ref, vm_ref, kpe_ref,
                     oa_o, od_o, om_o, *, lam_init):
    part = dict(ka=ka_ref[...].astype(BF16), va=va_ref[...].astype(BF16),
                kd=kd_ref[...].astype(BF16), vd=vd_ref[...].astype(BF16),
                kmn=kmn_ref[...], vm=vm_ref[...], kpe=kpe_ref[...])
    _three_mixers(scal_ref, subln_ref, (qa_ref[...], qd_ref[...], qmn_ref[...], qmr_ref[...]),
                  [part], oa_o, od_o, om_o, lam_init)


def _attn_lat_kernel(scal_ref, subln_ref, wukv_ref, qa_ref, qd_ref, qmn_ref, qmr_ref,
                     ka_ref, va_ref, kd_ref, vd_ref, kmn_ref, vm_ref, kpe_ref,
                     cka_ref, cva_ref, ckd_ref, cvd_ref, cckv_ref, ckpe_ref,
                     oa_in, od_in, om_in, oa_o, od_o, om_o, *, lam_init):
    del oa_in, od_in, om_in
    n = pl.program_id(1)
    kv_c = _dot(cckv_ref[...].astype(BF16), wukv_ref[...])
    cached = dict(ka=cka_ref[...].astype(BF16), va=cva_ref[...].astype(BF16),
                  kd=ckd_ref[...].astype(BF16), vd=cvd_ref[...].astype(BF16),
                  kmn=kv_c[:, 0:512].astype(BF16), vm=kv_c[:, 512:1024].astype(BF16),
                  kpe=ckpe_ref[...].astype(BF16))
    span = 3 * Q_BLOCK
    start = pl.multiple_of(jnp.clip((n - 1) * Q_BLOCK, 0, LAT_SEQ - span), Q_BLOCK)
    qpos = n * Q_BLOCK + lax.broadcasted_iota(jnp.int32, (A_STACK * Q_BLOCK, span), 0) % Q_BLOCK
    kpos = start + lax.broadcasted_iota(jnp.int32, (A_STACK * Q_BLOCK, span), 1)
    mask_a = jnp.abs(qpos - kpos) <= WINDOW
    new = dict(ka=ka_ref[pl.ds(start, span), :], va=va_ref[pl.ds(start, span), :], mask_a=mask_a,
               kd=kd_ref[...], vd=vd_ref[...], kmn=kmn_ref[...], vm=vm_ref[...], kpe=kpe_ref[...])
    _three_mixers(scal_ref, subln_ref, (qa_ref[...], qd_ref[...], qmn_ref[...], qmr_ref[...]),
                  [cached, new], oa_o, od_o, om_o, lam_init)


_SMEM_SPEC = pl.BlockSpec(memory_space=pltpu.SMEM)


def _attention_ctx(scal, subln, proj, states, layer, lam_init):
    qa, qd, qmn, qmr, kmn, vm, kpe4 = proj[:7]
    s_ka, s_va, s_kd, s_vd = states[:4]
    t = CTX_SEQ

    def blk(a):
        return pl.BlockSpec((t, a.shape[1]), lambda b: (b, 0))

    def sblk(a):
        return pl.BlockSpec((None, None, t, a.shape[-1]), lambda b: (b, layer, 0, 0))

    return pl.pallas_call(
        functools.partial(_attn_ctx_kernel, lam_init=lam_init),
        out_shape=[jax.ShapeDtypeStruct((N_TOK, 512), BF16)] * 3,
        grid=(N_CTX_BATCH,),
        in_specs=[_SMEM_SPEC, _layer_spec(subln, layer, 1)]
                 + [blk(a) for a in (qa, qd, qmn, qmr)]
                 + [sblk(a) for a in (s_ka, s_va, s_kd, s_vd)]
                 + [blk(a) for a in (kmn, vm, kpe4)],
        out_specs=[pl.BlockSpec((t, 512), lambda b: (b, 0))] * 3,
        compiler_params=pltpu.CompilerParams(vmem_limit_bytes=VMEM_LIMIT),
        name="attention_ctx",
    )(scal, subln, qa, qd, qmn, qmr, s_ka, s_va, s_kd, s_vd, kmn, vm, kpe4)


def _attention_lat(scal, subln, wukv, proj, caches, ctx_out, layer, lam_init):
    qa, qd, qmn, qmr, kmn, vm, kpe4, ka_l, va_l, kd_l, vd_l = proj[:11]
    nq = LAT_SEQ // Q_BLOCK
    q_off = N_CTX // Q_BLOCK
    kv_off = N_CTX // LAT_SEQ

    def qblk(a):
        return pl.BlockSpec((Q_BLOCK, a.shape[1]), lambda b, n: (q_off + b * nq + n, 0))

    def kvblk(a):
        return pl.BlockSpec((LAT_SEQ, a.shape[1]), lambda b, n: (b, 0))

    def kvblk_all(a):
        return pl.BlockSpec((LAT_SEQ, a.shape[1]), lambda b, n: (kv_off + b, 0))

    def cblk(a):
        return pl.BlockSpec((None, None, PAST, a.shape[-1]), lambda b, n: (b, layer, 0, 0))

    qs = [qa, qd, qmn, qmr]
    n_in = 3 + 4 + 7 + 6
    out_spec = pl.BlockSpec((Q_BLOCK, 512), lambda b, n: (q_off + b * nq + n, 0))
    return pl.pallas_call(
        functools.partial(_attn_lat_kernel, lam_init=lam_init),
        out_shape=[jax.ShapeDtypeStruct((N_TOK, 512), BF16)] * 3,
        grid=(N_LAT_BATCH, nq),
        in_specs=[_SMEM_SPEC, _layer_spec(subln, layer, 2), _layer_spec(wukv, layer, 2)]
                 + [qblk(a) for a in qs]
                 + [kvblk(a) for a in (ka_l, va_l, kd_l, vd_l)]
                 + [kvblk_all(a) for a in (kmn, vm, kpe4)]
                 + [cblk(a) for a in caches]
                 + [pl.BlockSpec(memory_space=pl.ANY)] * 3,
        out_specs=[out_spec] * 3,
        input_output_aliases={n_in + k: k for k in range(3)},
        compiler_params=pltpu.CompilerParams(vmem_limit_bytes=VMEM_LIMIT),
        name="attention_lat",
    )(scal, subln, wukv, *qs, ka_l, va_l, kd_l, vd_l, kmn, vm, kpe4, *caches, *ctx_out)


def _route(scores, biased):
    per = N_EXPERTS // N_GROUPS
    tm = scores.shape[1]
    sub = lax.broadcasted_iota(jnp.int32, (per, tm), 0).astype(F32)
    groups = [biased[g * per:(g + 1) * per, :] for g in range(N_GROUPS)]
    gscore = []
    for v in groups:
        m1 = jnp.max(v, axis=0, keepdims=True)
        first = jnp.min(jnp.where(v == m1, sub, float(per)), axis=0, keepdims=True)
        m2 = jnp.max(jnp.where(sub == first, -jnp.inf, v), axis=0, keepdims=True)
        gscore.append(m1 + m2)
    vals = []
    for g in range(N_GROUPS):
        rank = jnp.zeros((1, tm), F32)
        for o in range(N_GROUPS):
            if o == g:
                continue
            ahead = (gscore[o] >= gscore[g]) if o < g else (gscore[o] > gscore[g])
            rank = rank + jnp.where(ahead, 1.0, 0.0)
        vals.append(jnp.where(rank < TOPK_GROUPS, groups[g], NEG))
    idx = [sub + float(g * per) for g in range(N_GROUPS)]
    picks, weights = [], []
    for _ in range(TOP_K):
        best = functools.reduce(jnp.maximum, [jnp.max(v, axis=0, keepdims=True) for v in vals])
        cand = functools.reduce(jnp.minimum, [
            jnp.min(jnp.where(v == best, i, float(N_EXPERTS)), axis=0, keepdims=True)
            for v, i in zip(vals, idx)])
        wsel = jnp.zeros((1, tm), F32)
        for g in range(N_GROUPS):
            hit = idx[g] == cand
            wsel = wsel + jnp.sum(jnp.where(hit, scores[g * per:(g + 1) * per, :], 0.0),
                                  axis=0, keepdims=True)
            vals[g] = jnp.where(hit, -jnp.inf, vals[g])
        picks.append(cand)
        weights.append(wsel)
    total = functools.reduce(lambda a, b: a + b, weights)
    pad = [jnp.zeros((1, tm), F32)] * (8 - TOP_K)
    experts = jnp.concatenate(picks + pad, axis=0).astype(jnp.int32)
    gates = jnp.concatenate([x / total * ROUTED_SCALE for x in weights] + pad, axis=0)
    return experts, gates


def _post_kernel(*refs, n_x):
    (mod_ref, oa_ref, od_ref, om_ref, gpre_ref, wgl_ref, wa_ref, wb_ref, wc_ref, wo_ref, gpost_ref,
     gffn_ref, rwt_ref, rb_ref, xmid_o, h2_o, h2t_o, experts_o, gates_o) = refs[n_x:]
    x = _load_x(refs[:n_x])
    m = mod_ref[0]
    d = D_MODEL
    sh1, sc1, g1, sh2, sc2 = (m[:, 0:d], m[:, d:2 * d], m[:, 2 * d:3 * d], m[:, 3 * d:4 * d],
                              m[:, 4 * d:5 * d])
    h = _rms(x, gpre_ref[...]) * (1.0 + sc1) + sh1
    gate = jax.nn.sigmoid(_dot(h.astype(BF16), wgl_ref[...]))
    merged = (gate[:, 0:d] * _dot(oa_ref[...], wa_ref[...])
              + gate[:, d:2 * d] * _dot(od_ref[...], wb_ref[...])
              + gate[:, 2 * d:3 * d] * _dot(om_ref[...], wc_ref[...]))
    a = _dot(merged.astype(BF16), wo_ref[...])
    xm = x + g1 * _rms(a, gpost_ref[...])
    xmid_o[...] = xm
    h2f = _rms(xm, gffn_ref[...]) * (1.0 + sc2) + sh2
    h2 = h2f.astype(BF16)
    h2_o[...] = h2
    for c in range(ACC_ROWS):
        h2t_o[pl.ds(c, x.shape[0], stride=ACC_ROWS), :] = h2f[:, c * LANES:(c + 1) * LANES]
    scores = jax.nn.sigmoid(_dot_nt(rwt_ref[...], h2))
    experts, gates = _route(scores, scores + rb_ref[...])
    experts_o[...] = experts
    gates_o[...] = gates


def _post_attention(x, mod, oa, od, om, wts, layer):
    tm = ROW_TILE

    def row(i):
        return (i, 0)

    weights = [wts[k] for k in ("g_pre", "wgl", "wa", "wb", "wc", "wo", "g_post", "g_ffn", "rwt", "rb")]
    xs = list(x) if isinstance(x, tuple) else [x]
    return pl.pallas_call(
        functools.partial(_post_kernel, n_x=len(xs)),
        out_shape=[jax.ShapeDtypeStruct((N_TOK, D_MODEL), F32),
                   jax.ShapeDtypeStruct((N_TOK, D_MODEL), BF16),
                   jax.ShapeDtypeStruct((N_TOK * ACC_ROWS, LANES), F32),
                   jax.ShapeDtypeStruct((8, N_TOK), jnp.int32),
                   jax.ShapeDtypeStruct((8, N_TOK), F32)],
        grid=(N_TOK // tm,),
        in_specs=_x_specs(x) + [_mod_spec(layer),
                  pl.BlockSpec((tm, 512), row), pl.BlockSpec((tm, 512), row),
                  pl.BlockSpec((tm, 512), row)]
                 + [_layer_spec(w, layer, 1) for w in weights],
        out_specs=[pl.BlockSpec((tm, D_MODEL), row), pl.BlockSpec((tm, D_MODEL), row),
                   pl.BlockSpec((tm * ACC_ROWS, LANES), row),
                   pl.BlockSpec((8, tm), lambda i: (0, i)), pl.BlockSpec((8, tm), lambda i: (0, i))],
        compiler_params=pltpu.CompilerParams(vmem_limit_bytes=VMEM_LIMIT),
        name="post_attention",
    )(*xs, mod, oa, od, om, *weights)


def _ffn(x, wg, wu):
    return _silu(_dot(x, wg)) * _dot(x, wu)


def _moe_kernel(start_ref, count_ref, dst_ref, gate_ref, xs_ref, wg_ref, wu_ref, wd_ref,
                acc_ref, xa, xb, ya, yb, wgb, wub, wdb, pend):
    e = pl.program_id(0)

    @pl.when(e == 0)
    def _():
        def clear(i, carry):
            r0 = pl.multiple_of(i * MOE_ZERO_ROWS, MOE_ZERO_ROWS)
            acc_ref[pl.ds(r0, MOE_ZERO_ROWS), :] = jnp.zeros((MOE_ZERO_ROWS, LANES), F32)
            return carry

        lax.fori_loop(0, acc_ref.shape[0] // MOE_ZERO_ROWS, clear, 0)

        yb[...] = jnp.zeros_like(yb)
        pend[0] = MOE_LIST_PAD_BASE

    wgb[...] = wg_ref[0].astype(BF16)
    wub[...] = wu_ref[0].astype(BF16)
    wdb[...] = wd_ref[0].astype(BF16)
    start = start_ref[e]
    pairs = (count_ref[e] + 2 * MOE_SUB - 1) // (2 * MOE_SUB)

    def gather(base, xbuf):
        for mi in range(MOE_SUB):
            src = pl.multiple_of(jnp.minimum(dst_ref[base + mi], (N_TOK - 1) * ACC_ROWS), ACC_ROWS)
            xbuf[mi * ACC_ROWS:(mi + 1) * ACC_ROWS, :] = xs_ref[pl.ds(src, ACC_ROWS), :]

    def expert_ffn(xbuf, ybuf):
        x = jnp.concatenate([xbuf[pl.ds(c, MOE_SUB, stride=ACC_ROWS), :].astype(BF16)
                             for c in range(ACC_ROWS)], axis=1)
        hid = _ffn(x, wgb[...], wub[...])
        y = _dot(hid.astype(BF16), wdb[...])
        for c in range(ACC_ROWS):
            ybuf[pl.ds(c, MOE_SUB, stride=ACC_ROWS), :] = y[:, c * LANES:(c + 1) * LANES]

    def scatter(base, ybuf):
        for g0 in range(0, MOE_SUB, MOE_RMW_GROUP):
            updates = []
            for mi in range(g0, g0 + MOE_RMW_GROUP):
                dst = pl.multiple_of(dst_ref[base + mi], ACC_ROWS)
                yv = ybuf[mi * ACC_ROWS:(mi + 1) * ACC_ROWS, :]
                updates.append((dst, acc_ref[pl.ds(dst, ACC_ROWS), :] + gate_ref[base + mi] * yv))
            for dst, val in updates:
                acc_ref[pl.ds(dst, ACC_ROWS), :] = val

    start_next = start_ref[jnp.minimum(e + 1, N_EXPERTS - 1)]
    prev_count = count_ref[jnp.maximum(e - 1, 0)]

    @pl.when((pairs > 0) & ((e == 0) | (prev_count == 0)))
    def _():
        gather(start, xa)

    def pair(p, carry):
        base = start + p * (2 * MOE_SUB)
        gather(base + MOE_SUB, xb)
        expert_ffn(xa, ya)
        scatter(pend[0], yb)
        gather(jnp.where(p + 1 < pairs, base + 2 * MOE_SUB, start_next), xa)
        expert_ffn(xb, yb)
        scatter(base, ya)
        pend[0] = base + MOE_SUB
        return carry

    lax.fori_loop(0, pairs, pair, 0)

    @pl.when(e == N_EXPERTS - 1)
    def _():
        scatter(pend[0], yb)


def _moe(lists, h2t, wg, wu, wd, layer):
    def expert(e, *_):
        return (layer, e, 0, 0)

    def whole(e, *_):
        return (0, 0)

    acc_rows = (N_TOK + MOE_SPARE_TOKENS) * ACC_ROWS
    assert acc_rows % MOE_ZERO_ROWS == 0
    return pl.pallas_call(
        _moe_kernel,
        out_shape=jax.ShapeDtypeStruct((acc_rows, LANES), F32),
        grid_spec=pltpu.PrefetchScalarGridSpec(
            num_scalar_prefetch=len(lists),
            grid=(N_EXPERTS,),
            in_specs=[pl.BlockSpec(h2t.shape, whole, pipeline_mode=pl.Buffered(1)),
                      pl.BlockSpec((None, 1, D_MODEL, F_EXPERT), expert),
                      pl.BlockSpec((None, 1, D_MODEL, F_EXPERT), expert),
                      pl.BlockSpec((None, 1, F_EXPERT, D_MODEL), expert)],
            out_specs=pl.BlockSpec((acc_rows, LANES), whole, pipeline_mode=pl.Buffered(1)),
            scratch_shapes=[pltpu.VMEM((MOE_SUB * ACC_ROWS, LANES), F32)] * 4
                           + [pltpu.VMEM((D_MODEL, F_EXPERT), BF16),
                              pltpu.VMEM((D_MODEL, F_EXPERT), BF16),
                              pltpu.VMEM((F_EXPERT, D_MODEL), BF16),
                              pltpu.SMEM((1,), jnp.int32)]),
        compiler_params=pltpu.CompilerParams(
            dimension_semantics=("arbitrary",), vmem_limit_bytes=MOE_VMEM_LIMIT),
        name="moe",
    )(*lists, h2t, wg, wu, wd)


def _final_kernel(acc_ref, h2_ref, xmid_ref, mod_ref, swg_ref, swu_ref, swd_ref, gpost_ref, *out_refs):
    tm = h2_ref.shape[0]
    routed = jnp.concatenate(
        [acc_ref[pl.ds(c, tm, stride=ACC_ROWS), :] for c in range(ACC_ROWS)], axis=1)
    hid = _ffn(h2_ref[...], swg_ref[...], swu_ref[...])
    shared = _dot(hid.astype(BF16), swd_ref[...])
    g2 = mod_ref[0][:, 5 * D_MODEL:6 * D_MODEL]
    out = xmid_ref[...] + g2 * _rms(routed + shared, gpost_ref[...])
    if len(out_refs) == 1:
        out_refs[0][...] = out
    else:
        @pl.when(pl.program_id(0) < CTX_TILES)
        def _():
            out_refs[0][...] = out

        @pl.when(pl.program_id(0) >= CTX_TILES)
        def _():
            out_refs[1][...] = out


def _final(acc, h2, xmid, mod, wts, layer, split):
    tm = ROW_TILE

    def row(i):
        return (i, 0)

    weights = [wts[k] for k in ("swg", "swu", "swd", "g_ffn_post")]
    return pl.pallas_call(
        _final_kernel,
        out_shape=([jax.ShapeDtypeStruct((N_CTX, D_MODEL), F32),
                    jax.ShapeDtypeStruct((N_LAT, D_MODEL), F32)] if split
                   else jax.ShapeDtypeStruct((N_TOK, D_MODEL), F32)),
        grid=(N_TOK // tm,),
        in_specs=[pl.BlockSpec((tm * ACC_ROWS, LANES), row),
                  pl.BlockSpec((tm, D_MODEL), row),
                  pl.BlockSpec((tm, D_MODEL), row),
                  _mod_spec(layer)]
                 + [_layer_spec(w, layer, 1) for w in weights],
        out_specs=_x_specs((None, None)) if split else pl.BlockSpec((tm, D_MODEL), row),
        compiler_params=pltpu.CompilerParams(
            dimension_semantics=("arbitrary",), vmem_limit_bytes=VMEM_LIMIT),
        name="ffn_final",
    )(acc, h2, xmid, mod, *weights)


def _routing_lists(experts, gates):
    e = experts[:TOP_K].reshape(-1)
    tok = jnp.tile(jnp.arange(N_TOK, dtype=jnp.int32), TOP_K)
    pad_e = jnp.repeat(jnp.arange(N_EXPERTS, dtype=jnp.int32), MOE_PAD)
    pad_tok = N_TOK + jnp.tile(jnp.arange(MOE_PAD, dtype=jnp.int32), N_EXPERTS)
    keys = jnp.concatenate([e * TOK_KEY + tok, pad_e * TOK_KEY + pad_tok])
    vals = jnp.concatenate([gates[:TOP_K].reshape(-1), jnp.zeros((N_EXPERTS * MOE_PAD,), F32)])
    keys, gate_sorted = lax.sort((keys, vals), num_keys=1)
    tail = jnp.full((MOE_SUB,), N_TOK, jnp.int32)
    tok_sorted = jnp.concatenate([keys & (TOK_KEY - 1), tail])
    gate_sorted = jnp.concatenate([gate_sorted, jnp.zeros((MOE_SUB,), F32)])
    dst = jnp.minimum(tok_sorted, N_TOK) * ACC_ROWS
    counts = jnp.sum((e[None, :] == jnp.arange(N_EXPERTS, dtype=jnp.int32)[:, None]).astype(jnp.int32),
                     axis=1)
    start = jnp.cumsum(counts) - counts + MOE_PAD * jnp.arange(N_EXPERTS, dtype=jnp.int32)
    return start.astype(jnp.int32), counts, dst, gate_sorted


def _rope_tables():
    t = np.arange(LAT_SEQ)
    pos = np.stack([t // GRID_W, t % GRID_W], axis=1).astype(np.float64)

    def table(half):
        inv = ROPE_BASE ** (-np.arange(half, dtype=np.float64) / half)
        lane = np.arange(LANES)
        axis = (lane // (2 * half)) % 2
        freq = inv[lane % half]
        ang = pos[:, axis] * freq[None, :]
        sign = np.where((lane % (2 * half)) < half, -1.0, 1.0)
        cos = np.concatenate([np.ones((ROW_TILE, LANES)), np.cos(ang)], axis=0)
        sin = np.concatenate([np.zeros((ROW_TILE, LANES)), np.sin(ang) * sign[None, :]], axis=0)
        return jnp.asarray(cos, F32), jnp.asarray(sin, F32)

    cos64, sin64 = table(16)
    cos32, sin32 = table(8)
    return cos64, sin64, cos32, sin32


def _prepare_weights(norm_attn_pre, norm_attn_post, norm_ffn_pre, norm_ffn_post, w_in, dif_subln,
                     mla_q_norm, mla_w_uq, mla_kv_norm, mla_w_ukv, w_branch_a, w_branch_b,
                     w_branch_c, w_out, router_w, router_bias, shared_w_gate, shared_w_up,
                     shared_w_down):
    dp = DEPTH
    qa = w_in[:, :, 0:512].reshape(dp, D_MODEL, 2, 4, 64).transpose(0, 1, 3, 2, 4).reshape(dp, D_MODEL, 512)
    kpe = w_in[:, :, 2688:2720]
    w1 = jnp.concatenate([qa, w_in[:, :, 512:2688], kpe, kpe, kpe, kpe], axis=2).astype(BF16)
    uq = mla_w_uq.reshape(dp, 256, 8, 96)
    wuq = jnp.concatenate([uq[..., :64].reshape(dp, 256, 512), uq[..., 64:].reshape(dp, 256, 256)],
                          axis=2).astype(BF16)
    ukv = mla_w_ukv.reshape(dp, 128, 8, 128)
    wukv = jnp.concatenate([ukv[..., :64].reshape(dp, 128, 512), ukv[..., 64:].reshape(dp, 128, 512)],
                           axis=2).astype(BF16)
    wa = w_branch_a.reshape(dp, 2, 4, 64, D_MODEL).transpose(0, 2, 1, 3, 4).reshape(dp, 512, D_MODEL)
    return dict(
        g_pre=norm_attn_pre[:, None, :], g_post=norm_attn_post[:, None, :],
        g_ffn=norm_ffn_pre[:, None, :], g_ffn_post=norm_ffn_post[:, None, :],
        w1=w1, wgl=w_in[:, :, 2720:].astype(BF16),
        qnorm=mla_q_norm[:, None, :], wuq=wuq, kvnorm=mla_kv_norm[:, None, :], wukv=wukv,
        subln=dif_subln[:, None, :],
        wa=wa.astype(BF16), wb=w_branch_b.astype(BF16), wc=w_branch_c.astype(BF16),
        wo=w_out.astype(BF16), rwt=router_w.transpose(0, 2, 1).astype(BF16),
        rb=router_bias[:, :, None],
        swg=shared_w_gate.astype(BF16), swu=shared_w_up.astype(BF16), swd=shared_w_down.astype(BF16))


def kernel(x_prompt, x_sample, cache_swa_k, cache_swa_v, cache_dif_k, cache_dif_v, cache_mla_ckv, cache_mla_kpe, c, c_ctx, w_mod, b_mod, norm_attn_pre, norm_attn_post, norm_ffn_pre, norm_ffn_post, w_in, swa_sink, dif_lq1, dif_lk1, dif_lq2, dif_lk2, dif_subln, mla_q_norm, mla_w_uq, mla_kv_norm, mla_w_ukv, w_branch_a, w_branch_b, w_branch_c, w_out, router_w, router_bias, moe_w_gate, moe_w_up, moe_w_down, shared_w_gate, shared_w_up, shared_w_down):
    x = (x_prompt.reshape(N_CTX, D_MODEL), x_sample.reshape(N_LAT, D_MODEL))
    cvec = jnp.concatenate([c_ctx[None, :], c, jnp.zeros((8 - 1 - N_LAT_BATCH, D_MODEL), F32)], axis=0)
    mod = _modulation(cvec, w_mod, b_mod).reshape(DEPTH, 8, 1, 6 * D_MODEL)
    tables = _rope_tables()
    wts = _prepare_weights(norm_attn_pre, norm_attn_post, norm_ffn_pre, norm_ffn_post, w_in,
                           dif_subln, mla_q_norm, mla_w_uq, mla_kv_norm, mla_w_ukv, w_branch_a,
                           w_branch_b, w_branch_c, w_out, router_w, router_bias, shared_w_gate,
                           shared_w_up, shared_w_down)
    caches = [cache_swa_k.reshape(N_LAT_BATCH, DEPTH, PAST, 128),
              cache_swa_v.reshape(N_LAT_BATCH, DEPTH, PAST, 128),
              cache_dif_k.reshape(N_LAT_BATCH, DEPTH, PAST, 512),
              cache_dif_v.reshape(N_LAT_BATCH, DEPTH, PAST, 512),
              cache_mla_ckv,
              jnp.tile(cache_mla_kpe, (1, 1, 1, 4))]
    lam_init = [0.8 - 0.6 * math.exp(-0.3 * l) for l in range(DEPTH)]
    lam = (jnp.exp(jnp.sum(dif_lq1 * dif_lk1, axis=1)) - jnp.exp(jnp.sum(dif_lq2 * dif_lk2, axis=1))
           + jnp.asarray(lam_init, F32))
    scal = jnp.concatenate([swa_sink, lam[:, None]], axis=1).astype(F32)

    states = None
    for l in range(DEPTH):
        proj = _in_projection(x, mod, wts, tables, l, states)
        states = proj[11:]
        ctx_out = _attention_ctx(scal[l], wts["subln"], proj, states, l, lam_init[l])
        oa, od, om = _attention_lat(scal[l], wts["subln"], wts["wukv"], proj, caches, ctx_out, l,
                                    lam_init[l])
        xmid, h2, h2t, experts, gates = _post_attention(x, mod, oa, od, om, wts, l)
        lists = _routing_lists(experts, gates)
        acc = _moe(lists, h2t, moe_w_gate, moe_w_up, moe_w_down, l)
        x = _final(acc, h2, xmid, mod, wts, l, split=(l == DEPTH - 1))

    y_p = x[0].reshape(N_CTX_BATCH, CTX_SEQ, D_MODEL)
    y_s = x[1].reshape(N_LAT_BATCH, LAT_SEQ, D_MODEL)
    s_ka, s_va, s_kd, s_vd, s_ckv, s_kpe = states
    return (y_p, y_s,
            s_ka.reshape(N_CTX_BATCH, DEPTH, CTX_SEQ, 2, 64),
            s_va.reshape(N_CTX_BATCH, DEPTH, CTX_SEQ, 2, 64),
            s_kd.reshape(N_CTX_BATCH, DEPTH, CTX_SEQ, 4, 2, 64),
            s_vd.reshape(N_CTX_BATCH, DEPTH, CTX_SEQ, 4, 128),
            s_ckv, s_kpe)
```

```python
import functools
import math

import numpy as np
import jax
import jax.numpy as jnp
from jax import lax
from jax.experimental import pallas as pl
from jax.experimental.pallas import tpu as pltpu

F32 = jnp.float32
BF16 = jnp.bfloat16

D_MODEL = 1024
N_CTX_BATCH, CTX_SEQ = 16, 256
N_LAT_BATCH, LAT_SEQ = 2, 1024
PAST = 256
N_CTX = N_CTX_BATCH * CTX_SEQ
N_LAT = N_LAT_BATCH * LAT_SEQ
N_TOK = N_CTX + N_LAT
DEPTH = 2
GRID_W = 64
WINDOW = 128
N_EXPERTS = 64
N_GROUPS = 8
TOPK_GROUPS = 4
TOP_K = 6
F_EXPERT = 256
ROUTED_SCALE = 2.5
ROPE_BASE = 10000.0
EPS = 1e-6
NEG = -1e30

LANES = 128
ROW_TILE = 256
Q_BLOCK = 128
VMEM_LIMIT = 56 * 1024 * 1024
ACC_ROWS = 8
MOE_SPARE_TOKENS = 8
MOE_ZERO_ROWS = 64
MOE_VMEM_LIMIT = 62 * 1024 * 1024
MOE_SUB = 128
MOE_RMW_GROUP = 8
MOE_PAD = 2 * MOE_SUB
TOK_KEY = 8192
MOE_LIST_PAD_BASE = N_TOK * TOP_K + N_EXPERTS * MOE_PAD

C_QA, C_KA, C_VA, C_QD, C_KD, C_VD, C_QC, C_KVC, C_KPE, C_END = (
    0, 512, 640, 768, 1280, 1792, 2304, 2560, 2688, 2816)

SCALE_64 = 1.0 / math.sqrt(64.0)
SCALE_96 = 1.0 / math.sqrt(96.0)

CTX_TILES = N_CTX // ROW_TILE
assert ROW_TILE == CTX_SEQ and LAT_SEQ % ROW_TILE == 0


def _mod_row_of_tile(i):
    return jnp.where(i < CTX_TILES, 0, 1 + (i - CTX_TILES) // (LAT_SEQ // ROW_TILE))


def _rms(x, g):
    return x * lax.rsqrt(jnp.mean(x * x, axis=-1, keepdims=True) + EPS) * g


def _dot(a, b):
    return jnp.dot(a, b, preferred_element_type=F32)


def _dot_nt(a, b):
    return lax.dot_general(a, b, (((1,), (1,)), ((), ())), preferred_element_type=F32)


def _silu(x):
    return x * jax.nn.sigmoid(x)


def _layer_spec(arr, layer, grid_rank):
    zeros = (0,) * (arr.ndim - 1)
    return pl.BlockSpec((None,) + arr.shape[1:], lambda *_: (layer,) + zeros)


def _mod_kernel(c_ref, w_ref, b_ref, o_ref):
    c = c_ref[...]
    o_ref[0] = _dot(_silu(c).astype(BF16), w_ref[0].astype(BF16)) + b_ref[0]


def _modulation(cvec, w_mod, b_mod):
    tn = 1536
    n = w_mod.shape[-1]
    return pl.pallas_call(
        _mod_kernel,
        out_shape=jax.ShapeDtypeStruct((DEPTH, 8, n), F32),
        grid=(DEPTH, n // tn),
        in_specs=[
            pl.BlockSpec((8, D_MODEL), lambda l, j: (0, 0)),
            pl.BlockSpec((1, D_MODEL, tn), lambda l, j: (l, 0, j)),
            pl.BlockSpec((1, 1, tn), lambda l, j: (l, 0, j)),
        ],
        out_specs=pl.BlockSpec((1, 8, tn), lambda l, j: (l, 0, j)),
        compiler_params=pltpu.CompilerParams(vmem_limit_bytes=VMEM_LIMIT),
        name="modulation",
    )(cvec, w_mod, b_mod.reshape(DEPTH, 1, n))


def _x_specs(x):
    if isinstance(x, tuple):
        return [pl.BlockSpec((ROW_TILE, D_MODEL), lambda i: (jnp.minimum(i, CTX_TILES - 1), 0)),
                pl.BlockSpec((ROW_TILE, D_MODEL), lambda i: (jnp.maximum(i - CTX_TILES, 0), 0))]
    return [pl.BlockSpec((ROW_TILE, D_MODEL), lambda i: (i, 0))]


def _load_x(x_refs):
    if len(x_refs) == 1:
        return x_refs[0][...]
    return jnp.where(pl.program_id(0) < CTX_TILES, x_refs[0][...], x_refs[1][...])


def _mod_spec(layer):
    return pl.BlockSpec((None, 1, 1, 6 * D_MODEL), lambda i: (layer, _mod_row_of_tile(i), 0, 0))


def _rope128(x, cos, sin, half):
    lane = lax.broadcasted_iota(jnp.int32, x.shape, 1)
    first = (lane % (2 * half)) < half
    partner = jnp.where(first, pltpu.roll(x, LANES - half, 1), pltpu.roll(x, half, 1))
    return x * cos + partner * sin


def _rope_cols(x, cos, sin, half):
    chunks = [_rope128(x[:, c:c + LANES], cos, sin, half) for c in range(0, x.shape[1], LANES)]
    return chunks[0] if len(chunks) == 1 else jnp.concatenate(chunks, axis=1)


N_IN_INPUTS = 11


def _in_kernel(*refs, n_x):
    (mod_ref, g_ref, w1_ref, qn_ref, wuq_ref, kvn_ref, wukv_ref,
     cos64_ref, sin64_ref, cos32_ref, sin32_ref) = refs[n_x:n_x + N_IN_INPUTS]
    (qa_o, qd_o, qmn_o, qmr_o, kmn_o, vm_o, kpe4_o, ka_l, va_l, kd_l, vd_l,
     s_ka, s_va, s_kd, s_vd, s_ckv, s_kpe) = refs[-17:]
    i = pl.program_id(0)
    x = _load_x(refs[:n_x])
    m = mod_ref[0]
    sh1, sc1 = m[:, 0:D_MODEL], m[:, D_MODEL:2 * D_MODEL]
    h = _rms(x, g_ref[...]) * (1.0 + sc1) + sh1
    z = _dot(h.astype(BF16), w1_ref[...])
    cos64, sin64 = cos64_ref[...], sin64_ref[...]
    cos32, sin32 = cos32_ref[...], sin32_ref[...]
    qa_o[...] = (_rope_cols(z[:, C_QA:C_KA], cos64, sin64, 16) * SCALE_64).astype(BF16)
    qd_o[...] = (_rope_cols(z[:, C_QD:C_KD], cos64, sin64, 16) * SCALE_64).astype(BF16)
    ka = _rope_cols(z[:, C_KA:C_VA], cos64, sin64, 16)
    va = z[:, C_VA:C_QD]
    kd = _rope_cols(z[:, C_KD:C_VD], cos64, sin64, 16)
    vd = z[:, C_VD:C_QC]
    kpe4 = _rope_cols(z[:, C_KPE:C_END], cos32, sin32, 8)
    kpe4_o[...] = kpe4.astype(BF16)
    qn = _rms(z[:, C_QC:C_KVC], qn_ref[...])
    qm = _dot(qn.astype(BF16), wuq_ref[...]) * SCALE_96
    qmn_o[...] = qm[:, 0:512].astype(BF16)
    qmr_o[...] = _rope_cols(qm[:, 512:768], cos32, sin32, 8).astype(BF16)
    ckv = _rms(z[:, C_KVC:C_KPE], kvn_ref[...])
    kv = _dot(ckv.astype(BF16), wukv_ref[...])
    kmn_o[...] = kv[:, 0:512].astype(BF16)
    vm_o[...] = kv[:, 512:1024].astype(BF16)

    @pl.when(i < CTX_TILES)
    def _():
        s_ka[...] = ka
        s_va[...] = va
        s_kd[...] = kd
        s_vd[...] = vd
        s_ckv[...] = ckv
        s_kpe[...] = kpe4[:, 0:32]

    @pl.when(i >= CTX_TILES)
    def _():
        ka_l[...] = ka.astype(BF16)
        va_l[...] = va.astype(BF16)
        kd_l[...] = kd.astype(BF16)
        vd_l[...] = vd.astype(BF16)


STATE_WIDTHS = (128, 128, 512, 512, 128, 32)


def _in_projection(x, mod, wts, tables, layer, prev_states):
    tm = ROW_TILE

    def row(i):
        return (i, 0)

    def table_row(i):
        return (jnp.where(i < CTX_TILES, 0, 1 + (i - CTX_TILES) % (LAT_SEQ // tm)), 0)

    def lat_row(i):
        return (jnp.maximum(i - CTX_TILES, 0), 0)

    def state_block(i):
        return (jnp.minimum(i, CTX_TILES - 1), layer, 0, 0)

    table_spec = pl.BlockSpec((tm, LANES), table_row)
    weights = [wts["g_pre"], wts["w1"], wts["qnorm"], wts["wuq"], wts["kvnorm"], wts["wukv"]]
    all_rows = [(512, BF16)] * 3 + [(256, BF16), (512, BF16), (512, BF16), (128, BF16)]
    lat_rows = [(128, BF16), (128, BF16), (512, BF16), (512, BF16)]
    aliased = [] if prev_states is None else list(prev_states)
    xs = list(x) if isinstance(x, tuple) else [x]
    n_in = len(xs) + N_IN_INPUTS
    n_plain_out = len(all_rows) + len(lat_rows)
    return pl.pallas_call(
        functools.partial(_in_kernel, n_x=len(xs)),
        out_shape=[jax.ShapeDtypeStruct((N_TOK, w), dt) for w, dt in all_rows]
                  + [jax.ShapeDtypeStruct((N_LAT, w), dt) for w, dt in lat_rows]
                  + [jax.ShapeDtypeStruct((N_CTX_BATCH, DEPTH, CTX_SEQ, w), F32) for w in STATE_WIDTHS],
        grid=(N_TOK // tm,),
        in_specs=_x_specs(x) + [_mod_spec(layer)]
                 + [_layer_spec(w, layer, 1) for w in weights]
                 + [table_spec] * 4
                 + [pl.BlockSpec(memory_space=pl.ANY)] * len(aliased),
        out_specs=[pl.BlockSpec((tm, w), row) for w, _ in all_rows]
                  + [pl.BlockSpec((tm, w), lat_row) for w, _ in lat_rows]
                  + [pl.BlockSpec((None, None, CTX_SEQ, w), state_block) for w in STATE_WIDTHS],
        input_output_aliases={n_in + k: n_plain_out + k for k in range(len(aliased))},
        compiler_params=pltpu.CompilerParams(
            dimension_semantics=("arbitrary",), vmem_limit_bytes=VMEM_LIMIT),
        name="in_projection",
    )(*xs, mod, *weights, *tables, *aliased)


def _attend(q, parts, sink=None):
    scores = []
    for k, _, mask in parts:
        s = _dot_nt(q, k)
        if mask is not None:
            s = jnp.where(mask, s, NEG)
        scores.append(s)
    m = functools.reduce(jnp.maximum, [jnp.max(s, axis=-1, keepdims=True) for s in scores])
    if sink is not None:
        m = jnp.maximum(m, sink)
    den = None
    out = None
    for s, (_, v, _) in zip(scores, parts):
        e = jnp.exp(s - m)
        d = jnp.sum(e, axis=-1, keepdims=True)
        o = _dot(e.astype(BF16), v)
        den = d if den is None else den + d
        out = o if out is None else out + o
    if sink is not None:
        den = den + jnp.exp(sink - m)
    return out / den


A_HEADS = 8


def _attend_heads(qs, parts, sinks, stack):
    rows = qs[0].shape[0]
    if not stack:
        return [_attend(q, parts, None if sinks is None else sinks[i]) for i, q in enumerate(qs)]
    sink = None
    if sinks is not None:
        sink = jnp.concatenate([jnp.full((rows, 1), s, F32) for s in sinks], axis=0)
    o = _attend(jnp.concatenate(qs, axis=0), parts, sink)
    return [o[i * rows:(i + 1) * rows] for i in range(len(qs))]


def _three_mixers(scal_ref, subln_ref, q_blocks, kv_parts, oa_o, od_o, om_o, lam_init, stack):
    qa, qd, qmn, qmr = q_blocks
    rows = qa.shape[0]
    lane = lax.broadcasted_iota(jnp.int32, (rows, LANES), 1)
    low = lane < 64
    zero = jnp.zeros((rows, LANES), BF16)

    parts = [(p["ka"], p["va"], p.get("mask_a")) for p in kv_parts]
    qs = [jnp.where(low if kvh == 0 else ~low, qa[:, g * LANES:(g + 1) * LANES], zero)
          for kvh in range(2) for g in range(4)]
    out_a = _attend_heads(qs, parts, [scal_ref[h] for h in range(A_HEADS)], stack)
    for g in range(4):
        oa_o[:, g * LANES:(g + 1) * LANES] = jnp.where(low, out_a[g], out_a[4 + g]).astype(BF16)

    lam = scal_ref[8]
    subln = subln_ref[...]
    for h in range(4):
        sl = slice(h * LANES, (h + 1) * LANES)
        q128 = qd[:, sl]
        o1, o2 = _attend_heads([jnp.where(low, q128, zero), jnp.where(low, zero, q128)],
                               [(p["kd"][:, sl], p["vd"][:, sl], None) for p in kv_parts],
                               None, stack)
        od = o1 - lam * o2
        od_o[:, sl] = (_rms(od, subln) * (1.0 - lam_init)).astype(BF16)

    for i in range(4):
        sl = slice(i * LANES, (i + 1) * LANES)
        qs = []
        for sub in range(2):
            h = 2 * i + sub
            qn = jnp.where(low if sub == 0 else ~low, qmn[:, sl], zero)
            qr128 = qmr[:, (h // 4) * LANES:(h // 4 + 1) * LANES]
            qr = jnp.where((lane // 32) == (h % 4), qr128, zero)
            qs.append(jnp.concatenate([qn, qr], axis=1))
        parts = [(jnp.concatenate([p["kmn"][:, sl], p["kpe"]], axis=1), p["vm"][:, sl], None)
                 for p in kv_parts]
        o_even, o_odd = _attend_heads(qs, parts, None, stack)
        om_o[:, sl] = jnp.where(low, o_even, o_odd).astype(BF16)


def _attn_ctx_kernel(scal_ref, subln_ref, qa_ref, qd_ref, qmn_ref, qmr_ref,
                     ka_ref, va_ref, kd_ref, vd_ref, kmn_ref, vm_ref, kpe_ref,
                     oa_o, od_o, om_o, *, lam_init):
    part = dict(ka=ka_ref[...].astype(BF16), va=va_ref[...].astype(BF16),
                kd=kd_ref[...].astype(BF16), vd=vd_ref[...].astype(BF16),
                kmn=kmn_ref[...], vm=vm_ref[...], kpe=kpe_ref[...])
    _three_mixers(scal_ref, subln_ref, (qa_ref[...], qd_ref[...], qmn_ref[...], qmr_ref[...]),
                  [part], oa_o, od_o, om_o, lam_init, stack=False)


def _attn_lat_kernel(scal_ref, subln_ref, wukv_ref, qa_ref, qd_ref, qmn_ref, qmr_ref,
                     ka_ref, va_ref, kd_ref, vd_ref, kmn_ref, vm_ref, kpe_ref,
                     cka_ref, cva_ref, ckd_ref, cvd_ref, cckv_ref, ckpe_ref,
                     oa_in, od_in, om_in, oa_o, od_o, om_o, *, lam_init):
    del oa_in, od_in, om_in
    n = pl.program_id(1)
    kv_c = _dot(cckv_ref[...].astype(BF16), wukv_ref[...])
    cached = dict(ka=cka_ref[...].astype(BF16), va=cva_ref[...].astype(BF16),
                  kd=ckd_ref[...].astype(BF16), vd=cvd_ref[...].astype(BF16),
                  kmn=kv_c[:, 0:512].astype(BF16), vm=kv_c[:, 512:1024].astype(BF16),
                  kpe=ckpe_ref[...].astype(BF16))
    span = 3 * Q_BLOCK
    start = pl.multiple_of(jnp.clip((n - 1) * Q_BLOCK, 0, LAT_SEQ - span), Q_BLOCK)
    qpos = n * Q_BLOCK + lax.broadcasted_iota(jnp.int32, (A_HEADS * Q_BLOCK, span), 0) % Q_BLOCK
    kpos = start + lax.broadcasted_iota(jnp.int32, (A_HEADS * Q_BLOCK, span), 1)
    mask_a = jnp.abs(qpos - kpos) <= WINDOW
    new = dict(ka=ka_ref[pl.ds(start, span), :], va=va_ref[pl.ds(start, span), :], mask_a=mask_a,
               kd=kd_ref[...], vd=vd_ref[...], kmn=kmn_ref[...], vm=vm_ref[...], kpe=kpe_ref[...])
    _three_mixers(scal_ref, subln_ref, (qa_ref[...], qd_ref[...], qmn_ref[...], qmr_ref[...]),
                  [cached, new], oa_o, od_o, om_o, lam_init, stack=True)


_SMEM_SPEC = pl.BlockSpec(memory_space=pltpu.SMEM)


def _attention_ctx(scal, subln, proj, states, layer, lam_init):
    qa, qd, qmn, qmr, kmn, vm, kpe4 = proj[:7]
    s_ka, s_va, s_kd, s_vd = states[:4]
    t = CTX_SEQ

    def blk(a):
        return pl.BlockSpec((t, a.shape[1]), lambda b: (b, 0))

    def sblk(a):
        return pl.BlockSpec((None, None, t, a.shape[-1]), lambda b: (b, layer, 0, 0))

    return pl.pallas_call(
        functools.partial(_attn_ctx_kernel, lam_init=lam_init),
        out_shape=[jax.ShapeDtypeStruct((N_TOK, 512), BF16)] * 3,
        grid=(N_CTX_BATCH,),
        in_specs=[_SMEM_SPEC, _layer_spec(subln, layer, 1)]
                 + [blk(a) for a in (qa, qd, qmn, qmr)]
                 + [sblk(a) for a in (s_ka, s_va, s_kd, s_vd)]
                 + [blk(a) for a in (kmn, vm, kpe4)],
        out_specs=[pl.BlockSpec((t, 512), lambda b: (b, 0))] * 3,
        compiler_params=pltpu.CompilerParams(vmem_limit_bytes=VMEM_LIMIT),
        name="attention_ctx",
    )(scal, subln, qa, qd, qmn, qmr, s_ka, s_va, s_kd, s_vd, kmn, vm, kpe4)


def _attention_lat(scal, subln, wukv, proj, caches, ctx_out, layer, lam_init):
    qa, qd, qmn, qmr, kmn, vm, kpe4, ka_l, va_l, kd_l, vd_l = proj[:11]
    nq = LAT_SEQ // Q_BLOCK
    q_off = N_CTX // Q_BLOCK
    kv_off = N_CTX // LAT_SEQ

    def qblk(a):
        return pl.BlockSpec((Q_BLOCK, a.shape[1]), lambda b, n: (q_off + b * nq + n, 0))

    def kvblk(a):
        return pl.BlockSpec((LAT_SEQ, a.shape[1]), lambda b, n: (b, 0))

    def kvblk_all(a):
        return pl.BlockSpec((LAT_SEQ, a.shape[1]), lambda b, n: (kv_off + b, 0))

    def cblk(a):
        return pl.BlockSpec((None, None, PAST, a.shape[-1]), lambda b, n: (b, layer, 0, 0))

    qs = [qa, qd, qmn, qmr]
    n_in = 3 + 4 + 7 + 6
    out_spec = pl.BlockSpec((Q_BLOCK, 512), lambda b, n: (q_off + b * nq + n, 0))
    return pl.pallas_call(
        functools.partial(_attn_lat_kernel, lam_init=lam_init),
        out_shape=[jax.ShapeDtypeStruct((N_TOK, 512), BF16)] * 3,
        grid=(N_LAT_BATCH, nq),
        in_specs=[_SMEM_SPEC, _layer_spec(subln, layer, 2), _layer_spec(wukv, layer, 2)]
                 + [qblk(a) for a in qs]
                 + [kvblk(a) for a in (ka_l, va_l, kd_l, vd_l)]
                 + [kvblk_all(a) for a in (kmn, vm, kpe4)]
                 + [cblk(a) for a in caches]
                 + [pl.BlockSpec(memory_space=pl.ANY)] * 3,
        out_specs=[out_spec] * 3,
        input_output_aliases={n_in + k: k for k in range(3)},
        compiler_params=pltpu.CompilerParams(vmem_limit_bytes=VMEM_LIMIT),
        name="attention_lat",
    )(scal, subln, wukv, *qs, ka_l, va_l, kd_l, vd_l, kmn, vm, kpe4, *caches, *ctx_out)


def _route(scores, biased):
    per = N_EXPERTS // N_GROUPS
    tm = scores.shape[1]
    sub = lax.broadcasted_iota(jnp.int32, (per, tm), 0).astype(F32)
    groups = [biased[g * per:(g + 1) * per, :] for g in range(N_GROUPS)]
    gscore = []
    for v in groups:
        m1 = jnp.max(v, axis=0, keepdims=True)
        first = jnp.min(jnp.where(v == m1, sub, float(per)), axis=0, keepdims=True)
        m2 = jnp.max(jnp.where(sub == first, -jnp.inf, v), axis=0, keepdims=True)
        gscore.append(m1 + m2)
    vals = []
    for g in range(N_GROUPS):
        rank = jnp.zeros((1, tm), F32)
        for o in range(N_GROUPS):
            if o == g:
                continue
            ahead = (gscore[o] >= gscore[g]) if o < g else (gscore[o] > gscore[g])
            rank = rank + jnp.where(ahead, 1.0, 0.0)
        vals.append(jnp.where(rank < TOPK_GROUPS, groups[g], NEG))
    idx = [sub + float(g * per) for g in range(N_GROUPS)]
    picks, weights = [], []
    for _ in range(TOP_K):
        best = functools.reduce(jnp.maximum, [jnp.max(v, axis=0, keepdims=True) for v in vals])
        cand = functools.reduce(jnp.minimum, [
            jnp.min(jnp.where(v == best, i, float(N_EXPERTS)), axis=0, keepdims=True)
            for v, i in zip(vals, idx)])
        wsel = jnp.zeros((1, tm), F32)
        for g in range(N_GROUPS):
            hit = idx[g] == cand
            wsel = wsel + jnp.sum(jnp.where(hit, scores[g * per:(g + 1) * per, :], 0.0),
                                  axis=0, keepdims=True)
            vals[g] = jnp.where(hit, -jnp.inf, vals[g])
        picks.append(cand)
        weights.append(wsel)
    total = functools.reduce(lambda a, b: a + b, weights)
    pad = [jnp.zeros((1, tm), F32)] * (8 - TOP_K)
    experts = jnp.concatenate(picks + pad, axis=0).astype(jnp.int32)
    gates = jnp.concatenate([x / total * ROUTED_SCALE for x in weights] + pad, axis=0)
    return experts, gates


def _post_kernel(*refs, n_x):
    (mod_ref, oa_ref, od_ref, om_ref, gpre_ref, wgl_ref, wa_ref, wb_ref, wc_ref, wo_ref, gpost_ref,
     gffn_ref, rwt_ref, rb_ref, xmid_o, h2_o, h2t_o, experts_o, gates_o) = refs[n_x:]
    x = _load_x(refs[:n_x])
    m = mod_ref[0]
    d = D_MODEL
    sh1, sc1, g1, sh2, sc2 = (m[:, 0:d], m[:, d:2 * d], m[:, 2 * d:3 * d], m[:, 3 * d:4 * d],
                              m[:, 4 * d:5 * d])
    h = _rms(x, gpre_ref[...]) * (1.0 + sc1) + sh1
    gate = jax.nn.sigmoid(_dot(h.astype(BF16), wgl_ref[...]))
    merged = (gate[:, 0:d] * _dot(oa_ref[...], wa_ref[...])
              + gate[:, d:2 * d] * _dot(od_ref[...], wb_ref[...])
              + gate[:, 2 * d:3 * d] * _dot(om_ref[...], wc_ref[...]))
    a = _dot(merged.astype(BF16), wo_ref[...])
    xm = x + g1 * _rms(a, gpost_ref[...])
    xmid_o[...] = xm
    h2f = _rms(xm, gffn_ref[...]) * (1.0 + sc2) + sh2
    h2 = h2f.astype(BF16)
    h2_o[...] = h2
    for c in range(ACC_ROWS):
        h2t_o[pl.ds(c, x.shape[0], stride=ACC_ROWS), :] = h2f[:, c * LANES:(c + 1) * LANES]
    scores = jax.nn.sigmoid(_dot_nt(rwt_ref[...], h2))
    experts, gates = _route(scores, scores + rb_ref[...])
    experts_o[...] = experts
    gates_o[...] = gates


def _post_attention(x, mod, oa, od, om, wts, layer):
    tm = ROW_TILE

    def row(i):
        return (i, 0)

    weights = [wts[k] for k in ("g_pre", "wgl", "wa", "wb", "wc", "wo", "g_post", "g_ffn", "rwt", "rb")]
    xs = list(x) if isinstance(x, tuple) else [x]
    return pl.pallas_call(
        functools.partial(_post_kernel, n_x=len(xs)),
        out_shape=[jax.ShapeDtypeStruct((N_TOK, D_MODEL), F32),
                   jax.ShapeDtypeStruct((N_TOK, D_MODEL), BF16),
                   jax.ShapeDtypeStruct((N_TOK * ACC_ROWS, LANES), F32),
                   jax.ShapeDtypeStruct((8, N_TOK), jnp.int32),
                   jax.ShapeDtypeStruct((8, N_TOK), F32)],
        grid=(N_TOK // tm,),
        in_specs=_x_specs(x) + [_mod_spec(layer),
                  pl.BlockSpec((tm, 512), row), pl.BlockSpec((tm, 512), row),
                  pl.BlockSpec((tm, 512), row)]
                 + [_layer_spec(w, layer, 1) for w in weights],
        out_specs=[pl.BlockSpec((tm, D_MODEL), row), pl.BlockSpec((tm, D_MODEL), row),
                   pl.BlockSpec((tm * ACC_ROWS, LANES), row),
                   pl.BlockSpec((8, tm), lambda i: (0, i)), pl.BlockSpec((8, tm), lambda i: (0, i))],
        compiler_params=pltpu.CompilerParams(vmem_limit_bytes=VMEM_LIMIT),
        name="post_attention",
    )(*xs, mod, oa, od, om, *weights)


def _ffn(x, wg, wu):
    return _silu(_dot(x, wg)) * _dot(x, wu)


def _moe_kernel(start_ref, count_ref, dst_ref, gate_ref, xs_ref, wg_ref, wu_ref, wd_ref,
                acc_ref, xa, xb, ya, yb, wgb, wub, wdb, pend):
    e = pl.program_id(0)

    @pl.when(e == 0)
    def _():
        def clear(i, carry):
            r0 = pl.multiple_of(i * MOE_ZERO_ROWS, MOE_ZERO_ROWS)
            acc_ref[pl.ds(r0, MOE_ZERO_ROWS), :] = jnp.zeros((MOE_ZERO_ROWS, LANES), F32)
            return carry

        lax.fori_loop(0, acc_ref.shape[0] // MOE_ZERO_ROWS, clear, 0)

        yb[...] = jnp.zeros_like(yb)
        pend[0] = MOE_LIST_PAD_BASE

    wgb[...] = wg_ref[0].astype(BF16)
    wub[...] = wu_ref[0].astype(BF16)
    wdb[...] = wd_ref[0].astype(BF16)
    start = start_ref[e]
    pairs = (count_ref[e] + 2 * MOE_SUB - 1) // (2 * MOE_SUB)

    def gather(base, xbuf):
        for mi in range(MOE_SUB):
            src = pl.multiple_of(jnp.minimum(dst_ref[base + mi], (N_TOK - 1) * ACC_ROWS), ACC_ROWS)
            xbuf[mi * ACC_ROWS:(mi + 1) * ACC_ROWS, :] = xs_ref[pl.ds(src, ACC_ROWS), :]

    def expert_ffn(xbuf, ybuf):
        x = jnp.concatenate([xbuf[pl.ds(c, MOE_SUB, stride=ACC_ROWS), :].astype(BF16)
                             for c in range(ACC_ROWS)], axis=1)
        hid = _ffn(x, wgb[...], wub[...])
        y = _dot(hid.astype(BF16), wdb[...])
        for c in range(ACC_ROWS):
            ybuf[pl.ds(c, MOE_SUB, stride=ACC_ROWS), :] = y[:, c * LANES:(c + 1) * LANES]

    def scatter(base, ybuf):
        for g0 in range(0, MOE_SUB, MOE_RMW_GROUP):
            updates = []
            for mi in range(g0, g0 + MOE_RMW_GROUP):
                dst = pl.multiple_of(dst_ref[base + mi], ACC_ROWS)
                yv = ybuf[mi * ACC_ROWS:(mi + 1) * ACC_ROWS, :]
                updates.append((dst, acc_ref[pl.ds(dst, ACC_ROWS), :] + gate_ref[base + mi] * yv))
            for dst, val in updates:
                acc_ref[pl.ds(dst, ACC_ROWS), :] = val

    start_next = start_ref[jnp.minimum(e + 1, N_EXPERTS - 1)]
    prev_count = count_ref[jnp.maximum(e - 1, 0)]

    @pl.when((pairs > 0) & ((e == 0) | (prev_count == 0)))
    def _():
        gather(start, xa)

    def pair(p, carry):
        base = start + p * (2 * MOE_SUB)
        gather(base + MOE_SUB, xb)
        expert_ffn(xa, ya)
        scatter(pend[0], yb)
        gather(jnp.where(p + 1 < pairs, base + 2 * MOE_SUB, start_next), xa)
        expert_ffn(xb, yb)
        scatter(base, ya)
        pend[0] = base + MOE_SUB
        return carry

    lax.fori_loop(0, pairs, pair, 0)

    @pl.when(e == N_EXPERTS - 1)
    def _():
        scatter(pend[0], yb)


def _moe(lists, h2t, wg, wu, wd, layer):
    def expert(e, *_):
        return (layer, e, 0, 0)

    def whole(e, *_):
        return (0, 0)

    acc_rows = (N_TOK + MOE_SPARE_TOKENS) * ACC_ROWS
    assert acc_rows % MOE_ZERO_ROWS == 0
    return pl.pallas_call(
        _moe_kernel,
        out_shape=jax.ShapeDtypeStruct((acc_rows, LANES), F32),
        grid_spec=pltpu.PrefetchScalarGridSpec(
            num_scalar_prefetch=len(lists),
            grid=(N_EXPERTS,),
            in_specs=[pl.BlockSpec(h2t.shape, whole, pipeline_mode=pl.Buffered(1)),
                      pl.BlockSpec((None, 1, D_MODEL, F_EXPERT), expert),
                      pl.BlockSpec((None, 1, D_MODEL, F_EXPERT), expert),
                      pl.BlockSpec((None, 1, F_EXPERT, D_MODEL), expert)],
            out_specs=pl.BlockSpec((acc_rows, LANES), whole, pipeline_mode=pl.Buffered(1)),
            scratch_shapes=[pltpu.VMEM((MOE_SUB * ACC_ROWS, LANES), F32)] * 4
                           + [pltpu.VMEM((D_MODEL, F_EXPERT), BF16),
                              pltpu.VMEM((D_MODEL, F_EXPERT), BF16),
                              pltpu.VMEM((F_EXPERT, D_MODEL), BF16),
                              pltpu.SMEM((1,), jnp.int32)]),
        compiler_params=pltpu.CompilerParams(
            dimension_semantics=("arbitrary",), vmem_limit_bytes=MOE_VMEM_LIMIT),
        name="moe",
    )(*lists, h2t, wg, wu, wd)


def _final_kernel(acc_ref, h2_ref, xmid_ref, mod_ref, swg_ref, swu_ref, swd_ref, gpost_ref, *out_refs):
    tm = h2_ref.shape[0]
    routed = jnp.concatenate(
        [acc_ref[pl.ds(c, tm, stride=ACC_ROWS), :] for c in range(ACC_ROWS)], axis=1)
    hid = _ffn(h2_ref[...], swg_ref[...], swu_ref[...])
    shared = _dot(hid.astype(BF16), swd_ref[...])
    g2 = mod_ref[0][:, 5 * D_MODEL:6 * D_MODEL]
    out = xmid_ref[...] + g2 * _rms(routed + shared, gpost_ref[...])
    if len(out_refs) == 1:
        out_refs[0][...] = out
    else:
        @pl.when(pl.program_id(0) < CTX_TILES)
        def _():
            out_refs[0][...] = out

        @pl.when(pl.program_id(0) >= CTX_TILES)
        def _():
            out_refs[1][...] = out


def _final(acc, h2, xmid, mod, wts, layer, split):
    tm = ROW_TILE

    def row(i):
        return (i, 0)

    weights = [wts[k] for k in ("swg", "swu", "swd", "g_ffn_post")]
    return pl.pallas_call(
        _final_kernel,
        out_shape=([jax.ShapeDtypeStruct((N_CTX, D_MODEL), F32),
                    jax.ShapeDtypeStruct((N_LAT, D_MODEL), F32)] if split
                   else jax.ShapeDtypeStruct((N_TOK, D_MODEL), F32)),
        grid=(N_TOK // tm,),
        in_specs=[pl.BlockSpec((tm * ACC_ROWS, LANES), row),
                  pl.BlockSpec((tm, D_MODEL), row),
                  pl.BlockSpec((tm, D_MODEL), row),
                  _mod_spec(layer)]
                 + [_layer_spec(w, layer, 1) for w in weights],
        out_specs=_x_specs((None, None)) if split else pl.BlockSpec((tm, D_MODEL), row),
        compiler_params=pltpu.CompilerParams(
            dimension_semantics=("arbitrary",), vmem_limit_bytes=VMEM_LIMIT),
        name="ffn_final",
    )(acc, h2, xmid, mod, *weights)


def _routing_lists(experts, gates):
    e = experts[:TOP_K].reshape(-1)
    tok = jnp.tile(jnp.arange(N_TOK, dtype=jnp.int32), TOP_K)
    pad_e = jnp.repeat(jnp.arange(N_EXPERTS, dtype=jnp.int32), MOE_PAD)
    pad_tok = N_TOK + jnp.tile(jnp.arange(MOE_PAD, dtype=jnp.int32), N_EXPERTS)
    keys = jnp.concatenate([e * TOK_KEY + tok, pad_e * TOK_KEY + pad_tok])
    vals = jnp.concatenate([gates[:TOP_K].reshape(-1), jnp.zeros((N_EXPERTS * MOE_PAD,), F32)])
    keys, gate_sorted = lax.sort((keys, vals), num_keys=1)
    tail = jnp.full((MOE_SUB,), N_TOK, jnp.int32)
    tok_sorted = jnp.concatenate([keys & (TOK_KEY - 1), tail])
    gate_sorted = jnp.concatenate([gate_sorted, jnp.zeros((MOE_SUB,), F32)])
    dst = jnp.minimum(tok_sorted, N_TOK) * ACC_ROWS
    counts = jnp.sum((e[None, :] == jnp.arange(N_EXPERTS, dtype=jnp.int32)[:, None]).astype(jnp.int32),
                     axis=1)
    start = jnp.cumsum(counts) - counts + MOE_PAD * jnp.arange(N_EXPERTS, dtype=jnp.int32)
    return start.astype(jnp.int32), counts, dst, gate_sorted


def _rope_tables():
    t = np.arange(LAT_SEQ)
    pos = np.stack([t // GRID_W, t % GRID_W], axis=1).astype(np.float64)

    def table(half):
        inv = ROPE_BASE ** (-np.arange(half, dtype=np.float64) / half)
        lane = np.arange(LANES)
        axis = (lane // (2 * half)) % 2
        freq = inv[lane % half]
        ang = pos[:, axis] * freq[None, :]
        sign = np.where((lane % (2 * half)) < half, -1.0, 1.0)
        cos = np.concatenate([np.ones((ROW_TILE, LANES)), np.cos(ang)], axis=0)
        sin = np.concatenate([np.zeros((ROW_TILE, LANES)), np.sin(ang) * sign[None, :]], axis=0)
        return jnp.asarray(cos, F32), jnp.asarray(sin, F32)

    cos64, sin64 = table(16)
    cos32, sin32 = table(8)
    return cos64, sin64, cos32, sin32


def _prepare_weights(norm_attn_pre, norm_attn_post, norm_ffn_pre, norm_ffn_post, w_in, dif_subln,
                     mla_q_norm, mla_w_uq, mla_kv_norm, mla_w_ukv, w_branch_a, w_branch_b,
                     w_branch_c, w_out, router_w, router_bias, shared_w_gate, shared_w_up,
                     shared_w_down):
    dp = DEPTH
    qa = w_in[:, :, 0:512].reshape(dp, D_MODEL, 2, 4, 64).transpose(0, 1, 3, 2, 4).reshape(dp, D_MODEL, 512)
    kpe = w_in[:, :, 2688:2720]
    w1 = jnp.concatenate([qa, w_in[:, :, 512:2688], kpe, kpe, kpe, kpe], axis=2).astype(BF16)
    uq = mla_w_uq.reshape(dp, 256, 8, 96)
    wuq = jnp.concatenate([uq[..., :64].reshape(dp, 256, 512), uq[..., 64:].reshape(dp, 256, 256)],
                          axis=2).astype(BF16)
    ukv = mla_w_ukv.reshape(dp, 128, 8, 128)
    wukv = jnp.concatenate([ukv[..., :64].reshape(dp, 128, 512), ukv[..., 64:].reshape(dp, 128, 512)],
                           axis=2).astype(BF16)
    wa = w_branch_a.reshape(dp, 2, 4, 64, D_MODEL).transpose(0, 2, 1, 3, 4).reshape(dp, 512, D_MODEL)
    return dict(
        g_pre=norm_attn_pre[:, None, :], g_post=norm_attn_post[:, None, :],
        g_ffn=norm_ffn_pre[:, None, :], g_ffn_post=norm_ffn_post[:, None, :],
        w1=w1, wgl=w_in[:, :, 2720:].astype(BF16),
        qnorm=mla_q_norm[:, None, :], wuq=wuq, kvnorm=mla_kv_norm[:, None, :], wukv=wukv,
        subln=dif_subln[:, None, :],
        wa=wa.astype(BF16), wb=w_branch_b.astype(BF16), wc=w_branch_c.astype(BF16),
        wo=w_out.astype(BF16), rwt=router_w.transpose(0, 2, 1).astype(BF16),
        rb=router_bias[:, :, None],
        swg=shared_w_gate.astype(BF16), swu=shared_w_up.astype(BF16), swd=shared_w_down.astype(BF16))


def kernel(x_prompt, x_sample, cache_swa_k, cache_swa_v, cache_dif_k, cache_dif_v, cache_mla_ckv, cache_mla_kpe, c, c_ctx, w_mod, b_mod, norm_attn_pre, norm_attn_post, norm_ffn_pre, norm_ffn_post, w_in, swa_sink, dif_lq1, dif_lk1, dif_lq2, dif_lk2, dif_subln, mla_q_norm, mla_w_uq, mla_kv_norm, mla_w_ukv, w_branch_a, w_branch_b, w_branch_c, w_out, router_w, router_bias, moe_w_gate, moe_w_up, moe_w_down, shared_w_gate, shared_w_up, shared_w_down):
    x = (x_prompt.reshape(N_CTX, D_MODEL), x_sample.reshape(N_LAT, D_MODEL))
    cvec = jnp.concatenate([c_ctx[None, :], c, jnp.zeros((8 - 1 - N_LAT_BATCH, D_MODEL), F32)], axis=0)
    mod = _modulation(cvec, w_mod, b_mod).reshape(DEPTH, 8, 1, 6 * D_MODEL)
    tables = _rope_tables()
    wts = _prepare_weights(norm_attn_pre, norm_attn_post, norm_ffn_pre, norm_ffn_post, w_in,
                           dif_subln, mla_q_norm, mla_w_uq, mla_kv_norm, mla_w_ukv, w_branch_a,
                           w_branch_b, w_branch_c, w_out, router_w, router_bias, shared_w_gate,
                           shared_w_up, shared_w_down)
    caches = [cache_swa_k.reshape(N_LAT_BATCH, DEPTH, PAST, 128),
              cache_swa_v.reshape(N_LAT_BATCH, DEPTH, PAST, 128),
              cache_dif_k.reshape(N_LAT_BATCH, DEPTH, PAST, 512),
              cache_dif_v.reshape(N_LAT_BATCH, DEPTH, PAST, 512),
              cache_mla_ckv,
              jnp.tile(cache_mla_kpe, (1, 1, 1, 4))]
    lam_init = [0.8 - 0.6 * math.exp(-0.3 * l) for l in range(DEPTH)]
    lam = (jnp.exp(jnp.sum(dif_lq1 * dif_lk1, axis=1)) - jnp.exp(jnp.sum(dif_lq2 * dif_lk2, axis=1))
           + jnp.asarray(lam_init, F32))
    scal = jnp.concatenate([swa_sink, lam[:, None]], axis=1).astype(F32)

    states = None
    for l in range(DEPTH):
        proj = _in_projection(x, mod, wts, tables, l, states)
        states = proj[11:]
        ctx_out = _attention_ctx(scal[l], wts["subln"], proj, states, l, lam_init[l])
        oa, od, om = _attention_lat(scal[l], wts["subln"], wts["wukv"], proj, caches, ctx_out, l,
                                    lam_init[l])
        xmid, h2, h2t, experts, gates = _post_attention(x, mod, oa, od, om, wts, l)
        lists = _routing_lists(experts, gates)
        acc = _moe(lists, h2t, moe_w_gate, moe_w_up, moe_w_down, l)
        x = _final(acc, h2, xmid, mod, wts, l, split=(l == DEPTH - 1))

    y_p = x[0].reshape(N_CTX_BATCH, CTX_SEQ, D_MODEL)
    y_s = x[1].reshape(N_LAT_BATCH, LAT_SEQ, D_MODEL)
    s_ka, s_va, s_kd, s_vd, s_ckv, s_kpe = states
    return (y_p, y_s,
            s_ka.reshape(N_CTX_BATCH, DEPTH, CTX_SEQ, 2, 64),
            s_va.reshape(N_CTX_BATCH, DEPTH, CTX_SEQ, 2, 64),
            s_kd.reshape(N_CTX_BATCH, DEPTH, CTX_SEQ, 4, 2, 64),
            s_vd.reshape(N_CTX_BATCH, DEPTH, CTX_SEQ, 4, 128),
            s_ckv, s_kpe)
```

```python
import functools
import math

import numpy as np
import jax
import jax.numpy as jnp
from jax import lax
from jax.experimental import pallas as pl
from jax.experimental.pallas import tpu as pltpu

F32 = jnp.float32
BF16 = jnp.bfloat16

D_MODEL = 1024
N_CTX_BATCH, CTX_SEQ = 16, 256
N_LAT_BATCH, LAT_SEQ = 2, 1024
PAST = 256
N_CTX = N_CTX_BATCH * CTX_SEQ
N_LAT = N_LAT_BATCH * LAT_SEQ
N_TOK = N_CTX + N_LAT
DEPTH = 2
GRID_W = 64
WINDOW = 128
N_EXPERTS = 64
N_GROUPS = 8
TOPK_GROUPS = 4
TOP_K = 6
F_EXPERT = 256
ROUTED_SCALE = 2.5
ROPE_BASE = 10000.0
EPS = 1e-6
NEG = -1e30

LANES = 128
ROW_TILE = 256
Q_BLOCK = 256
VMEM_LIMIT = 56 * 1024 * 1024
ACC_ROWS = 8
MOE_SPARE_TOKENS = 8
MOE_ZERO_ROWS = 64
MOE_VMEM_LIMIT = 62 * 1024 * 1024
MOE_SUB = 128
MOE_RMW_GROUP = 8
MOE_PAD = 2 * MOE_SUB
TOK_KEY = 8192
MOE_LIST_PAD_BASE = N_TOK * TOP_K + N_EXPERTS * MOE_PAD

C_QA, C_KA, C_VA, C_QD, C_KD, C_VD, C_QC, C_KVC, C_KPE, C_END = (
    0, 512, 640, 768, 1280, 1792, 2304, 2560, 2688, 2816)

SCALE_64 = 1.0 / math.sqrt(64.0)
SCALE_96 = 1.0 / math.sqrt(96.0)

CTX_TILES = N_CTX // ROW_TILE
assert ROW_TILE == CTX_SEQ and LAT_SEQ % ROW_TILE == 0


def _mod_row_of_tile(i):
    return jnp.where(i < CTX_TILES, 0, 1 + (i - CTX_TILES) // (LAT_SEQ // ROW_TILE))


def _rms(x, g):
    return x * lax.rsqrt(jnp.mean(x * x, axis=-1, keepdims=True) + EPS) * g


def _dot(a, b):
    return jnp.dot(a, b, preferred_element_type=F32)


def _dot_nt(a, b):
    return lax.dot_general(a, b, (((1,), (1,)), ((), ())), preferred_element_type=F32)


def _silu(x):
    return x * jax.nn.sigmoid(x)


def _layer_spec(arr, layer, grid_rank):
    zeros = (0,) * (arr.ndim - 1)
    return pl.BlockSpec((None,) + arr.shape[1:], lambda *_: (layer,) + zeros)


def _mod_kernel(c_ref, w_ref, b_ref, o_ref):
    c = c_ref[...]
    o_ref[0] = _dot(_silu(c).astype(BF16), w_ref[0].astype(BF16)) + b_ref[0]


def _modulation(cvec, w_mod, b_mod):
    tn = 1536
    n = w_mod.shape[-1]
    return pl.pallas_call(
        _mod_kernel,
        out_shape=jax.ShapeDtypeStruct((DEPTH, 8, n), F32),
        grid=(DEPTH, n // tn),
        in_specs=[
            pl.BlockSpec((8, D_MODEL), lambda l, j: (0, 0)),
            pl.BlockSpec((1, D_MODEL, tn), lambda l, j: (l, 0, j)),
            pl.BlockSpec((1, 1, tn), lambda l, j: (l, 0, j)),
        ],
        out_specs=pl.BlockSpec((1, 8, tn), lambda l, j: (l, 0, j)),
        compiler_params=pltpu.CompilerParams(vmem_limit_bytes=VMEM_LIMIT),
        name="modulation",
    )(cvec, w_mod, b_mod.reshape(DEPTH, 1, n))


def _x_specs(x):
    if isinstance(x, tuple):
        return [pl.BlockSpec((ROW_TILE, D_MODEL), lambda i: (jnp.minimum(i, CTX_TILES - 1), 0)),
                pl.BlockSpec((ROW_TILE, D_MODEL), lambda i: (jnp.maximum(i - CTX_TILES, 0), 0))]
    return [pl.BlockSpec((ROW_TILE, D_MODEL), lambda i: (i, 0))]


def _load_x(x_refs):
    if len(x_refs) == 1:
        return x_refs[0][...]
    return jnp.where(pl.program_id(0) < CTX_TILES, x_refs[0][...], x_refs[1][...])


def _mod_spec(layer):
    return pl.BlockSpec((None, 1, 1, 6 * D_MODEL), lambda i: (layer, _mod_row_of_tile(i), 0, 0))


def _rope128(x, cos, sin, half):
    lane = lax.broadcasted_iota(jnp.int32, x.shape, 1)
    first = (lane % (2 * half)) < half
    partner = jnp.where(first, pltpu.roll(x, LANES - half, 1), pltpu.roll(x, half, 1))
    return x * cos + partner * sin


def _rope_cols(x, cos, sin, half):
    chunks = [_rope128(x[:, c:c + LANES], cos, sin, half) for c in range(0, x.shape[1], LANES)]
    return chunks[0] if len(chunks) == 1 else jnp.concatenate(chunks, axis=1)


N_IN_INPUTS = 11


def _in_kernel(*refs, n_x):
    (mod_ref, g_ref, w1_ref, qn_ref, wuq_ref, kvn_ref, wukv_ref,
     cos64_ref, sin64_ref, cos32_ref, sin32_ref) = refs[n_x:n_x + N_IN_INPUTS]
    (qa_o, qd_o, qmn_o, qmr_o, kmn_o, vm_o, kpe4_o, ka_l, va_l, kd_l, vd_l,
     s_ka, s_va, s_kd, s_vd, s_ckv, s_kpe) = refs[-17:]
    i = pl.program_id(0)
    x = _load_x(refs[:n_x])
    m = mod_ref[0]
    sh1, sc1 = m[:, 0:D_MODEL], m[:, D_MODEL:2 * D_MODEL]
    h = _rms(x, g_ref[...]) * (1.0 + sc1) + sh1
    z = _dot(h.astype(BF16), w1_ref[...])
    cos64, sin64 = cos64_ref[...], sin64_ref[...]
    cos32, sin32 = cos32_ref[...], sin32_ref[...]
    qa_o[...] = (_rope_cols(z[:, C_QA:C_KA], cos64, sin64, 16) * SCALE_64).astype(BF16)
    qd_o[...] = (_rope_cols(z[:, C_QD:C_KD], cos64, sin64, 16) * SCALE_64).astype(BF16)
    ka = _rope_cols(z[:, C_KA:C_VA], cos64, sin64, 16)
    va = z[:, C_VA:C_QD]
    kd = _rope_cols(z[:, C_KD:C_VD], cos64, sin64, 16)
    vd = z[:, C_VD:C_QC]
    kpe4 = _rope_cols(z[:, C_KPE:C_END], cos32, sin32, 8)
    kpe4_o[...] = kpe4.astype(BF16)
    qn = _rms(z[:, C_QC:C_KVC], qn_ref[...])
    qm = _dot(qn.astype(BF16), wuq_ref[...]) * SCALE_96
    qmn_o[...] = qm[:, 0:512].astype(BF16)
    qmr_o[...] = _rope_cols(qm[:, 512:768], cos32, sin32, 8).astype(BF16)
    ckv = _rms(z[:, C_KVC:C_KPE], kvn_ref[...])
    kv = _dot(ckv.astype(BF16), wukv_ref[...])
    kmn_o[...] = kv[:, 0:512].astype(BF16)
    vm_o[...] = kv[:, 512:1024].astype(BF16)

    @pl.when(i < CTX_TILES)
    def _():
        s_ka[...] = ka
        s_va[...] = va
        s_kd[...] = kd
        s_vd[...] = vd
        s_ckv[...] = ckv
        s_kpe[...] = kpe4[:, 0:32]

    @pl.when(i >= CTX_TILES)
    def _():
        ka_l[...] = ka.astype(BF16)
        va_l[...] = va.astype(BF16)
        kd_l[...] = kd.astype(BF16)
        vd_l[...] = vd.astype(BF16)


STATE_WIDTHS = (128, 128, 512, 512, 128, 32)


def _in_projection(x, mod, wts, tables, layer, prev_states):
    tm = ROW_TILE

    def row(i):
        return (i, 0)

    def table_row(i):
        return (jnp.where(i < CTX_TILES, 0, 1 + (i - CTX_TILES) % (LAT_SEQ // tm)), 0)

    def lat_row(i):
        return (jnp.maximum(i - CTX_TILES, 0), 0)

    def state_block(i):
        return (jnp.minimum(i, CTX_TILES - 1), layer, 0, 0)

    table_spec = pl.BlockSpec((tm, LANES), table_row)
    weights = [wts["g_pre"], wts["w1"], wts["qnorm"], wts["wuq"], wts["kvnorm"], wts["wukv"]]
    all_rows = [(512, BF16)] * 3 + [(256, BF16), (512, BF16), (512, BF16), (128, BF16)]
    lat_rows = [(128, BF16), (128, BF16), (512, BF16), (512, BF16)]
    aliased = [] if prev_states is None else list(prev_states)
    xs = list(x) if isinstance(x, tuple) else [x]
    n_in = len(xs) + N_IN_INPUTS
    n_plain_out = len(all_rows) + len(lat_rows)
    return pl.pallas_call(
        functools.partial(_in_kernel, n_x=len(xs)),
        out_shape=[jax.ShapeDtypeStruct((N_TOK, w), dt) for w, dt in all_rows]
                  + [jax.ShapeDtypeStruct((N_LAT, w), dt) for w, dt in lat_rows]
                  + [jax.ShapeDtypeStruct((N_CTX_BATCH, DEPTH, CTX_SEQ, w), F32) for w in STATE_WIDTHS],
        grid=(N_TOK // tm,),
        in_specs=_x_specs(x) + [_mod_spec(layer)]
                 + [_layer_spec(w, layer, 1) for w in weights]
                 + [table_spec] * 4
                 + [pl.BlockSpec(memory_space=pl.ANY)] * len(aliased),
        out_specs=[pl.BlockSpec((tm, w), row) for w, _ in all_rows]
                  + [pl.BlockSpec((tm, w), lat_row) for w, _ in lat_rows]
                  + [pl.BlockSpec((None, None, CTX_SEQ, w), state_block) for w in STATE_WIDTHS],
        input_output_aliases={n_in + k: n_plain_out + k for k in range(len(aliased))},
        compiler_params=pltpu.CompilerParams(
            dimension_semantics=("arbitrary",), vmem_limit_bytes=VMEM_LIMIT),
        name="in_projection",
    )(*xs, mod, *weights, *tables, *aliased)


def _attend(q, parts, sink=None):
    scores = []
    for k, _, mask in parts:
        s = _dot_nt(q, k)
        if mask is not None:
            s = jnp.where(mask, s, NEG)
        scores.append(s)
    m = functools.reduce(jnp.maximum, [jnp.max(s, axis=-1, keepdims=True) for s in scores])
    if sink is not None:
        m = jnp.maximum(m, sink)
    den = None
    out = None
    for s, (_, v, _) in zip(scores, parts):
        e = jnp.exp(s - m)
        d = jnp.sum(e, axis=-1, keepdims=True)
        o = _dot(e.astype(BF16), v)
        den = d if den is None else den + d
        out = o if out is None else out + o
    if sink is not None:
        den = den + jnp.exp(sink - m)
    return out / den


A_HEADS = 8


def _attend_heads(qs, parts, sinks, stack):
    rows = qs[0].shape[0]
    if not stack:
        return [_attend(q, parts, None if sinks is None else sinks[i]) for i, q in enumerate(qs)]
    sink = None
    if sinks is not None:
        sink = jnp.concatenate([jnp.full((rows, 1), s, F32) for s in sinks], axis=0)
    o = _attend(jnp.concatenate(qs, axis=0), parts, sink)
    return [o[i * rows:(i + 1) * rows] for i in range(len(qs))]


def _three_mixers(scal_ref, subln_ref, q_blocks, kv_parts, oa_o, od_o, om_o, lam_init, stack):
    qa, qd, qmn, qmr = q_blocks
    rows = qa.shape[0]
    lane = lax.broadcasted_iota(jnp.int32, (rows, LANES), 1)
    low = lane < 64
    zero = jnp.zeros((rows, LANES), BF16)

    parts = [(p["ka"], p["va"], p.get("mask_a")) for p in kv_parts]
    qs = [jnp.where(low if kvh == 0 else ~low, qa[:, g * LANES:(g + 1) * LANES], zero)
          for kvh in range(2) for g in range(4)]
    out_a = _attend_heads(qs, parts, [scal_ref[h] for h in range(A_HEADS)], stack)
    for g in range(4):
        oa_o[:, g * LANES:(g + 1) * LANES] = jnp.where(low, out_a[g], out_a[4 + g]).astype(BF16)

    lam = scal_ref[8]
    subln = subln_ref[...]
    for h in range(4):
        sl = slice(h * LANES, (h + 1) * LANES)
        q128 = qd[:, sl]
        o1, o2 = _attend_heads([jnp.where(low, q128, zero), jnp.where(low, zero, q128)],
                               [(p["kd"][:, sl], p["vd"][:, sl], None) for p in kv_parts],
                               None, stack)
        od = o1 - lam * o2
        od_o[:, sl] = (_rms(od, subln) * (1.0 - lam_init)).astype(BF16)

    for i in range(4):
        sl = slice(i * LANES, (i + 1) * LANES)
        qs = []
        for sub in range(2):
            h = 2 * i + sub
            qn = jnp.where(low if sub == 0 else ~low, qmn[:, sl], zero)
            qr128 = qmr[:, (h // 4) * LANES:(h // 4 + 1) * LANES]
            qr = jnp.where((lane // 32) == (h % 4), qr128, zero)
            qs.append(jnp.concatenate([qn, qr], axis=1))
        parts = [(jnp.concatenate([p["kmn"][:, sl], p["kpe"]], axis=1), p["vm"][:, sl], None)
                 for p in kv_parts]
        o_even, o_odd = _attend_heads(qs, parts, None, stack)
        om_o[:, sl] = jnp.where(low, o_even, o_odd).astype(BF16)


def _attn_ctx_kernel(scal_ref, subln_ref, qa_ref, qd_ref, qmn_ref, qmr_ref,
                     ka_ref, va_ref, kd_ref, vd_ref, kmn_ref, vm_ref, kpe_ref,
                     oa_o, od_o, om_o, *, lam_init):
    part = dict(ka=ka_ref[...].astype(BF16), va=va_ref[...].astype(BF16),
                kd=kd_ref[...].astype(BF16), vd=vd_ref[...].astype(BF16),
                kmn=kmn_ref[...], vm=vm_ref[...], kpe=kpe_ref[...])
    _three_mixers(scal_ref, subln_ref, (qa_ref[...], qd_ref[...], qmn_ref[...], qmr_ref[...]),
                  [part], oa_o, od_o, om_o, lam_init, stack=False)


def _attn_lat_kernel(scal_ref, subln_ref, wukv_ref, qa_ref, qd_ref, qmn_ref, qmr_ref,
                     ka_ref, va_ref, kd_ref, vd_ref, kmn_ref, vm_ref, kpe_ref,
                     cka_ref, cva_ref, ckd_ref, cvd_ref, cckv_ref, ckpe_ref,
                     oa_in, od_in, om_in, oa_o, od_o, om_o, *, lam_init):
    del oa_in, od_in, om_in
    n = pl.program_id(1)
    kv_c = _dot(cckv_ref[...].astype(BF16), wukv_ref[...])
    cached = dict(ka=cka_ref[...].astype(BF16), va=cva_ref[...].astype(BF16),
                  kd=ckd_ref[...].astype(BF16), vd=cvd_ref[...].astype(BF16),
                  kmn=kv_c[:, 0:512].astype(BF16), vm=kv_c[:, 512:1024].astype(BF16),
                  kpe=ckpe_ref[...].astype(BF16))
    span = Q_BLOCK + 2 * WINDOW
    start = pl.multiple_of(jnp.clip(n * Q_BLOCK - WINDOW, 0, LAT_SEQ - span), WINDOW)
    qpos = n * Q_BLOCK + lax.broadcasted_iota(jnp.int32, (A_HEADS * Q_BLOCK, span), 0) % Q_BLOCK
    kpos = start + lax.broadcasted_iota(jnp.int32, (A_HEADS * Q_BLOCK, span), 1)
    mask_a = jnp.abs(qpos - kpos) <= WINDOW
    new = dict(ka=ka_ref[pl.ds(start, span), :], va=va_ref[pl.ds(start, span), :], mask_a=mask_a,
               kd=kd_ref[...], vd=vd_ref[...], kmn=kmn_ref[...], vm=vm_ref[...], kpe=kpe_ref[...])
    _three_mixers(scal_ref, subln_ref, (qa_ref[...], qd_ref[...], qmn_ref[...], qmr_ref[...]),
                  [cached, new], oa_o, od_o, om_o, lam_init, stack=True)


_SMEM_SPEC = pl.BlockSpec(memory_space=pltpu.SMEM)


def _attention_ctx(scal, subln, proj, states, layer, lam_init):
    qa, qd, qmn, qmr, kmn, vm, kpe4 = proj[:7]
    s_ka, s_va, s_kd, s_vd = states[:4]
    t = CTX_SEQ

    def blk(a):
        return pl.BlockSpec((t, a.shape[1]), lambda b: (b, 0))

    def sblk(a):
        return pl.BlockSpec((None, None, t, a.shape[-1]), lambda b: (b, layer, 0, 0))

    return pl.pallas_call(
        functools.partial(_attn_ctx_kernel, lam_init=lam_init),
        out_shape=[jax.ShapeDtypeStruct((N_TOK, 512), BF16)] * 3,
        grid=(N_CTX_BATCH,),
        in_specs=[_SMEM_SPEC, _layer_spec(subln, layer, 1)]
                 + [blk(a) for a in (qa, qd, qmn, qmr)]
                 + [sblk(a) for a in (s_ka, s_va, s_kd, s_vd)]
                 + [blk(a) for a in (kmn, vm, kpe4)],
        out_specs=[pl.BlockSpec((t, 512), lambda b: (b, 0))] * 3,
        compiler_params=pltpu.CompilerParams(vmem_limit_bytes=VMEM_LIMIT),
        name="attention_ctx",
    )(scal, subln, qa, qd, qmn, qmr, s_ka, s_va, s_kd, s_vd, kmn, vm, kpe4)


def _attention_lat(scal, subln, wukv, proj, caches, ctx_out, layer, lam_init):
    qa, qd, qmn, qmr, kmn, vm, kpe4, ka_l, va_l, kd_l, vd_l = proj[:11]
    nq = LAT_SEQ // Q_BLOCK
    q_off = N_CTX // Q_BLOCK
    kv_off = N_CTX // LAT_SEQ

    def qblk(a):
        return pl.BlockSpec((Q_BLOCK, a.shape[1]), lambda b, n: (q_off + b * nq + n, 0))

    def kvblk(a):
        return pl.BlockSpec((LAT_SEQ, a.shape[1]), lambda b, n: (b, 0))

    def kvblk_all(a):
        return pl.BlockSpec((LAT_SEQ, a.shape[1]), lambda b, n: (kv_off + b, 0))

    def cblk(a):
        return pl.BlockSpec((None, None, PAST, a.shape[-1]), lambda b, n: (b, layer, 0, 0))

    qs = [qa, qd, qmn, qmr]
    n_in = 3 + 4 + 7 + 6
    out_spec = pl.BlockSpec((Q_BLOCK, 512), lambda b, n: (q_off + b * nq + n, 0))
    return pl.pallas_call(
        functools.partial(_attn_lat_kernel, lam_init=lam_init),
        out_shape=[jax.ShapeDtypeStruct((N_TOK, 512), BF16)] * 3,
        grid=(N_LAT_BATCH, nq),
        in_specs=[_SMEM_SPEC, _layer_spec(subln, layer, 2), _layer_spec(wukv, layer, 2)]
                 + [qblk(a) for a in qs]
                 + [kvblk(a) for a in (ka_l, va_l, kd_l, vd_l)]
                 + [kvblk_all(a) for a in (kmn, vm, kpe4)]
                 + [cblk(a) for a in caches]
                 + [pl.BlockSpec(memory_space=pl.ANY)] * 3,
        out_specs=[out_spec] * 3,
        input_output_aliases={n_in + k: k for k in range(3)},
        compiler_params=pltpu.CompilerParams(vmem_limit_bytes=VMEM_LIMIT),
        name="attention_lat",
    )(scal, subln, wukv, *qs, ka_l, va_l, kd_l, vd_l, kmn, vm, kpe4, *caches, *ctx_out)


def _route(scores, biased):
    per = N_EXPERTS // N_GROUPS
    tm = scores.shape[1]
    sub = lax.broadcasted_iota(jnp.int32, (per, tm), 0).astype(F32)
    groups = [biased[g * per:(g + 1) * per, :] for g in range(N_GROUPS)]
    gscore = []
    for v in groups:
        m1 = jnp.max(v, axis=0, keepdims=True)
        first = jnp.min(jnp.where(v == m1, sub, float(per)), axis=0, keepdims=True)
        m2 = jnp.max(jnp.where(sub == first, -jnp.inf, v), axis=0, keepdims=True)
        gscore.append(m1 + m2)
    vals = []
    for g in range(N_GROUPS):
        rank = jnp.zeros((1, tm), F32)
        for o in range(N_GROUPS):
            if o == g:
                continue
            ahead = (gscore[o] >= gscore[g]) if o < g else (gscore[o] > gscore[g])
            rank = rank + jnp.where(ahead, 1.0, 0.0)
        vals.append(jnp.where(rank < TOPK_GROUPS, groups[g], NEG))
    idx = [sub + float(g * per) for g in range(N_GROUPS)]
    picks, weights = [], []
    for _ in range(TOP_K):
        best = functools.reduce(jnp.maximum, [jnp.max(v, axis=0, keepdims=True) for v in vals])
        cand = functools.reduce(jnp.minimum, [
            jnp.min(jnp.where(v == best, i, float(N_EXPERTS)), axis=0, keepdims=True)
            for v, i in zip(vals, idx)])
        wsel = jnp.zeros((1, tm), F32)
        for g in range(N_GROUPS):
            hit = idx[g] == cand
            wsel = wsel + jnp.sum(jnp.where(hit, scores[g * per:(g + 1) * per, :], 0.0),
                                  axis=0, keepdims=True)
            vals[g] = jnp.where(hit, -jnp.inf, vals[g])
        picks.append(cand)
        weights.append(wsel)
    total = functools.reduce(lambda a, b: a + b, weights)
    pad = [jnp.zeros((1, tm), F32)] * (8 - TOP_K)
    experts = jnp.concatenate(picks + pad, axis=0).astype(jnp.int32)
    gates = jnp.concatenate([x / total * ROUTED_SCALE for x in weights] + pad, axis=0)
    return experts, gates


def _post_kernel(*refs, n_x):
    (mod_ref, oa_ref, od_ref, om_ref, gpre_ref, wgl_ref, wa_ref, wb_ref, wc_ref, wo_ref, gpost_ref,
     gffn_ref, rwt_ref, rb_ref, xmid_o, h2_o, h2t_o, experts_o, gates_o) = refs[n_x:]
    x = _load_x(refs[:n_x])
    m = mod_ref[0]
    d = D_MODEL
    sh1, sc1, g1, sh2, sc2 = (m[:, 0:d], m[:, d:2 * d], m[:, 2 * d:3 * d], m[:, 3 * d:4 * d],
                              m[:, 4 * d:5 * d])
    h = _rms(x, gpre_ref[...]) * (1.0 + sc1) + sh1
    gate = jax.nn.sigmoid(_dot(h.astype(BF16), wgl_ref[...]))
    merged = (gate[:, 0:d] * _dot(oa_ref[...], wa_ref[...])
              + gate[:, d:2 * d] * _dot(od_ref[...], wb_ref[...])
              + gate[:, 2 * d:3 * d] * _dot(om_ref[...], wc_ref[...]))
    a = _dot(merged.astype(BF16), wo_ref[...])
    xm = x + g1 * _rms(a, gpost_ref[...])
    xmid_o[...] = xm
    h2f = _rms(xm, gffn_ref[...]) * (1.0 + sc2) + sh2
    h2 = h2f.astype(BF16)
    h2_o[...] = h2
    for c in range(ACC_ROWS):
        h2t_o[pl.ds(c, x.shape[0], stride=ACC_ROWS), :] = h2f[:, c * LANES:(c + 1) * LANES]
    scores = jax.nn.sigmoid(_dot_nt(rwt_ref[...], h2))
    experts, gates = _route(scores, scores + rb_ref[...])
    experts_o[...] = experts
    gates_o[...] = gates


def _post_attention(x, mod, oa, od, om, wts, layer):
    tm = ROW_TILE

    def row(i):
        return (i, 0)

    weights = [wts[k] for k in ("g_pre", "wgl", "wa", "wb", "wc", "wo", "g_post", "g_ffn", "rwt", "rb")]
    xs = list(x) if isinstance(x, tuple) else [x]
    return pl.pallas_call(
        functools.partial(_post_kernel, n_x=len(xs)),
        out_shape=[jax.ShapeDtypeStruct((N_TOK, D_MODEL), F32),
                   jax.ShapeDtypeStruct((N_TOK, D_MODEL), BF16),
                   jax.ShapeDtypeStruct((N_TOK * ACC_ROWS, LANES), F32),
                   jax.ShapeDtypeStruct((8, N_TOK), jnp.int32),
                   jax.ShapeDtypeStruct((8, N_TOK), F32)],
        grid=(N_TOK // tm,),
        in_specs=_x_specs(x) + [_mod_spec(layer),
                  pl.BlockSpec((tm, 512), row), pl.BlockSpec((tm, 512), row),
                  pl.BlockSpec((tm, 512), row)]
                 + [_layer_spec(w, layer, 1) for w in weights],
        out_specs=[pl.BlockSpec((tm, D_MODEL), row), pl.BlockSpec((tm, D_MODEL), row),
                   pl.BlockSpec((tm * ACC_ROWS, LANES), row),
                   pl.BlockSpec((8, tm), lambda i: (0, i)), pl.BlockSpec((8, tm), lambda i: (0, i))],
        compiler_params=pltpu.CompilerParams(vmem_limit_bytes=VMEM_LIMIT),
        name="post_attention",
    )(*xs, mod, oa, od, om, *weights)


def _ffn(x, wg, wu):
    return _silu(_dot(x, wg)) * _dot(x, wu)


def _moe_kernel(start_ref, count_ref, dst_ref, gate_ref, xs_ref, wg_ref, wu_ref, wd_ref,
                acc_ref, xa, xb, ya, yb, wgb, wub, wdb, pend):
    e = pl.program_id(0)

    @pl.when(e == 0)
    def _():
        def clear(i, carry):
            r0 = pl.multiple_of(i * MOE_ZERO_ROWS, MOE_ZERO_ROWS)
            acc_ref[pl.ds(r0, MOE_ZERO_ROWS), :] = jnp.zeros((MOE_ZERO_ROWS, LANES), F32)
            return carry

        lax.fori_loop(0, acc_ref.shape[0] // MOE_ZERO_ROWS, clear, 0)

        yb[...] = jnp.zeros_like(yb)
        pend[0] = MOE_LIST_PAD_BASE

    wgb[...] = wg_ref[0].astype(BF16)
    wub[...] = wu_ref[0].astype(BF16)
    wdb[...] = wd_ref[0].astype(BF16)
    start = start_ref[e]
    pairs = (count_ref[e] + 2 * MOE_SUB - 1) // (2 * MOE_SUB)

    def gather(base, xbuf):
        for mi in range(MOE_SUB):
            src = pl.multiple_of(jnp.minimum(dst_ref[base + mi], (N_TOK - 1) * ACC_ROWS), ACC_ROWS)
            xbuf[mi * ACC_ROWS:(mi + 1) * ACC_ROWS, :] = xs_ref[pl.ds(src, ACC_ROWS), :]

    def expert_ffn(xbuf, ybuf):
        x = jnp.concatenate([xbuf[pl.ds(c, MOE_SUB, stride=ACC_ROWS), :].astype(BF16)
                             for c in range(ACC_ROWS)], axis=1)
        hid = _ffn(x, wgb[...], wub[...])
        y = _dot(hid.astype(BF16), wdb[...])
        for c in range(ACC_ROWS):
            ybuf[pl.ds(c, MOE_SUB, stride=ACC_ROWS), :] = y[:, c * LANES:(c + 1) * LANES]

    def scatter(base, ybuf):
        for g0 in range(0, MOE_SUB, MOE_RMW_GROUP):
            updates = []
            for mi in range(g0, g0 + MOE_RMW_GROUP):
                dst = pl.multiple_of(dst_ref[base + mi], ACC_ROWS)
                yv = ybuf[mi * ACC_ROWS:(mi + 1) * ACC_ROWS, :]
                updates.append((dst, acc_ref[pl.ds(dst, ACC_ROWS), :] + gate_ref[base + mi] * yv))
            for dst, val in updates:
                acc_ref[pl.ds(dst, ACC_ROWS), :] = val

    start_next = start_ref[jnp.minimum(e + 1, N_EXPERTS - 1)]
    prev_count = count_ref[jnp.maximum(e - 1, 0)]

    @pl.when((pairs > 0) & ((e == 0) | (prev_count == 0)))
    def _():
        gather(start, xa)

    def pair(p, carry):
        base = start + p * (2 * MOE_SUB)
        gather(base + MOE_SUB, xb)
        expert_ffn(xa, ya)
        scatter(pend[0], yb)
        gather(jnp.where(p + 1 < pairs, base + 2 * MOE_SUB, start_next), xa)
        expert_ffn(xb, yb)
        scatter(base, ya)
        pend[0] = base + MOE_SUB
        return carry

    lax.fori_loop(0, pairs, pair, 0)

    @pl.when(e == N_EXPERTS - 1)
    def _():
        scatter(pend[0], yb)


def _moe(lists, h2t, wg, wu, wd, layer):
    def expert(e, *_):
        return (layer, e, 0, 0)

    def whole(e, *_):
        return (0, 0)

    acc_rows = (N_TOK + MOE_SPARE_TOKENS) * ACC_ROWS
    assert acc_rows % MOE_ZERO_ROWS == 0
    return pl.pallas_call(
        _moe_kernel,
        out_shape=jax.ShapeDtypeStruct((acc_rows, LANES), F32),
        grid_spec=pltpu.PrefetchScalarGridSpec(
            num_scalar_prefetch=len(lists),
            grid=(N_EXPERTS,),
            in_specs=[pl.BlockSpec(h2t.shape, whole, pipeline_mode=pl.Buffered(1)),
                      pl.BlockSpec((None, 1, D_MODEL, F_EXPERT), expert),
                      pl.BlockSpec((None, 1, D_MODEL, F_EXPERT), expert),
                      pl.BlockSpec((None, 1, F_EXPERT, D_MODEL), expert)],
            out_specs=pl.BlockSpec((acc_rows, LANES), whole, pipeline_mode=pl.Buffered(1)),
            scratch_shapes=[pltpu.VMEM((MOE_SUB * ACC_ROWS, LANES), F32)] * 4
                           + [pltpu.VMEM((D_MODEL, F_EXPERT), BF16),
                              pltpu.VMEM((D_MODEL, F_EXPERT), BF16),
                              pltpu.VMEM((F_EXPERT, D_MODEL), BF16),
                              pltpu.SMEM((1,), jnp.int32)]),
        compiler_params=pltpu.CompilerParams(
            dimension_semantics=("arbitrary",), vmem_limit_bytes=MOE_VMEM_LIMIT),
        name="moe",
    )(*lists, h2t, wg, wu, wd)


def _final_kernel(acc_ref, h2_ref, xmid_ref, mod_ref, swg_ref, swu_ref, swd_ref, gpost_ref, *out_refs):
    tm = h2_ref.shape[0]
    routed = jnp.concatenate(
        [acc_ref[pl.ds(c, tm, stride=ACC_ROWS), :] for c in range(ACC_ROWS)], axis=1)
    hid = _ffn(h2_ref[...], swg_ref[...], swu_ref[...])
    shared = _dot(hid.astype(BF16), swd_ref[...])
    g2 = mod_ref[0][:, 5 * D_MODEL:6 * D_MODEL]
    out = xmid_ref[...] + g2 * _rms(routed + shared, gpost_ref[...])
    if len(out_refs) == 1:
        out_refs[0][...] = out
    else:
        @pl.when(pl.program_id(0) < CTX_TILES)
        def _():
            out_refs[0][...] = out

        @pl.when(pl.program_id(0) >= CTX_TILES)
        def _():
            out_refs[1][...] = out


def _final(acc, h2, xmid, mod, wts, layer, split):
    tm = ROW_TILE

    def row(i):
        return (i, 0)

    weights = [wts[k] for k in ("swg", "swu", "swd", "g_ffn_post")]
    return pl.pallas_call(
        _final_kernel,
        out_shape=([jax.ShapeDtypeStruct((N_CTX, D_MODEL), F32),
                    jax.ShapeDtypeStruct((N_LAT, D_MODEL), F32)] if split
                   else jax.ShapeDtypeStruct((N_TOK, D_MODEL), F32)),
        grid=(N_TOK // tm,),
        in_specs=[pl.BlockSpec((tm * ACC_ROWS, LANES), row),
                  pl.BlockSpec((tm, D_MODEL), row),
                  pl.BlockSpec((tm, D_MODEL), row),
                  _mod_spec(layer)]
                 + [_layer_spec(w, layer, 1) for w in weights],
        out_specs=_x_specs((None, None)) if split else pl.BlockSpec((tm, D_MODEL), row),
        compiler_params=pltpu.CompilerParams(
            dimension_semantics=("arbitrary",), vmem_limit_bytes=VMEM_LIMIT),
        name="ffn_final",
    )(acc, h2, xmid, mod, *weights)


def _routing_lists(experts, gates):
    e = experts[:TOP_K].reshape(-1)
    tok = jnp.tile(jnp.arange(N_TOK, dtype=jnp.int32), TOP_K)
    pad_e = jnp.repeat(jnp.arange(N_EXPERTS, dtype=jnp.int32), MOE_PAD)
    pad_tok = N_TOK + jnp.tile(jnp.arange(MOE_PAD, dtype=jnp.int32), N_EXPERTS)
    keys = jnp.concatenate([e * TOK_KEY + tok, pad_e * TOK_KEY + pad_tok])
    vals = jnp.concatenate([gates[:TOP_K].reshape(-1), jnp.zeros((N_EXPERTS * MOE_PAD,), F32)])
    keys, gate_sorted = lax.sort((keys, vals), num_keys=1)
    tail = jnp.full((MOE_SUB,), N_TOK, jnp.int32)
    tok_sorted = jnp.concatenate([keys & (TOK_KEY - 1), tail])
    gate_sorted = jnp.concatenate([gate_sorted, jnp.zeros((MOE_SUB,), F32)])
    dst = jnp.minimum(tok_sorted, N_TOK) * ACC_ROWS
    counts = jnp.sum((e[None, :] == jnp.arange(N_EXPERTS, dtype=jnp.int32)[:, None]).astype(jnp.int32),
                     axis=1)
    start = jnp.cumsum(counts) - counts + MOE_PAD * jnp.arange(N_EXPERTS, dtype=jnp.int32)
    return start.astype(jnp.int32), counts, dst, gate_sorted


def _rope_tables():
    t = np.arange(LAT_SEQ)
    pos = np.stack([t // GRID_W, t % GRID_W], axis=1).astype(np.float64)

    def table(half):
        inv = ROPE_BASE ** (-np.arange(half, dtype=np.float64) / half)
        lane = np.arange(LANES)
        axis = (lane // (2 * half)) % 2
        freq = inv[lane % half]
        ang = pos[:, axis] * freq[None, :]
        sign = np.where((lane % (2 * half)) < half, -1.0, 1.0)
        cos = np.concatenate([np.ones((ROW_TILE, LANES)), np.cos(ang)], axis=0)
        sin = np.concatenate([np.zeros((ROW_TILE, LANES)), np.sin(ang) * sign[None, :]], axis=0)
        return jnp.asarray(cos, F32), jnp.asarray(sin, F32)

    cos64, sin64 = table(16)
    cos32, sin32 = table(8)
    return cos64, sin64, cos32, sin32


def _prepare_weights(norm_attn_pre, norm_attn_post, norm_ffn_pre, norm_ffn_post, w_in, dif_subln,
                     mla_q_norm, mla_w_uq, mla_kv_norm, mla_w_ukv, w_branch_a, w_branch_b,
                     w_branch_c, w_out, router_w, router_bias, shared_w_gate, shared_w_up,
                     shared_w_down):
    dp = DEPTH
    qa = w_in[:, :, 0:512].reshape(dp, D_MODEL, 2, 4, 64).transpose(0, 1, 3, 2, 4).reshape(dp, D_MODEL, 512)
    kpe = w_in[:, :, 2688:2720]
    w1 = jnp.concatenate([qa, w_in[:, :, 512:2688], kpe, kpe, kpe, kpe], axis=2).astype(BF16)
    uq = mla_w_uq.reshape(dp, 256, 8, 96)
    wuq = jnp.concatenate([uq[..., :64].reshape(dp, 256, 512), uq[..., 64:].reshape(dp, 256, 256)],
                          axis=2).astype(BF16)
    ukv = mla_w_ukv.reshape(dp, 128, 8, 128)
    wukv = jnp.concatenate([ukv[..., :64].reshape(dp, 128, 512), ukv[..., 64:].reshape(dp, 128, 512)],
                           axis=2).astype(BF16)
    wa = w_branch_a.reshape(dp, 2, 4, 64, D_MODEL).transpose(0, 2, 1, 3, 4).reshape(dp, 512, D_MODEL)
    return dict(
        g_pre=norm_attn_pre[:, None, :], g_post=norm_attn_post[:, None, :],
        g_ffn=norm_ffn_pre[:, None, :], g_ffn_post=norm_ffn_post[:, None, :],
        w1=w1, wgl=w_in[:, :, 2720:].astype(BF16),
        qnorm=mla_q_norm[:, None, :], wuq=wuq, kvnorm=mla_kv_norm[:, None, :], wukv=wukv,
        subln=dif_subln[:, None, :],
        wa=wa.astype(BF16), wb=w_branch_b.astype(BF16), wc=w_branch_c.astype(BF16),
        wo=w_out.astype(BF16), rwt=router_w.transpose(0, 2, 1).astype(BF16),
        rb=router_bias[:, :, None],
        swg=shared_w_gate.astype(BF16), swu=shared_w_up.astype(BF16), swd=shared_w_down.astype(BF16))


def kernel(x_prompt, x_sample, cache_swa_k, cache_swa_v, cache_dif_k, cache_dif_v, cache_mla_ckv, cache_mla_kpe, c, c_ctx, w_mod, b_mod, norm_attn_pre, norm_attn_post, norm_ffn_pre, norm_ffn_post, w_in, swa_sink, dif_lq1, dif_lk1, dif_lq2, dif_lk2, dif_subln, mla_q_norm, mla_w_uq, mla_kv_norm, mla_w_ukv, w_branch_a, w_branch_b, w_branch_c, w_out, router_w, router_bias, moe_w_gate, moe_w_up, moe_w_down, shared_w_gate, shared_w_up, shared_w_down):
    x = (x_prompt.reshape(N_CTX, D_MODEL), x_sample.reshape(N_LAT, D_MODEL))
    cvec = jnp.concatenate([c_ctx[None, :], c, jnp.zeros((8 - 1 - N_LAT_BATCH, D_MODEL), F32)], axis=0)
    mod = _modulation(cvec, w_mod, b_mod).reshape(DEPTH, 8, 1, 6 * D_MODEL)
    tables = _rope_tables()
    wts = _prepare_weights(norm_attn_pre, norm_attn_post, norm_ffn_pre, norm_ffn_post, w_in,
                           dif_subln, mla_q_norm, mla_w_uq, mla_kv_norm, mla_w_ukv, w_branch_a,
                           w_branch_b, w_branch_c, w_out, router_w, router_bias, shared_w_gate,
                           shared_w_up, shared_w_down)
    caches = [cache_swa_k.reshape(N_LAT_BATCH, DEPTH, PAST, 128),
              cache_swa_v.reshape(N_LAT_BATCH, DEPTH, PAST, 128),
              cache_dif_k.reshape(N_LAT_BATCH, DEPTH, PAST, 512),
              cache_dif_v.reshape(N_LAT_BATCH, DEPTH, PAST, 512),
              cache_mla_ckv,
              jnp.tile(cache_mla_kpe, (1, 1, 1, 4))]
    lam_init = [0.8 - 0.6 * math.exp(-0.3 * l) for l in range(DEPTH)]
    lam = (jnp.exp(jnp.sum(dif_lq1 * dif_lk1, axis=1)) - jnp.exp(jnp.sum(dif_lq2 * dif_lk2, axis=1))
           + jnp.asarray(lam_init, F32))
    scal = jnp.concatenate([swa_sink, lam[:, None]], axis=1).astype(F32)

    states = None
    for l in range(DEPTH):
        proj = _in_projection(x, mod, wts, tables, l, states)
        states = proj[11:]
        ctx_out = _attention_ctx(scal[l], wts["subln"], proj, states, l, lam_init[l])
        oa, od, om = _attention_lat(scal[l], wts["subln"], wts["wukv"], proj, caches, ctx_out, l,
                                    lam_init[l])
        xmid, h2, h2t, experts, gates = _post_attention(x, mod, oa, od, om, wts, l)
        lists = _routing_lists(experts, gates)
        acc = _moe(lists, h2t, moe_w_gate, moe_w_up, moe_w_down, l)
        x = _final(acc, h2, xmid, mod, wts, l, split=(l == DEPTH - 1))

    y_p = x[0].reshape(N_CTX_BATCH, CTX_SEQ, D_MODEL)
    y_s = x[1].reshape(N_LAT_BATCH, LAT_SEQ, D_MODEL)
    s_ka, s_va, s_kd, s_vd, s_ckv, s_kpe = states
    return (y_p, y_s,
            s_ka.reshape(N_CTX_BATCH, DEPTH, CTX_SEQ, 2, 64),
            s_va.reshape(N_CTX_BATCH, DEPTH, CTX_SEQ, 2, 64),
            s_kd.reshape(N_CTX_BATCH, DEPTH, CTX_SEQ, 4, 2, 64),
            s_vd.reshape(N_CTX_BATCH, DEPTH, CTX_SEQ, 4, 128),
            s_ckv, s_kpe)
```

```python
import functools
import math

import numpy as np
import jax
import jax.numpy as jnp
from jax import lax
from jax.experimental import pallas as pl
from jax.experimental.pallas import tpu as pltpu

F32 = jnp.float32
BF16 = jnp.bfloat16

D_MODEL = 1024
N_CTX_BATCH, CTX_SEQ = 16, 256
N_LAT_BATCH, LAT_SEQ = 2, 1024
PAST = 256
N_CTX = N_CTX_BATCH * CTX_SEQ
N_LAT = N_LAT_BATCH * LAT_SEQ
N_TOK = N_CTX + N_LAT
DEPTH = 2
GRID_W = 64
WINDOW = 128
N_EXPERTS = 64
N_GROUPS = 8
TOPK_GROUPS = 4
TOP_K = 6
F_EXPERT = 256
ROUTED_SCALE = 2.5
ROPE_BASE = 10000.0
EPS = 1e-6
NEG = -1e30

LANES = 128
ROW_TILE = 256
Q_BLOCK = 256
VMEM_LIMIT = 56 * 1024 * 1024
ACC_ROWS = 8
MOE_SPARE_TOKENS = 8
MOE_ZERO_ROWS = 64
MOE_VMEM_LIMIT = 62 * 1024 * 1024
MOE_SUB = 128
MOE_RMW_GROUP = 8
MOE_PAD = 2 * MOE_SUB
TOK_KEY = 8192
MOE_LIST_PAD_BASE = N_TOK * TOP_K + N_EXPERTS * MOE_PAD

C_QA, C_KA, C_VA, C_QD, C_KD, C_VD, C_QC, C_KVC, C_KPE, C_END = (
    0, 512, 640, 768, 1280, 1792, 2304, 2560, 2688, 2816)

SCALE_64 = 1.0 / math.sqrt(64.0)
SCALE_96 = 1.0 / math.sqrt(96.0)

CTX_TILES = N_CTX // ROW_TILE
N_TILES = N_TOK // ROW_TILE
assert ROW_TILE == CTX_SEQ and LAT_SEQ % ROW_TILE == 0


def _mod_row_of_tile(i):
    return jnp.where(i < CTX_TILES, 0, 1 + (i - CTX_TILES) // (LAT_SEQ // ROW_TILE))


def _rms(x, g):
    return x * lax.rsqrt(jnp.mean(x * x, axis=-1, keepdims=True) + EPS) * g


def _dot(a, b):
    return jnp.dot(a, b, preferred_element_type=F32)


def _dot_nt(a, b):
    return lax.dot_general(a, b, (((1,), (1,)), ((), ())), preferred_element_type=F32)


def _silu(x):
    return x * jax.nn.sigmoid(x)


def _layer_spec(arr, layer, grid_rank):
    zeros = (0,) * (arr.ndim - 1)
    return pl.BlockSpec((None,) + arr.shape[1:], lambda *_: (layer,) + zeros)


def _mod_kernel(c_ref, w_ref, b_ref, o_ref):
    c = c_ref[...]
    o_ref[0] = _dot(_silu(c).astype(BF16), w_ref[0].astype(BF16)) + b_ref[0]


def _modulation(cvec, w_mod, b_mod):
    tn = 1536
    n = w_mod.shape[-1]
    return pl.pallas_call(
        _mod_kernel,
        out_shape=jax.ShapeDtypeStruct((DEPTH, 8, n), F32),
        grid=(DEPTH, n // tn),
        in_specs=[
            pl.BlockSpec((8, D_MODEL), lambda l, j: (0, 0)),
            pl.BlockSpec((1, D_MODEL, tn), lambda l, j: (l, 0, j)),
            pl.BlockSpec((1, 1, tn), lambda l, j: (l, 0, j)),
        ],
        out_specs=pl.BlockSpec((1, 8, tn), lambda l, j: (l, 0, j)),
        compiler_params=pltpu.CompilerParams(vmem_limit_bytes=VMEM_LIMIT),
        name="modulation",
    )(cvec, w_mod, b_mod.reshape(DEPTH, 1, n))


def _tile(i):
    return jnp.minimum(i, N_TILES - 1)


def _x_specs(x):
    if isinstance(x, tuple):
        return [pl.BlockSpec((ROW_TILE, D_MODEL), lambda i: (jnp.minimum(i, CTX_TILES - 1), 0)),
                pl.BlockSpec((ROW_TILE, D_MODEL), lambda i: (jnp.maximum(_tile(i) - CTX_TILES, 0), 0))]
    return [pl.BlockSpec((ROW_TILE, D_MODEL), lambda i: (_tile(i), 0))]


def _load_x(x_refs):
    if len(x_refs) == 1:
        return x_refs[0][...]
    return jnp.where(pl.program_id(0) < CTX_TILES, x_refs[0][...], x_refs[1][...])


def _mod_spec(layer):
    return pl.BlockSpec((None, 1, 1, 6 * D_MODEL),
                        lambda i: (layer, _mod_row_of_tile(_tile(i)), 0, 0))


def _rope128(x, cos, sin, half):
    lane = lax.broadcasted_iota(jnp.int32, x.shape, 1)
    first = (lane % (2 * half)) < half
    partner = jnp.where(first, pltpu.roll(x, LANES - half, 1), pltpu.roll(x, half, 1))
    return x * cos + partner * sin


def _rope_cols(x, cos, sin, half):
    chunks = [_rope128(x[:, c:c + LANES], cos, sin, half) for c in range(0, x.shape[1], LANES)]
    return chunks[0] if len(chunks) == 1 else jnp.concatenate(chunks, axis=1)


N_IN_INPUTS = 11


def _in_kernel(*refs, n_x):
    (mod_ref, g_ref, w1_ref, qn_ref, wuq_ref, kvn_ref, wukv_ref,
     cos64_ref, sin64_ref, cos32_ref, sin32_ref) = refs[n_x:n_x + N_IN_INPUTS]
    (qa_o, qd_o, qmn_o, qmr_o, kmn_o, vm_o, kpe4_o, ka_l, va_l, kd_l, vd_l,
     s_ka, s_va, s_kd, s_vd, s_ckv, s_kpe) = refs[-17:]
    i = pl.program_id(0)
    x = _load_x(refs[:n_x])
    m = mod_ref[0]
    sh1, sc1 = m[:, 0:D_MODEL], m[:, D_MODEL:2 * D_MODEL]
    h = _rms(x, g_ref[...]) * (1.0 + sc1) + sh1
    z = _dot(h.astype(BF16), w1_ref[...])
    cos64, sin64 = cos64_ref[...], sin64_ref[...]
    cos32, sin32 = cos32_ref[...], sin32_ref[...]
    qa_o[...] = (_rope_cols(z[:, C_QA:C_KA], cos64, sin64, 16) * SCALE_64).astype(BF16)
    qd_o[...] = (_rope_cols(z[:, C_QD:C_KD], cos64, sin64, 16) * SCALE_64).astype(BF16)
    ka = _rope_cols(z[:, C_KA:C_VA], cos64, sin64, 16)
    va = z[:, C_VA:C_QD]
    kd = _rope_cols(z[:, C_KD:C_VD], cos64, sin64, 16)
    vd = z[:, C_VD:C_QC]
    kpe4 = _rope_cols(z[:, C_KPE:C_END], cos32, sin32, 8)
    kpe4_o[...] = kpe4.astype(BF16)
    qn = _rms(z[:, C_QC:C_KVC], qn_ref[...])
    qm = _dot(qn.astype(BF16), wuq_ref[...]) * SCALE_96
    qmn_o[...] = qm[:, 0:512].astype(BF16)
    qmr_o[...] = _rope_cols(qm[:, 512:768], cos32, sin32, 8).astype(BF16)
    ckv = _rms(z[:, C_KVC:C_KPE], kvn_ref[...])
    kv = _dot(ckv.astype(BF16), wukv_ref[...])
    kmn_o[...] = kv[:, 0:512].astype(BF16)
    vm_o[...] = kv[:, 512:1024].astype(BF16)

    @pl.when(i < CTX_TILES)
    def _():
        s_ka[...] = ka
        s_va[...] = va
        s_kd[...] = kd
        s_vd[...] = vd
        s_ckv[...] = ckv
        s_kpe[...] = kpe4[:, 0:32]

    @pl.when(i >= CTX_TILES)
    def _():
        ka_l[...] = ka.astype(BF16)
        va_l[...] = va.astype(BF16)
        kd_l[...] = kd.astype(BF16)
        vd_l[...] = vd.astype(BF16)


STATE_WIDTHS = (128, 128, 512, 512, 128, 32)


def _in_projection(x, mod, wts, tables, layer, prev_states):
    tm = ROW_TILE

    def row(i):
        return (i, 0)

    def table_row(i):
        return (jnp.where(i < CTX_TILES, 0, 1 + (i - CTX_TILES) % (LAT_SEQ // tm)), 0)

    def lat_row(i):
        return (jnp.maximum(i - CTX_TILES, 0), 0)

    def state_block(i):
        return (jnp.minimum(i, CTX_TILES - 1), layer, 0, 0)

    table_spec = pl.BlockSpec((tm, LANES), table_row)
    weights = [wts["g_pre"], wts["w1"], wts["qnorm"], wts["wuq"], wts["kvnorm"], wts["wukv"]]
    all_rows = [(512, BF16)] * 3 + [(256, BF16), (512, BF16), (512, BF16), (128, BF16)]
    lat_rows = [(128, BF16), (128, BF16), (512, BF16), (512, BF16)]
    aliased = [] if prev_states is None else list(prev_states)
    xs = list(x) if isinstance(x, tuple) else [x]
    n_in = len(xs) + N_IN_INPUTS
    n_plain_out = len(all_rows) + len(lat_rows)
    return pl.pallas_call(
        functools.partial(_in_kernel, n_x=len(xs)),
        out_shape=[jax.ShapeDtypeStruct((N_TOK, w), dt) for w, dt in all_rows]
                  + [jax.ShapeDtypeStruct((N_LAT, w), dt) for w, dt in lat_rows]
                  + [jax.ShapeDtypeStruct((N_CTX_BATCH, DEPTH, CTX_SEQ, w), F32) for w in STATE_WIDTHS],
        grid=(N_TOK // tm,),
        in_specs=_x_specs(x) + [_mod_spec(layer)]
                 + [_layer_spec(w, layer, 1) for w in weights]
                 + [table_spec] * 4
                 + [pl.BlockSpec(memory_space=pl.ANY)] * len(aliased),
        out_specs=[pl.BlockSpec((tm, w), row) for w, _ in all_rows]
                  + [pl.BlockSpec((tm, w), lat_row) for w, _ in lat_rows]
                  + [pl.BlockSpec((None, None, CTX_SEQ, w), state_block) for w in STATE_WIDTHS],
        input_output_aliases={n_in + k: n_plain_out + k for k in range(len(aliased))},
        compiler_params=pltpu.CompilerParams(
            dimension_semantics=("arbitrary",), vmem_limit_bytes=VMEM_LIMIT),
        name="in_projection",
    )(*xs, mod, *weights, *tables, *aliased)


def _attend(q, parts, sink=None):
    scores = []
    for k, _, mask in parts:
        s = _dot_nt(q, k)
        if mask is not None:
            s = jnp.where(mask, s, NEG)
        scores.append(s)
    m = functools.reduce(jnp.maximum, [jnp.max(s, axis=-1, keepdims=True) for s in scores])
    if sink is not None:
        m = jnp.maximum(m, sink)
    den = None
    out = None
    for s, (_, v, _) in zip(scores, parts):
        e = jnp.exp(s - m)
        d = jnp.sum(e, axis=-1, keepdims=True)
        o = _dot(e.astype(BF16), v)
        den = d if den is None else den + d
        out = o if out is None else out + o
    if sink is not None:
        den = den + jnp.exp(sink - m)
    return out / den


A_HEADS = 8


def _attend_heads(qs, parts, sinks, stack):
    rows = qs[0].shape[0]
    if not stack:
        return [_attend(q, parts, None if sinks is None else sinks[i]) for i, q in enumerate(qs)]
    sink = None
    if sinks is not None:
        sink = jnp.concatenate([jnp.full((rows, 1), s, F32) for s in sinks], axis=0)
    o = _attend(jnp.concatenate(qs, axis=0), parts, sink)
    return [o[i * rows:(i + 1) * rows] for i in range(len(qs))]


def _three_mixers(scal_ref, subln_ref, q_blocks, kv_parts, oa_o, od_o, om_o, lam_init, stack):
    qa, qd, qmn, qmr = q_blocks
    rows = qa.shape[0]
    lane = lax.broadcasted_iota(jnp.int32, (rows, LANES), 1)
    low = lane < 64
    zero = jnp.zeros((rows, LANES), BF16)

    parts = [(p["ka"], p["va"], p.get("mask_a")) for p in kv_parts]
    qs = [jnp.where(low if kvh == 0 else ~low, qa[:, g * LANES:(g + 1) * LANES], zero)
          for kvh in range(2) for g in range(4)]
    out_a = _attend_heads(qs, parts, [scal_ref[h] for h in range(A_HEADS)], stack)
    for g in range(4):
        oa_o[:, g * LANES:(g + 1) * LANES] = jnp.where(low, out_a[g], out_a[4 + g]).astype(BF16)

    lam = scal_ref[8]
    subln = subln_ref[...]
    for h in range(4):
        sl = slice(h * LANES, (h + 1) * LANES)
        q128 = qd[:, sl]
        o1, o2 = _attend_heads([jnp.where(low, q128, zero), jnp.where(low, zero, q128)],
                               [(p["kd"][:, sl], p["vd"][:, sl], None) for p in kv_parts],
                               None, stack)
        od = o1 - lam * o2
        od_o[:, sl] = (_rms(od, subln) * (1.0 - lam_init)).astype(BF16)

    for i in range(4):
        sl = slice(i * LANES, (i + 1) * LANES)
        qs = []
        for sub in range(2):
            h = 2 * i + sub
            qn = jnp.where(low if sub == 0 else ~low, qmn[:, sl], zero)
            qr128 = qmr[:, (h // 4) * LANES:(h // 4 + 1) * LANES]
            qr = jnp.where((lane // 32) == (h % 4), qr128, zero)
            qs.append(jnp.concatenate([qn, qr], axis=1))
        parts = [(jnp.concatenate([p["kmn"][:, sl], p["kpe"]], axis=1), p["vm"][:, sl], None)
                 for p in kv_parts]
        o_even, o_odd = _attend_heads(qs, parts, None, stack)
        om_o[:, sl] = jnp.where(low, o_even, o_odd).astype(BF16)


def _attn_ctx_kernel(scal_ref, subln_ref, qa_ref, qd_ref, qmn_ref, qmr_ref,
                     ka_ref, va_ref, kd_ref, vd_ref, kmn_ref, vm_ref, kpe_ref,
                     oa_o, od_o, om_o, *, lam_init):
    part = dict(ka=ka_ref[...].astype(BF16), va=va_ref[...].astype(BF16),
                kd=kd_ref[...].astype(BF16), vd=vd_ref[...].astype(BF16),
                kmn=kmn_ref[...], vm=vm_ref[...], kpe=kpe_ref[...])
    _three_mixers(scal_ref, subln_ref, (qa_ref[...], qd_ref[...], qmn_ref[...], qmr_ref[...]),
                  [part], oa_o, od_o, om_o, lam_init, stack=False)


def _attn_lat_kernel(scal_ref, subln_ref, wukv_ref, qa_ref, qd_ref, qmn_ref, qmr_ref,
                     ka_ref, va_ref, kd_ref, vd_ref, kmn_ref, vm_ref, kpe_ref,
                     cka_ref, cva_ref, ckd_ref, cvd_ref, cckv_ref, ckpe_ref,
                     oa_in, od_in, om_in, oa_o, od_o, om_o, *, lam_init):
    del oa_in, od_in, om_in
    n = pl.program_id(1)
    kv_c = _dot(cckv_ref[...].astype(BF16), wukv_ref[...])
    cached = dict(ka=cka_ref[...].astype(BF16), va=cva_ref[...].astype(BF16),
                  kd=ckd_ref[...].astype(BF16), vd=cvd_ref[...].astype(BF16),
                  kmn=kv_c[:, 0:512].astype(BF16), vm=kv_c[:, 512:1024].astype(BF16),
                  kpe=ckpe_ref[...].astype(BF16))
    span = Q_BLOCK + 2 * WINDOW
    start = pl.multiple_of(jnp.clip(n * Q_BLOCK - WINDOW, 0, LAT_SEQ - span), WINDOW)
    qpos = n * Q_BLOCK + lax.broadcasted_iota(jnp.int32, (A_HEADS * Q_BLOCK, span), 0) % Q_BLOCK
    kpos = start + lax.broadcasted_iota(jnp.int32, (A_HEADS * Q_BLOCK, span), 1)
    mask_a = jnp.abs(qpos - kpos) <= WINDOW
    new = dict(ka=ka_ref[pl.ds(start, span), :], va=va_ref[pl.ds(start, span), :], mask_a=mask_a,
               kd=kd_ref[...], vd=vd_ref[...], kmn=kmn_ref[...], vm=vm_ref[...], kpe=kpe_ref[...])
    _three_mixers(scal_ref, subln_ref, (qa_ref[...], qd_ref[...], qmn_ref[...], qmr_ref[...]),
                  [cached, new], oa_o, od_o, om_o, lam_init, stack=True)


_SMEM_SPEC = pl.BlockSpec(memory_space=pltpu.SMEM)


def _attention_ctx(scal, subln, proj, states, layer, lam_init):
    qa, qd, qmn, qmr, kmn, vm, kpe4 = proj[:7]
    s_ka, s_va, s_kd, s_vd = states[:4]
    t = CTX_SEQ

    def blk(a):
        return pl.BlockSpec((t, a.shape[1]), lambda b: (b, 0))

    def sblk(a):
        return pl.BlockSpec((None, None, t, a.shape[-1]), lambda b: (b, layer, 0, 0))

    return pl.pallas_call(
        functools.partial(_attn_ctx_kernel, lam_init=lam_init),
        out_shape=[jax.ShapeDtypeStruct((N_TOK, 512), BF16)] * 3,
        grid=(N_CTX_BATCH,),
        in_specs=[_SMEM_SPEC, _layer_spec(subln, layer, 1)]
                 + [blk(a) for a in (qa, qd, qmn, qmr)]
                 + [sblk(a) for a in (s_ka, s_va, s_kd, s_vd)]
                 + [blk(a) for a in (kmn, vm, kpe4)],
        out_specs=[pl.BlockSpec((t, 512), lambda b: (b, 0))] * 3,
        compiler_params=pltpu.CompilerParams(vmem_limit_bytes=VMEM_LIMIT),
        name="attention_ctx",
    )(scal, subln, qa, qd, qmn, qmr, s_ka, s_va, s_kd, s_vd, kmn, vm, kpe4)


def _attention_lat(scal, subln, wukv, proj, caches, ctx_out, layer, lam_init):
    qa, qd, qmn, qmr, kmn, vm, kpe4, ka_l, va_l, kd_l, vd_l = proj[:11]
    nq = LAT_SEQ // Q_BLOCK
    q_off = N_CTX // Q_BLOCK
    kv_off = N_CTX // LAT_SEQ

    def qblk(a):
        return pl.BlockSpec((Q_BLOCK, a.shape[1]), lambda b, n: (q_off + b * nq + n, 0))

    def kvblk(a):
        return pl.BlockSpec((LAT_SEQ, a.shape[1]), lambda b, n: (b, 0))

    def kvblk_all(a):
        return pl.BlockSpec((LAT_SEQ, a.shape[1]), lambda b, n: (kv_off + b, 0))

    def cblk(a):
        return pl.BlockSpec((None, None, PAST, a.shape[-1]), lambda b, n: (b, layer, 0, 0))

    qs = [qa, qd, qmn, qmr]
    n_in = 3 + 4 + 7 + 6
    out_spec = pl.BlockSpec((Q_BLOCK, 512), lambda b, n: (q_off + b * nq + n, 0))
    return pl.pallas_call(
        functools.partial(_attn_lat_kernel, lam_init=lam_init),
        out_shape=[jax.ShapeDtypeStruct((N_TOK, 512), BF16)] * 3,
        grid=(N_LAT_BATCH, nq),
        in_specs=[_SMEM_SPEC, _layer_spec(subln, layer, 2), _layer_spec(wukv, layer, 2)]
                 + [qblk(a) for a in qs]
                 + [kvblk(a) for a in (ka_l, va_l, kd_l, vd_l)]
                 + [kvblk_all(a) for a in (kmn, vm, kpe4)]
                 + [cblk(a) for a in caches]
                 + [pl.BlockSpec(memory_space=pl.ANY)] * 3,
        out_specs=[out_spec] * 3,
        input_output_aliases={n_in + k: k for k in range(3)},
        compiler_params=pltpu.CompilerParams(vmem_limit_bytes=VMEM_LIMIT),
        name="attention_lat",
    )(scal, subln, wukv, *qs, ka_l, va_l, kd_l, vd_l, kmn, vm, kpe4, *caches, *ctx_out)


def _route(scores, biased):
    per = N_EXPERTS // N_GROUPS
    tm = scores.shape[1]
    sub = lax.broadcasted_iota(jnp.int32, (per, tm), 0).astype(F32)
    groups = [biased[g * per:(g + 1) * per, :] for g in range(N_GROUPS)]
    gscore = []
    for v in groups:
        m1 = jnp.max(v, axis=0, keepdims=True)
        first = jnp.min(jnp.where(v == m1, sub, float(per)), axis=0, keepdims=True)
        m2 = jnp.max(jnp.where(sub == first, -jnp.inf, v), axis=0, keepdims=True)
        gscore.append(m1 + m2)
    vals = []
    for g in range(N_GROUPS):
        rank = jnp.zeros((1, tm), F32)
        for o in range(N_GROUPS):
            if o == g:
                continue
            ahead = (gscore[o] >= gscore[g]) if o < g else (gscore[o] > gscore[g])
            rank = rank + jnp.where(ahead, 1.0, 0.0)
        vals.append(jnp.where(rank < TOPK_GROUPS, groups[g], NEG))
    idx = [sub + float(g * per) for g in range(N_GROUPS)]
    picks, weights = [], []
    for _ in range(TOP_K):
        best = functools.reduce(jnp.maximum, [jnp.max(v, axis=0, keepdims=True) for v in vals])
        cand = functools.reduce(jnp.minimum, [
            jnp.min(jnp.where(v == best, i, float(N_EXPERTS)), axis=0, keepdims=True)
            for v, i in zip(vals, idx)])
        wsel = jnp.zeros((1, tm), F32)
        for g in range(N_GROUPS):
            hit = idx[g] == cand
            wsel = wsel + jnp.sum(jnp.where(hit, scores[g * per:(g + 1) * per, :], 0.0),
                                  axis=0, keepdims=True)
            vals[g] = jnp.where(hit, -jnp.inf, vals[g])
        picks.append(cand)
        weights.append(wsel)
    total = functools.reduce(lambda a, b: a + b, weights)
    pad = [jnp.zeros((1, tm), F32)] * (8 - TOP_K)
    experts = jnp.concatenate(picks + pad, axis=0).astype(jnp.int32)
    gates = jnp.concatenate([x / total * ROUTED_SCALE for x in weights] + pad, axis=0)
    return experts, gates


def _post_kernel(*refs, n_x):
    (mod_ref, oa_ref, od_ref, om_ref, gpre_ref, wgl_ref, wa_ref, wb_ref, wc_ref, wo_ref, gpost_ref,
     gffn_ref, rwt_ref, rb_ref, xmid_o, h2_o, h2t_o, experts_o, gates_o) = refs[n_x:]

    @pl.when(pl.program_id(0) < N_TILES)
    def _():
        x = _load_x(refs[:n_x])
        m = mod_ref[0]
        d = D_MODEL
        sh1, sc1, g1, sh2, sc2 = (m[:, 0:d], m[:, d:2 * d], m[:, 2 * d:3 * d], m[:, 3 * d:4 * d],
                                  m[:, 4 * d:5 * d])
        h = _rms(x, gpre_ref[...]) * (1.0 + sc1) + sh1
        gate = jax.nn.sigmoid(_dot(h.astype(BF16), wgl_ref[...]))
        merged = (gate[:, 0:d] * _dot(oa_ref[...], wa_ref[...])
                  + gate[:, d:2 * d] * _dot(od_ref[...], wb_ref[...])
                  + gate[:, 2 * d:3 * d] * _dot(om_ref[...], wc_ref[...]))
        a = _dot(merged.astype(BF16), wo_ref[...])
        xm = x + g1 * _rms(a, gpost_ref[...])
        xmid_o[...] = xm
        h2f = _rms(xm, gffn_ref[...]) * (1.0 + sc2) + sh2
        h2 = h2f.astype(BF16)
        h2_o[...] = h2
        for c in range(ACC_ROWS):
            h2t_o[pl.ds(c, x.shape[0], stride=ACC_ROWS), :] = h2f[:, c * LANES:(c + 1) * LANES]
        scores = jax.nn.sigmoid(_dot_nt(rwt_ref[...], h2))
        experts, gates = _route(scores, scores + rb_ref[...])
        experts_o[...] = experts
        gates_o[...] = gates

    @pl.when(pl.program_id(0) == N_TILES)
    def _():
        h2t_o[...] = jnp.zeros_like(h2t_o)


def _post_attention(x, mod, oa, od, om, wts, layer):
    tm = ROW_TILE

    def row(i):
        return (_tile(i), 0)

    weights = [wts[k] for k in ("g_pre", "wgl", "wa", "wb", "wc", "wo", "g_post", "g_ffn", "rwt", "rb")]
    xs = list(x) if isinstance(x, tuple) else [x]
    return pl.pallas_call(
        functools.partial(_post_kernel, n_x=len(xs)),
        out_shape=[jax.ShapeDtypeStruct((N_TOK, D_MODEL), F32),
                   jax.ShapeDtypeStruct((N_TOK, D_MODEL), BF16),
                   jax.ShapeDtypeStruct(((N_TOK + tm) * ACC_ROWS, LANES), F32),
                   jax.ShapeDtypeStruct((8, N_TOK), jnp.int32),
                   jax.ShapeDtypeStruct((8, N_TOK), F32)],
        grid=(N_TILES + 1,),
        in_specs=_x_specs(x) + [_mod_spec(layer),
                  pl.BlockSpec((tm, 512), row), pl.BlockSpec((tm, 512), row),
                  pl.BlockSpec((tm, 512), row)]
                 + [_layer_spec(w, layer, 1) for w in weights],
        out_specs=[pl.BlockSpec((tm, D_MODEL), row), pl.BlockSpec((tm, D_MODEL), row),
                   pl.BlockSpec((tm * ACC_ROWS, LANES), lambda i: (i, 0)),
                   pl.BlockSpec((8, tm), lambda i: (0, _tile(i))),
                   pl.BlockSpec((8, tm), lambda i: (0, _tile(i)))],
        compiler_params=pltpu.CompilerParams(
            dimension_semantics=("arbitrary",), vmem_limit_bytes=VMEM_LIMIT),
        name="post_attention",
    )(*xs, mod, oa, od, om, *weights)


def _ffn(x, wg, wu):
    return _silu(_dot(x, wg)) * _dot(x, wu)


def _moe_kernel(start_ref, count_ref, dst_ref, gate_ref, xs_ref, wg_ref, wu_ref, wd_ref,
                acc_ref, xa, xb, ya, yb, wgb, wub, wdb, pend):
    e = pl.program_id(0)

    @pl.when(e == 0)
    def _():
        def clear(i, carry):
            r0 = pl.multiple_of(i * MOE_ZERO_ROWS, MOE_ZERO_ROWS)
            acc_ref[pl.ds(r0, MOE_ZERO_ROWS), :] = jnp.zeros((MOE_ZERO_ROWS, LANES), F32)
            return carry

        lax.fori_loop(0, acc_ref.shape[0] // MOE_ZERO_ROWS, clear, 0)

        yb[...] = jnp.zeros_like(yb)
        pend[0] = MOE_LIST_PAD_BASE

    wgb[...] = wg_ref[0].astype(BF16)
    wub[...] = wu_ref[0].astype(BF16)
    wdb[...] = wd_ref[0].astype(BF16)
    start = start_ref[e]
    pairs = (count_ref[e] + 2 * MOE_SUB - 1) // (2 * MOE_SUB)

    def gather(base, xbuf):
        for mi in range(MOE_SUB):
            src = pl.multiple_of(dst_ref[base + mi], ACC_ROWS)
            xbuf[mi * ACC_ROWS:(mi + 1) * ACC_ROWS, :] = xs_ref[pl.ds(src, ACC_ROWS), :]

    def expert_ffn(xbuf, ybuf):
        x = jnp.concatenate([xbuf[pl.ds(c, MOE_SUB, stride=ACC_ROWS), :].astype(BF16)
                             for c in range(ACC_ROWS)], axis=1)
        hid = _ffn(x, wgb[...], wub[...])
        y = _dot(hid.astype(BF16), wdb[...])
        for c in range(ACC_ROWS):
            ybuf[pl.ds(c, MOE_SUB, stride=ACC_ROWS), :] = y[:, c * LANES:(c + 1) * LANES]

    def scatter(base, ybuf):
        for g0 in range(0, MOE_SUB, MOE_RMW_GROUP):
            updates = []
            for mi in range(g0, g0 + MOE_RMW_GROUP):
                dst = pl.multiple_of(dst_ref[base + mi], ACC_ROWS)
                yv = ybuf[mi * ACC_ROWS:(mi + 1) * ACC_ROWS, :]
                updates.append((dst, acc_ref[pl.ds(dst, ACC_ROWS), :] + gate_ref[base + mi] * yv))
            for dst, val in updates:
                acc_ref[pl.ds(dst, ACC_ROWS), :] = val

    start_next = start_ref[jnp.minimum(e + 1, N_EXPERTS - 1)]
    prev_count = count_ref[jnp.maximum(e - 1, 0)]

    @pl.when((pairs > 0) & ((e == 0) | (prev_count == 0)))
    def _():
        gather(start, xa)

    def pair(p, carry):
        base = start + p * (2 * MOE_SUB)
        gather(base + MOE_SUB, xb)
        expert_ffn(xa, ya)
        scatter(pend[0], yb)
        gather(jnp.where(p + 1 < pairs, base + 2 * MOE_SUB, start_next), xa)
        expert_ffn(xb, yb)
        scatter(base, ya)
        pend[0] = base + MOE_SUB
        return carry

    lax.fori_loop(0, pairs, pair, 0)

    @pl.when(e == N_EXPERTS - 1)
    def _():
        scatter(pend[0], yb)


def _moe(lists, h2t, wg, wu, wd, layer):
    def expert(e, *_):
        return (layer, e, 0, 0)

    def whole(e, *_):
        return (0, 0)

    acc_rows = (N_TOK + MOE_SPARE_TOKENS) * ACC_ROWS
    assert acc_rows % MOE_ZERO_ROWS == 0
    return pl.pallas_call(
        _moe_kernel,
        out_shape=jax.ShapeDtypeStruct((acc_rows, LANES), F32),
        grid_spec=pltpu.PrefetchScalarGridSpec(
            num_scalar_prefetch=len(lists),
            grid=(N_EXPERTS,),
            in_specs=[pl.BlockSpec((acc_rows, LANES), whole, pipeline_mode=pl.Buffered(1)),
                      pl.BlockSpec((None, 1, D_MODEL, F_EXPERT), expert),
                      pl.BlockSpec((None, 1, D_MODEL, F_EXPERT), expert),
                      pl.BlockSpec((None, 1, F_EXPERT, D_MODEL), expert)],
            out_specs=pl.BlockSpec((acc_rows, LANES), whole, pipeline_mode=pl.Buffered(1)),
            scratch_shapes=[pltpu.VMEM((MOE_SUB * ACC_ROWS, LANES), F32)] * 4
                           + [pltpu.VMEM((D_MODEL, F_EXPERT), BF16),
                              pltpu.VMEM((D_MODEL, F_EXPERT), BF16),
                              pltpu.VMEM((F_EXPERT, D_MODEL), BF16),
                              pltpu.SMEM((1,), jnp.int32)]),
        compiler_params=pltpu.CompilerParams(
            dimension_semantics=("arbitrary",), vmem_limit_bytes=MOE_VMEM_LIMIT),
        name="moe",
    )(*lists, h2t, wg, wu, wd)


def _final_kernel(acc_ref, h2_ref, xmid_ref, mod_ref, swg_ref, swu_ref, swd_ref, gpost_ref, *out_refs):
    tm = h2_ref.shape[0]
    routed = jnp.concatenate(
        [acc_ref[pl.ds(c, tm, stride=ACC_ROWS), :] for c in range(ACC_ROWS)], axis=1)
    hid = _ffn(h2_ref[...], swg_ref[...], swu_ref[...])
    shared = _dot(hid.astype(BF16), swd_ref[...])
    g2 = mod_ref[0][:, 5 * D_MODEL:6 * D_MODEL]
    out = xmid_ref[...] + g2 * _rms(routed + shared, gpost_ref[...])
    if len(out_refs) == 1:
        out_refs[0][...] = out
    else:
        @pl.when(pl.program_id(0) < CTX_TILES)
        def _():
            out_refs[0][...] = out

        @pl.when(pl.program_id(0) >= CTX_TILES)
        def _():
            out_refs[1][...] = out


def _final(acc, h2, xmid, mod, wts, layer, split):
    tm = ROW_TILE

    def row(i):
        return (i, 0)

    weights = [wts[k] for k in ("swg", "swu", "swd", "g_ffn_post")]
    return pl.pallas_call(
        _final_kernel,
        out_shape=([jax.ShapeDtypeStruct((N_CTX, D_MODEL), F32),
                    jax.ShapeDtypeStruct((N_LAT, D_MODEL), F32)] if split
                   else jax.ShapeDtypeStruct((N_TOK, D_MODEL), F32)),
        grid=(N_TOK // tm,),
        in_specs=[pl.BlockSpec((tm * ACC_ROWS, LANES), row),
                  pl.BlockSpec((tm, D_MODEL), row),
                  pl.BlockSpec((tm, D_MODEL), row),
                  _mod_spec(layer)]
                 + [_layer_spec(w, layer, 1) for w in weights],
        out_specs=_x_specs((None, None)) if split else pl.BlockSpec((tm, D_MODEL), row),
        compiler_params=pltpu.CompilerParams(
            dimension_semantics=("arbitrary",), vmem_limit_bytes=VMEM_LIMIT),
        name="ffn_final",
    )(acc, h2, xmid, mod, *weights)


def _routing_lists(experts, gates):
    e = experts[:TOP_K].reshape(-1)
    tok = jnp.tile(jnp.arange(N_TOK, dtype=jnp.int32), TOP_K)
    pad_e = jnp.repeat(jnp.arange(N_EXPERTS, dtype=jnp.int32), MOE_PAD)
    pad_tok = N_TOK + jnp.tile(jnp.arange(MOE_PAD, dtype=jnp.int32), N_EXPERTS)
    keys = jnp.concatenate([e * TOK_KEY + tok, pad_e * TOK_KEY + pad_tok])
    vals = jnp.concatenate([gates[:TOP_K].reshape(-1), jnp.zeros((N_EXPERTS * MOE_PAD,), F32)])
    keys, gate_sorted = lax.sort((keys, vals), num_keys=1)
    tail = jnp.full((MOE_SUB,), N_TOK, jnp.int32)
    tok_sorted = jnp.concatenate([keys & (TOK_KEY - 1), tail])
    gate_sorted = jnp.concatenate([gate_sorted, jnp.zeros((MOE_SUB,), F32)])
    dst = jnp.minimum(tok_sorted, N_TOK) * ACC_ROWS
    counts = jnp.sum((e[None, :] == jnp.arange(N_EXPERTS, dtype=jnp.int32)[:, None]).astype(jnp.int32),
                     axis=1)
    start = jnp.cumsum(counts) - counts + MOE_PAD * jnp.arange(N_EXPERTS, dtype=jnp.int32)
    return start.astype(jnp.int32), counts, dst, gate_sorted


def _rope_tables():
    t = np.arange(LAT_SEQ)
    pos = np.stack([t // GRID_W, t % GRID_W], axis=1).astype(np.float64)

    def table(half):
        inv = ROPE_BASE ** (-np.arange(half, dtype=np.float64) / half)
        lane = np.arange(LANES)
        axis = (lane // (2 * half)) % 2
        freq = inv[lane % half]
        ang = pos[:, axis] * freq[None, :]
        sign = np.where((lane % (2 * half)) < half, -1.0, 1.0)
        cos = np.concatenate([np.ones((ROW_TILE, LANES)), np.cos(ang)], axis=0)
        sin = np.concatenate([np.zeros((ROW_TILE, LANES)), np.sin(ang) * sign[None, :]], axis=0)
        return jnp.asarray(cos, F32), jnp.asarray(sin, F32)

    cos64, sin64 = table(16)
    cos32, sin32 = table(8)
    return cos64, sin64, cos32, sin32


def _prepare_weights(norm_attn_pre, norm_attn_post, norm_ffn_pre, norm_ffn_post, w_in, dif_subln,
                     mla_q_norm, mla_w_uq, mla_kv_norm, mla_w_ukv, w_branch_a, w_branch_b,
                     w_branch_c, w_out, router_w, router_bias, shared_w_gate, shared_w_up,
                     shared_w_down):
    dp = DEPTH
    qa = w_in[:, :, 0:512].reshape(dp, D_MODEL, 2, 4, 64).transpose(0, 1, 3, 2, 4).reshape(dp, D_MODEL, 512)
    kpe = w_in[:, :, 2688:2720]
    w1 = jnp.concatenate([qa, w_in[:, :, 512:2688], kpe, kpe, kpe, kpe], axis=2).astype(BF16)
    uq = mla_w_uq.reshape(dp, 256, 8, 96)
    wuq = jnp.concatenate([uq[..., :64].reshape(dp, 256, 512), uq[..., 64:].reshape(dp, 256, 256)],
                          axis=2).astype(BF16)
    ukv = mla_w_ukv.reshape(dp, 128, 8, 128)
    wukv = jnp.concatenate([ukv[..., :64].reshape(dp, 128, 512), ukv[..., 64:].reshape(dp, 128, 512)],
                           axis=2).astype(BF16)
    wa = w_branch_a.reshape(dp, 2, 4, 64, D_MODEL).transpose(0, 2, 1, 3, 4).reshape(dp, 512, D_MODEL)
    return dict(
        g_pre=norm_attn_pre[:, None, :], g_post=norm_attn_post[:, None, :],
        g_ffn=norm_ffn_pre[:, None, :], g_ffn_post=norm_ffn_post[:, None, :],
        w1=w1, wgl=w_in[:, :, 2720:].astype(BF16),
        qnorm=mla_q_norm[:, None, :], wuq=wuq, kvnorm=mla_kv_norm[:, None, :], wukv=wukv,
        subln=dif_subln[:, None, :],
        wa=wa.astype(BF16), wb=w_branch_b.astype(BF16), wc=w_branch_c.astype(BF16),
        wo=w_out.astype(BF16), rwt=router_w.transpose(0, 2, 1).astype(BF16),
        rb=router_bias[:, :, None],
        swg=shared_w_gate.astype(BF16), swu=shared_w_up.astype(BF16), swd=shared_w_down.astype(BF16))


def kernel(x_prompt, x_sample, cache_swa_k, cache_swa_v, cache_dif_k, cache_dif_v, cache_mla_ckv, cache_mla_kpe, c, c_ctx, w_mod, b_mod, norm_attn_pre, norm_attn_post, norm_ffn_pre, norm_ffn_post, w_in, swa_sink, dif_lq1, dif_lk1, dif_lq2, dif_lk2, dif_subln, mla_q_norm, mla_w_uq, mla_kv_norm, mla_w_ukv, w_branch_a, w_branch_b, w_branch_c, w_out, router_w, router_bias, moe_w_gate, moe_w_up, moe_w_down, shared_w_gate, shared_w_up, shared_w_down):
    x = (x_prompt.reshape(N_CTX, D_MODEL), x_sample.reshape(N_LAT, D_MODEL))
    cvec = jnp.concatenate([c_ctx[None, :], c, jnp.zeros((8 - 1 - N_LAT_BATCH, D_MODEL), F32)], axis=0)
    mod = _modulation(cvec, w_mod, b_mod).reshape(DEPTH, 8, 1, 6 * D_MODEL)
    tables = _rope_tables()
    wts = _prepare_weights(norm_attn_pre, norm_attn_post, norm_ffn_pre, norm_ffn_post, w_in,
                           dif_subln, mla_q_norm, mla_w_uq, mla_kv_norm, mla_w_ukv, w_branch_a,
                           w_branch_b, w_branch_c, w_out, router_w, router_bias, shared_w_gate,
                           shared_w_up, shared_w_down)
    caches = [cache_swa_k.reshape(N_LAT_BATCH, DEPTH, PAST, 128),
              cache_swa_v.reshape(N_LAT_BATCH, DEPTH, PAST, 128),
              cache_dif_k.reshape(N_LAT_BATCH, DEPTH, PAST, 512),
              cache_dif_v.reshape(N_LAT_BATCH, DEPTH, PAST, 512),
              cache_mla_ckv,
              jnp.tile(cache_mla_kpe, (1, 1, 1, 4))]
    lam_init = [0.8 - 0.6 * math.exp(-0.3 * l) for l in range(DEPTH)]
    lam = (jnp.exp(jnp.sum(dif_lq1 * dif_lk1, axis=1)) - jnp.exp(jnp.sum(dif_lq2 * dif_lk2, axis=1))
           + jnp.asarray(lam_init, F32))
    scal = jnp.concatenate([swa_sink, lam[:, None]], axis=1).astype(F32)

    states = None
    for l in range(DEPTH):
        proj = _in_projection(x, mod, wts, tables, l, states)
        states = proj[11:]
        ctx_out = _attention_ctx(scal[l], wts["subln"], proj, states, l, lam_init[l])
        oa, od, om = _attention_lat(scal[l], wts["subln"], wts["wukv"], proj, caches, ctx_out, l,
                                    lam_init[l])
        xmid, h2, h2t, experts, gates = _post_attention(x, mod, oa, od, om, wts, l)
        lists = _routing_lists(experts, gates)
        acc = _moe(lists, h2t, moe_w_gate, moe_w_up, moe_w_down, l)
        x = _final(acc, h2, xmid, mod, wts, l, split=(l == DEPTH - 1))

    y_p = x[0].reshape(N_CTX_BATCH, CTX_SEQ, D_MODEL)
    y_s = x[1].reshape(N_LAT_BATCH, LAT_SEQ, D_MODEL)
    s_ka, s_va, s_kd, s_vd, s_ckv, s_kpe = states
    return (y_p, y_s,
            s_ka.reshape(N_CTX_BATCH, DEPTH, CTX_SEQ, 2, 64),
            s_va.reshape(N_CTX_BATCH, DEPTH, CTX_SEQ, 2, 64),
            s_kd.reshape(N_CTX_BATCH, DEPTH, CTX_SEQ, 4, 2, 64),
            s_vd.reshape(N_CTX_BATCH, DEPTH, CTX_SEQ, 4, 128),
            s_ckv, s_kpe)
```

```python
import functools
import math

import numpy as np
import jax
import jax.numpy as jnp
from jax import lax
from jax.experimental import pallas as pl
from jax.experimental.pallas import tpu as pltpu

F32 = jnp.float32
BF16 = jnp.bfloat16

D_MODEL = 1024
N_CTX_BATCH, CTX_SEQ = 16, 256
N_LAT_BATCH, LAT_SEQ = 2, 1024
PAST = 256
N_CTX = N_CTX_BATCH * CTX_SEQ
N_LAT = N_LAT_BATCH * LAT_SEQ
N_TOK = N_CTX + N_LAT
DEPTH = 2
GRID_W = 64
WINDOW = 128
N_EXPERTS = 64
N_GROUPS = 8
TOPK_GROUPS = 4
TOP_K = 6
F_EXPERT = 256
ROUTED_SCALE = 2.5
ROPE_BASE = 10000.0
EPS = 1e-6
NEG = -1e30

LANES = 128
ROW_TILE = 256
Q_BLOCK = 256
VMEM_LIMIT = 56 * 1024 * 1024
ACC_ROWS = 8
MOE_SPARE_TOKENS = 8
MOE_ZERO_ROWS = 64
MOE_VMEM_LIMIT = 62 * 1024 * 1024
MOE_SUB = 128
MOE_RMW_GROUP = 8
MOE_PAD = 2 * MOE_SUB
TOK_KEY = 8192
MOE_LIST_PAD_BASE = N_TOK * TOP_K + N_EXPERTS * MOE_PAD

C_QA, C_KA, C_VA, C_QD, C_KD, C_VD, C_QC, C_KVC, C_KPE, C_END = (
    0, 512, 640, 768, 1280, 1792, 2304, 2560, 2688, 2816)
GATE_SHIFT = 32

SCALE_64 = 1.0 / math.sqrt(64.0)
SCALE_96 = 1.0 / math.sqrt(96.0)

CTX_TILES = N_CTX // ROW_TILE
N_TILES = N_TOK // ROW_TILE
assert ROW_TILE == CTX_SEQ and LAT_SEQ % ROW_TILE == 0


def _mod_row_of_tile(i):
    return jnp.where(i < CTX_TILES, 0, 1 + (i - CTX_TILES) // (LAT_SEQ // ROW_TILE))


def _rms(x, g):
    return x * lax.rsqrt(jnp.mean(x * x, axis=-1, keepdims=True) + EPS) * g


def _dot(a, b):
    return jnp.dot(a, b, preferred_element_type=F32)


def _dot_nt(a, b):
    return lax.dot_general(a, b, (((1,), (1,)), ((), ())), preferred_element_type=F32)


def _silu(x):
    return x * jax.nn.sigmoid(x)


def _layer_spec(arr, layer, grid_rank):
    zeros = (0,) * (arr.ndim - 1)
    return pl.BlockSpec((None,) + arr.shape[1:], lambda *_: (layer,) + zeros)


def _mod_kernel(c_ref, w_ref, b_ref, o_ref):
    c = c_ref[...]
    o_ref[0] = _dot(_silu(c).astype(BF16), w_ref[0].astype(BF16)) + b_ref[0]


def _modulation(cvec, w_mod, b_mod):
    tn = 1536
    n = w_mod.shape[-1]
    return pl.pallas_call(
        _mod_kernel,
        out_shape=jax.ShapeDtypeStruct((DEPTH, 8, n), F32),
        grid=(DEPTH, n // tn),
        in_specs=[
            pl.BlockSpec((8, D_MODEL), lambda l, j: (0, 0)),
            pl.BlockSpec((1, D_MODEL, tn), lambda l, j: (l, 0, j)),
            pl.BlockSpec((1, 1, tn), lambda l, j: (l, 0, j)),
        ],
        out_specs=pl.BlockSpec((1, 8, tn), lambda l, j: (l, 0, j)),
        compiler_params=pltpu.CompilerParams(vmem_limit_bytes=VMEM_LIMIT),
        name="modulation",
    )(cvec, w_mod, b_mod.reshape(DEPTH, 1, n))


def _tile(i):
    return jnp.minimum(i, N_TILES - 1)


def _x_specs(x):
    if isinstance(x, tuple):
        return [pl.BlockSpec((ROW_TILE, D_MODEL), lambda i: (jnp.minimum(i, CTX_TILES - 1), 0)),
                pl.BlockSpec((ROW_TILE, D_MODEL), lambda i: (jnp.maximum(_tile(i) - CTX_TILES, 0), 0))]
    return [pl.BlockSpec((ROW_TILE, D_MODEL), lambda i: (_tile(i), 0))]


def _load_x(x_refs):
    if len(x_refs) == 1:
        return x_refs[0][...]
    return jnp.where(pl.program_id(0) < CTX_TILES, x_refs[0][...], x_refs[1][...])


def _mod_spec(layer):
    return pl.BlockSpec((None, 1, 1, 6 * D_MODEL),
                        lambda i: (layer, _mod_row_of_tile(_tile(i)), 0, 0))


def _rope128(x, cos, sin, half):
    lane = lax.broadcasted_iota(jnp.int32, x.shape, 1)
    first = (lane % (2 * half)) < half
    partner = jnp.where(first, pltpu.roll(x, LANES - half, 1), pltpu.roll(x, half, 1))
    return x * cos + partner * sin


def _rope_cols(x, cos, sin, half):
    chunks = [_rope128(x[:, c:c + LANES], cos, sin, half) for c in range(0, x.shape[1], LANES)]
    return chunks[0] if len(chunks) == 1 else jnp.concatenate(chunks, axis=1)


N_IN_INPUTS = 11


def _in_kernel(*refs, n_x):
    (mod_ref, g_ref, w1_ref, qn_ref, wuq_ref, kvn_ref, wukv_ref,
     cos64_ref, sin64_ref, cos32_ref, sin32_ref) = refs[n_x:n_x + N_IN_INPUTS]
    (qa_o, qd_o, qmn_o, qmr_o, kmn_o, vm_o, kpe4_o, ka_l, va_l, kd_l, vd_l,
     s_ka, s_va, s_kd, s_vd, s_ckv, s_kpe) = refs[-17:]
    i = pl.program_id(0)
    x = _load_x(refs[:n_x])
    m = mod_ref[0]
    sh1, sc1 = m[:, 0:D_MODEL], m[:, D_MODEL:2 * D_MODEL]
    h = _rms(x, g_ref[...]) * (1.0 + sc1) + sh1
    z = _dot(h.astype(BF16), w1_ref[...])
    cos64, sin64 = cos64_ref[...], sin64_ref[...]
    cos32, sin32 = cos32_ref[...], sin32_ref[...]
    lane = lax.broadcasted_iota(jnp.int32, (x.shape[0], LANES), 1)
    qa = _rope_cols(z[:, C_QA:C_KA], cos64, sin64, 16) * SCALE_64
    for g in range(4):
        a = qa[:, (g // 2) * LANES:(g // 2 + 1) * LANES]
        b = qa[:, (2 + g // 2) * LANES:(3 + g // 2) * LANES]
        if g % 2 == 0:
            b = pltpu.roll(b, LANES // 2, 1)
        else:
            a = pltpu.roll(a, LANES // 2, 1)
        qa_o[:, g * LANES:(g + 1) * LANES] = jnp.where(lane < 64, a, b).astype(BF16)
    qd_o[...] = (_rope_cols(z[:, C_QD:C_KD], cos64, sin64, 16) * SCALE_64).astype(BF16)
    ka = _rope_cols(z[:, C_KA:C_VA], cos64, sin64, 16)
    va = z[:, C_VA:C_QD]
    kd = _rope_cols(z[:, C_KD:C_VD], cos64, sin64, 16)
    vd = z[:, C_VD:C_QC]
    kpe = jnp.where(lane < 32, z[:, C_KPE:C_END], 0.0)
    kpe4 = kpe + pltpu.roll(kpe, 32, 1) + pltpu.roll(kpe, 64, 1) + pltpu.roll(kpe, 96, 1)
    kpe4 = _rope128(kpe4, cos32, sin32, 8)
    kpe4_o[...] = kpe4.astype(BF16)
    qn = _rms(z[:, C_QC:C_KVC], qn_ref[...])
    qm = _dot(qn.astype(BF16), wuq_ref[...]) * SCALE_96
    qmn_o[...] = qm[:, 0:512].astype(BF16)
    qmr_o[...] = _rope_cols(qm[:, 512:768], cos32, sin32, 8).astype(BF16)
    ckv = _rms(z[:, C_KVC:C_KPE], kvn_ref[...])
    kv = _dot(ckv.astype(BF16), wukv_ref[...])
    kmn_o[...] = kv[:, 0:512].astype(BF16)
    vm_o[...] = kv[:, 512:1024].astype(BF16)

    @pl.when(i < CTX_TILES)
    def _():
        s_ka[...] = ka
        s_va[...] = va
        s_kd[...] = kd
        s_vd[...] = vd
        s_ckv[...] = ckv
        s_kpe[...] = kpe4[:, 0:32]

    @pl.when(i >= CTX_TILES)
    def _():
        ka_l[...] = ka.astype(BF16)
        va_l[...] = va.astype(BF16)
        kd_l[...] = kd.astype(BF16)
        vd_l[...] = vd.astype(BF16)


STATE_WIDTHS = (128, 128, 512, 512, 128, 32)


def _in_projection(x, mod, wts, tables, layer, prev_states):
    tm = ROW_TILE

    def row(i):
        return (i, 0)

    def table_row(i):
        return (jnp.where(i < CTX_TILES, 0, 1 + (i - CTX_TILES) % (LAT_SEQ // tm)), 0)

    def lat_row(i):
        return (jnp.maximum(i - CTX_TILES, 0), 0)

    def state_block(i):
        return (jnp.minimum(i, CTX_TILES - 1), layer, 0, 0)

    table_spec = pl.BlockSpec((tm, LANES), table_row)
    weights = [wts["g_pre"], wts["w_in"], wts["qnorm"], wts["wuq"], wts["kvnorm"], wts["wukv"]]
    weight_specs = [_layer_spec(w, layer, 1) for w in weights]
    weight_specs[1] = pl.BlockSpec((None, D_MODEL, C_END), lambda i: (layer, 0, 0))
    all_rows = [(512, BF16)] * 3 + [(256, BF16), (512, BF16), (512, BF16), (128, BF16)]
    lat_rows = [(128, BF16), (128, BF16), (512, BF16), (512, BF16)]
    aliased = [] if prev_states is None else list(prev_states)
    xs = list(x) if isinstance(x, tuple) else [x]
    n_in = len(xs) + N_IN_INPUTS
    n_plain_out = len(all_rows) + len(lat_rows)
    return pl.pallas_call(
        functools.partial(_in_kernel, n_x=len(xs)),
        out_shape=[jax.ShapeDtypeStruct((N_TOK, w), dt) for w, dt in all_rows]
                  + [jax.ShapeDtypeStruct((N_LAT, w), dt) for w, dt in lat_rows]
                  + [jax.ShapeDtypeStruct((N_CTX_BATCH, DEPTH, CTX_SEQ, w), F32) for w in STATE_WIDTHS],
        grid=(N_TOK // tm,),
        in_specs=_x_specs(x) + [_mod_spec(layer)] + weight_specs + [table_spec] * 4
                 + [pl.BlockSpec(memory_space=pl.ANY)] * len(aliased),
        out_specs=[pl.BlockSpec((tm, w), row) for w, _ in all_rows]
                  + [pl.BlockSpec((tm, w), lat_row) for w, _ in lat_rows]
                  + [pl.BlockSpec((None, None, CTX_SEQ, w), state_block) for w in STATE_WIDTHS],
        input_output_aliases={n_in + k: n_plain_out + k for k in range(len(aliased))},
        compiler_params=pltpu.CompilerParams(
            dimension_semantics=("arbitrary",), vmem_limit_bytes=VMEM_LIMIT),
        name="in_projection",
    )(*xs, mod, *weights, *tables, *aliased)


def _attend(q, parts, sink=None):
    scores = []
    for k, _, mask in parts:
        s = _dot_nt(q, k)
        if mask is not None:
            s = jnp.where(mask, s, NEG)
        scores.append(s)
    m = functools.reduce(jnp.maximum, [jnp.max(s, axis=-1, keepdims=True) for s in scores])
    if sink is not None:
        m = jnp.maximum(m, sink)
    den = None
    out = None
    for s, (_, v, _) in zip(scores, parts):
        e = jnp.exp(s - m)
        d = jnp.sum(e, axis=-1, keepdims=True)
        o = _dot(e.astype(BF16), v)
        den = d if den is None else den + d
        out = o if out is None else out + o
    if sink is not None:
        den = den + jnp.exp(sink - m)
    return out / den


A_HEADS = 8


def _attend_heads(qs, parts, sinks, stack):
    rows = qs[0].shape[0]
    if not stack:
        return [_attend(q, parts, None if sinks is None else sinks[i]) for i, q in enumerate(qs)]
    sink = None
    if sinks is not None:
        sink = jnp.concatenate([jnp.full((rows, 1), s, F32) for s in sinks], axis=0)
    o = _attend(jnp.concatenate(qs, axis=0), parts, sink)
    return [o[i * rows:(i + 1) * rows] for i in range(len(qs))]


def _three_mixers(scal_ref, subln_ref, q_blocks, kv_parts, oa_o, od_o, om_o, lam_init, stack):
    qa, qd, qmn, qmr = q_blocks
    rows = qa.shape[0]
    lane = lax.broadcasted_iota(jnp.int32, (rows, LANES), 1)
    low = lane < 64
    zero = jnp.zeros((rows, LANES), BF16)

    parts = [(p["ka"], p["va"], p.get("mask_a")) for p in kv_parts]
    qs = [jnp.where(low if kvh == 0 else ~low, qa[:, g * LANES:(g + 1) * LANES], zero)
          for kvh in range(2) for g in range(4)]
    out_a = _attend_heads(qs, parts, [scal_ref[h] for h in range(A_HEADS)], stack)
    for c in range(4):
        even, odd = out_a[2 * c], out_a[2 * c + 1]
        if c < 2:
            odd = pltpu.roll(odd, LANES // 2, 1)
        else:
            even = pltpu.roll(even, LANES // 2, 1)
        oa_o[:, c * LANES:(c + 1) * LANES] = jnp.where(low, even, odd).astype(BF16)

    lam = scal_ref[8]
    subln = subln_ref[...]
    for h in range(4):
        sl = slice(h * LANES, (h + 1) * LANES)
        q128 = qd[:, sl]
        o1, o2 = _attend_heads([jnp.where(low, q128, zero), jnp.where(low, zero, q128)],
                               [(p["kd"][:, sl], p["vd"][:, sl], None) for p in kv_parts],
                               None, stack)
        od = o1 - lam * o2
        od_o[:, sl] = (_rms(od, subln) * (1.0 - lam_init)).astype(BF16)

    for i in range(4):
        sl = slice(i * LANES, (i + 1) * LANES)
        qs = []
        for sub in range(2):
            h = 2 * i + sub
            qn = jnp.where(low if sub == 0 else ~low, qmn[:, sl], zero)
            qr128 = qmr[:, (h // 4) * LANES:(h // 4 + 1) * LANES]
            qr = jnp.where((lane // 32) == (h % 4), qr128, zero)
            qs.append(jnp.concatenate([qn, qr], axis=1))
        parts = [(jnp.concatenate([p["kmn"][:, sl], p["kpe"]], axis=1), p["vm"][:, sl], None)
                 for p in kv_parts]
        o_even, o_odd = _attend_heads(qs, parts, None, stack)
        om_o[:, sl] = jnp.where(low, o_even, o_odd).astype(BF16)


def _attn_ctx_kernel(scal_ref, subln_ref, qa_ref, qd_ref, qmn_ref, qmr_ref,
                     ka_ref, va_ref, kd_ref, vd_ref, kmn_ref, vm_ref, kpe_ref,
                     oa_o, od_o, om_o, *, lam_init):
    part = dict(ka=ka_ref[...].astype(BF16), va=va_ref[...].astype(BF16),
                kd=kd_ref[...].astype(BF16), vd=vd_ref[...].astype(BF16),
                kmn=kmn_ref[...], vm=vm_ref[...], kpe=kpe_ref[...])
    _three_mixers(scal_ref, subln_ref, (qa_ref[...], qd_ref[...], qmn_ref[...], qmr_ref[...]),
                  [part], oa_o, od_o, om_o, lam_init, stack=False)


def _attn_lat_kernel(scal_ref, subln_ref, wukv_ref, qa_ref, qd_ref, qmn_ref, qmr_ref,
                     ka_ref, va_ref, kd_ref, vd_ref, kmn_ref, vm_ref, kpe_ref,
                     cka_ref, cva_ref, ckd_ref, cvd_ref, cckv_ref, ckpe_ref,
                     oa_in, od_in, om_in, oa_o, od_o, om_o, *, lam_init):
    del oa_in, od_in, om_in
    n = pl.program_id(1)
    kv_c = _dot(cckv_ref[...].astype(BF16), wukv_ref[...])
    cached = dict(ka=cka_ref[...].astype(BF16), va=cva_ref[...].astype(BF16),
                  kd=ckd_ref[...].astype(BF16), vd=cvd_ref[...].astype(BF16),
                  kmn=kv_c[:, 0:512].astype(BF16), vm=kv_c[:, 512:1024].astype(BF16),
                  kpe=ckpe_ref[...].astype(BF16))
    span = Q_BLOCK + 2 * WINDOW
    start = pl.multiple_of(jnp.clip(n * Q_BLOCK - WINDOW, 0, LAT_SEQ - span), WINDOW)
    qpos = n * Q_BLOCK + lax.broadcasted_iota(jnp.int32, (A_HEADS * Q_BLOCK, span), 0) % Q_BLOCK
    kpos = start + lax.broadcasted_iota(jnp.int32, (A_HEADS * Q_BLOCK, span), 1)
    mask_a = jnp.abs(qpos - kpos) <= WINDOW
    new = dict(ka=ka_ref[pl.ds(start, span), :], va=va_ref[pl.ds(start, span), :], mask_a=mask_a,
               kd=kd_ref[...], vd=vd_ref[...], kmn=kmn_ref[...], vm=vm_ref[...], kpe=kpe_ref[...])
    _three_mixers(scal_ref, subln_ref, (qa_ref[...], qd_ref[...], qmn_ref[...], qmr_ref[...]),
                  [cached, new], oa_o, od_o, om_o, lam_init, stack=True)


_SMEM_SPEC = pl.BlockSpec(memory_space=pltpu.SMEM)


def _attention_ctx(scal, subln, proj, states, layer, lam_init):
    qa, qd, qmn, qmr, kmn, vm, kpe4 = proj[:7]
    s_ka, s_va, s_kd, s_vd = states[:4]
    t = CTX_SEQ

    def blk(a):
        return pl.BlockSpec((t, a.shape[1]), lambda b: (b, 0))

    def sblk(a):
        return pl.BlockSpec((None, None, t, a.shape[-1]), lambda b: (b, layer, 0, 0))

    return pl.pallas_call(
        functools.partial(_attn_ctx_kernel, lam_init=lam_init),
        out_shape=[jax.ShapeDtypeStruct((N_TOK, 512), BF16)] * 3,
        grid=(N_CTX_BATCH,),
        in_specs=[_SMEM_SPEC, _layer_spec(subln, layer, 1)]
                 + [blk(a) for a in (qa, qd, qmn, qmr)]
                 + [sblk(a) for a in (s_ka, s_va, s_kd, s_vd)]
                 + [blk(a) for a in (kmn, vm, kpe4)],
        out_specs=[pl.BlockSpec((t, 512), lambda b: (b, 0))] * 3,
        compiler_params=pltpu.CompilerParams(vmem_limit_bytes=VMEM_LIMIT),
        name="attention_ctx",
    )(scal, subln, qa, qd, qmn, qmr, s_ka, s_va, s_kd, s_vd, kmn, vm, kpe4)


def _attention_lat(scal, subln, wukv, proj, caches, ctx_out, layer, lam_init):
    qa, qd, qmn, qmr, kmn, vm, kpe4, ka_l, va_l, kd_l, vd_l = proj[:11]
    nq = LAT_SEQ // Q_BLOCK
    q_off = N_CTX // Q_BLOCK
    kv_off = N_CTX // LAT_SEQ

    def qblk(a):
        return pl.BlockSpec((Q_BLOCK, a.shape[1]), lambda b, n: (q_off + b * nq + n, 0))

    def kvblk(a):
        return pl.BlockSpec((LAT_SEQ, a.shape[1]), lambda b, n: (b, 0))

    def kvblk_all(a):
        return pl.BlockSpec((LAT_SEQ, a.shape[1]), lambda b, n: (kv_off + b, 0))

    def cblk(a):
        return pl.BlockSpec((None, None, PAST, a.shape[-1]), lambda b, n: (b, layer, 0, 0))

    qs = [qa, qd, qmn, qmr]
    n_in = 3 + 4 + 7 + 6
    out_spec = pl.BlockSpec((Q_BLOCK, 512), lambda b, n: (q_off + b * nq + n, 0))
    return pl.pallas_call(
        functools.partial(_attn_lat_kernel, lam_init=lam_init),
        out_shape=[jax.ShapeDtypeStruct((N_TOK, 512), BF16)] * 3,
        grid=(N_LAT_BATCH, nq),
        in_specs=[_SMEM_SPEC, _layer_spec(subln, layer, 2), _layer_spec(wukv, layer, 2)]
                 + [qblk(a) for a in qs]
                 + [kvblk(a) for a in (ka_l, va_l, kd_l, vd_l)]
                 + [kvblk_all(a) for a in (kmn, vm, kpe4)]
                 + [cblk(a) for a in caches]
                 + [pl.BlockSpec(memory_space=pl.ANY)] * 3,
        out_specs=[out_spec] * 3,
        input_output_aliases={n_in + k: k for k in range(3)},
        compiler_params=pltpu.CompilerParams(vmem_limit_bytes=VMEM_LIMIT),
        name="attention_lat",
    )(scal, subln, wukv, *qs, ka_l, va_l, kd_l, vd_l, kmn, vm, kpe4, *caches, *ctx_out)


def _route(scores, biased):
    per = N_EXPERTS // N_GROUPS
    tm = scores.shape[1]
    sub = lax.broadcasted_iota(jnp.int32, (per, tm), 0).astype(F32)
    groups = [biased[g * per:(g + 1) * per, :] for g in range(N_GROUPS)]
    gscore = []
    for v in groups:
        m1 = jnp.max(v, axis=0, keepdims=True)
        first = jnp.min(jnp.where(v == m1, sub, float(per)), axis=0, keepdims=True)
        m2 = jnp.max(jnp.where(sub == first, -jnp.inf, v), axis=0, keepdims=True)
        gscore.append(m1 + m2)
    vals = []
    for g in range(N_GROUPS):
        rank = jnp.zeros((1, tm), F32)
        for o in range(N_GROUPS):
            if o == g:
                continue
            ahead = (gscore[o] >= gscore[g]) if o < g else (gscore[o] > gscore[g])
            rank = rank + jnp.where(ahead, 1.0, 0.0)
        vals.append(jnp.where(rank < TOPK_GROUPS, groups[g], NEG))
    idx = [sub + float(g * per) for g in range(N_GROUPS)]
    picks, weights = [], []
    for _ in range(TOP_K):
        best = functools.reduce(jnp.maximum, [jnp.max(v, axis=0, keepdims=True) for v in vals])
        cand = functools.reduce(jnp.minimum, [
            jnp.min(jnp.where(v == best, i, float(N_EXPERTS)), axis=0, keepdims=True)
            for v, i in zip(vals, idx)])
        wsel = jnp.zeros((1, tm), F32)
        for g in range(N_GROUPS):
            hit = idx[g] == cand
            wsel = wsel + jnp.sum(jnp.where(hit, scores[g * per:(g + 1) * per, :], 0.0),
                                  axis=0, keepdims=True)
            vals[g] = jnp.where(hit, -jnp.inf, vals[g])
        picks.append(cand)
        weights.append(wsel)
    total = functools.reduce(lambda a, b: a + b, weights)
    pad = [jnp.zeros((1, tm), F32)] * (8 - TOP_K)
    experts = jnp.concatenate(picks + pad, axis=0).astype(jnp.int32)
    gates = jnp.concatenate([x / total * ROUTED_SCALE for x in weights] + pad, axis=0)
    return experts, gates


def _post_kernel(*refs, n_x):
    (mod_ref, oa_ref, od_ref, om_ref, gpre_ref, wgl_ref, wa_ref, wb_ref, wc_ref, wo_ref, gpost_ref,
     gffn_ref, rwt_ref, rb_ref, xmid_o, h2_o, h2t_o, experts_o, gates_o) = refs[n_x:]

    @pl.when(pl.program_id(0) < N_TILES)
    def _():
        x = _load_x(refs[:n_x])
        m = mod_ref[0]
        d = D_MODEL
        sh1, sc1, g1, sh2, sc2 = (m[:, 0:d], m[:, d:2 * d], m[:, 2 * d:3 * d], m[:, 3 * d:4 * d],
                                  m[:, 4 * d:5 * d])
        h = _rms(x, gpre_ref[...]) * (1.0 + sc1) + sh1
        zg = _dot(h.astype(BF16), wgl_ref[:, C_KPE:])
        lane = lax.broadcasted_iota(jnp.int32, (zg.shape[0], LANES), 1)
        moved = [pltpu.roll(zg[:, k * LANES:(k + 1) * LANES], LANES - GATE_SHIFT, 1)
                 for k in range(3 * d // LANES + 1)]
        gate = jax.nn.sigmoid(jnp.concatenate(
            [jnp.where(lane < LANES - GATE_SHIFT, moved[k], moved[k + 1])
             for k in range(3 * d // LANES)], axis=1))
        merged = (gate[:, 0:d] * _dot(oa_ref[...], wa_ref[...])
                  + gate[:, d:2 * d] * _dot(od_ref[...], wb_ref[...])
                  + gate[:, 2 * d:3 * d] * _dot(om_ref[...], wc_ref[...]))
        a = _dot(merged.astype(BF16), wo_ref[...])
        xm = x + g1 * _rms(a, gpost_ref[...])
        xmid_o[...] = xm
        h2f = _rms(xm, gffn_ref[...]) * (1.0 + sc2) + sh2
        h2 = h2f.astype(BF16)
        h2_o[...] = h2
        for c in range(ACC_ROWS):
            h2t_o[pl.ds(c, x.shape[0], stride=ACC_ROWS), :] = h2f[:, c * LANES:(c + 1) * LANES]
        scores = jax.nn.sigmoid(_dot_nt(rwt_ref[...], h2))
        experts, gates = _route(scores, scores + rb_ref[...])
        experts_o[...] = experts
        gates_o[...] = gates

    @pl.when(pl.program_id(0) == N_TILES)
    def _():
        h2t_o[...] = jnp.zeros_like(h2t_o)


def _post_attention(x, mod, oa, od, om, wts, layer):
    tm = ROW_TILE

    def row(i):
        return (_tile(i), 0)

    weights = [wts[k] for k in ("g_pre", "w_in", "wa", "wb", "wc", "wo", "g_post", "g_ffn", "rwt", "rb")]
    xs = list(x) if isinstance(x, tuple) else [x]
    return pl.pallas_call(
        functools.partial(_post_kernel, n_x=len(xs)),
        out_shape=[jax.ShapeDtypeStruct((N_TOK, D_MODEL), F32),
                   jax.ShapeDtypeStruct((N_TOK, D_MODEL), BF16),
                   jax.ShapeDtypeStruct(((N_TOK + tm) * ACC_ROWS, LANES), F32),
                   jax.ShapeDtypeStruct((8, N_TOK), jnp.int32),
                   jax.ShapeDtypeStruct((8, N_TOK), F32)],
        grid=(N_TILES + 1,),
        in_specs=_x_specs(x) + [_mod_spec(layer),
                  pl.BlockSpec((tm, 512), row), pl.BlockSpec((tm, 512), row),
                  pl.BlockSpec((tm, 512), row)]
                 + [_layer_spec(w, layer, 1) for w in weights],
        out_specs=[pl.BlockSpec((tm, D_MODEL), row), pl.BlockSpec((tm, D_MODEL), row),
                   pl.BlockSpec((tm * ACC_ROWS, LANES), lambda i: (i, 0)),
                   pl.BlockSpec((8, tm), lambda i: (0, _tile(i))),
                   pl.BlockSpec((8, tm), lambda i: (0, _tile(i)))],
        compiler_params=pltpu.CompilerParams(
            dimension_semantics=("arbitrary",), vmem_limit_bytes=VMEM_LIMIT),
        name="post_attention",
    )(*xs, mod, oa, od, om, *weights)


def _ffn(x, wg, wu):
    return _silu(_dot(x, wg)) * _dot(x, wu)


def _moe_kernel(start_ref, count_ref, dst_ref, gate_ref, xs_ref, wg_ref, wu_ref, wd_ref,
                acc_ref, xa, xb, ya, yb, wgb, wub, wdb, pend):
    e = pl.program_id(0)

    @pl.when(e == 0)
    def _():
        def clear(i, carry):
            r0 = pl.multiple_of(i * MOE_ZERO_ROWS, MOE_ZERO_ROWS)
            acc_ref[pl.ds(r0, MOE_ZERO_ROWS), :] = jnp.zeros((MOE_ZERO_ROWS, LANES), F32)
            return carry

        lax.fori_loop(0, acc_ref.shape[0] // MOE_ZERO_ROWS, clear, 0)

        yb[...] = jnp.zeros_like(yb)
        pend[0] = MOE_LIST_PAD_BASE

    wgb[...] = wg_ref[0].astype(BF16)
    wub[...] = wu_ref[0].astype(BF16)
    wdb[...] = wd_ref[0].astype(BF16)
    start = start_ref[e]
    pairs = (count_ref[e] + 2 * MOE_SUB - 1) // (2 * MOE_SUB)

    def gather(base, xbuf):
        for mi in range(MOE_SUB):
            src = pl.multiple_of(dst_ref[base + mi], ACC_ROWS)
            xbuf[mi * ACC_ROWS:(mi + 1) * ACC_ROWS, :] = xs_ref[pl.ds(src, ACC_ROWS), :]

    def expert_ffn(xbuf, ybuf):
        x = jnp.concatenate([xbuf[pl.ds(c, MOE_SUB, stride=ACC_ROWS), :].astype(BF16)
                             for c in range(ACC_ROWS)], axis=1)
        hid = _ffn(x, wgb[...], wub[...])
        y = _dot(hid.astype(BF16), wdb[...])
        for c in range(ACC_ROWS):
            ybuf[pl.ds(c, MOE_SUB, stride=ACC_ROWS), :] = y[:, c * LANES:(c + 1) * LANES]

    def scatter(base, ybuf):
        for g0 in range(0, MOE_SUB, MOE_RMW_GROUP):
            updates = []
            for mi in range(g0, g0 + MOE_RMW_GROUP):
                dst = pl.multiple_of(dst_ref[base + mi], ACC_ROWS)
                yv = ybuf[mi * ACC_ROWS:(mi + 1) * ACC_ROWS, :]
                updates.append((dst, acc_ref[pl.ds(dst, ACC_ROWS), :] + gate_ref[base + mi] * yv))
            for dst, val in updates:
                acc_ref[pl.ds(dst, ACC_ROWS), :] = val

    start_next = start_ref[jnp.minimum(e + 1, N_EXPERTS - 1)]
    prev_count = count_ref[jnp.maximum(e - 1, 0)]

    @pl.when((pairs > 0) & ((e == 0) | (prev_count == 0)))
    def _():
        gather(start, xa)

    def pair(p, carry):
        base = start + p * (2 * MOE_SUB)
        gather(base + MOE_SUB, xb)
        expert_ffn(xa, ya)
        scatter(pend[0], yb)
        gather(jnp.where(p + 1 < pairs, base + 2 * MOE_SUB, start_next), xa)
        expert_ffn(xb, yb)
        scatter(base, ya)
        pend[0] = base + MOE_SUB
        return carry

    lax.fori_loop(0, pairs, pair, 0)

    @pl.when(e == N_EXPERTS - 1)
    def _():
        scatter(pend[0], yb)


def _moe(lists, h2t, wg, wu, wd, layer):
    def expert(e, *_):
        return (layer, e, 0, 0)

    def whole(e, *_):
        return (0, 0)

    acc_rows = (N_TOK + MOE_SPARE_TOKENS) * ACC_ROWS
    assert acc_rows % MOE_ZERO_ROWS == 0
    return pl.pallas_call(
        _moe_kernel,
        out_shape=jax.ShapeDtypeStruct((acc_rows, LANES), F32),
        grid_spec=pltpu.PrefetchScalarGridSpec(
            num_scalar_prefetch=len(lists),
            grid=(N_EXPERTS,),
            in_specs=[pl.BlockSpec((acc_rows, LANES), whole, pipeline_mode=pl.Buffered(1)),
                      pl.BlockSpec((None, 1, D_MODEL, F_EXPERT), expert),
                      pl.BlockSpec((None, 1, D_MODEL, F_EXPERT), expert),
                      pl.BlockSpec((None, 1, F_EXPERT, D_MODEL), expert)],
            out_specs=pl.BlockSpec((acc_rows, LANES), whole, pipeline_mode=pl.Buffered(1)),
            scratch_shapes=[pltpu.VMEM((MOE_SUB * ACC_ROWS, LANES), F32)] * 4
                           + [pltpu.VMEM((D_MODEL, F_EXPERT), BF16),
                              pltpu.VMEM((D_MODEL, F_EXPERT), BF16),
                              pltpu.VMEM((F_EXPERT, D_MODEL), BF16),
                              pltpu.SMEM((1,), jnp.int32)]),
        compiler_params=pltpu.CompilerParams(
            dimension_semantics=("arbitrary",), vmem_limit_bytes=MOE_VMEM_LIMIT),
        name="moe",
    )(*lists, h2t, wg, wu, wd)


def _final_kernel(acc_ref, h2_ref, xmid_ref, mod_ref, swg_ref, swu_ref, swd_ref, gpost_ref, *out_refs):
    tm = h2_ref.shape[0]
    routed = jnp.concatenate(
        [acc_ref[pl.ds(c, tm, stride=ACC_ROWS), :] for c in range(ACC_ROWS)], axis=1)
    hid = _ffn(h2_ref[...], swg_ref[...], swu_ref[...])
    shared = _dot(hid.astype(BF16), swd_ref[...])
    g2 = mod_ref[0][:, 5 * D_MODEL:6 * D_MODEL]
    out = xmid_ref[...] + g2 * _rms(routed + shared, gpost_ref[...])
    if len(out_refs) == 1:
        out_refs[0][...] = out
    else:
        @pl.when(pl.program_id(0) < CTX_TILES)
        def _():
            out_refs[0][...] = out

        @pl.when(pl.program_id(0) >= CTX_TILES)
        def _():
            out_refs[1][...] = out


def _final(acc, h2, xmid, mod, wts, layer, split):
    tm = ROW_TILE

    def row(i):
        return (i, 0)

    weights = [wts[k] for k in ("swg", "swu", "swd", "g_ffn_post")]
    return pl.pallas_call(
        _final_kernel,
        out_shape=([jax.ShapeDtypeStruct((N_CTX, D_MODEL), F32),
                    jax.ShapeDtypeStruct((N_LAT, D_MODEL), F32)] if split
                   else jax.ShapeDtypeStruct((N_TOK, D_MODEL), F32)),
        grid=(N_TOK // tm,),
        in_specs=[pl.BlockSpec((tm * ACC_ROWS, LANES), row),
                  pl.BlockSpec((tm, D_MODEL), row),
                  pl.BlockSpec((tm, D_MODEL), row),
                  _mod_spec(layer)]
                 + [_layer_spec(w, layer, 1) for w in weights],
        out_specs=_x_specs((None, None)) if split else pl.BlockSpec((tm, D_MODEL), row),
        compiler_params=pltpu.CompilerParams(
            dimension_semantics=("arbitrary",), vmem_limit_bytes=VMEM_LIMIT),
        name="ffn_final",
    )(acc, h2, xmid, mod, *weights)


def _routing_lists(experts, gates):
    e = experts[:TOP_K].reshape(-1)
    tok = jnp.tile(jnp.arange(N_TOK, dtype=jnp.int32), TOP_K)
    pad_e = jnp.repeat(jnp.arange(N_EXPERTS, dtype=jnp.int32), MOE_PAD)
    pad_tok = N_TOK + jnp.tile(jnp.arange(MOE_PAD, dtype=jnp.int32), N_EXPERTS)
    keys = jnp.concatenate([e * TOK_KEY + tok, pad_e * TOK_KEY + pad_tok])
    vals = jnp.concatenate([gates[:TOP_K].reshape(-1), jnp.zeros((N_EXPERTS * MOE_PAD,), F32)])
    keys, gate_sorted = lax.sort((keys, vals), num_keys=1)
    tail = jnp.full((MOE_SUB,), N_TOK, jnp.int32)
    tok_sorted = jnp.concatenate([keys & (TOK_KEY - 1), tail])
    gate_sorted = jnp.concatenate([gate_sorted, jnp.zeros((MOE_SUB,), F32)])
    dst = jnp.minimum(tok_sorted, N_TOK) * ACC_ROWS
    counts = jnp.sum((e[None, :] == jnp.arange(N_EXPERTS, dtype=jnp.int32)[:, None]).astype(jnp.int32),
                     axis=1)
    start = jnp.cumsum(counts) - counts + MOE_PAD * jnp.arange(N_EXPERTS, dtype=jnp.int32)
    return start.astype(jnp.int32), counts, dst, gate_sorted


def _rope_tables():
    t = np.arange(LAT_SEQ)
    pos = np.stack([t // GRID_W, t % GRID_W], axis=1).astype(np.float64)

    def table(half):
        inv = ROPE_BASE ** (-np.arange(half, dtype=np.float64) / half)
        lane = np.arange(LANES)
        axis = (lane // (2 * half)) % 2
        freq = inv[lane % half]
        ang = pos[:, axis] * freq[None, :]
        sign = np.where((lane % (2 * half)) < half, -1.0, 1.0)
        cos = np.concatenate([np.ones((ROW_TILE, LANES)), np.cos(ang)], axis=0)
        sin = np.concatenate([np.zeros((ROW_TILE, LANES)), np.sin(ang) * sign[None, :]], axis=0)
        return jnp.asarray(cos, F32), jnp.asarray(sin, F32)

    cos64, sin64 = table(16)
    cos32, sin32 = table(8)
    return cos64, sin64, cos32, sin32


def _prepare_weights(norm_attn_pre, norm_attn_post, norm_ffn_pre, norm_ffn_post, w_in, dif_subln,
                     mla_q_norm, mla_w_uq, mla_kv_norm, mla_w_ukv, w_branch_a, w_branch_b,
                     w_branch_c, w_out, router_w, router_bias, shared_w_gate, shared_w_up,
                     shared_w_down):
    dp = DEPTH
    in_cols = w_in.shape[2]
    w_in_b = jnp.pad(w_in, ((0, 0), (0, 0), (0, C_KPE + 3 * D_MODEL + LANES - in_cols))).astype(BF16)
    uq = mla_w_uq.reshape(dp, 256, 8, 96)
    wuq = jnp.concatenate([uq[..., :64].reshape(dp, 256, 512), uq[..., 64:].reshape(dp, 256, 256)],
                          axis=2).astype(BF16)
    ukv = mla_w_ukv.reshape(dp, 128, 8, 128)
    wukv = jnp.concatenate([ukv[..., :64].reshape(dp, 128, 512), ukv[..., 64:].reshape(dp, 128, 512)],
                           axis=2).astype(BF16)
    return dict(
        g_pre=norm_attn_pre[:, None, :], g_post=norm_attn_post[:, None, :],
        g_ffn=norm_ffn_pre[:, None, :], g_ffn_post=norm_ffn_post[:, None, :],
        w_in=w_in_b,
        qnorm=mla_q_norm[:, None, :], wuq=wuq, kvnorm=mla_kv_norm[:, None, :], wukv=wukv,
        subln=dif_subln[:, None, :],
        wa=w_branch_a.astype(BF16), wb=w_branch_b.astype(BF16), wc=w_branch_c.astype(BF16),
        wo=w_out.astype(BF16), rwt=router_w.transpose(0, 2, 1).astype(BF16),
        rb=router_bias[:, :, None],
        swg=shared_w_gate.astype(BF16), swu=shared_w_up.astype(BF16), swd=shared_w_down.astype(BF16))


def kernel(x_prompt, x_sample, cache_swa_k, cache_swa_v, cache_dif_k, cache_dif_v, cache_mla_ckv, cache_mla_kpe, c, c_ctx, w_mod, b_mod, norm_attn_pre, norm_attn_post, norm_ffn_pre, norm_ffn_post, w_in, swa_sink, dif_lq1, dif_lk1, dif_lq2, dif_lk2, dif_subln, mla_q_norm, mla_w_uq, mla_kv_norm, mla_w_ukv, w_branch_a, w_branch_b, w_branch_c, w_out, router_w, router_bias, moe_w_gate, moe_w_up, moe_w_down, shared_w_gate, shared_w_up, shared_w_down):
    x = (x_prompt.reshape(N_CTX, D_MODEL), x_sample.reshape(N_LAT, D_MODEL))
    cvec = jnp.concatenate([c_ctx[None, :], c, jnp.zeros((8 - 1 - N_LAT_BATCH, D_MODEL), F32)], axis=0)
    mod = _modulation(cvec, w_mod, b_mod).reshape(DEPTH, 8, 1, 6 * D_MODEL)
    tables = _rope_tables()
    wts = _prepare_weights(norm_attn_pre, norm_attn_post, norm_ffn_pre, norm_ffn_post, w_in,
                           dif_subln, mla_q_norm, mla_w_uq, mla_kv_norm, mla_w_ukv, w_branch_a,
                           w_branch_b, w_branch_c, w_out, router_w, router_bias, shared_w_gate,
                           shared_w_up, shared_w_down)
    caches = [cache_swa_k.reshape(N_LAT_BATCH, DEPTH, PAST, 128),
              cache_swa_v.reshape(N_LAT_BATCH, DEPTH, PAST, 128),
              cache_dif_k.reshape(N_LAT_BATCH, DEPTH, PAST, 512),
              cache_dif_v.reshape(N_LAT_BATCH, DEPTH, PAST, 512),
              cache_mla_ckv,
              jnp.tile(cache_mla_kpe, (1, 1, 1, 4))]
    lam_init = [0.8 - 0.6 * math.exp(-0.3 * l) for l in range(DEPTH)]
    lam = (jnp.exp(jnp.sum(dif_lq1 * dif_lk1, axis=1)) - jnp.exp(jnp.sum(dif_lq2 * dif_lk2, axis=1))
           + jnp.asarray(lam_init, F32))
    scal = jnp.concatenate([swa_sink, lam[:, None]], axis=1).astype(F32)

    states = None
    for l in range(DEPTH):
        proj = _in_projection(x, mod, wts, tables, l, states)
        states = proj[11:]
        ctx_out = _attention_ctx(scal[l], wts["subln"], proj, states, l, lam_init[l])
        oa, od, om = _attention_lat(scal[l], wts["subln"], wts["wukv"], proj, caches, ctx_out, l,
                                    lam_init[l])
        xmid, h2, h2t, experts, gates = _post_attention(x, mod, oa, od, om, wts, l)
        lists = _routing_lists(experts, gates)
        acc = _moe(lists, h2t, moe_w_gate, moe_w_up, moe_w_down, l)
        x = _final(acc, h2, xmid, mod, wts, l, split=(l == DEPTH - 1))

    y_p = x[0].reshape(N_CTX_BATCH, CTX_SEQ, D_MODEL)
    y_s = x[1].reshape(N_LAT_BATCH, LAT_SEQ, D_MODEL)
    s_ka, s_va, s_kd, s_vd, s_ckv, s_kpe = states
    return (y_p, y_s,
            s_ka.reshape(N_CTX_BATCH, DEPTH, CTX_SEQ, 2, 64),
            s_va.reshape(N_CTX_BATCH, DEPTH, CTX_SEQ, 2, 64),
            s_kd.reshape(N_CTX_BATCH, DEPTH, CTX_SEQ, 4, 2, 64),
            s_vd.reshape(N_CTX_BATCH, DEPTH, CTX_SEQ, 4, 128),
            s_ckv, s_kpe)
```

```python
import functools
import math

import numpy as np
import jax
import jax.numpy as jnp
from jax import lax
from jax.experimental import pallas as pl
from jax.experimental.pallas import tpu as pltpu

F32 = jnp.float32
BF16 = jnp.bfloat16

D_MODEL = 1024
N_CTX_BATCH, CTX_SEQ = 16, 256
N_LAT_BATCH, LAT_SEQ = 2, 1024
PAST = 256
N_CTX = N_CTX_BATCH * CTX_SEQ
N_LAT = N_LAT_BATCH * LAT_SEQ
N_TOK = N_CTX + N_LAT
DEPTH = 2
GRID_W = 64
WINDOW = 128
N_EXPERTS = 64
N_GROUPS = 8
TOPK_GROUPS = 4
TOP_K = 6
F_EXPERT = 256
ROUTED_SCALE = 2.5
ROPE_BASE = 10000.0
EPS = 1e-6
NEG = -1e30

LANES = 128
ROW_TILE = 256
Q_BLOCK = 256
VMEM_LIMIT = 56 * 1024 * 1024
ACC_ROWS = 8
MOE_SPARE_TOKENS = 8
MOE_ZERO_ROWS = 64
MOE_VMEM_LIMIT = 62 * 1024 * 1024
MOE_SUB = 128
MOE_RMW_GROUP = 8
MOE_PAD = 2 * MOE_SUB
TOK_KEY = 8192
MOE_LIST_PAD_BASE = N_TOK * TOP_K + N_EXPERTS * MOE_PAD

C_QA, C_KA, C_VA, C_QD, C_KD, C_VD, C_QC, C_KVC, C_KPE, C_END = (
    0, 512, 640, 768, 1280, 1792, 2304, 2560, 2688, 2816)
SCALE_64 = 1.0 / math.sqrt(64.0)
SCALE_96 = 1.0 / math.sqrt(96.0)

CTX_TILES = N_CTX // ROW_TILE
N_TILES = N_TOK // ROW_TILE
assert ROW_TILE == CTX_SEQ and LAT_SEQ % ROW_TILE == 0


def _mod_row_of_tile(i):
    return jnp.where(i < CTX_TILES, 0, 1 + (i - CTX_TILES) // (LAT_SEQ // ROW_TILE))


def _rms(x, g):
    return x * lax.rsqrt(jnp.mean(x * x, axis=-1, keepdims=True) + EPS) * g


def _dot(a, b):
    return jnp.dot(a, b, preferred_element_type=F32)


def _dot_nt(a, b):
    return lax.dot_general(a, b, (((1,), (1,)), ((), ())), preferred_element_type=F32)


def _silu(x):
    return x * jax.nn.sigmoid(x)


def _layer_spec(arr, layer, grid_rank):
    zeros = (0,) * (arr.ndim - 1)
    return pl.BlockSpec((None,) + arr.shape[1:], lambda *_: (layer,) + zeros)


def _mod_kernel(c_ref, w_ref, b_ref, o_ref):
    c = c_ref[...]
    o_ref[0] = _dot(_silu(c).astype(BF16), w_ref[0].astype(BF16)) + b_ref[0]


def _modulation(cvec, w_mod, b_mod):
    tn = 1536
    n = w_mod.shape[-1]
    return pl.pallas_call(
        _mod_kernel,
        out_shape=jax.ShapeDtypeStruct((DEPTH, 8, n), F32),
        grid=(DEPTH, n // tn),
        in_specs=[
            pl.BlockSpec((8, D_MODEL), lambda l, j: (0, 0)),
            pl.BlockSpec((1, D_MODEL, tn), lambda l, j: (l, 0, j)),
            pl.BlockSpec((1, 1, tn), lambda l, j: (l, 0, j)),
        ],
        out_specs=pl.BlockSpec((1, 8, tn), lambda l, j: (l, 0, j)),
        compiler_params=pltpu.CompilerParams(vmem_limit_bytes=VMEM_LIMIT),
        name="modulation",
    )(cvec, w_mod, b_mod.reshape(DEPTH, 1, n))


def _tile(i):
    return jnp.minimum(i, N_TILES - 1)


def _x_specs(x):
    if isinstance(x, tuple):
        return [pl.BlockSpec((ROW_TILE, D_MODEL), lambda i: (jnp.minimum(i, CTX_TILES - 1), 0)),
                pl.BlockSpec((ROW_TILE, D_MODEL), lambda i: (jnp.maximum(_tile(i) - CTX_TILES, 0), 0))]
    return [pl.BlockSpec((ROW_TILE, D_MODEL), lambda i: (_tile(i), 0))]


def _load_x(x_refs):
    if len(x_refs) == 1:
        return x_refs[0][...]
    return jnp.where(pl.program_id(0) < CTX_TILES, x_refs[0][...], x_refs[1][...])


def _mod_spec(layer):
    return pl.BlockSpec((None, 1, 1, 6 * D_MODEL),
                        lambda i: (layer, _mod_row_of_tile(_tile(i)), 0, 0))


def _rope128(x, cos, sin, half):
    lane = lax.broadcasted_iota(jnp.int32, x.shape, 1)
    first = (lane % (2 * half)) < half
    partner = jnp.where(first, pltpu.roll(x, LANES - half, 1), pltpu.roll(x, half, 1))
    return x * cos + partner * sin


def _rope_cols(x, cos, sin, half):
    chunks = [_rope128(x[:, c:c + LANES], cos, sin, half) for c in range(0, x.shape[1], LANES)]
    return chunks[0] if len(chunks) == 1 else jnp.concatenate(chunks, axis=1)


N_IN_INPUTS = 11


def _in_kernel(*refs, n_x):
    (mod_ref, g_ref, w1_ref, qn_ref, wuq_ref, kvn_ref, wukv_ref,
     cos64_ref, sin64_ref, cos32_ref, sin32_ref) = refs[n_x:n_x + N_IN_INPUTS]
    (qa_o, qd_o, qmn_o, qmr_o, kmn_o, vm_o, kpe4_o, ka_l, va_l, kd_l, vd_l,
     s_ka, s_va, s_kd, s_vd, s_ckv, s_kpe) = refs[-17:]
    i = pl.program_id(0)
    x = _load_x(refs[:n_x])
    m = mod_ref[0]
    sh1, sc1 = m[:, 0:D_MODEL], m[:, D_MODEL:2 * D_MODEL]
    h = _rms(x, g_ref[...]) * (1.0 + sc1) + sh1
    z = _dot(h.astype(BF16), w1_ref[...])
    cos64, sin64 = cos64_ref[...], sin64_ref[...]
    cos32, sin32 = cos32_ref[...], sin32_ref[...]
    qa_o[...] = (_rope_cols(z[:, C_QA:C_KA], cos64, sin64, 16) * SCALE_64).astype(BF16)
    qd_o[...] = (_rope_cols(z[:, C_QD:C_KD], cos64, sin64, 16) * SCALE_64).astype(BF16)
    ka = _rope_cols(z[:, C_KA:C_VA], cos64, sin64, 16)
    va = z[:, C_VA:C_QD]
    kd = _rope_cols(z[:, C_KD:C_VD], cos64, sin64, 16)
    vd = z[:, C_VD:C_QC]
    kpe4 = _rope_cols(z[:, C_KPE:C_END], cos32, sin32, 8)
    kpe4_o[...] = kpe4.astype(BF16)
    qn = _rms(z[:, C_QC:C_KVC], qn_ref[...])
    qm = _dot(qn.astype(BF16), wuq_ref[...]) * SCALE_96
    qmn_o[...] = qm[:, 0:512].astype(BF16)
    qmr_o[...] = _rope_cols(qm[:, 512:768], cos32, sin32, 8).astype(BF16)
    ckv = _rms(z[:, C_KVC:C_KPE], kvn_ref[...])
    kv = _dot(ckv.astype(BF16), wukv_ref[...])
    kmn_o[...] = kv[:, 0:512].astype(BF16)
    vm_o[...] = kv[:, 512:1024].astype(BF16)

    @pl.when(i < CTX_TILES)
    def _():
        s_ka[...] = ka
        s_va[...] = va
        s_kd[...] = kd
        s_vd[...] = vd
        s_ckv[...] = ckv
        s_kpe[...] = kpe4[:, 0:32]

    @pl.when(i >= CTX_TILES)
    def _():
        ka_l[...] = ka.astype(BF16)
        va_l[...] = va.astype(BF16)
        kd_l[...] = kd.astype(BF16)
        vd_l[...] = vd.astype(BF16)


STATE_WIDTHS = (128, 128, 512, 512, 128, 32)


def _in_projection(x, mod, wts, tables, layer, prev_states):
    tm = ROW_TILE

    def row(i):
        return (i, 0)

    def table_row(i):
        return (jnp.where(i < CTX_TILES, 0, 1 + (i - CTX_TILES) % (LAT_SEQ // tm)), 0)

    def lat_row(i):
        return (jnp.maximum(i - CTX_TILES, 0), 0)

    def state_block(i):
        return (jnp.minimum(i, CTX_TILES - 1), layer, 0, 0)

    table_spec = pl.BlockSpec((tm, LANES), table_row)
    weights = [wts["g_pre"], wts["w1"], wts["qnorm"], wts["wuq"], wts["kvnorm"], wts["wukv"]]
    all_rows = [(512, BF16)] * 3 + [(256, BF16), (512, BF16), (512, BF16), (128, BF16)]
    lat_rows = [(128, BF16), (128, BF16), (512, BF16), (512, BF16)]
    aliased = [] if prev_states is None else list(prev_states)
    xs = list(x) if isinstance(x, tuple) else [x]
    n_in = len(xs) + N_IN_INPUTS
    n_plain_out = len(all_rows) + len(lat_rows)
    return pl.pallas_call(
        functools.partial(_in_kernel, n_x=len(xs)),
        out_shape=[jax.ShapeDtypeStruct((N_TOK, w), dt) for w, dt in all_rows]
                  + [jax.ShapeDtypeStruct((N_LAT, w), dt) for w, dt in lat_rows]
                  + [jax.ShapeDtypeStruct((N_CTX_BATCH, DEPTH, CTX_SEQ, w), F32) for w in STATE_WIDTHS],
        grid=(N_TOK // tm,),
        in_specs=_x_specs(x) + [_mod_spec(layer)]
                 + [_layer_spec(w, layer, 1) for w in weights]
                 + [table_spec] * 4
                 + [pl.BlockSpec(memory_space=pl.ANY)] * len(aliased),
        out_specs=[pl.BlockSpec((tm, w), row) for w, _ in all_rows]
                  + [pl.BlockSpec((tm, w), lat_row) for w, _ in lat_rows]
                  + [pl.BlockSpec((None, None, CTX_SEQ, w), state_block) for w in STATE_WIDTHS],
        input_output_aliases={n_in + k: n_plain_out + k for k in range(len(aliased))},
        compiler_params=pltpu.CompilerParams(
            dimension_semantics=("arbitrary",), vmem_limit_bytes=VMEM_LIMIT),
        name="in_projection",
    )(*xs, mod, *weights, *tables, *aliased)


def _attend(q, parts, sink=None):
    scores = []
    for k, _, mask in parts:
        s = _dot_nt(q, k)
        if mask is not None:
            s = jnp.where(mask, s, NEG)
        scores.append(s)
    m = functools.reduce(jnp.maximum, [jnp.max(s, axis=-1, keepdims=True) for s in scores])
    if sink is not None:
        m = jnp.maximum(m, sink)
    den = None
    out = None
    for s, (_, v, _) in zip(scores, parts):
        e = jnp.exp(s - m)
        d = jnp.sum(e, axis=-1, keepdims=True)
        o = _dot(e.astype(BF16), v)
        den = d if den is None else den + d
        out = o if out is None else out + o
    if sink is not None:
        den = den + jnp.exp(sink - m)
    return out / den


A_HEADS = 8


def _attend_heads(qs, parts, sinks, stack):
    rows = qs[0].shape[0]
    if not stack:
        return [_attend(q, parts, None if sinks is None else sinks[i]) for i, q in enumerate(qs)]
    sink = None
    if sinks is not None:
        sink = jnp.concatenate([jnp.full((rows, 1), s, F32) for s in sinks], axis=0)
    o = _attend(jnp.concatenate(qs, axis=0), parts, sink)
    return [o[i * rows:(i + 1) * rows] for i in range(len(qs))]


def _three_mixers(scal_ref, subln_ref, q_blocks, kv_parts, oa_o, od_o, om_o, lam_init, stack):
    qa, qd, qmn, qmr = q_blocks
    rows = qa.shape[0]
    lane = lax.broadcasted_iota(jnp.int32, (rows, LANES), 1)
    low = lane < 64
    zero = jnp.zeros((rows, LANES), BF16)

    parts = [(p["ka"], p["va"], p.get("mask_a")) for p in kv_parts]
    qs = [jnp.where(low if kvh == 0 else ~low, qa[:, g * LANES:(g + 1) * LANES], zero)
          for kvh in range(2) for g in range(4)]
    out_a = _attend_heads(qs, parts, [scal_ref[h] for h in range(A_HEADS)], stack)
    for g in range(4):
        oa_o[:, g * LANES:(g + 1) * LANES] = jnp.where(low, out_a[g], out_a[4 + g]).astype(BF16)

    lam = scal_ref[8]
    subln = subln_ref[...]
    for h in range(4):
        sl = slice(h * LANES, (h + 1) * LANES)
        q128 = qd[:, sl]
        o1, o2 = _attend_heads([jnp.where(low, q128, zero), jnp.where(low, zero, q128)],
                               [(p["kd"][:, sl], p["vd"][:, sl], None) for p in kv_parts],
                               None, stack)
        od = o1 - lam * o2
        od_o[:, sl] = (_rms(od, subln) * (1.0 - lam_init)).astype(BF16)

    for i in range(4):
        sl = slice(i * LANES, (i + 1) * LANES)
        qs = []
        for sub in range(2):
            h = 2 * i + sub
            qn = jnp.where(low if sub == 0 else ~low, qmn[:, sl], zero)
            qr128 = qmr[:, (h // 4) * LANES:(h // 4 + 1) * LANES]
            qr = jnp.where((lane // 32) == (h % 4), qr128, zero)
            qs.append(jnp.concatenate([qn, qr], axis=1))
        parts = [(jnp.concatenate([p["kmn"][:, sl], p["kpe"]], axis=1), p["vm"][:, sl], None)
                 for p in kv_parts]
        o_even, o_odd = _attend_heads(qs, parts, None, stack)
        om_o[:, sl] = jnp.where(low, o_even, o_odd).astype(BF16)


def _attn_ctx_kernel(scal_ref, subln_ref, qa_ref, qd_ref, qmn_ref, qmr_ref,
                     ka_ref, va_ref, kd_ref, vd_ref, kmn_ref, vm_ref, kpe_ref,
                     oa_o, od_o, om_o, *, lam_init):
    part = dict(ka=ka_ref[...].astype(BF16), va=va_ref[...].astype(BF16),
                kd=kd_ref[...].astype(BF16), vd=vd_ref[...].astype(BF16),
                kmn=kmn_ref[...], vm=vm_ref[...], kpe=kpe_ref[...])
    _three_mixers(scal_ref, subln_ref, (qa_ref[...], qd_ref[...], qmn_ref[...], qmr_ref[...]),
                  [part], oa_o, od_o, om_o, lam_init, stack=False)


def _attn_lat_kernel(scal_ref, subln_ref, wukv_ref, qa_ref, qd_ref, qmn_ref, qmr_ref,
                     ka_ref, va_ref, kd_ref, vd_ref, kmn_ref, vm_ref, kpe_ref,
                     cka_ref, cva_ref, ckd_ref, cvd_ref, cckv_ref, ckpe_ref,
                     oa_in, od_in, om_in, oa_o, od_o, om_o, *, lam_init):
    del oa_in, od_in, om_in
    n = pl.program_id(1)
    kv_c = _dot(cckv_ref[...].astype(BF16), wukv_ref[...])
    cached = dict(ka=cka_ref[...].astype(BF16), va=cva_ref[...].astype(BF16),
                  kd=ckd_ref[...].astype(BF16), vd=cvd_ref[...].astype(BF16),
                  kmn=kv_c[:, 0:512].astype(BF16), vm=kv_c[:, 512:1024].astype(BF16),
                  kpe=ckpe_ref[...].astype(BF16))
    span = Q_BLOCK + 2 * WINDOW
    start = pl.multiple_of(jnp.clip(n * Q_BLOCK - WINDOW, 0, LAT_SEQ - span), WINDOW)
    qpos = n * Q_BLOCK + lax.broadcasted_iota(jnp.int32, (A_HEADS * Q_BLOCK, span), 0) % Q_BLOCK
    kpos = start + lax.broadcasted_iota(jnp.int32, (A_HEADS * Q_BLOCK, span), 1)
    mask_a = jnp.abs(qpos - kpos) <= WINDOW
    new = dict(ka=ka_ref[pl.ds(start, span), :], va=va_ref[pl.ds(start, span), :], mask_a=mask_a,
               kd=kd_ref[...], vd=vd_ref[...], kmn=kmn_ref[...], vm=vm_ref[...], kpe=kpe_ref[...])
    _three_mixers(scal_ref, subln_ref, (qa_ref[...], qd_ref[...], qmn_ref[...], qmr_ref[...]),
                  [cached, new], oa_o, od_o, om_o, lam_init, stack=True)


_SMEM_SPEC = pl.BlockSpec(memory_space=pltpu.SMEM)


def _attention_ctx(scal, subln, proj, states, layer, lam_init):
    qa, qd, qmn, qmr, kmn, vm, kpe4 = proj[:7]
    s_ka, s_va, s_kd, s_vd = states[:4]
    t = CTX_SEQ

    def blk(a):
        return pl.BlockSpec((t, a.shape[1]), lambda b: (b, 0))

    def sblk(a):
        return pl.BlockSpec((None, None, t, a.shape[-1]), lambda b: (b, layer, 0, 0))

    return pl.pallas_call(
        functools.partial(_attn_ctx_kernel, lam_init=lam_init),
        out_shape=[jax.ShapeDtypeStruct((N_TOK, 512), BF16)] * 3,
        grid=(N_CTX_BATCH,),
        in_specs=[_SMEM_SPEC, _layer_spec(subln, layer, 1)]
                 + [blk(a) for a in (qa, qd, qmn, qmr)]
                 + [sblk(a) for a in (s_ka, s_va, s_kd, s_vd)]
                 + [blk(a) for a in (kmn, vm, kpe4)],
        out_specs=[pl.BlockSpec((t, 512), lambda b: (b, 0))] * 3,
        compiler_params=pltpu.CompilerParams(vmem_limit_bytes=VMEM_LIMIT),
        name="attention_ctx",
    )(scal, subln, qa, qd, qmn, qmr, s_ka, s_va, s_kd, s_vd, kmn, vm, kpe4)


def _attention_lat(scal, subln, wukv, proj, caches, ctx_out, layer, lam_init):
    qa, qd, qmn, qmr, kmn, vm, kpe4, ka_l, va_l, kd_l, vd_l = proj[:11]
    nq = LAT_SEQ // Q_BLOCK
    q_off = N_CTX // Q_BLOCK
    kv_off = N_CTX // LAT_SEQ

    def qblk(a):
        return pl.BlockSpec((Q_BLOCK, a.shape[1]), lambda b, n: (q_off + b * nq + n, 0))

    def kvblk(a):
        return pl.BlockSpec((LAT_SEQ, a.shape[1]), lambda b, n: (b, 0))

    def kvblk_all(a):
        return pl.BlockSpec((LAT_SEQ, a.shape[1]), lambda b, n: (kv_off + b, 0))

    def cblk(a):
        return pl.BlockSpec((None, None, PAST, a.shape[-1]), lambda b, n: (b, layer, 0, 0))

    qs = [qa, qd, qmn, qmr]
    n_in = 3 + 4 + 7 + 6
    out_spec = pl.BlockSpec((Q_BLOCK, 512), lambda b, n: (q_off + b * nq + n, 0))
    return pl.pallas_call(
        functools.partial(_attn_lat_kernel, lam_init=lam_init),
        out_shape=[jax.ShapeDtypeStruct((N_TOK, 512), BF16)] * 3,
        grid=(N_LAT_BATCH, nq),
        in_specs=[_SMEM_SPEC, _layer_spec(subln, layer, 2), _layer_spec(wukv, layer, 2)]
                 + [qblk(a) for a in qs]
                 + [kvblk(a) for a in (ka_l, va_l, kd_l, vd_l)]
                 + [kvblk_all(a) for a in (kmn, vm, kpe4)]
                 + [cblk(a) for a in caches]
                 + [pl.BlockSpec(memory_space=pl.ANY)] * 3,
        out_specs=[out_spec] * 3,
        input_output_aliases={n_in + k: k for k in range(3)},
        compiler_params=pltpu.CompilerParams(vmem_limit_bytes=VMEM_LIMIT),
        name="attention_lat",
    )(scal, subln, wukv, *qs, ka_l, va_l, kd_l, vd_l, kmn, vm, kpe4, *caches, *ctx_out)


def _route(scores, biased):
    per = N_EXPERTS // N_GROUPS
    tm = scores.shape[1]
    sub = lax.broadcasted_iota(jnp.int32, (per, tm), 0).astype(F32)
    groups = [biased[g * per:(g + 1) * per, :] for g in range(N_GROUPS)]
    gscore = []
    for v in groups:
        m1 = jnp.max(v, axis=0, keepdims=True)
        first = jnp.min(jnp.where(v == m1, sub, float(per)), axis=0, keepdims=True)
        m2 = jnp.max(jnp.where(sub == first, -jnp.inf, v), axis=0, keepdims=True)
        gscore.append(m1 + m2)
    vals = []
    for g in range(N_GROUPS):
        rank = jnp.zeros((1, tm), F32)
        for o in range(N_GROUPS):
            if o == g:
                continue
            ahead = (gscore[o] >= gscore[g]) if o < g else (gscore[o] > gscore[g])
            rank = rank + jnp.where(ahead, 1.0, 0.0)
        vals.append(jnp.where(rank < TOPK_GROUPS, groups[g], NEG))
    idx = [sub + float(g * per) for g in range(N_GROUPS)]
    picks, weights = [], []
    for _ in range(TOP_K):
        best = functools.reduce(jnp.maximum, [jnp.max(v, axis=0, keepdims=True) for v in vals])
        cand = functools.reduce(jnp.minimum, [
            jnp.min(jnp.where(v == best, i, float(N_EXPERTS)), axis=0, keepdims=True)
            for v, i in zip(vals, idx)])
        wsel = jnp.zeros((1, tm), F32)
        for g in range(N_GROUPS):
            hit = idx[g] == cand
            wsel = wsel + jnp.sum(jnp.where(hit, scores[g * per:(g + 1) * per, :], 0.0),
                                  axis=0, keepdims=True)
            vals[g] = jnp.where(hit, -jnp.inf, vals[g])
        picks.append(cand)
        weights.append(wsel)
    total = functools.reduce(lambda a, b: a + b, weights)
    pad = [jnp.zeros((1, tm), F32)] * (8 - TOP_K)
    experts = jnp.concatenate(picks + pad, axis=0).astype(jnp.int32)
    gates = jnp.concatenate([x / total * ROUTED_SCALE for x in weights] + pad, axis=0)
    return experts, gates


def _post_kernel(*refs, n_x):
    (mod_ref, oa_ref, od_ref, om_ref, gpre_ref, wgl_ref, wa_ref, wb_ref, wc_ref, wo_ref, gpost_ref,
     gffn_ref, rwt_ref, rb_ref, xmid_o, h2_o, h2t_o, experts_o, gates_o) = refs[n_x:]

    @pl.when(pl.program_id(0) < N_TILES)
    def _():
        x = _load_x(refs[:n_x])
        m = mod_ref[0]
        d = D_MODEL
        sh1, sc1, g1, sh2, sc2 = (m[:, 0:d], m[:, d:2 * d], m[:, 2 * d:3 * d], m[:, 3 * d:4 * d],
                                  m[:, 4 * d:5 * d])
        h = _rms(x, gpre_ref[...]) * (1.0 + sc1) + sh1
        gate = jax.nn.sigmoid(_dot(h.astype(BF16), wgl_ref[...]))
        merged = (gate[:, 0:d] * _dot(oa_ref[...], wa_ref[...])
                  + gate[:, d:2 * d] * _dot(od_ref[...], wb_ref[...])
                  + gate[:, 2 * d:3 * d] * _dot(om_ref[...], wc_ref[...]))
        a = _dot(merged.astype(BF16), wo_ref[...])
        xm = x + g1 * _rms(a, gpost_ref[...])
        xmid_o[...] = xm
        h2f = _rms(xm, gffn_ref[...]) * (1.0 + sc2) + sh2
        h2 = h2f.astype(BF16)
        h2_o[...] = h2
        for c in range(ACC_ROWS):
            h2t_o[pl.ds(c, x.shape[0], stride=ACC_ROWS), :] = h2f[:, c * LANES:(c + 1) * LANES]
        scores = jax.nn.sigmoid(_dot_nt(rwt_ref[...], h2))
        experts, gates = _route(scores, scores + rb_ref[...])
        experts_o[...] = experts
        gates_o[...] = gates

    @pl.when(pl.program_id(0) == N_TILES)
    def _():
        h2t_o[...] = jnp.zeros_like(h2t_o)


def _post_attention(x, mod, oa, od, om, wts, layer):
    tm = ROW_TILE

    def row(i):
        return (_tile(i), 0)

    weights = [wts[k] for k in ("g_pre", "wgl", "wa", "wb", "wc", "wo", "g_post", "g_ffn", "rwt", "rb")]
    xs = list(x) if isinstance(x, tuple) else [x]
    return pl.pallas_call(
        functools.partial(_post_kernel, n_x=len(xs)),
        out_shape=[jax.ShapeDtypeStruct((N_TOK, D_MODEL), F32),
                   jax.ShapeDtypeStruct((N_TOK, D_MODEL), BF16),
                   jax.ShapeDtypeStruct(((N_TOK + tm) * ACC_ROWS, LANES), F32),
                   jax.ShapeDtypeStruct((8, N_TOK), jnp.int32),
                   jax.ShapeDtypeStruct((8, N_TOK), F32)],
        grid=(N_TILES + 1,),
        in_specs=_x_specs(x) + [_mod_spec(layer),
                  pl.BlockSpec((tm, 512), row), pl.BlockSpec((tm, 512), row),
                  pl.BlockSpec((tm, 512), row)]
                 + [_layer_spec(w, layer, 1) for w in weights],
        out_specs=[pl.BlockSpec((tm, D_MODEL), row), pl.BlockSpec((tm, D_MODEL), row),
                   pl.BlockSpec((tm * ACC_ROWS, LANES), lambda i: (i, 0)),
                   pl.BlockSpec((8, tm), lambda i: (0, _tile(i))),
                   pl.BlockSpec((8, tm), lambda i: (0, _tile(i)))],
        compiler_params=pltpu.CompilerParams(
            dimension_semantics=("arbitrary",), vmem_limit_bytes=VMEM_LIMIT),
        name="post_attention",
    )(*xs, mod, oa, od, om, *weights)


def _ffn(x, wg, wu):
    return _silu(_dot(x, wg)) * _dot(x, wu)


def _moe_kernel(start_ref, count_ref, dst_ref, gate_ref, xs_ref, wg_ref, wu_ref, wd_ref,
                acc_ref, xa, xb, ya, yb, wgb, wub, wdb, pend):
    e = pl.program_id(0)

    @pl.when(e == 0)
    def _():
        def clear(i, carry):
            r0 = pl.multiple_of(i * MOE_ZERO_ROWS, MOE_ZERO_ROWS)
            acc_ref[pl.ds(r0, MOE_ZERO_ROWS), :] = jnp.zeros((MOE_ZERO_ROWS, LANES), F32)
            return carry

        lax.fori_loop(0, acc_ref.shape[0] // MOE_ZERO_ROWS, clear, 0)

        yb[...] = jnp.zeros_like(yb)
        pend[0] = MOE_LIST_PAD_BASE

    wgb[...] = wg_ref[0].astype(BF16)
    wub[...] = wu_ref[0].astype(BF16)
    wdb[...] = wd_ref[0].astype(BF16)
    start = start_ref[e]
    pairs = (count_ref[e] + 2 * MOE_SUB - 1) // (2 * MOE_SUB)

    def gather(base, xbuf):
        for mi in range(MOE_SUB):
            src = pl.multiple_of(dst_ref[base + mi], ACC_ROWS)
            xbuf[mi * ACC_ROWS:(mi + 1) * ACC_ROWS, :] = xs_ref[pl.ds(src, ACC_ROWS), :]

    def expert_ffn(xbuf, ybuf):
        x = jnp.concatenate([xbuf[pl.ds(c, MOE_SUB, stride=ACC_ROWS), :].astype(BF16)
                             for c in range(ACC_ROWS)], axis=1)
        hid = _ffn(x, wgb[...], wub[...])
        y = _dot(hid.astype(BF16), wdb[...])
        for c in range(ACC_ROWS):
            ybuf[pl.ds(c, MOE_SUB, stride=ACC_ROWS), :] = y[:, c * LANES:(c + 1) * LANES]

    def scatter(base, ybuf):
        for g0 in range(0, MOE_SUB, MOE_RMW_GROUP):
            updates = []
            for mi in range(g0, g0 + MOE_RMW_GROUP):
                dst = pl.multiple_of(dst_ref[base + mi], ACC_ROWS)
                yv = ybuf[mi * ACC_ROWS:(mi + 1) * ACC_ROWS, :]
                updates.append((dst, acc_ref[pl.ds(dst, ACC_ROWS), :] + gate_ref[base + mi] * yv))
            for dst, val in updates:
                acc_ref[pl.ds(dst, ACC_ROWS), :] = val

    start_next = start_ref[jnp.minimum(e + 1, N_EXPERTS - 1)]
    prev_count = count_ref[jnp.maximum(e - 1, 0)]

    @pl.when((pairs > 0) & ((e == 0) | (prev_count == 0)))
    def _():
        gather(start, xa)
        gather(start + MOE_SUB, xb)

    def pair(p, carry):
        base = start + p * (2 * MOE_SUB)
        ahead = jnp.where(p + 1 < pairs, base + 2 * MOE_SUB, start_next)
        expert_ffn(xa, ya)
        gather(ahead, xa)
        scatter(pend[0], yb)
        expert_ffn(xb, yb)
        gather(ahead + MOE_SUB, xb)
        scatter(base, ya)
        pend[0] = base + MOE_SUB
        return carry

    lax.fori_loop(0, pairs, pair, 0)

    @pl.when(e == N_EXPERTS - 1)
    def _():
        scatter(pend[0], yb)


def _moe(lists, h2t, wg, wu, wd, layer):
    def expert(e, *_):
        return (layer, e, 0, 0)

    def whole(e, *_):
        return (0, 0)

    acc_rows = (N_TOK + MOE_SPARE_TOKENS) * ACC_ROWS
    assert acc_rows % MOE_ZERO_ROWS == 0
    return pl.pallas_call(
        _moe_kernel,
        out_shape=jax.ShapeDtypeStruct((acc_rows, LANES), F32),
        grid_spec=pltpu.PrefetchScalarGridSpec(
            num_scalar_prefetch=len(lists),
            grid=(N_EXPERTS,),
            in_specs=[pl.BlockSpec((acc_rows, LANES), whole, pipeline_mode=pl.Buffered(1)),
                      pl.BlockSpec((None, 1, D_MODEL, F_EXPERT), expert),
                      pl.BlockSpec((None, 1, D_MODEL, F_EXPERT), expert),
                      pl.BlockSpec((None, 1, F_EXPERT, D_MODEL), expert)],
            out_specs=pl.BlockSpec((acc_rows, LANES), whole, pipeline_mode=pl.Buffered(1)),
            scratch_shapes=[pltpu.VMEM((MOE_SUB * ACC_ROWS, LANES), F32)] * 4
                           + [pltpu.VMEM((D_MODEL, F_EXPERT), BF16),
                              pltpu.VMEM((D_MODEL, F_EXPERT), BF16),
                              pltpu.VMEM((F_EXPERT, D_MODEL), BF16),
                              pltpu.SMEM((1,), jnp.int32)]),
        compiler_params=pltpu.CompilerParams(
            dimension_semantics=("arbitrary",), vmem_limit_bytes=MOE_VMEM_LIMIT),
        name="moe",
    )(*lists, h2t, wg, wu, wd)


def _final_kernel(acc_ref, h2_ref, xmid_ref, mod_ref, swg_ref, swu_ref, swd_ref, gpost_ref, *out_refs):
    tm = h2_ref.shape[0]
    routed = jnp.concatenate(
        [acc_ref[pl.ds(c, tm, stride=ACC_ROWS), :] for c in range(ACC_ROWS)], axis=1)
    hid = _ffn(h2_ref[...], swg_ref[...], swu_ref[...])
    shared = _dot(hid.astype(BF16), swd_ref[...])
    g2 = mod_ref[0][:, 5 * D_MODEL:6 * D_MODEL]
    out = xmid_ref[...] + g2 * _rms(routed + shared, gpost_ref[...])
    if len(out_refs) == 1:
        out_refs[0][...] = out
    else:
        @pl.when(pl.program_id(0) < CTX_TILES)
        def _():
            out_refs[0][...] = out

        @pl.when(pl.program_id(0) >= CTX_TILES)
        def _():
            out_refs[1][...] = out


def _final(acc, h2, xmid, mod, wts, layer, split):
    tm = ROW_TILE

    def row(i):
        return (i, 0)

    weights = [wts[k] for k in ("swg", "swu", "swd", "g_ffn_post")]
    return pl.pallas_call(
        _final_kernel,
        out_shape=([jax.ShapeDtypeStruct((N_CTX, D_MODEL), F32),
                    jax.ShapeDtypeStruct((N_LAT, D_MODEL), F32)] if split
                   else jax.ShapeDtypeStruct((N_TOK, D_MODEL), F32)),
        grid=(N_TOK // tm,),
        in_specs=[pl.BlockSpec((tm * ACC_ROWS, LANES), row),
                  pl.BlockSpec((tm, D_MODEL), row),
                  pl.BlockSpec((tm, D_MODEL), row),
                  _mod_spec(layer)]
                 + [_layer_spec(w, layer, 1) for w in weights],
        out_specs=_x_specs((None, None)) if split else pl.BlockSpec((tm, D_MODEL), row),
        compiler_params=pltpu.CompilerParams(
            dimension_semantics=("arbitrary",), vmem_limit_bytes=VMEM_LIMIT),
        name="ffn_final",
    )(acc, h2, xmid, mod, *weights)


def _routing_lists(experts, gates):
    e = experts[:TOP_K].reshape(-1)
    tok = jnp.tile(jnp.arange(N_TOK, dtype=jnp.int32), TOP_K)
    pad_e = jnp.repeat(jnp.arange(N_EXPERTS, dtype=jnp.int32), MOE_PAD)
    pad_tok = N_TOK + jnp.tile(jnp.arange(MOE_PAD, dtype=jnp.int32), N_EXPERTS)
    keys = jnp.concatenate([e * TOK_KEY + tok, pad_e * TOK_KEY + pad_tok])
    vals = jnp.concatenate([gates[:TOP_K].reshape(-1), jnp.zeros((N_EXPERTS * MOE_PAD,), F32)])
    keys, gate_sorted = lax.sort((keys, vals), num_keys=1)
    tail = jnp.full((MOE_SUB,), N_TOK, jnp.int32)
    tok_sorted = jnp.concatenate([keys & (TOK_KEY - 1), tail])
    gate_sorted = jnp.concatenate([gate_sorted, jnp.zeros((MOE_SUB,), F32)])
    dst = jnp.minimum(tok_sorted, N_TOK) * ACC_ROWS
    counts = jnp.sum((e[None, :] == jnp.arange(N_EXPERTS, dtype=jnp.int32)[:, None]).astype(jnp.int32),
                     axis=1)
    start = jnp.cumsum(counts) - counts + MOE_PAD * jnp.arange(N_EXPERTS, dtype=jnp.int32)
    return start.astype(jnp.int32), counts, dst, gate_sorted


def _rope_tables():
    t = np.arange(LAT_SEQ)
    pos = np.stack([t // GRID_W, t % GRID_W], axis=1).astype(np.float64)

    def table(half):
        inv = ROPE_BASE ** (-np.arange(half, dtype=np.float64) / half)
        lane = np.arange(LANES)
        axis = (lane // (2 * half)) % 2
        freq = inv[lane % half]
        ang = pos[:, axis] * freq[None, :]
        sign = np.where((lane % (2 * half)) < half, -1.0, 1.0)
        cos = np.concatenate([np.ones((ROW_TILE, LANES)), np.cos(ang)], axis=0)
        sin = np.concatenate([np.zeros((ROW_TILE, LANES)), np.sin(ang) * sign[None, :]], axis=0)
        return jnp.asarray(cos, F32), jnp.asarray(sin, F32)

    cos64, sin64 = table(16)
    cos32, sin32 = table(8)
    return cos64, sin64, cos32, sin32


def _prepare_weights(norm_attn_pre, norm_attn_post, norm_ffn_pre, norm_ffn_post, w_in, dif_subln,
                     mla_q_norm, mla_w_uq, mla_kv_norm, mla_w_ukv, w_branch_a, w_branch_b,
                     w_branch_c, w_out, router_w, router_bias, shared_w_gate, shared_w_up,
                     shared_w_down):
    dp = DEPTH
    qa = w_in[:, :, 0:512].reshape(dp, D_MODEL, 2, 4, 64).transpose(0, 1, 3, 2, 4).reshape(dp, D_MODEL, 512)
    kpe = w_in[:, :, 2688:2720]
    w1 = jnp.concatenate([qa, w_in[:, :, 512:2688], kpe, kpe, kpe, kpe], axis=2).astype(BF16)
    uq = mla_w_uq.reshape(dp, 256, 8, 96)
    wuq = jnp.concatenate([uq[..., :64].reshape(dp, 256, 512), uq[..., 64:].reshape(dp, 256, 256)],
                          axis=2).astype(BF16)
    ukv = mla_w_ukv.reshape(dp, 128, 8, 128)
    wukv = jnp.concatenate([ukv[..., :64].reshape(dp, 128, 512), ukv[..., 64:].reshape(dp, 128, 512)],
                           axis=2).astype(BF16)
    wa = w_branch_a.reshape(dp, 2, 4, 64, D_MODEL).transpose(0, 2, 1, 3, 4).reshape(dp, 512, D_MODEL)
    return dict(
        g_pre=norm_attn_pre[:, None, :], g_post=norm_attn_post[:, None, :],
        g_ffn=norm_ffn_pre[:, None, :], g_ffn_post=norm_ffn_post[:, None, :],
        w1=w1, wgl=w_in[:, :, 2720:].astype(BF16),
        qnorm=mla_q_norm[:, None, :], wuq=wuq, kvnorm=mla_kv_norm[:, None, :], wukv=wukv,
        subln=dif_subln[:, None, :],
        wa=wa.astype(BF16), wb=w_branch_b.astype(BF16), wc=w_branch_c.astype(BF16),
        wo=w_out.astype(BF16), rwt=router_w.transpose(0, 2, 1).astype(BF16),
        rb=router_bias[:, :, None],
        swg=shared_w_gate.astype(BF16), swu=shared_w_up.astype(BF16), swd=shared_w_down.astype(BF16))


def kernel(x_prompt, x_sample, cache_swa_k, cache_swa_v, cache_dif_k, cache_dif_v, cache_mla_ckv, cache_mla_kpe, c, c_ctx, w_mod, b_mod, norm_attn_pre, norm_attn_post, norm_ffn_pre, norm_ffn_post, w_in, swa_sink, dif_lq1, dif_lk1, dif_lq2, dif_lk2, dif_subln, mla_q_norm, mla_w_uq, mla_kv_norm, mla_w_ukv, w_branch_a, w_branch_b, w_branch_c, w_out, router_w, router_bias, moe_w_gate, moe_w_up, moe_w_down, shared_w_gate, shared_w_up, shared_w_down):
    x = (x_prompt.reshape(N_CTX, D_MODEL), x_sample.reshape(N_LAT, D_MODEL))
    cvec = jnp.concatenate([c_ctx[None, :], c, jnp.zeros((8 - 1 - N_LAT_BATCH, D_MODEL), F32)], axis=0)
    mod = _modulation(cvec, w_mod, b_mod).reshape(DEPTH, 8, 1, 6 * D_MODEL)
    tables = _rope_tables()
    wts = _prepare_weights(norm_attn_pre, norm_attn_post, norm_ffn_pre, norm_ffn_post, w_in,
                           dif_subln, mla_q_norm, mla_w_uq, mla_kv_norm, mla_w_ukv, w_branch_a,
                           w_branch_b, w_branch_c, w_out, router_w, router_bias, shared_w_gate,
                           shared_w_up, shared_w_down)
    caches = [cache_swa_k.reshape(N_LAT_BATCH, DEPTH, PAST, 128),
              cache_swa_v.reshape(N_LAT_BATCH, DEPTH, PAST, 128),
              cache_dif_k.reshape(N_LAT_BATCH, DEPTH, PAST, 512),
              cache_dif_v.reshape(N_LAT_BATCH, DEPTH, PAST, 512),
              cache_mla_ckv,
              jnp.tile(cache_mla_kpe, (1, 1, 1, 4))]
    lam_init = [0.8 - 0.6 * math.exp(-0.3 * l) for l in range(DEPTH)]
    lam = (jnp.exp(jnp.sum(dif_lq1 * dif_lk1, axis=1)) - jnp.exp(jnp.sum(dif_lq2 * dif_lk2, axis=1))
           + jnp.asarray(lam_init, F32))
    scal = jnp.concatenate([swa_sink, lam[:, None]], axis=1).astype(F32)

    states = None
    for l in range(DEPTH):
        proj = _in_projection(x, mod, wts, tables, l, states)
        states = proj[11:]
        ctx_out = _attention_ctx(scal[l], wts["subln"], proj, states, l, lam_init[l])
        oa, od, om = _attention_lat(scal[l], wts["subln"], wts["wukv"], proj, caches, ctx_out, l,
                                    lam_init[l])
        xmid, h2, h2t, experts, gates = _post_attention(x, mod, oa, od, om, wts, l)
        lists = _routing_lists(experts, gates)
        acc = _moe(lists, h2t, moe_w_gate, moe_w_up, moe_w_down, l)
        x = _final(acc, h2, xmid, mod, wts, l, split=(l == DEPTH - 1))

    y_p = x[0].reshape(N_CTX_BATCH, CTX_SEQ, D_MODEL)
    y_s = x[1].reshape(N_LAT_BATCH, LAT_SEQ, D_MODEL)
    s_ka, s_va, s_kd, s_vd, s_ckv, s_kpe = states
    return (y_p, y_s,
            s_ka.reshape(N_CTX_BATCH, DEPTH, CTX_SEQ, 2, 64),
            s_va.reshape(N_CTX_BATCH, DEPTH, CTX_SEQ, 2, 64),
            s_kd.reshape(N_CTX_BATCH, DEPTH, CTX_SEQ, 4, 2, 64),
            s_vd.reshape(N_CTX_BATCH, DEPTH, CTX_SEQ, 4, 128),
            s_ckv, s_kpe)
```

```python
import functools
import math

import numpy as np
import jax
import jax.numpy as jnp
from jax import lax
from jax.experimental import pallas as pl
from jax.experimental.pallas import tpu as pltpu

F32 = jnp.float32
BF16 = jnp.bfloat16

D_MODEL = 1024
N_CTX_BATCH, CTX_SEQ = 16, 256
N_LAT_BATCH, LAT_SEQ = 2, 1024
PAST = 256
N_CTX = N_CTX_BATCH * CTX_SEQ
N_LAT = N_LAT_BATCH * LAT_SEQ
N_TOK = N_CTX + N_LAT
DEPTH = 2
GRID_W = 64
WINDOW = 128
N_EXPERTS = 64
N_GROUPS = 8
TOPK_GROUPS = 4
TOP_K = 6
F_EXPERT = 256
ROUTED_SCALE = 2.5
ROPE_BASE = 10000.0
EPS = 1e-6
NEG = -1e30

LANES = 128
ROW_TILE = 256
Q_BLOCK = 256
VMEM_LIMIT = 56 * 1024 * 1024
ACC_ROWS = 8
MOE_SPARE_TOKENS = 8
MOE_ZERO_ROWS = 64
MOE_VMEM_LIMIT = 62 * 1024 * 1024
MOE_SUB = 128
MOE_RMW_GROUP = 32
MOE_PAD = 2 * MOE_SUB
TOK_KEY = 8192
MOE_LIST_PAD_BASE = N_TOK * TOP_K + N_EXPERTS * MOE_PAD

C_QA, C_KA, C_VA, C_QD, C_KD, C_VD, C_QC, C_KVC, C_KPE, C_END = (
    0, 512, 640, 768, 1280, 1792, 2304, 2560, 2688, 2816)
SCALE_64 = 1.0 / math.sqrt(64.0)
SCALE_96 = 1.0 / math.sqrt(96.0)

CTX_TILES = N_CTX // ROW_TILE
N_TILES = N_TOK // ROW_TILE
assert ROW_TILE == CTX_SEQ and LAT_SEQ % ROW_TILE == 0


def _mod_row_of_tile(i):
    return jnp.where(i < CTX_TILES, 0, 1 + (i - CTX_TILES) // (LAT_SEQ // ROW_TILE))


def _rms(x, g):
    return x * lax.rsqrt(jnp.mean(x * x, axis=-1, keepdims=True) + EPS) * g


def _dot(a, b):
    return jnp.dot(a, b, preferred_element_type=F32)


def _dot_nt(a, b):
    return lax.dot_general(a, b, (((1,), (1,)), ((), ())), preferred_element_type=F32)


def _silu(x):
    return x * jax.nn.sigmoid(x)


def _layer_spec(arr, layer, grid_rank):
    zeros = (0,) * (arr.ndim - 1)
    return pl.BlockSpec((None,) + arr.shape[1:], lambda *_: (layer,) + zeros)


def _mod_kernel(c_ref, w_ref, b_ref, o_ref):
    c = c_ref[...]
    o_ref[0] = _dot(_silu(c).astype(BF16), w_ref[0].astype(BF16)) + b_ref[0]


def _modulation(cvec, w_mod, b_mod):
    tn = 1536
    n = w_mod.shape[-1]
    return pl.pallas_call(
        _mod_kernel,
        out_shape=jax.ShapeDtypeStruct((DEPTH, 8, n), F32),
        grid=(DEPTH, n // tn),
        in_specs=[
            pl.BlockSpec((8, D_MODEL), lambda l, j: (0, 0)),
            pl.BlockSpec((1, D_MODEL, tn), lambda l, j: (l, 0, j)),
            pl.BlockSpec((1, 1, tn), lambda l, j: (l, 0, j)),
        ],
        out_specs=pl.BlockSpec((1, 8, tn), lambda l, j: (l, 0, j)),
        compiler_params=pltpu.CompilerParams(vmem_limit_bytes=VMEM_LIMIT),
        name="modulation",
    )(cvec, w_mod, b_mod.reshape(DEPTH, 1, n))


def _tile(i):
    return jnp.minimum(i, N_TILES - 1)


def _x_specs(x):
    if isinstance(x, tuple):
        return [pl.BlockSpec((ROW_TILE, D_MODEL), lambda i: (jnp.minimum(i, CTX_TILES - 1), 0)),
                pl.BlockSpec((ROW_TILE, D_MODEL), lambda i: (jnp.maximum(_tile(i) - CTX_TILES, 0), 0))]
    return [pl.BlockSpec((ROW_TILE, D_MODEL), lambda i: (_tile(i), 0))]


def _load_x(x_refs):
    if len(x_refs) == 1:
        return x_refs[0][...]
    return jnp.where(pl.program_id(0) < CTX_TILES, x_refs[0][...], x_refs[1][...])


def _mod_spec(layer):
    return pl.BlockSpec((None, 1, 1, 6 * D_MODEL),
                        lambda i: (layer, _mod_row_of_tile(_tile(i)), 0, 0))


def _rope128(x, cos, sin, half):
    lane = lax.broadcasted_iota(jnp.int32, x.shape, 1)
    first = (lane % (2 * half)) < half
    partner = jnp.where(first, pltpu.roll(x, LANES - half, 1), pltpu.roll(x, half, 1))
    return x * cos + partner * sin


def _rope_cols(x, cos, sin, half):
    chunks = [_rope128(x[:, c:c + LANES], cos, sin, half) for c in range(0, x.shape[1], LANES)]
    return chunks[0] if len(chunks) == 1 else jnp.concatenate(chunks, axis=1)


N_IN_INPUTS = 11


def _in_kernel(*refs, n_x):
    (mod_ref, g_ref, w1_ref, qn_ref, wuq_ref, kvn_ref, wukv_ref,
     cos64_ref, sin64_ref, cos32_ref, sin32_ref) = refs[n_x:n_x + N_IN_INPUTS]
    (qa_o, qd_o, qmn_o, qmr_o, kmn_o, vm_o, kpe4_o, ka_l, va_l, kd_l, vd_l,
     s_ka, s_va, s_kd, s_vd, s_ckv, s_kpe) = refs[-17:]
    i = pl.program_id(0)
    x = _load_x(refs[:n_x])
    m = mod_ref[0]
    sh1, sc1 = m[:, 0:D_MODEL], m[:, D_MODEL:2 * D_MODEL]
    h = _rms(x, g_ref[...]) * (1.0 + sc1) + sh1
    z = _dot(h.astype(BF16), w1_ref[...])
    cos64, sin64 = cos64_ref[...], sin64_ref[...]
    cos32, sin32 = cos32_ref[...], sin32_ref[...]
    qa_o[...] = (_rope_cols(z[:, C_QA:C_KA], cos64, sin64, 16) * SCALE_64).astype(BF16)
    qd_o[...] = (_rope_cols(z[:, C_QD:C_KD], cos64, sin64, 16) * SCALE_64).astype(BF16)
    ka = _rope_cols(z[:, C_KA:C_VA], cos64, sin64, 16)
    va = z[:, C_VA:C_QD]
    kd = _rope_cols(z[:, C_KD:C_VD], cos64, sin64, 16)
    vd = z[:, C_VD:C_QC]
    kpe4 = _rope_cols(z[:, C_KPE:C_END], cos32, sin32, 8)
    kpe4_o[...] = kpe4.astype(BF16)
    qn = _rms(z[:, C_QC:C_KVC], qn_ref[...])
    qm = _dot(qn.astype(BF16), wuq_ref[...]) * SCALE_96
    qmn_o[...] = qm[:, 0:512].astype(BF16)
    qmr_o[...] = _rope_cols(qm[:, 512:768], cos32, sin32, 8).astype(BF16)
    ckv = _rms(z[:, C_KVC:C_KPE], kvn_ref[...])
    kv = _dot(ckv.astype(BF16), wukv_ref[...])
    kmn_o[...] = kv[:, 0:512].astype(BF16)
    vm_o[...] = kv[:, 512:1024].astype(BF16)

    @pl.when(i < CTX_TILES)
    def _():
        s_ka[...] = ka
        s_va[...] = va
        s_kd[...] = kd
        s_vd[...] = vd
        s_ckv[...] = ckv
        s_kpe[...] = kpe4[:, 0:32]

    @pl.when(i >= CTX_TILES)
    def _():
        ka_l[...] = ka.astype(BF16)
        va_l[...] = va.astype(BF16)
        kd_l[...] = kd.astype(BF16)
        vd_l[...] = vd.astype(BF16)


STATE_WIDTHS = (128, 128, 512, 512, 128, 32)


def _in_projection(x, mod, wts, tables, layer, prev_states):
    tm = ROW_TILE

    def row(i):
        return (i, 0)

    def table_row(i):
        return (jnp.where(i < CTX_TILES, 0, 1 + (i - CTX_TILES) % (LAT_SEQ // tm)), 0)

    def lat_row(i):
        return (jnp.maximum(i - CTX_TILES, 0), 0)

    def state_block(i):
        return (jnp.minimum(i, CTX_TILES - 1), layer, 0, 0)

    table_spec = pl.BlockSpec((tm, LANES), table_row)
    weights = [wts["g_pre"], wts["w1"], wts["qnorm"], wts["wuq"], wts["kvnorm"], wts["wukv"]]
    all_rows = [(512, BF16)] * 3 + [(256, BF16), (512, BF16), (512, BF16), (128, BF16)]
    lat_rows = [(128, BF16), (128, BF16), (512, BF16), (512, BF16)]
    aliased = [] if prev_states is None else list(prev_states)
    xs = list(x) if isinstance(x, tuple) else [x]
    n_in = len(xs) + N_IN_INPUTS
    n_plain_out = len(all_rows) + len(lat_rows)
    return pl.pallas_call(
        functools.partial(_in_kernel, n_x=len(xs)),
        out_shape=[jax.ShapeDtypeStruct((N_TOK, w), dt) for w, dt in all_rows]
                  + [jax.ShapeDtypeStruct((N_LAT, w), dt) for w, dt in lat_rows]
                  + [jax.ShapeDtypeStruct((N_CTX_BATCH, DEPTH, CTX_SEQ, w), F32) for w in STATE_WIDTHS],
        grid=(N_TOK // tm,),
        in_specs=_x_specs(x) + [_mod_spec(layer)]
                 + [_layer_spec(w, layer, 1) for w in weights]
                 + [table_spec] * 4
                 + [pl.BlockSpec(memory_space=pl.ANY)] * len(aliased),
        out_specs=[pl.BlockSpec((tm, w), row) for w, _ in all_rows]
                  + [pl.BlockSpec((tm, w), lat_row) for w, _ in lat_rows]
                  + [pl.BlockSpec((None, None, CTX_SEQ, w), state_block) for w in STATE_WIDTHS],
        input_output_aliases={n_in + k: n_plain_out + k for k in range(len(aliased))},
        compiler_params=pltpu.CompilerParams(
            dimension_semantics=("arbitrary",), vmem_limit_bytes=VMEM_LIMIT),
        name="in_projection",
    )(*xs, mod, *weights, *tables, *aliased)


def _attend(q, parts, sink=None):
    scores = []
    for k, _, mask in parts:
        s = _dot_nt(q, k)
        if mask is not None:
            s = jnp.where(mask, s, NEG)
        scores.append(s)
    m = functools.reduce(jnp.maximum, [jnp.max(s, axis=-1, keepdims=True) for s in scores])
    if sink is not None:
        m = jnp.maximum(m, sink)
    den = None
    out = None
    for s, (_, v, _) in zip(scores, parts):
        e = jnp.exp(s - m)
        d = jnp.sum(e, axis=-1, keepdims=True)
        o = _dot(e.astype(BF16), v)
        den = d if den is None else den + d
        out = o if out is None else out + o
    if sink is not None:
        den = den + jnp.exp(sink - m)
    return out / den


A_HEADS = 8


def _attend_heads(qs, parts, sinks, stack):
    rows = qs[0].shape[0]
    if not stack:
        return [_attend(q, parts, None if sinks is None else sinks[i]) for i, q in enumerate(qs)]
    sink = None
    if sinks is not None:
        sink = jnp.concatenate([jnp.full((rows, 1), s, F32) for s in sinks], axis=0)
    o = _attend(jnp.concatenate(qs, axis=0), parts, sink)
    return [o[i * rows:(i + 1) * rows] for i in range(len(qs))]


def _three_mixers(scal_ref, subln_ref, q_blocks, kv_parts, oa_o, od_o, om_o, lam_init, stack):
    qa, qd, qmn, qmr = q_blocks
    rows = qa.shape[0]
    lane = lax.broadcasted_iota(jnp.int32, (rows, LANES), 1)
    low = lane < 64
    zero = jnp.zeros((rows, LANES), BF16)

    parts = [(p["ka"], p["va"], p.get("mask_a")) for p in kv_parts]
    qs = [jnp.where(low if kvh == 0 else ~low, qa[:, g * LANES:(g + 1) * LANES], zero)
          for kvh in range(2) for g in range(4)]
    out_a = _attend_heads(qs, parts, [scal_ref[h] for h in range(A_HEADS)], stack)
    for g in range(4):
        oa_o[:, g * LANES:(g + 1) * LANES] = jnp.where(low, out_a[g], out_a[4 + g]).astype(BF16)

    lam = scal_ref[8]
    subln = subln_ref[...]
    for h in range(4):
        sl = slice(h * LANES, (h + 1) * LANES)
        q128 = qd[:, sl]
        o1, o2 = _attend_heads([jnp.where(low, q128, zero), jnp.where(low, zero, q128)],
                               [(p["kd"][:, sl], p["vd"][:, sl], None) for p in kv_parts],
                               None, stack)
        od = o1 - lam * o2
        od_o[:, sl] = (_rms(od, subln) * (1.0 - lam_init)).astype(BF16)

    for i in range(4):
        sl = slice(i * LANES, (i + 1) * LANES)
        qs = []
        for sub in range(2):
            h = 2 * i + sub
            qn = jnp.where(low if sub == 0 else ~low, qmn[:, sl], zero)
            qr128 = qmr[:, (h // 4) * LANES:(h // 4 + 1) * LANES]
            qr = jnp.where((lane // 32) == (h % 4), qr128, zero)
            qs.append(jnp.concatenate([qn, qr], axis=1))
        parts = [(jnp.concatenate([p["kmn"][:, sl], p["kpe"]], axis=1), p["vm"][:, sl], None)
                 for p in kv_parts]
        o_even, o_odd = _attend_heads(qs, parts, None, stack)
        om_o[:, sl] = jnp.where(low, o_even, o_odd).astype(BF16)


def _attn_ctx_kernel(scal_ref, subln_ref, qa_ref, qd_ref, qmn_ref, qmr_ref,
                     ka_ref, va_ref, kd_ref, vd_ref, kmn_ref, vm_ref, kpe_ref,
                     oa_o, od_o, om_o, *, lam_init):
    part = dict(ka=ka_ref[...].astype(BF16), va=va_ref[...].astype(BF16),
                kd=kd_ref[...].astype(BF16), vd=vd_ref[...].astype(BF16),
                kmn=kmn_ref[...], vm=vm_ref[...], kpe=kpe_ref[...])
    _three_mixers(scal_ref, subln_ref, (qa_ref[...], qd_ref[...], qmn_ref[...], qmr_ref[...]),
                  [part], oa_o, od_o, om_o, lam_init, stack=False)


def _attn_lat_kernel(scal_ref, subln_ref, wukv_ref, qa_ref, qd_ref, qmn_ref, qmr_ref,
                     ka_ref, va_ref, kd_ref, vd_ref, kmn_ref, vm_ref, kpe_ref,
                     cka_ref, cva_ref, ckd_ref, cvd_ref, cckv_ref, ckpe_ref,
                     oa_in, od_in, om_in, oa_o, od_o, om_o, *, lam_init):
    del oa_in, od_in, om_in
    n = pl.program_id(1)
    kv_c = _dot(cckv_ref[...].astype(BF16), wukv_ref[...])
    cached = dict(ka=cka_ref[...].astype(BF16), va=cva_ref[...].astype(BF16),
                  kd=ckd_ref[...].astype(BF16), vd=cvd_ref[...].astype(BF16),
                  kmn=kv_c[:, 0:512].astype(BF16), vm=kv_c[:, 512:1024].astype(BF16),
                  kpe=ckpe_ref[...].astype(BF16))
    span = Q_BLOCK + 2 * WINDOW
    start = pl.multiple_of(jnp.clip(n * Q_BLOCK - WINDOW, 0, LAT_SEQ - span), WINDOW)
    qpos = n * Q_BLOCK + lax.broadcasted_iota(jnp.int32, (A_HEADS * Q_BLOCK, span), 0) % Q_BLOCK
    kpos = start + lax.broadcasted_iota(jnp.int32, (A_HEADS * Q_BLOCK, span), 1)
    mask_a = jnp.abs(qpos - kpos) <= WINDOW
    new = dict(ka=ka_ref[pl.ds(start, span), :], va=va_ref[pl.ds(start, span), :], mask_a=mask_a,
               kd=kd_ref[...], vd=vd_ref[...], kmn=kmn_ref[...], vm=vm_ref[...], kpe=kpe_ref[...])
    _three_mixers(scal_ref, subln_ref, (qa_ref[...], qd_ref[...], qmn_ref[...], qmr_ref[...]),
                  [cached, new], oa_o, od_o, om_o, lam_init, stack=True)


_SMEM_SPEC = pl.BlockSpec(memory_space=pltpu.SMEM)


def _attention_ctx(scal, subln, proj, states, layer, lam_init):
    qa, qd, qmn, qmr, kmn, vm, kpe4 = proj[:7]
    s_ka, s_va, s_kd, s_vd = states[:4]
    t = CTX_SEQ

    def blk(a):
        return pl.BlockSpec((t, a.shape[1]), lambda b: (b, 0))

    def sblk(a):
        return pl.BlockSpec((None, None, t, a.shape[-1]), lambda b: (b, layer, 0, 0))

    return pl.pallas_call(
        functools.partial(_attn_ctx_kernel, lam_init=lam_init),
        out_shape=[jax.ShapeDtypeStruct((N_TOK, 512), BF16)] * 3,
        grid=(N_CTX_BATCH,),
        in_specs=[_SMEM_SPEC, _layer_spec(subln, layer, 1)]
                 + [blk(a) for a in (qa, qd, qmn, qmr)]
                 + [sblk(a) for a in (s_ka, s_va, s_kd, s_vd)]
                 + [blk(a) for a in (kmn, vm, kpe4)],
        out_specs=[pl.BlockSpec((t, 512), lambda b: (b, 0))] * 3,
        compiler_params=pltpu.CompilerParams(vmem_limit_bytes=VMEM_LIMIT),
        name="attention_ctx",
    )(scal, subln, qa, qd, qmn, qmr, s_ka, s_va, s_kd, s_vd, kmn, vm, kpe4)


def _attention_lat(scal, subln, wukv, proj, caches, ctx_out, layer, lam_init):
    qa, qd, qmn, qmr, kmn, vm, kpe4, ka_l, va_l, kd_l, vd_l = proj[:11]
    nq = LAT_SEQ // Q_BLOCK
    q_off = N_CTX // Q_BLOCK
    kv_off = N_CTX // LAT_SEQ

    def qblk(a):
        return pl.BlockSpec((Q_BLOCK, a.shape[1]), lambda b, n: (q_off + b * nq + n, 0))

    def kvblk(a):
        return pl.BlockSpec((LAT_SEQ, a.shape[1]), lambda b, n: (b, 0))

    def kvblk_all(a):
        return pl.BlockSpec((LAT_SEQ, a.shape[1]), lambda b, n: (kv_off + b, 0))

    def cblk(a):
        return pl.BlockSpec((None, None, PAST, a.shape[-1]), lambda b, n: (b, layer, 0, 0))

    qs = [qa, qd, qmn, qmr]
    n_in = 3 + 4 + 7 + 6
    out_spec = pl.BlockSpec((Q_BLOCK, 512), lambda b, n: (q_off + b * nq + n, 0))
    return pl.pallas_call(
        functools.partial(_attn_lat_kernel, lam_init=lam_init),
        out_shape=[jax.ShapeDtypeStruct((N_TOK, 512), BF16)] * 3,
        grid=(N_LAT_BATCH, nq),
        in_specs=[_SMEM_SPEC, _layer_spec(subln, layer, 2), _layer_spec(wukv, layer, 2)]
                 + [qblk(a) for a in qs]
                 + [kvblk(a) for a in (ka_l, va_l, kd_l, vd_l)]
                 + [kvblk_all(a) for a in (kmn, vm, kpe4)]
                 + [cblk(a) for a in caches]
                 + [pl.BlockSpec(memory_space=pl.ANY)] * 3,
        out_specs=[out_spec] * 3,
        input_output_aliases={n_in + k: k for k in range(3)},
        compiler_params=pltpu.CompilerParams(vmem_limit_bytes=VMEM_LIMIT),
        name="attention_lat",
    )(scal, subln, wukv, *qs, ka_l, va_l, kd_l, vd_l, kmn, vm, kpe4, *caches, *ctx_out)


def _route(scores, biased):
    per = N_EXPERTS // N_GROUPS
    tm = scores.shape[1]
    sub = lax.broadcasted_iota(jnp.int32, (per, tm), 0).astype(F32)
    groups = [biased[g * per:(g + 1) * per, :] for g in range(N_GROUPS)]
    gscore = []
    for v in groups:
        m1 = jnp.max(v, axis=0, keepdims=True)
        first = jnp.min(jnp.where(v == m1, sub, float(per)), axis=0, keepdims=True)
        m2 = jnp.max(jnp.where(sub == first, -jnp.inf, v), axis=0, keepdims=True)
        gscore.append(m1 + m2)
    vals = []
    for g in range(N_GROUPS):
        rank = jnp.zeros((1, tm), F32)
        for o in range(N_GROUPS):
            if o == g:
                continue
            ahead = (gscore[o] >= gscore[g]) if o < g else (gscore[o] > gscore[g])
            rank = rank + jnp.where(ahead, 1.0, 0.0)
        vals.append(jnp.where(rank < TOPK_GROUPS, groups[g], NEG))
    idx = [sub + float(g * per) for g in range(N_GROUPS)]
    picks, weights = [], []
    for _ in range(TOP_K):
        best = functools.reduce(jnp.maximum, [jnp.max(v, axis=0, keepdims=True) for v in vals])
        cand = functools.reduce(jnp.minimum, [
            jnp.min(jnp.where(v == best, i, float(N_EXPERTS)), axis=0, keepdims=True)
            for v, i in zip(vals, idx)])
        wsel = jnp.zeros((1, tm), F32)
        for g in range(N_GROUPS):
            hit = idx[g] == cand
            wsel = wsel + jnp.sum(jnp.where(hit, scores[g * per:(g + 1) * per, :], 0.0),
                                  axis=0, keepdims=True)
            vals[g] = jnp.where(hit, -jnp.inf, vals[g])
        picks.append(cand)
        weights.append(wsel)
    total = functools.reduce(lambda a, b: a + b, weights)
    pad = [jnp.zeros((1, tm), F32)] * (8 - TOP_K)
    experts = jnp.concatenate(picks + pad, axis=0).astype(jnp.int32)
    gates = jnp.concatenate([x / total * ROUTED_SCALE for x in weights] + pad, axis=0)
    return experts, gates


def _post_kernel(*refs, n_x):
    (mod_ref, oa_ref, od_ref, om_ref, gpre_ref, wgl_ref, wa_ref, wb_ref, wc_ref, wo_ref, gpost_ref,
     gffn_ref, rwt_ref, rb_ref, xmid_o, h2_o, h2t_o, experts_o, gates_o) = refs[n_x:]

    @pl.when(pl.program_id(0) < N_TILES)
    def _():
        x = _load_x(refs[:n_x])
        m = mod_ref[0]
        d = D_MODEL
        sh1, sc1, g1, sh2, sc2 = (m[:, 0:d], m[:, d:2 * d], m[:, 2 * d:3 * d], m[:, 3 * d:4 * d],
                                  m[:, 4 * d:5 * d])
        h = _rms(x, gpre_ref[...]) * (1.0 + sc1) + sh1
        gate = jax.nn.sigmoid(_dot(h.astype(BF16), wgl_ref[...]))
        merged = (gate[:, 0:d] * _dot(oa_ref[...], wa_ref[...])
                  + gate[:, d:2 * d] * _dot(od_ref[...], wb_ref[...])
                  + gate[:, 2 * d:3 * d] * _dot(om_ref[...], wc_ref[...]))
        a = _dot(merged.astype(BF16), wo_ref[...])
        xm = x + g1 * _rms(a, gpost_ref[...])
        xmid_o[...] = xm
        h2f = _rms(xm, gffn_ref[...]) * (1.0 + sc2) + sh2
        h2 = h2f.astype(BF16)
        h2_o[...] = h2
        for c in range(ACC_ROWS):
            h2t_o[pl.ds(c, x.shape[0], stride=ACC_ROWS), :] = h2f[:, c * LANES:(c + 1) * LANES]
        scores = jax.nn.sigmoid(_dot_nt(rwt_ref[...], h2))
        experts, gates = _route(scores, scores + rb_ref[...])
        experts_o[...] = experts
        gates_o[...] = gates

    @pl.when(pl.program_id(0) == N_TILES)
    def _():
        h2t_o[...] = jnp.zeros_like(h2t_o)


def _post_attention(x, mod, oa, od, om, wts, layer):
    tm = ROW_TILE

    def row(i):
        return (_tile(i), 0)

    weights = [wts[k] for k in ("g_pre", "wgl", "wa", "wb", "wc", "wo", "g_post", "g_ffn", "rwt", "rb")]
    xs = list(x) if isinstance(x, tuple) else [x]
    return pl.pallas_call(
        functools.partial(_post_kernel, n_x=len(xs)),
        out_shape=[jax.ShapeDtypeStruct((N_TOK, D_MODEL), F32),
                   jax.ShapeDtypeStruct((N_TOK, D_MODEL), BF16),
                   jax.ShapeDtypeStruct(((N_TOK + tm) * ACC_ROWS, LANES), F32),
                   jax.ShapeDtypeStruct((8, N_TOK), jnp.int32),
                   jax.ShapeDtypeStruct((8, N_TOK), F32)],
        grid=(N_TILES + 1,),
        in_specs=_x_specs(x) + [_mod_spec(layer),
                  pl.BlockSpec((tm, 512), row), pl.BlockSpec((tm, 512), row),
                  pl.BlockSpec((tm, 512), row)]
                 + [_layer_spec(w, layer, 1) for w in weights],
        out_specs=[pl.BlockSpec((tm, D_MODEL), row), pl.BlockSpec((tm, D_MODEL), row),
                   pl.BlockSpec((tm * ACC_ROWS, LANES), lambda i: (i, 0)),
                   pl.BlockSpec((8, tm), lambda i: (0, _tile(i))),
                   pl.BlockSpec((8, tm), lambda i: (0, _tile(i)))],
        compiler_params=pltpu.CompilerParams(
            dimension_semantics=("arbitrary",), vmem_limit_bytes=VMEM_LIMIT),
        name="post_attention",
    )(*xs, mod, oa, od, om, *weights)


def _ffn(x, wg, wu):
    return _silu(_dot(x, wg)) * _dot(x, wu)


def _moe_kernel(start_ref, count_ref, dst_ref, gate_ref, xs_ref, wg_ref, wu_ref, wd_ref,
                acc_ref, xa, xb, ya, yb, wgb, wub, wdb, pend):
    e = pl.program_id(0)

    @pl.when(e == 0)
    def _():
        def clear(i, carry):
            r0 = pl.multiple_of(i * MOE_ZERO_ROWS, MOE_ZERO_ROWS)
            acc_ref[pl.ds(r0, MOE_ZERO_ROWS), :] = jnp.zeros((MOE_ZERO_ROWS, LANES), F32)
            return carry

        lax.fori_loop(0, acc_ref.shape[0] // MOE_ZERO_ROWS, clear, 0)

        yb[...] = jnp.zeros_like(yb)
        pend[0] = MOE_LIST_PAD_BASE

    wgb[...] = wg_ref[0].astype(BF16)
    wub[...] = wu_ref[0].astype(BF16)
    wdb[...] = wd_ref[0].astype(BF16)
    start = start_ref[e]
    pairs = (count_ref[e] + 2 * MOE_SUB - 1) // (2 * MOE_SUB)

    def gather(base, xbuf):
        for mi in range(MOE_SUB):
            src = pl.multiple_of(dst_ref[base + mi], ACC_ROWS)
            xbuf[mi * ACC_ROWS:(mi + 1) * ACC_ROWS, :] = xs_ref[pl.ds(src, ACC_ROWS), :]

    def expert_ffn(xbuf, ybuf):
        x = jnp.concatenate([xbuf[pl.ds(c, MOE_SUB, stride=ACC_ROWS), :].astype(BF16)
                             for c in range(ACC_ROWS)], axis=1)
        hid = _ffn(x, wgb[...], wub[...])
        y = _dot(hid.astype(BF16), wdb[...])
        for c in range(ACC_ROWS):
            ybuf[pl.ds(c, MOE_SUB, stride=ACC_ROWS), :] = y[:, c * LANES:(c + 1) * LANES]

    def scatter(base, ybuf):
        for g0 in range(0, MOE_SUB, MOE_RMW_GROUP):
            updates = []
            for mi in range(g0, g0 + MOE_RMW_GROUP):
                dst = pl.multiple_of(dst_ref[base + mi], ACC_ROWS)
                yv = ybuf[mi * ACC_ROWS:(mi + 1) * ACC_ROWS, :]
                updates.append((dst, acc_ref[pl.ds(dst, ACC_ROWS), :] + gate_ref[base + mi] * yv))
            for dst, val in updates:
                acc_ref[pl.ds(dst, ACC_ROWS), :] = val

    start_next = start_ref[jnp.minimum(e + 1, N_EXPERTS - 1)]
    prev_count = count_ref[jnp.maximum(e - 1, 0)]

    @pl.when((pairs > 0) & ((e == 0) | (prev_count == 0)))
    def _():
        gather(start, xa)

    def pair(p, carry):
        base = start + p * (2 * MOE_SUB)
        gather(base + MOE_SUB, xb)
        expert_ffn(xa, ya)
        scatter(pend[0], yb)
        gather(jnp.where(p + 1 < pairs, base + 2 * MOE_SUB, start_next), xa)
        expert_ffn(xb, yb)
        scatter(base, ya)
        pend[0] = base + MOE_SUB
        return carry

    lax.fori_loop(0, pairs, pair, 0)

    @pl.when(e == N_EXPERTS - 1)
    def _():
        scatter(pend[0], yb)


def _moe(lists, h2t, wg, wu, wd, layer):
    def expert(e, *_):
        return (layer, e, 0, 0)

    def whole(e, *_):
        return (0, 0)

    acc_rows = (N_TOK + MOE_SPARE_TOKENS) * ACC_ROWS
    assert acc_rows % MOE_ZERO_ROWS == 0
    return pl.pallas_call(
        _moe_kernel,
        out_shape=jax.ShapeDtypeStruct((acc_rows, LANES), F32),
        grid_spec=pltpu.PrefetchScalarGridSpec(
            num_scalar_prefetch=len(lists),
            grid=(N_EXPERTS,),
            in_specs=[pl.BlockSpec((acc_rows, LANES), whole, pipeline_mode=pl.Buffered(1)),
                      pl.BlockSpec((None, 1, D_MODEL, F_EXPERT), expert),
                      pl.BlockSpec((None, 1, D_MODEL, F_EXPERT), expert),
                      pl.BlockSpec((None, 1, F_EXPERT, D_MODEL), expert)],
            out_specs=pl.BlockSpec((acc_rows, LANES), whole, pipeline_mode=pl.Buffered(1)),
            scratch_shapes=[pltpu.VMEM((MOE_SUB * ACC_ROWS, LANES), F32)] * 4
                           + [pltpu.VMEM((D_MODEL, F_EXPERT), BF16),
                              pltpu.VMEM((D_MODEL, F_EXPERT), BF16),
                              pltpu.VMEM((F_EXPERT, D_MODEL), BF16),
                              pltpu.SMEM((1,), jnp.int32)]),
        compiler_params=pltpu.CompilerParams(
            dimension_semantics=("arbitrary",), vmem_limit_bytes=MOE_VMEM_LIMIT),
        name="moe",
    )(*lists, h2t, wg, wu, wd)


def _final_kernel(acc_ref, h2_ref, xmid_ref, mod_ref, swg_ref, swu_ref, swd_ref, gpost_ref, *out_refs):
    tm = h2_ref.shape[0]
    routed = jnp.concatenate(
        [acc_ref[pl.ds(c, tm, stride=ACC_ROWS), :] for c in range(ACC_ROWS)], axis=1)
    hid = _ffn(h2_ref[...], swg_ref[...], swu_ref[...])
    shared = _dot(hid.astype(BF16), swd_ref[...])
    g2 = mod_ref[0][:, 5 * D_MODEL:6 * D_MODEL]
    out = xmid_ref[...] + g2 * _rms(routed + shared, gpost_ref[...])
    if len(out_refs) == 1:
        out_refs[0][...] = out
    else:
        @pl.when(pl.program_id(0) < CTX_TILES)
        def _():
            out_refs[0][...] = out

        @pl.when(pl.program_id(0) >= CTX_TILES)
        def _():
            out_refs[1][...] = out


def _final(acc, h2, xmid, mod, wts, layer, split):
    tm = ROW_TILE

    def row(i):
        return (i, 0)

    weights = [wts[k] for k in ("swg", "swu", "swd", "g_ffn_post")]
    return pl.pallas_call(
        _final_kernel,
        out_shape=([jax.ShapeDtypeStruct((N_CTX, D_MODEL), F32),
                    jax.ShapeDtypeStruct((N_LAT, D_MODEL), F32)] if split
                   else jax.ShapeDtypeStruct((N_TOK, D_MODEL), F32)),
        grid=(N_TOK // tm,),
        in_specs=[pl.BlockSpec((tm * ACC_ROWS, LANES), row),
                  pl.BlockSpec((tm, D_MODEL), row),
                  pl.BlockSpec((tm, D_MODEL), row),
                  _mod_spec(layer)]
                 + [_layer_spec(w, layer, 1) for w in weights],
        out_specs=_x_specs((None, None)) if split else pl.BlockSpec((tm, D_MODEL), row),
        compiler_params=pltpu.CompilerParams(
            dimension_semantics=("arbitrary",), vmem_limit_bytes=VMEM_LIMIT),
        name="ffn_final",
    )(acc, h2, xmid, mod, *weights)


def _routing_lists(experts, gates):
    e = experts[:TOP_K].reshape(-1)
    tok = jnp.tile(jnp.arange(N_TOK, dtype=jnp.int32), TOP_K)
    pad_e = jnp.repeat(jnp.arange(N_EXPERTS, dtype=jnp.int32), MOE_PAD)
    pad_tok = N_TOK + jnp.tile(jnp.arange(MOE_PAD, dtype=jnp.int32), N_EXPERTS)
    keys = jnp.concatenate([e * TOK_KEY + tok, pad_e * TOK_KEY + pad_tok])
    vals = jnp.concatenate([gates[:TOP_K].reshape(-1), jnp.zeros((N_EXPERTS * MOE_PAD,), F32)])
    keys, gate_sorted = lax.sort((keys, vals), num_keys=1)
    tail = jnp.full((MOE_SUB,), N_TOK, jnp.int32)
    tok_sorted = jnp.concatenate([keys & (TOK_KEY - 1), tail])
    gate_sorted = jnp.concatenate([gate_sorted, jnp.zeros((MOE_SUB,), F32)])
    dst = jnp.minimum(tok_sorted, N_TOK) * ACC_ROWS
    counts = jnp.sum((e[None, :] == jnp.arange(N_EXPERTS, dtype=jnp.int32)[:, None]).astype(jnp.int32),
                     axis=1)
    start = jnp.cumsum(counts) - counts + MOE_PAD * jnp.arange(N_EXPERTS, dtype=jnp.int32)
    return start.astype(jnp.int32), counts, dst, gate_sorted


def _rope_tables():
    t = np.arange(LAT_SEQ)
    pos = np.stack([t // GRID_W, t % GRID_W], axis=1).astype(np.float64)

    def table(half):
        inv = ROPE_BASE ** (-np.arange(half, dtype=np.float64) / half)
        lane = np.arange(LANES)
        axis = (lane // (2 * half)) % 2
        freq = inv[lane % half]
        ang = pos[:, axis] * freq[None, :]
        sign = np.where((lane % (2 * half)) < half, -1.0, 1.0)
        cos = np.concatenate([np.ones((ROW_TILE, LANES)), np.cos(ang)], axis=0)
        sin = np.concatenate([np.zeros((ROW_TILE, LANES)), np.sin(ang) * sign[None, :]], axis=0)
        return jnp.asarray(cos, F32), jnp.asarray(sin, F32)

    cos64, sin64 = table(16)
    cos32, sin32 = table(8)
    return cos64, sin64, cos32, sin32


def _prepare_weights(norm_attn_pre, norm_attn_post, norm_ffn_pre, norm_ffn_post, w_in, dif_subln,
                     mla_q_norm, mla_w_uq, mla_kv_norm, mla_w_ukv, w_branch_a, w_branch_b,
                     w_branch_c, w_out, router_w, router_bias, shared_w_gate, shared_w_up,
                     shared_w_down):
    dp = DEPTH
    qa = w_in[:, :, 0:512].reshape(dp, D_MODEL, 2, 4, 64).transpose(0, 1, 3, 2, 4).reshape(dp, D_MODEL, 512)
    kpe = w_in[:, :, 2688:2720]
    w1 = jnp.concatenate([qa, w_in[:, :, 512:2688], kpe, kpe, kpe, kpe], axis=2).astype(BF16)
    uq = mla_w_uq.reshape(dp, 256, 8, 96)
    wuq = jnp.concatenate([uq[..., :64].reshape(dp, 256, 512), uq[..., 64:].reshape(dp, 256, 256)],
                          axis=2).astype(BF16)
    ukv = mla_w_ukv.reshape(dp, 128, 8, 128)
    wukv = jnp.concatenate([ukv[..., :64].reshape(dp, 128, 512), ukv[..., 64:].reshape(dp, 128, 512)],
                           axis=2).astype(BF16)
    wa = w_branch_a.reshape(dp, 2, 4, 64, D_MODEL).transpose(0, 2, 1, 3, 4).reshape(dp, 512, D_MODEL)
    return dict(
        g_pre=norm_attn_pre[:, None, :], g_post=norm_attn_post[:, None, :],
        g_ffn=norm_ffn_pre[:, None, :], g_ffn_post=norm_ffn_post[:, None, :],
        w1=w1, wgl=w_in[:, :, 2720:].astype(BF16),
        qnorm=mla_q_norm[:, None, :], wuq=wuq, kvnorm=mla_kv_norm[:, None, :], wukv=wukv,
        subln=dif_subln[:, None, :],
        wa=wa.astype(BF16), wb=w_branch_b.astype(BF16), wc=w_branch_c.astype(BF16),
        wo=w_out.astype(BF16), rwt=router_w.transpose(0, 2, 1).astype(BF16),
        rb=router_bias[:, :, None],
        swg=shared_w_gate.astype(BF16), swu=shared_w_up.astype(BF16), swd=shared_w_down.astype(BF16))


def kernel(x_prompt, x_sample, cache_swa_k, cache_swa_v, cache_dif_k, cache_dif_v, cache_mla_ckv, cache_mla_kpe, c, c_ctx, w_mod, b_mod, norm_attn_pre, norm_attn_post, norm_ffn_pre, norm_ffn_post, w_in, swa_sink, dif_lq1, dif_lk1, dif_lq2, dif_lk2, dif_subln, mla_q_norm, mla_w_uq, mla_kv_norm, mla_w_ukv, w_branch_a, w_branch_b, w_branch_c, w_out, router_w, router_bias, moe_w_gate, moe_w_up, moe_w_down, shared_w_gate, shared_w_up, shared_w_down):
    x = (x_prompt.reshape(N_CTX, D_MODEL), x_sample.reshape(N_LAT, D_MODEL))
    cvec = jnp.concatenate([c_ctx[None, :], c, jnp.zeros((8 - 1 - N_LAT_BATCH, D_MODEL), F32)], axis=0)
    mod = _modulation(cvec, w_mod, b_mod).reshape(DEPTH, 8, 1, 6 * D_MODEL)
    tables = _rope_tables()
    wts = _prepare_weights(norm_attn_pre, norm_attn_post, norm_ffn_pre, norm_ffn_post, w_in,
                           dif_subln, mla_q_norm, mla_w_uq, mla_kv_norm, mla_w_ukv, w_branch_a,
                           w_branch_b, w_branch_c, w_out, router_w, router_bias, shared_w_gate,
                           shared_w_up, shared_w_down)
    caches = [cache_swa_k.reshape(N_LAT_BATCH, DEPTH, PAST, 128),
              cache_swa_v.reshape(N_LAT_BATCH, DEPTH, PAST, 128),
              cache_dif_k.reshape(N_LAT_BATCH, DEPTH, PAST, 512),
              cache_dif_v.reshape(N_LAT_BATCH, DEPTH, PAST, 512),
              cache_mla_ckv,
              jnp.tile(cache_mla_kpe, (1, 1, 1, 4))]
    lam_init = [0.8 - 0.6 * math.exp(-0.3 * l) for l in range(DEPTH)]
    lam = (jnp.exp(jnp.sum(dif_lq1 * dif_lk1, axis=1)) - jnp.exp(jnp.sum(dif_lq2 * dif_lk2, axis=1))
           + jnp.asarray(lam_init, F32))
    scal = jnp.concatenate([swa_sink, lam[:, None]], axis=1).astype(F32)

    states = None
    for l in range(DEPTH):
        proj = _in_projection(x, mod, wts, tables, l, states)
        states = proj[11:]
        ctx_out = _attention_ctx(scal[l], wts["subln"], proj, states, l, lam_init[l])
        oa, od, om = _attention_lat(scal[l], wts["subln"], wts["wukv"], proj, caches, ctx_out, l,
                                    lam_init[l])
        xmid, h2, h2t, experts, gates = _post_attention(x, mod, oa, od, om, wts, l)
        lists = _routing_lists(experts, gates)
        acc = _moe(lists, h2t, moe_w_gate, moe_w_up, moe_w_down, l)
        x = _final(acc, h2, xmid, mod, wts, l, split=(l == DEPTH - 1))

    y_p = x[0].reshape(N_CTX_BATCH, CTX_SEQ, D_MODEL)
    y_s = x[1].reshape(N_LAT_BATCH, LAT_SEQ, D_MODEL)
    s_ka, s_va, s_kd, s_vd, s_ckv, s_kpe = states
    return (y_p, y_s,
            s_ka.reshape(N_CTX_BATCH, DEPTH, CTX_SEQ, 2, 64),
            s_va.reshape(N_CTX_BATCH, DEPTH, CTX_SEQ, 2, 64),
            s_kd.reshape(N_CTX_BATCH, DEPTH, CTX_SEQ, 4, 2, 64),
            s_vd.reshape(N_CTX_BATCH, DEPTH, CTX_SEQ, 4, 128),
            s_ckv, s_kpe)
```

```python
import functools
import math

import numpy as np
import jax
import jax.numpy as jnp
from jax import lax
from jax.experimental import pallas as pl
from jax.experimental.pallas import tpu as pltpu

F32 = jnp.float32
BF16 = jnp.bfloat16

D_MODEL = 1024
N_CTX_BATCH, CTX_SEQ = 16, 256
N_LAT_BATCH, LAT_SEQ = 2, 1024
PAST = 256
N_CTX = N_CTX_BATCH * CTX_SEQ
N_LAT = N_LAT_BATCH * LAT_SEQ
N_TOK = N_CTX + N_LAT
DEPTH = 2
GRID_W = 64
WINDOW = 128
N_EXPERTS = 64
N_GROUPS = 8
TOPK_GROUPS = 4
TOP_K = 6
F_EXPERT = 256
ROUTED_SCALE = 2.5
ROPE_BASE = 10000.0
EPS = 1e-6
NEG = -1e30

LANES = 128
ROW_TILE = 256
Q_BLOCK = 256
VMEM_LIMIT = 56 * 1024 * 1024
ACC_ROWS = 8
MOE_SPARE_TOKENS = 8
MOE_ZERO_ROWS = 64
MOE_VMEM_LIMIT = 62 * 1024 * 1024
MOE_SUB = 128
MOE_RMW_GROUP = 4
MOE_PAD = 2 * MOE_SUB
TOK_KEY = 8192
MOE_LIST_PAD_BASE = N_TOK * TOP_K + N_EXPERTS * MOE_PAD

C_QA, C_KA, C_VA, C_QD, C_KD, C_VD, C_QC, C_KVC, C_KPE, C_END = (
    0, 512, 640, 768, 1280, 1792, 2304, 2560, 2688, 2816)
SCALE_64 = 1.0 / math.sqrt(64.0)
SCALE_96 = 1.0 / math.sqrt(96.0)

CTX_TILES = N_CTX // ROW_TILE
N_TILES = N_TOK // ROW_TILE
assert ROW_TILE == CTX_SEQ and LAT_SEQ % ROW_TILE == 0


def _mod_row_of_tile(i):
    return jnp.where(i < CTX_TILES, 0, 1 + (i - CTX_TILES) // (LAT_SEQ // ROW_TILE))


def _rms(x, g):
    return x * lax.rsqrt(jnp.mean(x * x, axis=-1, keepdims=True) + EPS) * g


def _dot(a, b):
    return jnp.dot(a, b, preferred_element_type=F32)


def _dot_nt(a, b):
    return lax.dot_general(a, b, (((1,), (1,)), ((), ())), preferred_element_type=F32)


def _silu(x):
    return x * jax.nn.sigmoid(x)


def _layer_spec(arr, layer, grid_rank):
    zeros = (0,) * (arr.ndim - 1)
    return pl.BlockSpec((None,) + arr.shape[1:], lambda *_: (layer,) + zeros)


def _mod_kernel(c_ref, w_ref, b_ref, o_ref):
    c = c_ref[...]
    o_ref[0] = _dot(_silu(c).astype(BF16), w_ref[0].astype(BF16)) + b_ref[0]


def _modulation(cvec, w_mod, b_mod):
    tn = 1536
    n = w_mod.shape[-1]
    return pl.pallas_call(
        _mod_kernel,
        out_shape=jax.ShapeDtypeStruct((DEPTH, 8, n), F32),
        grid=(DEPTH, n // tn),
        in_specs=[
            pl.BlockSpec((8, D_MODEL), lambda l, j: (0, 0)),
            pl.BlockSpec((1, D_MODEL, tn), lambda l, j: (l, 0, j)),
            pl.BlockSpec((1, 1, tn), lambda l, j: (l, 0, j)),
        ],
        out_specs=pl.BlockSpec((1, 8, tn), lambda l, j: (l, 0, j)),
        compiler_params=pltpu.CompilerParams(vmem_limit_bytes=VMEM_LIMIT),
        name="modulation",
    )(cvec, w_mod, b_mod.reshape(DEPTH, 1, n))


def _tile(i):
    return jnp.minimum(i, N_TILES - 1)


def _x_specs(x):
    if isinstance(x, tuple):
        return [pl.BlockSpec((ROW_TILE, D_MODEL), lambda i: (jnp.minimum(i, CTX_TILES - 1), 0)),
                pl.BlockSpec((ROW_TILE, D_MODEL), lambda i: (jnp.maximum(_tile(i) - CTX_TILES, 0), 0))]
    return [pl.BlockSpec((ROW_TILE, D_MODEL), lambda i: (_tile(i), 0))]


def _load_x(x_refs):
    if len(x_refs) == 1:
        return x_refs[0][...]
    return jnp.where(pl.program_id(0) < CTX_TILES, x_refs[0][...], x_refs[1][...])


def _mod_spec(layer):
    return pl.BlockSpec((None, 1, 1, 6 * D_MODEL),
                        lambda i: (layer, _mod_row_of_tile(_tile(i)), 0, 0))


def _rope128(x, cos, sin, half):
    lane = lax.broadcasted_iota(jnp.int32, x.shape, 1)
    first = (lane % (2 * half)) < half
    partner = jnp.where(first, pltpu.roll(x, LANES - half, 1), pltpu.roll(x, half, 1))
    return x * cos + partner * sin


def _rope_cols(x, cos, sin, half):
    chunks = [_rope128(x[:, c:c + LANES], cos, sin, half) for c in range(0, x.shape[1], LANES)]
    return chunks[0] if len(chunks) == 1 else jnp.concatenate(chunks, axis=1)


N_IN_INPUTS = 11


def _in_kernel(*refs, n_x):
    (mod_ref, g_ref, w1_ref, qn_ref, wuq_ref, kvn_ref, wukv_ref,
     cos64_ref, sin64_ref, cos32_ref, sin32_ref) = refs[n_x:n_x + N_IN_INPUTS]
    (qa_o, qd_o, qmn_o, qmr_o, kmn_o, vm_o, kpe4_o, ka_l, va_l, kd_l, vd_l,
     s_ka, s_va, s_kd, s_vd, s_ckv, s_kpe) = refs[-17:]
    i = pl.program_id(0)
    x = _load_x(refs[:n_x])
    m = mod_ref[0]
    sh1, sc1 = m[:, 0:D_MODEL], m[:, D_MODEL:2 * D_MODEL]
    h = _rms(x, g_ref[...]) * (1.0 + sc1) + sh1
    z = _dot(h.astype(BF16), w1_ref[...])
    cos64, sin64 = cos64_ref[...], sin64_ref[...]
    cos32, sin32 = cos32_ref[...], sin32_ref[...]
    qa_o[...] = (_rope_cols(z[:, C_QA:C_KA], cos64, sin64, 16) * SCALE_64).astype(BF16)
    qd_o[...] = (_rope_cols(z[:, C_QD:C_KD], cos64, sin64, 16) * SCALE_64).astype(BF16)
    ka = _rope_cols(z[:, C_KA:C_VA], cos64, sin64, 16)
    va = z[:, C_VA:C_QD]
    kd = _rope_cols(z[:, C_KD:C_VD], cos64, sin64, 16)
    vd = z[:, C_VD:C_QC]
    kpe4 = _rope_cols(z[:, C_KPE:C_END], cos32, sin32, 8)
    kpe4_o[...] = kpe4.astype(BF16)
    qn = _rms(z[:, C_QC:C_KVC], qn_ref[...])
    qm = _dot(qn.astype(BF16), wuq_ref[...]) * SCALE_96
    qmn_o[...] = qm[:, 0:512].astype(BF16)
    qmr_o[...] = _rope_cols(qm[:, 512:768], cos32, sin32, 8).astype(BF16)
    ckv = _rms(z[:, C_KVC:C_KPE], kvn_ref[...])
    kv = _dot(ckv.astype(BF16), wukv_ref[...])
    kmn_o[...] = kv[:, 0:512].astype(BF16)
    vm_o[...] = kv[:, 512:1024].astype(BF16)

    @pl.when(i < CTX_TILES)
    def _():
        s_ka[...] = ka
        s_va[...] = va
        s_kd[...] = kd
        s_vd[...] = vd
        s_ckv[...] = ckv
        s_kpe[...] = kpe4[:, 0:32]

    @pl.when(i >= CTX_TILES)
    def _():
        ka_l[...] = ka.astype(BF16)
        va_l[...] = va.astype(BF16)
        kd_l[...] = kd.astype(BF16)
        vd_l[...] = vd.astype(BF16)


STATE_WIDTHS = (128, 128, 512, 512, 128, 32)


def _in_projection(x, mod, wts, tables, layer, prev_states):
    tm = ROW_TILE

    def row(i):
        return (i, 0)

    def table_row(i):
        return (jnp.where(i < CTX_TILES, 0, 1 + (i - CTX_TILES) % (LAT_SEQ // tm)), 0)

    def lat_row(i):
        return (jnp.maximum(i - CTX_TILES, 0), 0)

    def state_block(i):
        return (jnp.minimum(i, CTX_TILES - 1), layer, 0, 0)

    table_spec = pl.BlockSpec((tm, LANES), table_row)
    weights = [wts["g_pre"], wts["w1"], wts["qnorm"], wts["wuq"], wts["kvnorm"], wts["wukv"]]
    all_rows = [(512, BF16)] * 3 + [(256, BF16), (512, BF16), (512, BF16), (128, BF16)]
    lat_rows = [(128, BF16), (128, BF16), (512, BF16), (512, BF16)]
    aliased = [] if prev_states is None else list(prev_states)
    xs = list(x) if isinstance(x, tuple) else [x]
    n_in = len(xs) + N_IN_INPUTS
    n_plain_out = len(all_rows) + len(lat_rows)
    return pl.pallas_call(
        functools.partial(_in_kernel, n_x=len(xs)),
        out_shape=[jax.ShapeDtypeStruct((N_TOK, w), dt) for w, dt in all_rows]
                  + [jax.ShapeDtypeStruct((N_LAT, w), dt) for w, dt in lat_rows]
                  + [jax.ShapeDtypeStruct((N_CTX_BATCH, DEPTH, CTX_SEQ, w), F32) for w in STATE_WIDTHS],
        grid=(N_TOK // tm,),
        in_specs=_x_specs(x) + [_mod_spec(layer)]
                 + [_layer_spec(w, layer, 1) for w in weights]
                 + [table_spec] * 4
                 + [pl.BlockSpec(memory_space=pl.ANY)] * len(aliased),
        out_specs=[pl.BlockSpec((tm, w), row) for w, _ in all_rows]
                  + [pl.BlockSpec((tm, w), lat_row) for w, _ in lat_rows]
                  + [pl.BlockSpec((None, None, CTX_SEQ, w), state_block) for w in STATE_WIDTHS],
        input_output_aliases={n_in + k: n_plain_out + k for k in range(len(aliased))},
        compiler_params=pltpu.CompilerParams(
            dimension_semantics=("arbitrary",), vmem_limit_bytes=VMEM_LIMIT),
        name="in_projection",
    )(*xs, mod, *weights, *tables, *aliased)


def _attend(q, parts, sink=None):
    scores = []
    for k, _, mask in parts:
        s = _dot_nt(q, k)
        if mask is not None:
            s = jnp.where(mask, s, NEG)
        scores.append(s)
    m = functools.reduce(jnp.maximum, [jnp.max(s, axis=-1, keepdims=True) for s in scores])
    if sink is not None:
        m = jnp.maximum(m, sink)
    den = None
    out = None
    for s, (_, v, _) in zip(scores, parts):
        e = jnp.exp(s - m)
        d = jnp.sum(e, axis=-1, keepdims=True)
        o = _dot(e.astype(BF16), v)
        den = d if den is None else den + d
        out = o if out is None else out + o
    if sink is not None:
        den = den + jnp.exp(sink - m)
    return out / den


A_HEADS = 8


def _attend_heads(qs, parts, sinks, stack):
    rows = qs[0].shape[0]
    if not stack:
        return [_attend(q, parts, None if sinks is None else sinks[i]) for i, q in enumerate(qs)]
    sink = None
    if sinks is not None:
        sink = jnp.concatenate([jnp.full((rows, 1), s, F32) for s in sinks], axis=0)
    o = _attend(jnp.concatenate(qs, axis=0), parts, sink)
    return [o[i * rows:(i + 1) * rows] for i in range(len(qs))]


def _three_mixers(scal_ref, subln_ref, q_blocks, kv_parts, oa_o, od_o, om_o, lam_init, stack):
    qa, qd, qmn, qmr = q_blocks
    rows = qa.shape[0]
    lane = lax.broadcasted_iota(jnp.int32, (rows, LANES), 1)
    low = lane < 64
    zero = jnp.zeros((rows, LANES), BF16)

    parts = [(p["ka"], p["va"], p.get("mask_a")) for p in kv_parts]
    qs = [jnp.where(low if kvh == 0 else ~low, qa[:, g * LANES:(g + 1) * LANES], zero)
          for kvh in range(2) for g in range(4)]
    out_a = _attend_heads(qs, parts, [scal_ref[h] for h in range(A_HEADS)], stack)
    for g in range(4):
        oa_o[:, g * LANES:(g + 1) * LANES] = jnp.where(low, out_a[g], out_a[4 + g]).astype(BF16)

    lam = scal_ref[8]
    subln = subln_ref[...]
    for h in range(4):
        sl = slice(h * LANES, (h + 1) * LANES)
        q128 = qd[:, sl]
        o1, o2 = _attend_heads([jnp.where(low, q128, zero), jnp.where(low, zero, q128)],
                               [(p["kd"][:, sl], p["vd"][:, sl], None) for p in kv_parts],
                               None, stack)
        od = o1 - lam * o2
        od_o[:, sl] = (_rms(od, subln) * (1.0 - lam_init)).astype(BF16)

    for i in range(4):
        sl = slice(i * LANES, (i + 1) * LANES)
        qs = []
        for sub in range(2):
            h = 2 * i + sub
            qn = jnp.where(low if sub == 0 else ~low, qmn[:, sl], zero)
            qr128 = qmr[:, (h // 4) * LANES:(h // 4 + 1) * LANES]
            qr = jnp.where((lane // 32) == (h % 4), qr128, zero)
            qs.append(jnp.concatenate([qn, qr], axis=1))
        parts = [(jnp.concatenate([p["kmn"][:, sl], p["kpe"]], axis=1), p["vm"][:, sl], None)
                 for p in kv_parts]
        o_even, o_odd = _attend_heads(qs, parts, None, stack)
        om_o[:, sl] = jnp.where(low, o_even, o_odd).astype(BF16)


def _attn_ctx_kernel(scal_ref, subln_ref, qa_ref, qd_ref, qmn_ref, qmr_ref,
                     ka_ref, va_ref, kd_ref, vd_ref, kmn_ref, vm_ref, kpe_ref,
                     oa_o, od_o, om_o, *, lam_init):
    part = dict(ka=ka_ref[...].astype(BF16), va=va_ref[...].astype(BF16),
                kd=kd_ref[...].astype(BF16), vd=vd_ref[...].astype(BF16),
                kmn=kmn_ref[...], vm=vm_ref[...], kpe=kpe_ref[...])
    _three_mixers(scal_ref, subln_ref, (qa_ref[...], qd_ref[...], qmn_ref[...], qmr_ref[...]),
                  [part], oa_o, od_o, om_o, lam_init, stack=False)


def _attn_lat_kernel(scal_ref, subln_ref, wukv_ref, qa_ref, qd_ref, qmn_ref, qmr_ref,
                     ka_ref, va_ref, kd_ref, vd_ref, kmn_ref, vm_ref, kpe_ref,
                     cka_ref, cva_ref, ckd_ref, cvd_ref, cckv_ref, ckpe_ref,
                     oa_in, od_in, om_in, oa_o, od_o, om_o, *, lam_init):
    del oa_in, od_in, om_in
    n = pl.program_id(1)
    kv_c = _dot(cckv_ref[...].astype(BF16), wukv_ref[...])
    cached = dict(ka=cka_ref[...].astype(BF16), va=cva_ref[...].astype(BF16),
                  kd=ckd_ref[...].astype(BF16), vd=cvd_ref[...].astype(BF16),
                  kmn=kv_c[:, 0:512].astype(BF16), vm=kv_c[:, 512:1024].astype(BF16),
                  kpe=ckpe_ref[...].astype(BF16))
    span = Q_BLOCK + 2 * WINDOW
    start = pl.multiple_of(jnp.clip(n * Q_BLOCK - WINDOW, 0, LAT_SEQ - span), WINDOW)
    qpos = n * Q_BLOCK + lax.broadcasted_iota(jnp.int32, (A_HEADS * Q_BLOCK, span), 0) % Q_BLOCK
    kpos = start + lax.broadcasted_iota(jnp.int32, (A_HEADS * Q_BLOCK, span), 1)
    mask_a = jnp.abs(qpos - kpos) <= WINDOW
    new = dict(ka=ka_ref[pl.ds(start, span), :], va=va_ref[pl.ds(start, span), :], mask_a=mask_a,
               kd=kd_ref[...], vd=vd_ref[...], kmn=kmn_ref[...], vm=vm_ref[...], kpe=kpe_ref[...])
    _three_mixers(scal_ref, subln_ref, (qa_ref[...], qd_ref[...], qmn_ref[...], qmr_ref[...]),
                  [cached, new], oa_o, od_o, om_o, lam_init, stack=True)


_SMEM_SPEC = pl.BlockSpec(memory_space=pltpu.SMEM)


def _attention_ctx(scal, subln, proj, states, layer, lam_init):
    qa, qd, qmn, qmr, kmn, vm, kpe4 = proj[:7]
    s_ka, s_va, s_kd, s_vd = states[:4]
    t = CTX_SEQ

    def blk(a):
        return pl.BlockSpec((t, a.shape[1]), lambda b: (b, 0))

    def sblk(a):
        return pl.BlockSpec((None, None, t, a.shape[-1]), lambda b: (b, layer, 0, 0))

    return pl.pallas_call(
        functools.partial(_attn_ctx_kernel, lam_init=lam_init),
        out_shape=[jax.ShapeDtypeStruct((N_TOK, 512), BF16)] * 3,
        grid=(N_CTX_BATCH,),
        in_specs=[_SMEM_SPEC, _layer_spec(subln, layer, 1)]
                 + [blk(a) for a in (qa, qd, qmn, qmr)]
                 + [sblk(a) for a in (s_ka, s_va, s_kd, s_vd)]
                 + [blk(a) for a in (kmn, vm, kpe4)],
        out_specs=[pl.BlockSpec((t, 512), lambda b: (b, 0))] * 3,
        compiler_params=pltpu.CompilerParams(vmem_limit_bytes=VMEM_LIMIT),
        name="attention_ctx",
    )(scal, subln, qa, qd, qmn, qmr, s_ka, s_va, s_kd, s_vd, kmn, vm, kpe4)


def _attention_lat(scal, subln, wukv, proj, caches, ctx_out, layer, lam_init):
    qa, qd, qmn, qmr, kmn, vm, kpe4, ka_l, va_l, kd_l, vd_l = proj[:11]
    nq = LAT_SEQ // Q_BLOCK
    q_off = N_CTX // Q_BLOCK
    kv_off = N_CTX // LAT_SEQ

    def qblk(a):
        return pl.BlockSpec((Q_BLOCK, a.shape[1]), lambda b, n: (q_off + b * nq + n, 0))

    def kvblk(a):
        return pl.BlockSpec((LAT_SEQ, a.shape[1]), lambda b, n: (b, 0))

    def kvblk_all(a):
        return pl.BlockSpec((LAT_SEQ, a.shape[1]), lambda b, n: (kv_off + b, 0))

    def cblk(a):
        return pl.BlockSpec((None, None, PAST, a.shape[-1]), lambda b, n: (b, layer, 0, 0))

    qs = [qa, qd, qmn, qmr]
    n_in = 3 + 4 + 7 + 6
    out_spec = pl.BlockSpec((Q_BLOCK, 512), lambda b, n: (q_off + b * nq + n, 0))
    return pl.pallas_call(
        functools.partial(_attn_lat_kernel, lam_init=lam_init),
        out_shape=[jax.ShapeDtypeStruct((N_TOK, 512), BF16)] * 3,
        grid=(N_LAT_BATCH, nq),
        in_specs=[_SMEM_SPEC, _layer_spec(subln, layer, 2), _layer_spec(wukv, layer, 2)]
                 + [qblk(a) for a in qs]
                 + [kvblk(a) for a in (ka_l, va_l, kd_l, vd_l)]
                 + [kvblk_all(a) for a in (kmn, vm, kpe4)]
                 + [cblk(a) for a in caches]
                 + [pl.BlockSpec(memory_space=pl.ANY)] * 3,
        out_specs=[out_spec] * 3,
        input_output_aliases={n_in + k: k for k in range(3)},
        compiler_params=pltpu.CompilerParams(vmem_limit_bytes=VMEM_LIMIT),
        name="attention_lat",
    )(scal, subln, wukv, *qs, ka_l, va_l, kd_l, vd_l, kmn, vm, kpe4, *caches, *ctx_out)


def _route(scores, biased):
    per = N_EXPERTS // N_GROUPS
    tm = scores.shape[1]
    sub = lax.broadcasted_iota(jnp.int32, (per, tm), 0).astype(F32)
    groups = [biased[g * per:(g + 1) * per, :] for g in range(N_GROUPS)]
    gscore = []
    for v in groups:
        m1 = jnp.max(v, axis=0, keepdims=True)
        first = jnp.min(jnp.where(v == m1, sub, float(per)), axis=0, keepdims=True)
        m2 = jnp.max(jnp.where(sub == first, -jnp.inf, v), axis=0, keepdims=True)
        gscore.append(m1 + m2)
    vals = []
    for g in range(N_GROUPS):
        rank = jnp.zeros((1, tm), F32)
        for o in range(N_GROUPS):
            if o == g:
                continue
            ahead = (gscore[o] >= gscore[g]) if o < g else (gscore[o] > gscore[g])
            rank = rank + jnp.where(ahead, 1.0, 0.0)
        vals.append(jnp.where(rank < TOPK_GROUPS, groups[g], NEG))
    idx = [sub + float(g * per) for g in range(N_GROUPS)]
    picks, weights = [], []
    for _ in range(TOP_K):
        best = functools.reduce(jnp.maximum, [jnp.max(v, axis=0, keepdims=True) for v in vals])
        cand = functools.reduce(jnp.minimum, [
            jnp.min(jnp.where(v == best, i, float(N_EXPERTS)), axis=0, keepdims=True)
            for v, i in zip(vals, idx)])
        wsel = jnp.zeros((1, tm), F32)
        for g in range(N_GROUPS):
            hit = idx[g] == cand
            wsel = wsel + jnp.sum(jnp.where(hit, scores[g * per:(g + 1) * per, :], 0.0),
                                  axis=0, keepdims=True)
            vals[g] = jnp.where(hit, -jnp.inf, vals[g])
        picks.append(cand)
        weights.append(wsel)
    total = functools.reduce(lambda a, b: a + b, weights)
    pad = [jnp.zeros((1, tm), F32)] * (8 - TOP_K)
    experts = jnp.concatenate(picks + pad, axis=0).astype(jnp.int32)
    gates = jnp.concatenate([x / total * ROUTED_SCALE for x in weights] + pad, axis=0)
    return experts, gates


def _post_kernel(*refs, n_x):
    (mod_ref, oa_ref, od_ref, om_ref, gpre_ref, wgl_ref, wa_ref, wb_ref, wc_ref, wo_ref, gpost_ref,
     gffn_ref, rwt_ref, rb_ref, xmid_o, h2_o, h2t_o, experts_o, gates_o) = refs[n_x:]

    @pl.when(pl.program_id(0) < N_TILES)
    def _():
        x = _load_x(refs[:n_x])
        m = mod_ref[0]
        d = D_MODEL
        sh1, sc1, g1, sh2, sc2 = (m[:, 0:d], m[:, d:2 * d], m[:, 2 * d:3 * d], m[:, 3 * d:4 * d],
                                  m[:, 4 * d:5 * d])
        h = _rms(x, gpre_ref[...]) * (1.0 + sc1) + sh1
        gate = jax.nn.sigmoid(_dot(h.astype(BF16), wgl_ref[...]))
        merged = (gate[:, 0:d] * _dot(oa_ref[...], wa_ref[...])
                  + gate[:, d:2 * d] * _dot(od_ref[...], wb_ref[...])
                  + gate[:, 2 * d:3 * d] * _dot(om_ref[...], wc_ref[...]))
        a = _dot(merged.astype(BF16), wo_ref[...])
        xm = x + g1 * _rms(a, gpost_ref[...])
        xmid_o[...] = xm
        h2f = _rms(xm, gffn_ref[...]) * (1.0 + sc2) + sh2
        h2 = h2f.astype(BF16)
        h2_o[...] = h2
        for c in range(ACC_ROWS):
            h2t_o[pl.ds(c, x.shape[0], stride=ACC_ROWS), :] = h2f[:, c * LANES:(c + 1) * LANES]
        scores = jax.nn.sigmoid(_dot_nt(rwt_ref[...], h2))
        experts, gates = _route(scores, scores + rb_ref[...])
        experts_o[...] = experts
        gates_o[...] = gates

    @pl.when(pl.program_id(0) == N_TILES)
    def _():
        h2t_o[...] = jnp.zeros_like(h2t_o)


def _post_attention(x, mod, oa, od, om, wts, layer):
    tm = ROW_TILE

    def row(i):
        return (_tile(i), 0)

    weights = [wts[k] for k in ("g_pre", "wgl", "wa", "wb", "wc", "wo", "g_post", "g_ffn", "rwt", "rb")]
    xs = list(x) if isinstance(x, tuple) else [x]
    return pl.pallas_call(
        functools.partial(_post_kernel, n_x=len(xs)),
        out_shape=[jax.ShapeDtypeStruct((N_TOK, D_MODEL), F32),
                   jax.ShapeDtypeStruct((N_TOK, D_MODEL), BF16),
                   jax.ShapeDtypeStruct(((N_TOK + tm) * ACC_ROWS, LANES), F32),
                   jax.ShapeDtypeStruct((8, N_TOK), jnp.int32),
                   jax.ShapeDtypeStruct((8, N_TOK), F32)],
        grid=(N_TILES + 1,),
        in_specs=_x_specs(x) + [_mod_spec(layer),
                  pl.BlockSpec((tm, 512), row), pl.BlockSpec((tm, 512), row),
                  pl.BlockSpec((tm, 512), row)]
                 + [_layer_spec(w, layer, 1) for w in weights],
        out_specs=[pl.BlockSpec((tm, D_MODEL), row), pl.BlockSpec((tm, D_MODEL), row),
                   pl.BlockSpec((tm * ACC_ROWS, LANES), lambda i: (i, 0)),
                   pl.BlockSpec((8, tm), lambda i: (0, _tile(i))),
                   pl.BlockSpec((8, tm), lambda i: (0, _tile(i)))],
        compiler_params=pltpu.CompilerParams(
            dimension_semantics=("arbitrary",), vmem_limit_bytes=VMEM_LIMIT),
        name="post_attention",
    )(*xs, mod, oa, od, om, *weights)


def _ffn(x, wg, wu):
    return _silu(_dot(x, wg)) * _dot(x, wu)


def _moe_kernel(start_ref, count_ref, dst_ref, gate_ref, xs_ref, wg_ref, wu_ref, wd_ref,
                acc_ref, xa, xb, ya, yb, wgb, wub, wdb, pend):
    e = pl.program_id(0)

    @pl.when(e == 0)
    def _():
        def clear(i, carry):
            r0 = pl.multiple_of(i * MOE_ZERO_ROWS, MOE_ZERO_ROWS)
            acc_ref[pl.ds(r0, MOE_ZERO_ROWS), :] = jnp.zeros((MOE_ZERO_ROWS, LANES), F32)
            return carry

        lax.fori_loop(0, acc_ref.shape[0] // MOE_ZERO_ROWS, clear, 0)

        yb[...] = jnp.zeros_like(yb)
        pend[0] = MOE_LIST_PAD_BASE

    wgb[...] = wg_ref[0].astype(BF16)
    wub[...] = wu_ref[0].astype(BF16)
    wdb[...] = wd_ref[0].astype(BF16)
    start = start_ref[e]
    pairs = (count_ref[e] + 2 * MOE_SUB - 1) // (2 * MOE_SUB)

    def gather(base, xbuf):
        for mi in range(MOE_SUB):
            src = pl.multiple_of(dst_ref[base + mi], ACC_ROWS)
            xbuf[mi * ACC_ROWS:(mi + 1) * ACC_ROWS, :] = xs_ref[pl.ds(src, ACC_ROWS), :]

    def expert_ffn(xbuf, ybuf):
        x = jnp.concatenate([xbuf[pl.ds(c, MOE_SUB, stride=ACC_ROWS), :].astype(BF16)
                             for c in range(ACC_ROWS)], axis=1)
        hid = _ffn(x, wgb[...], wub[...])
        y = _dot(hid.astype(BF16), wdb[...])
        for c in range(ACC_ROWS):
            ybuf[pl.ds(c, MOE_SUB, stride=ACC_ROWS), :] = y[:, c * LANES:(c + 1) * LANES]

    def scatter(base, ybuf):
        for g0 in range(0, MOE_SUB, MOE_RMW_GROUP):
            updates = []
            for mi in range(g0, g0 + MOE_RMW_GROUP):
                dst = pl.multiple_of(dst_ref[base + mi], ACC_ROWS)
                yv = ybuf[mi * ACC_ROWS:(mi + 1) * ACC_ROWS, :]
                updates.append((dst, acc_ref[pl.ds(dst, ACC_ROWS), :] + gate_ref[base + mi] * yv))
            for dst, val in updates:
                acc_ref[pl.ds(dst, ACC_ROWS), :] = val

    start_next = start_ref[jnp.minimum(e + 1, N_EXPERTS - 1)]
    prev_count = count_ref[jnp.maximum(e - 1, 0)]

    @pl.when((pairs > 0) & ((e == 0) | (prev_count == 0)))
    def _():
        gather(start, xa)

    def pair(p, carry):
        base = start + p * (2 * MOE_SUB)
        gather(base + MOE_SUB, xb)
        expert_ffn(xa, ya)
        scatter(pend[0], yb)
        gather(jnp.where(p + 1 < pairs, base + 2 * MOE_SUB, start_next), xa)
        expert_ffn(xb, yb)
        scatter(base, ya)
        pend[0] = base + MOE_SUB
        return carry

    lax.fori_loop(0, pairs, pair, 0)

    @pl.when(e == N_EXPERTS - 1)
    def _():
        scatter(pend[0], yb)


def _moe(lists, h2t, wg, wu, wd, layer):
    def expert(e, *_):
        return (layer, e, 0, 0)

    def whole(e, *_):
        return (0, 0)

    acc_rows = (N_TOK + MOE_SPARE_TOKENS) * ACC_ROWS
    assert acc_rows % MOE_ZERO_ROWS == 0
    return pl.pallas_call(
        _moe_kernel,
        out_shape=jax.ShapeDtypeStruct((acc_rows, LANES), F32),
        grid_spec=pltpu.PrefetchScalarGridSpec(
            num_scalar_prefetch=len(lists),
            grid=(N_EXPERTS,),
            in_specs=[pl.BlockSpec((acc_rows, LANES), whole, pipeline_mode=pl.Buffered(1)),
                      pl.BlockSpec((None, 1, D_MODEL, F_EXPERT), expert),
                      pl.BlockSpec((None, 1, D_MODEL, F_EXPERT), expert),
                      pl.BlockSpec((None, 1, F_EXPERT, D_MODEL), expert)],
            out_specs=pl.BlockSpec((acc_rows, LANES), whole, pipeline_mode=pl.Buffered(1)),
            scratch_shapes=[pltpu.VMEM((MOE_SUB * ACC_ROWS, LANES), F32)] * 4
                           + [pltpu.VMEM((D_MODEL, F_EXPERT), BF16),
                              pltpu.VMEM((D_MODEL, F_EXPERT), BF16),
                              pltpu.VMEM((F_EXPERT, D_MODEL), BF16),
                              pltpu.SMEM((1,), jnp.int32)]),
        compiler_params=pltpu.CompilerParams(
            dimension_semantics=("arbitrary",), vmem_limit_bytes=MOE_VMEM_LIMIT),
        name="moe",
    )(*lists, h2t, wg, wu, wd)


def _final_kernel(acc_ref, h2_ref, xmid_ref, mod_ref, swg_ref, swu_ref, swd_ref, gpost_ref, *out_refs):
    tm = h2_ref.shape[0]
    routed = jnp.concatenate(
        [acc_ref[pl.ds(c, tm, stride=ACC_ROWS), :] for c in range(ACC_ROWS)], axis=1)
    hid = _ffn(h2_ref[...], swg_ref[...], swu_ref[...])
    shared = _dot(hid.astype(BF16), swd_ref[...])
    g2 = mod_ref[0][:, 5 * D_MODEL:6 * D_MODEL]
    out = xmid_ref[...] + g2 * _rms(routed + shared, gpost_ref[...])
    if len(out_refs) == 1:
        out_refs[0][...] = out
    else:
        @pl.when(pl.program_id(0) < CTX_TILES)
        def _():
            out_refs[0][...] = out

        @pl.when(pl.program_id(0) >= CTX_TILES)
        def _():
            out_refs[1][...] = out


def _final(acc, h2, xmid, mod, wts, layer, split):
    tm = ROW_TILE

    def row(i):
        return (i, 0)

    weights = [wts[k] for k in ("swg", "swu", "swd", "g_ffn_post")]
    return pl.pallas_call(
        _final_kernel,
        out_shape=([jax.ShapeDtypeStruct((N_CTX, D_MODEL), F32),
                    jax.ShapeDtypeStruct((N_LAT, D_MODEL), F32)] if split
                   else jax.ShapeDtypeStruct((N_TOK, D_MODEL), F32)),
        grid=(N_TOK // tm,),
        in_specs=[pl.BlockSpec((tm * ACC_ROWS, LANES), row),
                  pl.BlockSpec((tm, D_MODEL), row),
                  pl.BlockSpec((tm, D_MODEL), row),
                  _mod_spec(layer)]
                 + [_layer_spec(w, layer, 1) for w in weights],
        out_specs=_x_specs((None, None)) if split else pl.BlockSpec((tm, D_MODEL), row),
        compiler_params=pltpu.CompilerParams(
            dimension_semantics=("arbitrary",), vmem_limit_bytes=VMEM_LIMIT),
        name="ffn_final",
    )(acc, h2, xmid, mod, *weights)


def _routing_lists(experts, gates):
    e = experts[:TOP_K].reshape(-1)
    tok = jnp.tile(jnp.arange(N_TOK, dtype=jnp.int32), TOP_K)
    pad_e = jnp.repeat(jnp.arange(N_EXPERTS, dtype=jnp.int32), MOE_PAD)
    pad_tok = N_TOK + jnp.tile(jnp.arange(MOE_PAD, dtype=jnp.int32), N_EXPERTS)
    keys = jnp.concatenate([e * TOK_KEY + tok, pad_e * TOK_KEY + pad_tok])
    vals = jnp.concatenate([gates[:TOP_K].reshape(-1), jnp.zeros((N_EXPERTS * MOE_PAD,), F32)])
    keys, gate_sorted = lax.sort((keys, vals), num_keys=1)
    tail = jnp.full((MOE_SUB,), N_TOK, jnp.int32)
    tok_sorted = jnp.concatenate([keys & (TOK_KEY - 1), tail])
    gate_sorted = jnp.concatenate([gate_sorted, jnp.zeros((MOE_SUB,), F32)])
    dst = jnp.minimum(tok_sorted, N_TOK) * ACC_ROWS
    counts = jnp.sum((e[None, :] == jnp.arange(N_EXPERTS, dtype=jnp.int32)[:, None]).astype(jnp.int32),
                     axis=1)
    start = jnp.cumsum(counts) - counts + MOE_PAD * jnp.arange(N_EXPERTS, dtype=jnp.int32)
    return start.astype(jnp.int32), counts, dst, gate_sorted


def _rope_tables():
    t = np.arange(LAT_SEQ)
    pos = np.stack([t // GRID_W, t % GRID_W], axis=1).astype(np.float64)

    def table(half):
        inv = ROPE_BASE ** (-np.arange(half, dtype=np.float64) / half)
        lane = np.arange(LANES)
        axis = (lane // (2 * half)) % 2
        freq = inv[lane % half]
        ang = pos[:, axis] * freq[None, :]
        sign = np.where((lane % (2 * half)) < half, -1.0, 1.0)
        cos = np.concatenate([np.ones((ROW_TILE, LANES)), np.cos(ang)], axis=0)
        sin = np.concatenate([np.zeros((ROW_TILE, LANES)), np.sin(ang) * sign[None, :]], axis=0)
        return jnp.asarray(cos, F32), jnp.asarray(sin, F32)

    cos64, sin64 = table(16)
    cos32, sin32 = table(8)
    return cos64, sin64, cos32, sin32


def _prepare_weights(norm_attn_pre, norm_attn_post, norm_ffn_pre, norm_ffn_post, w_in, dif_subln,
                     mla_q_norm, mla_w_uq, mla_kv_norm, mla_w_ukv, w_branch_a, w_branch_b,
                     w_branch_c, w_out, router_w, router_bias, shared_w_gate, shared_w_up,
                     shared_w_down):
    dp = DEPTH
    qa = w_in[:, :, 0:512].reshape(dp, D_MODEL, 2, 4, 64).transpose(0, 1, 3, 2, 4).reshape(dp, D_MODEL, 512)
    kpe = w_in[:, :, 2688:2720]
    w1 = jnp.concatenate([qa, w_in[:, :, 512:2688], kpe, kpe, kpe, kpe], axis=2).astype(BF16)
    uq = mla_w_uq.reshape(dp, 256, 8, 96)
    wuq = jnp.concatenate([uq[..., :64].reshape(dp, 256, 512), uq[..., 64:].reshape(dp, 256, 256)],
                          axis=2).astype(BF16)
    ukv = mla_w_ukv.reshape(dp, 128, 8, 128)
    wukv = jnp.concatenate([ukv[..., :64].reshape(dp, 128, 512), ukv[..., 64:].reshape(dp, 128, 512)],
                           axis=2).astype(BF16)
    wa = w_branch_a.reshape(dp, 2, 4, 64, D_MODEL).transpose(0, 2, 1, 3, 4).reshape(dp, 512, D_MODEL)
    return dict(
        g_pre=norm_attn_pre[:, None, :], g_post=norm_attn_post[:, None, :],
        g_ffn=norm_ffn_pre[:, None, :], g_ffn_post=norm_ffn_post[:, None, :],
        w1=w1, wgl=w_in[:, :, 2720:].astype(BF16),
        qnorm=mla_q_norm[:, None, :], wuq=wuq, kvnorm=mla_kv_norm[:, None, :], wukv=wukv,
        subln=dif_subln[:, None, :],
        wa=wa.astype(BF16), wb=w_branch_b.astype(BF16), wc=w_branch_c.astype(BF16),
        wo=w_out.astype(BF16), rwt=router_w.transpose(0, 2, 1).astype(BF16),
        rb=router_bias[:, :, None],
        swg=shared_w_gate.astype(BF16), swu=shared_w_up.astype(BF16), swd=shared_w_down.astype(BF16))


def kernel(x_prompt, x_sample, cache_swa_k, cache_swa_v, cache_dif_k, cache_dif_v, cache_mla_ckv, cache_mla_kpe, c, c_ctx, w_mod, b_mod, norm_attn_pre, norm_attn_post, norm_ffn_pre, norm_ffn_post, w_in, swa_sink, dif_lq1, dif_lk1, dif_lq2, dif_lk2, dif_subln, mla_q_norm, mla_w_uq, mla_kv_norm, mla_w_ukv, w_branch_a, w_branch_b, w_branch_c, w_out, router_w, router_bias, moe_w_gate, moe_w_up, moe_w_down, shared_w_gate, shared_w_up, shared_w_down):
    x = (x_prompt.reshape(N_CTX, D_MODEL), x_sample.reshape(N_LAT, D_MODEL))
    cvec = jnp.concatenate([c_ctx[None, :], c, jnp.zeros((8 - 1 - N_LAT_BATCH, D_MODEL), F32)], axis=0)
    mod = _modulation(cvec, w_mod, b_mod).reshape(DEPTH, 8, 1, 6 * D_MODEL)
    tables = _rope_tables()
    wts = _prepare_weights(norm_attn_pre, norm_attn_post, norm_ffn_pre, norm_ffn_post, w_in,
                           dif_subln, mla_q_norm, mla_w_uq, mla_kv_norm, mla_w_ukv, w_branch_a,
                           w_branch_b, w_branch_c, w_out, router_w, router_bias, shared_w_gate,
                           shared_w_up, shared_w_down)
    caches = [cache_swa_k.reshape(N_LAT_BATCH, DEPTH, PAST, 128),
              cache_swa_v.reshape(N_LAT_BATCH, DEPTH, PAST, 128),
              cache_dif_k.reshape(N_LAT_BATCH, DEPTH, PAST, 512),
              cache_dif_v.reshape(N_LAT_BATCH, DEPTH, PAST, 512),
              cache_mla_ckv,
              jnp.tile(cache_mla_kpe, (1, 1, 1, 4))]
    lam_init = [0.8 - 0.6 * math.exp(-0.3 * l) for l in range(DEPTH)]
    lam = (jnp.exp(jnp.sum(dif_lq1 * dif_lk1, axis=1)) - jnp.exp(jnp.sum(dif_lq2 * dif_lk2, axis=1))
           + jnp.asarray(lam_init, F32))
    scal = jnp.concatenate([swa_sink, lam[:, None]], axis=1).astype(F32)

    states = None
    for l in range(DEPTH):
        proj = _in_projection(x, mod, wts, tables, l, states)
        states = proj[11:]
        ctx_out = _attention_ctx(scal[l], wts["subln"], proj, states, l, lam_init[l])
        oa, od, om = _attention_lat(scal[l], wts["subln"], wts["wukv"], proj, caches, ctx_out, l,
                                    lam_init[l])
        xmid, h2, h2t, experts, gates = _post_attention(x, mod, oa, od, om, wts, l)
        lists = _routing_lists(experts, gates)
        acc = _moe(lists, h2t, moe_w_gate, moe_w_up, moe_w_down, l)
        x = _final(acc, h2, xmid, mod, wts, l, split=(l == DEPTH - 1))

    y_p = x[0].reshape(N_CTX_BATCH, CTX_SEQ, D_MODEL)
    y_s = x[1].reshape(N_LAT_BATCH, LAT_SEQ, D_MODEL)
    s_ka, s_va, s_kd, s_vd, s_ckv, s_kpe = states
    return (y_p, y_s,
            s_ka.reshape(N_CTX_BATCH, DEPTH, CTX_SEQ, 2, 64),
            s_va.reshape(N_CTX_BATCH, DEPTH, CTX_SEQ, 2, 64),
            s_kd.reshape(N_CTX_BATCH, DEPTH, CTX_SEQ, 4, 2, 64),
            s_vd.reshape(N_CTX_BATCH, DEPTH, CTX_SEQ, 4, 128),
            s_ckv, s_kpe)
```

```python
import functools
import math

import numpy as np
import jax
import jax.numpy as jnp
from jax import lax
from jax.experimental import pallas as pl
from jax.experimental.pallas import tpu as pltpu

F32 = jnp.float32
BF16 = jnp.bfloat16

D_MODEL = 1024
N_CTX_BATCH, CTX_SEQ = 16, 256
N_LAT_BATCH, LAT_SEQ = 2, 1024
PAST = 256
N_CTX = N_CTX_BATCH * CTX_SEQ
N_LAT = N_LAT_BATCH * LAT_SEQ
N_TOK = N_CTX + N_LAT
DEPTH = 2
GRID_W = 64
WINDOW = 128
N_EXPERTS = 64
N_GROUPS = 8
TOPK_GROUPS = 4
TOP_K = 6
F_EXPERT = 256
ROUTED_SCALE = 2.5
ROPE_BASE = 10000.0
EPS = 1e-6
NEG = -1e30

LANES = 128
ROW_TILE = 256
Q_BLOCK = 256
VMEM_LIMIT = 56 * 1024 * 1024
ACC_ROWS = 8
MOE_SPARE_TOKENS = 8
MOE_ZERO_ROWS = 64
MOE_VMEM_LIMIT = 62 * 1024 * 1024
MOE_SUB = 128
MOE_RMW_GROUP = 4
MOE_PAD = 2 * MOE_SUB
TOK_KEY = 8192
MOE_LIST_PAD_BASE = N_TOK * TOP_K + N_EXPERTS * 2 * MOE_PAD

C_QA, C_KA, C_VA, C_QD, C_KD, C_VD, C_QC, C_KVC, C_KPE, C_END = (
    0, 512, 640, 768, 1280, 1792, 2304, 2560, 2688, 2816)
SCALE_64 = 1.0 / math.sqrt(64.0)
SCALE_96 = 1.0 / math.sqrt(96.0)

CTX_TILES = N_CTX // ROW_TILE
N_TILES = N_TOK // ROW_TILE
assert ROW_TILE == CTX_SEQ and LAT_SEQ % ROW_TILE == 0


def _mod_row_of_tile(i):
    return jnp.where(i < CTX_TILES, 0, 1 + (i - CTX_TILES) // (LAT_SEQ // ROW_TILE))


def _rms(x, g):
    return x * lax.rsqrt(jnp.mean(x * x, axis=-1, keepdims=True) + EPS) * g


def _dot(a, b):
    return jnp.dot(a, b, preferred_element_type=F32)


def _dot_nt(a, b):
    return lax.dot_general(a, b, (((1,), (1,)), ((), ())), preferred_element_type=F32)


def _silu(x):
    return x * jax.nn.sigmoid(x)


def _layer_spec(arr, layer, grid_rank):
    zeros = (0,) * (arr.ndim - 1)
    return pl.BlockSpec((None,) + arr.shape[1:], lambda *_: (layer,) + zeros)


def _mod_kernel(c_ref, w_ref, b_ref, o_ref):
    c = c_ref[...]
    o_ref[0] = _dot(_silu(c).astype(BF16), w_ref[0].astype(BF16)) + b_ref[0]


def _modulation(cvec, w_mod, b_mod):
    tn = 1536
    n = w_mod.shape[-1]
    return pl.pallas_call(
        _mod_kernel,
        out_shape=jax.ShapeDtypeStruct((DEPTH, 8, n), F32),
        grid=(DEPTH, n // tn),
        in_specs=[
            pl.BlockSpec((8, D_MODEL), lambda l, j: (0, 0)),
            pl.BlockSpec((1, D_MODEL, tn), lambda l, j: (l, 0, j)),
            pl.BlockSpec((1, 1, tn), lambda l, j: (l, 0, j)),
        ],
        out_specs=pl.BlockSpec((1, 8, tn), lambda l, j: (l, 0, j)),
        compiler_params=pltpu.CompilerParams(vmem_limit_bytes=VMEM_LIMIT),
        name="modulation",
    )(cvec, w_mod, b_mod.reshape(DEPTH, 1, n))


def _tile(i):
    return jnp.minimum(i, N_TILES - 1)


def _x_specs(x):
    if isinstance(x, tuple):
        return [pl.BlockSpec((ROW_TILE, D_MODEL), lambda i: (jnp.minimum(i, CTX_TILES - 1), 0)),
                pl.BlockSpec((ROW_TILE, D_MODEL), lambda i: (jnp.maximum(_tile(i) - CTX_TILES, 0), 0))]
    return [pl.BlockSpec((ROW_TILE, D_MODEL), lambda i: (_tile(i), 0))]


def _load_x(x_refs):
    if len(x_refs) == 1:
        return x_refs[0][...]
    return jnp.where(pl.program_id(0) < CTX_TILES, x_refs[0][...], x_refs[1][...])


def _mod_spec(layer):
    return pl.BlockSpec((None, 1, 1, 6 * D_MODEL),
                        lambda i: (layer, _mod_row_of_tile(_tile(i)), 0, 0))


def _rope128(x, cos, sin, half):
    lane = lax.broadcasted_iota(jnp.int32, x.shape, 1)
    first = (lane % (2 * half)) < half
    partner = jnp.where(first, pltpu.roll(x, LANES - half, 1), pltpu.roll(x, half, 1))
    return x * cos + partner * sin


def _rope_cols(x, cos, sin, half):
    chunks = [_rope128(x[:, c:c + LANES], cos, sin, half) for c in range(0, x.shape[1], LANES)]
    return chunks[0] if len(chunks) == 1 else jnp.concatenate(chunks, axis=1)


N_IN_INPUTS = 11


def _in_kernel(*refs, n_x):
    (mod_ref, g_ref, w1_ref, qn_ref, wuq_ref, kvn_ref, wukv_ref,
     cos64_ref, sin64_ref, cos32_ref, sin32_ref) = refs[n_x:n_x + N_IN_INPUTS]
    (qa_o, qd_o, qmn_o, qmr_o, kmn_o, vm_o, kpe4_o, ka_l, va_l, kd_l, vd_l,
     s_ka, s_va, s_kd, s_vd, s_ckv, s_kpe) = refs[-17:]
    i = pl.program_id(0)
    x = _load_x(refs[:n_x])
    m = mod_ref[0]
    sh1, sc1 = m[:, 0:D_MODEL], m[:, D_MODEL:2 * D_MODEL]
    h = _rms(x, g_ref[...]) * (1.0 + sc1) + sh1
    z = _dot(h.astype(BF16), w1_ref[...])
    cos64, sin64 = cos64_ref[...], sin64_ref[...]
    cos32, sin32 = cos32_ref[...], sin32_ref[...]
    qa_o[...] = (_rope_cols(z[:, C_QA:C_KA], cos64, sin64, 16) * SCALE_64).astype(BF16)
    qd_o[...] = (_rope_cols(z[:, C_QD:C_KD], cos64, sin64, 16) * SCALE_64).astype(BF16)
    ka = _rope_cols(z[:, C_KA:C_VA], cos64, sin64, 16)
    va = z[:, C_VA:C_QD]
    kd = _rope_cols(z[:, C_KD:C_VD], cos64, sin64, 16)
    vd = z[:, C_VD:C_QC]
    kpe4 = _rope_cols(z[:, C_KPE:C_END], cos32, sin32, 8)
    kpe4_o[...] = kpe4.astype(BF16)
    qn = _rms(z[:, C_QC:C_KVC], qn_ref[...])
    qm = _dot(qn.astype(BF16), wuq_ref[...]) * SCALE_96
    qmn_o[...] = qm[:, 0:512].astype(BF16)
    qmr_o[...] = _rope_cols(qm[:, 512:768], cos32, sin32, 8).astype(BF16)
    ckv = _rms(z[:, C_KVC:C_KPE], kvn_ref[...])
    kv = _dot(ckv.astype(BF16), wukv_ref[...])
    kmn_o[...] = kv[:, 0:512].astype(BF16)
    vm_o[...] = kv[:, 512:1024].astype(BF16)

    @pl.when(i < CTX_TILES)
    def _():
        s_ka[...] = ka
        s_va[...] = va
        s_kd[...] = kd
        s_vd[...] = vd
        s_ckv[...] = ckv
        s_kpe[...] = kpe4[:, 0:32]

    @pl.when(i >= CTX_TILES)
    def _():
        ka_l[...] = ka.astype(BF16)
        va_l[...] = va.astype(BF16)
        kd_l[...] = kd.astype(BF16)
        vd_l[...] = vd.astype(BF16)


STATE_WIDTHS = (128, 128, 512, 512, 128, 32)


def _in_projection(x, mod, wts, tables, layer, prev_states):
    tm = ROW_TILE

    def row(i):
        return (i, 0)

    def table_row(i):
        return (jnp.where(i < CTX_TILES, 0, 1 + (i - CTX_TILES) % (LAT_SEQ // tm)), 0)

    def lat_row(i):
        return (jnp.maximum(i - CTX_TILES, 0), 0)

    def state_block(i):
        return (jnp.minimum(i, CTX_TILES - 1), layer, 0, 0)

    table_spec = pl.BlockSpec((tm, LANES), table_row)
    weights = [wts["g_pre"], wts["w1"], wts["qnorm"], wts["wuq"], wts["kvnorm"], wts["wukv"]]
    all_rows = [(512, BF16)] * 3 + [(256, BF16), (512, BF16), (512, BF16), (128, BF16)]
    lat_rows = [(128, BF16), (128, BF16), (512, BF16), (512, BF16)]
    aliased = [] if prev_states is None else list(prev_states)
    xs = list(x) if isinstance(x, tuple) else [x]
    n_in = len(xs) + N_IN_INPUTS
    n_plain_out = len(all_rows) + len(lat_rows)
    return pl.pallas_call(
        functools.partial(_in_kernel, n_x=len(xs)),
        out_shape=[jax.ShapeDtypeStruct((N_TOK, w), dt) for w, dt in all_rows]
                  + [jax.ShapeDtypeStruct((N_LAT, w), dt) for w, dt in lat_rows]
                  + [jax.ShapeDtypeStruct((N_CTX_BATCH, DEPTH, CTX_SEQ, w), F32) for w in STATE_WIDTHS],
        grid=(N_TOK // tm,),
        in_specs=_x_specs(x) + [_mod_spec(layer)]
                 + [_layer_spec(w, layer, 1) for w in weights]
                 + [table_spec] * 4
                 + [pl.BlockSpec(memory_space=pl.ANY)] * len(aliased),
        out_specs=[pl.BlockSpec((tm, w), row) for w, _ in all_rows]
                  + [pl.BlockSpec((tm, w), lat_row) for w, _ in lat_rows]
                  + [pl.BlockSpec((None, None, CTX_SEQ, w), state_block) for w in STATE_WIDTHS],
        input_output_aliases={n_in + k: n_plain_out + k for k in range(len(aliased))},
        compiler_params=pltpu.CompilerParams(
            dimension_semantics=("arbitrary",), vmem_limit_bytes=VMEM_LIMIT),
        name="in_projection",
    )(*xs, mod, *weights, *tables, *aliased)


def _attend(q, parts, sink=None):
    scores = []
    for k, _, mask in parts:
        s = _dot_nt(q, k)
        if mask is not None:
            s = jnp.where(mask, s, NEG)
        scores.append(s)
    m = functools.reduce(jnp.maximum, [jnp.max(s, axis=-1, keepdims=True) for s in scores])
    if sink is not None:
        m = jnp.maximum(m, sink)
    den = None
    out = None
    for s, (_, v, _) in zip(scores, parts):
        e = jnp.exp(s - m)
        d = jnp.sum(e, axis=-1, keepdims=True)
        o = _dot(e.astype(BF16), v)
        den = d if den is None else den + d
        out = o if out is None else out + o
    if sink is not None:
        den = den + jnp.exp(sink - m)
    return out / den


A_HEADS = 8


def _attend_heads(qs, parts, sinks, stack):
    rows = qs[0].shape[0]
    if not stack:
        return [_attend(q, parts, None if sinks is None else sinks[i]) for i, q in enumerate(qs)]
    sink = None
    if sinks is not None:
        sink = jnp.concatenate([jnp.full((rows, 1), s, F32) for s in sinks], axis=0)
    o = _attend(jnp.concatenate(qs, axis=0), parts, sink)
    return [o[i * rows:(i + 1) * rows] for i in range(len(qs))]


def _three_mixers(scal_ref, subln_ref, q_blocks, kv_parts, oa_o, od_o, om_o, lam_init, stack):
    qa, qd, qmn, qmr = q_blocks
    rows = qa.shape[0]
    lane = lax.broadcasted_iota(jnp.int32, (rows, LANES), 1)
    low = lane < 64
    zero = jnp.zeros((rows, LANES), BF16)

    parts = [(p["ka"], p["va"], p.get("mask_a")) for p in kv_parts]
    qs = [jnp.where(low if kvh == 0 else ~low, qa[:, g * LANES:(g + 1) * LANES], zero)
          for kvh in range(2) for g in range(4)]
    out_a = _attend_heads(qs, parts, [scal_ref[h] for h in range(A_HEADS)], stack)
    for g in range(4):
        oa_o[:, g * LANES:(g + 1) * LANES] = jnp.where(low, out_a[g], out_a[4 + g]).astype(BF16)

    lam = scal_ref[8]
    subln = subln_ref[...]
    for h in range(4):
        sl = slice(h * LANES, (h + 1) * LANES)
        q128 = qd[:, sl]
        o1, o2 = _attend_heads([jnp.where(low, q128, zero), jnp.where(low, zero, q128)],
                               [(p["kd"][:, sl], p["vd"][:, sl], None) for p in kv_parts],
                               None, stack)
        od = o1 - lam * o2
        od_o[:, sl] = (_rms(od, subln) * (1.0 - lam_init)).astype(BF16)

    for i in range(4):
        sl = slice(i * LANES, (i + 1) * LANES)
        qs = []
        for sub in range(2):
            h = 2 * i + sub
            qn = jnp.where(low if sub == 0 else ~low, qmn[:, sl], zero)
            qr128 = qmr[:, (h // 4) * LANES:(h // 4 + 1) * LANES]
            qr = jnp.where((lane // 32) == (h % 4), qr128, zero)
            qs.append(jnp.concatenate([qn, qr], axis=1))
        parts = [(jnp.concatenate([p["kmn"][:, sl], p["kpe"]], axis=1), p["vm"][:, sl], None)
                 for p in kv_parts]
        o_even, o_odd = _attend_heads(qs, parts, None, stack)
        om_o[:, sl] = jnp.where(low, o_even, o_odd).astype(BF16)


def _attn_ctx_kernel(scal_ref, subln_ref, qa_ref, qd_ref, qmn_ref, qmr_ref,
                     ka_ref, va_ref, kd_ref, vd_ref, kmn_ref, vm_ref, kpe_ref,
                     oa_o, od_o, om_o, *, lam_init):
    part = dict(ka=ka_ref[...].astype(BF16), va=va_ref[...].astype(BF16),
                kd=kd_ref[...].astype(BF16), vd=vd_ref[...].astype(BF16),
                kmn=kmn_ref[...], vm=vm_ref[...], kpe=kpe_ref[...])
    _three_mixers(scal_ref, subln_ref, (qa_ref[...], qd_ref[...], qmn_ref[...], qmr_ref[...]),
                  [part], oa_o, od_o, om_o, lam_init, stack=False)


def _attn_lat_kernel(scal_ref, subln_ref, wukv_ref, qa_ref, qd_ref, qmn_ref, qmr_ref,
                     ka_ref, va_ref, kd_ref, vd_ref, kmn_ref, vm_ref, kpe_ref,
                     cka_ref, cva_ref, ckd_ref, cvd_ref, cckv_ref, ckpe_ref,
                     oa_in, od_in, om_in, oa_o, od_o, om_o, *, lam_init):
    del oa_in, od_in, om_in
    n = pl.program_id(1)
    kv_c = _dot(cckv_ref[...].astype(BF16), wukv_ref[...])
    cached = dict(ka=cka_ref[...].astype(BF16), va=cva_ref[...].astype(BF16),
                  kd=ckd_ref[...].astype(BF16), vd=cvd_ref[...].astype(BF16),
                  kmn=kv_c[:, 0:512].astype(BF16), vm=kv_c[:, 512:1024].astype(BF16),
                  kpe=ckpe_ref[...].astype(BF16))
    span = Q_BLOCK + 2 * WINDOW
    start = pl.multiple_of(jnp.clip(n * Q_BLOCK - WINDOW, 0, LAT_SEQ - span), WINDOW)
    qpos = n * Q_BLOCK + lax.broadcasted_iota(jnp.int32, (A_HEADS * Q_BLOCK, span), 0) % Q_BLOCK
    kpos = start + lax.broadcasted_iota(jnp.int32, (A_HEADS * Q_BLOCK, span), 1)
    mask_a = jnp.abs(qpos - kpos) <= WINDOW
    new = dict(ka=ka_ref[pl.ds(start, span), :], va=va_ref[pl.ds(start, span), :], mask_a=mask_a,
               kd=kd_ref[...], vd=vd_ref[...], kmn=kmn_ref[...], vm=vm_ref[...], kpe=kpe_ref[...])
    _three_mixers(scal_ref, subln_ref, (qa_ref[...], qd_ref[...], qmn_ref[...], qmr_ref[...]),
                  [cached, new], oa_o, od_o, om_o, lam_init, stack=True)


_SMEM_SPEC = pl.BlockSpec(memory_space=pltpu.SMEM)


def _attention_ctx(scal, subln, proj, states, layer, lam_init):
    qa, qd, qmn, qmr, kmn, vm, kpe4 = proj[:7]
    s_ka, s_va, s_kd, s_vd = states[:4]
    t = CTX_SEQ

    def blk(a):
        return pl.BlockSpec((t, a.shape[1]), lambda b: (b, 0))

    def sblk(a):
        return pl.BlockSpec((None, None, t, a.shape[-1]), lambda b: (b, layer, 0, 0))

    return pl.pallas_call(
        functools.partial(_attn_ctx_kernel, lam_init=lam_init),
        out_shape=[jax.ShapeDtypeStruct((N_TOK, 512), BF16)] * 3,
        grid=(N_CTX_BATCH,),
        in_specs=[_SMEM_SPEC, _layer_spec(subln, layer, 1)]
                 + [blk(a) for a in (qa, qd, qmn, qmr)]
                 + [sblk(a) for a in (s_ka, s_va, s_kd, s_vd)]
                 + [blk(a) for a in (kmn, vm, kpe4)],
        out_specs=[pl.BlockSpec((t, 512), lambda b: (b, 0))] * 3,
        compiler_params=pltpu.CompilerParams(vmem_limit_bytes=VMEM_LIMIT),
        name="attention_ctx",
    )(scal, subln, qa, qd, qmn, qmr, s_ka, s_va, s_kd, s_vd, kmn, vm, kpe4)


def _attention_lat(scal, subln, wukv, proj, caches, ctx_out, layer, lam_init):
    qa, qd, qmn, qmr, kmn, vm, kpe4, ka_l, va_l, kd_l, vd_l = proj[:11]
    nq = LAT_SEQ // Q_BLOCK
    q_off = N_CTX // Q_BLOCK
    kv_off = N_CTX // LAT_SEQ

    def qblk(a):
        return pl.BlockSpec((Q_BLOCK, a.shape[1]), lambda b, n: (q_off + b * nq + n, 0))

    def kvblk(a):
        return pl.BlockSpec((LAT_SEQ, a.shape[1]), lambda b, n: (b, 0))

    def kvblk_all(a):
        return pl.BlockSpec((LAT_SEQ, a.shape[1]), lambda b, n: (kv_off + b, 0))

    def cblk(a):
        return pl.BlockSpec((None, None, PAST, a.shape[-1]), lambda b, n: (b, layer, 0, 0))

    qs = [qa, qd, qmn, qmr]
    n_in = 3 + 4 + 7 + 6
    out_spec = pl.BlockSpec((Q_BLOCK, 512), lambda b, n: (q_off + b * nq + n, 0))
    return pl.pallas_call(
        functools.partial(_attn_lat_kernel, lam_init=lam_init),
        out_shape=[jax.ShapeDtypeStruct((N_TOK, 512), BF16)] * 3,
        grid=(N_LAT_BATCH, nq),
        in_specs=[_SMEM_SPEC, _layer_spec(subln, layer, 2), _layer_spec(wukv, layer, 2)]
                 + [qblk(a) for a in qs]
                 + [kvblk(a) for a in (ka_l, va_l, kd_l, vd_l)]
                 + [kvblk_all(a) for a in (kmn, vm, kpe4)]
                 + [cblk(a) for a in caches]
                 + [pl.BlockSpec(memory_space=pl.ANY)] * 3,
        out_specs=[out_spec] * 3,
        input_output_aliases={n_in + k: k for k in range(3)},
        compiler_params=pltpu.CompilerParams(vmem_limit_bytes=VMEM_LIMIT),
        name="attention_lat",
    )(scal, subln, wukv, *qs, ka_l, va_l, kd_l, vd_l, kmn, vm, kpe4, *caches, *ctx_out)


def _route(scores, biased):
    per = N_EXPERTS // N_GROUPS
    tm = scores.shape[1]
    sub = lax.broadcasted_iota(jnp.int32, (per, tm), 0).astype(F32)
    groups = [biased[g * per:(g + 1) * per, :] for g in range(N_GROUPS)]
    gscore = []
    for v in groups:
        m1 = jnp.max(v, axis=0, keepdims=True)
        first = jnp.min(jnp.where(v == m1, sub, float(per)), axis=0, keepdims=True)
        m2 = jnp.max(jnp.where(sub == first, -jnp.inf, v), axis=0, keepdims=True)
        gscore.append(m1 + m2)
    vals = []
    for g in range(N_GROUPS):
        rank = jnp.zeros((1, tm), F32)
        for o in range(N_GROUPS):
            if o == g:
                continue
            ahead = (gscore[o] >= gscore[g]) if o < g else (gscore[o] > gscore[g])
            rank = rank + jnp.where(ahead, 1.0, 0.0)
        vals.append(jnp.where(rank < TOPK_GROUPS, groups[g], NEG))
    idx = [sub + float(g * per) for g in range(N_GROUPS)]
    picks, weights = [], []
    for _ in range(TOP_K):
        best = functools.reduce(jnp.maximum, [jnp.max(v, axis=0, keepdims=True) for v in vals])
        cand = functools.reduce(jnp.minimum, [
            jnp.min(jnp.where(v == best, i, float(N_EXPERTS)), axis=0, keepdims=True)
            for v, i in zip(vals, idx)])
        wsel = jnp.zeros((1, tm), F32)
        for g in range(N_GROUPS):
            hit = idx[g] == cand
            wsel = wsel + jnp.sum(jnp.where(hit, scores[g * per:(g + 1) * per, :], 0.0),
                                  axis=0, keepdims=True)
            vals[g] = jnp.where(hit, -jnp.inf, vals[g])
        picks.append(cand)
        weights.append(wsel)
    total = functools.reduce(lambda a, b: a + b, weights)
    pad = [jnp.zeros((1, tm), F32)] * (8 - TOP_K)
    experts = jnp.concatenate(picks + pad, axis=0).astype(jnp.int32)
    gates = jnp.concatenate([x / total * ROUTED_SCALE for x in weights] + pad, axis=0)
    return experts, gates


def _post_kernel(*refs, n_x):
    (mod_ref, oa_ref, od_ref, om_ref, gpre_ref, wgl_ref, wa_ref, wb_ref, wc_ref, wo_ref, gpost_ref,
     gffn_ref, rwt_ref, rb_ref, xmid_o, h2_o, h2t_o, experts_o, gates_o) = refs[n_x:]

    @pl.when(pl.program_id(0) < N_TILES)
    def _():
        x = _load_x(refs[:n_x])
        m = mod_ref[0]
        d = D_MODEL
        sh1, sc1, g1, sh2, sc2 = (m[:, 0:d], m[:, d:2 * d], m[:, 2 * d:3 * d], m[:, 3 * d:4 * d],
                                  m[:, 4 * d:5 * d])
        h = _rms(x, gpre_ref[...]) * (1.0 + sc1) + sh1
        gate = jax.nn.sigmoid(_dot(h.astype(BF16), wgl_ref[...]))
        merged = (gate[:, 0:d] * _dot(oa_ref[...], wa_ref[...])
                  + gate[:, d:2 * d] * _dot(od_ref[...], wb_ref[...])
                  + gate[:, 2 * d:3 * d] * _dot(om_ref[...], wc_ref[...]))
        a = _dot(merged.astype(BF16), wo_ref[...])
        xm = x + g1 * _rms(a, gpost_ref[...])
        xmid_o[...] = xm
        h2f = _rms(xm, gffn_ref[...]) * (1.0 + sc2) + sh2
        h2 = h2f.astype(BF16)
        h2_o[...] = h2
        for c in range(ACC_ROWS):
            h2t_o[pl.ds(c, x.shape[0], stride=ACC_ROWS), :] = h2f[:, c * LANES:(c + 1) * LANES]
        scores = jax.nn.sigmoid(_dot_nt(rwt_ref[...], h2))
        experts, gates = _route(scores, scores + rb_ref[...])
        experts_o[...] = experts
        gates_o[...] = gates

    @pl.when(pl.program_id(0) == N_TILES)
    def _():
        h2t_o[...] = jnp.zeros_like(h2t_o)


def _post_attention(x, mod, oa, od, om, wts, layer):
    tm = ROW_TILE

    def row(i):
        return (_tile(i), 0)

    weights = [wts[k] for k in ("g_pre", "wgl", "wa", "wb", "wc", "wo", "g_post", "g_ffn", "rwt", "rb")]
    xs = list(x) if isinstance(x, tuple) else [x]
    return pl.pallas_call(
        functools.partial(_post_kernel, n_x=len(xs)),
        out_shape=[jax.ShapeDtypeStruct((N_TOK, D_MODEL), F32),
                   jax.ShapeDtypeStruct((N_TOK, D_MODEL), BF16),
                   jax.ShapeDtypeStruct(((N_TOK + tm) * ACC_ROWS, LANES), F32),
                   jax.ShapeDtypeStruct((8, N_TOK), jnp.int32),
                   jax.ShapeDtypeStruct((8, N_TOK), F32)],
        grid=(N_TILES + 1,),
        in_specs=_x_specs(x) + [_mod_spec(layer),
                  pl.BlockSpec((tm, 512), row), pl.BlockSpec((tm, 512), row),
                  pl.BlockSpec((tm, 512), row)]
                 + [_layer_spec(w, layer, 1) for w in weights],
        out_specs=[pl.BlockSpec((tm, D_MODEL), row), pl.BlockSpec((tm, D_MODEL), row),
                   pl.BlockSpec((tm * ACC_ROWS, LANES), lambda i: (i, 0)),
                   pl.BlockSpec((8, tm), lambda i: (0, _tile(i))),
                   pl.BlockSpec((8, tm), lambda i: (0, _tile(i)))],
        compiler_params=pltpu.CompilerParams(
            dimension_semantics=("arbitrary",), vmem_limit_bytes=VMEM_LIMIT),
        name="post_attention",
    )(*xs, mod, oa, od, om, *weights)


def _ffn(x, wg, wu):
    return _silu(_dot(x, wg)) * _dot(x, wu)


def _moe_kernel(start_ref, count_ref, dst_ref, xs_ref, gates_ref, wg_ref, wu_ref, wd_ref,
                acc_ref, xa, xb, ya, yb, wgb, wub, wdb, pend):
    e = pl.program_id(0)

    @pl.when(e == 0)
    def _():
        def clear(i, carry):
            r0 = pl.multiple_of(i * MOE_ZERO_ROWS, MOE_ZERO_ROWS)
            acc_ref[pl.ds(r0, MOE_ZERO_ROWS), :] = jnp.zeros((MOE_ZERO_ROWS, LANES), F32)
            return carry

        lax.fori_loop(0, acc_ref.shape[0] // MOE_ZERO_ROWS, clear, 0)

        yb[...] = jnp.zeros_like(yb)
        pend[0] = MOE_LIST_PAD_BASE

    wgb[...] = wg_ref[0].astype(BF16)
    wub[...] = wu_ref[0].astype(BF16)
    wdb[...] = wd_ref[0].astype(BF16)
    start = start_ref[e]
    pairs = (count_ref[e] + 2 * MOE_SUB - 1) // (2 * MOE_SUB)

    def gather(base, xbuf):
        for mi in range(MOE_SUB):
            src = pl.multiple_of(dst_ref[base + mi], ACC_ROWS)
            xbuf[mi * ACC_ROWS:(mi + 1) * ACC_ROWS, :] = xs_ref[pl.ds(src, ACC_ROWS), :]

    def expert_ffn(base, xbuf, ybuf):
        x = jnp.concatenate([xbuf[pl.ds(c, MOE_SUB, stride=ACC_ROWS), :].astype(BF16)
                             for c in range(ACC_ROWS)], axis=1)
        hid = _ffn(x, wgb[...], wub[...])
        y = _dot(hid.astype(BF16), wdb[...])
        r = base // MOE_SUB
        tile = gates_ref[pl.ds(pl.multiple_of((r // 8) * 8, 8), 8), :]
        sub = lax.broadcasted_iota(jnp.int32, tile.shape, 0)
        row = jnp.sum(jnp.where(sub == r % 8, tile, 0.0), axis=0, keepdims=True)
        gate = jnp.transpose(jnp.broadcast_to(row, (MOE_SUB, LANES)))
        for c in range(ACC_ROWS):
            ybuf[pl.ds(c, MOE_SUB, stride=ACC_ROWS), :] = y[:, c * LANES:(c + 1) * LANES] * gate

    def scatter(base, ybuf):
        for g0 in range(0, MOE_SUB, MOE_RMW_GROUP):
            updates = []
            for mi in range(g0, g0 + MOE_RMW_GROUP):
                dst = pl.multiple_of(dst_ref[base + mi], ACC_ROWS)
                yv = ybuf[mi * ACC_ROWS:(mi + 1) * ACC_ROWS, :]
                updates.append((dst, acc_ref[pl.ds(dst, ACC_ROWS), :] + yv))
            for dst, val in updates:
                acc_ref[pl.ds(dst, ACC_ROWS), :] = val

    start_next = start_ref[jnp.minimum(e + 1, N_EXPERTS - 1)]
    prev_count = count_ref[jnp.maximum(e - 1, 0)]

    @pl.when((pairs > 0) & ((e == 0) | (prev_count == 0)))
    def _():
        gather(start, xa)

    def pair(p, carry):
        base = start + p * (2 * MOE_SUB)
        gather(base + MOE_SUB, xb)
        expert_ffn(base, xa, ya)
        scatter(pend[0], yb)
        gather(jnp.where(p + 1 < pairs, base + 2 * MOE_SUB, start_next), xa)
        expert_ffn(base + MOE_SUB, xb, yb)
        scatter(base, ya)
        pend[0] = base + MOE_SUB
        return carry

    lax.fori_loop(0, pairs, pair, 0)

    @pl.when(e == N_EXPERTS - 1)
    def _():
        scatter(pend[0], yb)


def _moe(lists, gates2d, h2t, wg, wu, wd, layer):
    def expert(e, *_):
        return (layer, e, 0, 0)

    def whole(e, *_):
        return (0, 0)

    acc_rows = (N_TOK + MOE_SPARE_TOKENS) * ACC_ROWS
    assert acc_rows % MOE_ZERO_ROWS == 0
    return pl.pallas_call(
        _moe_kernel,
        out_shape=jax.ShapeDtypeStruct((acc_rows, LANES), F32),
        grid_spec=pltpu.PrefetchScalarGridSpec(
            num_scalar_prefetch=len(lists),
            grid=(N_EXPERTS,),
            in_specs=[pl.BlockSpec((acc_rows, LANES), whole, pipeline_mode=pl.Buffered(1)),
                      pl.BlockSpec(gates2d.shape, whole),
                      pl.BlockSpec((None, 1, D_MODEL, F_EXPERT), expert),
                      pl.BlockSpec((None, 1, D_MODEL, F_EXPERT), expert),
                      pl.BlockSpec((None, 1, F_EXPERT, D_MODEL), expert)],
            out_specs=pl.BlockSpec((acc_rows, LANES), whole, pipeline_mode=pl.Buffered(1)),
            scratch_shapes=[pltpu.VMEM((MOE_SUB * ACC_ROWS, LANES), F32)] * 4
                           + [pltpu.VMEM((D_MODEL, F_EXPERT), BF16),
                              pltpu.VMEM((D_MODEL, F_EXPERT), BF16),
                              pltpu.VMEM((F_EXPERT, D_MODEL), BF16),
                              pltpu.SMEM((1,), jnp.int32)]),
        compiler_params=pltpu.CompilerParams(
            dimension_semantics=("arbitrary",), vmem_limit_bytes=MOE_VMEM_LIMIT),
        name="moe",
    )(*lists, h2t, gates2d, wg, wu, wd)


def _final_kernel(acc_ref, h2_ref, xmid_ref, mod_ref, swg_ref, swu_ref, swd_ref, gpost_ref, *out_refs):
    tm = h2_ref.shape[0]
    routed = jnp.concatenate(
        [acc_ref[pl.ds(c, tm, stride=ACC_ROWS), :] for c in range(ACC_ROWS)], axis=1)
    hid = _ffn(h2_ref[...], swg_ref[...], swu_ref[...])
    shared = _dot(hid.astype(BF16), swd_ref[...])
    g2 = mod_ref[0][:, 5 * D_MODEL:6 * D_MODEL]
    out = xmid_ref[...] + g2 * _rms(routed + shared, gpost_ref[...])
    if len(out_refs) == 1:
        out_refs[0][...] = out
    else:
        @pl.when(pl.program_id(0) < CTX_TILES)
        def _():
            out_refs[0][...] = out

        @pl.when(pl.program_id(0) >= CTX_TILES)
        def _():
            out_refs[1][...] = out


def _final(acc, h2, xmid, mod, wts, layer, split):
    tm = ROW_TILE

    def row(i):
        return (i, 0)

    weights = [wts[k] for k in ("swg", "swu", "swd", "g_ffn_post")]
    return pl.pallas_call(
        _final_kernel,
        out_shape=([jax.ShapeDtypeStruct((N_CTX, D_MODEL), F32),
                    jax.ShapeDtypeStruct((N_LAT, D_MODEL), F32)] if split
                   else jax.ShapeDtypeStruct((N_TOK, D_MODEL), F32)),
        grid=(N_TOK // tm,),
        in_specs=[pl.BlockSpec((tm * ACC_ROWS, LANES), row),
                  pl.BlockSpec((tm, D_MODEL), row),
                  pl.BlockSpec((tm, D_MODEL), row),
                  _mod_spec(layer)]
                 + [_layer_spec(w, layer, 1) for w in weights],
        out_specs=_x_specs((None, None)) if split else pl.BlockSpec((tm, D_MODEL), row),
        compiler_params=pltpu.CompilerParams(
            dimension_semantics=("arbitrary",), vmem_limit_bytes=VMEM_LIMIT),
        name="ffn_final",
    )(acc, h2, xmid, mod, *weights)


def _routing_lists(experts, gates):
    e = experts[:TOP_K].reshape(-1)
    tok = jnp.tile(jnp.arange(N_TOK, dtype=jnp.int32), TOP_K)
    ids = jnp.arange(N_EXPERTS, dtype=jnp.int32)
    counts = jnp.sum((e[None, :] == ids[:, None]).astype(jnp.int32), axis=1)
    need = MOE_PAD + (-counts) % MOE_PAD
    cand = jnp.arange(2 * MOE_PAD, dtype=jnp.int32)
    pad_keys = jnp.where(cand[None, :] < need[:, None],
                         ids[:, None] * TOK_KEY + N_TOK + cand[None, :], N_EXPERTS * TOK_KEY)
    keys = jnp.concatenate([e * TOK_KEY + tok, pad_keys.reshape(-1)])
    vals = jnp.concatenate([gates[:TOP_K].reshape(-1), jnp.zeros((N_EXPERTS * 2 * MOE_PAD,), F32)])
    keys, gate_sorted = lax.sort((keys, vals), num_keys=1)
    tok_sorted = jnp.concatenate([keys & (TOK_KEY - 1), jnp.full((MOE_SUB,), N_TOK, jnp.int32)])
    gate_sorted = jnp.concatenate([gate_sorted, jnp.zeros((MOE_SUB,), F32)])
    dst = jnp.minimum(tok_sorted, N_TOK) * ACC_ROWS
    seg = counts + need
    start = (jnp.cumsum(seg) - seg).astype(jnp.int32)
    return (start, counts, dst), gate_sorted.reshape(-1, LANES)


def _rope_tables():
    t = np.arange(LAT_SEQ)
    pos = np.stack([t // GRID_W, t % GRID_W], axis=1).astype(np.float64)

    def table(half):
        inv = ROPE_BASE ** (-np.arange(half, dtype=np.float64) / half)
        lane = np.arange(LANES)
        axis = (lane // (2 * half)) % 2
        freq = inv[lane % half]
        ang = pos[:, axis] * freq[None, :]
        sign = np.where((lane % (2 * half)) < half, -1.0, 1.0)
        cos = np.concatenate([np.ones((ROW_TILE, LANES)), np.cos(ang)], axis=0)
        sin = np.concatenate([np.zeros((ROW_TILE, LANES)), np.sin(ang) * sign[None, :]], axis=0)
        return jnp.asarray(cos, F32), jnp.asarray(sin, F32)

    cos64, sin64 = table(16)
    cos32, sin32 = table(8)
    return cos64, sin64, cos32, sin32


def _prepare_weights(norm_attn_pre, norm_attn_post, norm_ffn_pre, norm_ffn_post, w_in, dif_subln,
                     mla_q_norm, mla_w_uq, mla_kv_norm, mla_w_ukv, w_branch_a, w_branch_b,
                     w_branch_c, w_out, router_w, router_bias, shared_w_gate, shared_w_up,
                     shared_w_down):
    dp = DEPTH
    qa = w_in[:, :, 0:512].reshape(dp, D_MODEL, 2, 4, 64).transpose(0, 1, 3, 2, 4).reshape(dp, D_MODEL, 512)
    kpe = w_in[:, :, 2688:2720]
    w1 = jnp.concatenate([qa, w_in[:, :, 512:2688], kpe, kpe, kpe, kpe], axis=2).astype(BF16)
    uq = mla_w_uq.reshape(dp, 256, 8, 96)
    wuq = jnp.concatenate([uq[..., :64].reshape(dp, 256, 512), uq[..., 64:].reshape(dp, 256, 256)],
                          axis=2).astype(BF16)
    ukv = mla_w_ukv.reshape(dp, 128, 8, 128)
    wukv = jnp.concatenate([ukv[..., :64].reshape(dp, 128, 512), ukv[..., 64:].reshape(dp, 128, 512)],
                           axis=2).astype(BF16)
    wa = w_branch_a.reshape(dp, 2, 4, 64, D_MODEL).transpose(0, 2, 1, 3, 4).reshape(dp, 512, D_MODEL)
    return dict(
        g_pre=norm_attn_pre[:, None, :], g_post=norm_attn_post[:, None, :],
        g_ffn=norm_ffn_pre[:, None, :], g_ffn_post=norm_ffn_post[:, None, :],
        w1=w1, wgl=w_in[:, :, 2720:].astype(BF16),
        qnorm=mla_q_norm[:, None, :], wuq=wuq, kvnorm=mla_kv_norm[:, None, :], wukv=wukv,
        subln=dif_subln[:, None, :],
        wa=wa.astype(BF16), wb=w_branch_b.astype(BF16), wc=w_branch_c.astype(BF16),
        wo=w_out.astype(BF16), rwt=router_w.transpose(0, 2, 1).astype(BF16),
        rb=router_bias[:, :, None],
        swg=shared_w_gate.astype(BF16), swu=shared_w_up.astype(BF16), swd=shared_w_down.astype(BF16))


def kernel(x_prompt, x_sample, cache_swa_k, cache_swa_v, cache_dif_k, cache_dif_v, cache_mla_ckv, cache_mla_kpe, c, c_ctx, w_mod, b_mod, norm_attn_pre, norm_attn_post, norm_ffn_pre, norm_ffn_post, w_in, swa_sink, dif_lq1, dif_lk1, dif_lq2, dif_lk2, dif_subln, mla_q_norm, mla_w_uq, mla_kv_norm, mla_w_ukv, w_branch_a, w_branch_b, w_branch_c, w_out, router_w, router_bias, moe_w_gate, moe_w_up, moe_w_down, shared_w_gate, shared_w_up, shared_w_down):
    x = (x_prompt.reshape(N_CTX, D_MODEL), x_sample.reshape(N_LAT, D_MODEL))
    cvec = jnp.concatenate([c_ctx[None, :], c, jnp.zeros((8 - 1 - N_LAT_BATCH, D_MODEL), F32)], axis=0)
    mod = _modulation(cvec, w_mod, b_mod).reshape(DEPTH, 8, 1, 6 * D_MODEL)
    tables = _rope_tables()
    wts = _prepare_weights(norm_attn_pre, norm_attn_post, norm_ffn_pre, norm_ffn_post, w_in,
                           dif_subln, mla_q_norm, mla_w_uq, mla_kv_norm, mla_w_ukv, w_branch_a,
                           w_branch_b, w_branch_c, w_out, router_w, router_bias, shared_w_gate,
                           shared_w_up, shared_w_down)
    caches = [cache_swa_k.reshape(N_LAT_BATCH, DEPTH, PAST, 128),
              cache_swa_v.reshape(N_LAT_BATCH, DEPTH, PAST, 128),
              cache_dif_k.reshape(N_LAT_BATCH, DEPTH, PAST, 512),
              cache_dif_v.reshape(N_LAT_BATCH, DEPTH, PAST, 512),
              cache_mla_ckv,
              jnp.tile(cache_mla_kpe, (1, 1, 1, 4))]
    lam_init = [0.8 - 0.6 * math.exp(-0.3 * l) for l in range(DEPTH)]
    lam = (jnp.exp(jnp.sum(dif_lq1 * dif_lk1, axis=1)) - jnp.exp(jnp.sum(dif_lq2 * dif_lk2, axis=1))
           + jnp.asarray(lam_init, F32))
    scal = jnp.concatenate([swa_sink, lam[:, None]], axis=1).astype(F32)

    states = None
    for l in range(DEPTH):
        proj = _in_projection(x, mod, wts, tables, l, states)
        states = proj[11:]
        ctx_out = _attention_ctx(scal[l], wts["subln"], proj, states, l, lam_init[l])
        oa, od, om = _attention_lat(scal[l], wts["subln"], wts["wukv"], proj, caches, ctx_out, l,
                                    lam_init[l])
        xmid, h2, h2t, experts, gates = _post_attention(x, mod, oa, od, om, wts, l)
        lists, gates2d = _routing_lists(experts, gates)
        acc = _moe(lists, gates2d, h2t, moe_w_gate, moe_w_up, moe_w_down, l)
        x = _final(acc, h2, xmid, mod, wts, l, split=(l == DEPTH - 1))

    y_p = x[0].reshape(N_CTX_BATCH, CTX_SEQ, D_MODEL)
    y_s = x[1].reshape(N_LAT_BATCH, LAT_SEQ, D_MODEL)
    s_ka, s_va, s_kd, s_vd, s_ckv, s_kpe = states
    return (y_p, y_s,
            s_ka.reshape(N_CTX_BATCH, DEPTH, CTX_SEQ, 2, 64),
            s_va.reshape(N_CTX_BATCH, DEPTH, CTX_SEQ, 2, 64),
            s_kd.reshape(N_CTX_BATCH, DEPTH, CTX_SEQ, 4, 2, 64),
            s_vd.reshape(N_CTX_BATCH, DEPTH, CTX_SEQ, 4, 128),
            s_ckv, s_kpe)
```

```python
import functools
import math

import numpy as np
import jax
import jax.numpy as jnp
from jax import lax
from jax.experimental import pallas as pl
from jax.experimental.pallas import tpu as pltpu

F32 = jnp.float32
BF16 = jnp.bfloat16

D_MODEL = 1024
N_CTX_BATCH, CTX_SEQ = 16, 256
N_LAT_BATCH, LAT_SEQ = 2, 1024
PAST = 256
N_CTX = N_CTX_BATCH * CTX_SEQ
N_LAT = N_LAT_BATCH * LAT_SEQ
N_TOK = N_CTX + N_LAT
DEPTH = 2
GRID_W = 64
WINDOW = 128
N_EXPERTS = 64
N_GROUPS = 8
TOPK_GROUPS = 4
TOP_K = 6
F_EXPERT = 256
ROUTED_SCALE = 2.5
ROPE_BASE = 10000.0
EPS = 1e-6
NEG = -1e30

LANES = 128
ROW_TILE = 256
Q_BLOCK = 256
VMEM_LIMIT = 56 * 1024 * 1024
ACC_ROWS = 8
MOE_SPARE_TOKENS = 8
MOE_ZERO_ROWS = 64
MOE_VMEM_LIMIT = 62 * 1024 * 1024
MOE_SUB = 128
MOE_RMW_GROUP = 2
MOE_PAD = 2 * MOE_SUB
TOK_KEY = 8192
MOE_LIST_PAD_BASE = N_TOK * TOP_K + N_EXPERTS * MOE_PAD

C_QA, C_KA, C_VA, C_QD, C_KD, C_VD, C_QC, C_KVC, C_KPE, C_END = (
    0, 512, 640, 768, 1280, 1792, 2304, 2560, 2688, 2816)
SCALE_64 = 1.0 / math.sqrt(64.0)
SCALE_96 = 1.0 / math.sqrt(96.0)

CTX_TILES = N_CTX // ROW_TILE
N_TILES = N_TOK // ROW_TILE
assert ROW_TILE == CTX_SEQ and LAT_SEQ % ROW_TILE == 0


def _mod_row_of_tile(i):
    return jnp.where(i < CTX_TILES, 0, 1 + (i - CTX_TILES) // (LAT_SEQ // ROW_TILE))


def _rms(x, g):
    return x * lax.rsqrt(jnp.mean(x * x, axis=-1, keepdims=True) + EPS) * g


def _dot(a, b):
    return jnp.dot(a, b, preferred_element_type=F32)


def _dot_nt(a, b):
    return lax.dot_general(a, b, (((1,), (1,)), ((), ())), preferred_element_type=F32)


def _silu(x):
    return x * jax.nn.sigmoid(x)


def _layer_spec(arr, layer, grid_rank):
    zeros = (0,) * (arr.ndim - 1)
    return pl.BlockSpec((None,) + arr.shape[1:], lambda *_: (layer,) + zeros)


def _mod_kernel(c_ref, w_ref, b_ref, o_ref):
    c = c_ref[...]
    o_ref[0] = _dot(_silu(c).astype(BF16), w_ref[0].astype(BF16)) + b_ref[0]


def _modulation(cvec, w_mod, b_mod):
    tn = 1536
    n = w_mod.shape[-1]
    return pl.pallas_call(
        _mod_kernel,
        out_shape=jax.ShapeDtypeStruct((DEPTH, 8, n), F32),
        grid=(DEPTH, n // tn),
        in_specs=[
            pl.BlockSpec((8, D_MODEL), lambda l, j: (0, 0)),
            pl.BlockSpec((1, D_MODEL, tn), lambda l, j: (l, 0, j)),
            pl.BlockSpec((1, 1, tn), lambda l, j: (l, 0, j)),
        ],
        out_specs=pl.BlockSpec((1, 8, tn), lambda l, j: (l, 0, j)),
        compiler_params=pltpu.CompilerParams(vmem_limit_bytes=VMEM_LIMIT),
        name="modulation",
    )(cvec, w_mod, b_mod.reshape(DEPTH, 1, n))


def _tile(i):
    return jnp.minimum(i, N_TILES - 1)


def _x_specs(x):
    if isinstance(x, tuple):
        return [pl.BlockSpec((ROW_TILE, D_MODEL), lambda i: (jnp.minimum(i, CTX_TILES - 1), 0)),
                pl.BlockSpec((ROW_TILE, D_MODEL), lambda i: (jnp.maximum(_tile(i) - CTX_TILES, 0), 0))]
    return [pl.BlockSpec((ROW_TILE, D_MODEL), lambda i: (_tile(i), 0))]


def _load_x(x_refs):
    if len(x_refs) == 1:
        return x_refs[0][...]
    return jnp.where(pl.program_id(0) < CTX_TILES, x_refs[0][...], x_refs[1][...])


def _mod_spec(layer):
    return pl.BlockSpec((None, 1, 1, 6 * D_MODEL),
                        lambda i: (layer, _mod_row_of_tile(_tile(i)), 0, 0))


def _rope128(x, cos, sin, half):
    lane = lax.broadcasted_iota(jnp.int32, x.shape, 1)
    first = (lane % (2 * half)) < half
    partner = jnp.where(first, pltpu.roll(x, LANES - half, 1), pltpu.roll(x, half, 1))
    return x * cos + partner * sin


def _rope_cols(x, cos, sin, half):
    chunks = [_rope128(x[:, c:c + LANES], cos, sin, half) for c in range(0, x.shape[1], LANES)]
    return chunks[0] if len(chunks) == 1 else jnp.concatenate(chunks, axis=1)


N_IN_INPUTS = 11


def _in_kernel(*refs, n_x):
    (mod_ref, g_ref, w1_ref, qn_ref, wuq_ref, kvn_ref, wukv_ref,
     cos64_ref, sin64_ref, cos32_ref, sin32_ref) = refs[n_x:n_x + N_IN_INPUTS]
    (qa_o, qd_o, qmn_o, qmr_o, kmn_o, vm_o, kpe4_o, ka_l, va_l, kd_l, vd_l,
     s_ka, s_va, s_kd, s_vd, s_ckv, s_kpe) = refs[-17:]
    i = pl.program_id(0)
    x = _load_x(refs[:n_x])
    m = mod_ref[0]
    sh1, sc1 = m[:, 0:D_MODEL], m[:, D_MODEL:2 * D_MODEL]
    h = _rms(x, g_ref[...]) * (1.0 + sc1) + sh1
    z = _dot(h.astype(BF16), w1_ref[...])
    cos64, sin64 = cos64_ref[...], sin64_ref[...]
    cos32, sin32 = cos32_ref[...], sin32_ref[...]
    qa_o[...] = (_rope_cols(z[:, C_QA:C_KA], cos64, sin64, 16) * SCALE_64).astype(BF16)
    qd_o[...] = (_rope_cols(z[:, C_QD:C_KD], cos64, sin64, 16) * SCALE_64).astype(BF16)
    ka = _rope_cols(z[:, C_KA:C_VA], cos64, sin64, 16)
    va = z[:, C_VA:C_QD]
    kd = _rope_cols(z[:, C_KD:C_VD], cos64, sin64, 16)
    vd = z[:, C_VD:C_QC]
    kpe4 = _rope_cols(z[:, C_KPE:C_END], cos32, sin32, 8)
    kpe4_o[...] = kpe4.astype(BF16)
    qn = _rms(z[:, C_QC:C_KVC], qn_ref[...])
    qm = _dot(qn.astype(BF16), wuq_ref[...]) * SCALE_96
    qmn_o[...] = qm[:, 0:512].astype(BF16)
    qmr_o[...] = _rope_cols(qm[:, 512:768], cos32, sin32, 8).astype(BF16)
    ckv = _rms(z[:, C_KVC:C_KPE], kvn_ref[...])
    kv = _dot(ckv.astype(BF16), wukv_ref[...])
    kmn_o[...] = kv[:, 0:512].astype(BF16)
    vm_o[...] = kv[:, 512:1024].astype(BF16)

    @pl.when(i < CTX_TILES)
    def _():
        s_ka[...] = ka
        s_va[...] = va
        s_kd[...] = kd
        s_vd[...] = vd
        s_ckv[...] = ckv
        s_kpe[...] = kpe4[:, 0:32]

    @pl.when(i >= CTX_TILES)
    def _():
        ka_l[...] = ka.astype(BF16)
        va_l[...] = va.astype(BF16)
        kd_l[...] = kd.astype(BF16)
        vd_l[...] = vd.astype(BF16)


STATE_WIDTHS = (128, 128, 512, 512, 128, 32)


def _in_projection(x, mod, wts, tables, layer, prev_states):
    tm = ROW_TILE

    def row(i):
        return (i, 0)

    def table_row(i):
        return (jnp.where(i < CTX_TILES, 0, 1 + (i - CTX_TILES) % (LAT_SEQ // tm)), 0)

    def lat_row(i):
        return (jnp.maximum(i - CTX_TILES, 0), 0)

    def state_block(i):
        return (jnp.minimum(i, CTX_TILES - 1), layer, 0, 0)

    table_spec = pl.BlockSpec((tm, LANES), table_row)
    weights = [wts["g_pre"], wts["w1"], wts["qnorm"], wts["wuq"], wts["kvnorm"], wts["wukv"]]
    all_rows = [(512, BF16)] * 3 + [(256, BF16), (512, BF16), (512, BF16), (128, BF16)]
    lat_rows = [(128, BF16), (128, BF16), (512, BF16), (512, BF16)]
    aliased = [] if prev_states is None else list(prev_states)
    xs = list(x) if isinstance(x, tuple) else [x]
    n_in = len(xs) + N_IN_INPUTS
    n_plain_out = len(all_rows) + len(lat_rows)
    return pl.pallas_call(
        functools.partial(_in_kernel, n_x=len(xs)),
        out_shape=[jax.ShapeDtypeStruct((N_TOK, w), dt) for w, dt in all_rows]
                  + [jax.ShapeDtypeStruct((N_LAT, w), dt) for w, dt in lat_rows]
                  + [jax.ShapeDtypeStruct((N_CTX_BATCH, DEPTH, CTX_SEQ, w), F32) for w in STATE_WIDTHS],
        grid=(N_TOK // tm,),
        in_specs=_x_specs(x) + [_mod_spec(layer)]
                 + [_layer_spec(w, layer, 1) for w in weights]
                 + [table_spec] * 4
                 + [pl.BlockSpec(memory_space=pl.ANY)] * len(aliased),
        out_specs=[pl.BlockSpec((tm, w), row) for w, _ in all_rows]
                  + [pl.BlockSpec((tm, w), lat_row) for w, _ in lat_rows]
                  + [pl.BlockSpec((None, None, CTX_SEQ, w), state_block) for w in STATE_WIDTHS],
        input_output_aliases={n_in + k: n_plain_out + k for k in range(len(aliased))},
        compiler_params=pltpu.CompilerParams(
            dimension_semantics=("arbitrary",), vmem_limit_bytes=VMEM_LIMIT),
        name="in_projection",
    )(*xs, mod, *weights, *tables, *aliased)


def _attend(q, parts, sink=None):
    scores = []
    for k, _, mask in parts:
        s = _dot_nt(q, k)
        if mask is not None:
            s = jnp.where(mask, s, NEG)
        scores.append(s)
    m = functools.reduce(jnp.maximum, [jnp.max(s, axis=-1, keepdims=True) for s in scores])
    if sink is not None:
        m = jnp.maximum(m, sink)
    den = None
    out = None
    for s, (_, v, _) in zip(scores, parts):
        e = jnp.exp(s - m)
        d = jnp.sum(e, axis=-1, keepdims=True)
        o = _dot(e.astype(BF16), v)
        den = d if den is None else den + d
        out = o if out is None else out + o
    if sink is not None:
        den = den + jnp.exp(sink - m)
    return out / den


A_HEADS = 8


def _attend_heads(qs, parts, sinks, stack):
    rows = qs[0].shape[0]
    if not stack:
        return [_attend(q, parts, None if sinks is None else sinks[i]) for i, q in enumerate(qs)]
    sink = None
    if sinks is not None:
        sink = jnp.concatenate([jnp.full((rows, 1), s, F32) for s in sinks], axis=0)
    o = _attend(jnp.concatenate(qs, axis=0), parts, sink)
    return [o[i * rows:(i + 1) * rows] for i in range(len(qs))]


def _three_mixers(scal_ref, subln_ref, q_blocks, kv_parts, oa_o, od_o, om_o, lam_init, stack):
    qa, qd, qmn, qmr = q_blocks
    rows = qa.shape[0]
    lane = lax.broadcasted_iota(jnp.int32, (rows, LANES), 1)
    low = lane < 64
    zero = jnp.zeros((rows, LANES), BF16)

    parts = [(p["ka"], p["va"], p.get("mask_a")) for p in kv_parts]
    qs = [jnp.where(low if kvh == 0 else ~low, qa[:, g * LANES:(g + 1) * LANES], zero)
          for kvh in range(2) for g in range(4)]
    out_a = _attend_heads(qs, parts, [scal_ref[h] for h in range(A_HEADS)], stack)
    for g in range(4):
        oa_o[:, g * LANES:(g + 1) * LANES] = jnp.where(low, out_a[g], out_a[4 + g]).astype(BF16)

    lam = scal_ref[8]
    subln = subln_ref[...]
    for h in range(4):
        sl = slice(h * LANES, (h + 1) * LANES)
        q128 = qd[:, sl]
        o1, o2 = _attend_heads([jnp.where(low, q128, zero), jnp.where(low, zero, q128)],
                               [(p["kd"][:, sl], p["vd"][:, sl], None) for p in kv_parts],
                               None, stack)
        od = o1 - lam * o2
        od_o[:, sl] = (_rms(od, subln) * (1.0 - lam_init)).astype(BF16)

    for i in range(4):
        sl = slice(i * LANES, (i + 1) * LANES)
        qs = []
        for sub in range(2):
            h = 2 * i + sub
            qn = jnp.where(low if sub == 0 else ~low, qmn[:, sl], zero)
            qr128 = qmr[:, (h // 4) * LANES:(h // 4 + 1) * LANES]
            qr = jnp.where((lane // 32) == (h % 4), qr128, zero)
            qs.append(jnp.concatenate([qn, qr], axis=1))
        parts = [(jnp.concatenate([p["kmn"][:, sl], p["kpe"]], axis=1), p["vm"][:, sl], None)
                 for p in kv_parts]
        o_even, o_odd = _attend_heads(qs, parts, None, stack)
        om_o[:, sl] = jnp.where(low, o_even, o_odd).astype(BF16)


def _attn_ctx_kernel(scal_ref, subln_ref, qa_ref, qd_ref, qmn_ref, qmr_ref,
                     ka_ref, va_ref, kd_ref, vd_ref, kmn_ref, vm_ref, kpe_ref,
                     oa_o, od_o, om_o, *, lam_init):
    part = dict(ka=ka_ref[...].astype(BF16), va=va_ref[...].astype(BF16),
                kd=kd_ref[...].astype(BF16), vd=vd_ref[...].astype(BF16),
                kmn=kmn_ref[...], vm=vm_ref[...], kpe=kpe_ref[...])
    _three_mixers(scal_ref, subln_ref, (qa_ref[...], qd_ref[...], qmn_ref[...], qmr_ref[...]),
                  [part], oa_o, od_o, om_o, lam_init, stack=False)


def _attn_lat_kernel(scal_ref, subln_ref, wukv_ref, qa_ref, qd_ref, qmn_ref, qmr_ref,
                     ka_ref, va_ref, kd_ref, vd_ref, kmn_ref, vm_ref, kpe_ref,
                     cka_ref, cva_ref, ckd_ref, cvd_ref, cckv_ref, ckpe_ref,
                     oa_in, od_in, om_in, oa_o, od_o, om_o, *, lam_init):
    del oa_in, od_in, om_in
    n = pl.program_id(1)
    kv_c = _dot(cckv_ref[...].astype(BF16), wukv_ref[...])
    cached = dict(ka=cka_ref[...].astype(BF16), va=cva_ref[...].astype(BF16),
                  kd=ckd_ref[...].astype(BF16), vd=cvd_ref[...].astype(BF16),
                  kmn=kv_c[:, 0:512].astype(BF16), vm=kv_c[:, 512:1024].astype(BF16),
                  kpe=ckpe_ref[...].astype(BF16))
    span = Q_BLOCK + 2 * WINDOW
    start = pl.multiple_of(jnp.clip(n * Q_BLOCK - WINDOW, 0, LAT_SEQ - span), WINDOW)
    qpos = n * Q_BLOCK + lax.broadcasted_iota(jnp.int32, (A_HEADS * Q_BLOCK, span), 0) % Q_BLOCK
    kpos = start + lax.broadcasted_iota(jnp.int32, (A_HEADS * Q_BLOCK, span), 1)
    mask_a = jnp.abs(qpos - kpos) <= WINDOW
    new = dict(ka=ka_ref[pl.ds(start, span), :], va=va_ref[pl.ds(start, span), :], mask_a=mask_a,
               kd=kd_ref[...], vd=vd_ref[...], kmn=kmn_ref[...], vm=vm_ref[...], kpe=kpe_ref[...])
    _three_mixers(scal_ref, subln_ref, (qa_ref[...], qd_ref[...], qmn_ref[...], qmr_ref[...]),
                  [cached, new], oa_o, od_o, om_o, lam_init, stack=True)


_SMEM_SPEC = pl.BlockSpec(memory_space=pltpu.SMEM)


def _attention_ctx(scal, subln, proj, states, layer, lam_init):
    qa, qd, qmn, qmr, kmn, vm, kpe4 = proj[:7]
    s_ka, s_va, s_kd, s_vd = states[:4]
    t = CTX_SEQ

    def blk(a):
        return pl.BlockSpec((t, a.shape[1]), lambda b: (b, 0))

    def sblk(a):
        return pl.BlockSpec((None, None, t, a.shape[-1]), lambda b: (b, layer, 0, 0))

    return pl.pallas_call(
        functools.partial(_attn_ctx_kernel, lam_init=lam_init),
        out_shape=[jax.ShapeDtypeStruct((N_TOK, 512), BF16)] * 3,
        grid=(N_CTX_BATCH,),
        in_specs=[_SMEM_SPEC, _layer_spec(subln, layer, 1)]
                 + [blk(a) for a in (qa, qd, qmn, qmr)]
                 + [sblk(a) for a in (s_ka, s_va, s_kd, s_vd)]
                 + [blk(a) for a in (kmn, vm, kpe4)],
        out_specs=[pl.BlockSpec((t, 512), lambda b: (b, 0))] * 3,
        compiler_params=pltpu.CompilerParams(vmem_limit_bytes=VMEM_LIMIT),
        name="attention_ctx",
    )(scal, subln, qa, qd, qmn, qmr, s_ka, s_va, s_kd, s_vd, kmn, vm, kpe4)


def _attention_lat(scal, subln, wukv, proj, caches, ctx_out, layer, lam_init):
    qa, qd, qmn, qmr, kmn, vm, kpe4, ka_l, va_l, kd_l, vd_l = proj[:11]
    nq = LAT_SEQ // Q_BLOCK
    q_off = N_CTX // Q_BLOCK
    kv_off = N_CTX // LAT_SEQ

    def qblk(a):
        return pl.BlockSpec((Q_BLOCK, a.shape[1]), lambda b, n: (q_off + b * nq + n, 0))

    def kvblk(a):
        return pl.BlockSpec((LAT_SEQ, a.shape[1]), lambda b, n: (b, 0))

    def kvblk_all(a):
        return pl.BlockSpec((LAT_SEQ, a.shape[1]), lambda b, n: (kv_off + b, 0))

    def cblk(a):
        return pl.BlockSpec((None, None, PAST, a.shape[-1]), lambda b, n: (b, layer, 0, 0))

    qs = [qa, qd, qmn, qmr]
    n_in = 3 + 4 + 7 + 6
    out_spec = pl.BlockSpec((Q_BLOCK, 512), lambda b, n: (q_off + b * nq + n, 0))
    return pl.pallas_call(
        functools.partial(_attn_lat_kernel, lam_init=lam_init),
        out_shape=[jax.ShapeDtypeStruct((N_TOK, 512), BF16)] * 3,
        grid=(N_LAT_BATCH, nq),
        in_specs=[_SMEM_SPEC, _layer_spec(subln, layer, 2), _layer_spec(wukv, layer, 2)]
                 + [qblk(a) for a in qs]
                 + [kvblk(a) for a in (ka_l, va_l, kd_l, vd_l)]
                 + [kvblk_all(a) for a in (kmn, vm, kpe4)]
                 + [cblk(a) for a in caches]
                 + [pl.BlockSpec(memory_space=pl.ANY)] * 3,
        out_specs=[out_spec] * 3,
        input_output_aliases={n_in + k: k for k in range(3)},
        compiler_params=pltpu.CompilerParams(vmem_limit_bytes=VMEM_LIMIT),
        name="attention_lat",
    )(scal, subln, wukv, *qs, ka_l, va_l, kd_l, vd_l, kmn, vm, kpe4, *caches, *ctx_out)


def _route(scores, biased):
    per = N_EXPERTS // N_GROUPS
    tm = scores.shape[1]
    sub = lax.broadcasted_iota(jnp.int32, (per, tm), 0).astype(F32)
    groups = [biased[g * per:(g + 1) * per, :] for g in range(N_GROUPS)]
    gscore = []
    for v in groups:
        m1 = jnp.max(v, axis=0, keepdims=True)
        first = jnp.min(jnp.where(v == m1, sub, float(per)), axis=0, keepdims=True)
        m2 = jnp.max(jnp.where(sub == first, -jnp.inf, v), axis=0, keepdims=True)
        gscore.append(m1 + m2)
    vals = []
    for g in range(N_GROUPS):
        rank = jnp.zeros((1, tm), F32)
        for o in range(N_GROUPS):
            if o == g:
                continue
            ahead = (gscore[o] >= gscore[g]) if o < g else (gscore[o] > gscore[g])
            rank = rank + jnp.where(ahead, 1.0, 0.0)
        vals.append(jnp.where(rank < TOPK_GROUPS, groups[g], NEG))
    idx = [sub + float(g * per) for g in range(N_GROUPS)]
    picks, weights = [], []
    for _ in range(TOP_K):
        best = functools.reduce(jnp.maximum, [jnp.max(v, axis=0, keepdims=True) for v in vals])
        cand = functools.reduce(jnp.minimum, [
            jnp.min(jnp.where(v == best, i, float(N_EXPERTS)), axis=0, keepdims=True)
            for v, i in zip(vals, idx)])
        wsel = jnp.zeros((1, tm), F32)
        for g in range(N_GROUPS):
            hit = idx[g] == cand
            wsel = wsel + jnp.sum(jnp.where(hit, scores[g * per:(g + 1) * per, :], 0.0),
                                  axis=0, keepdims=True)
            vals[g] = jnp.where(hit, -jnp.inf, vals[g])
        picks.append(cand)
        weights.append(wsel)
    total = functools.reduce(lambda a, b: a + b, weights)
    pad = [jnp.zeros((1, tm), F32)] * (8 - TOP_K)
    experts = jnp.concatenate(picks + pad, axis=0).astype(jnp.int32)
    gates = jnp.concatenate([x / total * ROUTED_SCALE for x in weights] + pad, axis=0)
    return experts, gates


def _post_kernel(*refs, n_x):
    (mod_ref, oa_ref, od_ref, om_ref, gpre_ref, wgl_ref, wa_ref, wb_ref, wc_ref, wo_ref, gpost_ref,
     gffn_ref, rwt_ref, rb_ref, xmid_o, h2_o, h2t_o, experts_o, gates_o) = refs[n_x:]

    @pl.when(pl.program_id(0) < N_TILES)
    def _():
        x = _load_x(refs[:n_x])
        m = mod_ref[0]
        d = D_MODEL
        sh1, sc1, g1, sh2, sc2 = (m[:, 0:d], m[:, d:2 * d], m[:, 2 * d:3 * d], m[:, 3 * d:4 * d],
                                  m[:, 4 * d:5 * d])
        h = _rms(x, gpre_ref[...]) * (1.0 + sc1) + sh1
        gate = jax.nn.sigmoid(_dot(h.astype(BF16), wgl_ref[...]))
        merged = (gate[:, 0:d] * _dot(oa_ref[...], wa_ref[...])
                  + gate[:, d:2 * d] * _dot(od_ref[...], wb_ref[...])
                  + gate[:, 2 * d:3 * d] * _dot(om_ref[...], wc_ref[...]))
        a = _dot(merged.astype(BF16), wo_ref[...])
        xm = x + g1 * _rms(a, gpost_ref[...])
        xmid_o[...] = xm
        h2f = _rms(xm, gffn_ref[...]) * (1.0 + sc2) + sh2
        h2 = h2f.astype(BF16)
        h2_o[...] = h2
        for c in range(ACC_ROWS):
            h2t_o[pl.ds(c, x.shape[0], stride=ACC_ROWS), :] = h2f[:, c * LANES:(c + 1) * LANES]
        scores = jax.nn.sigmoid(_dot_nt(rwt_ref[...], h2))
        experts, gates = _route(scores, scores + rb_ref[...])
        experts_o[...] = experts
        gates_o[...] = gates

    @pl.when(pl.program_id(0) == N_TILES)
    def _():
        h2t_o[...] = jnp.zeros_like(h2t_o)


def _post_attention(x, mod, oa, od, om, wts, layer):
    tm = ROW_TILE

    def row(i):
        return (_tile(i), 0)

    weights = [wts[k] for k in ("g_pre", "wgl", "wa", "wb", "wc", "wo", "g_post", "g_ffn", "rwt", "rb")]
    xs = list(x) if isinstance(x, tuple) else [x]
    return pl.pallas_call(
        functools.partial(_post_kernel, n_x=len(xs)),
        out_shape=[jax.ShapeDtypeStruct((N_TOK, D_MODEL), F32),
                   jax.ShapeDtypeStruct((N_TOK, D_MODEL), BF16),
                   jax.ShapeDtypeStruct(((N_TOK + tm) * ACC_ROWS, LANES), F32),
                   jax.ShapeDtypeStruct((8, N_TOK), jnp.int32),
                   jax.ShapeDtypeStruct((8, N_TOK), F32)],
        grid=(N_TILES + 1,),
        in_specs=_x_specs(x) + [_mod_spec(layer),
                  pl.BlockSpec((tm, 512), row), pl.BlockSpec((tm, 512), row),
                  pl.BlockSpec((tm, 512), row)]
                 + [_layer_spec(w, layer, 1) for w in weights],
        out_specs=[pl.BlockSpec((tm, D_MODEL), row), pl.BlockSpec((tm, D_MODEL), row),
                   pl.BlockSpec((tm * ACC_ROWS, LANES), lambda i: (i, 0)),
                   pl.BlockSpec((8, tm), lambda i: (0, _tile(i))),
                   pl.BlockSpec((8, tm), lambda i: (0, _tile(i)))],
        compiler_params=pltpu.CompilerParams(
            dimension_semantics=("arbitrary",), vmem_limit_bytes=VMEM_LIMIT),
        name="post_attention",
    )(*xs, mod, oa, od, om, *weights)


def _ffn(x, wg, wu):
    return _silu(_dot(x, wg)) * _dot(x, wu)


def _moe_kernel(start_ref, count_ref, dst_ref, gate_ref, xs_ref, wg_ref, wu_ref, wd_ref,
                acc_ref, xa, xb, ya, yb, wgb, wub, wdb, pend):
    e = pl.program_id(0)

    @pl.when(e == 0)
    def _():
        def clear(i, carry):
            r0 = pl.multiple_of(i * MOE_ZERO_ROWS, MOE_ZERO_ROWS)
            acc_ref[pl.ds(r0, MOE_ZERO_ROWS), :] = jnp.zeros((MOE_ZERO_ROWS, LANES), F32)
            return carry

        lax.fori_loop(0, acc_ref.shape[0] // MOE_ZERO_ROWS, clear, 0)

        yb[...] = jnp.zeros_like(yb)
        pend[0] = MOE_LIST_PAD_BASE

    wgb[...] = wg_ref[0].astype(BF16)
    wub[...] = wu_ref[0].astype(BF16)
    wdb[...] = wd_ref[0].astype(BF16)
    start = start_ref[e]
    pairs = (count_ref[e] + 2 * MOE_SUB - 1) // (2 * MOE_SUB)

    def gather(base, xbuf):
        for mi in range(MOE_SUB):
            src = pl.multiple_of(dst_ref[base + mi], ACC_ROWS)
            xbuf[mi * ACC_ROWS:(mi + 1) * ACC_ROWS, :] = xs_ref[pl.ds(src, ACC_ROWS), :]

    def expert_ffn(xbuf, ybuf):
        x = jnp.concatenate([xbuf[pl.ds(c, MOE_SUB, stride=ACC_ROWS), :].astype(BF16)
                             for c in range(ACC_ROWS)], axis=1)
        hid = _ffn(x, wgb[...], wub[...])
        y = _dot(hid.astype(BF16), wdb[...])
        for c in range(ACC_ROWS):
            ybuf[pl.ds(c, MOE_SUB, stride=ACC_ROWS), :] = y[:, c * LANES:(c + 1) * LANES]

    def scatter(base, ybuf):
        for g0 in range(0, MOE_SUB, MOE_RMW_GROUP):
            updates = []
            for mi in range(g0, g0 + MOE_RMW_GROUP):
                dst = pl.multiple_of(dst_ref[base + mi], ACC_ROWS)
                yv = ybuf[mi * ACC_ROWS:(mi + 1) * ACC_ROWS, :]
                updates.append((dst, acc_ref[pl.ds(dst, ACC_ROWS), :] + gate_ref[base + mi] * yv))
            for dst, val in updates:
                acc_ref[pl.ds(dst, ACC_ROWS), :] = val

    start_next = start_ref[jnp.minimum(e + 1, N_EXPERTS - 1)]
    prev_count = count_ref[jnp.maximum(e - 1, 0)]

    @pl.when((pairs > 0) & ((e == 0) | (prev_count == 0)))
    def _():
        gather(start, xa)

    def pair(p, carry):
        base = start + p * (2 * MOE_SUB)
        gather(base + MOE_SUB, xb)
        expert_ffn(xa, ya)
        scatter(pend[0], yb)
        gather(jnp.where(p + 1 < pairs, base + 2 * MOE_SUB, start_next), xa)
        expert_ffn(xb, yb)
        scatter(base, ya)
        pend[0] = base + MOE_SUB
        return carry

    lax.fori_loop(0, pairs, pair, 0)

    @pl.when(e == N_EXPERTS - 1)
    def _():
        scatter(pend[0], yb)


def _moe(lists, h2t, wg, wu, wd, layer):
    def expert(e, *_):
        return (layer, e, 0, 0)

    def whole(e, *_):
        return (0, 0)

    acc_rows = (N_TOK + MOE_SPARE_TOKENS) * ACC_ROWS
    assert acc_rows % MOE_ZERO_ROWS == 0
    return pl.pallas_call(
        _moe_kernel,
        out_shape=jax.ShapeDtypeStruct((acc_rows, LANES), F32),
        grid_spec=pltpu.PrefetchScalarGridSpec(
            num_scalar_prefetch=len(lists),
            grid=(N_EXPERTS,),
            in_specs=[pl.BlockSpec((acc_rows, LANES), whole, pipeline_mode=pl.Buffered(1)),
                      pl.BlockSpec((None, 1, D_MODEL, F_EXPERT), expert),
                      pl.BlockSpec((None, 1, D_MODEL, F_EXPERT), expert),
                      pl.BlockSpec((None, 1, F_EXPERT, D_MODEL), expert)],
            out_specs=pl.BlockSpec((acc_rows, LANES), whole, pipeline_mode=pl.Buffered(1)),
            scratch_shapes=[pltpu.VMEM((MOE_SUB * ACC_ROWS, LANES), F32)] * 4
                           + [pltpu.VMEM((D_MODEL, F_EXPERT), BF16),
                              pltpu.VMEM((D_MODEL, F_EXPERT), BF16),
                              pltpu.VMEM((F_EXPERT, D_MODEL), BF16),
                              pltpu.SMEM((1,), jnp.int32)]),
        compiler_params=pltpu.CompilerParams(
            dimension_semantics=("arbitrary",), vmem_limit_bytes=MOE_VMEM_LIMIT),
        name="moe",
    )(*lists, h2t, wg, wu, wd)


def _final_kernel(acc_ref, h2_ref, xmid_ref, mod_ref, swg_ref, swu_ref, swd_ref, gpost_ref, *out_refs):
    tm = h2_ref.shape[0]
    routed = jnp.concatenate(
        [acc_ref[pl.ds(c, tm, stride=ACC_ROWS), :] for c in range(ACC_ROWS)], axis=1)
    hid = _ffn(h2_ref[...], swg_ref[...], swu_ref[...])
    shared = _dot(hid.astype(BF16), swd_ref[...])
    g2 = mod_ref[0][:, 5 * D_MODEL:6 * D_MODEL]
    out = xmid_ref[...] + g2 * _rms(routed + shared, gpost_ref[...])
    if len(out_refs) == 1:
        out_refs[0][...] = out
    else:
        @pl.when(pl.program_id(0) < CTX_TILES)
        def _():
            out_refs[0][...] = out

        @pl.when(pl.program_id(0) >= CTX_TILES)
        def _():
            out_refs[1][...] = out


def _final(acc, h2, xmid, mod, wts, layer, split):
    tm = ROW_TILE

    def row(i):
        return (i, 0)

    weights = [wts[k] for k in ("swg", "swu", "swd", "g_ffn_post")]
    return pl.pallas_call(
        _final_kernel,
        out_shape=([jax.ShapeDtypeStruct((N_CTX, D_MODEL), F32),
                    jax.ShapeDtypeStruct((N_LAT, D_MODEL), F32)] if split
                   else jax.ShapeDtypeStruct((N_TOK, D_MODEL), F32)),
        grid=(N_TOK // tm,),
        in_specs=[pl.BlockSpec((tm * ACC_ROWS, LANES), row),
                  pl.BlockSpec((tm, D_MODEL), row),
                  pl.BlockSpec((tm, D_MODEL), row),
                  _mod_spec(layer)]
                 + [_layer_spec(w, layer, 1) for w in weights],
        out_specs=_x_specs((None, None)) if split else pl.BlockSpec((tm, D_MODEL), row),
        compiler_params=pltpu.CompilerParams(
            dimension_semantics=("arbitrary",), vmem_limit_bytes=VMEM_LIMIT),
        name="ffn_final",
    )(acc, h2, xmid, mod, *weights)


def _routing_lists(experts, gates):
    e = experts[:TOP_K].reshape(-1)
    tok = jnp.tile(jnp.arange(N_TOK, dtype=jnp.int32), TOP_K)
    pad_e = jnp.repeat(jnp.arange(N_EXPERTS, dtype=jnp.int32), MOE_PAD)
    pad_tok = N_TOK + jnp.tile(jnp.arange(MOE_PAD, dtype=jnp.int32), N_EXPERTS)
    keys = jnp.concatenate([e * TOK_KEY + tok, pad_e * TOK_KEY + pad_tok])
    vals = jnp.concatenate([gates[:TOP_K].reshape(-1), jnp.zeros((N_EXPERTS * MOE_PAD,), F32)])
    keys, gate_sorted = lax.sort((keys, vals), num_keys=1)
    tail = jnp.full((MOE_SUB,), N_TOK, jnp.int32)
    tok_sorted = jnp.concatenate([keys & (TOK_KEY - 1), tail])
    gate_sorted = jnp.concatenate([gate_sorted, jnp.zeros((MOE_SUB,), F32)])
    dst = jnp.minimum(tok_sorted, N_TOK) * ACC_ROWS
    counts = jnp.sum((e[None, :] == jnp.arange(N_EXPERTS, dtype=jnp.int32)[:, None]).astype(jnp.int32),
                     axis=1)
    start = jnp.cumsum(counts) - counts + MOE_PAD * jnp.arange(N_EXPERTS, dtype=jnp.int32)
    return start.astype(jnp.int32), counts, dst, gate_sorted


def _rope_tables():
    t = np.arange(LAT_SEQ)
    pos = np.stack([t // GRID_W, t % GRID_W], axis=1).astype(np.float64)

    def table(half):
        inv = ROPE_BASE ** (-np.arange(half, dtype=np.float64) / half)
        lane = np.arange(LANES)
        axis = (lane // (2 * half)) % 2
        freq = inv[lane % half]
        ang = pos[:, axis] * freq[None, :]
        sign = np.where((lane % (2 * half)) < half, -1.0, 1.0)
        cos = np.concatenate([np.ones((ROW_TILE, LANES)), np.cos(ang)], axis=0)
        sin = np.concatenate([np.zeros((ROW_TILE, LANES)), np.sin(ang) * sign[None, :]], axis=0)
        return jnp.asarray(cos, F32), jnp.asarray(sin, F32)

    cos64, sin64 = table(16)
    cos32, sin32 = table(8)
    return cos64, sin64, cos32, sin32


def _prepare_weights(norm_attn_pre, norm_attn_post, norm_ffn_pre, norm_ffn_post, w_in, dif_subln,
                     mla_q_norm, mla_w_uq, mla_kv_norm, mla_w_ukv, w_branch_a, w_branch_b,
                     w_branch_c, w_out, router_w, router_bias, shared_w_gate, shared_w_up,
                     shared_w_down):
    dp = DEPTH
    qa = w_in[:, :, 0:512].reshape(dp, D_MODEL, 2, 4, 64).transpose(0, 1, 3, 2, 4).reshape(dp, D_MODEL, 512)
    kpe = w_in[:, :, 2688:2720]
    w1 = jnp.concatenate([qa, w_in[:, :, 512:2688], kpe, kpe, kpe, kpe], axis=2).astype(BF16)
    uq = mla_w_uq.reshape(dp, 256, 8, 96)
    wuq = jnp.concatenate([uq[..., :64].reshape(dp, 256, 512), uq[..., 64:].reshape(dp, 256, 256)],
                          axis=2).astype(BF16)
    ukv = mla_w_ukv.reshape(dp, 128, 8, 128)
    wukv = jnp.concatenate([ukv[..., :64].reshape(dp, 128, 512), ukv[..., 64:].reshape(dp, 128, 512)],
                           axis=2).astype(BF16)
    wa = w_branch_a.reshape(dp, 2, 4, 64, D_MODEL).transpose(0, 2, 1, 3, 4).reshape(dp, 512, D_MODEL)
    return dict(
        g_pre=norm_attn_pre[:, None, :], g_post=norm_attn_post[:, None, :],
        g_ffn=norm_ffn_pre[:, None, :], g_ffn_post=norm_ffn_post[:, None, :],
        w1=w1, wgl=w_in[:, :, 2720:].astype(BF16),
        qnorm=mla_q_norm[:, None, :], wuq=wuq, kvnorm=mla_kv_norm[:, None, :], wukv=wukv,
        subln=dif_subln[:, None, :],
        wa=wa.astype(BF16), wb=w_branch_b.astype(BF16), wc=w_branch_c.astype(BF16),
        wo=w_out.astype(BF16), rwt=router_w.transpose(0, 2, 1).astype(BF16),
        rb=router_bias[:, :, None],
        swg=shared_w_gate.astype(BF16), swu=shared_w_up.astype(BF16), swd=shared_w_down.astype(BF16))


def kernel(x_prompt, x_sample, cache_swa_k, cache_swa_v, cache_dif_k, cache_dif_v, cache_mla_ckv, cache_mla_kpe, c, c_ctx, w_mod, b_mod, norm_attn_pre, norm_attn_post, norm_ffn_pre, norm_ffn_post, w_in, swa_sink, dif_lq1, dif_lk1, dif_lq2, dif_lk2, dif_subln, mla_q_norm, mla_w_uq, mla_kv_norm, mla_w_ukv, w_branch_a, w_branch_b, w_branch_c, w_out, router_w, router_bias, moe_w_gate, moe_w_up, moe_w_down, shared_w_gate, shared_w_up, shared_w_down):
    x = (x_prompt.reshape(N_CTX, D_MODEL), x_sample.reshape(N_LAT, D_MODEL))
    cvec = jnp.concatenate([c_ctx[None, :], c, jnp.zeros((8 - 1 - N_LAT_BATCH, D_MODEL), F32)], axis=0)
    mod = _modulation(cvec, w_mod, b_mod).reshape(DEPTH, 8, 1, 6 * D_MODEL)
    tables = _rope_tables()
    wts = _prepare_weights(norm_attn_pre, norm_attn_post, norm_ffn_pre, norm_ffn_post, w_in,
                           dif_subln, mla_q_norm, mla_w_uq, mla_kv_norm, mla_w_ukv, w_branch_a,
                           w_branch_b, w_branch_c, w_out, router_w, router_bias, shared_w_gate,
                           shared_w_up, shared_w_down)
    caches = [cache_swa_k.reshape(N_LAT_BATCH, DEPTH, PAST, 128),
              cache_swa_v.reshape(N_LAT_BATCH, DEPTH, PAST, 128),
              cache_dif_k.reshape(N_LAT_BATCH, DEPTH, PAST, 512),
              cache_dif_v.reshape(N_LAT_BATCH, DEPTH, PAST, 512),
              cache_mla_ckv,
              jnp.tile(cache_mla_kpe, (1, 1, 1, 4))]
    lam_init = [0.8 - 0.6 * math.exp(-0.3 * l) for l in range(DEPTH)]
    lam = (jnp.exp(jnp.sum(dif_lq1 * dif_lk1, axis=1)) - jnp.exp(jnp.sum(dif_lq2 * dif_lk2, axis=1))
           + jnp.asarray(lam_init, F32))
    scal = jnp.concatenate([swa_sink, lam[:, None]], axis=1).astype(F32)

    states = None
    for l in range(DEPTH):
        proj = _in_projection(x, mod, wts, tables, l, states)
        states = proj[11:]
        ctx_out = _attention_ctx(scal[l], wts["subln"], proj, states, l, lam_init[l])
        oa, od, om = _attention_lat(scal[l], wts["subln"], wts["wukv"], proj, caches, ctx_out, l,
                                    lam_init[l])
        xmid, h2, h2t, experts, gates = _post_attention(x, mod, oa, od, om, wts, l)
        lists = _routing_lists(experts, gates)
        acc = _moe(lists, h2t, moe_w_gate, moe_w_up, moe_w_down, l)
        x = _final(acc, h2, xmid, mod, wts, l, split=(l == DEPTH - 1))

    y_p = x[0].reshape(N_CTX_BATCH, CTX_SEQ, D_MODEL)
    y_s = x[1].reshape(N_LAT_BATCH, LAT_SEQ, D_MODEL)
    s_ka, s_va, s_kd, s_vd, s_ckv, s_kpe = states
    return (y_p, y_s,
            s_ka.reshape(N_CTX_BATCH, DEPTH, CTX_SEQ, 2, 64),
            s_va.reshape(N_CTX_BATCH, DEPTH, CTX_SEQ, 2, 64),
            s_kd.reshape(N_CTX_BATCH, DEPTH, CTX_SEQ, 4, 2, 64),
            s_vd.reshape(N_CTX_BATCH, DEPTH, CTX_SEQ, 4, 128),
            s_ckv, s_kpe)
```

```python
import functools
import math

import numpy as np
import jax
import jax.numpy as jnp
from jax import lax
from jax.experimental import pallas as pl
from jax.experimental.pallas import tpu as pltpu

F32 = jnp.float32
BF16 = jnp.bfloat16

D_MODEL = 1024
N_CTX_BATCH, CTX_SEQ = 16, 256
N_LAT_BATCH, LAT_SEQ = 2, 1024
PAST = 256
N_CTX = N_CTX_BATCH * CTX_SEQ
N_LAT = N_LAT_BATCH * LAT_SEQ
N_TOK = N_CTX + N_LAT
DEPTH = 2
GRID_W = 64
WINDOW = 128
N_EXPERTS = 64
N_GROUPS = 8
TOPK_GROUPS = 4
TOP_K = 6
F_EXPERT = 256
ROUTED_SCALE = 2.5
ROPE_BASE = 10000.0
EPS = 1e-6
NEG = -1e30

LANES = 128
ROW_TILE = 256
Q_BLOCK = 256
VMEM_LIMIT = 56 * 1024 * 1024
ACC_ROWS = 8
MOE_SPARE_TOKENS = 8
MOE_ZERO_ROWS = 64
MOE_VMEM_LIMIT = 62 * 1024 * 1024
MOE_SUB = 128
MOE_RMW_GROUP = 4
MOE_PAD = 2 * MOE_SUB
TOK_KEY = 8192
MOE_LIST_PAD_BASE = N_TOK * TOP_K + N_EXPERTS * MOE_PAD

C_QA, C_KA, C_VA, C_QD, C_KD, C_VD, C_QC, C_KVC, C_KPE, C_END = (
    0, 512, 640, 768, 1280, 1792, 2304, 2560, 2688, 2816)
SCALE_64 = 1.0 / math.sqrt(64.0)
SCALE_96 = 1.0 / math.sqrt(96.0)

CTX_TILES = N_CTX // ROW_TILE
N_TILES = N_TOK // ROW_TILE
assert ROW_TILE == CTX_SEQ and LAT_SEQ % ROW_TILE == 0


def _mod_row_of_tile(i):
    return jnp.where(i < CTX_TILES, 0, 1 + (i - CTX_TILES) // (LAT_SEQ // ROW_TILE))


def _rms(x, g):
    return x * lax.rsqrt(jnp.mean(x * x, axis=-1, keepdims=True) + EPS) * g


def _dot(a, b):
    return jnp.dot(a, b, preferred_element_type=F32)


def _dot_nt(a, b):
    return lax.dot_general(a, b, (((1,), (1,)), ((), ())), preferred_element_type=F32)


def _silu(x):
    return x * jax.nn.sigmoid(x)


def _layer_spec(arr, layer, grid_rank):
    zeros = (0,) * (arr.ndim - 1)
    return pl.BlockSpec((None,) + arr.shape[1:], lambda *_: (layer,) + zeros)


def _mod_kernel(c_ref, w_ref, b_ref, o_ref):
    c = c_ref[...]
    o_ref[0] = _dot(_silu(c).astype(BF16), w_ref[0].astype(BF16)) + b_ref[0]


def _modulation(cvec, w_mod, b_mod):
    tn = 1536
    n = w_mod.shape[-1]
    return pl.pallas_call(
        _mod_kernel,
        out_shape=jax.ShapeDtypeStruct((DEPTH, 8, n), F32),
        grid=(DEPTH, n // tn),
        in_specs=[
            pl.BlockSpec((8, D_MODEL), lambda l, j: (0, 0)),
            pl.BlockSpec((1, D_MODEL, tn), lambda l, j: (l, 0, j)),
            pl.BlockSpec((1, 1, tn), lambda l, j: (l, 0, j)),
        ],
        out_specs=pl.BlockSpec((1, 8, tn), lambda l, j: (l, 0, j)),
        compiler_params=pltpu.CompilerParams(vmem_limit_bytes=VMEM_LIMIT),
        name="modulation",
    )(cvec, w_mod, b_mod.reshape(DEPTH, 1, n))


def _tile(i):
    return jnp.minimum(i, N_TILES - 1)


def _x_specs(x):
    if isinstance(x, tuple):
        return [pl.BlockSpec((ROW_TILE, D_MODEL), lambda i: (jnp.minimum(i, CTX_TILES - 1), 0)),
                pl.BlockSpec((ROW_TILE, D_MODEL), lambda i: (jnp.maximum(_tile(i) - CTX_TILES, 0), 0))]
    return [pl.BlockSpec((ROW_TILE, D_MODEL), lambda i: (_tile(i), 0))]


def _load_x(x_refs):
    if len(x_refs) == 1:
        return x_refs[0][...]
    return jnp.where(pl.program_id(0) < CTX_TILES, x_refs[0][...], x_refs[1][...])


def _mod_spec(layer):
    return pl.BlockSpec((None, 1, 1, 6 * D_MODEL),
                        lambda i: (layer, _mod_row_of_tile(_tile(i)), 0, 0))


def _rope128(x, cos, sin, half):
    lane = lax.broadcasted_iota(jnp.int32, x.shape, 1)
    first = (lane % (2 * half)) < half
    partner = jnp.where(first, pltpu.roll(x, LANES - half, 1), pltpu.roll(x, half, 1))
    return x * cos + partner * sin


def _rope_cols(x, cos, sin, half):
    chunks = [_rope128(x[:, c:c + LANES], cos, sin, half) for c in range(0, x.shape[1], LANES)]
    return chunks[0] if len(chunks) == 1 else jnp.concatenate(chunks, axis=1)


N_IN_INPUTS = 11


def _in_kernel(*refs, n_x):
    (mod_ref, g_ref, w1_ref, qn_ref, wuq_ref, kvn_ref, wukv_ref,
     cos64_ref, sin64_ref, cos32_ref, sin32_ref) = refs[n_x:n_x + N_IN_INPUTS]
    (qa_o, qd_o, qmn_o, qmr_o, kmn_o, vm_o, kpe4_o, ka_l, va_l, kd_l, vd_l,
     s_ka, s_va, s_kd, s_vd, s_ckv, s_kpe) = refs[-17:]
    i = pl.program_id(0)
    x = _load_x(refs[:n_x])
    m = mod_ref[0]
    sh1, sc1 = m[:, 0:D_MODEL], m[:, D_MODEL:2 * D_MODEL]
    h = _rms(x, g_ref[...]) * (1.0 + sc1) + sh1
    z = _dot(h.astype(BF16), w1_ref[...])
    cos64, sin64 = cos64_ref[...], sin64_ref[...]
    cos32, sin32 = cos32_ref[...], sin32_ref[...]
    qa_o[...] = (_rope_cols(z[:, C_QA:C_KA], cos64, sin64, 16) * SCALE_64).astype(BF16)
    qd_o[...] = (_rope_cols(z[:, C_QD:C_KD], cos64, sin64, 16) * SCALE_64).astype(BF16)
    ka = _rope_cols(z[:, C_KA:C_VA], cos64, sin64, 16)
    va = z[:, C_VA:C_QD]
    kd = _rope_cols(z[:, C_KD:C_VD], cos64, sin64, 16)
    vd = z[:, C_VD:C_QC]
    kpe4 = _rope_cols(z[:, C_KPE:C_END], cos32, sin32, 8)
    kpe4_o[...] = kpe4.astype(BF16)
    qn = _rms(z[:, C_QC:C_KVC], qn_ref[...])
    qm = _dot(qn.astype(BF16), wuq_ref[...]) * SCALE_96
    qmn_o[...] = qm[:, 0:512].astype(BF16)
    qmr_o[...] = _rope_cols(qm[:, 512:768], cos32, sin32, 8).astype(BF16)
    ckv = _rms(z[:, C_KVC:C_KPE], kvn_ref[...])
    kv = _dot(ckv.astype(BF16), wukv_ref[...])
    kmn_o[...] = kv[:, 0:512].astype(BF16)
    vm_o[...] = kv[:, 512:1024].astype(BF16)

    @pl.when(i < CTX_TILES)
    def _():
        s_ka[...] = ka
        s_va[...] = va
        s_kd[...] = kd
        s_vd[...] = vd
        s_ckv[...] = ckv
        s_kpe[...] = kpe4[:, 0:32]

    @pl.when(i >= CTX_TILES)
    def _():
        ka_l[...] = ka.astype(BF16)
        va_l[...] = va.astype(BF16)
        kd_l[...] = kd.astype(BF16)
        vd_l[...] = vd.astype(BF16)


STATE_WIDTHS = (128, 128, 512, 512, 128, 32)


def _in_projection(x, mod, wts, tables, layer, prev_states):
    tm = ROW_TILE

    def row(i):
        return (i, 0)

    def table_row(i):
        return (jnp.where(i < CTX_TILES, 0, 1 + (i - CTX_TILES) % (LAT_SEQ // tm)), 0)

    def lat_row(i):
        return (jnp.maximum(i - CTX_TILES, 0), 0)

    def state_block(i):
        return (jnp.minimum(i, CTX_TILES - 1), layer, 0, 0)

    table_spec = pl.BlockSpec((tm, LANES), table_row)
    weights = [wts["g_pre"], wts["w1"], wts["qnorm"], wts["wuq"], wts["kvnorm"], wts["wukv"]]
    all_rows = [(512, BF16)] * 3 + [(256, BF16), (512, BF16), (512, BF16), (128, BF16)]
    lat_rows = [(128, BF16), (128, BF16), (512, BF16), (512, BF16)]
    aliased = [] if prev_states is None else list(prev_states)
    xs = list(x) if isinstance(x, tuple) else [x]
    n_in = len(xs) + N_IN_INPUTS
    n_plain_out = len(all_rows) + len(lat_rows)
    return pl.pallas_call(
        functools.partial(_in_kernel, n_x=len(xs)),
        out_shape=[jax.ShapeDtypeStruct((N_TOK, w), dt) for w, dt in all_rows]
                  + [jax.ShapeDtypeStruct((N_LAT, w), dt) for w, dt in lat_rows]
                  + [jax.ShapeDtypeStruct((N_CTX_BATCH, DEPTH, CTX_SEQ, w), F32) for w in STATE_WIDTHS],
        grid=(N_TOK // tm,),
        in_specs=_x_specs(x) + [_mod_spec(layer)]
                 + [_layer_spec(w, layer, 1) for w in weights]
                 + [table_spec] * 4
                 + [pl.BlockSpec(memory_space=pl.ANY)] * len(aliased),
        out_specs=[pl.BlockSpec((tm, w), row) for w, _ in all_rows]
                  + [pl.BlockSpec((tm, w), lat_row) for w, _ in lat_rows]
                  + [pl.BlockSpec((None, None, CTX_SEQ, w), state_block) for w in STATE_WIDTHS],
        input_output_aliases={n_in + k: n_plain_out + k for k in range(len(aliased))},
        compiler_params=pltpu.CompilerParams(
            dimension_semantics=("arbitrary",), vmem_limit_bytes=VMEM_LIMIT),
        name="in_projection",
    )(*xs, mod, *weights, *tables, *aliased)


def _attend(q, parts, sink=None):
    scores = []
    for k, _, mask in parts:
        s = _dot_nt(q, k)
        if mask is not None:
            s = jnp.where(mask, s, NEG)
        scores.append(s)
    m = functools.reduce(jnp.maximum, [jnp.max(s, axis=-1, keepdims=True) for s in scores])
    if sink is not None:
        m = jnp.maximum(m, sink)
    den = None
    out = None
    for s, (_, v, _) in zip(scores, parts):
        e = jnp.exp(s - m)
        d = jnp.sum(e, axis=-1, keepdims=True)
        o = _dot(e.astype(BF16), v)
        den = d if den is None else den + d
        out = o if out is None else out + o
    if sink is not None:
        den = den + jnp.exp(sink - m)
    return out / den


A_HEADS = 8


def _attend_heads(qs, parts, sinks, stack):
    rows = qs[0].shape[0]
    if not stack:
        return [_attend(q, parts, None if sinks is None else sinks[i]) for i, q in enumerate(qs)]
    sink = None
    if sinks is not None:
        sink = jnp.concatenate([jnp.full((rows, 1), s, F32) for s in sinks], axis=0)
    o = _attend(jnp.concatenate(qs, axis=0), parts, sink)
    return [o[i * rows:(i + 1) * rows] for i in range(len(qs))]


def _three_mixers(scal_ref, subln_ref, q_blocks, kv_parts, oa_o, od_o, om_o, lam_init, stack_a,
                  stack_bc):
    qa, qd, qmn, qmr = q_blocks
    rows = qa.shape[0]
    lane = lax.broadcasted_iota(jnp.int32, (rows, LANES), 1)
    low = lane < 64
    zero = jnp.zeros((rows, LANES), BF16)

    parts = [(p["ka"], p["va"], p.get("mask_a")) for p in kv_parts]
    qs = [jnp.where(low if kvh == 0 else ~low, qa[:, g * LANES:(g + 1) * LANES], zero)
          for kvh in range(2) for g in range(4)]
    out_a = _attend_heads(qs, parts, [scal_ref[h] for h in range(A_HEADS)], stack_a)
    for g in range(4):
        oa_o[:, g * LANES:(g + 1) * LANES] = jnp.where(low, out_a[g], out_a[4 + g]).astype(BF16)

    lam = scal_ref[8]
    subln = subln_ref[...]
    for h in range(4):
        sl = slice(h * LANES, (h + 1) * LANES)
        q128 = qd[:, sl]
        o1, o2 = _attend_heads([jnp.where(low, q128, zero), jnp.where(low, zero, q128)],
                               [(p["kd"][:, sl], p["vd"][:, sl], None) for p in kv_parts],
                               None, stack_bc)
        od = o1 - lam * o2
        od_o[:, sl] = (_rms(od, subln) * (1.0 - lam_init)).astype(BF16)

    for i in range(4):
        sl = slice(i * LANES, (i + 1) * LANES)
        qs = []
        for sub in range(2):
            h = 2 * i + sub
            qn = jnp.where(low if sub == 0 else ~low, qmn[:, sl], zero)
            qr128 = qmr[:, (h // 4) * LANES:(h // 4 + 1) * LANES]
            qr = jnp.where((lane // 32) == (h % 4), qr128, zero)
            qs.append(jnp.concatenate([qn, qr], axis=1))
        parts = [(jnp.concatenate([p["kmn"][:, sl], p["kpe"]], axis=1), p["vm"][:, sl], None)
                 for p in kv_parts]
        o_even, o_odd = _attend_heads(qs, parts, None, stack_bc)
        om_o[:, sl] = jnp.where(low, o_even, o_odd).astype(BF16)


def _attn_ctx_kernel(scal_ref, subln_ref, qa_ref, qd_ref, qmn_ref, qmr_ref,
                     ka_ref, va_ref, kd_ref, vd_ref, kmn_ref, vm_ref, kpe_ref,
                     oa_o, od_o, om_o, *, lam_init):
    part = dict(ka=ka_ref[...].astype(BF16), va=va_ref[...].astype(BF16),
                kd=kd_ref[...].astype(BF16), vd=vd_ref[...].astype(BF16),
                kmn=kmn_ref[...], vm=vm_ref[...], kpe=kpe_ref[...])
    _three_mixers(scal_ref, subln_ref, (qa_ref[...], qd_ref[...], qmn_ref[...], qmr_ref[...]),
                  [part], oa_o, od_o, om_o, lam_init, stack_a=False, stack_bc=True)


def _attn_lat_kernel(scal_ref, subln_ref, wukv_ref, qa_ref, qd_ref, qmn_ref, qmr_ref,
                     ka_ref, va_ref, kd_ref, vd_ref, kmn_ref, vm_ref, kpe_ref,
                     cka_ref, cva_ref, ckd_ref, cvd_ref, cckv_ref, ckpe_ref,
                     oa_in, od_in, om_in, oa_o, od_o, om_o, *, lam_init):
    del oa_in, od_in, om_in
    n = pl.program_id(1)
    kv_c = _dot(cckv_ref[...].astype(BF16), wukv_ref[...])
    cached = dict(ka=cka_ref[...].astype(BF16), va=cva_ref[...].astype(BF16),
                  kd=ckd_ref[...].astype(BF16), vd=cvd_ref[...].astype(BF16),
                  kmn=kv_c[:, 0:512].astype(BF16), vm=kv_c[:, 512:1024].astype(BF16),
                  kpe=ckpe_ref[...].astype(BF16))
    span = Q_BLOCK + 2 * WINDOW
    start = pl.multiple_of(jnp.clip(n * Q_BLOCK - WINDOW, 0, LAT_SEQ - span), WINDOW)
    qpos = n * Q_BLOCK + lax.broadcasted_iota(jnp.int32, (A_HEADS * Q_BLOCK, span), 0) % Q_BLOCK
    kpos = start + lax.broadcasted_iota(jnp.int32, (A_HEADS * Q_BLOCK, span), 1)
    mask_a = jnp.abs(qpos - kpos) <= WINDOW
    new = dict(ka=ka_ref[pl.ds(start, span), :], va=va_ref[pl.ds(start, span), :], mask_a=mask_a,
               kd=kd_ref[...], vd=vd_ref[...], kmn=kmn_ref[...], vm=vm_ref[...], kpe=kpe_ref[...])
    _three_mixers(scal_ref, subln_ref, (qa_ref[...], qd_ref[...], qmn_ref[...], qmr_ref[...]),
                  [cached, new], oa_o, od_o, om_o, lam_init, stack_a=True, stack_bc=True)


_SMEM_SPEC = pl.BlockSpec(memory_space=pltpu.SMEM)


def _attention_ctx(scal, subln, proj, states, layer, lam_init):
    qa, qd, qmn, qmr, kmn, vm, kpe4 = proj[:7]
    s_ka, s_va, s_kd, s_vd = states[:4]
    t = CTX_SEQ

    def blk(a):
        return pl.BlockSpec((t, a.shape[1]), lambda b: (b, 0))

    def sblk(a):
        return pl.BlockSpec((None, None, t, a.shape[-1]), lambda b: (b, layer, 0, 0))

    return pl.pallas_call(
        functools.partial(_attn_ctx_kernel, lam_init=lam_init),
        out_shape=[jax.ShapeDtypeStruct((N_TOK, 512), BF16)] * 3,
        grid=(N_CTX_BATCH,),
        in_specs=[_SMEM_SPEC, _layer_spec(subln, layer, 1)]
                 + [blk(a) for a in (qa, qd, qmn, qmr)]
                 + [sblk(a) for a in (s_ka, s_va, s_kd, s_vd)]
                 + [blk(a) for a in (kmn, vm, kpe4)],
        out_specs=[pl.BlockSpec((t, 512), lambda b: (b, 0))] * 3,
        compiler_params=pltpu.CompilerParams(vmem_limit_bytes=VMEM_LIMIT),
        name="attention_ctx",
    )(scal, subln, qa, qd, qmn, qmr, s_ka, s_va, s_kd, s_vd, kmn, vm, kpe4)


def _attention_lat(scal, subln, wukv, proj, caches, ctx_out, layer, lam_init):
    qa, qd, qmn, qmr, kmn, vm, kpe4, ka_l, va_l, kd_l, vd_l = proj[:11]
    nq = LAT_SEQ // Q_BLOCK
    q_off = N_CTX // Q_BLOCK
    kv_off = N_CTX // LAT_SEQ

    def qblk(a):
        return pl.BlockSpec((Q_BLOCK, a.shape[1]), lambda b, n: (q_off + b * nq + n, 0))

    def kvblk(a):
        return pl.BlockSpec((LAT_SEQ, a.shape[1]), lambda b, n: (b, 0))

    def kvblk_all(a):
        return pl.BlockSpec((LAT_SEQ, a.shape[1]), lambda b, n: (kv_off + b, 0))

    def cblk(a):
        return pl.BlockSpec((None, None, PAST, a.shape[-1]), lambda b, n: (b, layer, 0, 0))

    qs = [qa, qd, qmn, qmr]
    n_in = 3 + 4 + 7 + 6
    out_spec = pl.BlockSpec((Q_BLOCK, 512), lambda b, n: (q_off + b * nq + n, 0))
    return pl.pallas_call(
        functools.partial(_attn_lat_kernel, lam_init=lam_init),
        out_shape=[jax.ShapeDtypeStruct((N_TOK, 512), BF16)] * 3,
        grid=(N_LAT_BATCH, nq),
        in_specs=[_SMEM_SPEC, _layer_spec(subln, layer, 2), _layer_spec(wukv, layer, 2)]
                 + [qblk(a) for a in qs]
                 + [kvblk(a) for a in (ka_l, va_l, kd_l, vd_l)]
                 + [kvblk_all(a) for a in (kmn, vm, kpe4)]
                 + [cblk(a) for a in caches]
                 + [pl.BlockSpec(memory_space=pl.ANY)] * 3,
        out_specs=[out_spec] * 3,
        input_output_aliases={n_in + k: k for k in range(3)},
        compiler_params=pltpu.CompilerParams(vmem_limit_bytes=VMEM_LIMIT),
        name="attention_lat",
    )(scal, subln, wukv, *qs, ka_l, va_l, kd_l, vd_l, kmn, vm, kpe4, *caches, *ctx_out)


def _route(scores, biased):
    per = N_EXPERTS // N_GROUPS
    tm = scores.shape[1]
    sub = lax.broadcasted_iota(jnp.int32, (per, tm), 0).astype(F32)
    groups = [biased[g * per:(g + 1) * per, :] for g in range(N_GROUPS)]
    gscore = []
    for v in groups:
        m1 = jnp.max(v, axis=0, keepdims=True)
        first = jnp.min(jnp.where(v == m1, sub, float(per)), axis=0, keepdims=True)
        m2 = jnp.max(jnp.where(sub == first, -jnp.inf, v), axis=0, keepdims=True)
        gscore.append(m1 + m2)
    vals = []
    for g in range(N_GROUPS):
        rank = jnp.zeros((1, tm), F32)
        for o in range(N_GROUPS):
            if o == g:
                continue
            ahead = (gscore[o] >= gscore[g]) if o < g else (gscore[o] > gscore[g])
            rank = rank + jnp.where(ahead, 1.0, 0.0)
        vals.append(jnp.where(rank < TOPK_GROUPS, groups[g], NEG))
    idx = [sub + float(g * per) for g in range(N_GROUPS)]
    picks, weights = [], []
    for _ in range(TOP_K):
        best = functools.reduce(jnp.maximum, [jnp.max(v, axis=0, keepdims=True) for v in vals])
        cand = functools.reduce(jnp.minimum, [
            jnp.min(jnp.where(v == best, i, float(N_EXPERTS)), axis=0, keepdims=True)
            for v, i in zip(vals, idx)])
        wsel = jnp.zeros((1, tm), F32)
        for g in range(N_GROUPS):
            hit = idx[g] == cand
            wsel = wsel + jnp.sum(jnp.where(hit, scores[g * per:(g + 1) * per, :], 0.0),
                                  axis=0, keepdims=True)
            vals[g] = jnp.where(hit, -jnp.inf, vals[g])
        picks.append(cand)
        weights.append(wsel)
    total = functools.reduce(lambda a, b: a + b, weights)
    pad = [jnp.zeros((1, tm), F32)] * (8 - TOP_K)
    experts = jnp.concatenate(picks + pad, axis=0).astype(jnp.int32)
    gates = jnp.concatenate([x / total * ROUTED_SCALE for x in weights] + pad, axis=0)
    return experts, gates


def _post_kernel(*refs, n_x):
    (mod_ref, oa_ref, od_ref, om_ref, gpre_ref, wgl_ref, wa_ref, wb_ref, wc_ref, wo_ref, gpost_ref,
     gffn_ref, rwt_ref, rb_ref, xmid_o, h2_o, h2t_o, experts_o, gates_o) = refs[n_x:]

    @pl.when(pl.program_id(0) < N_TILES)
    def _():
        x = _load_x(refs[:n_x])
        m = mod_ref[0]
        d = D_MODEL
        sh1, sc1, g1, sh2, sc2 = (m[:, 0:d], m[:, d:2 * d], m[:, 2 * d:3 * d], m[:, 3 * d:4 * d],
                                  m[:, 4 * d:5 * d])
        h = _rms(x, gpre_ref[...]) * (1.0 + sc1) + sh1
        gate = jax.nn.sigmoid(_dot(h.astype(BF16), wgl_ref[...]))
        merged = (gate[:, 0:d] * _dot(oa_ref[...], wa_ref[...])
                  + gate[:, d:2 * d] * _dot(od_ref[...], wb_ref[...])
                  + gate[:, 2 * d:3 * d] * _dot(om_ref[...], wc_ref[...]))
        a = _dot(merged.astype(BF16), wo_ref[...])
        xm = x + g1 * _rms(a, gpost_ref[...])
        xmid_o[...] = xm
        h2f = _rms(xm, gffn_ref[...]) * (1.0 + sc2) + sh2
        h2 = h2f.astype(BF16)
        h2_o[...] = h2
        for c in range(ACC_ROWS):
            h2t_o[pl.ds(c, x.shape[0], stride=ACC_ROWS), :] = h2f[:, c * LANES:(c + 1) * LANES]
        scores = jax.nn.sigmoid(_dot_nt(rwt_ref[...], h2))
        experts, gates = _route(scores, scores + rb_ref[...])
        experts_o[...] = experts
        gates_o[...] = gates

    @pl.when(pl.program_id(0) == N_TILES)
    def _():
        h2t_o[...] = jnp.zeros_like(h2t_o)


def _post_attention(x, mod, oa, od, om, wts, layer):
    tm = ROW_TILE

    def row(i):
        return (_tile(i), 0)

    weights = [wts[k] for k in ("g_pre", "wgl", "wa", "wb", "wc", "wo", "g_post", "g_ffn", "rwt", "rb")]
    xs = list(x) if isinstance(x, tuple) else [x]
    return pl.pallas_call(
        functools.partial(_post_kernel, n_x=len(xs)),
        out_shape=[jax.ShapeDtypeStruct((N_TOK, D_MODEL), F32),
                   jax.ShapeDtypeStruct((N_TOK, D_MODEL), BF16),
                   jax.ShapeDtypeStruct(((N_TOK + tm) * ACC_ROWS, LANES), F32),
                   jax.ShapeDtypeStruct((8, N_TOK), jnp.int32),
                   jax.ShapeDtypeStruct((8, N_TOK), F32)],
        grid=(N_TILES + 1,),
        in_specs=_x_specs(x) + [_mod_spec(layer),
                  pl.BlockSpec((tm, 512), row), pl.BlockSpec((tm, 512), row),
                  pl.BlockSpec((tm, 512), row)]
                 + [_layer_spec(w, layer, 1) for w in weights],
        out_specs=[pl.BlockSpec((tm, D_MODEL), row), pl.BlockSpec((tm, D_MODEL), row),
                   pl.BlockSpec((tm * ACC_ROWS, LANES), lambda i: (i, 0)),
                   pl.BlockSpec((8, tm), lambda i: (0, _tile(i))),
                   pl.BlockSpec((8, tm), lambda i: (0, _tile(i)))],
        compiler_params=pltpu.CompilerParams(
            dimension_semantics=("arbitrary",), vmem_limit_bytes=VMEM_LIMIT),
        name="post_attention",
    )(*xs, mod, oa, od, om, *weights)


def _ffn(x, wg, wu):
    return _silu(_dot(x, wg)) * _dot(x, wu)


def _moe_kernel(start_ref, count_ref, dst_ref, gate_ref, xs_ref, wg_ref, wu_ref, wd_ref,
                acc_ref, xa, xb, ya, yb, wgb, wub, wdb, pend):
    e = pl.program_id(0)

    @pl.when(e == 0)
    def _():
        def clear(i, carry):
            r0 = pl.multiple_of(i * MOE_ZERO_ROWS, MOE_ZERO_ROWS)
            acc_ref[pl.ds(r0, MOE_ZERO_ROWS), :] = jnp.zeros((MOE_ZERO_ROWS, LANES), F32)
            return carry

        lax.fori_loop(0, acc_ref.shape[0] // MOE_ZERO_ROWS, clear, 0)

        yb[...] = jnp.zeros_like(yb)
        pend[0] = MOE_LIST_PAD_BASE

    wgb[...] = wg_ref[0].astype(BF16)
    wub[...] = wu_ref[0].astype(BF16)
    wdb[...] = wd_ref[0].astype(BF16)
    start = start_ref[e]
    pairs = (count_ref[e] + 2 * MOE_SUB - 1) // (2 * MOE_SUB)

    def gather(base, xbuf):
        for mi in range(MOE_SUB):
            src = pl.multiple_of(dst_ref[base + mi], ACC_ROWS)
            xbuf[mi * ACC_ROWS:(mi + 1) * ACC_ROWS, :] = xs_ref[pl.ds(src, ACC_ROWS), :]

    def expert_ffn(xbuf, ybuf):
        x = jnp.concatenate([xbuf[pl.ds(c, MOE_SUB, stride=ACC_ROWS), :].astype(BF16)
                             for c in range(ACC_ROWS)], axis=1)
        hid = _ffn(x, wgb[...], wub[...])
        y = _dot(hid.astype(BF16), wdb[...])
        for c in range(ACC_ROWS):
            ybuf[pl.ds(c, MOE_SUB, stride=ACC_ROWS), :] = y[:, c * LANES:(c + 1) * LANES]

    def scatter(base, ybuf):
        for g0 in range(0, MOE_SUB, MOE_RMW_GROUP):
            updates = []
            for mi in range(g0, g0 + MOE_RMW_GROUP):
                dst = pl.multiple_of(dst_ref[base + mi], ACC_ROWS)
                yv = ybuf[mi * ACC_ROWS:(mi + 1) * ACC_ROWS, :]
                updates.append((dst, acc_ref[pl.ds(dst, ACC_ROWS), :] + gate_ref[base + mi] * yv))
            for dst, val in updates:
                acc_ref[pl.ds(dst, ACC_ROWS), :] = val

    start_next = start_ref[jnp.minimum(e + 1, N_EXPERTS - 1)]
    prev_count = count_ref[jnp.maximum(e - 1, 0)]

    @pl.when((pairs > 0) & ((e == 0) | (prev_count == 0)))
    def _():
        gather(start, xa)

    def pair(p, carry):
        base = start + p * (2 * MOE_SUB)
        gather(base + MOE_SUB, xb)
        expert_ffn(xa, ya)
        scatter(pend[0], yb)
        gather(jnp.where(p + 1 < pairs, base + 2 * MOE_SUB, start_next), xa)
        expert_ffn(xb, yb)
        scatter(base, ya)
        pend[0] = base + MOE_SUB
        return carry

    lax.fori_loop(0, pairs, pair, 0)

    @pl.when(e == N_EXPERTS - 1)
    def _():
        scatter(pend[0], yb)


def _moe(lists, h2t, wg, wu, wd, layer):
    def expert(e, *_):
        return (layer, e, 0, 0)

    def whole(e, *_):
        return (0, 0)

    acc_rows = (N_TOK + MOE_SPARE_TOKENS) * ACC_ROWS
    assert acc_rows % MOE_ZERO_ROWS == 0
    return pl.pallas_call(
        _moe_kernel,
        out_shape=jax.ShapeDtypeStruct((acc_rows, LANES), F32),
        grid_spec=pltpu.PrefetchScalarGridSpec(
            num_scalar_prefetch=len(lists),
            grid=(N_EXPERTS,),
            in_specs=[pl.BlockSpec((acc_rows, LANES), whole, pipeline_mode=pl.Buffered(1)),
                      pl.BlockSpec((None, 1, D_MODEL, F_EXPERT), expert),
                      pl.BlockSpec((None, 1, D_MODEL, F_EXPERT), expert),
                      pl.BlockSpec((None, 1, F_EXPERT, D_MODEL), expert)],
            out_specs=pl.BlockSpec((acc_rows, LANES), whole, pipeline_mode=pl.Buffered(1)),
            scratch_shapes=[pltpu.VMEM((MOE_SUB * ACC_ROWS, LANES), F32)] * 4
                           + [pltpu.VMEM((D_MODEL, F_EXPERT), BF16),
                              pltpu.VMEM((D_MODEL, F_EXPERT), BF16),
                              pltpu.VMEM((F_EXPERT, D_MODEL), BF16),
                              pltpu.SMEM((1,), jnp.int32)]),
        compiler_params=pltpu.CompilerParams(
            dimension_semantics=("arbitrary",), vmem_limit_bytes=MOE_VMEM_LIMIT),
        name="moe",
    )(*lists, h2t, wg, wu, wd)


def _final_kernel(acc_ref, h2_ref, xmid_ref, mod_ref, swg_ref, swu_ref, swd_ref, gpost_ref, *out_refs):
    tm = h2_ref.shape[0]
    routed = jnp.concatenate(
        [acc_ref[pl.ds(c, tm, stride=ACC_ROWS), :] for c in range(ACC_ROWS)], axis=1)
    hid = _ffn(h2_ref[...], swg_ref[...], swu_ref[...])
    shared = _dot(hid.astype(BF16), swd_ref[...])
    g2 = mod_ref[0][:, 5 * D_MODEL:6 * D_MODEL]
    out = xmid_ref[...] + g2 * _rms(routed + shared, gpost_ref[...])
    if len(out_refs) == 1:
        out_refs[0][...] = out
    else:
        @pl.when(pl.program_id(0) < CTX_TILES)
        def _():
            out_refs[0][...] = out

        @pl.when(pl.program_id(0) >= CTX_TILES)
        def _():
            out_refs[1][...] = out


def _final(acc, h2, xmid, mod, wts, layer, split):
    tm = ROW_TILE

    def row(i):
        return (i, 0)

    weights = [wts[k] for k in ("swg", "swu", "swd", "g_ffn_post")]
    return pl.pallas_call(
        _final_kernel,
        out_shape=([jax.ShapeDtypeStruct((N_CTX, D_MODEL), F32),
                    jax.ShapeDtypeStruct((N_LAT, D_MODEL), F32)] if split
                   else jax.ShapeDtypeStruct((N_TOK, D_MODEL), F32)),
        grid=(N_TOK // tm,),
        in_specs=[pl.BlockSpec((tm * ACC_ROWS, LANES), row),
                  pl.BlockSpec((tm, D_MODEL), row),
                  pl.BlockSpec((tm, D_MODEL), row),
                  _mod_spec(layer)]
                 + [_layer_spec(w, layer, 1) for w in weights],
        out_specs=_x_specs((None, None)) if split else pl.BlockSpec((tm, D_MODEL), row),
        compiler_params=pltpu.CompilerParams(
            dimension_semantics=("arbitrary",), vmem_limit_bytes=VMEM_LIMIT),
        name="ffn_final",
    )(acc, h2, xmid, mod, *weights)


def _routing_lists(experts, gates):
    e = experts[:TOP_K].reshape(-1)
    tok = jnp.tile(jnp.arange(N_TOK, dtype=jnp.int32), TOP_K)
    pad_e = jnp.repeat(jnp.arange(N_EXPERTS, dtype=jnp.int32), MOE_PAD)
    pad_tok = N_TOK + jnp.tile(jnp.arange(MOE_PAD, dtype=jnp.int32), N_EXPERTS)
    keys = jnp.concatenate([e * TOK_KEY + tok, pad_e * TOK_KEY + pad_tok])
    vals = jnp.concatenate([gates[:TOP_K].reshape(-1), jnp.zeros((N_EXPERTS * MOE_PAD,), F32)])
    keys, gate_sorted = lax.sort((keys, vals), num_keys=1)
    tail = jnp.full((MOE_SUB,), N_TOK, jnp.int32)
    tok_sorted = jnp.concatenate([keys & (TOK_KEY - 1), tail])
    gate_sorted = jnp.concatenate([gate_sorted, jnp.zeros((MOE_SUB,), F32)])
    dst = jnp.minimum(tok_sorted, N_TOK) * ACC_ROWS
    counts = jnp.sum((e[None, :] == jnp.arange(N_EXPERTS, dtype=jnp.int32)[:, None]).astype(jnp.int32),
                     axis=1)
    start = jnp.cumsum(counts) - counts + MOE_PAD * jnp.arange(N_EXPERTS, dtype=jnp.int32)
    return start.astype(jnp.int32), counts, dst, gate_sorted


def _rope_tables():
    t = np.arange(LAT_SEQ)
    pos = np.stack([t // GRID_W, t % GRID_W], axis=1).astype(np.float64)

    def table(half):
        inv = ROPE_BASE ** (-np.arange(half, dtype=np.float64) / half)
        lane = np.arange(LANES)
        axis = (lane // (2 * half)) % 2
        freq = inv[lane % half]
        ang = pos[:, axis] * freq[None, :]
        sign = np.where((lane % (2 * half)) < half, -1.0, 1.0)
        cos = np.concatenate([np.ones((ROW_TILE, LANES)), np.cos(ang)], axis=0)
        sin = np.concatenate([np.zeros((ROW_TILE, LANES)), np.sin(ang) * sign[None, :]], axis=0)
        return jnp.asarray(cos, F32), jnp.asarray(sin, F32)

    cos64, sin64 = table(16)
    cos32, sin32 = table(8)
    return cos64, sin64, cos32, sin32


def _prepare_weights(norm_attn_pre, norm_attn_post, norm_ffn_pre, norm_ffn_post, w_in, dif_subln,
                     mla_q_norm, mla_w_uq, mla_kv_norm, mla_w_ukv, w_branch_a, w_branch_b,
                     w_branch_c, w_out, router_w, router_bias, shared_w_gate, shared_w_up,
                     shared_w_down):
    dp = DEPTH
    qa = w_in[:, :, 0:512].reshape(dp, D_MODEL, 2, 4, 64).transpose(0, 1, 3, 2, 4).reshape(dp, D_MODEL, 512)
    kpe = w_in[:, :, 2688:2720]
    w1 = jnp.concatenate([qa, w_in[:, :, 512:2688], kpe, kpe, kpe, kpe], axis=2).astype(BF16)
    uq = mla_w_uq.reshape(dp, 256, 8, 96)
    wuq = jnp.concatenate([uq[..., :64].reshape(dp, 256, 512), uq[..., 64:].reshape(dp, 256, 256)],
                          axis=2).astype(BF16)
    ukv = mla_w_ukv.reshape(dp, 128, 8, 128)
    wukv = jnp.concatenate([ukv[..., :64].reshape(dp, 128, 512), ukv[..., 64:].reshape(dp, 128, 512)],
                           axis=2).astype(BF16)
    wa = w_branch_a.reshape(dp, 2, 4, 64, D_MODEL).transpose(0, 2, 1, 3, 4).reshape(dp, 512, D_MODEL)
    return dict(
        g_pre=norm_attn_pre[:, None, :], g_post=norm_attn_post[:, None, :],
        g_ffn=norm_ffn_pre[:, None, :], g_ffn_post=norm_ffn_post[:, None, :],
        w1=w1, wgl=w_in[:, :, 2720:].astype(BF16),
        qnorm=mla_q_norm[:, None, :], wuq=wuq, kvnorm=mla_kv_norm[:, None, :], wukv=wukv,
        subln=dif_subln[:, None, :],
        wa=wa.astype(BF16), wb=w_branch_b.astype(BF16), wc=w_branch_c.astype(BF16),
        wo=w_out.astype(BF16), rwt=router_w.transpose(0, 2, 1).astype(BF16),
        rb=router_bias[:, :, None],
        swg=shared_w_gate.astype(BF16), swu=shared_w_up.astype(BF16), swd=shared_w_down.astype(BF16))


def kernel(x_prompt, x_sample, cache_swa_k, cache_swa_v, cache_dif_k, cache_dif_v, cache_mla_ckv, cache_mla_kpe, c, c_ctx, w_mod, b_mod, norm_attn_pre, norm_attn_post, norm_ffn_pre, norm_ffn_post, w_in, swa_sink, dif_lq1, dif_lk1, dif_lq2, dif_lk2, dif_subln, mla_q_norm, mla_w_uq, mla_kv_norm, mla_w_ukv, w_branch_a, w_branch_b, w_branch_c, w_out, router_w, router_bias, moe_w_gate, moe_w_up, moe_w_down, shared_w_gate, shared_w_up, shared_w_down):
    x = (x_prompt.reshape(N_CTX, D_MODEL), x_sample.reshape(N_LAT, D_MODEL))
    cvec = jnp.concatenate([c_ctx[None, :], c, jnp.zeros((8 - 1 - N_LAT_BATCH, D_MODEL), F32)], axis=0)
    mod = _modulation(cvec, w_mod, b_mod).reshape(DEPTH, 8, 1, 6 * D_MODEL)
    tables = _rope_tables()
    wts = _prepare_weights(norm_attn_pre, norm_attn_post, norm_ffn_pre, norm_ffn_post, w_in,
                           dif_subln, mla_q_norm, mla_w_uq, mla_kv_norm, mla_w_ukv, w_branch_a,
                           w_branch_b, w_branch_c, w_out, router_w, router_bias, shared_w_gate,
                           shared_w_up, shared_w_down)
    caches = [cache_swa_k.reshape(N_LAT_BATCH, DEPTH, PAST, 128),
              cache_swa_v.reshape(N_LAT_BATCH, DEPTH, PAST, 128),
              cache_dif_k.reshape(N_LAT_BATCH, DEPTH, PAST, 512),
              cache_dif_v.reshape(N_LAT_BATCH, DEPTH, PAST, 512),
              cache_mla_ckv,
              jnp.tile(cache_mla_kpe, (1, 1, 1, 4))]
    lam_init = [0.8 - 0.6 * math.exp(-0.3 * l) for l in range(DEPTH)]
    lam = (jnp.exp(jnp.sum(dif_lq1 * dif_lk1, axis=1)) - jnp.exp(jnp.sum(dif_lq2 * dif_lk2, axis=1))
           + jnp.asarray(lam_init, F32))
    scal = jnp.concatenate([swa_sink, lam[:, None]], axis=1).astype(F32)

    states = None
    for l in range(DEPTH):
        proj = _in_projection(x, mod, wts, tables, l, states)
        states = proj[11:]
        ctx_out = _attention_ctx(scal[l], wts["subln"], proj, states, l, lam_init[l])
        oa, od, om = _attention_lat(scal[l], wts["subln"], wts["wukv"], proj, caches, ctx_out, l,
                                    lam_init[l])
        xmid, h2, h2t, experts, gates = _post_attention(x, mod, oa, od, om, wts, l)
        lists = _routing_lists(experts, gates)
        acc = _moe(lists, h2t, moe_w_gate, moe_w_up, moe_w_down, l)
        x = _final(acc, h2, xmid, mod, wts, l, split=(l == DEPTH - 1))

    y_p = x[0].reshape(N_CTX_BATCH, CTX_SEQ, D_MODEL)
    y_s = x[1].reshape(N_LAT_BATCH, LAT_SEQ, D_MODEL)
    s_ka, s_va, s_kd, s_vd, s_ckv, s_kpe = states
    return (y_p, y_s,
            s_ka.reshape(N_CTX_BATCH, DEPTH, CTX_SEQ, 2, 64),
            s_va.reshape(N_CTX_BATCH, DEPTH, CTX_SEQ, 2, 64),
            s_kd.reshape(N_CTX_BATCH, DEPTH, CTX_SEQ, 4, 2, 64),
            s_vd.reshape(N_CTX_BATCH, DEPTH, CTX_SEQ, 4, 128),
            s_ckv, s_kpe)
```

```python
import functools
import math

import numpy as np
import jax
import jax.numpy as jnp
from jax import lax
from jax.experimental import pallas as pl
from jax.experimental.pallas import tpu as pltpu

F32 = jnp.float32
BF16 = jnp.bfloat16

D_MODEL = 1024
N_CTX_BATCH, CTX_SEQ = 16, 256
N_LAT_BATCH, LAT_SEQ = 2, 1024
PAST = 256
N_CTX = N_CTX_BATCH * CTX_SEQ
N_LAT = N_LAT_BATCH * LAT_SEQ
N_TOK = N_CTX + N_LAT
DEPTH = 2
GRID_W = 64
WINDOW = 128
N_EXPERTS = 64
N_GROUPS = 8
TOPK_GROUPS = 4
TOP_K = 6
F_EXPERT = 256
ROUTED_SCALE = 2.5
ROPE_BASE = 10000.0
EPS = 1e-6
NEG = -1e30

LANES = 128
ROW_TILE = 256
Q_BLOCK = 256
MOD_COL_TILE = 3072
VMEM_LIMIT = 56 * 1024 * 1024
ACC_ROWS = 8
MOE_SPARE_TOKENS = 8
MOE_ZERO_ROWS = 64
MOE_VMEM_LIMIT = 62 * 1024 * 1024
MOE_SUB = 128
MOE_RMW_GROUP = 4
MOE_PAD = 2 * MOE_SUB
TOK_KEY = 8192
MOE_LIST_PAD_BASE = N_TOK * TOP_K + N_EXPERTS * MOE_PAD

C_QA, C_KA, C_VA, C_QD, C_KD, C_VD, C_QC, C_KVC, C_KPE, C_END = (
    0, 512, 640, 768, 1280, 1792, 2304, 2560, 2688, 2816)
SCALE_64 = 1.0 / math.sqrt(64.0)
SCALE_96 = 1.0 / math.sqrt(96.0)

CTX_TILES = N_CTX // ROW_TILE
N_TILES = N_TOK // ROW_TILE
assert ROW_TILE == CTX_SEQ and LAT_SEQ % ROW_TILE == 0


def _mod_row_of_tile(i):
    return jnp.where(i < CTX_TILES, 0, 1 + (i - CTX_TILES) // (LAT_SEQ // ROW_TILE))


def _rms(x, g):
    return x * lax.rsqrt(jnp.mean(x * x, axis=-1, keepdims=True) + EPS) * g


def _dot(a, b):
    return jnp.dot(a, b, preferred_element_type=F32)


def _dot_nt(a, b):
    return lax.dot_general(a, b, (((1,), (1,)), ((), ())), preferred_element_type=F32)


def _silu(x):
    return x * jax.nn.sigmoid(x)


def _layer_spec(arr, layer):
    zeros = (0,) * (arr.ndim - 1)
    return pl.BlockSpec((None,) + arr.shape[1:], lambda *_: (layer,) + zeros)


def _mod_kernel(c_ref, w_ref, b_ref, o_ref):
    c = c_ref[...]
    o_ref[0] = _dot(_silu(c).astype(BF16), w_ref[0].astype(BF16)) + b_ref[0]


def _modulation(cvec, w_mod, b_mod):
    tn = MOD_COL_TILE
    n = w_mod.shape[-1]
    return pl.pallas_call(
        _mod_kernel,
        out_shape=jax.ShapeDtypeStruct((DEPTH, 8, n), F32),
        grid=(DEPTH, n // tn),
        in_specs=[
            pl.BlockSpec((8, D_MODEL), lambda l, j: (0, 0)),
            pl.BlockSpec((1, D_MODEL, tn), lambda l, j: (l, 0, j)),
            pl.BlockSpec((1, 1, tn), lambda l, j: (l, 0, j)),
        ],
        out_specs=pl.BlockSpec((1, 8, tn), lambda l, j: (l, 0, j)),
        compiler_params=pltpu.CompilerParams(vmem_limit_bytes=VMEM_LIMIT),
        name="modulation",
    )(cvec, w_mod, b_mod.reshape(DEPTH, 1, n))


def _tile(i):
    return jnp.minimum(i, N_TILES - 1)


def _x_specs(x):
    if isinstance(x, tuple):
        return [pl.BlockSpec((ROW_TILE, D_MODEL), lambda i: (jnp.minimum(i, CTX_TILES - 1), 0)),
                pl.BlockSpec((ROW_TILE, D_MODEL), lambda i: (jnp.maximum(_tile(i) - CTX_TILES, 0), 0))]
    return [pl.BlockSpec((ROW_TILE, D_MODEL), lambda i: (_tile(i), 0))]


def _load_x(x_refs):
    if len(x_refs) == 1:
        return x_refs[0][...]
    return jnp.where(pl.program_id(0) < CTX_TILES, x_refs[0][...], x_refs[1][...])


def _mod_spec(layer):
    return pl.BlockSpec((None, 1, 1, 6 * D_MODEL),
                        lambda i: (layer, _mod_row_of_tile(_tile(i)), 0, 0))


def _rope128(x, cos, sin, half):
    lane = lax.broadcasted_iota(jnp.int32, x.shape, 1)
    first = (lane % (2 * half)) < half
    partner = jnp.where(first, pltpu.roll(x, LANES - half, 1), pltpu.roll(x, half, 1))
    return x * cos + partner * sin


def _rope_cols(x, cos, sin, half):
    chunks = [_rope128(x[:, c:c + LANES], cos, sin, half) for c in range(0, x.shape[1], LANES)]
    return chunks[0] if len(chunks) == 1 else jnp.concatenate(chunks, axis=1)


N_IN_INPUTS = 11


def _in_kernel(*refs, n_x):
    (mod_ref, g_ref, w1_ref, qn_ref, wuq_ref, kvn_ref, wukv_ref,
     cos64_ref, sin64_ref, cos32_ref, sin32_ref) = refs[n_x:n_x + N_IN_INPUTS]
    (qa_o, qd_o, qmn_o, qmr_o, kmn_o, vm_o, kpe4_o, ka_l, va_l, kd_l, vd_l,
     s_ka, s_va, s_kd, s_vd, s_ckv, s_kpe) = refs[-17:]
    i = pl.program_id(0)
    x = _load_x(refs[:n_x])
    m = mod_ref[0]
    sh1, sc1 = m[:, 0:D_MODEL], m[:, D_MODEL:2 * D_MODEL]
    h = _rms(x, g_ref[...]) * (1.0 + sc1) + sh1
    z = _dot(h.astype(BF16), w1_ref[...])
    cos64, sin64 = cos64_ref[...], sin64_ref[...]
    cos32, sin32 = cos32_ref[...], sin32_ref[...]
    qa_o[...] = (_rope_cols(z[:, C_QA:C_KA], cos64, sin64, 16) * SCALE_64).astype(BF16)
    qd_o[...] = (_rope_cols(z[:, C_QD:C_KD], cos64, sin64, 16) * SCALE_64).astype(BF16)
    ka = _rope_cols(z[:, C_KA:C_VA], cos64, sin64, 16)
    va = z[:, C_VA:C_QD]
    kd = _rope_cols(z[:, C_KD:C_VD], cos64, sin64, 16)
    vd = z[:, C_VD:C_QC]
    kpe4 = _rope_cols(z[:, C_KPE:C_END], cos32, sin32, 8)
    kpe4_o[...] = kpe4.astype(BF16)
    qn = _rms(z[:, C_QC:C_KVC], qn_ref[...])
    qm = _dot(qn.astype(BF16), wuq_ref[...]) * SCALE_96
    qmn_o[...] = qm[:, 0:512].astype(BF16)
    qmr_o[...] = _rope_cols(qm[:, 512:768], cos32, sin32, 8).astype(BF16)
    ckv = _rms(z[:, C_KVC:C_KPE], kvn_ref[...])
    kv = _dot(ckv.astype(BF16), wukv_ref[...])
    kmn_o[...] = kv[:, 0:512].astype(BF16)
    vm_o[...] = kv[:, 512:1024].astype(BF16)

    @pl.when(i < CTX_TILES)
    def _():
        s_ka[...] = ka
        s_va[...] = va
        s_kd[...] = kd
        s_vd[...] = vd
        s_ckv[...] = ckv
        s_kpe[...] = kpe4[:, 0:32]

    @pl.when(i >= CTX_TILES)
    def _():
        ka_l[...] = ka.astype(BF16)
        va_l[...] = va.astype(BF16)
        kd_l[...] = kd.astype(BF16)
        vd_l[...] = vd.astype(BF16)


STATE_WIDTHS = (128, 128, 512, 512, 128, 32)


def _in_projection(x, mod, wts, tables, layer, prev_states):
    tm = ROW_TILE

    def row(i):
        return (i, 0)

    def table_row(i):
        return (jnp.where(i < CTX_TILES, 0, 1 + (i - CTX_TILES) % (LAT_SEQ // tm)), 0)

    def lat_row(i):
        return (jnp.maximum(i - CTX_TILES, 0), 0)

    def state_block(i):
        return (jnp.minimum(i, CTX_TILES - 1), layer, 0, 0)

    table_spec = pl.BlockSpec((tm, LANES), table_row)
    weights = [wts["g_pre"], wts["w1"], wts["qnorm"], wts["wuq"], wts["kvnorm"], wts["wukv"]]
    all_rows = [(512, BF16)] * 3 + [(256, BF16), (512, BF16), (512, BF16), (128, BF16)]
    lat_rows = [(128, BF16), (128, BF16), (512, BF16), (512, BF16)]
    aliased = [] if prev_states is None else list(prev_states)
    xs = list(x) if isinstance(x, tuple) else [x]
    n_in = len(xs) + N_IN_INPUTS
    n_plain_out = len(all_rows) + len(lat_rows)
    return pl.pallas_call(
        functools.partial(_in_kernel, n_x=len(xs)),
        out_shape=[jax.ShapeDtypeStruct((N_TOK, w), dt) for w, dt in all_rows]
                  + [jax.ShapeDtypeStruct((N_LAT, w), dt) for w, dt in lat_rows]
                  + [jax.ShapeDtypeStruct((N_CTX_BATCH, DEPTH, CTX_SEQ, w), F32) for w in STATE_WIDTHS],
        grid=(N_TOK // tm,),
        in_specs=_x_specs(x) + [_mod_spec(layer)]
                 + [_layer_spec(w, layer) for w in weights]
                 + [table_spec] * 4
                 + [pl.BlockSpec(memory_space=pl.ANY)] * len(aliased),
        out_specs=[pl.BlockSpec((tm, w), row) for w, _ in all_rows]
                  + [pl.BlockSpec((tm, w), lat_row) for w, _ in lat_rows]
                  + [pl.BlockSpec((None, None, CTX_SEQ, w), state_block) for w in STATE_WIDTHS],
        input_output_aliases={n_in + k: n_plain_out + k for k in range(len(aliased))},
        compiler_params=pltpu.CompilerParams(
            dimension_semantics=("arbitrary",), vmem_limit_bytes=VMEM_LIMIT),
        name="in_projection",
    )(*xs, mod, *weights, *tables, *aliased)


def _attend(q, parts, sink=None):
    scores = []
    for k, _, mask in parts:
        s = _dot_nt(q, k)
        if mask is not None:
            s = jnp.where(mask, s, NEG)
        scores.append(s)
    m = functools.reduce(jnp.maximum, [jnp.max(s, axis=-1, keepdims=True) for s in scores])
    if sink is not None:
        m = jnp.maximum(m, sink)
    den = None
    out = None
    for s, (_, v, _) in zip(scores, parts):
        e = jnp.exp(s - m)
        d = jnp.sum(e, axis=-1, keepdims=True)
        o = _dot(e.astype(BF16), v)
        den = d if den is None else den + d
        out = o if out is None else out + o
    if sink is not None:
        den = den + jnp.exp(sink - m)
    return out / den


A_HEADS = 8


def _attend_heads(qs, parts, sinks, stack):
    rows = qs[0].shape[0]
    if not stack:
        return [_attend(q, parts, None if sinks is None else sinks[i]) for i, q in enumerate(qs)]
    sink = None
    if sinks is not None:
        sink = jnp.concatenate([jnp.full((rows, 1), s, F32) for s in sinks], axis=0)
    o = _attend(jnp.concatenate(qs, axis=0), parts, sink)
    return [o[i * rows:(i + 1) * rows] for i in range(len(qs))]


def _three_mixers(scal_ref, subln_ref, q_blocks, kv_parts, oa_o, od_o, om_o, lam_init, stack_a,
                  stack_bc):
    qa, qd, qmn, qmr = q_blocks
    rows = qa.shape[0]
    lane = lax.broadcasted_iota(jnp.int32, (rows, LANES), 1)
    low = lane < 64
    zero = jnp.zeros((rows, LANES), BF16)

    parts = [(p["ka"], p["va"], p.get("mask_a")) for p in kv_parts]
    qs = [jnp.where(low if kvh == 0 else ~low, qa[:, g * LANES:(g + 1) * LANES], zero)
          for kvh in range(2) for g in range(4)]
    out_a = _attend_heads(qs, parts, [scal_ref[h] for h in range(A_HEADS)], stack_a)
    for g in range(4):
        oa_o[:, g * LANES:(g + 1) * LANES] = jnp.where(low, out_a[g], out_a[4 + g]).astype(BF16)

    lam = scal_ref[8]
    subln = subln_ref[...]
    for h in range(4):
        sl = slice(h * LANES, (h + 1) * LANES)
        q128 = qd[:, sl]
        o1, o2 = _attend_heads([jnp.where(low, q128, zero), jnp.where(low, zero, q128)],
                               [(p["kd"][:, sl], p["vd"][:, sl], None) for p in kv_parts],
                               None, stack_bc)
        od = o1 - lam * o2
        od_o[:, sl] = (_rms(od, subln) * (1.0 - lam_init)).astype(BF16)

    for i in range(4):
        sl = slice(i * LANES, (i + 1) * LANES)
        qs = []
        for sub in range(2):
            h = 2 * i + sub
            qn = jnp.where(low if sub == 0 else ~low, qmn[:, sl], zero)
            qr128 = qmr[:, (h // 4) * LANES:(h // 4 + 1) * LANES]
            qr = jnp.where((lane // 32) == (h % 4), qr128, zero)
            qs.append(jnp.concatenate([qn, qr], axis=1))
        parts = [(jnp.concatenate([p["kmn"][:, sl], p["kpe"]], axis=1), p["vm"][:, sl], None)
                 for p in kv_parts]
        o_even, o_odd = _attend_heads(qs, parts, None, stack_bc)
        om_o[:, sl] = jnp.where(low, o_even, o_odd).astype(BF16)


def _attn_ctx_kernel(scal_ref, subln_ref, qa_ref, qd_ref, qmn_ref, qmr_ref,
                     ka_ref, va_ref, kd_ref, vd_ref, kmn_ref, vm_ref, kpe_ref,
                     oa_o, od_o, om_o, *, lam_init):
    part = dict(ka=ka_ref[...].astype(BF16), va=va_ref[...].astype(BF16),
                kd=kd_ref[...].astype(BF16), vd=vd_ref[...].astype(BF16),
                kmn=kmn_ref[...], vm=vm_ref[...], kpe=kpe_ref[...])
    _three_mixers(scal_ref, subln_ref, (qa_ref[...], qd_ref[...], qmn_ref[...], qmr_ref[...]),
                  [part], oa_o, od_o, om_o, lam_init, stack_a=False, stack_bc=True)


def _attn_lat_kernel(scal_ref, subln_ref, wukv_ref, qa_ref, qd_ref, qmn_ref, qmr_ref,
                     ka_ref, va_ref, kd_ref, vd_ref, kmn_ref, vm_ref, kpe_ref,
                     cka_ref, cva_ref, ckd_ref, cvd_ref, cckv_ref, ckpe_ref,
                     oa_in, od_in, om_in, oa_o, od_o, om_o, *, lam_init):
    del oa_in, od_in, om_in
    n = pl.program_id(1)
    kv_c = _dot(cckv_ref[...].astype(BF16), wukv_ref[...])
    cached = dict(ka=cka_ref[...].astype(BF16), va=cva_ref[...].astype(BF16),
                  kd=ckd_ref[...].astype(BF16), vd=cvd_ref[...].astype(BF16),
                  kmn=kv_c[:, 0:512].astype(BF16), vm=kv_c[:, 512:1024].astype(BF16),
                  kpe=ckpe_ref[...].astype(BF16))
    span = Q_BLOCK + 2 * WINDOW
    start = pl.multiple_of(jnp.clip(n * Q_BLOCK - WINDOW, 0, LAT_SEQ - span), WINDOW)
    qpos = n * Q_BLOCK + lax.broadcasted_iota(jnp.int32, (A_HEADS * Q_BLOCK, span), 0) % Q_BLOCK
    kpos = start + lax.broadcasted_iota(jnp.int32, (A_HEADS * Q_BLOCK, span), 1)
    mask_a = jnp.abs(qpos - kpos) <= WINDOW
    new = dict(ka=ka_ref[pl.ds(start, span), :], va=va_ref[pl.ds(start, span), :], mask_a=mask_a,
               kd=kd_ref[...], vd=vd_ref[...], kmn=kmn_ref[...], vm=vm_ref[...], kpe=kpe_ref[...])
    _three_mixers(scal_ref, subln_ref, (qa_ref[...], qd_ref[...], qmn_ref[...], qmr_ref[...]),
                  [cached, new], oa_o, od_o, om_o, lam_init, stack_a=True, stack_bc=True)


_SMEM_SPEC = pl.BlockSpec(memory_space=pltpu.SMEM)


def _attention_ctx(scal, subln, proj, states, layer, lam_init):
    qa, qd, qmn, qmr, kmn, vm, kpe4 = proj[:7]
    s_ka, s_va, s_kd, s_vd = states[:4]
    t = CTX_SEQ

    def blk(a):
        return pl.BlockSpec((t, a.shape[1]), lambda b: (b, 0))

    def sblk(a):
        return pl.BlockSpec((None, None, t, a.shape[-1]), lambda b: (b, layer, 0, 0))

    return pl.pallas_call(
        functools.partial(_attn_ctx_kernel, lam_init=lam_init),
        out_shape=[jax.ShapeDtypeStruct((N_TOK, 512), BF16)] * 3,
        grid=(N_CTX_BATCH,),
        in_specs=[_SMEM_SPEC, _layer_spec(subln, layer)]
                 + [blk(a) for a in (qa, qd, qmn, qmr)]
                 + [sblk(a) for a in (s_ka, s_va, s_kd, s_vd)]
                 + [blk(a) for a in (kmn, vm, kpe4)],
        out_specs=[pl.BlockSpec((t, 512), lambda b: (b, 0))] * 3,
        compiler_params=pltpu.CompilerParams(vmem_limit_bytes=VMEM_LIMIT),
        name="attention_ctx",
    )(scal, subln, qa, qd, qmn, qmr, s_ka, s_va, s_kd, s_vd, kmn, vm, kpe4)


def _attention_lat(scal, subln, wukv, proj, caches, ctx_out, layer, lam_init):
    qa, qd, qmn, qmr, kmn, vm, kpe4, ka_l, va_l, kd_l, vd_l = proj[:11]
    nq = LAT_SEQ // Q_BLOCK
    q_off = N_CTX // Q_BLOCK
    kv_off = N_CTX // LAT_SEQ

    def qblk(a):
        return pl.BlockSpec((Q_BLOCK, a.shape[1]), lambda b, n: (q_off + b * nq + n, 0))

    def kvblk(a):
        return pl.BlockSpec((LAT_SEQ, a.shape[1]), lambda b, n: (b, 0))

    def kvblk_all(a):
        return pl.BlockSpec((LAT_SEQ, a.shape[1]), lambda b, n: (kv_off + b, 0))

    def cblk(a):
        return pl.BlockSpec((None, None, PAST, a.shape[-1]), lambda b, n: (b, layer, 0, 0))

    qs = [qa, qd, qmn, qmr]
    n_in = 3 + 4 + 7 + 6
    out_spec = pl.BlockSpec((Q_BLOCK, 512), lambda b, n: (q_off + b * nq + n, 0))
    return pl.pallas_call(
        functools.partial(_attn_lat_kernel, lam_init=lam_init),
        out_shape=[jax.ShapeDtypeStruct((N_TOK, 512), BF16)] * 3,
        grid=(N_LAT_BATCH, nq),
        in_specs=[_SMEM_SPEC, _layer_spec(subln, layer), _layer_spec(wukv, layer)]
                 + [qblk(a) for a in qs]
                 + [kvblk(a) for a in (ka_l, va_l, kd_l, vd_l)]
                 + [kvblk_all(a) for a in (kmn, vm, kpe4)]
                 + [cblk(a) for a in caches]
                 + [pl.BlockSpec(memory_space=pl.ANY)] * 3,
        out_specs=[out_spec] * 3,
        input_output_aliases={n_in + k: k for k in range(3)},
        compiler_params=pltpu.CompilerParams(vmem_limit_bytes=VMEM_LIMIT),
        name="attention_lat",
    )(scal, subln, wukv, *qs, ka_l, va_l, kd_l, vd_l, kmn, vm, kpe4, *caches, *ctx_out)


def _route(scores, biased):
    per = N_EXPERTS // N_GROUPS
    tm = scores.shape[1]
    sub = lax.broadcasted_iota(jnp.int32, (per, tm), 0).astype(F32)
    groups = [biased[g * per:(g + 1) * per, :] for g in range(N_GROUPS)]
    gscore = []
    for v in groups:
        m1 = jnp.max(v, axis=0, keepdims=True)
        first = jnp.min(jnp.where(v == m1, sub, float(per)), axis=0, keepdims=True)
        m2 = jnp.max(jnp.where(sub == first, -jnp.inf, v), axis=0, keepdims=True)
        gscore.append(m1 + m2)
    vals = []
    for g in range(N_GROUPS):
        rank = jnp.zeros((1, tm), F32)
        for o in range(N_GROUPS):
            if o == g:
                continue
            ahead = (gscore[o] >= gscore[g]) if o < g else (gscore[o] > gscore[g])
            rank = rank + jnp.where(ahead, 1.0, 0.0)
        vals.append(jnp.where(rank < TOPK_GROUPS, groups[g], NEG))
    idx = [sub + float(g * per) for g in range(N_GROUPS)]
    picks, weights = [], []
    for _ in range(TOP_K):
        best = functools.reduce(jnp.maximum, [jnp.max(v, axis=0, keepdims=True) for v in vals])
        cand = functools.reduce(jnp.minimum, [
            jnp.min(jnp.where(v == best, i, float(N_EXPERTS)), axis=0, keepdims=True)
            for v, i in zip(vals, idx)])
        wsel = jnp.zeros((1, tm), F32)
        for g in range(N_GROUPS):
            hit = idx[g] == cand
            wsel = wsel + jnp.sum(jnp.where(hit, scores[g * per:(g + 1) * per, :], 0.0),
                                  axis=0, keepdims=True)
            vals[g] = jnp.where(hit, -jnp.inf, vals[g])
        picks.append(cand)
        weights.append(wsel)
    total = functools.reduce(lambda a, b: a + b, weights)
    pad = [jnp.zeros((1, tm), F32)] * (8 - TOP_K)
    experts = jnp.concatenate(picks + pad, axis=0).astype(jnp.int32)
    gates = jnp.concatenate([x / total * ROUTED_SCALE for x in weights] + pad, axis=0)
    return experts, gates


def _post_kernel(*refs, n_x):
    (mod_ref, oa_ref, od_ref, om_ref, gpre_ref, wgl_ref, wa_ref, wb_ref, wc_ref, wo_ref, gpost_ref,
     gffn_ref, rwt_ref, rb_ref, xmid_o, h2_o, h2t_o, experts_o, gates_o) = refs[n_x:]

    @pl.when(pl.program_id(0) < N_TILES)
    def _():
        x = _load_x(refs[:n_x])
        m = mod_ref[0]
        d = D_MODEL
        sh1, sc1, g1, sh2, sc2 = (m[:, 0:d], m[:, d:2 * d], m[:, 2 * d:3 * d], m[:, 3 * d:4 * d],
                                  m[:, 4 * d:5 * d])
        h = _rms(x, gpre_ref[...]) * (1.0 + sc1) + sh1
        gate = jax.nn.sigmoid(_dot(h.astype(BF16), wgl_ref[...]))
        merged = (gate[:, 0:d] * _dot(oa_ref[...], wa_ref[...])
                  + gate[:, d:2 * d] * _dot(od_ref[...], wb_ref[...])
                  + gate[:, 2 * d:3 * d] * _dot(om_ref[...], wc_ref[...]))
        a = _dot(merged.astype(BF16), wo_ref[...])
        xm = x + g1 * _rms(a, gpost_ref[...])
        xmid_o[...] = xm
        h2f = _rms(xm, gffn_ref[...]) * (1.0 + sc2) + sh2
        h2 = h2f.astype(BF16)
        h2_o[...] = h2
        for c in range(ACC_ROWS):
            h2t_o[pl.ds(c, x.shape[0], stride=ACC_ROWS), :] = h2f[:, c * LANES:(c + 1) * LANES]
        scores = jax.nn.sigmoid(_dot_nt(rwt_ref[...], h2))
        experts, gates = _route(scores, scores + rb_ref[...])
        experts_o[...] = experts
        gates_o[...] = gates

    @pl.when(pl.program_id(0) == N_TILES)
    def _():
        h2t_o[...] = jnp.zeros_like(h2t_o)


def _post_attention(x, mod, oa, od, om, wts, layer):
    tm = ROW_TILE

    def row(i):
        return (_tile(i), 0)

    weights = [wts[k] for k in ("g_pre", "wgl", "wa", "wb", "wc", "wo", "g_post", "g_ffn", "rwt", "rb")]
    xs = list(x) if isinstance(x, tuple) else [x]
    return pl.pallas_call(
        functools.partial(_post_kernel, n_x=len(xs)),
        out_shape=[jax.ShapeDtypeStruct((N_TOK, D_MODEL), F32),
                   jax.ShapeDtypeStruct((N_TOK, D_MODEL), BF16),
                   jax.ShapeDtypeStruct(((N_TOK + tm) * ACC_ROWS, LANES), F32),
                   jax.ShapeDtypeStruct((8, N_TOK), jnp.int32),
                   jax.ShapeDtypeStruct((8, N_TOK), F32)],
        grid=(N_TILES + 1,),
        in_specs=_x_specs(x) + [_mod_spec(layer),
                  pl.BlockSpec((tm, 512), row), pl.BlockSpec((tm, 512), row),
                  pl.BlockSpec((tm, 512), row)]
                 + [_layer_spec(w, layer) for w in weights],
        out_specs=[pl.BlockSpec((tm, D_MODEL), row), pl.BlockSpec((tm, D_MODEL), row),
                   pl.BlockSpec((tm * ACC_ROWS, LANES), lambda i: (i, 0)),
                   pl.BlockSpec((8, tm), lambda i: (0, _tile(i))),
                   pl.BlockSpec((8, tm), lambda i: (0, _tile(i)))],
        compiler_params=pltpu.CompilerParams(
            dimension_semantics=("arbitrary",), vmem_limit_bytes=VMEM_LIMIT),
        name="post_attention",
    )(*xs, mod, oa, od, om, *weights)


def _ffn(x, wg, wu):
    return _silu(_dot(x, wg)) * _dot(x, wu)


def _moe_kernel(start_ref, count_ref, dst_ref, gate_ref, xs_ref, wg_ref, wu_ref, wd_ref,
                acc_ref, xa, xb, ya, yb, wgb, wub, wdb, pend):
    e = pl.program_id(0)

    @pl.when(e == 0)
    def _():
        def clear(i, carry):
            r0 = pl.multiple_of(i * MOE_ZERO_ROWS, MOE_ZERO_ROWS)
            acc_ref[pl.ds(r0, MOE_ZERO_ROWS), :] = jnp.zeros((MOE_ZERO_ROWS, LANES), F32)
            return carry

        lax.fori_loop(0, acc_ref.shape[0] // MOE_ZERO_ROWS, clear, 0)

        yb[...] = jnp.zeros_like(yb)
        pend[0] = MOE_LIST_PAD_BASE

    wgb[...] = wg_ref[0].astype(BF16)
    wub[...] = wu_ref[0].astype(BF16)
    wdb[...] = wd_ref[0].astype(BF16)
    start = start_ref[e]
    pairs = (count_ref[e] + 2 * MOE_SUB - 1) // (2 * MOE_SUB)

    def gather(base, xbuf):
        for mi in range(MOE_SUB):
            src = pl.multiple_of(dst_ref[base + mi], ACC_ROWS)
            xbuf[mi * ACC_ROWS:(mi + 1) * ACC_ROWS, :] = xs_ref[pl.ds(src, ACC_ROWS), :]

    def expert_ffn(xbuf, ybuf):
        x = jnp.concatenate([xbuf[pl.ds(c, MOE_SUB, stride=ACC_ROWS), :].astype(BF16)
                             for c in range(ACC_ROWS)], axis=1)
        hid = _ffn(x, wgb[...], wub[...])
        y = _dot(hid.astype(BF16), wdb[...])
        for c in range(ACC_ROWS):
            ybuf[pl.ds(c, MOE_SUB, stride=ACC_ROWS), :] = y[:, c * LANES:(c + 1) * LANES]

    def scatter(base, ybuf):
        for g0 in range(0, MOE_SUB, MOE_RMW_GROUP):
            updates = []
            for mi in range(g0, g0 + MOE_RMW_GROUP):
                dst = pl.multiple_of(dst_ref[base + mi], ACC_ROWS)
                yv = ybuf[mi * ACC_ROWS:(mi + 1) * ACC_ROWS, :]
                updates.append((dst, acc_ref[pl.ds(dst, ACC_ROWS), :] + gate_ref[base + mi] * yv))
            for dst, val in updates:
                acc_ref[pl.ds(dst, ACC_ROWS), :] = val

    start_next = start_ref[jnp.minimum(e + 1, N_EXPERTS - 1)]
    prev_count = count_ref[jnp.maximum(e - 1, 0)]

    @pl.when((pairs > 0) & ((e == 0) | (prev_count == 0)))
    def _():
        gather(start, xa)

    def pair(p, carry):
        base = start + p * (2 * MOE_SUB)
        gather(base + MOE_SUB, xb)
        expert_ffn(xa, ya)
        scatter(pend[0], yb)
        gather(jnp.where(p + 1 < pairs, base + 2 * MOE_SUB, start_next), xa)
        expert_ffn(xb, yb)
        scatter(base, ya)
        pend[0] = base + MOE_SUB
        return carry

    lax.fori_loop(0, pairs, pair, 0)

    @pl.when(e == N_EXPERTS - 1)
    def _():
        scatter(pend[0], yb)


def _moe(lists, h2t, wg, wu, wd, layer):
    def expert(e, *_):
        return (layer, e, 0, 0)

    def whole(e, *_):
        return (0, 0)

    acc_rows = (N_TOK + MOE_SPARE_TOKENS) * ACC_ROWS
    assert acc_rows % MOE_ZERO_ROWS == 0
    return pl.pallas_call(
        _moe_kernel,
        out_shape=jax.ShapeDtypeStruct((acc_rows, LANES), F32),
        grid_spec=pltpu.PrefetchScalarGridSpec(
            num_scalar_prefetch=len(lists),
            grid=(N_EXPERTS,),
            in_specs=[pl.BlockSpec((acc_rows, LANES), whole, pipeline_mode=pl.Buffered(1)),
                      pl.BlockSpec((None, 1, D_MODEL, F_EXPERT), expert),
                      pl.BlockSpec((None, 1, D_MODEL, F_EXPERT), expert),
                      pl.BlockSpec((None, 1, F_EXPERT, D_MODEL), expert)],
            out_specs=pl.BlockSpec((acc_rows, LANES), whole, pipeline_mode=pl.Buffered(1)),
            scratch_shapes=[pltpu.VMEM((MOE_SUB * ACC_ROWS, LANES), F32)] * 4
                           + [pltpu.VMEM((D_MODEL, F_EXPERT), BF16),
                              pltpu.VMEM((D_MODEL, F_EXPERT), BF16),
                              pltpu.VMEM((F_EXPERT, D_MODEL), BF16),
                              pltpu.SMEM((1,), jnp.int32)]),
        compiler_params=pltpu.CompilerParams(
            dimension_semantics=("arbitrary",), vmem_limit_bytes=MOE_VMEM_LIMIT),
        name="moe",
    )(*lists, h2t, wg, wu, wd)


def _final_kernel(acc_ref, h2_ref, xmid_ref, mod_ref, swg_ref, swu_ref, swd_ref, gpost_ref, *out_refs):
    tm = h2_ref.shape[0]
    routed = jnp.concatenate(
        [acc_ref[pl.ds(c, tm, stride=ACC_ROWS), :] for c in range(ACC_ROWS)], axis=1)
    hid = _ffn(h2_ref[...], swg_ref[...], swu_ref[...])
    shared = _dot(hid.astype(BF16), swd_ref[...])
    g2 = mod_ref[0][:, 5 * D_MODEL:6 * D_MODEL]
    out = xmid_ref[...] + g2 * _rms(routed + shared, gpost_ref[...])
    if len(out_refs) == 1:
        out_refs[0][...] = out
    else:
        @pl.when(pl.program_id(0) < CTX_TILES)
        def _():
            out_refs[0][...] = out

        @pl.when(pl.program_id(0) >= CTX_TILES)
        def _():
            out_refs[1][...] = out


def _final(acc, h2, xmid, mod, wts, layer, split):
    tm = ROW_TILE

    def row(i):
        return (i, 0)

    weights = [wts[k] for k in ("swg", "swu", "swd", "g_ffn_post")]
    return pl.pallas_call(
        _final_kernel,
        out_shape=([jax.ShapeDtypeStruct((N_CTX, D_MODEL), F32),
                    jax.ShapeDtypeStruct((N_LAT, D_MODEL), F32)] if split
                   else jax.ShapeDtypeStruct((N_TOK, D_MODEL), F32)),
        grid=(N_TOK // tm,),
        in_specs=[pl.BlockSpec((tm * ACC_ROWS, LANES), row),
                  pl.BlockSpec((tm, D_MODEL), row),
                  pl.BlockSpec((tm, D_MODEL), row),
                  _mod_spec(layer)]
                 + [_layer_spec(w, layer) for w in weights],
        out_specs=_x_specs((None, None)) if split else pl.BlockSpec((tm, D_MODEL), row),
        compiler_params=pltpu.CompilerParams(
            dimension_semantics=("arbitrary",), vmem_limit_bytes=VMEM_LIMIT),
        name="ffn_final",
    )(acc, h2, xmid, mod, *weights)


def _routing_lists(experts, gates):
    e = experts[:TOP_K].reshape(-1)
    tok = jnp.tile(jnp.arange(N_TOK, dtype=jnp.int32), TOP_K)
    pad_e = jnp.repeat(jnp.arange(N_EXPERTS, dtype=jnp.int32), MOE_PAD)
    pad_tok = N_TOK + jnp.tile(jnp.arange(MOE_PAD, dtype=jnp.int32), N_EXPERTS)
    keys = jnp.concatenate([e * TOK_KEY + tok, pad_e * TOK_KEY + pad_tok])
    vals = jnp.concatenate([gates[:TOP_K].reshape(-1), jnp.zeros((N_EXPERTS * MOE_PAD,), F32)])
    keys, gate_sorted = lax.sort((keys, vals), num_keys=1)
    tail = jnp.full((MOE_SUB,), N_TOK, jnp.int32)
    tok_sorted = jnp.concatenate([keys & (TOK_KEY - 1), tail])
    gate_sorted = jnp.concatenate([gate_sorted, jnp.zeros((MOE_SUB,), F32)])
    dst = jnp.minimum(tok_sorted, N_TOK) * ACC_ROWS
    counts = jnp.sum((e[None, :] == jnp.arange(N_EXPERTS, dtype=jnp.int32)[:, None]).astype(jnp.int32),
                     axis=1)
    start = jnp.cumsum(counts) - counts + MOE_PAD * jnp.arange(N_EXPERTS, dtype=jnp.int32)
    return start.astype(jnp.int32), counts, dst, gate_sorted


def _rope_tables():
    t = np.arange(LAT_SEQ)
    pos = np.stack([t // GRID_W, t % GRID_W], axis=1).astype(np.float64)

    def table(half):
        inv = ROPE_BASE ** (-np.arange(half, dtype=np.float64) / half)
        lane = np.arange(LANES)
        axis = (lane // (2 * half)) % 2
        freq = inv[lane % half]
        ang = pos[:, axis] * freq[None, :]
        sign = np.where((lane % (2 * half)) < half, -1.0, 1.0)
        cos = np.concatenate([np.ones((ROW_TILE, LANES)), np.cos(ang)], axis=0)
        sin = np.concatenate([np.zeros((ROW_TILE, LANES)), np.sin(ang) * sign[None, :]], axis=0)
        return jnp.asarray(cos, F32), jnp.asarray(sin, F32)

    cos64, sin64 = table(16)
    cos32, sin32 = table(8)
    return cos64, sin64, cos32, sin32


def _prepare_weights(norm_attn_pre, norm_attn_post, norm_ffn_pre, norm_ffn_post, w_in, dif_subln,
                     mla_q_norm, mla_w_uq, mla_kv_norm, mla_w_ukv, w_branch_a, w_branch_b,
                     w_branch_c, w_out, router_w, router_bias, shared_w_gate, shared_w_up,
                     shared_w_down):
    dp = DEPTH
    qa = w_in[:, :, 0:512].reshape(dp, D_MODEL, 2, 4, 64).transpose(0, 1, 3, 2, 4).reshape(dp, D_MODEL, 512)
    kpe = w_in[:, :, 2688:2720]
    w1 = jnp.concatenate([qa, w_in[:, :, 512:2688], kpe, kpe, kpe, kpe], axis=2).astype(BF16)
    uq = mla_w_uq.reshape(dp, 256, 8, 96)
    wuq = jnp.concatenate([uq[..., :64].reshape(dp, 256, 512), uq[..., 64:].reshape(dp, 256, 256)],
                          axis=2).astype(BF16)
    ukv = mla_w_ukv.reshape(dp, 128, 8, 128)
    wukv = jnp.concatenate([ukv[..., :64].reshape(dp, 128, 512), ukv[..., 64:].reshape(dp, 128, 512)],
                           axis=2).astype(BF16)
    wa = w_branch_a.reshape(dp, 2, 4, 64, D_MODEL).transpose(0, 2, 1, 3, 4).reshape(dp, 512, D_MODEL)
    return dict(
        g_pre=norm_attn_pre[:, None, :], g_post=norm_attn_post[:, None, :],
        g_ffn=norm_ffn_pre[:, None, :], g_ffn_post=norm_ffn_post[:, None, :],
        w1=w1, wgl=w_in[:, :, 2720:].astype(BF16),
        qnorm=mla_q_norm[:, None, :], wuq=wuq, kvnorm=mla_kv_norm[:, None, :], wukv=wukv,
        subln=dif_subln[:, None, :],
        wa=wa.astype(BF16), wb=w_branch_b.astype(BF16), wc=w_branch_c.astype(BF16),
        wo=w_out.astype(BF16), rwt=router_w.transpose(0, 2, 1).astype(BF16),
        rb=router_bias[:, :, None],
        swg=shared_w_gate.astype(BF16), swu=shared_w_up.astype(BF16), swd=shared_w_down.astype(BF16))


def kernel(x_prompt, x_sample, cache_swa_k, cache_swa_v, cache_dif_k, cache_dif_v, cache_mla_ckv, cache_mla_kpe, c, c_ctx, w_mod, b_mod, norm_attn_pre, norm_attn_post, norm_ffn_pre, norm_ffn_post, w_in, swa_sink, dif_lq1, dif_lk1, dif_lq2, dif_lk2, dif_subln, mla_q_norm, mla_w_uq, mla_kv_norm, mla_w_ukv, w_branch_a, w_branch_b, w_branch_c, w_out, router_w, router_bias, moe_w_gate, moe_w_up, moe_w_down, shared_w_gate, shared_w_up, shared_w_down):
    x = (x_prompt.reshape(N_CTX, D_MODEL), x_sample.reshape(N_LAT, D_MODEL))
    cvec = jnp.concatenate([c_ctx[None, :], c, jnp.zeros((8 - 1 - N_LAT_BATCH, D_MODEL), F32)], axis=0)
    mod = _modulation(cvec, w_mod, b_mod).reshape(DEPTH, 8, 1, 6 * D_MODEL)
    tables = _rope_tables()
    wts = _prepare_weights(norm_attn_pre, norm_attn_post, norm_ffn_pre, norm_ffn_post, w_in,
                           dif_subln, mla_q_norm, mla_w_uq, mla_kv_norm, mla_w_ukv, w_branch_a,
                           w_branch_b, w_branch_c, w_out, router_w, router_bias, shared_w_gate,
                           shared_w_up, shared_w_down)
    caches = [cache_swa_k.reshape(N_LAT_BATCH, DEPTH, PAST, 128),
              cache_swa_v.reshape(N_LAT_BATCH, DEPTH, PAST, 128),
              cache_dif_k.reshape(N_LAT_BATCH, DEPTH, PAST, 512),
              cache_dif_v.reshape(N_LAT_BATCH, DEPTH, PAST, 512),
              cache_mla_ckv,
              jnp.tile(cache_mla_kpe, (1, 1, 1, 4))]
    lam_init = [0.8 - 0.6 * math.exp(-0.3 * l) for l in range(DEPTH)]
    lam = (jnp.exp(jnp.sum(dif_lq1 * dif_lk1, axis=1)) - jnp.exp(jnp.sum(dif_lq2 * dif_lk2, axis=1))
           + jnp.asarray(lam_init, F32))
    scal = jnp.concatenate([swa_sink, lam[:, None]], axis=1).astype(F32)

    states = None
    for l in range(DEPTH):
        proj = _in_projection(x, mod, wts, tables, l, states)
        states = proj[11:]
        ctx_out = _attention_ctx(scal[l], wts["subln"], proj, states, l, lam_init[l])
        oa, od, om = _attention_lat(scal[l], wts["subln"], wts["wukv"], proj, caches, ctx_out, l,
                                    lam_init[l])
        xmid, h2, h2t, experts, gates = _post_attention(x, mod, oa, od, om, wts, l)
        lists = _routing_lists(experts, gates)
        acc = _moe(lists, h2t, moe_w_gate, moe_w_up, moe_w_down, l)
        x = _final(acc, h2, xmid, mod, wts, l, split=(l == DEPTH - 1))

    y_p = x[0].reshape(N_CTX_BATCH, CTX_SEQ, D_MODEL)
    y_s = x[1].reshape(N_LAT_BATCH, LAT_SEQ, D_MODEL)
    s_ka, s_va, s_kd, s_vd, s_ckv, s_kpe = states
    return (y_p, y_s,
            s_ka.reshape(N_CTX_BATCH, DEPTH, CTX_SEQ, 2, 64),
            s_va.reshape(N_CTX_BATCH, DEPTH, CTX_SEQ, 2, 64),
            s_kd.reshape(N_CTX_BATCH, DEPTH, CTX_SEQ, 4, 2, 64),
            s_vd.reshape(N_CTX_BATCH, DEPTH, CTX_SEQ, 4, 128),
            s_ckv, s_kpe)
```

```python
import functools
import math

import numpy as np
import jax
import jax.numpy as jnp
from jax import lax
from jax.experimental import pallas as pl
from jax.experimental.pallas import tpu as pltpu

F32 = jnp.float32
BF16 = jnp.bfloat16

D_MODEL = 1024
N_CTX_BATCH, CTX_SEQ = 16, 256
N_LAT_BATCH, LAT_SEQ = 2, 1024
PAST = 256
N_CTX = N_CTX_BATCH * CTX_SEQ
N_LAT = N_LAT_BATCH * LAT_SEQ
N_TOK = N_CTX + N_LAT
DEPTH = 2
GRID_W = 64
WINDOW = 128
N_EXPERTS = 64
N_GROUPS = 8
TOPK_GROUPS = 4
TOP_K = 6
F_EXPERT = 256
ROUTED_SCALE = 2.5
ROPE_BASE = 10000.0
EPS = 1e-6
NEG = -1e30

LANES = 128
ROW_TILE = 256
Q_BLOCK = 256
MOD_COL_TILE = 3072
VMEM_LIMIT = 56 * 1024 * 1024
ACC_ROWS = 8
MOE_SPARE_TOKENS = 8
MOE_ZERO_ROWS = 64
MOE_VMEM_LIMIT = 62 * 1024 * 1024
MOE_SUB = 128
MOE_RMW_GROUP = 4
MOE_PAD = 2 * MOE_SUB
TOK_KEY = 8192
MOE_LIST_PAD_BASE = N_TOK * TOP_K + N_EXPERTS * MOE_PAD

C_QA, C_KA, C_VA, C_QD, C_KD, C_VD, C_QC, C_KVC, C_KPE, C_END = (
    0, 512, 640, 768, 1280, 1792, 2304, 2560, 2688, 2816)
SCALE_64 = 1.0 / math.sqrt(64.0)
SCALE_96 = 1.0 / math.sqrt(96.0)

CTX_TILES = N_CTX // ROW_TILE
N_TILES = N_TOK // ROW_TILE
assert ROW_TILE == CTX_SEQ and LAT_SEQ % ROW_TILE == 0


def _mod_row_of_tile(i):
    return jnp.where(i < CTX_TILES, 0, 1 + (i - CTX_TILES) // (LAT_SEQ // ROW_TILE))


def _rms(x, g):
    return x * lax.rsqrt(jnp.mean(x * x, axis=-1, keepdims=True) + EPS) * g


def _dot(a, b):
    return jnp.dot(a, b, preferred_element_type=F32)


def _dot_nt(a, b):
    return lax.dot_general(a, b, (((1,), (1,)), ((), ())), preferred_element_type=F32)


def _silu(x):
    return x * jax.nn.sigmoid(x)


def _layer_spec(arr, layer):
    zeros = (0,) * (arr.ndim - 1)
    return pl.BlockSpec((None,) + arr.shape[1:], lambda *_: (layer,) + zeros)


def _mod_kernel(c_ref, w_ref, b_ref, o_ref):
    c = c_ref[...]
    o_ref[0] = _dot(_silu(c).astype(BF16), w_ref[0].astype(BF16)) + b_ref[0]


def _modulation(cvec, w_mod, b_mod):
    tn = MOD_COL_TILE
    n = w_mod.shape[-1]
    return pl.pallas_call(
        _mod_kernel,
        out_shape=jax.ShapeDtypeStruct((DEPTH, 8, n), F32),
        grid=(DEPTH, n // tn),
        in_specs=[
            pl.BlockSpec((8, D_MODEL), lambda l, j: (0, 0)),
            pl.BlockSpec((1, D_MODEL, tn), lambda l, j: (l, 0, j)),
            pl.BlockSpec((1, 1, tn), lambda l, j: (l, 0, j)),
        ],
        out_specs=pl.BlockSpec((1, 8, tn), lambda l, j: (l, 0, j)),
        compiler_params=pltpu.CompilerParams(vmem_limit_bytes=VMEM_LIMIT),
        name="modulation",
    )(cvec, w_mod, b_mod.reshape(DEPTH, 1, n))


def _tile(i):
    return jnp.minimum(i, N_TILES - 1)


def _x_specs(x):
    if isinstance(x, tuple):
        return [pl.BlockSpec((ROW_TILE, D_MODEL), lambda i: (jnp.minimum(i, CTX_TILES - 1), 0)),
                pl.BlockSpec((ROW_TILE, D_MODEL), lambda i: (jnp.maximum(_tile(i) - CTX_TILES, 0), 0))]
    return [pl.BlockSpec((ROW_TILE, D_MODEL), lambda i: (_tile(i), 0))]


def _load_x(x_refs):
    if len(x_refs) == 1:
        return x_refs[0][...]
    return jnp.where(pl.program_id(0) < CTX_TILES, x_refs[0][...], x_refs[1][...])


def _mod_spec(layer):
    return pl.BlockSpec((None, 1, 1, 6 * D_MODEL),
                        lambda i: (layer, _mod_row_of_tile(_tile(i)), 0, 0))


def _w1_kernel(w_ref, o_ref):
    w = w_ref[0]
    lane = lax.broadcasted_iota(jnp.int32, (w.shape[0], LANES), 1)
    chunks = []
    for g in range(4):
        a = w[:, (g // 2) * LANES:(g // 2 + 1) * LANES]
        b = w[:, (2 + g // 2) * LANES:(3 + g // 2) * LANES]
        if g % 2 == 0:
            b = pltpu.roll(b, LANES // 2, 1)
        else:
            a = pltpu.roll(a, LANES // 2, 1)
        chunks.append(jnp.where(lane < LANES // 2, a, b))
    chunks.append(w[:, C_KA:C_KPE])
    kpe = jnp.where(lane < 32, w[:, C_KPE:C_END], 0.0)
    chunks.append(kpe + pltpu.roll(kpe, 32, 1) + pltpu.roll(kpe, 64, 1) + pltpu.roll(kpe, 96, 1))
    o_ref[0] = jnp.concatenate(chunks, axis=1).astype(BF16)


def _repack_in_weights(w_in):
    tk = ROW_TILE
    return pl.pallas_call(
        _w1_kernel,
        out_shape=jax.ShapeDtypeStruct((DEPTH, D_MODEL, C_END), BF16),
        grid=(DEPTH, D_MODEL // tk),
        in_specs=[pl.BlockSpec((1, tk, C_END), lambda l, i: (l, i, 0))],
        out_specs=pl.BlockSpec((1, tk, C_END), lambda l, i: (l, i, 0)),
        compiler_params=pltpu.CompilerParams(vmem_limit_bytes=VMEM_LIMIT),
        name="repack_in_weights",
    )(w_in)


def _rope128(x, cos, sin, half):
    lane = lax.broadcasted_iota(jnp.int32, x.shape, 1)
    first = (lane % (2 * half)) < half
    partner = jnp.where(first, pltpu.roll(x, LANES - half, 1), pltpu.roll(x, half, 1))
    return x * cos + partner * sin


def _rope_cols(x, cos, sin, half):
    chunks = [_rope128(x[:, c:c + LANES], cos, sin, half) for c in range(0, x.shape[1], LANES)]
    return chunks[0] if len(chunks) == 1 else jnp.concatenate(chunks, axis=1)


N_IN_INPUTS = 11


def _in_kernel(*refs, n_x):
    (mod_ref, g_ref, w1_ref, qn_ref, wuq_ref, kvn_ref, wukv_ref,
     cos64_ref, sin64_ref, cos32_ref, sin32_ref) = refs[n_x:n_x + N_IN_INPUTS]
    (qa_o, qd_o, qmn_o, qmr_o, kmn_o, vm_o, kpe4_o, ka_l, va_l, kd_l, vd_l,
     s_ka, s_va, s_kd, s_vd, s_ckv, s_kpe) = refs[-17:]
    i = pl.program_id(0)
    x = _load_x(refs[:n_x])
    m = mod_ref[0]
    sh1, sc1 = m[:, 0:D_MODEL], m[:, D_MODEL:2 * D_MODEL]
    h = _rms(x, g_ref[...]) * (1.0 + sc1) + sh1
    z = _dot(h.astype(BF16), w1_ref[...])
    cos64, sin64 = cos64_ref[...], sin64_ref[...]
    cos32, sin32 = cos32_ref[...], sin32_ref[...]
    qa_o[...] = (_rope_cols(z[:, C_QA:C_KA], cos64, sin64, 16) * SCALE_64).astype(BF16)
    qd_o[...] = (_rope_cols(z[:, C_QD:C_KD], cos64, sin64, 16) * SCALE_64).astype(BF16)
    ka = _rope_cols(z[:, C_KA:C_VA], cos64, sin64, 16)
    va = z[:, C_VA:C_QD]
    kd = _rope_cols(z[:, C_KD:C_VD], cos64, sin64, 16)
    vd = z[:, C_VD:C_QC]
    kpe4 = _rope_cols(z[:, C_KPE:C_END], cos32, sin32, 8)
    kpe4_o[...] = kpe4.astype(BF16)
    qn = _rms(z[:, C_QC:C_KVC], qn_ref[...])
    qm = _dot(qn.astype(BF16), wuq_ref[...]) * SCALE_96
    qmn_o[...] = qm[:, 0:512].astype(BF16)
    qmr_o[...] = _rope_cols(qm[:, 512:768], cos32, sin32, 8).astype(BF16)
    ckv = _rms(z[:, C_KVC:C_KPE], kvn_ref[...])
    kv = _dot(ckv.astype(BF16), wukv_ref[...])
    kmn_o[...] = kv[:, 0:512].astype(BF16)
    vm_o[...] = kv[:, 512:1024].astype(BF16)

    @pl.when(i < CTX_TILES)
    def _():
        s_ka[...] = ka
        s_va[...] = va
        s_kd[...] = kd
        s_vd[...] = vd
        s_ckv[...] = ckv
        s_kpe[...] = kpe4[:, 0:32]

    @pl.when(i >= CTX_TILES)
    def _():
        ka_l[...] = ka.astype(BF16)
        va_l[...] = va.astype(BF16)
        kd_l[...] = kd.astype(BF16)
        vd_l[...] = vd.astype(BF16)


STATE_WIDTHS = (128, 128, 512, 512, 128, 32)


def _in_projection(x, mod, wts, tables, layer, prev_states):
    tm = ROW_TILE

    def row(i):
        return (i, 0)

    def table_row(i):
        return (jnp.where(i < CTX_TILES, 0, 1 + (i - CTX_TILES) % (LAT_SEQ // tm)), 0)

    def lat_row(i):
        return (jnp.maximum(i - CTX_TILES, 0), 0)

    def state_block(i):
        return (jnp.minimum(i, CTX_TILES - 1), layer, 0, 0)

    table_spec = pl.BlockSpec((tm, LANES), table_row)
    weights = [wts["g_pre"], wts["w1"], wts["qnorm"], wts["wuq"], wts["kvnorm"], wts["wukv"]]
    all_rows = [(512, BF16)] * 3 + [(256, BF16), (512, BF16), (512, BF16), (128, BF16)]
    lat_rows = [(128, BF16), (128, BF16), (512, BF16), (512, BF16)]
    aliased = [] if prev_states is None else list(prev_states)
    xs = list(x) if isinstance(x, tuple) else [x]
    n_in = len(xs) + N_IN_INPUTS
    n_plain_out = len(all_rows) + len(lat_rows)
    return pl.pallas_call(
        functools.partial(_in_kernel, n_x=len(xs)),
        out_shape=[jax.ShapeDtypeStruct((N_TOK, w), dt) for w, dt in all_rows]
                  + [jax.ShapeDtypeStruct((N_LAT, w), dt) for w, dt in lat_rows]
                  + [jax.ShapeDtypeStruct((N_CTX_BATCH, DEPTH, CTX_SEQ, w), F32) for w in STATE_WIDTHS],
        grid=(N_TOK // tm,),
        in_specs=_x_specs(x) + [_mod_spec(layer)]
                 + [_layer_spec(w, layer) for w in weights]
                 + [table_spec] * 4
                 + [pl.BlockSpec(memory_space=pl.ANY)] * len(aliased),
        out_specs=[pl.BlockSpec((tm, w), row) for w, _ in all_rows]
                  + [pl.BlockSpec((tm, w), lat_row) for w, _ in lat_rows]
                  + [pl.BlockSpec((None, None, CTX_SEQ, w), state_block) for w in STATE_WIDTHS],
        input_output_aliases={n_in + k: n_plain_out + k for k in range(len(aliased))},
        compiler_params=pltpu.CompilerParams(
            dimension_semantics=("arbitrary",), vmem_limit_bytes=VMEM_LIMIT),
        name="in_projection",
    )(*xs, mod, *weights, *tables, *aliased)


def _attend(q, parts, sink=None):
    scores = []
    for k, _, mask in parts:
        s = _dot_nt(q, k)
        if mask is not None:
            s = jnp.where(mask, s, NEG)
        scores.append(s)
    m = functools.reduce(jnp.maximum, [jnp.max(s, axis=-1, keepdims=True) for s in scores])
    if sink is not None:
        m = jnp.maximum(m, sink)
    den = None
    out = None
    for s, (_, v, _) in zip(scores, parts):
        e = jnp.exp(s - m)
        d = jnp.sum(e, axis=-1, keepdims=True)
        o = _dot(e.astype(BF16), v)
        den = d if den is None else den + d
        out = o if out is None else out + o
    if sink is not None:
        den = den + jnp.exp(sink - m)
    return out / den


A_HEADS = 8


def _attend_heads(qs, parts, sinks, stack):
    rows = qs[0].shape[0]
    if not stack:
        return [_attend(q, parts, None if sinks is None else sinks[i]) for i, q in enumerate(qs)]
    sink = None
    if sinks is not None:
        sink = jnp.concatenate([jnp.full((rows, 1), s, F32) for s in sinks], axis=0)
    o = _attend(jnp.concatenate(qs, axis=0), parts, sink)
    return [o[i * rows:(i + 1) * rows] for i in range(len(qs))]


def _three_mixers(scal_ref, subln_ref, q_blocks, kv_parts, oa_o, od_o, om_o, lam_init, stack_a,
                  stack_bc):
    qa, qd, qmn, qmr = q_blocks
    rows = qa.shape[0]
    lane = lax.broadcasted_iota(jnp.int32, (rows, LANES), 1)
    low = lane < 64
    zero = jnp.zeros((rows, LANES), BF16)

    parts = [(p["ka"], p["va"], p.get("mask_a")) for p in kv_parts]
    qs = [jnp.where(low if kvh == 0 else ~low, qa[:, g * LANES:(g + 1) * LANES], zero)
          for kvh in range(2) for g in range(4)]
    out_a = _attend_heads(qs, parts, [scal_ref[h] for h in range(A_HEADS)], stack_a)
    for g in range(4):
        oa_o[:, g * LANES:(g + 1) * LANES] = jnp.where(low, out_a[g], out_a[4 + g]).astype(BF16)

    lam = scal_ref[8]
    subln = subln_ref[...]
    for h in range(4):
        sl = slice(h * LANES, (h + 1) * LANES)
        q128 = qd[:, sl]
        o1, o2 = _attend_heads([jnp.where(low, q128, zero), jnp.where(low, zero, q128)],
                               [(p["kd"][:, sl], p["vd"][:, sl], None) for p in kv_parts],
                               None, stack_bc)
        od = o1 - lam * o2
        od_o[:, sl] = (_rms(od, subln) * (1.0 - lam_init)).astype(BF16)

    for i in range(4):
        sl = slice(i * LANES, (i + 1) * LANES)
        qs = []
        for sub in range(2):
            h = 2 * i + sub
            qn = jnp.where(low if sub == 0 else ~low, qmn[:, sl], zero)
            qr128 = qmr[:, (h // 4) * LANES:(h // 4 + 1) * LANES]
            qr = jnp.where((lane // 32) == (h % 4), qr128, zero)
            qs.append(jnp.concatenate([qn, qr], axis=1))
        parts = [(jnp.concatenate([p["kmn"][:, sl], p["kpe"]], axis=1), p["vm"][:, sl], None)
                 for p in kv_parts]
        o_even, o_odd = _attend_heads(qs, parts, None, stack_bc)
        om_o[:, sl] = jnp.where(low, o_even, o_odd).astype(BF16)


def _attn_ctx_kernel(scal_ref, subln_ref, qa_ref, qd_ref, qmn_ref, qmr_ref,
                     ka_ref, va_ref, kd_ref, vd_ref, kmn_ref, vm_ref, kpe_ref,
                     oa_o, od_o, om_o, *, lam_init):
    part = dict(ka=ka_ref[...].astype(BF16), va=va_ref[...].astype(BF16),
                kd=kd_ref[...].astype(BF16), vd=vd_ref[...].astype(BF16),
                kmn=kmn_ref[...], vm=vm_ref[...], kpe=kpe_ref[...])
    _three_mixers(scal_ref, subln_ref, (qa_ref[...], qd_ref[...], qmn_ref[...], qmr_ref[...]),
                  [part], oa_o, od_o, om_o, lam_init, stack_a=False, stack_bc=True)


def _attn_lat_kernel(scal_ref, subln_ref, wukv_ref, qa_ref, qd_ref, qmn_ref, qmr_ref,
                     ka_ref, va_ref, kd_ref, vd_ref, kmn_ref, vm_ref, kpe_ref,
                     cka_ref, cva_ref, ckd_ref, cvd_ref, cckv_ref, ckpe_ref,
                     oa_in, od_in, om_in, oa_o, od_o, om_o, *, lam_init):
    del oa_in, od_in, om_in
    n = pl.program_id(1)
    kv_c = _dot(cckv_ref[...].astype(BF16), wukv_ref[...])
    cached = dict(ka=cka_ref[...].astype(BF16), va=cva_ref[...].astype(BF16),
                  kd=ckd_ref[...].astype(BF16), vd=cvd_ref[...].astype(BF16),
                  kmn=kv_c[:, 0:512].astype(BF16), vm=kv_c[:, 512:1024].astype(BF16),
                  kpe=ckpe_ref[...].astype(BF16))
    span = Q_BLOCK + 2 * WINDOW
    start = pl.multiple_of(jnp.clip(n * Q_BLOCK - WINDOW, 0, LAT_SEQ - span), WINDOW)
    qpos = n * Q_BLOCK + lax.broadcasted_iota(jnp.int32, (A_HEADS * Q_BLOCK, span), 0) % Q_BLOCK
    kpos = start + lax.broadcasted_iota(jnp.int32, (A_HEADS * Q_BLOCK, span), 1)
    mask_a = jnp.abs(qpos - kpos) <= WINDOW
    new = dict(ka=ka_ref[pl.ds(start, span), :], va=va_ref[pl.ds(start, span), :], mask_a=mask_a,
               kd=kd_ref[...], vd=vd_ref[...], kmn=kmn_ref[...], vm=vm_ref[...], kpe=kpe_ref[...])
    _three_mixers(scal_ref, subln_ref, (qa_ref[...], qd_ref[...], qmn_ref[...], qmr_ref[...]),
                  [cached, new], oa_o, od_o, om_o, lam_init, stack_a=True, stack_bc=True)


_SMEM_SPEC = pl.BlockSpec(memory_space=pltpu.SMEM)


def _attention_ctx(scal, subln, proj, states, layer, lam_init):
    qa, qd, qmn, qmr, kmn, vm, kpe4 = proj[:7]
    s_ka, s_va, s_kd, s_vd = states[:4]
    t = CTX_SEQ

    def blk(a):
        return pl.BlockSpec((t, a.shape[1]), lambda b: (b, 0))

    def sblk(a):
        return pl.BlockSpec((None, None, t, a.shape[-1]), lambda b: (b, layer, 0, 0))

    return pl.pallas_call(
        functools.partial(_attn_ctx_kernel, lam_init=lam_init),
        out_shape=[jax.ShapeDtypeStruct((N_TOK, 512), BF16)] * 3,
        grid=(N_CTX_BATCH,),
        in_specs=[_SMEM_SPEC, _layer_spec(subln, layer)]
                 + [blk(a) for a in (qa, qd, qmn, qmr)]
                 + [sblk(a) for a in (s_ka, s_va, s_kd, s_vd)]
                 + [blk(a) for a in (kmn, vm, kpe4)],
        out_specs=[pl.BlockSpec((t, 512), lambda b: (b, 0))] * 3,
        compiler_params=pltpu.CompilerParams(vmem_limit_bytes=VMEM_LIMIT),
        name="attention_ctx",
    )(scal, subln, qa, qd, qmn, qmr, s_ka, s_va, s_kd, s_vd, kmn, vm, kpe4)


def _attention_lat(scal, subln, wukv, proj, caches, ctx_out, layer, lam_init):
    qa, qd, qmn, qmr, kmn, vm, kpe4, ka_l, va_l, kd_l, vd_l = proj[:11]
    nq = LAT_SEQ // Q_BLOCK
    q_off = N_CTX // Q_BLOCK
    kv_off = N_CTX // LAT_SEQ

    def qblk(a):
        return pl.BlockSpec((Q_BLOCK, a.shape[1]), lambda b, n: (q_off + b * nq + n, 0))

    def kvblk(a):
        return pl.BlockSpec((LAT_SEQ, a.shape[1]), lambda b, n: (b, 0))

    def kvblk_all(a):
        return pl.BlockSpec((LAT_SEQ, a.shape[1]), lambda b, n: (kv_off + b, 0))

    def cblk(a):
        return pl.BlockSpec((None, None, PAST, a.shape[-1]), lambda b, n: (b, layer, 0, 0))

    qs = [qa, qd, qmn, qmr]
    n_in = 3 + 4 + 7 + 6
    out_spec = pl.BlockSpec((Q_BLOCK, 512), lambda b, n: (q_off + b * nq + n, 0))
    return pl.pallas_call(
        functools.partial(_attn_lat_kernel, lam_init=lam_init),
        out_shape=[jax.ShapeDtypeStruct((N_TOK, 512), BF16)] * 3,
        grid=(N_LAT_BATCH, nq),
        in_specs=[_SMEM_SPEC, _layer_spec(subln, layer), _layer_spec(wukv, layer)]
                 + [qblk(a) for a in qs]
                 + [kvblk(a) for a in (ka_l, va_l, kd_l, vd_l)]
                 + [kvblk_all(a) for a in (kmn, vm, kpe4)]
                 + [cblk(a) for a in caches]
                 + [pl.BlockSpec(memory_space=pl.ANY)] * 3,
        out_specs=[out_spec] * 3,
        input_output_aliases={n_in + k: k for k in range(3)},
        compiler_params=pltpu.CompilerParams(vmem_limit_bytes=VMEM_LIMIT),
        name="attention_lat",
    )(scal, subln, wukv, *qs, ka_l, va_l, kd_l, vd_l, kmn, vm, kpe4, *caches, *ctx_out)


def _route(scores, biased):
    per = N_EXPERTS // N_GROUPS
    tm = scores.shape[1]
    sub = lax.broadcasted_iota(jnp.int32, (per, tm), 0).astype(F32)
    groups = [biased[g * per:(g + 1) * per, :] for g in range(N_GROUPS)]
    gscore = []
    for v in groups:
        m1 = jnp.max(v, axis=0, keepdims=True)
        first = jnp.min(jnp.where(v == m1, sub, float(per)), axis=0, keepdims=True)
        m2 = jnp.max(jnp.where(sub == first, -jnp.inf, v), axis=0, keepdims=True)
        gscore.append(m1 + m2)
    vals = []
    for g in range(N_GROUPS):
        rank = jnp.zeros((1, tm), F32)
        for o in range(N_GROUPS):
            if o == g:
                continue
            ahead = (gscore[o] >= gscore[g]) if o < g else (gscore[o] > gscore[g])
            rank = rank + jnp.where(ahead, 1.0, 0.0)
        vals.append(jnp.where(rank < TOPK_GROUPS, groups[g], NEG))
    idx = [sub + float(g * per) for g in range(N_GROUPS)]
    picks, weights = [], []
    for _ in range(TOP_K):
        best = functools.reduce(jnp.maximum, [jnp.max(v, axis=0, keepdims=True) for v in vals])
        cand = functools.reduce(jnp.minimum, [
            jnp.min(jnp.where(v == best, i, float(N_EXPERTS)), axis=0, keepdims=True)
            for v, i in zip(vals, idx)])
        wsel = jnp.zeros((1, tm), F32)
        for g in range(N_GROUPS):
            hit = idx[g] == cand
            wsel = wsel + jnp.sum(jnp.where(hit, scores[g * per:(g + 1) * per, :], 0.0),
                                  axis=0, keepdims=True)
            vals[g] = jnp.where(hit, -jnp.inf, vals[g])
        picks.append(cand)
        weights.append(wsel)
    total = functools.reduce(lambda a, b: a + b, weights)
    pad = [jnp.zeros((1, tm), F32)] * (8 - TOP_K)
    experts = jnp.concatenate(picks + pad, axis=0).astype(jnp.int32)
    gates = jnp.concatenate([x / total * ROUTED_SCALE for x in weights] + pad, axis=0)
    return experts, gates


def _post_kernel(*refs, n_x):
    (mod_ref, oa_ref, od_ref, om_ref, gpre_ref, wgl_ref, wa_ref, wb_ref, wc_ref, wo_ref, gpost_ref,
     gffn_ref, rwt_ref, rb_ref, xmid_o, h2_o, h2t_o, experts_o, gates_o) = refs[n_x:]

    @pl.when(pl.program_id(0) < N_TILES)
    def _():
        x = _load_x(refs[:n_x])
        m = mod_ref[0]
        d = D_MODEL
        sh1, sc1, g1, sh2, sc2 = (m[:, 0:d], m[:, d:2 * d], m[:, 2 * d:3 * d], m[:, 3 * d:4 * d],
                                  m[:, 4 * d:5 * d])
        h = _rms(x, gpre_ref[...]) * (1.0 + sc1) + sh1
        gate = jax.nn.sigmoid(_dot(h.astype(BF16), wgl_ref[...]))
        merged = (gate[:, 0:d] * _dot(oa_ref[...], wa_ref[...])
                  + gate[:, d:2 * d] * _dot(od_ref[...], wb_ref[...])
                  + gate[:, 2 * d:3 * d] * _dot(om_ref[...], wc_ref[...]))
        a = _dot(merged.astype(BF16), wo_ref[...])
        xm = x + g1 * _rms(a, gpost_ref[...])
        xmid_o[...] = xm
        h2f = _rms(xm, gffn_ref[...]) * (1.0 + sc2) + sh2
        h2 = h2f.astype(BF16)
        h2_o[...] = h2
        for c in range(ACC_ROWS):
            h2t_o[pl.ds(c, x.shape[0], stride=ACC_ROWS), :] = h2f[:, c * LANES:(c + 1) * LANES]
        scores = jax.nn.sigmoid(_dot_nt(rwt_ref[...], h2))
        experts, gates = _route(scores, scores + rb_ref[...])
        experts_o[...] = experts
        gates_o[...] = gates

    @pl.when(pl.program_id(0) == N_TILES)
    def _():
        h2t_o[...] = jnp.zeros_like(h2t_o)


def _post_attention(x, mod, oa, od, om, wts, layer):
    tm = ROW_TILE

    def row(i):
        return (_tile(i), 0)

    weights = [wts[k] for k in ("g_pre", "wgl", "wa", "wb", "wc", "wo", "g_post", "g_ffn", "rwt", "rb")]
    xs = list(x) if isinstance(x, tuple) else [x]
    return pl.pallas_call(
        functools.partial(_post_kernel, n_x=len(xs)),
        out_shape=[jax.ShapeDtypeStruct((N_TOK, D_MODEL), F32),
                   jax.ShapeDtypeStruct((N_TOK, D_MODEL), BF16),
                   jax.ShapeDtypeStruct(((N_TOK + tm) * ACC_ROWS, LANES), F32),
                   jax.ShapeDtypeStruct((8, N_TOK), jnp.int32),
                   jax.ShapeDtypeStruct((8, N_TOK), F32)],
        grid=(N_TILES + 1,),
        in_specs=_x_specs(x) + [_mod_spec(layer),
                  pl.BlockSpec((tm, 512), row), pl.BlockSpec((tm, 512), row),
                  pl.BlockSpec((tm, 512), row)]
                 + [_layer_spec(w, layer) for w in weights],
        out_specs=[pl.BlockSpec((tm, D_MODEL), row), pl.BlockSpec((tm, D_MODEL), row),
                   pl.BlockSpec((tm * ACC_ROWS, LANES), lambda i: (i, 0)),
                   pl.BlockSpec((8, tm), lambda i: (0, _tile(i))),
                   pl.BlockSpec((8, tm), lambda i: (0, _tile(i)))],
        compiler_params=pltpu.CompilerParams(
            dimension_semantics=("arbitrary",), vmem_limit_bytes=VMEM_LIMIT),
        name="post_attention",
    )(*xs, mod, oa, od, om, *weights)


def _ffn(x, wg, wu):
    return _silu(_dot(x, wg)) * _dot(x, wu)


def _moe_kernel(start_ref, count_ref, dst_ref, gate_ref, xs_ref, wg_ref, wu_ref, wd_ref,
                acc_ref, xa, xb, ya, yb, wgb, wub, wdb, pend):
    e = pl.program_id(0)

    @pl.when(e == 0)
    def _():
        def clear(i, carry):
            r0 = pl.multiple_of(i * MOE_ZERO_ROWS, MOE_ZERO_ROWS)
            acc_ref[pl.ds(r0, MOE_ZERO_ROWS), :] = jnp.zeros((MOE_ZERO_ROWS, LANES), F32)
            return carry

        lax.fori_loop(0, acc_ref.shape[0] // MOE_ZERO_ROWS, clear, 0)

        yb[...] = jnp.zeros_like(yb)
        pend[0] = MOE_LIST_PAD_BASE

    wgb[...] = wg_ref[0].astype(BF16)
    wub[...] = wu_ref[0].astype(BF16)
    wdb[...] = wd_ref[0].astype(BF16)
    start = start_ref[e]
    pairs = (count_ref[e] + 2 * MOE_SUB - 1) // (2 * MOE_SUB)

    def gather(base, xbuf):
        for mi in range(MOE_SUB):
            src = pl.multiple_of(dst_ref[base + mi], ACC_ROWS)
            xbuf[mi * ACC_ROWS:(mi + 1) * ACC_ROWS, :] = xs_ref[pl.ds(src, ACC_ROWS), :]

    def expert_ffn(xbuf, ybuf):
        x = jnp.concatenate([xbuf[pl.ds(c, MOE_SUB, stride=ACC_ROWS), :].astype(BF16)
                             for c in range(ACC_ROWS)], axis=1)
        hid = _ffn(x, wgb[...], wub[...])
        y = _dot(hid.astype(BF16), wdb[...])
        for c in range(ACC_ROWS):
            ybuf[pl.ds(c, MOE_SUB, stride=ACC_ROWS), :] = y[:, c * LANES:(c + 1) * LANES]

    def scatter(base, ybuf):
        for g0 in range(0, MOE_SUB, MOE_RMW_GROUP):
            updates = []
            for mi in range(g0, g0 + MOE_RMW_GROUP):
                dst = pl.multiple_of(dst_ref[base + mi], ACC_ROWS)
                yv = ybuf[mi * ACC_ROWS:(mi + 1) * ACC_ROWS, :]
                updates.append((dst, acc_ref[pl.ds(dst, ACC_ROWS), :] + gate_ref[base + mi] * yv))
            for dst, val in updates:
                acc_ref[pl.ds(dst, ACC_ROWS), :] = val

    start_next = start_ref[jnp.minimum(e + 1, N_EXPERTS - 1)]
    prev_count = count_ref[jnp.maximum(e - 1, 0)]

    @pl.when((pairs > 0) & ((e == 0) | (prev_count == 0)))
    def _():
        gather(start, xa)

    def pair(p, carry):
        base = start + p * (2 * MOE_SUB)
        gather(base + MOE_SUB, xb)
        expert_ffn(xa, ya)
        scatter(pend[0], yb)
        gather(jnp.where(p + 1 < pairs, base + 2 * MOE_SUB, start_next), xa)
        expert_ffn(xb, yb)
        scatter(base, ya)
        pend[0] = base + MOE_SUB
        return carry

    lax.fori_loop(0, pairs, pair, 0)

    @pl.when(e == N_EXPERTS - 1)
    def _():
        scatter(pend[0], yb)


def _moe(lists, h2t, wg, wu, wd, layer):
    def expert(e, *_):
        return (layer, e, 0, 0)

    def whole(e, *_):
        return (0, 0)

    acc_rows = (N_TOK + MOE_SPARE_TOKENS) * ACC_ROWS
    assert acc_rows % MOE_ZERO_ROWS == 0
    return pl.pallas_call(
        _moe_kernel,
        out_shape=jax.ShapeDtypeStruct((acc_rows, LANES), F32),
        grid_spec=pltpu.PrefetchScalarGridSpec(
            num_scalar_prefetch=len(lists),
            grid=(N_EXPERTS,),
            in_specs=[pl.BlockSpec((acc_rows, LANES), whole, pipeline_mode=pl.Buffered(1)),
                      pl.BlockSpec((None, 1, D_MODEL, F_EXPERT), expert),
                      pl.BlockSpec((None, 1, D_MODEL, F_EXPERT), expert),
                      pl.BlockSpec((None, 1, F_EXPERT, D_MODEL), expert)],
            out_specs=pl.BlockSpec((acc_rows, LANES), whole, pipeline_mode=pl.Buffered(1)),
            scratch_shapes=[pltpu.VMEM((MOE_SUB * ACC_ROWS, LANES), F32)] * 4
                           + [pltpu.VMEM((D_MODEL, F_EXPERT), BF16),
                              pltpu.VMEM((D_MODEL, F_EXPERT), BF16),
                              pltpu.VMEM((F_EXPERT, D_MODEL), BF16),
                              pltpu.SMEM((1,), jnp.int32)]),
        compiler_params=pltpu.CompilerParams(
            dimension_semantics=("arbitrary",), vmem_limit_bytes=MOE_VMEM_LIMIT),
        name="moe",
    )(*lists, h2t, wg, wu, wd)


def _final_kernel(acc_ref, h2_ref, xmid_ref, mod_ref, swg_ref, swu_ref, swd_ref, gpost_ref, *out_refs):
    tm = h2_ref.shape[0]
    routed = jnp.concatenate(
        [acc_ref[pl.ds(c, tm, stride=ACC_ROWS), :] for c in range(ACC_ROWS)], axis=1)
    hid = _ffn(h2_ref[...], swg_ref[...], swu_ref[...])
    shared = _dot(hid.astype(BF16), swd_ref[...])
    g2 = mod_ref[0][:, 5 * D_MODEL:6 * D_MODEL]
    out = xmid_ref[...] + g2 * _rms(routed + shared, gpost_ref[...])
    if len(out_refs) == 1:
        out_refs[0][...] = out
    else:
        @pl.when(pl.program_id(0) < CTX_TILES)
        def _():
            out_refs[0][...] = out

        @pl.when(pl.program_id(0) >= CTX_TILES)
        def _():
            out_refs[1][...] = out


def _final(acc, h2, xmid, mod, wts, layer, split):
    tm = ROW_TILE

    def row(i):
        return (i, 0)

    weights = [wts[k] for k in ("swg", "swu", "swd", "g_ffn_post")]
    return pl.pallas_call(
        _final_kernel,
        out_shape=([jax.ShapeDtypeStruct((N_CTX, D_MODEL), F32),
                    jax.ShapeDtypeStruct((N_LAT, D_MODEL), F32)] if split
                   else jax.ShapeDtypeStruct((N_TOK, D_MODEL), F32)),
        grid=(N_TOK // tm,),
        in_specs=[pl.BlockSpec((tm * ACC_ROWS, LANES), row),
                  pl.BlockSpec((tm, D_MODEL), row),
                  pl.BlockSpec((tm, D_MODEL), row),
                  _mod_spec(layer)]
                 + [_layer_spec(w, layer) for w in weights],
        out_specs=_x_specs((None, None)) if split else pl.BlockSpec((tm, D_MODEL), row),
        compiler_params=pltpu.CompilerParams(
            dimension_semantics=("arbitrary",), vmem_limit_bytes=VMEM_LIMIT),
        name="ffn_final",
    )(acc, h2, xmid, mod, *weights)


def _routing_lists(experts, gates):
    e = experts[:TOP_K].reshape(-1)
    tok = jnp.tile(jnp.arange(N_TOK, dtype=jnp.int32), TOP_K)
    pad_e = jnp.repeat(jnp.arange(N_EXPERTS, dtype=jnp.int32), MOE_PAD)
    pad_tok = N_TOK + jnp.tile(jnp.arange(MOE_PAD, dtype=jnp.int32), N_EXPERTS)
    keys = jnp.concatenate([e * TOK_KEY + tok, pad_e * TOK_KEY + pad_tok])
    vals = jnp.concatenate([gates[:TOP_K].reshape(-1), jnp.zeros((N_EXPERTS * MOE_PAD,), F32)])
    keys, gate_sorted = lax.sort((keys, vals), num_keys=1)
    tail = jnp.full((MOE_SUB,), N_TOK, jnp.int32)
    tok_sorted = jnp.concatenate([keys & (TOK_KEY - 1), tail])
    gate_sorted = jnp.concatenate([gate_sorted, jnp.zeros((MOE_SUB,), F32)])
    dst = jnp.minimum(tok_sorted, N_TOK) * ACC_ROWS
    counts = jnp.sum((e[None, :] == jnp.arange(N_EXPERTS, dtype=jnp.int32)[:, None]).astype(jnp.int32),
                     axis=1)
    start = jnp.cumsum(counts) - counts + MOE_PAD * jnp.arange(N_EXPERTS, dtype=jnp.int32)
    return start.astype(jnp.int32), counts, dst, gate_sorted


def _rope_tables():
    t = np.arange(LAT_SEQ)
    pos = np.stack([t // GRID_W, t % GRID_W], axis=1).astype(np.float64)

    def table(half):
        inv = ROPE_BASE ** (-np.arange(half, dtype=np.float64) / half)
        lane = np.arange(LANES)
        axis = (lane // (2 * half)) % 2
        freq = inv[lane % half]
        ang = pos[:, axis] * freq[None, :]
        sign = np.where((lane % (2 * half)) < half, -1.0, 1.0)
        cos = np.concatenate([np.ones((ROW_TILE, LANES)), np.cos(ang)], axis=0)
        sin = np.concatenate([np.zeros((ROW_TILE, LANES)), np.sin(ang) * sign[None, :]], axis=0)
        return jnp.asarray(cos, F32), jnp.asarray(sin, F32)

    cos64, sin64 = table(16)
    cos32, sin32 = table(8)
    return cos64, sin64, cos32, sin32


def _prepare_weights(norm_attn_pre, norm_attn_post, norm_ffn_pre, norm_ffn_post, w_in, dif_subln,
                     mla_q_norm, mla_w_uq, mla_kv_norm, mla_w_ukv, w_branch_a, w_branch_b,
                     w_branch_c, w_out, router_w, router_bias, shared_w_gate, shared_w_up,
                     shared_w_down):
    dp = DEPTH
    w1 = _repack_in_weights(w_in)
    uq = mla_w_uq.reshape(dp, 256, 8, 96)
    wuq = jnp.concatenate([uq[..., :64].reshape(dp, 256, 512), uq[..., 64:].reshape(dp, 256, 256)],
                          axis=2).astype(BF16)
    ukv = mla_w_ukv.reshape(dp, 128, 8, 128)
    wukv = jnp.concatenate([ukv[..., :64].reshape(dp, 128, 512), ukv[..., 64:].reshape(dp, 128, 512)],
                           axis=2).astype(BF16)
    wa = w_branch_a.reshape(dp, 2, 4, 64, D_MODEL).transpose(0, 2, 1, 3, 4).reshape(dp, 512, D_MODEL)
    return dict(
        g_pre=norm_attn_pre[:, None, :], g_post=norm_attn_post[:, None, :],
        g_ffn=norm_ffn_pre[:, None, :], g_ffn_post=norm_ffn_post[:, None, :],
        w1=w1, wgl=w_in[:, :, 2720:].astype(BF16),
        qnorm=mla_q_norm[:, None, :], wuq=wuq, kvnorm=mla_kv_norm[:, None, :], wukv=wukv,
        subln=dif_subln[:, None, :],
        wa=wa.astype(BF16), wb=w_branch_b.astype(BF16), wc=w_branch_c.astype(BF16),
        wo=w_out.astype(BF16), rwt=router_w.transpose(0, 2, 1).astype(BF16),
        rb=router_bias[:, :, None],
        swg=shared_w_gate.astype(BF16), swu=shared_w_up.astype(BF16), swd=shared_w_down.astype(BF16))


def kernel(x_prompt, x_sample, cache_swa_k, cache_swa_v, cache_dif_k, cache_dif_v, cache_mla_ckv, cache_mla_kpe, c, c_ctx, w_mod, b_mod, norm_attn_pre, norm_attn_post, norm_ffn_pre, norm_ffn_post, w_in, swa_sink, dif_lq1, dif_lk1, dif_lq2, dif_lk2, dif_subln, mla_q_norm, mla_w_uq, mla_kv_norm, mla_w_ukv, w_branch_a, w_branch_b, w_branch_c, w_out, router_w, router_bias, moe_w_gate, moe_w_up, moe_w_down, shared_w_gate, shared_w_up, shared_w_down):
    x = (x_prompt.reshape(N_CTX, D_MODEL), x_sample.reshape(N_LAT, D_MODEL))
    cvec = jnp.concatenate([c_ctx[None, :], c, jnp.zeros((8 - 1 - N_LAT_BATCH, D_MODEL), F32)], axis=0)
    mod = _modulation(cvec, w_mod, b_mod).reshape(DEPTH, 8, 1, 6 * D_MODEL)
    tables = _rope_tables()
    wts = _prepare_weights(norm_attn_pre, norm_attn_post, norm_ffn_pre, norm_ffn_post, w_in,
                           dif_subln, mla_q_norm, mla_w_uq, mla_kv_norm, mla_w_ukv, w_branch_a,
                           w_branch_b, w_branch_c, w_out, router_w, router_bias, shared_w_gate,
                           shared_w_up, shared_w_down)
    caches = [cache_swa_k.reshape(N_LAT_BATCH, DEPTH, PAST, 128),
              cache_swa_v.reshape(N_LAT_BATCH, DEPTH, PAST, 128),
              cache_dif_k.reshape(N_LAT_BATCH, DEPTH, PAST, 512),
              cache_dif_v.reshape(N_LAT_BATCH, DEPTH, PAST, 512),
              cache_mla_ckv,
              jnp.tile(cache_mla_kpe, (1, 1, 1, 4))]
    lam_init = [0.8 - 0.6 * math.exp(-0.3 * l) for l in range(DEPTH)]
    lam = (jnp.exp(jnp.sum(dif_lq1 * dif_lk1, axis=1)) - jnp.exp(jnp.sum(dif_lq2 * dif_lk2, axis=1))
           + jnp.asarray(lam_init, F32))
    scal = jnp.concatenate([swa_sink, lam[:, None]], axis=1).astype(F32)

    states = None
    for l in range(DEPTH):
        proj = _in_projection(x, mod, wts, tables, l, states)
        states = proj[11:]
        ctx_out = _attention_ctx(scal[l], wts["subln"], proj, states, l, lam_init[l])
        oa, od, om = _attention_lat(scal[l], wts["subln"], wts["wukv"], proj, caches, ctx_out, l,
                                    lam_init[l])
        xmid, h2, h2t, experts, gates = _post_attention(x, mod, oa, od, om, wts, l)
        lists = _routing_lists(experts, gates)
        acc = _moe(lists, h2t, moe_w_gate, moe_w_up, moe_w_down, l)
        x = _final(acc, h2, xmid, mod, wts, l, split=(l == DEPTH - 1))

    y_p = x[0].reshape(N_CTX_BATCH, CTX_SEQ, D_MODEL)
    y_s = x[1].reshape(N_LAT_BATCH, LAT_SEQ, D_MODEL)
    s_ka, s_va, s_kd, s_vd, s_ckv, s_kpe = states
    return (y_p, y_s,
            s_ka.reshape(N_CTX_BATCH, DEPTH, CTX_SEQ, 2, 64),
            s_va.reshape(N_CTX_BATCH, DEPTH, CTX_SEQ, 2, 64),
            s_kd.reshape(N_CTX_BATCH, DEPTH, CTX_SEQ, 4, 2, 64),
            s_vd.reshape(N_CTX_BATCH, DEPTH, CTX_SEQ, 4, 128),
            s_ckv, s_kpe)
```

```python
import functools
import math

import numpy as np
import jax
import jax.numpy as jnp
from jax import lax
from jax.experimental import pallas as pl
from jax.experimental.pallas import tpu as pltpu

F32 = jnp.float32
BF16 = jnp.bfloat16

D_MODEL = 1024
N_CTX_BATCH, CTX_SEQ = 16, 256
N_LAT_BATCH, LAT_SEQ = 2, 1024
PAST = 256
N_CTX = N_CTX_BATCH * CTX_SEQ
N_LAT = N_LAT_BATCH * LAT_SEQ
N_TOK = N_CTX + N_LAT
DEPTH = 2
GRID_W = 64
WINDOW = 128
N_EXPERTS = 64
N_GROUPS = 8
TOPK_GROUPS = 4
TOP_K = 6
F_EXPERT = 256
ROUTED_SCALE = 2.5
ROPE_BASE = 10000.0
EPS = 1e-6
NEG = -1e30

LANES = 128
ROW_TILE = 256
Q_BLOCK = 256
MOD_COL_TILE = 3072
VMEM_LIMIT = 56 * 1024 * 1024
ACC_ROWS = 8
MOE_SPARE_TOKENS = 8
MOE_ZERO_ROWS = 64
MOE_VMEM_LIMIT = 62 * 1024 * 1024
MOE_SUB = 128
MOE_RMW_GROUP = 4
MOE_PAD = 2 * MOE_SUB
TOK_KEY = 8192
MOE_LIST_PAD_BASE = N_TOK * TOP_K + N_EXPERTS * MOE_PAD

C_QA, C_KA, C_VA, C_QD, C_KD, C_VD, C_QC, C_KVC, C_KPE, C_END = (
    0, 512, 640, 768, 1280, 1792, 2304, 2560, 2688, 2816)
SCALE_64 = 1.0 / math.sqrt(64.0)
SCALE_96 = 1.0 / math.sqrt(96.0)

CTX_TILES = N_CTX // ROW_TILE
N_TILES = N_TOK // ROW_TILE
assert ROW_TILE == CTX_SEQ and LAT_SEQ % ROW_TILE == 0


def _mod_row_of_tile(i):
    return jnp.where(i < CTX_TILES, 0, 1 + (i - CTX_TILES) // (LAT_SEQ // ROW_TILE))


def _rms(x, g):
    return x * lax.rsqrt(jnp.mean(x * x, axis=-1, keepdims=True) + EPS) * g


def _dot(a, b):
    return jnp.dot(a, b, preferred_element_type=F32)


def _dot_nt(a, b):
    return lax.dot_general(a, b, (((1,), (1,)), ((), ())), preferred_element_type=F32)


def _silu(x):
    return x * jax.nn.sigmoid(x)


def _layer_spec(arr, layer):
    zeros = (0,) * (arr.ndim - 1)
    return pl.BlockSpec((None,) + arr.shape[1:], lambda *_: (layer,) + zeros)


def _mod_kernel(c_ref, w_ref, b_ref, o_ref):
    c = c_ref[...]
    o_ref[0] = _dot(_silu(c).astype(BF16), w_ref[0].astype(BF16)) + b_ref[0]


def _modulation(cvec, w_mod, b_mod):
    tn = MOD_COL_TILE
    n = w_mod.shape[-1]
    return pl.pallas_call(
        _mod_kernel,
        out_shape=jax.ShapeDtypeStruct((DEPTH, 8, n), F32),
        grid=(DEPTH, n // tn),
        in_specs=[
            pl.BlockSpec((8, D_MODEL), lambda l, j: (0, 0)),
            pl.BlockSpec((1, D_MODEL, tn), lambda l, j: (l, 0, j)),
            pl.BlockSpec((1, 1, tn), lambda l, j: (l, 0, j)),
        ],
        out_specs=pl.BlockSpec((1, 8, tn), lambda l, j: (l, 0, j)),
        compiler_params=pltpu.CompilerParams(vmem_limit_bytes=VMEM_LIMIT),
        name="modulation",
    )(cvec, w_mod, b_mod.reshape(DEPTH, 1, n))


def _tile(i):
    return jnp.minimum(i, N_TILES - 1)


def _x_specs(x):
    if isinstance(x, tuple):
        return [pl.BlockSpec((ROW_TILE, D_MODEL), lambda i: (jnp.minimum(i, CTX_TILES - 1), 0)),
                pl.BlockSpec((ROW_TILE, D_MODEL), lambda i: (jnp.maximum(_tile(i) - CTX_TILES, 0), 0))]
    return [pl.BlockSpec((ROW_TILE, D_MODEL), lambda i: (_tile(i), 0))]


def _load_x(x_refs):
    if len(x_refs) == 1:
        return x_refs[0][...]
    return jnp.where(pl.program_id(0) < CTX_TILES, x_refs[0][...], x_refs[1][...])


def _mod_spec(layer):
    return pl.BlockSpec((None, 1, 1, 6 * D_MODEL),
                        lambda i: (layer, _mod_row_of_tile(_tile(i)), 0, 0))


def _rope128(x, cos, sin, half):
    lane = lax.broadcasted_iota(jnp.int32, x.shape, 1)
    first = (lane % (2 * half)) < half
    partner = jnp.where(first, pltpu.roll(x, LANES - half, 1), pltpu.roll(x, half, 1))
    return x * cos + partner * sin


def _rope_cols(x, cos, sin, half):
    chunks = [_rope128(x[:, c:c + LANES], cos, sin, half) for c in range(0, x.shape[1], LANES)]
    return chunks[0] if len(chunks) == 1 else jnp.concatenate(chunks, axis=1)


N_IN_INPUTS = 11


def _in_kernel(*refs, n_x):
    (mod_ref, g_ref, w1_ref, qn_ref, wuq_ref, kvn_ref, wukv_ref,
     cos64_ref, sin64_ref, cos32_ref, sin32_ref) = refs[n_x:n_x + N_IN_INPUTS]
    (qa_o, qd_o, qmn_o, qmr_o, kmn_o, vm_o, kpe4_o, ka_l, va_l, kd_l, vd_l,
     s_ka, s_va, s_kd, s_vd, s_ckv, s_kpe) = refs[-17:]
    i = pl.program_id(0)
    x = _load_x(refs[:n_x])
    m = mod_ref[0]
    sh1, sc1 = m[:, 0:D_MODEL], m[:, D_MODEL:2 * D_MODEL]
    h = _rms(x, g_ref[...]) * (1.0 + sc1) + sh1
    z = _dot(h.astype(BF16), w1_ref[...])
    cos64, sin64 = cos64_ref[...], sin64_ref[...]
    cos32, sin32 = cos32_ref[...], sin32_ref[...]
    qa_o[...] = (_rope_cols(z[:, C_QA:C_KA], cos64, sin64, 16) * SCALE_64).astype(BF16)
    qd_o[...] = (_rope_cols(z[:, C_QD:C_KD], cos64, sin64, 16) * SCALE_64).astype(BF16)
    ka = _rope_cols(z[:, C_KA:C_VA], cos64, sin64, 16)
    va = z[:, C_VA:C_QD]
    kd = _rope_cols(z[:, C_KD:C_VD], cos64, sin64, 16)
    vd = z[:, C_VD:C_QC]
    kpe4 = _rope_cols(z[:, C_KPE:C_END], cos32, sin32, 8)
    kpe4_o[...] = kpe4.astype(BF16)
    qn = _rms(z[:, C_QC:C_KVC], qn_ref[...])
    qm = _dot(qn.astype(BF16), wuq_ref[...]) * SCALE_96
    qmn_o[...] = qm[:, 0:512].astype(BF16)
    qmr_o[...] = _rope_cols(qm[:, 512:768], cos32, sin32, 8).astype(BF16)
    ckv = _rms(z[:, C_KVC:C_KPE], kvn_ref[...])
    kv = _dot(ckv.astype(BF16), wukv_ref[...])
    kmn_o[...] = kv[:, 0:512].astype(BF16)
    vm_o[...] = kv[:, 512:1024].astype(BF16)

    @pl.when(i < CTX_TILES)
    def _():
        s_ka[...] = ka
        s_va[...] = va
        s_kd[...] = kd
        s_vd[...] = vd
        s_ckv[...] = ckv
        s_kpe[...] = kpe4[:, 0:32]

    @pl.when(i >= CTX_TILES)
    def _():
        ka_l[...] = ka.astype(BF16)
        va_l[...] = va.astype(BF16)
        kd_l[...] = kd.astype(BF16)
        vd_l[...] = vd.astype(BF16)


STATE_WIDTHS = (128, 128, 512, 512, 128, 32)


def _in_projection(x, mod, wts, tables, layer, prev_states):
    tm = ROW_TILE

    def row(i):
        return (i, 0)

    def table_row(i):
        return (jnp.where(i < CTX_TILES, 0, 1 + (i - CTX_TILES) % (LAT_SEQ // tm)), 0)

    def lat_row(i):
        return (jnp.maximum(i - CTX_TILES, 0), 0)

    def state_block(i):
        return (jnp.minimum(i, CTX_TILES - 1), layer, 0, 0)

    table_spec = pl.BlockSpec((tm, LANES), table_row)
    weights = [wts["g_pre"], wts["w1"], wts["qnorm"], wts["wuq"], wts["kvnorm"], wts["wukv"]]
    all_rows = [(512, BF16)] * 3 + [(256, BF16), (512, BF16), (512, BF16), (128, BF16)]
    lat_rows = [(128, BF16), (128, BF16), (512, BF16), (512, BF16)]
    aliased = [] if prev_states is None else list(prev_states)
    xs = list(x) if isinstance(x, tuple) else [x]
    n_in = len(xs) + N_IN_INPUTS
    n_plain_out = len(all_rows) + len(lat_rows)
    return pl.pallas_call(
        functools.partial(_in_kernel, n_x=len(xs)),
        out_shape=[jax.ShapeDtypeStruct((N_TOK, w), dt) for w, dt in all_rows]
                  + [jax.ShapeDtypeStruct((N_LAT, w), dt) for w, dt in lat_rows]
                  + [jax.ShapeDtypeStruct((N_CTX_BATCH, DEPTH, CTX_SEQ, w), F32) for w in STATE_WIDTHS],
        grid=(N_TOK // tm,),
        in_specs=_x_specs(x) + [_mod_spec(layer)]
                 + [_layer_spec(w, layer) for w in weights]
                 + [table_spec] * 4
                 + [pl.BlockSpec(memory_space=pl.ANY)] * len(aliased),
        out_specs=[pl.BlockSpec((tm, w), row) for w, _ in all_rows]
                  + [pl.BlockSpec((tm, w), lat_row) for w, _ in lat_rows]
                  + [pl.BlockSpec((None, None, CTX_SEQ, w), state_block) for w in STATE_WIDTHS],
        input_output_aliases={n_in + k: n_plain_out + k for k in range(len(aliased))},
        compiler_params=pltpu.CompilerParams(
            dimension_semantics=("arbitrary",), vmem_limit_bytes=VMEM_LIMIT),
        name="in_projection",
    )(*xs, mod, *weights, *tables, *aliased)


def _attend(q, parts, sink=None):
    scores = []
    for k, _, mask in parts:
        s = _dot_nt(q, k)
        if mask is not None:
            s = jnp.where(mask, s, NEG)
        scores.append(s)
    m = functools.reduce(jnp.maximum, [jnp.max(s, axis=-1, keepdims=True) for s in scores])
    if sink is not None:
        m = jnp.maximum(m, sink)
    den = None
    out = None
    for s, (_, v, _) in zip(scores, parts):
        e = jnp.exp(s - m)
        d = jnp.sum(e, axis=-1, keepdims=True)
        o = _dot(e.astype(BF16), v)
        den = d if den is None else den + d
        out = o if out is None else out + o
    if sink is not None:
        den = den + jnp.exp(sink - m)
    return out / den


A_HEADS = 8


def _attend_heads(qs, parts, sinks, stack):
    rows = qs[0].shape[0]
    if not stack:
        return [_attend(q, parts, None if sinks is None else sinks[i]) for i, q in enumerate(qs)]
    sink = None
    if sinks is not None:
        sink = jnp.concatenate([jnp.full((rows, 1), s, F32) for s in sinks], axis=0)
    o = _attend(jnp.concatenate(qs, axis=0), parts, sink)
    return [o[i * rows:(i + 1) * rows] for i in range(len(qs))]


def _three_mixers(scal_ref, subln_ref, q_blocks, kv_parts, oa_o, od_o, om_o, lam_init, stack_a,
                  stack_bc):
    qa, qd, qmn, qmr = q_blocks
    rows = qa.shape[0]
    lane = lax.broadcasted_iota(jnp.int32, (rows, LANES), 1)
    low = lane < 64
    zero = jnp.zeros((rows, LANES), BF16)

    parts = [(p["ka"], p["va"], p.get("mask_a")) for p in kv_parts]
    qs = [jnp.where(low if kvh == 0 else ~low, qa[:, g * LANES:(g + 1) * LANES], zero)
          for kvh in range(2) for g in range(4)]
    out_a = _attend_heads(qs, parts, [scal_ref[h] for h in range(A_HEADS)], stack_a)
    for g in range(4):
        oa_o[:, g * LANES:(g + 1) * LANES] = jnp.where(low, out_a[g], out_a[4 + g]).astype(BF16)

    lam = scal_ref[8]
    subln = subln_ref[...]
    for h in range(4):
        sl = slice(h * LANES, (h + 1) * LANES)
        q128 = qd[:, sl]
        o1, o2 = _attend_heads([jnp.where(low, q128, zero), jnp.where(low, zero, q128)],
                               [(p["kd"][:, sl], p["vd"][:, sl], None) for p in kv_parts],
                               None, stack_bc)
        od = o1 - lam * o2
        od_o[:, sl] = (_rms(od, subln) * (1.0 - lam_init)).astype(BF16)

    for i in range(4):
        sl = slice(i * LANES, (i + 1) * LANES)
        qs = []
        for sub in range(2):
            h = 2 * i + sub
            qn = jnp.where(low if sub == 0 else ~low, qmn[:, sl], zero)
            qr128 = qmr[:, (h // 4) * LANES:(h // 4 + 1) * LANES]
            qr = jnp.where((lane // 32) == (h % 4), qr128, zero)
            qs.append(jnp.concatenate([qn, qr], axis=1))
        parts = [(jnp.concatenate([p["kmn"][:, sl], p["kpe"]], axis=1), p["vm"][:, sl], None)
                 for p in kv_parts]
        o_even, o_odd = _attend_heads(qs, parts, None, stack_bc)
        om_o[:, sl] = jnp.where(low, o_even, o_odd).astype(BF16)


def _attn_ctx_kernel(scal_ref, subln_ref, qa_ref, qd_ref, qmn_ref, qmr_ref,
                     ka_ref, va_ref, kd_ref, vd_ref, kmn_ref, vm_ref, kpe_ref,
                     oa_o, od_o, om_o, *, lam_init):
    part = dict(ka=ka_ref[...].astype(BF16), va=va_ref[...].astype(BF16),
                kd=kd_ref[...].astype(BF16), vd=vd_ref[...].astype(BF16),
                kmn=kmn_ref[...], vm=vm_ref[...], kpe=kpe_ref[...])
    _three_mixers(scal_ref, subln_ref, (qa_ref[...], qd_ref[...], qmn_ref[...], qmr_ref[...]),
                  [part], oa_o, od_o, om_o, lam_init, stack_a=False, stack_bc=True)


def _attn_lat_kernel(scal_ref, subln_ref, wukv_ref, qa_ref, qd_ref, qmn_ref, qmr_ref,
                     ka_ref, va_ref, kd_ref, vd_ref, kmn_ref, vm_ref, kpe_ref,
                     cka_ref, cva_ref, ckd_ref, cvd_ref, cckv_ref, ckpe_ref,
                     oa_in, od_in, om_in, oa_o, od_o, om_o, *, lam_init):
    del oa_in, od_in, om_in
    n = pl.program_id(1)
    kv_c = _dot(cckv_ref[...].astype(BF16), wukv_ref[...])
    cached = dict(ka=cka_ref[...].astype(BF16), va=cva_ref[...].astype(BF16),
                  kd=ckd_ref[...].astype(BF16), vd=cvd_ref[...].astype(BF16),
                  kmn=kv_c[:, 0:512].astype(BF16), vm=kv_c[:, 512:1024].astype(BF16),
                  kpe=ckpe_ref[...].astype(BF16))
    span = Q_BLOCK + 2 * WINDOW
    start = pl.multiple_of(jnp.clip(n * Q_BLOCK - WINDOW, 0, LAT_SEQ - span), WINDOW)
    qpos = n * Q_BLOCK + lax.broadcasted_iota(jnp.int32, (A_HEADS * Q_BLOCK, span), 0) % Q_BLOCK
    kpos = start + lax.broadcasted_iota(jnp.int32, (A_HEADS * Q_BLOCK, span), 1)
    mask_a = jnp.abs(qpos - kpos) <= WINDOW
    new = dict(ka=ka_ref[pl.ds(start, span), :], va=va_ref[pl.ds(start, span), :], mask_a=mask_a,
               kd=kd_ref[...], vd=vd_ref[...], kmn=kmn_ref[...], vm=vm_ref[...], kpe=kpe_ref[...])
    _three_mixers(scal_ref, subln_ref, (qa_ref[...], qd_ref[...], qmn_ref[...], qmr_ref[...]),
                  [cached, new], oa_o, od_o, om_o, lam_init, stack_a=True, stack_bc=True)


_SMEM_SPEC = pl.BlockSpec(memory_space=pltpu.SMEM)


def _attention_ctx(scal, subln, proj, states, layer, lam_init):
    qa, qd, qmn, qmr, kmn, vm, kpe4 = proj[:7]
    s_ka, s_va, s_kd, s_vd = states[:4]
    t = CTX_SEQ

    def blk(a):
        return pl.BlockSpec((t, a.shape[1]), lambda b: (b, 0))

    def sblk(a):
        return pl.BlockSpec((None, None, t, a.shape[-1]), lambda b: (b, layer, 0, 0))

    return pl.pallas_call(
        functools.partial(_attn_ctx_kernel, lam_init=lam_init),
        out_shape=[jax.ShapeDtypeStruct((N_TOK, 512), BF16)] * 3,
        grid=(N_CTX_BATCH,),
        in_specs=[_SMEM_SPEC, _layer_spec(subln, layer)]
                 + [blk(a) for a in (qa, qd, qmn, qmr)]
                 + [sblk(a) for a in (s_ka, s_va, s_kd, s_vd)]
                 + [blk(a) for a in (kmn, vm, kpe4)],
        out_specs=[pl.BlockSpec((t, 512), lambda b: (b, 0))] * 3,
        compiler_params=pltpu.CompilerParams(vmem_limit_bytes=VMEM_LIMIT),
        name="attention_ctx",
    )(scal, subln, qa, qd, qmn, qmr, s_ka, s_va, s_kd, s_vd, kmn, vm, kpe4)


def _attention_lat(scal, subln, wukv, proj, caches, ctx_out, layer, lam_init):
    qa, qd, qmn, qmr, kmn, vm, kpe4, ka_l, va_l, kd_l, vd_l = proj[:11]
    nq = LAT_SEQ // Q_BLOCK
    q_off = N_CTX // Q_BLOCK
    kv_off = N_CTX // LAT_SEQ

    def qblk(a):
        return pl.BlockSpec((Q_BLOCK, a.shape[1]), lambda b, n: (q_off + b * nq + n, 0))

    def kvblk(a):
        return pl.BlockSpec((LAT_SEQ, a.shape[1]), lambda b, n: (b, 0))

    def kvblk_all(a):
        return pl.BlockSpec((LAT_SEQ, a.shape[1]), lambda b, n: (kv_off + b, 0))

    def cblk(a):
        return pl.BlockSpec((None, None, PAST, a.shape[-1]), lambda b, n: (b, layer, 0, 0))

    qs = [qa, qd, qmn, qmr]
    n_in = 3 + 4 + 7 + 6
    out_spec = pl.BlockSpec((Q_BLOCK, 512), lambda b, n: (q_off + b * nq + n, 0))
    return pl.pallas_call(
        functools.partial(_attn_lat_kernel, lam_init=lam_init),
        out_shape=[jax.ShapeDtypeStruct((N_TOK, 512), BF16)] * 3,
        grid=(N_LAT_BATCH, nq),
        in_specs=[_SMEM_SPEC, _layer_spec(subln, layer), _layer_spec(wukv, layer)]
                 + [qblk(a) for a in qs]
                 + [kvblk(a) for a in (ka_l, va_l, kd_l, vd_l)]
                 + [kvblk_all(a) for a in (kmn, vm, kpe4)]
                 + [cblk(a) for a in caches]
                 + [pl.BlockSpec(memory_space=pl.ANY)] * 3,
        out_specs=[out_spec] * 3,
        input_output_aliases={n_in + k: k for k in range(3)},
        compiler_params=pltpu.CompilerParams(vmem_limit_bytes=VMEM_LIMIT),
        name="attention_lat",
    )(scal, subln, wukv, *qs, ka_l, va_l, kd_l, vd_l, kmn, vm, kpe4, *caches, *ctx_out)


def _route(scores, biased):
    per = N_EXPERTS // N_GROUPS
    tm = scores.shape[1]
    sub = lax.broadcasted_iota(jnp.int32, (per, tm), 0).astype(F32)
    groups = [biased[g * per:(g + 1) * per, :] for g in range(N_GROUPS)]
    gscore = []
    for v in groups:
        m1 = jnp.max(v, axis=0, keepdims=True)
        first = jnp.min(jnp.where(v == m1, sub, float(per)), axis=0, keepdims=True)
        m2 = jnp.max(jnp.where(sub == first, -jnp.inf, v), axis=0, keepdims=True)
        gscore.append(m1 + m2)
    vals = []
    for g in range(N_GROUPS):
        rank = jnp.zeros((1, tm), F32)
        for o in range(N_GROUPS):
            if o == g:
                continue
            ahead = (gscore[o] >= gscore[g]) if o < g else (gscore[o] > gscore[g])
            rank = rank + jnp.where(ahead, 1.0, 0.0)
        vals.append(jnp.where(rank < TOPK_GROUPS, groups[g], NEG))
    idx = [sub + float(g * per) for g in range(N_GROUPS)]
    picks, weights = [], []
    for _ in range(TOP_K):
        best = functools.reduce(jnp.maximum, [jnp.max(v, axis=0, keepdims=True) for v in vals])
        cand = functools.reduce(jnp.minimum, [
            jnp.min(jnp.where(v == best, i, float(N_EXPERTS)), axis=0, keepdims=True)
            for v, i in zip(vals, idx)])
        wsel = jnp.zeros((1, tm), F32)
        for g in range(N_GROUPS):
            hit = idx[g] == cand
            wsel = wsel + jnp.sum(jnp.where(hit, scores[g * per:(g + 1) * per, :], 0.0),
                                  axis=0, keepdims=True)
            vals[g] = jnp.where(hit, -jnp.inf, vals[g])
        picks.append(cand)
        weights.append(wsel)
    total = functools.reduce(lambda a, b: a + b, weights)
    pad = [jnp.zeros((1, tm), F32)] * (8 - TOP_K)
    experts = jnp.concatenate(picks + pad, axis=0).astype(jnp.int32)
    gates = jnp.concatenate([x / total * ROUTED_SCALE for x in weights] + pad, axis=0)
    return experts, gates


def _post_kernel(*refs, n_x):
    (mod_ref, oa_ref, od_ref, om_ref, gpre_ref, wgl_ref, wa_ref, wb_ref, wc_ref, wo_ref, gpost_ref,
     gffn_ref, rwt_ref, rb_ref, xmid_o, h2_o, h2t_o, experts_o, gates_o) = refs[n_x:]

    @pl.when(pl.program_id(0) < N_TILES)
    def _():
        x = _load_x(refs[:n_x])
        m = mod_ref[0]
        d = D_MODEL
        sh1, sc1, g1, sh2, sc2 = (m[:, 0:d], m[:, d:2 * d], m[:, 2 * d:3 * d], m[:, 3 * d:4 * d],
                                  m[:, 4 * d:5 * d])
        h = _rms(x, gpre_ref[...]) * (1.0 + sc1) + sh1
        gate = 0.5 * jnp.tanh(0.5 * _dot(h.astype(BF16), wgl_ref[...])) + 0.5
        merged = (gate[:, 0:d] * _dot(oa_ref[...], wa_ref[...])
                  + gate[:, d:2 * d] * _dot(od_ref[...], wb_ref[...])
                  + gate[:, 2 * d:3 * d] * _dot(om_ref[...], wc_ref[...]))
        a = _dot(merged.astype(BF16), wo_ref[...])
        xm = x + g1 * _rms(a, gpost_ref[...])
        xmid_o[...] = xm
        h2f = _rms(xm, gffn_ref[...]) * (1.0 + sc2) + sh2
        h2 = h2f.astype(BF16)
        h2_o[...] = h2
        for c in range(ACC_ROWS):
            h2t_o[pl.ds(c, x.shape[0], stride=ACC_ROWS), :] = h2f[:, c * LANES:(c + 1) * LANES]
        scores = jax.nn.sigmoid(_dot_nt(rwt_ref[...], h2))
        experts, gates = _route(scores, scores + rb_ref[...])
        experts_o[...] = experts
        gates_o[...] = gates

    @pl.when(pl.program_id(0) == N_TILES)
    def _():
        h2t_o[...] = jnp.zeros_like(h2t_o)


def _post_attention(x, mod, oa, od, om, wts, layer):
    tm = ROW_TILE

    def row(i):
        return (_tile(i), 0)

    weights = [wts[k] for k in ("g_pre", "wgl", "wa", "wb", "wc", "wo", "g_post", "g_ffn", "rwt", "rb")]
    xs = list(x) if isinstance(x, tuple) else [x]
    return pl.pallas_call(
        functools.partial(_post_kernel, n_x=len(xs)),
        out_shape=[jax.ShapeDtypeStruct((N_TOK, D_MODEL), F32),
                   jax.ShapeDtypeStruct((N_TOK, D_MODEL), BF16),
                   jax.ShapeDtypeStruct(((N_TOK + tm) * ACC_ROWS, LANES), F32),
                   jax.ShapeDtypeStruct((8, N_TOK), jnp.int32),
                   jax.ShapeDtypeStruct((8, N_TOK), F32)],
        grid=(N_TILES + 1,),
        in_specs=_x_specs(x) + [_mod_spec(layer),
                  pl.BlockSpec((tm, 512), row), pl.BlockSpec((tm, 512), row),
                  pl.BlockSpec((tm, 512), row)]
                 + [_layer_spec(w, layer) for w in weights],
        out_specs=[pl.BlockSpec((tm, D_MODEL), row), pl.BlockSpec((tm, D_MODEL), row),
                   pl.BlockSpec((tm * ACC_ROWS, LANES), lambda i: (i, 0)),
                   pl.BlockSpec((8, tm), lambda i: (0, _tile(i))),
                   pl.BlockSpec((8, tm), lambda i: (0, _tile(i)))],
        compiler_params=pltpu.CompilerParams(
            dimension_semantics=("arbitrary",), vmem_limit_bytes=VMEM_LIMIT),
        name="post_attention",
    )(*xs, mod, oa, od, om, *weights)


def _ffn(x, wg, wu):
    return _silu(_dot(x, wg)) * _dot(x, wu)


def _moe_kernel(start_ref, count_ref, dst_ref, gate_ref, xs_ref, wg_ref, wu_ref, wd_ref,
                acc_ref, xa, xb, ya, yb, wgb, wub, wdb, pend):
    e = pl.program_id(0)

    @pl.when(e == 0)
    def _():
        def clear(i, carry):
            r0 = pl.multiple_of(i * MOE_ZERO_ROWS, MOE_ZERO_ROWS)
            acc_ref[pl.ds(r0, MOE_ZERO_ROWS), :] = jnp.zeros((MOE_ZERO_ROWS, LANES), F32)
            return carry

        lax.fori_loop(0, acc_ref.shape[0] // MOE_ZERO_ROWS, clear, 0)

        yb[...] = jnp.zeros_like(yb)
        pend[0] = MOE_LIST_PAD_BASE

    wgb[...] = wg_ref[0].astype(BF16)
    wub[...] = wu_ref[0].astype(BF16)
    wdb[...] = wd_ref[0].astype(BF16)
    start = start_ref[e]
    pairs = (count_ref[e] + 2 * MOE_SUB - 1) // (2 * MOE_SUB)

    def gather(base, xbuf):
        for mi in range(MOE_SUB):
            src = pl.multiple_of(dst_ref[base + mi], ACC_ROWS)
            xbuf[mi * ACC_ROWS:(mi + 1) * ACC_ROWS, :] = xs_ref[pl.ds(src, ACC_ROWS), :]

    def expert_ffn(xbuf, ybuf):
        x = jnp.concatenate([xbuf[pl.ds(c, MOE_SUB, stride=ACC_ROWS), :].astype(BF16)
                             for c in range(ACC_ROWS)], axis=1)
        hid = _ffn(x, wgb[...], wub[...])
        y = _dot(hid.astype(BF16), wdb[...])
        for c in range(ACC_ROWS):
            ybuf[pl.ds(c, MOE_SUB, stride=ACC_ROWS), :] = y[:, c * LANES:(c + 1) * LANES]

    def scatter(base, ybuf):
        for g0 in range(0, MOE_SUB, MOE_RMW_GROUP):
            updates = []
            for mi in range(g0, g0 + MOE_RMW_GROUP):
                dst = pl.multiple_of(dst_ref[base + mi], ACC_ROWS)
                yv = ybuf[mi * ACC_ROWS:(mi + 1) * ACC_ROWS, :]
                updates.append((dst, acc_ref[pl.ds(dst, ACC_ROWS), :] + gate_ref[base + mi] * yv))
            for dst, val in updates:
                acc_ref[pl.ds(dst, ACC_ROWS), :] = val

    start_next = start_ref[jnp.minimum(e + 1, N_EXPERTS - 1)]
    prev_count = count_ref[jnp.maximum(e - 1, 0)]

    @pl.when((pairs > 0) & ((e == 0) | (prev_count == 0)))
    def _():
        gather(start, xa)

    def pair(p, carry):
        base = start + p * (2 * MOE_SUB)
        gather(base + MOE_SUB, xb)
        expert_ffn(xa, ya)
        scatter(pend[0], yb)
        gather(jnp.where(p + 1 < pairs, base + 2 * MOE_SUB, start_next), xa)
        expert_ffn(xb, yb)
        scatter(base, ya)
        pend[0] = base + MOE_SUB
        return carry

    lax.fori_loop(0, pairs, pair, 0)

    @pl.when(e == N_EXPERTS - 1)
    def _():
        scatter(pend[0], yb)


def _moe(lists, h2t, wg, wu, wd, layer):
    def expert(e, *_):
        return (layer, e, 0, 0)

    def whole(e, *_):
        return (0, 0)

    acc_rows = (N_TOK + MOE_SPARE_TOKENS) * ACC_ROWS
    assert acc_rows % MOE_ZERO_ROWS == 0
    return pl.pallas_call(
        _moe_kernel,
        out_shape=jax.ShapeDtypeStruct((acc_rows, LANES), F32),
        grid_spec=pltpu.PrefetchScalarGridSpec(
            num_scalar_prefetch=len(lists),
            grid=(N_EXPERTS,),
            in_specs=[pl.BlockSpec((acc_rows, LANES), whole, pipeline_mode=pl.Buffered(1)),
                      pl.BlockSpec((None, 1, D_MODEL, F_EXPERT), expert),
                      pl.BlockSpec((None, 1, D_MODEL, F_EXPERT), expert),
                      pl.BlockSpec((None, 1, F_EXPERT, D_MODEL), expert)],
            out_specs=pl.BlockSpec((acc_rows, LANES), whole, pipeline_mode=pl.Buffered(1)),
            scratch_shapes=[pltpu.VMEM((MOE_SUB * ACC_ROWS, LANES), F32)] * 4
                           + [pltpu.VMEM((D_MODEL, F_EXPERT), BF16),
                              pltpu.VMEM((D_MODEL, F_EXPERT), BF16),
                              pltpu.VMEM((F_EXPERT, D_MODEL), BF16),
                              pltpu.SMEM((1,), jnp.int32)]),
        compiler_params=pltpu.CompilerParams(
            dimension_semantics=("arbitrary",), vmem_limit_bytes=MOE_VMEM_LIMIT),
        name="moe",
    )(*lists, h2t, wg, wu, wd)


def _final_kernel(acc_ref, h2_ref, xmid_ref, mod_ref, swg_ref, swu_ref, swd_ref, gpost_ref, *out_refs):
    tm = h2_ref.shape[0]
    routed = jnp.concatenate(
        [acc_ref[pl.ds(c, tm, stride=ACC_ROWS), :] for c in range(ACC_ROWS)], axis=1)
    hid = _ffn(h2_ref[...], swg_ref[...], swu_ref[...])
    shared = _dot(hid.astype(BF16), swd_ref[...])
    g2 = mod_ref[0][:, 5 * D_MODEL:6 * D_MODEL]
    out = xmid_ref[...] + g2 * _rms(routed + shared, gpost_ref[...])
    if len(out_refs) == 1:
        out_refs[0][...] = out
    else:
        @pl.when(pl.program_id(0) < CTX_TILES)
        def _():
            out_refs[0][...] = out

        @pl.when(pl.program_id(0) >= CTX_TILES)
        def _():
            out_refs[1][...] = out


def _final(acc, h2, xmid, mod, wts, layer, split):
    tm = ROW_TILE

    def row(i):
        return (i, 0)

    weights = [wts[k] for k in ("swg", "swu", "swd", "g_ffn_post")]
    return pl.pallas_call(
        _final_kernel,
        out_shape=([jax.ShapeDtypeStruct((N_CTX, D_MODEL), F32),
                    jax.ShapeDtypeStruct((N_LAT, D_MODEL), F32)] if split
                   else jax.ShapeDtypeStruct((N_TOK, D_MODEL), F32)),
        grid=(N_TOK // tm,),
        in_specs=[pl.BlockSpec((tm * ACC_ROWS, LANES), row),
                  pl.BlockSpec((tm, D_MODEL), row),
                  pl.BlockSpec((tm, D_MODEL), row),
                  _mod_spec(layer)]
                 + [_layer_spec(w, layer) for w in weights],
        out_specs=_x_specs((None, None)) if split else pl.BlockSpec((tm, D_MODEL), row),
        compiler_params=pltpu.CompilerParams(
            dimension_semantics=("arbitrary",), vmem_limit_bytes=VMEM_LIMIT),
        name="ffn_final",
    )(acc, h2, xmid, mod, *weights)


def _routing_lists(experts, gates):
    e = experts[:TOP_K].reshape(-1)
    tok = jnp.tile(jnp.arange(N_TOK, dtype=jnp.int32), TOP_K)
    pad_e = jnp.repeat(jnp.arange(N_EXPERTS, dtype=jnp.int32), MOE_PAD)
    pad_tok = N_TOK + jnp.tile(jnp.arange(MOE_PAD, dtype=jnp.int32), N_EXPERTS)
    keys = jnp.concatenate([e * TOK_KEY + tok, pad_e * TOK_KEY + pad_tok])
    vals = jnp.concatenate([gates[:TOP_K].reshape(-1), jnp.zeros((N_EXPERTS * MOE_PAD,), F32)])
    keys, gate_sorted = lax.sort((keys, vals), num_keys=1)
    tail = jnp.full((MOE_SUB,), N_TOK, jnp.int32)
    tok_sorted = jnp.concatenate([keys & (TOK_KEY - 1), tail])
    gate_sorted = jnp.concatenate([gate_sorted, jnp.zeros((MOE_SUB,), F32)])
    dst = jnp.minimum(tok_sorted, N_TOK) * ACC_ROWS
    counts = jnp.sum((e[None, :] == jnp.arange(N_EXPERTS, dtype=jnp.int32)[:, None]).astype(jnp.int32),
                     axis=1)
    start = jnp.cumsum(counts) - counts + MOE_PAD * jnp.arange(N_EXPERTS, dtype=jnp.int32)
    return start.astype(jnp.int32), counts, dst, gate_sorted


def _rope_tables():
    t = np.arange(LAT_SEQ)
    pos = np.stack([t // GRID_W, t % GRID_W], axis=1).astype(np.float64)

    def table(half):
        inv = ROPE_BASE ** (-np.arange(half, dtype=np.float64) / half)
        lane = np.arange(LANES)
        axis = (lane // (2 * half)) % 2
        freq = inv[lane % half]
        ang = pos[:, axis] * freq[None, :]
        sign = np.where((lane % (2 * half)) < half, -1.0, 1.0)
        cos = np.concatenate([np.ones((ROW_TILE, LANES)), np.cos(ang)], axis=0)
        sin = np.concatenate([np.zeros((ROW_TILE, LANES)), np.sin(ang) * sign[None, :]], axis=0)
        return jnp.asarray(cos, F32), jnp.asarray(sin, F32)

    cos64, sin64 = table(16)
    cos32, sin32 = table(8)
    return cos64, sin64, cos32, sin32


def _prepare_weights(norm_attn_pre, norm_attn_post, norm_ffn_pre, norm_ffn_post, w_in, dif_subln,
                     mla_q_norm, mla_w_uq, mla_kv_norm, mla_w_ukv, w_branch_a, w_branch_b,
                     w_branch_c, w_out, router_w, router_bias, shared_w_gate, shared_w_up,
                     shared_w_down):
    dp = DEPTH
    qa = w_in[:, :, 0:512].reshape(dp, D_MODEL, 2, 4, 64).transpose(0, 1, 3, 2, 4).reshape(dp, D_MODEL, 512)
    kpe = w_in[:, :, 2688:2720]
    w1 = jnp.concatenate([qa, w_in[:, :, 512:2688], kpe, kpe, kpe, kpe], axis=2).astype(BF16)
    uq = mla_w_uq.reshape(dp, 256, 8, 96)
    wuq = jnp.concatenate([uq[..., :64].reshape(dp, 256, 512), uq[..., 64:].reshape(dp, 256, 256)],
                          axis=2).astype(BF16)
    ukv = mla_w_ukv.reshape(dp, 128, 8, 128)
    wukv = jnp.concatenate([ukv[..., :64].reshape(dp, 128, 512), ukv[..., 64:].reshape(dp, 128, 512)],
                           axis=2).astype(BF16)
    wa = w_branch_a.reshape(dp, 2, 4, 64, D_MODEL).transpose(0, 2, 1, 3, 4).reshape(dp, 512, D_MODEL)
    return dict(
        g_pre=norm_attn_pre[:, None, :], g_post=norm_attn_post[:, None, :],
        g_ffn=norm_ffn_pre[:, None, :], g_ffn_post=norm_ffn_post[:, None, :],
        w1=w1, wgl=w_in[:, :, 2720:].astype(BF16),
        qnorm=mla_q_norm[:, None, :], wuq=wuq, kvnorm=mla_kv_norm[:, None, :], wukv=wukv,
        subln=dif_subln[:, None, :],
        wa=wa.astype(BF16), wb=w_branch_b.astype(BF16), wc=w_branch_c.astype(BF16),
        wo=w_out.astype(BF16), rwt=router_w.transpose(0, 2, 1).astype(BF16),
        rb=router_bias[:, :, None],
        swg=shared_w_gate.astype(BF16), swu=shared_w_up.astype(BF16), swd=shared_w_down.astype(BF16))


def kernel(x_prompt, x_sample, cache_swa_k, cache_swa_v, cache_dif_k, cache_dif_v, cache_mla_ckv, cache_mla_kpe, c, c_ctx, w_mod, b_mod, norm_attn_pre, norm_attn_post, norm_ffn_pre, norm_ffn_post, w_in, swa_sink, dif_lq1, dif_lk1, dif_lq2, dif_lk2, dif_subln, mla_q_norm, mla_w_uq, mla_kv_norm, mla_w_ukv, w_branch_a, w_branch_b, w_branch_c, w_out, router_w, router_bias, moe_w_gate, moe_w_up, moe_w_down, shared_w_gate, shared_w_up, shared_w_down):
    x = (x_prompt.reshape(N_CTX, D_MODEL), x_sample.reshape(N_LAT, D_MODEL))
    cvec = jnp.concatenate([c_ctx[None, :], c, jnp.zeros((8 - 1 - N_LAT_BATCH, D_MODEL), F32)], axis=0)
    mod = _modulation(cvec, w_mod, b_mod).reshape(DEPTH, 8, 1, 6 * D_MODEL)
    tables = _rope_tables()
    wts = _prepare_weights(norm_attn_pre, norm_attn_post, norm_ffn_pre, norm_ffn_post, w_in,
                           dif_subln, mla_q_norm, mla_w_uq, mla_kv_norm, mla_w_ukv, w_branch_a,
                           w_branch_b, w_branch_c, w_out, router_w, router_bias, shared_w_gate,
                           shared_w_up, shared_w_down)
    caches = [cache_swa_k.reshape(N_LAT_BATCH, DEPTH, PAST, 128),
              cache_swa_v.reshape(N_LAT_BATCH, DEPTH, PAST, 128),
              cache_dif_k.reshape(N_LAT_BATCH, DEPTH, PAST, 512),
              cache_dif_v.reshape(N_LAT_BATCH, DEPTH, PAST, 512),
              cache_mla_ckv,
              jnp.tile(cache_mla_kpe, (1, 1, 1, 4))]
    lam_init = [0.8 - 0.6 * math.exp(-0.3 * l) for l in range(DEPTH)]
    lam = (jnp.exp(jnp.sum(dif_lq1 * dif_lk1, axis=1)) - jnp.exp(jnp.sum(dif_lq2 * dif_lk2, axis=1))
           + jnp.asarray(lam_init, F32))
    scal = jnp.concatenate([swa_sink, lam[:, None]], axis=1).astype(F32)

    states = None
    for l in range(DEPTH):
        proj = _in_projection(x, mod, wts, tables, l, states)
        states = proj[11:]
        ctx_out = _attention_ctx(scal[l], wts["subln"], proj, states, l, lam_init[l])
        oa, od, om = _attention_lat(scal[l], wts["subln"], wts["wukv"], proj, caches, ctx_out, l,
                                    lam_init[l])
        xmid, h2, h2t, experts, gates = _post_attention(x, mod, oa, od, om, wts, l)
        lists = _routing_lists(experts, gates)
        acc = _moe(lists, h2t, moe_w_gate, moe_w_up, moe_w_down, l)
        x = _final(acc, h2, xmid, mod, wts, l, split=(l == DEPTH - 1))

    y_p = x[0].reshape(N_CTX_BATCH, CTX_SEQ, D_MODEL)
    y_s = x[1].reshape(N_LAT_BATCH, LAT_SEQ, D_MODEL)
    s_ka, s_va, s_kd, s_vd, s_ckv, s_kpe = states
    return (y_p, y_s,
            s_ka.reshape(N_CTX_BATCH, DEPTH, CTX_SEQ, 2, 64),
            s_va.reshape(N_CTX_BATCH, DEPTH, CTX_SEQ, 2, 64),
            s_kd.reshape(N_CTX_BATCH, DEPTH, CTX_SEQ, 4, 2, 64),
            s_vd.reshape(N_CTX_BATCH, DEPTH, CTX_SEQ, 4, 128),
            s_ckv, s_kpe)
```

```python
import functools
import math

import numpy as np
import jax
import jax.numpy as jnp
from jax import lax
from jax.experimental import pallas as pl
from jax.experimental.pallas import tpu as pltpu

F32 = jnp.float32
BF16 = jnp.bfloat16

D_MODEL = 1024
N_CTX_BATCH, CTX_SEQ = 16, 256
N_LAT_BATCH, LAT_SEQ = 2, 1024
PAST = 256
N_CTX = N_CTX_BATCH * CTX_SEQ
N_LAT = N_LAT_BATCH * LAT_SEQ
N_TOK = N_CTX + N_LAT
DEPTH = 2
GRID_W = 64
WINDOW = 128
N_EXPERTS = 64
N_GROUPS = 8
TOPK_GROUPS = 4
TOP_K = 6
F_EXPERT = 256
ROUTED_SCALE = 2.5
ROPE_BASE = 10000.0
EPS = 1e-6
NEG = -1e30

LANES = 128
ROW_TILE = 256
Q_BLOCK = 256
MOD_COL_TILE = 3072
VMEM_LIMIT = 56 * 1024 * 1024
ACC_ROWS = 8
MOE_SPARE_TOKENS = 8
MOE_ZERO_ROWS = 64
MOE_VMEM_LIMIT = 62 * 1024 * 1024
MOE_SUB = 128
MOE_RMW_GROUP = 4
MOE_PAD = 2 * MOE_SUB
TOK_KEY = 8192
MOE_LIST_PAD_BASE = N_TOK * TOP_K + N_EXPERTS * MOE_PAD

C_QA, C_KA, C_VA, C_QD, C_KD, C_VD, C_QC, C_KVC, C_KPE, C_END = (
    0, 512, 640, 768, 1280, 1792, 2304, 2560, 2688, 2816)
SCALE_64 = 1.0 / math.sqrt(64.0)
SCALE_96 = 1.0 / math.sqrt(96.0)

CTX_TILES = N_CTX // ROW_TILE
N_TILES = N_TOK // ROW_TILE
assert ROW_TILE == CTX_SEQ and LAT_SEQ % ROW_TILE == 0


def _mod_row_of_tile(i):
    return jnp.where(i < CTX_TILES, 0, 1 + (i - CTX_TILES) // (LAT_SEQ // ROW_TILE))


def _rms(x, g):
    return x * lax.rsqrt(jnp.mean(x * x, axis=-1, keepdims=True) + EPS) * g


def _dot(a, b):
    return jnp.dot(a, b, preferred_element_type=F32)


def _dot_nt(a, b):
    return lax.dot_general(a, b, (((1,), (1,)), ((), ())), preferred_element_type=F32)


def _silu(x):
    return x * jax.nn.sigmoid(x)


def _layer_spec(arr, layer):
    zeros = (0,) * (arr.ndim - 1)
    return pl.BlockSpec((None,) + arr.shape[1:], lambda *_: (layer,) + zeros)


def _mod_kernel(c_ref, w_ref, b_ref, o_ref):
    c = c_ref[...]
    o_ref[0] = _dot(_silu(c).astype(BF16), w_ref[0].astype(BF16)) + b_ref[0]


def _modulation(cvec, w_mod, b_mod):
    tn = MOD_COL_TILE
    n = w_mod.shape[-1]
    return pl.pallas_call(
        _mod_kernel,
        out_shape=jax.ShapeDtypeStruct((DEPTH, 8, n), F32),
        grid=(DEPTH, n // tn),
        in_specs=[
            pl.BlockSpec((8, D_MODEL), lambda l, j: (0, 0)),
            pl.BlockSpec((1, D_MODEL, tn), lambda l, j: (l, 0, j)),
            pl.BlockSpec((1, 1, tn), lambda l, j: (l, 0, j)),
        ],
        out_specs=pl.BlockSpec((1, 8, tn), lambda l, j: (l, 0, j)),
        compiler_params=pltpu.CompilerParams(vmem_limit_bytes=VMEM_LIMIT),
        name="modulation",
    )(cvec, w_mod, b_mod.reshape(DEPTH, 1, n))


def _tile(i):
    return jnp.minimum(i, N_TILES - 1)


def _x_specs(x):
    if isinstance(x, tuple):
        return [pl.BlockSpec((ROW_TILE, D_MODEL), lambda i: (jnp.minimum(i, CTX_TILES - 1), 0)),
                pl.BlockSpec((ROW_TILE, D_MODEL), lambda i: (jnp.maximum(_tile(i) - CTX_TILES, 0), 0))]
    return [pl.BlockSpec((ROW_TILE, D_MODEL), lambda i: (_tile(i), 0))]


def _load_x(x_refs):
    if len(x_refs) == 1:
        return x_refs[0][...]
    return jnp.where(pl.program_id(0) < CTX_TILES, x_refs[0][...], x_refs[1][...])


def _mod_spec(layer):
    return pl.BlockSpec((None, 1, 1, 6 * D_MODEL),
                        lambda i: (layer, _mod_row_of_tile(_tile(i)), 0, 0))


def _rope128(x, cos, sin, half):
    lane = lax.broadcasted_iota(jnp.int32, x.shape, 1)
    first = (lane % (2 * half)) < half
    partner = jnp.where(first, pltpu.roll(x, LANES - half, 1), pltpu.roll(x, half, 1))
    return x * cos + partner * sin


def _rope_cols(x, cos, sin, half):
    chunks = [_rope128(x[:, c:c + LANES], cos, sin, half) for c in range(0, x.shape[1], LANES)]
    return chunks[0] if len(chunks) == 1 else jnp.concatenate(chunks, axis=1)


N_IN_INPUTS = 11


def _in_kernel(*refs, n_x):
    (mod_ref, g_ref, w1_ref, qn_ref, wuq_ref, kvn_ref, wukv_ref,
     cos64_ref, sin64_ref, cos32_ref, sin32_ref) = refs[n_x:n_x + N_IN_INPUTS]
    (qa_o, qd_o, qmn_o, qmr_o, kmn_o, vm_o, kpe4_o, ka_l, va_l, kd_l, vd_l,
     s_ka, s_va, s_kd, s_vd, s_ckv, s_kpe) = refs[-17:]
    i = pl.program_id(0)
    x = _load_x(refs[:n_x])
    m = mod_ref[0]
    sh1, sc1 = m[:, 0:D_MODEL], m[:, D_MODEL:2 * D_MODEL]
    h = _rms(x, g_ref[...]) * (1.0 + sc1) + sh1
    z = _dot(h.astype(BF16), w1_ref[...])
    cos64, sin64 = cos64_ref[...], sin64_ref[...]
    cos32, sin32 = cos32_ref[...], sin32_ref[...]
    qa_o[...] = (_rope_cols(z[:, C_QA:C_KA], cos64, sin64, 16) * SCALE_64).astype(BF16)
    qd_o[...] = (_rope_cols(z[:, C_QD:C_KD], cos64, sin64, 16) * SCALE_64).astype(BF16)
    ka = _rope_cols(z[:, C_KA:C_VA], cos64, sin64, 16)
    va = z[:, C_VA:C_QD]
    kd = _rope_cols(z[:, C_KD:C_VD], cos64, sin64, 16)
    vd = z[:, C_VD:C_QC]
    kpe4 = _rope_cols(z[:, C_KPE:C_END], cos32, sin32, 8)
    kpe4_o[...] = kpe4.astype(BF16)
    qn = _rms(z[:, C_QC:C_KVC], qn_ref[...])
    qm = _dot(qn.astype(BF16), wuq_ref[...]) * SCALE_96
    qmn_o[...] = qm[:, 0:512].astype(BF16)
    qmr_o[...] = _rope_cols(qm[:, 512:768], cos32, sin32, 8).astype(BF16)
    ckv = _rms(z[:, C_KVC:C_KPE], kvn_ref[...])
    kv = _dot(ckv.astype(BF16), wukv_ref[...])
    kmn_o[...] = kv[:, 0:512].astype(BF16)
    vm_o[...] = kv[:, 512:1024].astype(BF16)

    @pl.when(i < CTX_TILES)
    def _():
        s_ka[...] = ka
        s_va[...] = va
        s_kd[...] = kd
        s_vd[...] = vd
        s_ckv[...] = ckv
        s_kpe[...] = kpe4[:, 0:32]

    @pl.when(i >= CTX_TILES)
    def _():
        ka_l[...] = ka.astype(BF16)
        va_l[...] = va.astype(BF16)
        kd_l[...] = kd.astype(BF16)
        vd_l[...] = vd.astype(BF16)


STATE_WIDTHS = (128, 128, 512, 512, 128, 32)


def _in_projection(x, mod, wts, tables, layer, prev_states):
    tm = ROW_TILE

    def row(i):
        return (i, 0)

    def table_row(i):
        return (jnp.where(i < CTX_TILES, 0, 1 + (i - CTX_TILES) % (LAT_SEQ // tm)), 0)

    def lat_row(i):
        return (jnp.maximum(i - CTX_TILES, 0), 0)

    def state_block(i):
        return (jnp.minimum(i, CTX_TILES - 1), layer, 0, 0)

    table_spec = pl.BlockSpec((tm, LANES), table_row)
    weights = [wts["g_pre"], wts["w1"], wts["qnorm"], wts["wuq"], wts["kvnorm"], wts["wukv"]]
    all_rows = [(512, BF16)] * 3 + [(256, BF16), (512, BF16), (512, BF16), (128, BF16)]
    lat_rows = [(128, BF16), (128, BF16), (512, BF16), (512, BF16)]
    aliased = [] if prev_states is None else list(prev_states)
    xs = list(x) if isinstance(x, tuple) else [x]
    n_in = len(xs) + N_IN_INPUTS
    n_plain_out = len(all_rows) + len(lat_rows)
    return pl.pallas_call(
        functools.partial(_in_kernel, n_x=len(xs)),
        out_shape=[jax.ShapeDtypeStruct((N_TOK, w), dt) for w, dt in all_rows]
                  + [jax.ShapeDtypeStruct((N_LAT, w), dt) for w, dt in lat_rows]
                  + [jax.ShapeDtypeStruct((N_CTX_BATCH, DEPTH, CTX_SEQ, w), F32) for w in STATE_WIDTHS],
        grid=(N_TOK // tm,),
        in_specs=_x_specs(x) + [_mod_spec(layer)]
                 + [_layer_spec(w, layer) for w in weights]
                 + [table_spec] * 4
                 + [pl.BlockSpec(memory_space=pl.ANY)] * len(aliased),
        out_specs=[pl.BlockSpec((tm, w), row) for w, _ in all_rows]
                  + [pl.BlockSpec((tm, w), lat_row) for w, _ in lat_rows]
                  + [pl.BlockSpec((None, None, CTX_SEQ, w), state_block) for w in STATE_WIDTHS],
        input_output_aliases={n_in + k: n_plain_out + k for k in range(len(aliased))},
        compiler_params=pltpu.CompilerParams(
            dimension_semantics=("arbitrary",), vmem_limit_bytes=VMEM_LIMIT),
        name="in_projection",
    )(*xs, mod, *weights, *tables, *aliased)


def _attend(q, parts, sink=None):
    scores = []
    for k, _, mask in parts:
        s = _dot_nt(q, k)
        if mask is not None:
            s = jnp.where(mask, s, NEG)
        scores.append(s)
    m = functools.reduce(jnp.maximum, [jnp.max(s, axis=-1, keepdims=True) for s in scores])
    if sink is not None:
        m = jnp.maximum(m, sink)
    den = None
    out = None
    for s, (_, v, _) in zip(scores, parts):
        e = jnp.exp(s - m)
        d = jnp.sum(e, axis=-1, keepdims=True)
        o = _dot(e.astype(BF16), v)
        den = d if den is None else den + d
        out = o if out is None else out + o
    if sink is not None:
        den = den + jnp.exp(sink - m)
    return out / den


A_HEADS = 8


def _attend_heads(qs, parts, sinks, stack):
    rows = qs[0].shape[0]
    if not stack:
        return [_attend(q, parts, None if sinks is None else sinks[i]) for i, q in enumerate(qs)]
    sink = None
    if sinks is not None:
        sink = jnp.concatenate([jnp.full((rows, 1), s, F32) for s in sinks], axis=0)
    o = _attend(jnp.concatenate(qs, axis=0), parts, sink)
    return [o[i * rows:(i + 1) * rows] for i in range(len(qs))]


def _three_mixers(scal_ref, subln_ref, q_blocks, kv_parts, oa_o, od_o, om_o, lam_init, stack_a,
                  stack_bc):
    qa, qd, qmn, qmr = q_blocks
    rows = qa.shape[0]
    lane = lax.broadcasted_iota(jnp.int32, (rows, LANES), 1)
    low = lane < 64
    zero = jnp.zeros((rows, LANES), BF16)

    parts = [(p["ka"], p["va"], p.get("mask_a")) for p in kv_parts]
    qs = [jnp.where(low if kvh == 0 else ~low, qa[:, g * LANES:(g + 1) * LANES], zero)
          for kvh in range(2) for g in range(4)]
    out_a = _attend_heads(qs, parts, [scal_ref[h] for h in range(A_HEADS)], stack_a)
    for g in range(4):
        oa_o[:, g * LANES:(g + 1) * LANES] = jnp.where(low, out_a[g], out_a[4 + g]).astype(BF16)

    lam = scal_ref[8]
    subln = subln_ref[...]
    for h in range(4):
        sl = slice(h * LANES, (h + 1) * LANES)
        q128 = qd[:, sl]
        o1, o2 = _attend_heads([jnp.where(low, q128, zero), jnp.where(low, zero, q128)],
                               [(p["kd"][:, sl], p["vd"][:, sl], None) for p in kv_parts],
                               None, stack_bc)
        od = o1 - lam * o2
        od_o[:, sl] = (_rms(od, subln) * (1.0 - lam_init)).astype(BF16)

    for i in range(4):
        sl = slice(i * LANES, (i + 1) * LANES)
        qs = []
        for sub in range(2):
            h = 2 * i + sub
            qn = jnp.where(low if sub == 0 else ~low, qmn[:, sl], zero)
            qr128 = qmr[:, (h // 4) * LANES:(h // 4 + 1) * LANES]
            qr = jnp.where((lane // 32) == (h % 4), qr128, zero)
            qs.append(jnp.concatenate([qn, qr], axis=1))
        parts = [(jnp.concatenate([p["kmn"][:, sl], p["kpe"]], axis=1), p["vm"][:, sl], None)
                 for p in kv_parts]
        o_even, o_odd = _attend_heads(qs, parts, None, stack_bc)
        om_o[:, sl] = jnp.where(low, o_even, o_odd).astype(BF16)


def _attn_ctx_kernel(scal_ref, subln_ref, qa_ref, qd_ref, qmn_ref, qmr_ref,
                     ka_ref, va_ref, kd_ref, vd_ref, kmn_ref, vm_ref, kpe_ref,
                     oa_o, od_o, om_o, *, lam_init):
    part = dict(ka=ka_ref[...].astype(BF16), va=va_ref[...].astype(BF16),
                kd=kd_ref[...].astype(BF16), vd=vd_ref[...].astype(BF16),
                kmn=kmn_ref[...], vm=vm_ref[...], kpe=kpe_ref[...])
    _three_mixers(scal_ref, subln_ref, (qa_ref[...], qd_ref[...], qmn_ref[...], qmr_ref[...]),
                  [part], oa_o, od_o, om_o, lam_init, stack_a=False, stack_bc=True)


def _attn_lat_kernel(scal_ref, subln_ref, wukv_ref, qa_ref, qd_ref, qmn_ref, qmr_ref,
                     ka_ref, va_ref, kd_ref, vd_ref, kmn_ref, vm_ref, kpe_ref,
                     cka_ref, cva_ref, ckd_ref, cvd_ref, cckv_ref, ckpe_ref,
                     oa_in, od_in, om_in, oa_o, od_o, om_o, *, lam_init):
    del oa_in, od_in, om_in
    n = pl.program_id(1)
    kv_c = _dot(cckv_ref[...].astype(BF16), wukv_ref[...])
    cached = dict(ka=cka_ref[...].astype(BF16), va=cva_ref[...].astype(BF16),
                  kd=ckd_ref[...].astype(BF16), vd=cvd_ref[...].astype(BF16),
                  kmn=kv_c[:, 0:512].astype(BF16), vm=kv_c[:, 512:1024].astype(BF16),
                  kpe=ckpe_ref[...].astype(BF16))
    span = Q_BLOCK + 2 * WINDOW
    start = pl.multiple_of(jnp.clip(n * Q_BLOCK - WINDOW, 0, LAT_SEQ - span), WINDOW)
    qpos = n * Q_BLOCK + lax.broadcasted_iota(jnp.int32, (A_HEADS * Q_BLOCK, span), 0) % Q_BLOCK
    kpos = start + lax.broadcasted_iota(jnp.int32, (A_HEADS * Q_BLOCK, span), 1)
    mask_a = jnp.abs(qpos - kpos) <= WINDOW
    new = dict(ka=ka_ref[pl.ds(start, span), :], va=va_ref[pl.ds(start, span), :], mask_a=mask_a,
               kd=kd_ref[...], vd=vd_ref[...], kmn=kmn_ref[...], vm=vm_ref[...], kpe=kpe_ref[...])
    _three_mixers(scal_ref, subln_ref, (qa_ref[...], qd_ref[...], qmn_ref[...], qmr_ref[...]),
                  [cached, new], oa_o, od_o, om_o, lam_init, stack_a=True, stack_bc=True)


_SMEM_SPEC = pl.BlockSpec(memory_space=pltpu.SMEM)


def _attention_ctx(scal, subln, proj, states, layer, lam_init):
    qa, qd, qmn, qmr, kmn, vm, kpe4 = proj[:7]
    s_ka, s_va, s_kd, s_vd = states[:4]
    t = CTX_SEQ

    def blk(a):
        return pl.BlockSpec((t, a.shape[1]), lambda b: (b, 0))

    def sblk(a):
        return pl.BlockSpec((None, None, t, a.shape[-1]), lambda b: (b, layer, 0, 0))

    return pl.pallas_call(
        functools.partial(_attn_ctx_kernel, lam_init=lam_init),
        out_shape=[jax.ShapeDtypeStruct((N_TOK, 512), BF16)] * 3,
        grid=(N_CTX_BATCH,),
        in_specs=[_SMEM_SPEC, _layer_spec(subln, layer)]
                 + [blk(a) for a in (qa, qd, qmn, qmr)]
                 + [sblk(a) for a in (s_ka, s_va, s_kd, s_vd)]
                 + [blk(a) for a in (kmn, vm, kpe4)],
        out_specs=[pl.BlockSpec((t, 512), lambda b: (b, 0))] * 3,
        compiler_params=pltpu.CompilerParams(vmem_limit_bytes=VMEM_LIMIT),
        name="attention_ctx",
    )(scal, subln, qa, qd, qmn, qmr, s_ka, s_va, s_kd, s_vd, kmn, vm, kpe4)


def _attention_lat(scal, subln, wukv, proj, caches, ctx_out, layer, lam_init):
    qa, qd, qmn, qmr, kmn, vm, kpe4, ka_l, va_l, kd_l, vd_l = proj[:11]
    nq = LAT_SEQ // Q_BLOCK
    q_off = N_CTX // Q_BLOCK
    kv_off = N_CTX // LAT_SEQ

    def qblk(a):
        return pl.BlockSpec((Q_BLOCK, a.shape[1]), lambda b, n: (q_off + b * nq + n, 0))

    def kvblk(a):
        return pl.BlockSpec((LAT_SEQ, a.shape[1]), lambda b, n: (b, 0))

    def kvblk_all(a):
        return pl.BlockSpec((LAT_SEQ, a.shape[1]), lambda b, n: (kv_off + b, 0))

    def cblk(a):
        return pl.BlockSpec((None, None, PAST, a.shape[-1]), lambda b, n: (b, layer, 0, 0))

    qs = [qa, qd, qmn, qmr]
    n_in = 3 + 4 + 7 + 6
    out_spec = pl.BlockSpec((Q_BLOCK, 512), lambda b, n: (q_off + b * nq + n, 0))
    return pl.pallas_call(
        functools.partial(_attn_lat_kernel, lam_init=lam_init),
        out_shape=[jax.ShapeDtypeStruct((N_TOK, 512), BF16)] * 3,
        grid=(N_LAT_BATCH, nq),
        in_specs=[_SMEM_SPEC, _layer_spec(subln, layer), _layer_spec(wukv, layer)]
                 + [qblk(a) for a in qs]
                 + [kvblk(a) for a in (ka_l, va_l, kd_l, vd_l)]
                 + [kvblk_all(a) for a in (kmn, vm, kpe4)]
                 + [cblk(a) for a in caches]
                 + [pl.BlockSpec(memory_space=pl.ANY)] * 3,
        out_specs=[out_spec] * 3,
        input_output_aliases={n_in + k: k for k in range(3)},
        compiler_params=pltpu.CompilerParams(vmem_limit_bytes=VMEM_LIMIT),
        name="attention_lat",
    )(scal, subln, wukv, *qs, ka_l, va_l, kd_l, vd_l, kmn, vm, kpe4, *caches, *ctx_out)


def _route(scores, biased):
    per = N_EXPERTS // N_GROUPS
    tm = scores.shape[1]
    sub = lax.broadcasted_iota(jnp.int32, (per, tm), 0).astype(F32)
    groups = [biased[g * per:(g + 1) * per, :] for g in range(N_GROUPS)]
    gscore = []
    for v in groups:
        m1 = jnp.max(v, axis=0, keepdims=True)
        first = jnp.min(jnp.where(v == m1, sub, float(per)), axis=0, keepdims=True)
        m2 = jnp.max(jnp.where(sub == first, -jnp.inf, v), axis=0, keepdims=True)
        gscore.append(m1 + m2)
    vals = []
    for g in range(N_GROUPS):
        rank = jnp.zeros((1, tm), F32)
        for o in range(N_GROUPS):
            if o == g:
                continue
            ahead = (gscore[o] >= gscore[g]) if o < g else (gscore[o] > gscore[g])
            rank = rank + jnp.where(ahead, 1.0, 0.0)
        vals.append(jnp.where(rank < TOPK_GROUPS, groups[g], NEG))
    idx = [sub + float(g * per) for g in range(N_GROUPS)]
    picks, weights = [], []
    for _ in range(TOP_K):
        best = functools.reduce(jnp.maximum, [jnp.max(v, axis=0, keepdims=True) for v in vals])
        cand = functools.reduce(jnp.minimum, [
            jnp.min(jnp.where(v == best, i, float(N_EXPERTS)), axis=0, keepdims=True)
            for v, i in zip(vals, idx)])
        wsel = jnp.zeros((1, tm), F32)
        for g in range(N_GROUPS):
            hit = idx[g] == cand
            wsel = wsel + jnp.sum(jnp.where(hit, scores[g * per:(g + 1) * per, :], 0.0),
                                  axis=0, keepdims=True)
            vals[g] = jnp.where(hit, -jnp.inf, vals[g])
        picks.append(cand)
        weights.append(wsel)
    total = functools.reduce(lambda a, b: a + b, weights)
    pad = [jnp.zeros((1, tm), F32)] * (8 - TOP_K)
    experts = jnp.concatenate(picks + pad, axis=0).astype(jnp.int32)
    gates = jnp.concatenate([x / total * ROUTED_SCALE for x in weights] + pad, axis=0)
    return experts, gates


def _post_kernel(*refs, n_x):
    (mod_ref, oa_ref, od_ref, om_ref, gpre_ref, wgl_ref, wa_ref, wb_ref, wc_ref, wo_ref, gpost_ref,
     gffn_ref, rwt_ref, rb_ref, xmid_o, h2_o, h2t_o, experts_o, gates_o) = refs[n_x:]

    @pl.when(pl.program_id(0) < N_TILES)
    def _():
        x = _load_x(refs[:n_x])
        m = mod_ref[0]
        d = D_MODEL
        sh1, sc1, g1, sh2, sc2 = (m[:, 0:d], m[:, d:2 * d], m[:, 2 * d:3 * d], m[:, 3 * d:4 * d],
                                  m[:, 4 * d:5 * d])
        h = _rms(x, gpre_ref[...]) * (1.0 + sc1) + sh1
        hb = h.astype(BF16)
        merged = None
        for k, (o_ref, w_ref) in enumerate(((oa_ref, wa_ref), (od_ref, wb_ref), (om_ref, wc_ref))):
            gate = jax.nn.sigmoid(_dot(hb, wgl_ref[:, k * d:(k + 1) * d]))
            term = gate * _dot(o_ref[...], w_ref[...])
            merged = term if merged is None else merged + term
        a = _dot(merged.astype(BF16), wo_ref[...])
        xm = x + g1 * _rms(a, gpost_ref[...])
        xmid_o[...] = xm
        h2f = _rms(xm, gffn_ref[...]) * (1.0 + sc2) + sh2
        h2 = h2f.astype(BF16)
        h2_o[...] = h2
        for c in range(ACC_ROWS):
            h2t_o[pl.ds(c, x.shape[0], stride=ACC_ROWS), :] = h2f[:, c * LANES:(c + 1) * LANES]
        scores = jax.nn.sigmoid(_dot_nt(rwt_ref[...], h2))
        experts, gates = _route(scores, scores + rb_ref[...])
        experts_o[...] = experts
        gates_o[...] = gates

    @pl.when(pl.program_id(0) == N_TILES)
    def _():
        h2t_o[...] = jnp.zeros_like(h2t_o)


def _post_attention(x, mod, oa, od, om, wts, layer):
    tm = ROW_TILE

    def row(i):
        return (_tile(i), 0)

    weights = [wts[k] for k in ("g_pre", "wgl", "wa", "wb", "wc", "wo", "g_post", "g_ffn", "rwt", "rb")]
    xs = list(x) if isinstance(x, tuple) else [x]
    return pl.pallas_call(
        functools.partial(_post_kernel, n_x=len(xs)),
        out_shape=[jax.ShapeDtypeStruct((N_TOK, D_MODEL), F32),
                   jax.ShapeDtypeStruct((N_TOK, D_MODEL), BF16),
                   jax.ShapeDtypeStruct(((N_TOK + tm) * ACC_ROWS, LANES), F32),
                   jax.ShapeDtypeStruct((8, N_TOK), jnp.int32),
                   jax.ShapeDtypeStruct((8, N_TOK), F32)],
        grid=(N_TILES + 1,),
        in_specs=_x_specs(x) + [_mod_spec(layer),
                  pl.BlockSpec((tm, 512), row), pl.BlockSpec((tm, 512), row),
                  pl.BlockSpec((tm, 512), row)]
                 + [_layer_spec(w, layer) for w in weights],
        out_specs=[pl.BlockSpec((tm, D_MODEL), row), pl.BlockSpec((tm, D_MODEL), row),
                   pl.BlockSpec((tm * ACC_ROWS, LANES), lambda i: (i, 0)),
                   pl.BlockSpec((8, tm), lambda i: (0, _tile(i))),
                   pl.BlockSpec((8, tm), lambda i: (0, _tile(i)))],
        compiler_params=pltpu.CompilerParams(
            dimension_semantics=("arbitrary",), vmem_limit_bytes=VMEM_LIMIT),
        name="post_attention",
    )(*xs, mod, oa, od, om, *weights)


def _ffn(x, wg, wu):
    return _silu(_dot(x, wg)) * _dot(x, wu)


def _moe_kernel(start_ref, count_ref, dst_ref, gate_ref, xs_ref, wg_ref, wu_ref, wd_ref,
                acc_ref, xa, xb, ya, yb, wgb, wub, wdb, pend):
    e = pl.program_id(0)

    @pl.when(e == 0)
    def _():
        def clear(i, carry):
            r0 = pl.multiple_of(i * MOE_ZERO_ROWS, MOE_ZERO_ROWS)
            acc_ref[pl.ds(r0, MOE_ZERO_ROWS), :] = jnp.zeros((MOE_ZERO_ROWS, LANES), F32)
            return carry

        lax.fori_loop(0, acc_ref.shape[0] // MOE_ZERO_ROWS, clear, 0)

        yb[...] = jnp.zeros_like(yb)
        pend[0] = MOE_LIST_PAD_BASE

    wgb[...] = wg_ref[0].astype(BF16)
    wub[...] = wu_ref[0].astype(BF16)
    wdb[...] = wd_ref[0].astype(BF16)
    start = start_ref[e]
    pairs = (count_ref[e] + 2 * MOE_SUB - 1) // (2 * MOE_SUB)

    def gather(base, xbuf):
        for mi in range(MOE_SUB):
            src = pl.multiple_of(dst_ref[base + mi], ACC_ROWS)
            xbuf[mi * ACC_ROWS:(mi + 1) * ACC_ROWS, :] = xs_ref[pl.ds(src, ACC_ROWS), :]

    def expert_ffn(xbuf, ybuf):
        x = jnp.concatenate([xbuf[pl.ds(c, MOE_SUB, stride=ACC_ROWS), :].astype(BF16)
                             for c in range(ACC_ROWS)], axis=1)
        hid = _ffn(x, wgb[...], wub[...])
        y = _dot(hid.astype(BF16), wdb[...])
        for c in range(ACC_ROWS):
            ybuf[pl.ds(c, MOE_SUB, stride=ACC_ROWS), :] = y[:, c * LANES:(c + 1) * LANES]

    def scatter(base, ybuf):
        for g0 in range(0, MOE_SUB, MOE_RMW_GROUP):
            updates = []
            for mi in range(g0, g0 + MOE_RMW_GROUP):
                dst = pl.multiple_of(dst_ref[base + mi], ACC_ROWS)
                yv = ybuf[mi * ACC_ROWS:(mi + 1) * ACC_ROWS, :]
                updates.append((dst, acc_ref[pl.ds(dst, ACC_ROWS), :] + gate_ref[base + mi] * yv))
            for dst, val in updates:
                acc_ref[pl.ds(dst, ACC_ROWS), :] = val

    start_next = start_ref[jnp.minimum(e + 1, N_EXPERTS - 1)]
    prev_count = count_ref[jnp.maximum(e - 1, 0)]

    @pl.when((pairs > 0) & ((e == 0) | (prev_count == 0)))
    def _():
        gather(start, xa)

    def pair(p, carry):
        base = start + p * (2 * MOE_SUB)
        gather(base + MOE_SUB, xb)
        expert_ffn(xa, ya)
        scatter(pend[0], yb)
        gather(jnp.where(p + 1 < pairs, base + 2 * MOE_SUB, start_next), xa)
        expert_ffn(xb, yb)
        scatter(base, ya)
        pend[0] = base + MOE_SUB
        return carry

    lax.fori_loop(0, pairs, pair, 0)

    @pl.when(e == N_EXPERTS - 1)
    def _():
        scatter(pend[0], yb)


def _moe(lists, h2t, wg, wu, wd, layer):
    def expert(e, *_):
        return (layer, e, 0, 0)

    def whole(e, *_):
        return (0, 0)

    acc_rows = (N_TOK + MOE_SPARE_TOKENS) * ACC_ROWS
    assert acc_rows % MOE_ZERO_ROWS == 0
    return pl.pallas_call(
        _moe_kernel,
        out_shape=jax.ShapeDtypeStruct((acc_rows, LANES), F32),
        grid_spec=pltpu.PrefetchScalarGridSpec(
            num_scalar_prefetch=len(lists),
            grid=(N_EXPERTS,),
            in_specs=[pl.BlockSpec((acc_rows, LANES), whole, pipeline_mode=pl.Buffered(1)),
                      pl.BlockSpec((None, 1, D_MODEL, F_EXPERT), expert),
                      pl.BlockSpec((None, 1, D_MODEL, F_EXPERT), expert),
                      pl.BlockSpec((None, 1, F_EXPERT, D_MODEL), expert)],
            out_specs=pl.BlockSpec((acc_rows, LANES), whole, pipeline_mode=pl.Buffered(1)),
            scratch_shapes=[pltpu.VMEM((MOE_SUB * ACC_ROWS, LANES), F32)] * 4
                           + [pltpu.VMEM((D_MODEL, F_EXPERT), BF16),
                              pltpu.VMEM((D_MODEL, F_EXPERT), BF16),
                              pltpu.VMEM((F_EXPERT, D_MODEL), BF16),
                              pltpu.SMEM((1,), jnp.int32)]),
        compiler_params=pltpu.CompilerParams(
            dimension_semantics=("arbitrary",), vmem_limit_bytes=MOE_VMEM_LIMIT),
        name="moe",
    )(*lists, h2t, wg, wu, wd)


def _final_kernel(acc_ref, h2_ref, xmid_ref, mod_ref, swg_ref, swu_ref, swd_ref, gpost_ref, *out_refs):
    tm = h2_ref.shape[0]
    routed = jnp.concatenate(
        [acc_ref[pl.ds(c, tm, stride=ACC_ROWS), :] for c in range(ACC_ROWS)], axis=1)
    hid = _ffn(h2_ref[...], swg_ref[...], swu_ref[...])
    shared = _dot(hid.astype(BF16), swd_ref[...])
    g2 = mod_ref[0][:, 5 * D_MODEL:6 * D_MODEL]
    out = xmid_ref[...] + g2 * _rms(routed + shared, gpost_ref[...])
    if len(out_refs) == 1:
        out_refs[0][...] = out
    else:
        @pl.when(pl.program_id(0) < CTX_TILES)
        def _():
            out_refs[0][...] = out

        @pl.when(pl.program_id(0) >= CTX_TILES)
        def _():
            out_refs[1][...] = out


def _final(acc, h2, xmid, mod, wts, layer, split):
    tm = ROW_TILE

    def row(i):
        return (i, 0)

    weights = [wts[k] for k in ("swg", "swu", "swd", "g_ffn_post")]
    return pl.pallas_call(
        _final_kernel,
        out_shape=([jax.ShapeDtypeStruct((N_CTX, D_MODEL), F32),
                    jax.ShapeDtypeStruct((N_LAT, D_MODEL), F32)] if split
                   else jax.ShapeDtypeStruct((N_TOK, D_MODEL), F32)),
        grid=(N_TOK // tm,),
        in_specs=[pl.BlockSpec((tm * ACC_ROWS, LANES), row),
                  pl.BlockSpec((tm, D_MODEL), row),
                  pl.BlockSpec((tm, D_MODEL), row),
                  _mod_spec(layer)]
                 + [_layer_spec(w, layer) for w in weights],
        out_specs=_x_specs((None, None)) if split else pl.BlockSpec((tm, D_MODEL), row),
        compiler_params=pltpu.CompilerParams(
            dimension_semantics=("arbitrary",), vmem_limit_bytes=VMEM_LIMIT),
        name="ffn_final",
    )(acc, h2, xmid, mod, *weights)


def _routing_lists(experts, gates):
    e = experts[:TOP_K].reshape(-1)
    tok = jnp.tile(jnp.arange(N_TOK, dtype=jnp.int32), TOP_K)
    pad_e = jnp.repeat(jnp.arange(N_EXPERTS, dtype=jnp.int32), MOE_PAD)
    pad_tok = N_TOK + jnp.tile(jnp.arange(MOE_PAD, dtype=jnp.int32), N_EXPERTS)
    keys = jnp.concatenate([e * TOK_KEY + tok, pad_e * TOK_KEY + pad_tok])
    vals = jnp.concatenate([gates[:TOP_K].reshape(-1), jnp.zeros((N_EXPERTS * MOE_PAD,), F32)])
    keys, gate_sorted = lax.sort((keys, vals), num_keys=1)
    tail = jnp.full((MOE_SUB,), N_TOK, jnp.int32)
    tok_sorted = jnp.concatenate([keys & (TOK_KEY - 1), tail])
    gate_sorted = jnp.concatenate([gate_sorted, jnp.zeros((MOE_SUB,), F32)])
    dst = jnp.minimum(tok_sorted, N_TOK) * ACC_ROWS
    counts = jnp.sum((e[None, :] == jnp.arange(N_EXPERTS, dtype=jnp.int32)[:, None]).astype(jnp.int32),
                     axis=1)
    start = jnp.cumsum(counts) - counts + MOE_PAD * jnp.arange(N_EXPERTS, dtype=jnp.int32)
    return start.astype(jnp.int32), counts, dst, gate_sorted


def _rope_tables():
    t = np.arange(LAT_SEQ)
    pos = np.stack([t // GRID_W, t % GRID_W], axis=1).astype(np.float64)

    def table(half):
        inv = ROPE_BASE ** (-np.arange(half, dtype=np.float64) / half)
        lane = np.arange(LANES)
        axis = (lane // (2 * half)) % 2
        freq = inv[lane % half]
        ang = pos[:, axis] * freq[None, :]
        sign = np.where((lane % (2 * half)) < half, -1.0, 1.0)
        cos = np.concatenate([np.ones((ROW_TILE, LANES)), np.cos(ang)], axis=0)
        sin = np.concatenate([np.zeros((ROW_TILE, LANES)), np.sin(ang) * sign[None, :]], axis=0)
        return jnp.asarray(cos, F32), jnp.asarray(sin, F32)

    cos64, sin64 = table(16)
    cos32, sin32 = table(8)
    return cos64, sin64, cos32, sin32


def _prepare_weights(norm_attn_pre, norm_attn_post, norm_ffn_pre, norm_ffn_post, w_in, dif_subln,
                     mla_q_norm, mla_w_uq, mla_kv_norm, mla_w_ukv, w_branch_a, w_branch_b,
                     w_branch_c, w_out, router_w, router_bias, shared_w_gate, shared_w_up,
                     shared_w_down):
    dp = DEPTH
    qa = w_in[:, :, 0:512].reshape(dp, D_MODEL, 2, 4, 64).transpose(0, 1, 3, 2, 4).reshape(dp, D_MODEL, 512)
    kpe = w_in[:, :, 2688:2720]
    w1 = jnp.concatenate([qa, w_in[:, :, 512:2688], kpe, kpe, kpe, kpe], axis=2).astype(BF16)
    uq = mla_w_uq.reshape(dp, 256, 8, 96)
    wuq = jnp.concatenate([uq[..., :64].reshape(dp, 256, 512), uq[..., 64:].reshape(dp, 256, 256)],
                          axis=2).astype(BF16)
    ukv = mla_w_ukv.reshape(dp, 128, 8, 128)
    wukv = jnp.concatenate([ukv[..., :64].reshape(dp, 128, 512), ukv[..., 64:].reshape(dp, 128, 512)],
                           axis=2).astype(BF16)
    wa = w_branch_a.reshape(dp, 2, 4, 64, D_MODEL).transpose(0, 2, 1, 3, 4).reshape(dp, 512, D_MODEL)
    return dict(
        g_pre=norm_attn_pre[:, None, :], g_post=norm_attn_post[:, None, :],
        g_ffn=norm_ffn_pre[:, None, :], g_ffn_post=norm_ffn_post[:, None, :],
        w1=w1, wgl=w_in[:, :, 2720:].astype(BF16),
        qnorm=mla_q_norm[:, None, :], wuq=wuq, kvnorm=mla_kv_norm[:, None, :], wukv=wukv,
        subln=dif_subln[:, None, :],
        wa=wa.astype(BF16), wb=w_branch_b.astype(BF16), wc=w_branch_c.astype(BF16),
        wo=w_out.astype(BF16), rwt=router_w.transpose(0, 2, 1).astype(BF16),
        rb=router_bias[:, :, None],
        swg=shared_w_gate.astype(BF16), swu=shared_w_up.astype(BF16), swd=shared_w_down.astype(BF16))


def kernel(x_prompt, x_sample, cache_swa_k, cache_swa_v, cache_dif_k, cache_dif_v, cache_mla_ckv, cache_mla_kpe, c, c_ctx, w_mod, b_mod, norm_attn_pre, norm_attn_post, norm_ffn_pre, norm_ffn_post, w_in, swa_sink, dif_lq1, dif_lk1, dif_lq2, dif_lk2, dif_subln, mla_q_norm, mla_w_uq, mla_kv_norm, mla_w_ukv, w_branch_a, w_branch_b, w_branch_c, w_out, router_w, router_bias, moe_w_gate, moe_w_up, moe_w_down, shared_w_gate, shared_w_up, shared_w_down):
    x = (x_prompt.reshape(N_CTX, D_MODEL), x_sample.reshape(N_LAT, D_MODEL))
    cvec = jnp.concatenate([c_ctx[None, :], c, jnp.zeros((8 - 1 - N_LAT_BATCH, D_MODEL), F32)], axis=0)
    mod = _modulation(cvec, w_mod, b_mod).reshape(DEPTH, 8, 1, 6 * D_MODEL)
    tables = _rope_tables()
    wts = _prepare_weights(norm_attn_pre, norm_attn_post, norm_ffn_pre, norm_ffn_post, w_in,
                           dif_subln, mla_q_norm, mla_w_uq, mla_kv_norm, mla_w_ukv, w_branch_a,
                           w_branch_b, w_branch_c, w_out, router_w, router_bias, shared_w_gate,
                           shared_w_up, shared_w_down)
    caches = [cache_swa_k.reshape(N_LAT_BATCH, DEPTH, PAST, 128),
              cache_swa_v.reshape(N_LAT_BATCH, DEPTH, PAST, 128),
              cache_dif_k.reshape(N_LAT_BATCH, DEPTH, PAST, 512),
              cache_dif_v.reshape(N_LAT_BATCH, DEPTH, PAST, 512),
              cache_mla_ckv,
              jnp.tile(cache_mla_kpe, (1, 1, 1, 4))]
    lam_init = [0.8 - 0.6 * math.exp(-0.3 * l) for l in range(DEPTH)]
    lam = (jnp.exp(jnp.sum(dif_lq1 * dif_lk1, axis=1)) - jnp.exp(jnp.sum(dif_lq2 * dif_lk2, axis=1))
           + jnp.asarray(lam_init, F32))
    scal = jnp.concatenate([swa_sink, lam[:, None]], axis=1).astype(F32)

    states = None
    for l in range(DEPTH):
        proj = _in_projection(x, mod, wts, tables, l, states)
        states = proj[11:]
        ctx_out = _attention_ctx(scal[l], wts["subln"], proj, states, l, lam_init[l])
        oa, od, om = _attention_lat(scal[l], wts["subln"], wts["wukv"], proj, caches, ctx_out, l,
                                    lam_init[l])
        xmid, h2, h2t, experts, gates = _post_attention(x, mod, oa, od, om, wts, l)
        lists = _routing_lists(experts, gates)
        acc = _moe(lists, h2t, moe_w_gate, moe_w_up, moe_w_down, l)
        x = _final(acc, h2, xmid, mod, wts, l, split=(l == DEPTH - 1))

    y_p = x[0].reshape(N_CTX_BATCH, CTX_SEQ, D_MODEL)
    y_s = x[1].reshape(N_LAT_BATCH, LAT_SEQ, D_MODEL)
    s_ka, s_va, s_kd, s_vd, s_ckv, s_kpe = states
    return (y_p, y_s,
            s_ka.reshape(N_CTX_BATCH, DEPTH, CTX_SEQ, 2, 64),
            s_va.reshape(N_CTX_BATCH, DEPTH, CTX_SEQ, 2, 64),
            s_kd.reshape(N_CTX_BATCH, DEPTH, CTX_SEQ, 4, 2, 64),
            s_vd.reshape(N_CTX_BATCH, DEPTH, CTX_SEQ, 4, 128),
            s_ckv, s_kpe)
```
